```python
import jax, jax.numpy as jnp
from jax import lax
import numpy as np

D_MODEL = 1024
BATCH = 2
SEQ = 8192
DEPTH = 1

GDN_HEADS = 4
GDN_DK = 128
GDN_DV = 128
HG_HEADS = 4
HG_DK = 128
HG_DV = 128
GDN_QK = GDN_HEADS * GDN_DK
GDN_V = GDN_HEADS * GDN_DV
HG_K = HG_HEADS * HG_DK
HG_V = HG_HEADS * HG_DV
D_MIX = GDN_V + HG_V
D_IN = 2 * GDN_QK + 2 * GDN_V + 2 * GDN_HEADS + 2 * HG_K + 2 * HG_V
CONV_K = 4
CHUNK = 64
N_EXPERTS = 64
N_GROUPS = 8
TOPK_GROUPS = 4
TOP_K = 8
D_EXPERT = 256
D_SHARED = 256
ROUTE_SCALE = 2.5
MOE_BLOCK = 128
EPS = 1e-6

kernel_name = "hybrid_gdn_hgrn2_moe_adaln"


def rms_norm(x, w):
    xf = x.astype(jnp.float32)
    y = xf * lax.rsqrt(jnp.mean(xf * xf, axis=-1, keepdims=True) + EPS)
    return (y * w.astype(jnp.float32)).astype(x.dtype)


def l2_normalize(x):
    return x * lax.rsqrt(jnp.sum(x * x, axis=-1, keepdims=True) + EPS)


def causal_short_conv(x, w):
    T = x.shape[1]
    xp = jnp.pad(x, ((0, 0), (CONV_K - 1, 0), (0, 0)))
    y = sum(xp[:, j:j + T, :] * w[j] for j in range(CONV_K))
    return jax.nn.silu(y)


def to_chunks(t):
    B, T, H, D = t.shape
    return t.reshape(B, T // CHUNK, CHUNK, H, D).transpose(0, 3, 1, 2, 4)


def from_chunks(o):
    N, B, H, C, D = o.shape
    return o.transpose(1, 0, 3, 2, 4).reshape(B, N * C, H, D)


def gated_delta_chunked(q, k, v, g, beta):
    B, T, H, DK = q.shape
    DV = v.shape[-1]
    N = T // CHUNK
    qc, kc, vc = to_chunks(q), to_chunks(k), to_chunks(v)
    gc = jnp.cumsum(g.reshape(B, N, CHUNK, H).transpose(0, 3, 1, 2), axis=-1)
    bc = beta.reshape(B, N, CHUNK, H).transpose(0, 3, 1, 2)
    idx = jnp.arange(CHUNK)
    causal = idx[:, None] >= idx[None, :]
    strict = idx[:, None] > idx[None, :]
    decay = jnp.exp(jnp.where(causal, gc[..., :, None] - gc[..., None, :], -jnp.inf))
    kb = kc * bc[..., None]
    L = jnp.where(strict, jnp.einsum('bhncd,bhnsd->bhncs', kb, kc) * decay, 0.0)
    rhs = jnp.concatenate([vc * bc[..., None], kb * jnp.exp(gc)[..., None]], axis=-1)
    sol = lax.linalg.triangular_solve(L + jnp.eye(CHUNK, dtype=L.dtype), rhs,
                                      left_side=True, lower=True, unit_diagonal=True)
    u, w = sol[..., :DV], sol[..., DV:]
    attn = jnp.einsum('bhncd,bhnsd->bhncs', qc, kc) * decay
    q_dec = qc * jnp.exp(gc)[..., None]
    g_last = gc[..., -1]
    k_tail = kc * jnp.exp(g_last[..., None] - gc)[..., None]

    def step(S, xs):
        u_n, w_n, a_n, qd_n, kt_n, gl_n = xs
        v_new = u_n - jnp.einsum('bhcd,bhdv->bhcv', w_n, S)
        o = jnp.einsum('bhcd,bhdv->bhcv', qd_n, S) + jnp.einsum('bhcs,bhsv->bhcv', a_n, v_new)
        S = S * jnp.exp(gl_n)[..., None, None] + jnp.einsum('bhcd,bhcv->bhdv', kt_n, v_new)
        return S, o

    xs = tuple(jnp.moveaxis(t, 2, 0) for t in (u, w, attn, q_dec, k_tail, g_last))
    S0 = jnp.zeros((B, H, DK, DV), q.dtype)
    _, o = lax.scan(step, S0, xs)
    return from_chunks(o)


def hgrn2_chunked(q, k, v, logf):
    B, T, H, DK = q.shape
    DV = v.shape[-1]
    qc, kc, vc = to_chunks(q), to_chunks(k), to_chunks(v)
    bc = jnp.cumsum(to_chunks(logf), axis=3)
    idx = jnp.arange(CHUNK)
    causal = (idx[:, None] >= idx[None, :])[:, :, None]

    def step(S, xs):
        q_n, k_n, v_n, b_n = xs
        decay = jnp.exp(jnp.where(causal, b_n[..., :, None, :] - b_n[..., None, :, :], -jnp.inf))
        A = jnp.einsum('bhtd,bhsd,bhtsd->bhts', q_n, k_n, decay)
        b_last = b_n[..., -1:, :]
        o = jnp.einsum('bhts,bhsv->bhtv', A, v_n) + jnp.einsum('bhtd,bhdv->bhtv', q_n * jnp.exp(b_n), S)
        S = jnp.exp(b_last[..., 0, :])[..., :, None] * S + jnp.einsum('bhsd,bhsv->bhdv', k_n * jnp.exp(b_last - b_n), v_n)
        return S, o

    xs = tuple(jnp.moveaxis(t, 2, 0) for t in (qc, kc, vc, bc))
    S0 = jnp.zeros((B, H, DK, DV), q.dtype)
    _, o = lax.scan(step, S0, xs)
    return from_chunks(o)


def token_mixer(h, w_in, conv_w, a_log, dt_bias, gdn_norm_w, lb, hg_norm_w, w_out):
    B, T, _ = h.shape
    f32 = jnp.float32
    proj = jnp.einsum('btd,de->bte', h, w_in).astype(f32)
    qkv_w = 2 * GDN_QK + GDN_V
    qkv = causal_short_conv(proj[..., :qkv_w], conv_w.astype(f32))
    qA, kA, vA = jnp.split(qkv, [GDN_QK, 2 * GDN_QK], axis=-1)
    sizes = (GDN_V, GDN_HEADS, GDN_HEADS, HG_K, HG_V, HG_K, HG_V)
    cuts = np.cumsum(sizes)[:-1].tolist()
    gateA, beta_raw, a_raw, f_raw, iB, qB, gateB = jnp.split(proj[..., qkv_w:], cuts, axis=-1)

    qA = l2_normalize(qA.reshape(B, T, GDN_HEADS, GDN_DK)) * (GDN_DK ** -0.5)
    kA = l2_normalize(kA.reshape(B, T, GDN_HEADS, GDN_DK))
    vA = vA.reshape(B, T, GDN_HEADS, GDN_DV)
    beta = jax.nn.sigmoid(beta_raw)
    g_log = -jnp.exp(a_log.astype(f32)) * jax.nn.softplus(a_raw + dt_bias.astype(f32))
    oA = gated_delta_chunked(qA, kA, vA, g_log, beta)
    oA = rms_norm(oA, gdn_norm_w) * jax.nn.silu(gateA.reshape(B, T, GDN_HEADS, GDN_DV))
    oA = oA.reshape(B, T, GDN_V)

    lb = lb.reshape(HG_HEADS, HG_DK)
    f_raw = f_raw.reshape(B, T, HG_HEADS, HG_DK)
    logf = jnp.log(lb + (1.0 - lb) * jax.nn.sigmoid(f_raw))
    kB = (1.0 - lb) * jax.nn.sigmoid(-f_raw)
    qB = jax.nn.silu(qB.reshape(B, T, HG_HEADS, HG_DK))
    oB = hgrn2_chunked(qB, kB, iB.reshape(B, T, HG_HEADS, HG_DV), logf).reshape(B, T, HG_V)
    oB = rms_norm(oB, hg_norm_w) * jax.nn.silu(gateB)

    o = jnp.concatenate([oA, oB], axis=-1).astype(h.dtype)
    return jnp.einsum('bte,ed->btd', o, w_out)


def moe(h, w_router, router_bias, w_gate, w_up, w_down, ws_gate, ws_up, ws_down):
    B, T, D = h.shape
    xt = h.reshape(-1, D)
    M = xt.shape[0]
    scores = jax.nn.sigmoid(xt.astype(jnp.float32) @ w_router.astype(jnp.float32))
    sel = scores + router_bias.astype(jnp.float32)
    group_score = lax.top_k(sel.reshape(M, N_GROUPS, N_EXPERTS // N_GROUPS), 2)[0].sum(-1)
    _, gidx = lax.top_k(group_score, TOPK_GROUPS)
    gmask = jax.nn.one_hot(gidx, N_GROUPS, dtype=jnp.float32).sum(-2) > 0
    emask = jnp.repeat(gmask, N_EXPERTS // N_GROUPS, axis=-1)
    _, eidx = lax.top_k(jnp.where(emask, sel, -jnp.inf), TOP_K)
    wts = jnp.take_along_axis(scores, eidx, axis=-1)
    wts = wts / jnp.sum(wts, axis=-1, keepdims=True) * ROUTE_SCALE
    gates = jnp.einsum('mk,mke->me', wts, jax.nn.one_hot(eidx, N_EXPERTS, dtype=wts.dtype)).astype(h.dtype)

    def expert_block(args):
        x_b, g_b = args
        hg = jnp.einsum('td,edf->tef', x_b, w_gate)
        hu = jnp.einsum('td,edf->tef', x_b, w_up)
        return jnp.einsum('tef,efd->td', jax.nn.silu(hg) * hu * g_b[..., None], w_down)

    routed = lax.map(expert_block, (xt.reshape(-1, MOE_BLOCK, D), gates.reshape(-1, MOE_BLOCK, N_EXPERTS)))
    shared = (jax.nn.silu(xt @ ws_gate) * (xt @ ws_up)) @ ws_down
    return (routed.reshape(M, D) + shared).reshape(B, T, D)


def setup_inputs(seed: int = 0) -> dict:
    key = jax.random.key(seed)
    ks = jax.random.split(key, 24)
    f32 = jnp.float32
    nrm = lambda k, s, sc: jax.random.normal(k, s, f32) * sc
    dt = jnp.exp(jax.random.uniform(ks[8], (DEPTH, GDN_HEADS), f32, np.log(1e-3), np.log(1e-1)))
    return {
        "x": nrm(ks[0], (BATCH, SEQ, D_MODEL), 1.0),
        "c": nrm(ks[1], (BATCH, D_MODEL), 1.0),
        "w_ada": nrm(ks[2], (DEPTH, D_MODEL, 6 * D_MODEL), 0.5 * D_MODEL ** -0.5),
        "b_ada": nrm(ks[3], (DEPTH, 6 * D_MODEL), 0.01),
        "norm1_w": 1.0 + nrm(ks[4], (DEPTH, D_MODEL), 0.02),
        "w_in": nrm(ks[5], (DEPTH, D_MODEL, D_IN), D_MODEL ** -0.5),
        "conv_w": nrm(ks[6], (DEPTH, CONV_K, 2 * GDN_QK + GDN_V), CONV_K ** -0.5),
        "gdn_a_log": jnp.log(jax.random.uniform(ks[7], (DEPTH, GDN_HEADS), f32, 1.0, 16.0)),
        "gdn_dt_bias": dt + jnp.log(-jnp.expm1(-dt)),
        "gdn_norm_w": 1.0 + nrm(ks[9], (DEPTH, GDN_DV), 0.02),
        "hg_lb": nrm(ks[10], (DEPTH + 1, HG_K), 0.1),
        "hg_norm_w": 1.0 + nrm(ks[11], (DEPTH, HG_V), 0.02),
        "w_out": nrm(ks[12], (DEPTH, D_MIX, D_MODEL), D_MIX ** -0.5),
        "norm2_w": 1.0 + nrm(ks[13], (DEPTH, D_MODEL), 0.02),
        "w_router": nrm(ks[14], (DEPTH, D_MODEL, N_EXPERTS), D_MODEL ** -0.5),
        "router_bias": nrm(ks[15], (DEPTH, N_EXPERTS), 0.01),
        "w_gate": nrm(ks[16], (DEPTH, N_EXPERTS, D_MODEL, D_EXPERT), D_MODEL ** -0.5),
        "w_up": nrm(ks[17], (DEPTH, N_EXPERTS, D_MODEL, D_EXPERT), D_MODEL ** -0.5),
        "w_down": nrm(ks[18], (DEPTH, N_EXPERTS, D_EXPERT, D_MODEL), D_EXPERT ** -0.5),
        "ws_gate": nrm(ks[19], (DEPTH, D_MODEL, D_SHARED), D_MODEL ** -0.5),
        "ws_up": nrm(ks[20], (DEPTH, D_MODEL, D_SHARED), D_MODEL ** -0.5),
        "ws_down": nrm(ks[21], (DEPTH, D_SHARED, D_MODEL), D_SHARED ** -0.5),
        "final_norm_w": 1.0 + nrm(ks[22], (D_MODEL,), 0.02),
    }


def reference(x, c, w_ada, b_ada, norm1_w, w_in, conv_w, gdn_a_log, gdn_dt_bias, gdn_norm_w,
              hg_lb, hg_norm_w, w_out, norm2_w, w_router, router_bias, w_gate, w_up, w_down,
              ws_gate, ws_up, ws_down, final_norm_w):
    lb_all = jnp.cumsum(jax.nn.softmax(hg_lb.astype(jnp.float32), axis=0), axis=0)
    c_act = jax.nn.silu(c)
    for l in range(DEPTH):
        mod = jnp.einsum('bd,de->be', c_act, w_ada[l]) + b_ada[l]
        sh1, sc1, g1, sh2, sc2, g2 = jnp.split(mod, 6, axis=-1)
        h = rms_norm(x, norm1_w[l]) * (1.0 + sc1[:, None]) + sh1[:, None]
        x = x + g1[:, None] * token_mixer(h, w_in[l], conv_w[l], gdn_a_log[l], gdn_dt_bias[l],
                                          gdn_norm_w[l], lb_all[l], hg_norm_w[l], w_out[l])
        h = rms_norm(x, norm2_w[l]) * (1.0 + sc2[:, None]) + sh2[:, None]
        x = x + g2[:, None] * moe(h, w_router[l], router_bias[l], w_gate[l], w_up[l], w_down[l],
                                  ws_gate[l], ws_up[l], ws_down[l])
    return rms_norm(x, final_norm_w)
```

```python
import functools

import jax
import jax.numpy as jnp
from jax import lax
from jax.experimental import pallas as pl
from jax.experimental.pallas import tpu as pltpu

F32 = jnp.float32
BF16 = jnp.bfloat16

EPS = 1e-6
CHUNK = 64
SUB = 16
HEADS = 4
DH = 128
GW = HEADS * DH
CONV_K = 4
N_EXPERTS = 64
N_GROUPS = 8
GROUP_SIZE = N_EXPERTS // N_GROUPS
TOPK_GROUPS = 4
TOP_K = 8
D_EXPERT = 256
ROUTE_SCALE = 2.5
GATE_LANES = 128

VMEM_LIMIT_BYTES = 56 * 1024 * 1024

ACT = BF16


def _silu(x):
    return x * jax.nn.sigmoid(x)


def _dot(a, b):
    return jnp.dot(a.astype(BF16), b.astype(BF16), preferred_element_type=F32)


def _dot_nt(a, b):
    return lax.dot_general(a.astype(BF16), b.astype(BF16), (((1,), (1,)), ((), ())),
                           preferred_element_type=F32)


def _dot_tn(a, b):
    return lax.dot_general(a.astype(BF16), b.astype(BF16), (((0,), (0,)), ((), ())),
                           preferred_element_type=F32)


def _split2(x):
    hi = x.astype(BF16)
    lo = (x - hi.astype(F32)).astype(BF16)
    return hi, lo


def _dot3(a, b, dot=_dot):
    ah, al = _split2(a)
    bh, bl = _split2(b)
    return dot(ah, bh) + dot(ah, bl) + dot(al, bh)


def _cumsum_rows(tri, x):
    hi = x.astype(BF16)
    r = x - hi.astype(F32)
    mid = r.astype(BF16)
    lo = (r - mid.astype(F32)).astype(BF16)
    return (jnp.dot(tri, hi, preferred_element_type=F32)
            + jnp.dot(tri, mid, preferred_element_type=F32)
            + jnp.dot(tri, lo, preferred_element_type=F32))


def _lane_pick(tile, lane, idx):
    return jnp.sum(jnp.where(lane == idx, tile, 0.0), axis=1, keepdims=True)


def _rms(x):
    return x * lax.rsqrt(jnp.mean(x * x, axis=-1, keepdims=True) + EPS)


def _ada_kernel(c_ref, w_ref, b_ref, o_ref):
    ca = _silu(c_ref[...])
    o_ref[...] = _dot3(ca, w_ref[...]) + b_ref[...]


def _ada(c_pad, w, b):
    rows, d = c_pad.shape
    n = w.shape[1]
    tn = 1024
    return pl.pallas_call(
        _ada_kernel,
        grid=(n // tn,),
        in_specs=[pl.BlockSpec((rows, d), lambda j: (0, 0)),
                  pl.BlockSpec((d, tn), lambda j: (0, j)),
                  pl.BlockSpec((1, tn), lambda j: (0, j))],
        out_specs=pl.BlockSpec((rows, tn), lambda j: (0, j)),
        out_shape=jax.ShapeDtypeStruct((rows, n), F32),
        compiler_params=pltpu.CompilerParams(dimension_semantics=("arbitrary",),
                                             vmem_limit_bytes=VMEM_LIMIT_BYTES),
        name="ada",
    )(c_pad, w, b)


def _inproj_kernel(layer, x_ref, n1_ref, sc_ref, sh_ref, w_ref, cw_ref, alog_ref, dt_ref, lb_ref, tri_ref,
                   qa_ref, ka_ref, va_ref, ga_ref, sm_ref, b_ref, kb_ref, ib_ref, qb_ref, gb_ref,
                   pbuf):
    tt = x_ref.shape[1]
    t = pl.program_id(1)

    h = _rms(x_ref[0]) * n1_ref[...]
    h = h * (1.0 + sc_ref[0]) + sh_ref[0]
    hb = h.astype(BF16)

    def proj(g, width=GW):
        return jnp.dot(hb, w_ref[:, g * GW:g * GW + width], preferred_element_type=F32)

    @pl.when(t == 0)
    def _():
        pbuf[:, 0:8, :] = jnp.zeros((3, 8, GW), F32)

    for g, out_ref in enumerate((qa_ref, ka_ref, va_ref)):
        cols = slice(g * GW, (g + 1) * GW)
        p = proj(g)
        pbuf[g, 8:8 + tt, :] = p
        y = p * cw_ref[CONV_K - 1:CONV_K, cols]
        for j in range(1, CONV_K):
            y = y + pbuf[g, 8 - j:8 - j + tt, :] * cw_ref[CONV_K - 1 - j:CONV_K - j, cols]
        pbuf[g, 0:8, :] = pbuf[g, tt:tt + 8, :]
        y = _silu(y)
        if g == 2:
            out_ref[0] = y.astype(out_ref.dtype)
        else:
            scale = DH ** -0.5 if g == 0 else 1.0
            for hh in range(HEADS):
                hs = slice(hh * DH, (hh + 1) * DH)
                yh = y[:, hs]
                inv = lax.rsqrt(jnp.sum(yh * yh, axis=-1, keepdims=True) + EPS)
                out_ref[0, :, hs] = (yh * inv * scale).astype(out_ref.dtype)

    ga_ref[0] = _silu(proj(3)).astype(ga_ref.dtype)

    ps = proj(8, GATE_LANES)
    lane = lax.broadcasted_iota(jnp.int32, ps.shape, 1)
    beta = jax.nn.sigmoid(ps)
    z = ps + dt_ref[...]
    softplus = jnp.maximum(z, 0.0) + jnp.log1p(jnp.exp(-jnp.abs(z)))
    g_log = -jnp.exp(alog_ref[...]) * softplus
    tri = tri_ref[...]
    gc = _cumsum_rows(tri, jnp.where((lane >= HEADS) & (lane < 2 * HEADS), g_log, 0.0))
    sm_ref[0] = jnp.where(lane < HEADS, beta, gc)

    hl = lb_ref[...]
    e = jnp.exp(hl - jnp.max(hl, axis=0, keepdims=True))
    lb = jnp.sum(e[0:layer + 1], axis=0, keepdims=True) / jnp.sum(e, axis=0, keepdims=True)
    fr = proj(4)
    logf = jnp.log(lb + (1.0 - lb) * jax.nn.sigmoid(fr))
    b_ref[0] = _cumsum_rows(tri, logf)
    kb_ref[0] = ((1.0 - lb) * jax.nn.sigmoid(-fr)).astype(kb_ref.dtype)
    ib_ref[0] = proj(5).astype(ib_ref.dtype)
    qb_ref[0] = _silu(proj(6)).astype(qb_ref.dtype)
    gb_ref[0] = _silu(proj(7)).astype(gb_ref.dtype)


def _inproj(layer, x, n1, sc1, sh1, w_all, conv_w, alog_pad, dt_pad, hg_lb, tri, tt):
    B, T, D = x.shape
    const = lambda shape: pl.BlockSpec(shape, lambda b, t: (0,) * len(shape))
    act = lambda dt: jax.ShapeDtypeStruct((B, T, GW), dt)
    tile = lambda w: pl.BlockSpec((1, tt, w), lambda b, t: (b, t, 0))
    per_batch = pl.BlockSpec((1, 1, D), lambda b, t: (b, 0, 0))
    return pl.pallas_call(
        functools.partial(_inproj_kernel, layer),
        grid=(B, T // tt),
        in_specs=[tile(D), const((1, D)), per_batch, per_batch,
                  const(w_all.shape), const(conv_w.shape), const((1, GATE_LANES)), const((1, GATE_LANES)),
                  const(hg_lb.shape), const((tt, tt))],
        out_specs=[tile(GW), tile(GW), tile(GW), tile(GW), tile(GATE_LANES), tile(GW),
                   tile(GW), tile(GW), tile(GW), tile(GW)],
        out_shape=[act(ACT), act(ACT), act(ACT), act(ACT),
                   jax.ShapeDtypeStruct((B, T, GATE_LANES), F32), act(F32),
                   act(ACT), act(ACT), act(ACT), act(ACT)],
        scratch_shapes=[pltpu.VMEM((3, tt + 8, GW), F32)],
        compiler_params=pltpu.CompilerParams(dimension_semantics=("arbitrary", "arbitrary"),
                                             vmem_limit_bytes=VMEM_LIMIT_BYTES),
        name="inproj",
    )(x, n1, sc1, sh1, w_all, conv_w, alog_pad, dt_pad, hg_lb, tri)


def _gdn_kernel(q_ref, k_ref, v_ref, sg_ref, sm_ref, gct_ref, nw_ref, o_ref, s_ref):
    tt = q_ref.shape[1]
    nc = tt // CHUNK

    @pl.when(pl.program_id(1) == 0)
    def _():
        s_ref[...] = jnp.zeros(s_ref.shape, F32)

    row = lax.broadcasted_iota(jnp.int32, (CHUNK, CHUNK), 0)
    col = lax.broadcasted_iota(jnp.int32, (CHUNK, CHUNK), 1)
    causal = row >= col
    strict = row > col
    eye = jnp.where(row == col, 1.0, 0.0)
    lane = lax.broadcasted_iota(jnp.int32, (CHUNK, GATE_LANES), 1)
    nw = nw_ref[...]

    def body(c, carry):
        r0 = pl.multiple_of(c * CHUNK, CHUNK)
        rows = pl.ds(r0, CHUNK)
        smt = sm_ref[0, rows, :]
        for hh in range(HEADS):
            hs = slice(hh * DH, (hh + 1) * DH)
            q = q_ref[0, rows, hs].astype(F32)
            k = k_ref[0, rows, hs].astype(F32)
            v = v_ref[0, rows, hs].astype(F32)
            beta = _lane_pick(smt, lane, hh)
            gcol = _lane_pick(smt, lane, HEADS + hh)
            grow = gct_ref[0, hh, pl.ds(c, 1), :]
            decay = jnp.exp(jnp.where(causal, gcol - grow, -jnp.inf))
            kb = k * beta
            L = jnp.where(strict, _dot_nt(kb, k) * decay, 0.0)
            tinv = eye - L
            pw = _dot3(L, L)
            for _ in range(4):
                tinv = tinv + _dot3(tinv, pw)
                pw = _dot3(pw, pw)
            tinv = tinv + _dot3(tinv, pw)
            eg = jnp.exp(gcol)
            sol = _dot(tinv, jnp.concatenate([v * beta, kb * eg], axis=1))
            u, w = sol[:, :DH], sol[:, DH:]
            attn = _dot_nt(q, k) * decay
            glast = gcol[CHUNK - 1:CHUNK, :]
            k_tail = k * jnp.exp(glast - gcol)
            S = s_ref[hh]
            ws = _dot(jnp.concatenate([w, q * eg], axis=0), S)
            v_new = u - ws[:CHUNK]
            o = ws[CHUNK:] + _dot(attn, v_new)
            s_ref[hh] = S * jnp.exp(glast) + _dot_tn(k_tail, v_new)
            o = _rms(o) * nw * sg_ref[0, rows, hs].astype(F32)
            o_ref[0, rows, hs] = o.astype(o_ref.dtype)
        return carry

    lax.fori_loop(0, nc, body, 0)


def _gdn(q, k, v, sg, sm, gct, nw, tt):
    B, T, _ = q.shape
    nc = tt // CHUNK
    tile = lambda w: pl.BlockSpec((1, tt, w), lambda b, t: (b, t, 0))
    return pl.pallas_call(
        _gdn_kernel,
        grid=(B, T // tt),
        in_specs=[tile(GW), tile(GW), tile(GW), tile(GW), tile(GATE_LANES),
                  pl.BlockSpec((1, HEADS, nc, CHUNK), lambda b, t: (b, 0, t, 0)),
                  pl.BlockSpec((1, DH), lambda b, t: (0, 0))],
        out_specs=tile(GW),
        out_shape=jax.ShapeDtypeStruct((B, T, GW), ACT),
        scratch_shapes=[pltpu.VMEM((HEADS, DH, DH), F32)],
        compiler_params=pltpu.CompilerParams(dimension_semantics=("arbitrary", "arbitrary"),
                                             vmem_limit_bytes=VMEM_LIMIT_BYTES),
        name="gdn",
    )(q, k, v, sg, sm, gct, nw)


def _hgrn_kernel(q_ref, k_ref, v_ref, b_ref, sg_ref, nw_ref, o_ref, st_ref):
    tt = q_ref.shape[1]
    nc = tt // CHUNK

    @pl.when(pl.program_id(1) == 0)
    def _():
        st_ref[...] = jnp.zeros(st_ref.shape, F32)

    row = lax.broadcasted_iota(jnp.int32, (CHUNK, CHUNK), 0)
    col = lax.broadcasted_iota(jnp.int32, (CHUNK, CHUNK), 1)
    earlier_block = (col // SUB) < (row // SUB)
    diag_block = ((col // SUB) == (row // SUB)) & (col <= row)

    def body(c, carry):
        r0 = pl.multiple_of(c * CHUNK, CHUNK)
        rows = pl.ds(r0, CHUNK)
        outs = []
        for hh in range(HEADS):
            hs = slice(hh * DH, (hh + 1) * DH)
            q = q_ref[0, rows, hs].astype(F32)
            k = k_ref[0, rows, hs].astype(F32)
            v = v_ref[0, rows, hs].astype(F32)
            b = b_ref[0, rows, hs]
            blast = b[CHUNK - 1:CHUNK, :]
            st = st_ref[hh]
            o = _dot_nt(q * jnp.exp(b), st)

            blocks = []
            for i in range(CHUNK // SUB):
                if i == 0:
                    blocks.append(jnp.zeros((SUB, CHUNK), F32))
                    continue
                bs = slice(i * SUB, (i + 1) * SUB)
                r = b[i * SUB:i * SUB + 1, :]
                qi = q[bs] * jnp.exp(b[bs] - r)
                kj = k * jnp.exp(jnp.minimum(r - b, 0.0))
                blocks.append(_dot_nt(qi, kj))
            a_diag = jnp.zeros((CHUNK, CHUNK), F32)
            for delta in range(SUB):
                bsh = b if delta == 0 else pltpu.roll(b, delta, 0)
                ksh = k if delta == 0 else pltpu.roll(k, delta, 0)
                term = q * ksh * jnp.exp(jnp.minimum(b - bsh, 0.0))
                colv = jnp.sum(term, axis=1, keepdims=True)
                a_diag = jnp.where(row - col == delta, colv, a_diag)
            a = jnp.where(earlier_block, jnp.concatenate(blocks, axis=0),
                          jnp.where(diag_block, a_diag, 0.0))

            o = o + _dot(a, v)
            k_tail = k * jnp.exp(blast - b)
            st_ref[hh] = st * jnp.exp(blast) + _dot_tn(v, k_tail)
            outs.append(o)
        o = jnp.concatenate(outs, axis=1)
        o = _rms(o) * nw_ref[...] * sg_ref[0, rows, :].astype(F32)
        o_ref[0, rows, :] = o.astype(o_ref.dtype)
        return carry

    lax.fori_loop(0, nc, body, 0)


def _hgrn(q, k, v, b, sg, nw, tt):
    B, T, _ = q.shape
    tile = pl.BlockSpec((1, tt, GW), lambda bi, t: (bi, t, 0))
    return pl.pallas_call(
        _hgrn_kernel,
        grid=(B, T // tt),
        in_specs=[tile, tile, tile, tile, tile, pl.BlockSpec((1, GW), lambda bi, t: (0, 0))],
        out_specs=tile,
        out_shape=jax.ShapeDtypeStruct((B, T, GW), ACT),
        scratch_shapes=[pltpu.VMEM((HEADS, DH, DH), F32)],
        compiler_params=pltpu.CompilerParams(dimension_semantics=("arbitrary", "arbitrary"),
                                             vmem_limit_bytes=VMEM_LIMIT_BYTES),
        name="hgrn",
    )(q, k, v, b, sg, nw)


def _outproj_kernel(oa_ref, ob_ref, x_ref, wa_ref, wb_ref, g1_ref, n2_ref, sc_ref, sh_ref, wr_ref, rb_ref,
                    x1_ref, h2_ref, gates_ref):
    tm = x_ref.shape[1]
    mix = (jnp.dot(oa_ref[0], wa_ref[...], preferred_element_type=F32)
           + jnp.dot(ob_ref[0], wb_ref[...], preferred_element_type=F32))
    x1 = x_ref[0] + g1_ref[0] * mix
    x1_ref[0] = x1
    h2 = _rms(x1) * n2_ref[...]
    h2 = h2 * (1.0 + sc_ref[0]) + sh_ref[0]
    h2_ref[0] = h2.astype(h2_ref.dtype)

    scores = jax.nn.sigmoid(_dot3(wr_ref[...], h2, dot=_dot_nt))
    sel = scores + rb_ref[...]
    sub = lax.broadcasted_iota(jnp.int32, (GROUP_SIZE, tm), 0)
    neg = -jnp.inf

    def take_max(blk):
        m = jnp.max(blk, axis=0, keepdims=True)
        first = jnp.min(jnp.where(blk == m, sub, GROUP_SIZE), axis=0, keepdims=True)
        hit = sub == first
        return m, hit, jnp.where(hit, neg, blk)

    sel_blk = [sel[g * GROUP_SIZE:(g + 1) * GROUP_SIZE] for g in range(N_GROUPS)]
    group_score = jnp.zeros((N_GROUPS, tm), F32)
    for g in range(N_GROUPS):
        m1, _, rest = take_max(sel_blk[g])
        m2 = jnp.max(rest, axis=0, keepdims=True)
        group_score = jnp.where(sub == g, m1 + m2, group_score)
    group_on = jnp.zeros((N_GROUPS, tm), F32)
    for _ in range(TOPK_GROUPS):
        _, hit, group_score = take_max(group_score)
        group_on = jnp.where(hit, 1.0, group_on)

    cand = [jnp.where(group_on[g:g + 1] > 0.0, sel_blk[g], neg) for g in range(N_GROUPS)]
    picked = [jnp.zeros((GROUP_SIZE, tm), F32) for _ in range(N_GROUPS)]
    for _ in range(TOP_K):
        m = jnp.max(functools.reduce(jnp.maximum, cand), axis=0, keepdims=True)
        first = functools.reduce(jnp.minimum, [jnp.where(cand[g] == m, sub + g * GROUP_SIZE, N_EXPERTS)
                                               for g in range(N_GROUPS)])
        first = jnp.min(first, axis=0, keepdims=True)
        for g in range(N_GROUPS):
            hit = (sub + g * GROUP_SIZE) == first
            picked[g] = jnp.where(hit, 1.0, picked[g])
            cand[g] = jnp.where(hit, neg, cand[g])

    wts = [jnp.where(picked[g] > 0.0, scores[g * GROUP_SIZE:(g + 1) * GROUP_SIZE], 0.0) for g in range(N_GROUPS)]
    denom = jnp.sum(functools.reduce(jnp.add, wts), axis=0, keepdims=True)
    gates = [w / denom * ROUTE_SCALE for w in wts]
    shared_one = jnp.where(sub == 0, 1.0, 0.0)
    pad = jnp.zeros((GATE_LANES - N_EXPERTS - GROUP_SIZE, tm), F32)
    gates_t = jnp.concatenate(gates + [shared_one, pad], axis=0)
    gates_ref[0] = gates_t.T


def _outproj(oa, ob, x, wa, wb, g1, n2, sc2, sh2, wr_t, rb, tm):
    B, T, D = x.shape
    const = lambda shape: pl.BlockSpec(shape, lambda b, t: (0,) * len(shape))
    tile = lambda w: pl.BlockSpec((1, tm, w), lambda b, t: (b, t, 0))
    per_batch = pl.BlockSpec((1, 1, D), lambda b, t: (b, 0, 0))
    return pl.pallas_call(
        _outproj_kernel,
        grid=(B, T // tm),
        in_specs=[tile(GW), tile(GW), tile(D), const(wa.shape), const(wb.shape), per_batch,
                  const((1, D)), per_batch, per_batch, const(wr_t.shape), const(rb.shape)],
        out_specs=[tile(D), tile(D), tile(GATE_LANES)],
        out_shape=[jax.ShapeDtypeStruct((B, T, D), F32), jax.ShapeDtypeStruct((B, T, D), BF16),
                   jax.ShapeDtypeStruct((B, T, GATE_LANES), F32)],
        compiler_params=pltpu.CompilerParams(dimension_semantics=("arbitrary", "arbitrary"),
                                             vmem_limit_bytes=VMEM_LIMIT_BYTES),
        name="outproj",
    )(oa, ob, x, wa, wb, g1, n2, sc2, sh2, wr_t, rb)


def _moe_kernel(h_ref, gates_ref, wgu_ref, wd_ref, x1_ref, g2_ref, fw_ref, o_ref, acc_ref):
    e = pl.program_id(2)

    @pl.when(e == 0)
    def _():
        acc_ref[...] = jnp.zeros(acc_ref.shape, F32)

    gu = jnp.dot(h_ref[0], wgu_ref[0], preferred_element_type=F32)
    gates = gates_ref[0]
    lane = lax.broadcasted_iota(jnp.int32, gates.shape, 1)
    gate = _lane_pick(gates, lane, e)
    act = _silu(gu[:, :D_EXPERT]) * gu[:, D_EXPERT:] * gate
    acc_ref[...] += jnp.dot(act.astype(BF16), wd_ref[0], preferred_element_type=F32)

    @pl.when(e == pl.num_programs(2) - 1)
    def _():
        y = x1_ref[0] + g2_ref[0] * acc_ref[...]
        o_ref[0] = _rms(y) * fw_ref[...]


def _moe(h2, gates, wgu, wd, x1, g2, fw, tm):
    B, T, D = x1.shape
    ne = wgu.shape[0]
    tile = lambda w: pl.BlockSpec((1, tm, w), lambda b, t, e: (b, t, 0))
    return pl.pallas_call(
        _moe_kernel,
        grid=(B, T // tm, ne),
        in_specs=[tile(D), tile(GATE_LANES),
                  pl.BlockSpec((1, D, 2 * D_EXPERT), lambda b, t, e: (e, 0, 0)),
                  pl.BlockSpec((1, D_EXPERT, D), lambda b, t, e: (e, 0, 0)),
                  tile(D),
                  pl.BlockSpec((1, 1, D), lambda b, t, e: (b, 0, 0)),
                  pl.BlockSpec((1, D), lambda b, t, e: (0, 0))],
        out_specs=tile(D),
        out_shape=jax.ShapeDtypeStruct((B, T, D), F32),
        scratch_shapes=[pltpu.VMEM((tm, D), F32)],
        compiler_params=pltpu.CompilerParams(dimension_semantics=("arbitrary", "arbitrary", "arbitrary"),
                                             vmem_limit_bytes=VMEM_LIMIT_BYTES),
        name="moe",
    )(h2, gates, wgu, wd, x1, g2, fw)


def _pick_tile(n, want):
    t = min(n, want)
    assert n % t == 0 and t % CHUNK == 0, (n, want)
    return t


def kernel(x, c, w_ada, b_ada, norm1_w, w_in, conv_w, gdn_a_log, gdn_dt_bias, gdn_norm_w, hg_lb, hg_norm_w,
           w_out, norm2_w, w_router, router_bias, w_gate, w_up, w_down, ws_gate, ws_up, ws_down, final_norm_w):
    B, T, D = x.shape
    depth = w_ada.shape[0]
    assert depth == 1 and T % CHUNK == 0 and B <= 8
    layer = 0
    tt = _pick_tile(T, 512)
    tm_moe = _pick_tile(T, 1024)

    c_pad = jnp.pad(c, ((0, 8 - B), (0, 0)))
    mod = _ada(c_pad, w_ada[layer], b_ada[layer].reshape(1, -1))[:B]
    sh1, sc1, g1, sh2, sc2, g2 = (m.reshape(B, 1, D) for m in jnp.split(mod, 6, axis=-1))

    w = w_in[layer]
    qkv_w = 3 * GW
    sizes = (GW, HEADS, HEADS, GW, GW, GW, GW)
    offs = [qkv_w]
    for s in sizes:
        offs.append(offs[-1] + s)
    seg = lambda i: w[:, offs[i]:offs[i + 1]]
    small = jnp.pad(jnp.concatenate([seg(1), seg(2)], axis=1), ((0, 0), (0, GATE_LANES - 2 * HEADS)))
    w_all = jnp.concatenate([w[:, :qkv_w], seg(0), seg(3), seg(4), seg(5), seg(6), small], axis=1).astype(BF16)
    lane_pad = lambda v: jnp.pad(v.astype(F32).reshape(1, HEADS), ((0, 0), (HEADS, GATE_LANES - 2 * HEADS)))
    idx = jnp.arange(tt)
    tri = ((idx[:, None] >= idx[None, :]) & (idx[:, None] // CHUNK == idx[None, :] // CHUNK)).astype(BF16)

    qa, ka, va, ga, sm, bcum, kb, ib, qb, gb = _inproj(
        layer, x, norm1_w[layer].reshape(1, D), sc1, sh1, w_all, conv_w[layer].astype(F32),
        lane_pad(gdn_a_log[layer]), lane_pad(gdn_dt_bias[layer]), hg_lb.astype(F32), tri, tt)

    gct = sm[:, :, HEADS:2 * HEADS].transpose(0, 2, 1).reshape(B, HEADS, T // CHUNK, CHUNK)
    oa = _gdn(qa, ka, va, ga, sm, gct, gdn_norm_w[layer].reshape(1, DH), tt)
    ob = _hgrn(qb, kb, ib, bcum, gb, hg_norm_w[layer].reshape(1, GW), tt)

    wo = w_out[layer].astype(BF16)
    x1, h2, gates = _outproj(oa, ob, x, wo[:GW], wo[GW:], g1, norm2_w[layer].reshape(1, D), sc2, sh2,
                             w_router[layer].T, router_bias[layer].reshape(N_EXPERTS, 1), tt)

    wgu = jnp.concatenate([jnp.concatenate([w_gate[layer], w_up[layer]], axis=-1),
                           jnp.concatenate([ws_gate[layer], ws_up[layer]], axis=-1)[None]], axis=0).astype(BF16)
    wd = jnp.concatenate([w_down[layer], ws_down[layer][None]], axis=0).astype(BF16)
    return _moe(h2, gates, wgu, wd, x1, g2, final_norm_w.reshape(1, D), tm_moe)
```

```python
import functools

import jax
import jax.numpy as jnp
from jax import lax
from jax.experimental import pallas as pl
from jax.experimental.pallas import tpu as pltpu

F32 = jnp.float32
BF16 = jnp.bfloat16

EPS = 1e-6
CHUNK = 64
SUB = 8
HEADS = 4
DH = 128
GW = HEADS * DH
CONV_K = 4
N_EXPERTS = 64
N_GROUPS = 8
GROUP_SIZE = N_EXPERTS // N_GROUPS
TOPK_GROUPS = 4
TOP_K = 8
D_EXPERT = 256
ROUTE_SCALE = 2.5
GATE_LANES = 128
GDN_CHUNKS_PER_ITER = 2

VMEM_LIMIT_BYTES = 56 * 1024 * 1024

ACT = BF16


def _silu(x):
    return x * jax.nn.sigmoid(x)


def _dot(a, b):
    return jnp.dot(a.astype(BF16), b.astype(BF16), preferred_element_type=F32)


def _dot_nt(a, b):
    return lax.dot_general(a.astype(BF16), b.astype(BF16), (((1,), (1,)), ((), ())),
                           preferred_element_type=F32)


def _dot_tn(a, b):
    return lax.dot_general(a.astype(BF16), b.astype(BF16), (((0,), (0,)), ((), ())),
                           preferred_element_type=F32)


def _split2(x):
    hi = x.astype(BF16)
    lo = (x - hi.astype(F32)).astype(BF16)
    return hi, lo


def _dot3(a, b, dot=_dot):
    ah, al = _split2(a)
    bh, bl = _split2(b)
    return dot(ah, bh) + dot(ah, bl) + dot(al, bh)


def _cumsum_rows(tri, x):
    hi = x.astype(BF16)
    r = x - hi.astype(F32)
    mid = r.astype(BF16)
    lo = (r - mid.astype(F32)).astype(BF16)
    return (jnp.dot(tri, hi, preferred_element_type=F32)
            + jnp.dot(tri, mid, preferred_element_type=F32)
            + jnp.dot(tri, lo, preferred_element_type=F32))


def _lane_pick(tile, lane, idx):
    return jnp.sum(jnp.where(lane == idx, tile, 0.0), axis=1, keepdims=True)


def _rms(x):
    return x * lax.rsqrt(jnp.mean(x * x, axis=-1, keepdims=True) + EPS)


def _ada_kernel(c_ref, w_ref, b_ref, o_ref):
    ca = _silu(c_ref[...])
    o_ref[...] = _dot3(ca, w_ref[...]) + b_ref[...]


def _ada(c_pad, w, b):
    rows, d = c_pad.shape
    n = w.shape[1]
    tn = 1024
    return pl.pallas_call(
        _ada_kernel,
        grid=(n // tn,),
        in_specs=[pl.BlockSpec((rows, d), lambda j: (0, 0)),
                  pl.BlockSpec((d, tn), lambda j: (0, j)),
                  pl.BlockSpec((1, tn), lambda j: (0, j))],
        out_specs=pl.BlockSpec((rows, tn), lambda j: (0, j)),
        out_shape=jax.ShapeDtypeStruct((rows, n), F32),
        compiler_params=pltpu.CompilerParams(dimension_semantics=("arbitrary",),
                                             vmem_limit_bytes=VMEM_LIMIT_BYTES),
        name="ada",
    )(c_pad, w, b)


def _inproj_kernel(layer, x_ref, n1_ref, sc_ref, sh_ref, w_ref, cw_ref, alog_ref, dt_ref, lb_ref, tri_ref,
                   qa_ref, ka_ref, va_ref, ga_ref, sm_ref, b_ref, kb_ref, ib_ref, qb_ref, gb_ref,
                   pbuf):
    tt = x_ref.shape[1]
    t = pl.program_id(1)

    h = _rms(x_ref[0]) * n1_ref[...]
    h = h * (1.0 + sc_ref[0]) + sh_ref[0]
    hb = h.astype(BF16)

    def proj(g, width=GW):
        return jnp.dot(hb, w_ref[:, g * GW:g * GW + width], preferred_element_type=F32)

    @pl.when(t == 0)
    def _():
        pbuf[:, 0:8, :] = jnp.zeros((3, 8, GW), F32)

    for g, out_ref in enumerate((qa_ref, ka_ref, va_ref)):
        cols = slice(g * GW, (g + 1) * GW)
        p = proj(g)
        pbuf[g, 8:8 + tt, :] = p
        y = p * cw_ref[CONV_K - 1:CONV_K, cols]
        for j in range(1, CONV_K):
            y = y + pbuf[g, 8 - j:8 - j + tt, :] * cw_ref[CONV_K - 1 - j:CONV_K - j, cols]
        pbuf[g, 0:8, :] = pbuf[g, tt:tt + 8, :]
        y = _silu(y)
        if g == 2:
            out_ref[0] = y.astype(out_ref.dtype)
        else:
            scale = DH ** -0.5 if g == 0 else 1.0
            for hh in range(HEADS):
                hs = slice(hh * DH, (hh + 1) * DH)
                yh = y[:, hs]
                inv = lax.rsqrt(jnp.sum(yh * yh, axis=-1, keepdims=True) + EPS)
                out_ref[0, :, hs] = (yh * inv * scale).astype(out_ref.dtype)

    ga_ref[0] = _silu(proj(3)).astype(ga_ref.dtype)

    ps = proj(8, GATE_LANES)
    lane = lax.broadcasted_iota(jnp.int32, ps.shape, 1)
    beta = jax.nn.sigmoid(ps)
    z = ps + dt_ref[...]
    softplus = jnp.maximum(z, 0.0) + jnp.log1p(jnp.exp(-jnp.abs(z)))
    g_log = -jnp.exp(alog_ref[...]) * softplus
    tri = tri_ref[...]
    gc = _cumsum_rows(tri, jnp.where((lane >= HEADS) & (lane < 2 * HEADS), g_log, 0.0))
    sm_ref[0] = jnp.where(lane < HEADS, beta, gc)

    hl = lb_ref[...]
    e = jnp.exp(hl - jnp.max(hl, axis=0, keepdims=True))
    lb = jnp.sum(e[0:layer + 1], axis=0, keepdims=True) / jnp.sum(e, axis=0, keepdims=True)
    fr = proj(4)
    logf = jnp.log(lb + (1.0 - lb) * jax.nn.sigmoid(fr))
    b_ref[0] = _cumsum_rows(tri, logf)
    kb_ref[0] = ((1.0 - lb) * jax.nn.sigmoid(-fr)).astype(kb_ref.dtype)
    ib_ref[0] = proj(5).astype(ib_ref.dtype)
    qb_ref[0] = _silu(proj(6)).astype(qb_ref.dtype)
    gb_ref[0] = _silu(proj(7)).astype(gb_ref.dtype)


def _inproj(layer, x, n1, sc1, sh1, w_all, conv_w, alog_pad, dt_pad, hg_lb, tri, tt):
    B, T, D = x.shape
    const = lambda shape: pl.BlockSpec(shape, lambda b, t: (0,) * len(shape))
    act = lambda dt: jax.ShapeDtypeStruct((B, T, GW), dt)
    tile = lambda w: pl.BlockSpec((1, tt, w), lambda b, t: (b, t, 0))
    per_batch = pl.BlockSpec((1, 1, D), lambda b, t: (b, 0, 0))
    return pl.pallas_call(
        functools.partial(_inproj_kernel, layer),
        grid=(B, T // tt),
        in_specs=[tile(D), const((1, D)), per_batch, per_batch,
                  const(w_all.shape), const(conv_w.shape), const((1, GATE_LANES)), const((1, GATE_LANES)),
                  const(hg_lb.shape), const((tt, tt))],
        out_specs=[tile(GW), tile(GW), tile(GW), tile(GW), tile(GATE_LANES), tile(GW),
                   tile(GW), tile(GW), tile(GW), tile(GW)],
        out_shape=[act(ACT), act(ACT), act(ACT), act(ACT),
                   jax.ShapeDtypeStruct((B, T, GATE_LANES), F32), act(F32),
                   act(ACT), act(ACT), act(ACT), act(ACT)],
        scratch_shapes=[pltpu.VMEM((3, tt + 8, GW), F32)],
        compiler_params=pltpu.CompilerParams(dimension_semantics=("arbitrary", "arbitrary"),
                                             vmem_limit_bytes=VMEM_LIMIT_BYTES),
        name="inproj",
    )(x, n1, sc1, sh1, w_all, conv_w, alog_pad, dt_pad, hg_lb, tri)


def _gdn_prep_kernel(q_ref, k_ref, v_ref, sm_ref, gct_ref, o_ref, qt_ref, m_ref, n_ref):
    tt = q_ref.shape[1]
    nc = tt // CHUNK
    row = lax.broadcasted_iota(jnp.int32, (CHUNK, CHUNK), 0)
    col = lax.broadcasted_iota(jnp.int32, (CHUNK, CHUNK), 1)
    causal = row >= col
    strict = row > col
    eye = jnp.where(row == col, 1.0, 0.0)
    lane = lax.broadcasted_iota(jnp.int32, (CHUNK, GATE_LANES), 1)

    def body(i, carry):
        chains = [(GDN_CHUNKS_PER_ITER * i + j, hh) for j in range(GDN_CHUNKS_PER_ITER) for hh in range(HEADS)]
        rows = [pl.ds(pl.multiple_of(c * CHUNK, CHUNK), CHUNK) for c, _ in chains]
        hs = [slice(hh * DH, (hh + 1) * DH) for _, hh in chains]
        n = range(len(chains))
        sm = [sm_ref[0, rows[j], :] for j in n]
        q = [q_ref[0, rows[j], hs[j]].astype(F32) for j in n]
        k = [k_ref[0, rows[j], hs[j]].astype(F32) for j in n]
        v = [v_ref[0, rows[j], hs[j]].astype(F32) for j in n]
        beta = [_lane_pick(sm[j], lane, chains[j][1]) for j in n]
        gcol = [_lane_pick(sm[j], lane, HEADS + chains[j][1]) for j in n]
        grow = [gct_ref[0, hh, pl.ds(c, 1), :] for c, hh in chains]
        decay = [jnp.exp(jnp.where(causal, gcol[j] - grow[j], -jnp.inf)) for j in n]
        kb = [k[j] * beta[j] for j in n]
        L = [jnp.where(strict, _dot_nt(kb[j], k[j]) * decay[j], 0.0) for j in n]
        tinv = [eye - L[j] for j in n]
        pw = [_dot3(L[j], L[j]) for j in n]
        for _ in range(4):
            tinv = [tinv[j] + _dot3(tinv[j], pw[j]) for j in n]
            pw = [_dot3(pw[j], pw[j]) for j in n]
        tinv = [tinv[j] + _dot3(tinv[j], pw[j]) for j in n]
        eg = [jnp.exp(gcol[j]) for j in n]
        sol = [_dot(tinv[j], jnp.concatenate([v[j] * beta[j], kb[j] * eg[j]], axis=1)) for j in n]
        attn = [_dot_nt(q[j], k[j]) * decay[j] for j in n]
        k_tail = [k[j] * jnp.exp(gcol[j][CHUNK - 1:CHUNK, :] - gcol[j]) for j in n]
        au = [_dot(attn[j], sol[j]) for j in n]
        ku = [_dot_tn(k_tail[j], sol[j]) for j in n]
        for j, (c, hh) in enumerate(chains):
            o_ref[0, rows[j], hs[j]] = au[j][:, :DH]
            qt_ref[0, rows[j], hs[j]] = (q[j] * eg[j] - au[j][:, DH:]).astype(qt_ref.dtype)
            n_ref[0, hh, c] = ku[j][:, :DH].astype(n_ref.dtype)
            m_ref[0, hh, c] = (-ku[j][:, DH:]).astype(m_ref.dtype)
        return carry

    lax.fori_loop(0, nc // GDN_CHUNKS_PER_ITER, body, 0)


def _gdn_scan_kernel(o_ref, qt_ref, m_ref, n_ref, gct_ref, sg_ref, nw_ref, out_ref, s_ref):
    nb, tt = o_ref.shape[0], o_ref.shape[1]
    nc = tt // CHUNK

    @pl.when(pl.program_id(0) == 0)
    def _():
        s_ref[...] = jnp.zeros(s_ref.shape, F32)

    nw = nw_ref[...]

    def body(c, carry):
        rows = pl.ds(pl.multiple_of(c * CHUNK, CHUNK), CHUNK)
        for b in range(nb):
            for hh in range(HEADS):
                hs = slice(hh * DH, (hh + 1) * DH)
                S = s_ref[b, hh]
                Sb = S.astype(BF16)
                glast = gct_ref[b, hh, pl.ds(c, 1), :][:, CHUNK - 1:CHUNK]
                o = o_ref[b, rows, hs] + jnp.dot(qt_ref[b, rows, hs], Sb, preferred_element_type=F32)
                s_ref[b, hh] = (S * jnp.exp(glast) + jnp.dot(m_ref[b, hh, c], Sb, preferred_element_type=F32)
                                + n_ref[b, hh, c].astype(F32))
                o = _rms(o) * nw * sg_ref[b, rows, hs].astype(F32)
                out_ref[b, rows, hs] = o.astype(out_ref.dtype)
        return carry

    lax.fori_loop(0, nc, body, 0)


def _gdn(q, k, v, sg, sm, gct, nw, tt):
    B, T, _ = q.shape
    nc = tt // CHUNK
    assert nc % GDN_CHUNKS_PER_ITER == 0
    n_chunks = T // CHUNK
    tile = lambda w: pl.BlockSpec((1, tt, w), lambda b, t: (b, t, 0))
    mat = jax.ShapeDtypeStruct((B, HEADS, n_chunks, DH, DH), ACT)
    o_part, qt, m, n = pl.pallas_call(
        _gdn_prep_kernel,
        grid=(B, T // tt),
        in_specs=[tile(GW), tile(GW), tile(GW), tile(GATE_LANES),
                  pl.BlockSpec((1, HEADS, nc, CHUNK), lambda b, t: (b, 0, t, 0))],
        out_specs=[tile(GW), tile(GW),
                   pl.BlockSpec((1, HEADS, nc, DH, DH), lambda b, t: (b, 0, t, 0, 0)),
                   pl.BlockSpec((1, HEADS, nc, DH, DH), lambda b, t: (b, 0, t, 0, 0))],
        out_shape=[jax.ShapeDtypeStruct((B, T, GW), F32), jax.ShapeDtypeStruct((B, T, GW), ACT), mat, mat],
        compiler_params=pltpu.CompilerParams(dimension_semantics=("arbitrary", "arbitrary"),
                                             vmem_limit_bytes=VMEM_LIMIT_BYTES),
        name="gdn_prep",
    )(q, k, v, sm, gct)

    full = lambda w: pl.BlockSpec((B, tt, w), lambda t: (0, t, 0))
    mats = pl.BlockSpec((B, HEADS, nc, DH, DH), lambda t: (0, 0, t, 0, 0))
    return pl.pallas_call(
        _gdn_scan_kernel,
        grid=(T // tt,),
        in_specs=[full(GW), full(GW), mats, mats,
                  pl.BlockSpec((B, HEADS, nc, CHUNK), lambda t: (0, 0, t, 0)),
                  full(GW), pl.BlockSpec((1, DH), lambda t: (0, 0))],
        out_specs=full(GW),
        out_shape=jax.ShapeDtypeStruct((B, T, GW), ACT),
        scratch_shapes=[pltpu.VMEM((B, HEADS, DH, DH), F32)],
        compiler_params=pltpu.CompilerParams(dimension_semantics=("arbitrary",),
                                             vmem_limit_bytes=VMEM_LIMIT_BYTES),
        name="gdn_scan",
    )(o_part, qt, m, n, gct, sg, nw)


def _hgrn_kernel(q_ref, k_ref, v_ref, b_ref, sg_ref, nw_ref, o_ref, st_ref):
    tt = q_ref.shape[1]
    nc = tt // CHUNK

    @pl.when(pl.program_id(1) == 0)
    def _():
        st_ref[...] = jnp.zeros(st_ref.shape, F32)

    row = lax.broadcasted_iota(jnp.int32, (CHUNK, CHUNK), 0)
    col = lax.broadcasted_iota(jnp.int32, (CHUNK, CHUNK), 1)
    diag_block = ((col // SUB) == (row // SUB)) & (col <= row)
    heads = range(HEADS)
    hs = [slice(hh * DH, (hh + 1) * DH) for hh in heads]

    def body(c, carry):
        rows = pl.ds(pl.multiple_of(c * CHUNK, CHUNK), CHUNK)
        q = [q_ref[0, rows, hs[h]].astype(F32) for h in heads]
        k = [k_ref[0, rows, hs[h]].astype(F32) for h in heads]
        v = [v_ref[0, rows, hs[h]].astype(F32) for h in heads]
        b = [b_ref[0, rows, hs[h]] for h in heads]
        blast = [b[h][CHUNK - 1:CHUNK, :] for h in heads]
        st = [st_ref[h] for h in heads]
        o = [_dot_nt(q[h] * jnp.exp(b[h]), st[h]) for h in heads]
        k_tail = [k[h] * jnp.exp(blast[h] - b[h]) for h in heads]
        for h in heads:
            st_ref[h] = st[h] * jnp.exp(blast[h]) + _dot_tn(v[h], k_tail[h])

        blocks = [[jnp.zeros((SUB, CHUNK), F32)] for _ in heads]
        for i in range(1, CHUNK // SUB):
            lo, hi = i * SUB, (i + 1) * SUB
            for h in heads:
                r = b[h][lo:lo + 1, :]
                qi = q[h][lo:hi] * jnp.exp(b[h][lo:hi] - r)
                kj = k[h][:lo] * jnp.exp(jnp.minimum(r - b[h][:lo], 0.0))
                kj = jnp.concatenate([kj, jnp.zeros((CHUNK - lo, DH), F32)], axis=0)
                blocks[h].append(_dot_nt(qi, kj))
        a = []
        for h in heads:
            f = jnp.exp(jnp.minimum(b[h] - pltpu.roll(b[h], 1, 0), 0.0))
            e = None
            a_diag = jnp.zeros((CHUNK, CHUNK), F32)
            for delta in range(SUB):
                if delta == 0:
                    term = q[h] * k[h]
                else:
                    fsh = f if delta == 1 else pltpu.roll(f, delta - 1, 0)
                    e = fsh if e is None else e * fsh
                    term = q[h] * pltpu.roll(k[h], delta, 0) * e
                colv = jnp.sum(term, axis=1, keepdims=True)
                a_diag = jnp.where(row - col == delta, colv, a_diag)
            a.append(jnp.where(diag_block, a_diag, jnp.concatenate(blocks[h], axis=0)))

        o = [o[h] + _dot(a[h], v[h]) for h in heads]
        o = jnp.concatenate(o, axis=1)
        o = _rms(o) * nw_ref[...] * sg_ref[0, rows, :].astype(F32)
        o_ref[0, rows, :] = o.astype(o_ref.dtype)
        return carry

    lax.fori_loop(0, nc, body, 0)


def _hgrn(q, k, v, b, sg, nw, tt):
    B, T, _ = q.shape
    tile = pl.BlockSpec((1, tt, GW), lambda bi, t: (bi, t, 0))
    return pl.pallas_call(
        _hgrn_kernel,
        grid=(B, T // tt),
        in_specs=[tile, tile, tile, tile, tile, pl.BlockSpec((1, GW), lambda bi, t: (0, 0))],
        out_specs=tile,
        out_shape=jax.ShapeDtypeStruct((B, T, GW), ACT),
        scratch_shapes=[pltpu.VMEM((HEADS, DH, DH), F32)],
        compiler_params=pltpu.CompilerParams(dimension_semantics=("arbitrary", "arbitrary"),
                                             vmem_limit_bytes=VMEM_LIMIT_BYTES),
        name="hgrn",
    )(q, k, v, b, sg, nw)


def _outproj_kernel(oa_ref, ob_ref, x_ref, wa_ref, wb_ref, g1_ref, n2_ref, sc_ref, sh_ref, wr_ref, rb_ref,
                    x1_ref, h2_ref, gates_ref):
    tm = x_ref.shape[1]
    mix = (jnp.dot(oa_ref[0], wa_ref[...], preferred_element_type=F32)
           + jnp.dot(ob_ref[0], wb_ref[...], preferred_element_type=F32))
    x1 = x_ref[0] + g1_ref[0] * mix
    x1_ref[0] = x1
    h2 = _rms(x1) * n2_ref[...]
    h2 = h2 * (1.0 + sc_ref[0]) + sh_ref[0]
    h2_ref[0] = h2.astype(h2_ref.dtype)

    scores = jax.nn.sigmoid(_dot3(wr_ref[...], h2, dot=_dot_nt))
    sel = scores + rb_ref[...]
    sub = lax.broadcasted_iota(jnp.int32, (GROUP_SIZE, tm), 0)
    neg = -jnp.inf

    def take_max(blk):
        m = jnp.max(blk, axis=0, keepdims=True)
        first = jnp.min(jnp.where(blk == m, sub, GROUP_SIZE), axis=0, keepdims=True)
        hit = sub == first
        return m, hit, jnp.where(hit, neg, blk)

    sel_blk = [sel[g * GROUP_SIZE:(g + 1) * GROUP_SIZE] for g in range(N_GROUPS)]
    group_score = jnp.zeros((N_GROUPS, tm), F32)
    for g in range(N_GROUPS):
        m1, _, rest = take_max(sel_blk[g])
        m2 = jnp.max(rest, axis=0, keepdims=True)
        group_score = jnp.where(sub == g, m1 + m2, group_score)
    group_on = jnp.zeros((N_GROUPS, tm), F32)
    for _ in range(TOPK_GROUPS):
        _, hit, group_score = take_max(group_score)
        group_on = jnp.where(hit, 1.0, group_on)

    cand = [jnp.where(group_on[g:g + 1] > 0.0, sel_blk[g], neg) for g in range(N_GROUPS)]
    picked = [jnp.zeros((GROUP_SIZE, tm), F32) for _ in range(N_GROUPS)]
    for _ in range(TOP_K):
        m = jnp.max(functools.reduce(jnp.maximum, cand), axis=0, keepdims=True)
        first = functools.reduce(jnp.minimum, [jnp.where(cand[g] == m, sub + g * GROUP_SIZE, N_EXPERTS)
                                               for g in range(N_GROUPS)])
        first = jnp.min(first, axis=0, keepdims=True)
        for g in range(N_GROUPS):
            hit = (sub + g * GROUP_SIZE) == first
            picked[g] = jnp.where(hit, 1.0, picked[g])
            cand[g] = jnp.where(hit, neg, cand[g])

    wts = [jnp.where(picked[g] > 0.0, scores[g * GROUP_SIZE:(g + 1) * GROUP_SIZE], 0.0) for g in range(N_GROUPS)]
    denom = jnp.sum(functools.reduce(jnp.add, wts), axis=0, keepdims=True)
    gates = [w / denom * ROUTE_SCALE for w in wts]
    shared_one = jnp.where(sub == 0, 1.0, 0.0)
    pad = jnp.zeros((GATE_LANES - N_EXPERTS - GROUP_SIZE, tm), F32)
    gates_t = jnp.concatenate(gates + [shared_one, pad], axis=0)
    gates_ref[0] = gates_t.T


def _outproj(oa, ob, x, wa, wb, g1, n2, sc2, sh2, wr_t, rb, tm):
    B, T, D = x.shape
    const = lambda shape: pl.BlockSpec(shape, lambda b, t: (0,) * len(shape))
    tile = lambda w: pl.BlockSpec((1, tm, w), lambda b, t: (b, t, 0))
    per_batch = pl.BlockSpec((1, 1, D), lambda b, t: (b, 0, 0))
    return pl.pallas_call(
        _outproj_kernel,
        grid=(B, T // tm),
        in_specs=[tile(GW), tile(GW), tile(D), const(wa.shape), const(wb.shape), per_batch,
                  const((1, D)), per_batch, per_batch, const(wr_t.shape), const(rb.shape)],
        out_specs=[tile(D), tile(D), tile(GATE_LANES)],
        out_shape=[jax.ShapeDtypeStruct((B, T, D), F32), jax.ShapeDtypeStruct((B, T, D), BF16),
                   jax.ShapeDtypeStruct((B, T, GATE_LANES), F32)],
        compiler_params=pltpu.CompilerParams(dimension_semantics=("arbitrary", "arbitrary"),
                                             vmem_limit_bytes=VMEM_LIMIT_BYTES),
        name="outproj",
    )(oa, ob, x, wa, wb, g1, n2, sc2, sh2, wr_t, rb)


def _moe_kernel(h_ref, gates_ref, wgu_ref, wd_ref, x1_ref, g2_ref, fw_ref, o_ref, acc_ref):
    e = pl.program_id(2)

    @pl.when(e == 0)
    def _():
        acc_ref[...] = jnp.zeros(acc_ref.shape, F32)

    gu = jnp.dot(h_ref[0], wgu_ref[0], preferred_element_type=F32)
    gates = gates_ref[0]
    lane = lax.broadcasted_iota(jnp.int32, gates.shape, 1)
    gate = _lane_pick(gates, lane, e)
    act = _silu(gu[:, :D_EXPERT]) * gu[:, D_EXPERT:] * gate
    acc_ref[...] += jnp.dot(act.astype(BF16), wd_ref[0], preferred_element_type=F32)

    @pl.when(e == pl.num_programs(2) - 1)
    def _():
        y = x1_ref[0] + g2_ref[0] * acc_ref[...]
        o_ref[0] = _rms(y) * fw_ref[...]


def _moe(h2, gates, wgu, wd, x1, g2, fw, tm):
    B, T, D = x1.shape
    ne = wgu.shape[0]
    tile = lambda w: pl.BlockSpec((1, tm, w), lambda b, t, e: (b, t, 0))
    return pl.pallas_call(
        _moe_kernel,
        grid=(B, T // tm, ne),
        in_specs=[tile(D), tile(GATE_LANES),
                  pl.BlockSpec((1, D, 2 * D_EXPERT), lambda b, t, e: (e, 0, 0)),
                  pl.BlockSpec((1, D_EXPERT, D), lambda b, t, e: (e, 0, 0)),
                  tile(D),
                  pl.BlockSpec((1, 1, D), lambda b, t, e: (b, 0, 0)),
                  pl.BlockSpec((1, D), lambda b, t, e: (0, 0))],
        out_specs=tile(D),
        out_shape=jax.ShapeDtypeStruct((B, T, D), F32),
        scratch_shapes=[pltpu.VMEM((tm, D), F32)],
        compiler_params=pltpu.CompilerParams(dimension_semantics=("arbitrary", "arbitrary", "arbitrary"),
                                             vmem_limit_bytes=VMEM_LIMIT_BYTES),
        name="moe",
    )(h2, gates, wgu, wd, x1, g2, fw)


def _pick_tile(n, want):
    t = min(n, want)
    assert n % t == 0 and t % CHUNK == 0, (n, want)
    return t


def kernel(x, c, w_ada, b_ada, norm1_w, w_in, conv_w, gdn_a_log, gdn_dt_bias, gdn_norm_w, hg_lb, hg_norm_w,
           w_out, norm2_w, w_router, router_bias, w_gate, w_up, w_down, ws_gate, ws_up, ws_down, final_norm_w):
    B, T, D = x.shape
    depth = w_ada.shape[0]
    assert depth == 1 and T % CHUNK == 0 and B <= 8
    layer = 0
    tt = _pick_tile(T, 512)
    tm_moe = _pick_tile(T, 1024)

    c_pad = jnp.pad(c, ((0, 8 - B), (0, 0)))
    mod = _ada(c_pad, w_ada[layer], b_ada[layer].reshape(1, -1))[:B]
    sh1, sc1, g1, sh2, sc2, g2 = (m.reshape(B, 1, D) for m in jnp.split(mod, 6, axis=-1))

    w = w_in[layer]
    qkv_w = 3 * GW
    sizes = (GW, HEADS, HEADS, GW, GW, GW, GW)
    offs = [qkv_w]
    for s in sizes:
        offs.append(offs[-1] + s)
    seg = lambda i: w[:, offs[i]:offs[i + 1]]
    small = jnp.pad(jnp.concatenate([seg(1), seg(2)], axis=1), ((0, 0), (0, GATE_LANES - 2 * HEADS)))
    w_all = jnp.concatenate([w[:, :qkv_w], seg(0), seg(3), seg(4), seg(5), seg(6), small], axis=1).astype(BF16)
    lane_pad = lambda v: jnp.pad(v.astype(F32).reshape(1, HEADS), ((0, 0), (HEADS, GATE_LANES - 2 * HEADS)))
    idx = jnp.arange(tt)
    tri = ((idx[:, None] >= idx[None, :]) & (idx[:, None] // CHUNK == idx[None, :] // CHUNK)).astype(BF16)

    qa, ka, va, ga, sm, bcum, kb, ib, qb, gb = _inproj(
        layer, x, norm1_w[layer].reshape(1, D), sc1, sh1, w_all, conv_w[layer].astype(F32),
        lane_pad(gdn_a_log[layer]), lane_pad(gdn_dt_bias[layer]), hg_lb.astype(F32), tri, tt)

    gct = sm[:, :, HEADS:2 * HEADS].transpose(0, 2, 1).reshape(B, HEADS, T // CHUNK, CHUNK)
    oa = _gdn(qa, ka, va, ga, sm, gct, gdn_norm_w[layer].reshape(1, DH), tt)
    ob = _hgrn(qb, kb, ib, bcum, gb, hg_norm_w[layer].reshape(1, GW), tt)

    wo = w_out[layer].astype(BF16)
    x1, h2, gates = _outproj(oa, ob, x, wo[:GW], wo[GW:], g1, norm2_w[layer].reshape(1, D), sc2, sh2,
                             w_router[layer].T, router_bias[layer].reshape(N_EXPERTS, 1), tt)

    wgu = jnp.concatenate([jnp.concatenate([w_gate[layer], w_up[layer]], axis=-1),
                           jnp.concatenate([ws_gate[layer], ws_up[layer]], axis=-1)[None]], axis=0).astype(BF16)
    wd = jnp.concatenate([w_down[layer], ws_down[layer][None]], axis=0).astype(BF16)
    return _moe(h2, gates, wgu, wd, x1, g2, final_norm_w.reshape(1, D), tm_moe)
```

```python
import functools

import jax
import jax.numpy as jnp
from jax import lax
from jax.experimental import pallas as pl
from jax.experimental.pallas import tpu as pltpu

F32 = jnp.float32
BF16 = jnp.bfloat16

EPS = 1e-6
CHUNK = 64
SUB = 8
HEADS = 4
DH = 128
GW = HEADS * DH
CONV_K = 4
N_EXPERTS = 64
N_GROUPS = 8
GROUP_SIZE = N_EXPERTS // N_GROUPS
TOPK_GROUPS = 4
TOP_K = 8
D_EXPERT = 256
ROUTE_SCALE = 2.5
GATE_LANES = 128
GDN_CHUNKS_PER_ITER = 2
MOE_ROWS = 256
ROW_TILE, LANES = 8, 128

VMEM_LIMIT_BYTES = 56 * 1024 * 1024

ACT = BF16


def _silu(x):
    return x * jax.nn.sigmoid(x)


def _dot(a, b):
    return jnp.dot(a.astype(BF16), b.astype(BF16), preferred_element_type=F32)


def _dot_nt(a, b):
    return lax.dot_general(a.astype(BF16), b.astype(BF16), (((1,), (1,)), ((), ())),
                           preferred_element_type=F32)


def _dot_tn(a, b):
    return lax.dot_general(a.astype(BF16), b.astype(BF16), (((0,), (0,)), ((), ())),
                           preferred_element_type=F32)


def _split2(x):
    hi = x.astype(BF16)
    lo = (x - hi.astype(F32)).astype(BF16)
    return hi, lo


def _dot3(a, b, dot=_dot):
    ah, al = _split2(a)
    bh, bl = _split2(b)
    return dot(ah, bh) + dot(ah, bl) + dot(al, bh)


def _cumsum_rows(tri, x):
    hi = x.astype(BF16)
    r = x - hi.astype(F32)
    mid = r.astype(BF16)
    lo = (r - mid.astype(F32)).astype(BF16)
    return (jnp.dot(tri, hi, preferred_element_type=F32)
            + jnp.dot(tri, mid, preferred_element_type=F32)
            + jnp.dot(tri, lo, preferred_element_type=F32))


def _lane_pick(tile, lane, idx):
    return jnp.sum(jnp.where(lane == idx, tile, 0.0), axis=1, keepdims=True)


def _rms(x):
    return x * lax.rsqrt(jnp.mean(x * x, axis=-1, keepdims=True) + EPS)


def _load_row_tiled(ref, n_rows, lead=()):
    return jnp.concatenate([ref[lead + (pl.ds(s, n_rows, stride=ROW_TILE), slice(None))] for s in range(ROW_TILE)],
                           axis=1)


def _store_row_tiled(ref, x, lead=()):
    n_rows = x.shape[0]
    for s in range(ROW_TILE):
        ref[lead + (pl.ds(s, n_rows, stride=ROW_TILE), slice(None))] = x[:, s * LANES:(s + 1) * LANES]


def _ada_kernel(c_ref, w_ref, b_ref, o_ref):
    ca = _silu(c_ref[...])
    o_ref[...] = _dot3(ca, w_ref[...]) + b_ref[...]


def _ada(c_pad, w, b):
    rows, d = c_pad.shape
    n = w.shape[1]
    tn = 1024
    return pl.pallas_call(
        _ada_kernel,
        grid=(n // tn,),
        in_specs=[pl.BlockSpec((rows, d), lambda j: (0, 0)),
                  pl.BlockSpec((d, tn), lambda j: (0, j)),
                  pl.BlockSpec((1, tn), lambda j: (0, j))],
        out_specs=pl.BlockSpec((rows, tn), lambda j: (0, j)),
        out_shape=jax.ShapeDtypeStruct((rows, n), F32),
        compiler_params=pltpu.CompilerParams(dimension_semantics=("arbitrary",),
                                             vmem_limit_bytes=VMEM_LIMIT_BYTES),
        name="ada",
    )(c_pad, w, b)


def _inproj_kernel(layer, x_ref, n1_ref, sc_ref, sh_ref, w_ref, cw_ref, alog_ref, dt_ref, lb_ref, tri_ref,
                   qa_ref, ka_ref, va_ref, ga_ref, sm_ref, b_ref, kb_ref, ib_ref, qb_ref, gb_ref,
                   pbuf):
    tt = x_ref.shape[1]
    t = pl.program_id(1)

    h = _rms(x_ref[0]) * n1_ref[...]
    h = h * (1.0 + sc_ref[0]) + sh_ref[0]
    hb = h.astype(BF16)

    def proj(g, width=GW):
        return jnp.dot(hb, w_ref[:, g * GW:g * GW + width], preferred_element_type=F32)

    @pl.when(t == 0)
    def _():
        pbuf[:, 0:8, :] = jnp.zeros((3, 8, GW), F32)

    for g, out_ref in enumerate((qa_ref, ka_ref, va_ref)):
        cols = slice(g * GW, (g + 1) * GW)
        p = proj(g)
        pbuf[g, 8:8 + tt, :] = p
        y = p * cw_ref[CONV_K - 1:CONV_K, cols]
        for j in range(1, CONV_K):
            y = y + pbuf[g, 8 - j:8 - j + tt, :] * cw_ref[CONV_K - 1 - j:CONV_K - j, cols]
        pbuf[g, 0:8, :] = pbuf[g, tt:tt + 8, :]
        y = _silu(y)
        if g == 2:
            out_ref[0] = y.astype(out_ref.dtype)
        else:
            scale = DH ** -0.5 if g == 0 else 1.0
            for hh in range(HEADS):
                hs = slice(hh * DH, (hh + 1) * DH)
                yh = y[:, hs]
                inv = lax.rsqrt(jnp.sum(yh * yh, axis=-1, keepdims=True) + EPS)
                out_ref[0, :, hs] = (yh * inv * scale).astype(out_ref.dtype)

    ga_ref[0] = _silu(proj(3)).astype(ga_ref.dtype)

    ps = proj(8, GATE_LANES)
    lane = lax.broadcasted_iota(jnp.int32, ps.shape, 1)
    beta = jax.nn.sigmoid(ps)
    z = ps + dt_ref[...]
    softplus = jnp.maximum(z, 0.0) + jnp.log1p(jnp.exp(-jnp.abs(z)))
    g_log = -jnp.exp(alog_ref[...]) * softplus
    tri = tri_ref[...]
    gc = _cumsum_rows(tri, jnp.where((lane >= HEADS) & (lane < 2 * HEADS), g_log, 0.0))
    sm_ref[0] = jnp.where(lane < HEADS, beta, gc)

    hl = lb_ref[...]
    e = jnp.exp(hl - jnp.max(hl, axis=0, keepdims=True))
    lb = jnp.sum(e[0:layer + 1], axis=0, keepdims=True) / jnp.sum(e, axis=0, keepdims=True)
    fr = proj(4)
    logf = jnp.log(lb + (1.0 - lb) * jax.nn.sigmoid(fr))
    b_ref[0] = _cumsum_rows(tri, logf)
    kb_ref[0] = ((1.0 - lb) * jax.nn.sigmoid(-fr)).astype(kb_ref.dtype)
    ib_ref[0] = proj(5).astype(ib_ref.dtype)
    qb_ref[0] = _silu(proj(6)).astype(qb_ref.dtype)
    gb_ref[0] = _silu(proj(7)).astype(gb_ref.dtype)


def _inproj(layer, x, n1, sc1, sh1, w_all, conv_w, alog_pad, dt_pad, hg_lb, tri, tt):
    B, T, D = x.shape
    const = lambda shape: pl.BlockSpec(shape, lambda b, t: (0,) * len(shape))
    act = lambda dt: jax.ShapeDtypeStruct((B, T, GW), dt)
    tile = lambda w: pl.BlockSpec((1, tt, w), lambda b, t: (b, t, 0))
    per_batch = pl.BlockSpec((1, 1, D), lambda b, t: (b, 0, 0))
    return pl.pallas_call(
        functools.partial(_inproj_kernel, layer),
        grid=(B, T // tt),
        in_specs=[tile(D), const((1, D)), per_batch, per_batch,
                  const(w_all.shape), const(conv_w.shape), const((1, GATE_LANES)), const((1, GATE_LANES)),
                  const(hg_lb.shape), const((tt, tt))],
        out_specs=[tile(GW), tile(GW), tile(GW), tile(GW), tile(GATE_LANES), tile(GW),
                   tile(GW), tile(GW), tile(GW), tile(GW)],
        out_shape=[act(ACT), act(ACT), act(ACT), act(ACT),
                   jax.ShapeDtypeStruct((B, T, GATE_LANES), F32), act(F32),
                   act(ACT), act(ACT), act(ACT), act(ACT)],
        scratch_shapes=[pltpu.VMEM((3, tt + 8, GW), F32)],
        compiler_params=pltpu.CompilerParams(dimension_semantics=("arbitrary", "arbitrary"),
                                             vmem_limit_bytes=VMEM_LIMIT_BYTES),
        name="inproj",
    )(x, n1, sc1, sh1, w_all, conv_w, alog_pad, dt_pad, hg_lb, tri)


def _gdn_prep_kernel(q_ref, k_ref, v_ref, sm_ref, gct_ref, o_ref, qt_ref, m_ref, n_ref):
    tt = q_ref.shape[1]
    nc = tt // CHUNK
    row = lax.broadcasted_iota(jnp.int32, (CHUNK, CHUNK), 0)
    col = lax.broadcasted_iota(jnp.int32, (CHUNK, CHUNK), 1)
    causal = row >= col
    strict = row > col
    eye = jnp.where(row == col, 1.0, 0.0)
    lane = lax.broadcasted_iota(jnp.int32, (CHUNK, GATE_LANES), 1)

    def body(i, carry):
        chains = [(GDN_CHUNKS_PER_ITER * i + j, hh) for j in range(GDN_CHUNKS_PER_ITER) for hh in range(HEADS)]
        rows = [pl.ds(pl.multiple_of(c * CHUNK, CHUNK), CHUNK) for c, _ in chains]
        hs = [slice(hh * DH, (hh + 1) * DH) for _, hh in chains]
        n = range(len(chains))
        sm = [sm_ref[0, rows[j], :] for j in n]
        q = [q_ref[0, rows[j], hs[j]].astype(F32) for j in n]
        k = [k_ref[0, rows[j], hs[j]].astype(F32) for j in n]
        v = [v_ref[0, rows[j], hs[j]].astype(F32) for j in n]
        beta = [_lane_pick(sm[j], lane, chains[j][1]) for j in n]
        gcol = [_lane_pick(sm[j], lane, HEADS + chains[j][1]) for j in n]
        grow = [gct_ref[0, hh, pl.ds(c, 1), :] for c, hh in chains]
        decay = [jnp.exp(jnp.where(causal, gcol[j] - grow[j], -jnp.inf)) for j in n]
        kb = [k[j] * beta[j] for j in n]
        L = [jnp.where(strict, _dot_nt(kb[j], k[j]) * decay[j], 0.0) for j in n]
        tinv = [eye - L[j] for j in n]
        pw = [_dot3(L[j], L[j]) for j in n]
        for _ in range(4):
            tinv = [tinv[j] + _dot3(tinv[j], pw[j]) for j in n]
            pw = [_dot3(pw[j], pw[j]) for j in n]
        tinv = [tinv[j] + _dot3(tinv[j], pw[j]) for j in n]
        eg = [jnp.exp(gcol[j]) for j in n]
        sol = [_dot(tinv[j], jnp.concatenate([v[j] * beta[j], kb[j] * eg[j]], axis=1)) for j in n]
        attn = [_dot_nt(q[j], k[j]) * decay[j] for j in n]
        k_tail = [k[j] * jnp.exp(gcol[j][CHUNK - 1:CHUNK, :] - gcol[j]) for j in n]
        au = [_dot(attn[j], sol[j]) for j in n]
        ku = [_dot_tn(k_tail[j], sol[j]) for j in n]
        for j, (c, hh) in enumerate(chains):
            o_ref[0, rows[j], hs[j]] = au[j][:, :DH]
            qt_ref[0, rows[j], hs[j]] = (q[j] * eg[j] - au[j][:, DH:]).astype(qt_ref.dtype)
            n_ref[0, hh, c] = ku[j][:, :DH].astype(n_ref.dtype)
            m_ref[0, hh, c] = (-ku[j][:, DH:]).astype(m_ref.dtype)
        return carry

    lax.fori_loop(0, nc // GDN_CHUNKS_PER_ITER, body, 0)


def _gdn_scan_kernel(o_ref, qt_ref, m_ref, n_ref, gct_ref, sg_ref, nw_ref, out_ref, s_ref):
    nb, tt = o_ref.shape[0], o_ref.shape[1]
    nc = tt // CHUNK

    @pl.when(pl.program_id(0) == 0)
    def _():
        s_ref[...] = jnp.zeros(s_ref.shape, F32)

    nw = nw_ref[...]

    def body(c, carry):
        rows = pl.ds(pl.multiple_of(c * CHUNK, CHUNK), CHUNK)
        for b in range(nb):
            for hh in range(HEADS):
                hs = slice(hh * DH, (hh + 1) * DH)
                S = s_ref[b, hh]
                Sb = S.astype(BF16)
                glast = gct_ref[b, hh, pl.ds(c, 1), :][:, CHUNK - 1:CHUNK]
                o = o_ref[b, rows, hs] + jnp.dot(qt_ref[b, rows, hs], Sb, preferred_element_type=F32)
                s_ref[b, hh] = (S * jnp.exp(glast) + jnp.dot(m_ref[b, hh, c], Sb, preferred_element_type=F32)
                                + n_ref[b, hh, c].astype(F32))
                o = _rms(o) * nw * sg_ref[b, rows, hs].astype(F32)
                out_ref[b, rows, hs] = o.astype(out_ref.dtype)
        return carry

    lax.fori_loop(0, nc, body, 0)


def _gdn(q, k, v, sg, sm, gct, nw, tt):
    B, T, _ = q.shape
    nc = tt // CHUNK
    assert nc % GDN_CHUNKS_PER_ITER == 0
    n_chunks = T // CHUNK
    tile = lambda w: pl.BlockSpec((1, tt, w), lambda b, t: (b, t, 0))
    mat = jax.ShapeDtypeStruct((B, HEADS, n_chunks, DH, DH), ACT)
    o_part, qt, m, n = pl.pallas_call(
        _gdn_prep_kernel,
        grid=(B, T // tt),
        in_specs=[tile(GW), tile(GW), tile(GW), tile(GATE_LANES),
                  pl.BlockSpec((1, HEADS, nc, CHUNK), lambda b, t: (b, 0, t, 0))],
        out_specs=[tile(GW), tile(GW),
                   pl.BlockSpec((1, HEADS, nc, DH, DH), lambda b, t: (b, 0, t, 0, 0)),
                   pl.BlockSpec((1, HEADS, nc, DH, DH), lambda b, t: (b, 0, t, 0, 0))],
        out_shape=[jax.ShapeDtypeStruct((B, T, GW), F32), jax.ShapeDtypeStruct((B, T, GW), ACT), mat, mat],
        compiler_params=pltpu.CompilerParams(dimension_semantics=("arbitrary", "arbitrary"),
                                             vmem_limit_bytes=VMEM_LIMIT_BYTES),
        name="gdn_prep",
    )(q, k, v, sm, gct)

    full = lambda w: pl.BlockSpec((B, tt, w), lambda t: (0, t, 0))
    mats = pl.BlockSpec((B, HEADS, nc, DH, DH), lambda t: (0, 0, t, 0, 0))
    return pl.pallas_call(
        _gdn_scan_kernel,
        grid=(T // tt,),
        in_specs=[full(GW), full(GW), mats, mats,
                  pl.BlockSpec((B, HEADS, nc, CHUNK), lambda t: (0, 0, t, 0)),
                  full(GW), pl.BlockSpec((1, DH), lambda t: (0, 0))],
        out_specs=full(GW),
        out_shape=jax.ShapeDtypeStruct((B, T, GW), ACT),
        scratch_shapes=[pltpu.VMEM((B, HEADS, DH, DH), F32)],
        compiler_params=pltpu.CompilerParams(dimension_semantics=("arbitrary",),
                                             vmem_limit_bytes=VMEM_LIMIT_BYTES),
        name="gdn_scan",
    )(o_part, qt, m, n, gct, sg, nw)


def _hgrn_kernel(q_ref, k_ref, v_ref, b_ref, sg_ref, nw_ref, o_ref, st_ref):
    tt = q_ref.shape[1]
    nc = tt // CHUNK

    @pl.when(pl.program_id(1) == 0)
    def _():
        st_ref[...] = jnp.zeros(st_ref.shape, F32)

    row = lax.broadcasted_iota(jnp.int32, (CHUNK, CHUNK), 0)
    col = lax.broadcasted_iota(jnp.int32, (CHUNK, CHUNK), 1)
    diag_block = ((col // SUB) == (row // SUB)) & (col <= row)
    heads = range(HEADS)
    hs = [slice(hh * DH, (hh + 1) * DH) for hh in heads]

    def body(c, carry):
        rows = pl.ds(pl.multiple_of(c * CHUNK, CHUNK), CHUNK)
        q = [q_ref[0, rows, hs[h]].astype(F32) for h in heads]
        k = [k_ref[0, rows, hs[h]].astype(F32) for h in heads]
        v = [v_ref[0, rows, hs[h]].astype(F32) for h in heads]
        b = [b_ref[0, rows, hs[h]] for h in heads]
        blast = [b[h][CHUNK - 1:CHUNK, :] for h in heads]
        st = [st_ref[h] for h in heads]
        o = [_dot_nt(q[h] * jnp.exp(b[h]), st[h]) for h in heads]
        k_tail = [k[h] * jnp.exp(blast[h] - b[h]) for h in heads]
        for h in heads:
            st_ref[h] = st[h] * jnp.exp(blast[h]) + _dot_tn(v[h], k_tail[h])

        blocks = [[jnp.zeros((SUB, CHUNK), F32)] for _ in heads]
        for i in range(1, CHUNK // SUB):
            lo, hi = i * SUB, (i + 1) * SUB
            for h in heads:
                r = b[h][lo:lo + 1, :]
                qi = q[h][lo:hi] * jnp.exp(b[h][lo:hi] - r)
                kj = k[h][:lo] * jnp.exp(jnp.minimum(r - b[h][:lo], 0.0))
                kj = jnp.concatenate([kj, jnp.zeros((CHUNK - lo, DH), F32)], axis=0)
                blocks[h].append(_dot_nt(qi, kj))
        a = []
        for h in heads:
            f = jnp.exp(jnp.minimum(b[h] - pltpu.roll(b[h], 1, 0), 0.0))
            e = None
            a_diag = jnp.zeros((CHUNK, CHUNK), F32)
            for delta in range(SUB):
                if delta == 0:
                    term = q[h] * k[h]
                else:
                    fsh = f if delta == 1 else pltpu.roll(f, delta - 1, 0)
                    e = fsh if e is None else e * fsh
                    term = q[h] * pltpu.roll(k[h], delta, 0) * e
                colv = jnp.sum(term, axis=1, keepdims=True)
                a_diag = jnp.where(row - col == delta, colv, a_diag)
            a.append(jnp.where(diag_block, a_diag, jnp.concatenate(blocks[h], axis=0)))

        o = [o[h] + _dot(a[h], v[h]) for h in heads]
        o = jnp.concatenate(o, axis=1)
        o = _rms(o) * nw_ref[...] * sg_ref[0, rows, :].astype(F32)
        o_ref[0, rows, :] = o.astype(o_ref.dtype)
        return carry

    lax.fori_loop(0, nc, body, 0)


def _hgrn(q, k, v, b, sg, nw, tt):
    B, T, _ = q.shape
    tile = pl.BlockSpec((1, tt, GW), lambda bi, t: (bi, t, 0))
    return pl.pallas_call(
        _hgrn_kernel,
        grid=(B, T // tt),
        in_specs=[tile, tile, tile, tile, tile, pl.BlockSpec((1, GW), lambda bi, t: (0, 0))],
        out_specs=tile,
        out_shape=jax.ShapeDtypeStruct((B, T, GW), ACT),
        scratch_shapes=[pltpu.VMEM((HEADS, DH, DH), F32)],
        compiler_params=pltpu.CompilerParams(dimension_semantics=("arbitrary", "arbitrary"),
                                             vmem_limit_bytes=VMEM_LIMIT_BYTES),
        name="hgrn",
    )(q, k, v, b, sg, nw)


def _outproj_kernel(oa_ref, ob_ref, x_ref, wa_ref, wb_ref, g1_ref, n2_ref, sc_ref, sh_ref, wr_ref, rb_ref,
                    x1_ref, h2_ref, eidx_ref, rank_ref, wts_ref, cnt_ref):
    tm = x_ref.shape[1]

    @pl.when((pl.program_id(0) == 0) & (pl.program_id(1) == 0))
    def _():
        cnt_ref[...] = jnp.zeros(cnt_ref.shape, F32)

    mix = (jnp.dot(oa_ref[0], wa_ref[...], preferred_element_type=F32)
           + jnp.dot(ob_ref[0], wb_ref[...], preferred_element_type=F32))
    x1 = x_ref[0] + g1_ref[0] * mix
    x1_ref[0] = x1
    h2 = _rms(x1) * n2_ref[...]
    h2 = h2 * (1.0 + sc_ref[0]) + sh_ref[0]
    _store_row_tiled(h2_ref, h2, lead=(0,))

    scores = jax.nn.sigmoid(_dot3(wr_ref[...], h2, dot=_dot_nt))
    sel = scores + rb_ref[...]
    sub = lax.broadcasted_iota(jnp.int32, (GROUP_SIZE, tm), 0)
    neg = -jnp.inf
    groups = range(N_GROUPS)

    def take_max(blk):
        m = jnp.max(blk, axis=0, keepdims=True)
        first = jnp.min(jnp.where(blk == m, sub, GROUP_SIZE), axis=0, keepdims=True)
        hit = sub == first
        return m, hit, jnp.where(hit, neg, blk)

    blk_of = lambda a, g: a[g * GROUP_SIZE:(g + 1) * GROUP_SIZE]
    sel_blk = [blk_of(sel, g) for g in groups]
    group_score = jnp.zeros((N_GROUPS, tm), F32)
    for g in groups:
        m1, _, rest = take_max(sel_blk[g])
        m2 = jnp.max(rest, axis=0, keepdims=True)
        group_score = jnp.where(sub == g, m1 + m2, group_score)
    group_on = jnp.zeros((N_GROUPS, tm), F32)
    for _ in range(TOPK_GROUPS):
        _, hit, group_score = take_max(group_score)
        group_on = jnp.where(hit, 1.0, group_on)

    cand = [jnp.where(group_on[g:g + 1] > 0.0, sel_blk[g], neg) for g in groups]
    picked = [jnp.zeros((GROUP_SIZE, tm), F32) for _ in groups]
    chosen = []
    for _ in range(TOP_K):
        m = jnp.max(functools.reduce(jnp.maximum, cand), axis=0, keepdims=True)
        first = functools.reduce(jnp.minimum, [jnp.where(cand[g] == m, sub + g * GROUP_SIZE, N_EXPERTS)
                                               for g in groups])
        first = jnp.min(first, axis=0, keepdims=True)
        chosen.append(first)
        for g in groups:
            hit = (sub + g * GROUP_SIZE) == first
            picked[g] = jnp.where(hit, 1.0, picked[g])
            cand[g] = jnp.where(hit, neg, cand[g])

    picked_all = jnp.concatenate(picked, axis=0)
    r_i = lax.broadcasted_iota(jnp.int32, (tm, tm), 0)
    c_i = lax.broadcasted_iota(jnp.int32, (tm, tm), 1)
    earlier = jnp.where(r_i < c_i, 1.0, 0.0).astype(BF16)
    before = jnp.dot(picked_all.astype(BF16), earlier, preferred_element_type=F32) + cnt_ref[:, 0:1]
    cnt_ref[...] = cnt_ref[...] + jnp.sum(picked_all, axis=1, keepdims=True)

    def pick_value(table, first):
        parts = [jnp.where((sub + g * GROUP_SIZE) == first, blk_of(table, g), 0.0) for g in groups]
        return jnp.sum(functools.reduce(jnp.add, parts), axis=0, keepdims=True)

    w_k = [pick_value(scores, f) for f in chosen]
    denom = functools.reduce(jnp.add, w_k)
    eidx = jnp.zeros((TOP_K, tm), jnp.int32)
    rank = jnp.zeros((TOP_K, tm), jnp.int32)
    wts = jnp.zeros((TOP_K, tm), F32)
    for k in range(TOP_K):
        eidx = jnp.where(sub == k, chosen[k], eidx)
        rank = jnp.where(sub == k, pick_value(before, chosen[k]).astype(jnp.int32), rank)
        wts = jnp.where(sub == k, w_k[k] / denom * ROUTE_SCALE, wts)
    eidx_ref[...] = eidx
    rank_ref[...] = rank
    pad = jnp.zeros((GATE_LANES - TOP_K, tm), F32)
    wts_ref[0] = jnp.concatenate([wts, pad], axis=0).T


def _outproj(oa, ob, x, wa, wb, g1, n2, sc2, sh2, wr_t, rb, tm):
    B, T, D = x.shape
    nt = T // tm
    const = lambda shape: pl.BlockSpec(shape, lambda b, t: (0,) * len(shape))
    tile = lambda w: pl.BlockSpec((1, tm, w), lambda b, t: (b, t, 0))
    per_batch = pl.BlockSpec((1, 1, D), lambda b, t: (b, 0, 0))
    picks = pl.BlockSpec((TOP_K, tm), lambda b, t: (0, b * nt + t))
    return pl.pallas_call(
        _outproj_kernel,
        grid=(B, nt),
        in_specs=[tile(GW), tile(GW), tile(D), const(wa.shape), const(wb.shape), per_batch,
                  const((1, D)), per_batch, per_batch, const(wr_t.shape), const(rb.shape)],
        out_specs=[tile(D), pl.BlockSpec((1, tm * ROW_TILE, LANES), lambda b, t: (b, t, 0)),
                   picks, picks, tile(GATE_LANES), const((N_EXPERTS, GATE_LANES))],
        out_shape=[jax.ShapeDtypeStruct((B, T, D), F32), jax.ShapeDtypeStruct((B, T * ROW_TILE, LANES), F32),
                   jax.ShapeDtypeStruct((TOP_K, B * T), jnp.int32), jax.ShapeDtypeStruct((TOP_K, B * T), jnp.int32),
                   jax.ShapeDtypeStruct((B, T, GATE_LANES), F32),
                   jax.ShapeDtypeStruct((N_EXPERTS, GATE_LANES), F32)],
        compiler_params=pltpu.CompilerParams(dimension_semantics=("arbitrary", "arbitrary"),
                                             vmem_limit_bytes=VMEM_LIMIT_BYTES),
        name="outproj",
    )(oa, ob, x, wa, wb, g1, n2, sc2, sh2, wr_t, rb)


def _dispatch_kernel(slot_ref, zstart_ref, zon_ref, h_ref, xs_ref, zbuf, sem, zsem):
    tmd = h_ref.shape[0] // ROW_TILE
    zrows = zbuf.shape[0]

    @pl.when(pl.program_id(0) == 0)
    def _():
        zbuf[...] = jnp.zeros(zbuf.shape, F32)

        def fill(e):
            return pltpu.make_async_copy(zbuf, xs_ref.at[pl.ds(pl.multiple_of(zstart_ref[e], ROW_TILE), zrows)], zsem)

        def start(e, c):
            @pl.when(zon_ref[e] > 0)
            def _():
                fill(e).start()
            return c

        def wait(e, c):
            @pl.when(zon_ref[e] > 0)
            def _():
                fill(e).wait()
            return c

        lax.fori_loop(0, N_EXPERTS, start, 0)
        lax.fori_loop(0, N_EXPERTS, wait, 0)

    def row_copy(j, k):
        dst = pl.ds(pl.multiple_of(slot_ref[k, j], ROW_TILE), ROW_TILE)
        return pltpu.make_async_copy(h_ref.at[pl.ds(pl.multiple_of(j * ROW_TILE, ROW_TILE), ROW_TILE)],
                                     xs_ref.at[dst], sem)

    def start_rows(j, c):
        for k in range(TOP_K):
            row_copy(j, k).start()
        return c

    def wait_rows(j, c):
        for k in range(TOP_K):
            row_copy(j, k).wait()
        return c

    lax.fori_loop(0, tmd, start_rows, 0)
    lax.fori_loop(0, tmd, wait_rows, 0)


def _dispatch(slot, zstart, zon, h2, n_rows, tmd):
    M = h2.shape[0] // ROW_TILE
    smem = lambda shape, imap: pl.BlockSpec(shape, imap, memory_space=pltpu.SMEM)
    return pl.pallas_call(
        _dispatch_kernel,
        grid=(M // tmd,),
        in_specs=[smem((TOP_K, tmd), lambda i: (0, i)),
                  smem((N_EXPERTS,), lambda i: (0,)), smem((N_EXPERTS,), lambda i: (0,)),
                  pl.BlockSpec((tmd * ROW_TILE, LANES), lambda i: (i, 0))],
        out_specs=pl.BlockSpec(memory_space=pl.ANY),
        out_shape=jax.ShapeDtypeStruct((n_rows * ROW_TILE, LANES), F32),
        scratch_shapes=[pltpu.VMEM((MOE_ROWS * ROW_TILE, LANES), F32), pltpu.SemaphoreType.DMA,
                        pltpu.SemaphoreType.DMA],
        compiler_params=pltpu.CompilerParams(dimension_semantics=("arbitrary",),
                                             vmem_limit_bytes=VMEM_LIMIT_BYTES),
        name="dispatch",
    )(slot, zstart, zon, h2)


def _ffn_kernel(te_ref, nu_ref, x_ref, wg_ref, wu_ref, wd_ref, y_ref, wgu_s, wd_s):
    i = pl.program_id(0)

    @pl.when(i < nu_ref[0])
    def _():
        @pl.when((i == 0) | (te_ref[i] != te_ref[jnp.maximum(i - 1, 0)]))
        def _():
            wgu_s[:, :D_EXPERT] = wg_ref[0].astype(BF16)
            wgu_s[:, D_EXPERT:] = wu_ref[0].astype(BF16)
            wd_s[...] = wd_ref[0].astype(BF16)

        x = _load_row_tiled(x_ref, MOE_ROWS)
        gu = jnp.dot(x.astype(BF16), wgu_s[...], preferred_element_type=F32)
        act = _silu(gu[:, :D_EXPERT]) * gu[:, D_EXPERT:]
        _store_row_tiled(y_ref, jnp.dot(act.astype(BF16), wd_s[...], preferred_element_type=F32))


def _ffn(tile_expert, n_used, xs, wg, wu, wd):
    D = wg.shape[1]
    n_tiles = xs.shape[0] // (MOE_ROWS * ROW_TILE)
    rows = pl.BlockSpec((MOE_ROWS * ROW_TILE, LANES), lambda i, te, nu: (jnp.minimum(i, nu[0] - 1), 0))
    return pl.pallas_call(
        _ffn_kernel,
        grid_spec=pltpu.PrefetchScalarGridSpec(
            num_scalar_prefetch=2,
            grid=(n_tiles,),
            in_specs=[rows,
                      pl.BlockSpec((1, D, D_EXPERT), lambda i, te, nu: (te[i], 0, 0)),
                      pl.BlockSpec((1, D, D_EXPERT), lambda i, te, nu: (te[i], 0, 0)),
                      pl.BlockSpec((1, D_EXPERT, D), lambda i, te, nu: (te[i], 0, 0))],
            out_specs=rows,
            scratch_shapes=[pltpu.VMEM((D, 2 * D_EXPERT), BF16), pltpu.VMEM((D_EXPERT, D), BF16)]),
        out_shape=jax.ShapeDtypeStruct(xs.shape, F32),
        compiler_params=pltpu.CompilerParams(dimension_semantics=("arbitrary",),
                                             vmem_limit_bytes=VMEM_LIMIT_BYTES),
        name="ffn",
    )(tile_expert, n_used, xs, wg, wu, wd)


def _combine_kernel(slot_ref, w_ref, h_ref, x1_ref, g2_ref, fw_ref, wsgu_ref, wsd_ref, y_ref, o_ref, buf, sem):
    tmc = h_ref.shape[0] // ROW_TILE

    def row_copy(j, k):
        src = pl.ds(pl.multiple_of(slot_ref[k, j], ROW_TILE), ROW_TILE)
        return pltpu.make_async_copy(y_ref.at[src],
                                     buf.at[k, pl.ds(pl.multiple_of(j * ROW_TILE, ROW_TILE), ROW_TILE)], sem)

    def start_rows(j, c):
        for k in range(TOP_K):
            row_copy(j, k).start()
        return c

    def wait_rows(j, c):
        for k in range(TOP_K):
            row_copy(j, k).wait()
        return c

    lax.fori_loop(0, tmc, start_rows, 0)
    gu = jnp.dot(_load_row_tiled(h_ref, tmc).astype(BF16), wsgu_ref[...], preferred_element_type=F32)
    act = _silu(gu[:, :D_EXPERT]) * gu[:, D_EXPERT:]
    acc = jnp.dot(act.astype(BF16), wsd_ref[...], preferred_element_type=F32)
    lax.fori_loop(0, tmc, wait_rows, 0)
    w = w_ref[...]
    lane = lax.broadcasted_iota(jnp.int32, w.shape, 1)
    for k in range(TOP_K):
        acc = acc + _lane_pick(w, lane, k) * _load_row_tiled(buf, tmc, lead=(k,))
    y = x1_ref[...] + g2_ref[0] * acc
    o_ref[...] = _rms(y) * fw_ref[...]


def _combine(slot, wts, h2, x1, g2, fw, wsgu, wsd, ys, T, tmc):
    M, D = x1.shape
    const = lambda shape: pl.BlockSpec(shape, lambda i: (0,) * len(shape))
    tile = lambda w: pl.BlockSpec((tmc, w), lambda i: (i, 0))
    row_tiled = pl.BlockSpec((tmc * ROW_TILE, LANES), lambda i: (i, 0))
    return pl.pallas_call(
        _combine_kernel,
        grid=(M // tmc,),
        in_specs=[pl.BlockSpec((TOP_K, tmc), lambda i: (0, i), memory_space=pltpu.SMEM),
                  tile(GATE_LANES), row_tiled, tile(D),
                  pl.BlockSpec((1, 1, D), lambda i: (i // (T // tmc), 0, 0)),
                  const((1, D)), const(wsgu.shape), const(wsd.shape),
                  pl.BlockSpec(memory_space=pl.ANY)],
        out_specs=tile(D),
        out_shape=jax.ShapeDtypeStruct((M, D), F32),
        scratch_shapes=[pltpu.VMEM((TOP_K, tmc * ROW_TILE, LANES), F32), pltpu.SemaphoreType.DMA],
        compiler_params=pltpu.CompilerParams(dimension_semantics=("arbitrary",),
                                             vmem_limit_bytes=VMEM_LIMIT_BYTES),
        name="combine",
    )(slot, wts, h2, x1, g2, fw, wsgu, wsd, ys)


def _pick_tile(n, want):
    t = min(n, want)
    assert n % t == 0 and t % CHUNK == 0, (n, want)
    return t


def kernel(x, c, w_ada, b_ada, norm1_w, w_in, conv_w, gdn_a_log, gdn_dt_bias, gdn_norm_w, hg_lb, hg_norm_w,
           w_out, norm2_w, w_router, router_bias, w_gate, w_up, w_down, ws_gate, ws_up, ws_down, final_norm_w):
    B, T, D = x.shape
    M = B * T
    depth = w_ada.shape[0]
    assert depth == 1 and T % CHUNK == 0 and B <= 8
    layer = 0
    tt = _pick_tile(T, 512)

    c_pad = jnp.pad(c, ((0, 8 - B), (0, 0)))
    mod = _ada(c_pad, w_ada[layer], b_ada[layer].reshape(1, -1))[:B]
    sh1, sc1, g1, sh2, sc2, g2 = (m.reshape(B, 1, D) for m in jnp.split(mod, 6, axis=-1))

    w = w_in[layer]
    qkv_w = 3 * GW
    sizes = (GW, HEADS, HEADS, GW, GW, GW, GW)
    offs = [qkv_w]
    for s in sizes:
        offs.append(offs[-1] + s)
    seg = lambda i: w[:, offs[i]:offs[i + 1]]
    small = jnp.pad(jnp.concatenate([seg(1), seg(2)], axis=1), ((0, 0), (0, GATE_LANES - 2 * HEADS)))
    w_all = jnp.concatenate([w[:, :qkv_w], seg(0), seg(3), seg(4), seg(5), seg(6), small], axis=1).astype(BF16)
    lane_pad = lambda v: jnp.pad(v.astype(F32).reshape(1, HEADS), ((0, 0), (HEADS, GATE_LANES - 2 * HEADS)))
    idx = jnp.arange(tt)
    tri = ((idx[:, None] >= idx[None, :]) & (idx[:, None] // CHUNK == idx[None, :] // CHUNK)).astype(BF16)

    qa, ka, va, ga, sm, bcum, kb, ib, qb, gb = _inproj(
        layer, x, norm1_w[layer].reshape(1, D), sc1, sh1, w_all, conv_w[layer].astype(F32),
        lane_pad(gdn_a_log[layer]), lane_pad(gdn_dt_bias[layer]), hg_lb.astype(F32), tri, tt)

    gct = sm[:, :, HEADS:2 * HEADS].transpose(0, 2, 1).reshape(B, HEADS, T // CHUNK, CHUNK)
    oa = _gdn(qa, ka, va, ga, sm, gct, gdn_norm_w[layer].reshape(1, DH), tt)
    ob = _hgrn(qb, kb, ib, bcum, gb, hg_norm_w[layer].reshape(1, GW), tt)

    wo = w_out[layer].astype(BF16)
    x1, h2, eidx, rank, wts, cnt = _outproj(oa, ob, x, wo[:GW], wo[GW:], g1, norm2_w[layer].reshape(1, D), sc2, sh2,
                                            w_router[layer].T, router_bias[layer].reshape(N_EXPERTS, 1), tt)

    counts = cnt[:, 0].astype(jnp.int32)
    padded = (counts + MOE_ROWS - 1) // MOE_ROWS * MOE_ROWS
    ends = jnp.cumsum(padded)
    offsets = ends - padded
    n_tiles = (M * TOP_K) // MOE_ROWS + N_EXPERTS
    n_used = (ends[-1] // MOE_ROWS).astype(jnp.int32)
    tile_ids = jnp.minimum(jnp.arange(n_tiles, dtype=jnp.int32), n_used - 1)
    tile_expert = jnp.sum(((ends // MOE_ROWS)[None, :] <= tile_ids[:, None]).astype(jnp.int32), axis=1)
    tile_expert = jnp.minimum(tile_expert, N_EXPERTS - 1)
    slot = (jnp.take(offsets, eidx) + rank) * ROW_TILE
    zon = (padded > 0).astype(jnp.int32)
    zstart = (jnp.maximum(ends - MOE_ROWS, 0) * ROW_TILE).astype(jnp.int32)

    assert D == ROW_TILE * LANES
    h2f = h2.reshape(M * ROW_TILE, LANES)
    xs = _dispatch(slot, zstart, zon, h2f, n_tiles * MOE_ROWS, _pick_tile(M, 512))
    ys = _ffn(tile_expert.astype(jnp.int32), n_used.reshape(1), xs, w_gate[layer], w_up[layer], w_down[layer])
    wsgu = jnp.concatenate([ws_gate[layer], ws_up[layer]], axis=-1).astype(BF16)
    out = _combine(slot, wts.reshape(M, GATE_LANES), h2f, x1.reshape(M, D), g2, final_norm_w.reshape(1, D),
                   wsgu, ws_down[layer].astype(BF16), ys, T, _pick_tile(T, 128))
    return out.reshape(B, T, D)
```

```python
import functools

import jax
import jax.numpy as jnp
from jax import lax
from jax.experimental import pallas as pl
from jax.experimental.pallas import tpu as pltpu

F32 = jnp.float32
BF16 = jnp.bfloat16

EPS = 1e-6
CHUNK = 64
SUB = 8
HEADS = 4
DH = 128
GW = HEADS * DH
CONV_K = 4
N_EXPERTS = 64
N_GROUPS = 8
GROUP_SIZE = N_EXPERTS // N_GROUPS
TOPK_GROUPS = 4
TOP_K = 8
D_EXPERT = 256
ROUTE_SCALE = 2.5
GATE_LANES = 128
GDN_CHUNKS_PER_ITER = 2
MOE_ROWS = 256
ROW_TILE, LANES = 8, 128

VMEM_LIMIT_BYTES = 56 * 1024 * 1024

ACT = BF16


def _silu(x):
    return x * jax.nn.sigmoid(x)


def _dot(a, b):
    return jnp.dot(a.astype(BF16), b.astype(BF16), preferred_element_type=F32)


def _dot_nt(a, b):
    return lax.dot_general(a.astype(BF16), b.astype(BF16), (((1,), (1,)), ((), ())),
                           preferred_element_type=F32)


def _dot_tn(a, b):
    return lax.dot_general(a.astype(BF16), b.astype(BF16), (((0,), (0,)), ((), ())),
                           preferred_element_type=F32)


def _split2(x):
    hi = x.astype(BF16)
    lo = (x - hi.astype(F32)).astype(BF16)
    return hi, lo


def _dot3(a, b, dot=_dot):
    ah, al = _split2(a)
    bh, bl = _split2(b)
    return dot(ah, bh) + dot(ah, bl) + dot(al, bh)


def _cumsum_rows(tri, x):
    hi = x.astype(BF16)
    r = x - hi.astype(F32)
    mid = r.astype(BF16)
    lo = (r - mid.astype(F32)).astype(BF16)
    return (jnp.dot(tri, hi, preferred_element_type=F32)
            + jnp.dot(tri, mid, preferred_element_type=F32)
            + jnp.dot(tri, lo, preferred_element_type=F32))


def _lane_pick(tile, lane, idx):
    return jnp.sum(jnp.where(lane == idx, tile, 0.0), axis=1, keepdims=True)


def _rms(x):
    return x * lax.rsqrt(jnp.mean(x * x, axis=-1, keepdims=True) + EPS)


def _load_row_tiled(ref, n_rows, lead=()):
    return jnp.concatenate([ref[lead + (pl.ds(s, n_rows, stride=ROW_TILE), slice(None))] for s in range(ROW_TILE)],
                           axis=1)


def _store_row_tiled(ref, x, lead=()):
    n_rows = x.shape[0]
    for s in range(ROW_TILE):
        ref[lead + (pl.ds(s, n_rows, stride=ROW_TILE), slice(None))] = x[:, s * LANES:(s + 1) * LANES]


def _ada_kernel(c_ref, w_ref, b_ref, o_ref):
    ca = _silu(c_ref[...])
    o_ref[...] = _dot3(ca, w_ref[...]) + b_ref[...]


def _ada(c_pad, w, b):
    rows, d = c_pad.shape
    n = w.shape[1]
    tn = 1024
    return pl.pallas_call(
        _ada_kernel,
        grid=(n // tn,),
        in_specs=[pl.BlockSpec((rows, d), lambda j: (0, 0)),
                  pl.BlockSpec((d, tn), lambda j: (0, j)),
                  pl.BlockSpec((1, tn), lambda j: (0, j))],
        out_specs=pl.BlockSpec((rows, tn), lambda j: (0, j)),
        out_shape=jax.ShapeDtypeStruct((rows, n), F32),
        compiler_params=pltpu.CompilerParams(dimension_semantics=("arbitrary",),
                                             vmem_limit_bytes=VMEM_LIMIT_BYTES),
        name="ada",
    )(c_pad, w, b)


def _inproj_kernel(layer, x_ref, n1_ref, sc_ref, sh_ref, w_ref, cw_ref, alog_ref, dt_ref, lb_ref, tri_ref,
                   qa_ref, ka_ref, va_ref, ga_ref, sm_ref, b_ref, kb_ref, ib_ref, qb_ref, gb_ref,
                   pbuf):
    tt = x_ref.shape[1]
    t = pl.program_id(1)

    h = _rms(x_ref[0]) * n1_ref[...]
    h = h * (1.0 + sc_ref[0]) + sh_ref[0]
    hb = h.astype(BF16)

    def proj(g, width=GW):
        return jnp.dot(hb, w_ref[:, g * GW:g * GW + width], preferred_element_type=F32)

    @pl.when(t == 0)
    def _():
        pbuf[:, 0:8, :] = jnp.zeros((3, 8, GW), F32)

    for g, out_ref in enumerate((qa_ref, ka_ref, va_ref)):
        cols = slice(g * GW, (g + 1) * GW)
        p = proj(g)
        pbuf[g, 8:8 + tt, :] = p
        y = p * cw_ref[CONV_K - 1:CONV_K, cols]
        for j in range(1, CONV_K):
            y = y + pbuf[g, 8 - j:8 - j + tt, :] * cw_ref[CONV_K - 1 - j:CONV_K - j, cols]
        pbuf[g, 0:8, :] = pbuf[g, tt:tt + 8, :]
        y = _silu(y)
        if g == 2:
            out_ref[0] = y.astype(out_ref.dtype)
        else:
            scale = DH ** -0.5 if g == 0 else 1.0
            for hh in range(HEADS):
                hs = slice(hh * DH, (hh + 1) * DH)
                yh = y[:, hs]
                inv = lax.rsqrt(jnp.sum(yh * yh, axis=-1, keepdims=True) + EPS)
                out_ref[0, :, hs] = (yh * inv * scale).astype(out_ref.dtype)

    ga_ref[0] = _silu(proj(3)).astype(ga_ref.dtype)

    ps = proj(8, GATE_LANES)
    lane = lax.broadcasted_iota(jnp.int32, ps.shape, 1)
    beta = jax.nn.sigmoid(ps)
    z = ps + dt_ref[...]
    softplus = jnp.maximum(z, 0.0) + jnp.log1p(jnp.exp(-jnp.abs(z)))
    g_log = -jnp.exp(alog_ref[...]) * softplus
    tri = tri_ref[...]
    gc = _cumsum_rows(tri, jnp.where((lane >= HEADS) & (lane < 2 * HEADS), g_log, 0.0))
    sm_ref[0] = jnp.where(lane < HEADS, beta, gc)

    hl = lb_ref[...]
    e = jnp.exp(hl - jnp.max(hl, axis=0, keepdims=True))
    lb = jnp.sum(e[0:layer + 1], axis=0, keepdims=True) / jnp.sum(e, axis=0, keepdims=True)
    fr = proj(4)
    logf = jnp.log(lb + (1.0 - lb) * jax.nn.sigmoid(fr))
    b_ref[0] = _cumsum_rows(tri, logf)
    kb_ref[0] = ((1.0 - lb) * jax.nn.sigmoid(-fr)).astype(kb_ref.dtype)
    ib_ref[0] = proj(5).astype(ib_ref.dtype)
    qb_ref[0] = _silu(proj(6)).astype(qb_ref.dtype)
    gb_ref[0] = _silu(proj(7)).astype(gb_ref.dtype)


def _inproj(layer, x, n1, sc1, sh1, w_all, conv_w, alog_pad, dt_pad, hg_lb, tri, tt):
    B, T, D = x.shape
    const = lambda shape: pl.BlockSpec(shape, lambda b, t: (0,) * len(shape))
    act = lambda dt: jax.ShapeDtypeStruct((B, T, GW), dt)
    tile = lambda w: pl.BlockSpec((1, tt, w), lambda b, t: (b, t, 0))
    per_batch = pl.BlockSpec((1, 1, D), lambda b, t: (b, 0, 0))
    return pl.pallas_call(
        functools.partial(_inproj_kernel, layer),
        grid=(B, T // tt),
        in_specs=[tile(D), const((1, D)), per_batch, per_batch,
                  const(w_all.shape), const(conv_w.shape), const((1, GATE_LANES)), const((1, GATE_LANES)),
                  const(hg_lb.shape), const((tt, tt))],
        out_specs=[tile(GW), tile(GW), tile(GW), tile(GW), tile(GATE_LANES), tile(GW),
                   tile(GW), tile(GW), tile(GW), tile(GW)],
        out_shape=[act(ACT), act(ACT), act(ACT), act(ACT),
                   jax.ShapeDtypeStruct((B, T, GATE_LANES), F32), act(F32),
                   act(ACT), act(ACT), act(ACT), act(ACT)],
        scratch_shapes=[pltpu.VMEM((3, tt + 8, GW), F32)],
        compiler_params=pltpu.CompilerParams(dimension_semantics=("arbitrary", "arbitrary"),
                                             vmem_limit_bytes=VMEM_LIMIT_BYTES),
        name="inproj",
    )(x, n1, sc1, sh1, w_all, conv_w, alog_pad, dt_pad, hg_lb, tri)


def _gdn_prep_kernel(q_ref, k_ref, v_ref, sm_ref, gct_ref, o_ref, qt_ref, m_ref, n_ref):
    tt = q_ref.shape[1]
    nc = tt // CHUNK
    row = lax.broadcasted_iota(jnp.int32, (CHUNK, CHUNK), 0)
    col = lax.broadcasted_iota(jnp.int32, (CHUNK, CHUNK), 1)
    causal = row >= col
    strict = row > col
    eye = jnp.where(row == col, 1.0, 0.0)
    lane = lax.broadcasted_iota(jnp.int32, (CHUNK, GATE_LANES), 1)

    def body(i, carry):
        chains = [(GDN_CHUNKS_PER_ITER * i + j, hh) for j in range(GDN_CHUNKS_PER_ITER) for hh in range(HEADS)]
        rows = [pl.ds(pl.multiple_of(c * CHUNK, CHUNK), CHUNK) for c, _ in chains]
        hs = [slice(hh * DH, (hh + 1) * DH) for _, hh in chains]
        n = range(len(chains))
        sm = [sm_ref[0, rows[j], :] for j in n]
        q = [q_ref[0, rows[j], hs[j]].astype(F32) for j in n]
        k = [k_ref[0, rows[j], hs[j]].astype(F32) for j in n]
        v = [v_ref[0, rows[j], hs[j]].astype(F32) for j in n]
        beta = [_lane_pick(sm[j], lane, chains[j][1]) for j in n]
        gcol = [_lane_pick(sm[j], lane, HEADS + chains[j][1]) for j in n]
        grow = [gct_ref[0, hh, pl.ds(c, 1), :] for c, hh in chains]
        decay = [jnp.exp(jnp.where(causal, gcol[j] - grow[j], -jnp.inf)) for j in n]
        kb = [k[j] * beta[j] for j in n]
        L = [jnp.where(strict, _dot_nt(kb[j], k[j]) * decay[j], 0.0) for j in n]
        tinv = [eye - L[j] for j in n]
        pw = [_dot3(L[j], L[j]) for j in n]
        for _ in range(4):
            tinv = [tinv[j] + _dot3(tinv[j], pw[j]) for j in n]
            pw = [_dot3(pw[j], pw[j]) for j in n]
        tinv = [tinv[j] + _dot3(tinv[j], pw[j]) for j in n]
        eg = [jnp.exp(gcol[j]) for j in n]
        sol = [_dot(tinv[j], jnp.concatenate([v[j] * beta[j], kb[j] * eg[j]], axis=1)) for j in n]
        attn = [_dot_nt(q[j], k[j]) * decay[j] for j in n]
        k_tail = [k[j] * jnp.exp(gcol[j][CHUNK - 1:CHUNK, :] - gcol[j]) for j in n]
        au = [_dot(attn[j], sol[j]) for j in n]
        ku = [_dot_tn(k_tail[j], sol[j]) for j in n]
        for j, (c, hh) in enumerate(chains):
            o_ref[0, rows[j], hs[j]] = au[j][:, :DH]
            qt_ref[0, rows[j], hs[j]] = (q[j] * eg[j] - au[j][:, DH:]).astype(qt_ref.dtype)
            n_ref[0, hh, c] = ku[j][:, :DH].astype(n_ref.dtype)
            m_ref[0, hh, c] = (-ku[j][:, DH:]).astype(m_ref.dtype)
        return carry

    lax.fori_loop(0, nc // GDN_CHUNKS_PER_ITER, body, 0)


def _gdn_scan_kernel(o_ref, qt_ref, m_ref, n_ref, gct_ref, sg_ref, nw_ref, out_ref, s_ref):
    nb, tt = o_ref.shape[0], o_ref.shape[1]
    nc = tt // CHUNK

    @pl.when(pl.program_id(0) == 0)
    def _():
        s_ref[...] = jnp.zeros(s_ref.shape, F32)

    nw = nw_ref[...]

    def body(c, carry):
        rows = pl.ds(pl.multiple_of(c * CHUNK, CHUNK), CHUNK)
        for b in range(nb):
            for hh in range(HEADS):
                hs = slice(hh * DH, (hh + 1) * DH)
                S = s_ref[b, hh]
                Sb = S.astype(BF16)
                glast = gct_ref[b, hh, pl.ds(c, 1), :][:, CHUNK - 1:CHUNK]
                o = o_ref[b, rows, hs] + jnp.dot(qt_ref[b, rows, hs], Sb, preferred_element_type=F32)
                s_ref[b, hh] = (S * jnp.exp(glast) + jnp.dot(m_ref[b, hh, c], Sb, preferred_element_type=F32)
                                + n_ref[b, hh, c].astype(F32))
                o = _rms(o) * nw * sg_ref[b, rows, hs].astype(F32)
                out_ref[b, rows, hs] = o.astype(out_ref.dtype)
        return carry

    lax.fori_loop(0, nc, body, 0)


def _gdn(q, k, v, sg, sm, gct, nw, tt):
    B, T, _ = q.shape
    nc = tt // CHUNK
    assert nc % GDN_CHUNKS_PER_ITER == 0
    n_chunks = T // CHUNK
    tile = lambda w: pl.BlockSpec((1, tt, w), lambda b, t: (b, t, 0))
    mat = jax.ShapeDtypeStruct((B, HEADS, n_chunks, DH, DH), ACT)
    o_part, qt, m, n = pl.pallas_call(
        _gdn_prep_kernel,
        grid=(B, T // tt),
        in_specs=[tile(GW), tile(GW), tile(GW), tile(GATE_LANES),
                  pl.BlockSpec((1, HEADS, nc, CHUNK), lambda b, t: (b, 0, t, 0))],
        out_specs=[tile(GW), tile(GW),
                   pl.BlockSpec((1, HEADS, nc, DH, DH), lambda b, t: (b, 0, t, 0, 0)),
                   pl.BlockSpec((1, HEADS, nc, DH, DH), lambda b, t: (b, 0, t, 0, 0))],
        out_shape=[jax.ShapeDtypeStruct((B, T, GW), F32), jax.ShapeDtypeStruct((B, T, GW), ACT), mat, mat],
        compiler_params=pltpu.CompilerParams(dimension_semantics=("arbitrary", "arbitrary"),
                                             vmem_limit_bytes=VMEM_LIMIT_BYTES),
        name="gdn_prep",
    )(q, k, v, sm, gct)

    full = lambda w: pl.BlockSpec((B, tt, w), lambda t: (0, t, 0))
    mats = pl.BlockSpec((B, HEADS, nc, DH, DH), lambda t: (0, 0, t, 0, 0))
    return pl.pallas_call(
        _gdn_scan_kernel,
        grid=(T // tt,),
        in_specs=[full(GW), full(GW), mats, mats,
                  pl.BlockSpec((B, HEADS, nc, CHUNK), lambda t: (0, 0, t, 0)),
                  full(GW), pl.BlockSpec((1, DH), lambda t: (0, 0))],
        out_specs=full(GW),
        out_shape=jax.ShapeDtypeStruct((B, T, GW), ACT),
        scratch_shapes=[pltpu.VMEM((B, HEADS, DH, DH), F32)],
        compiler_params=pltpu.CompilerParams(dimension_semantics=("arbitrary",),
                                             vmem_limit_bytes=VMEM_LIMIT_BYTES),
        name="gdn_scan",
    )(o_part, qt, m, n, gct, sg, nw)


def _hgrn_kernel(q_ref, k_ref, v_ref, b_ref, sg_ref, nw_ref, o_ref, st_ref):
    tt = q_ref.shape[1]
    nc = tt // CHUNK

    @pl.when(pl.program_id(1) == 0)
    def _():
        st_ref[...] = jnp.zeros(st_ref.shape, F32)

    row = lax.broadcasted_iota(jnp.int32, (CHUNK, CHUNK), 0)
    col = lax.broadcasted_iota(jnp.int32, (CHUNK, CHUNK), 1)
    diag_block = ((col // SUB) == (row // SUB)) & (col <= row)
    heads = range(HEADS)
    hs = [slice(hh * DH, (hh + 1) * DH) for hh in heads]

    def body(c, carry):
        rows = pl.ds(pl.multiple_of(c * CHUNK, CHUNK), CHUNK)
        q = [q_ref[0, rows, hs[h]].astype(F32) for h in heads]
        k = [k_ref[0, rows, hs[h]].astype(F32) for h in heads]
        v = [v_ref[0, rows, hs[h]].astype(F32) for h in heads]
        b = [b_ref[0, rows, hs[h]] for h in heads]
        blast = [b[h][CHUNK - 1:CHUNK, :] for h in heads]
        st = [st_ref[h] for h in heads]
        o = [_dot_nt(q[h] * jnp.exp(b[h]), st[h]) for h in heads]
        k_tail = [k[h] * jnp.exp(blast[h] - b[h]) for h in heads]
        for h in heads:
            st_ref[h] = st[h] * jnp.exp(blast[h]) + _dot_tn(v[h], k_tail[h])

        blocks = [[jnp.zeros((SUB, CHUNK), F32)] for _ in heads]
        for i in range(1, CHUNK // SUB):
            lo, hi = i * SUB, (i + 1) * SUB
            for h in heads:
                r = b[h][lo:lo + 1, :]
                qi = q[h][lo:hi] * jnp.exp(b[h][lo:hi] - r)
                kj = k[h][:lo] * jnp.exp(jnp.minimum(r - b[h][:lo], 0.0))
                kj = jnp.concatenate([kj, jnp.zeros((CHUNK - lo, DH), F32)], axis=0)
                blocks[h].append(_dot_nt(qi, kj))
        a = []
        for h in heads:
            f = jnp.exp(jnp.minimum(b[h] - pltpu.roll(b[h], 1, 0), 0.0))
            e = None
            a_diag = jnp.zeros((CHUNK, CHUNK), F32)
            for delta in range(SUB):
                if delta == 0:
                    term = q[h] * k[h]
                else:
                    fsh = f if delta == 1 else pltpu.roll(f, delta - 1, 0)
                    e = fsh if e is None else e * fsh
                    term = q[h] * pltpu.roll(k[h], delta, 0) * e
                colv = jnp.sum(term, axis=1, keepdims=True)
                a_diag = jnp.where(row - col == delta, colv, a_diag)
            a.append(jnp.where(diag_block, a_diag, jnp.concatenate(blocks[h], axis=0)))

        o = [o[h] + _dot(a[h], v[h]) for h in heads]
        o = jnp.concatenate(o, axis=1)
        o = _rms(o) * nw_ref[...] * sg_ref[0, rows, :].astype(F32)
        o_ref[0, rows, :] = o.astype(o_ref.dtype)
        return carry

    lax.fori_loop(0, nc, body, 0)


def _hgrn(q, k, v, b, sg, nw, tt):
    B, T, _ = q.shape
    tile = pl.BlockSpec((1, tt, GW), lambda bi, t: (bi, t, 0))
    return pl.pallas_call(
        _hgrn_kernel,
        grid=(B, T // tt),
        in_specs=[tile, tile, tile, tile, tile, pl.BlockSpec((1, GW), lambda bi, t: (0, 0))],
        out_specs=tile,
        out_shape=jax.ShapeDtypeStruct((B, T, GW), ACT),
        scratch_shapes=[pltpu.VMEM((HEADS, DH, DH), F32)],
        compiler_params=pltpu.CompilerParams(dimension_semantics=("arbitrary", "arbitrary"),
                                             vmem_limit_bytes=VMEM_LIMIT_BYTES),
        name="hgrn",
    )(q, k, v, b, sg, nw)


def _outproj_kernel(oa_ref, ob_ref, x_ref, wa_ref, wb_ref, g1_ref, n2_ref, sc_ref, sh_ref, wr_ref, rb_ref,
                    x1_ref, h2_ref, eidx_ref, rank_ref, wts_ref, cnt_ref):
    tm = x_ref.shape[1]

    @pl.when((pl.program_id(0) == 0) & (pl.program_id(1) == 0))
    def _():
        cnt_ref[...] = jnp.zeros(cnt_ref.shape, F32)

    mix = (jnp.dot(oa_ref[0], wa_ref[...], preferred_element_type=F32)
           + jnp.dot(ob_ref[0], wb_ref[...], preferred_element_type=F32))
    x1 = x_ref[0] + g1_ref[0] * mix
    x1_ref[0] = x1
    h2 = _rms(x1) * n2_ref[...]
    h2 = h2 * (1.0 + sc_ref[0]) + sh_ref[0]
    _store_row_tiled(h2_ref, h2, lead=(0,))

    scores = jax.nn.sigmoid(_dot3(wr_ref[...], h2, dot=_dot_nt))
    sel = scores + rb_ref[...]
    sub = lax.broadcasted_iota(jnp.int32, (GROUP_SIZE, tm), 0)
    neg = -jnp.inf
    groups = range(N_GROUPS)

    def take_max(blk):
        m = jnp.max(blk, axis=0, keepdims=True)
        first = jnp.min(jnp.where(blk == m, sub, GROUP_SIZE), axis=0, keepdims=True)
        hit = sub == first
        return m, hit, jnp.where(hit, neg, blk)

    blk_of = lambda a, g: a[g * GROUP_SIZE:(g + 1) * GROUP_SIZE]
    sel_blk = [blk_of(sel, g) for g in groups]
    group_score = jnp.zeros((N_GROUPS, tm), F32)
    for g in groups:
        m1, _, rest = take_max(sel_blk[g])
        m2 = jnp.max(rest, axis=0, keepdims=True)
        group_score = jnp.where(sub == g, m1 + m2, group_score)
    group_on = jnp.zeros((N_GROUPS, tm), F32)
    for _ in range(TOPK_GROUPS):
        _, hit, group_score = take_max(group_score)
        group_on = jnp.where(hit, 1.0, group_on)

    cand = [jnp.where(group_on[g:g + 1] > 0.0, sel_blk[g], neg) for g in groups]
    picked = [jnp.zeros((GROUP_SIZE, tm), F32) for _ in groups]
    chosen = []
    for _ in range(TOP_K):
        m = jnp.max(functools.reduce(jnp.maximum, cand), axis=0, keepdims=True)
        first = functools.reduce(jnp.minimum, [jnp.where(cand[g] == m, sub + g * GROUP_SIZE, N_EXPERTS)
                                               for g in groups])
        first = jnp.min(first, axis=0, keepdims=True)
        chosen.append(first)
        for g in groups:
            hit = (sub + g * GROUP_SIZE) == first
            picked[g] = jnp.where(hit, 1.0, picked[g])
            cand[g] = jnp.where(hit, neg, cand[g])

    picked_all = jnp.concatenate(picked, axis=0)
    r_i = lax.broadcasted_iota(jnp.int32, (tm, tm), 0)
    c_i = lax.broadcasted_iota(jnp.int32, (tm, tm), 1)
    earlier = jnp.where(r_i < c_i, 1.0, 0.0).astype(BF16)
    before = jnp.dot(picked_all.astype(BF16), earlier, preferred_element_type=F32) + cnt_ref[:, 0:1]
    cnt_ref[...] = cnt_ref[...] + jnp.sum(picked_all, axis=1, keepdims=True)

    def pick_value(table, first):
        parts = [jnp.where((sub + g * GROUP_SIZE) == first, blk_of(table, g), 0.0) for g in groups]
        return jnp.sum(functools.reduce(jnp.add, parts), axis=0, keepdims=True)

    w_k = [pick_value(scores, f) for f in chosen]
    denom = functools.reduce(jnp.add, w_k)
    eidx = jnp.zeros((TOP_K, tm), jnp.int32)
    rank = jnp.zeros((TOP_K, tm), jnp.int32)
    wts = jnp.zeros((TOP_K, tm), F32)
    for k in range(TOP_K):
        eidx = jnp.where(sub == k, chosen[k], eidx)
        rank = jnp.where(sub == k, pick_value(before, chosen[k]).astype(jnp.int32), rank)
        wts = jnp.where(sub == k, w_k[k] / denom * ROUTE_SCALE, wts)
    eidx_ref[...] = eidx
    rank_ref[...] = rank
    pad = jnp.zeros((GATE_LANES - TOP_K, tm), F32)
    wts_ref[0] = jnp.concatenate([wts, pad], axis=0).T


def _outproj(oa, ob, x, wa, wb, g1, n2, sc2, sh2, wr_t, rb, tm):
    B, T, D = x.shape
    nt = T // tm
    const = lambda shape: pl.BlockSpec(shape, lambda b, t: (0,) * len(shape))
    tile = lambda w: pl.BlockSpec((1, tm, w), lambda b, t: (b, t, 0))
    per_batch = pl.BlockSpec((1, 1, D), lambda b, t: (b, 0, 0))
    picks = pl.BlockSpec((TOP_K, tm), lambda b, t: (0, b * nt + t))
    return pl.pallas_call(
        _outproj_kernel,
        grid=(B, nt),
        in_specs=[tile(GW), tile(GW), tile(D), const(wa.shape), const(wb.shape), per_batch,
                  const((1, D)), per_batch, per_batch, const(wr_t.shape), const(rb.shape)],
        out_specs=[tile(D), pl.BlockSpec((1, tm * ROW_TILE, LANES), lambda b, t: (b, t, 0)),
                   picks, picks, tile(GATE_LANES), const((N_EXPERTS, GATE_LANES))],
        out_shape=[jax.ShapeDtypeStruct((B, T, D), F32), jax.ShapeDtypeStruct((B, T * ROW_TILE, LANES), F32),
                   jax.ShapeDtypeStruct((TOP_K, B * T), jnp.int32), jax.ShapeDtypeStruct((TOP_K, B * T), jnp.int32),
                   jax.ShapeDtypeStruct((B, T, GATE_LANES), F32),
                   jax.ShapeDtypeStruct((N_EXPERTS, GATE_LANES), F32)],
        compiler_params=pltpu.CompilerParams(dimension_semantics=("arbitrary", "arbitrary"),
                                             vmem_limit_bytes=VMEM_LIMIT_BYTES),
        name="outproj",
    )(oa, ob, x, wa, wb, g1, n2, sc2, sh2, wr_t, rb)


def _dispatch_kernel(slot_ref, zstart_ref, zon_ref, h_ref, xs_ref, zbuf, sem, zsem):
    tmd = h_ref.shape[0] // ROW_TILE
    zrows = zbuf.shape[0]

    @pl.when(pl.program_id(0) == 0)
    def _():
        zbuf[...] = jnp.zeros(zbuf.shape, F32)

        def fill(e):
            return pltpu.make_async_copy(zbuf, xs_ref.at[pl.ds(pl.multiple_of(zstart_ref[e], ROW_TILE), zrows)], zsem)

        def start(e, c):
            @pl.when(zon_ref[e] > 0)
            def _():
                fill(e).start()
            return c

        def wait(e, c):
            @pl.when(zon_ref[e] > 0)
            def _():
                fill(e).wait()
            return c

        lax.fori_loop(0, N_EXPERTS, start, 0)
        lax.fori_loop(0, N_EXPERTS, wait, 0)

    def row_copy(j, k):
        dst = pl.ds(pl.multiple_of(slot_ref[k, j], ROW_TILE), ROW_TILE)
        return pltpu.make_async_copy(h_ref.at[pl.ds(pl.multiple_of(j * ROW_TILE, ROW_TILE), ROW_TILE)],
                                     xs_ref.at[dst], sem)

    def start_rows(j, c):
        for k in range(TOP_K):
            row_copy(j, k).start(priority=k % 2)
        return c

    def wait_rows(j, c):
        for k in range(TOP_K):
            row_copy(j, k).wait()
        return c

    lax.fori_loop(0, tmd, start_rows, 0)
    lax.fori_loop(0, tmd, wait_rows, 0)


def _dispatch(slot, zstart, zon, h2, n_rows, tmd):
    M = h2.shape[0] // ROW_TILE
    smem = lambda shape, imap: pl.BlockSpec(shape, imap, memory_space=pltpu.SMEM)
    return pl.pallas_call(
        _dispatch_kernel,
        grid=(M // tmd,),
        in_specs=[smem((TOP_K, tmd), lambda i: (0, i)),
                  smem((N_EXPERTS,), lambda i: (0,)), smem((N_EXPERTS,), lambda i: (0,)),
                  pl.BlockSpec((tmd * ROW_TILE, LANES), lambda i: (i, 0))],
        out_specs=pl.BlockSpec(memory_space=pl.ANY),
        out_shape=jax.ShapeDtypeStruct((n_rows * ROW_TILE, LANES), F32),
        scratch_shapes=[pltpu.VMEM((MOE_ROWS * ROW_TILE, LANES), F32), pltpu.SemaphoreType.DMA,
                        pltpu.SemaphoreType.DMA],
        compiler_params=pltpu.CompilerParams(dimension_semantics=("arbitrary",),
                                             vmem_limit_bytes=VMEM_LIMIT_BYTES),
        name="dispatch",
    )(slot, zstart, zon, h2)


def _ffn_kernel(te_ref, nu_ref, x_ref, wg_ref, wu_ref, wd_ref, y_ref, wgu_s, wd_s):
    i = pl.program_id(0)

    @pl.when(i < nu_ref[0])
    def _():
        @pl.when((i == 0) | (te_ref[i] != te_ref[jnp.maximum(i - 1, 0)]))
        def _():
            wgu_s[:, :D_EXPERT] = wg_ref[0].astype(BF16)
            wgu_s[:, D_EXPERT:] = wu_ref[0].astype(BF16)
            wd_s[...] = wd_ref[0].astype(BF16)

        x = _load_row_tiled(x_ref, MOE_ROWS)
        gu = jnp.dot(x.astype(BF16), wgu_s[...], preferred_element_type=F32)
        act = _silu(gu[:, :D_EXPERT]) * gu[:, D_EXPERT:]
        _store_row_tiled(y_ref, jnp.dot(act.astype(BF16), wd_s[...], preferred_element_type=F32))


def _ffn(tile_expert, n_used, xs, wg, wu, wd):
    D = wg.shape[1]
    n_tiles = xs.shape[0] // (MOE_ROWS * ROW_TILE)
    rows = pl.BlockSpec((MOE_ROWS * ROW_TILE, LANES), lambda i, te, nu: (jnp.minimum(i, nu[0] - 1), 0))
    return pl.pallas_call(
        _ffn_kernel,
        grid_spec=pltpu.PrefetchScalarGridSpec(
            num_scalar_prefetch=2,
            grid=(n_tiles,),
            in_specs=[rows,
                      pl.BlockSpec((1, D, D_EXPERT), lambda i, te, nu: (te[i], 0, 0)),
                      pl.BlockSpec((1, D, D_EXPERT), lambda i, te, nu: (te[i], 0, 0)),
                      pl.BlockSpec((1, D_EXPERT, D), lambda i, te, nu: (te[i], 0, 0))],
            out_specs=rows,
            scratch_shapes=[pltpu.VMEM((D, 2 * D_EXPERT), BF16), pltpu.VMEM((D_EXPERT, D), BF16)]),
        out_shape=jax.ShapeDtypeStruct(xs.shape, F32),
        compiler_params=pltpu.CompilerParams(dimension_semantics=("arbitrary",),
                                             vmem_limit_bytes=VMEM_LIMIT_BYTES),
        name="ffn",
    )(tile_expert, n_used, xs, wg, wu, wd)


def _combine_kernel(slot_ref, next_slot_ref, w_ref, h_ref, x1_ref, g2_ref, fw_ref, wsgu_ref, wsd_ref, y_ref,
                    o_ref, buf, sem):
    tmc = h_ref.shape[0] // ROW_TILE
    i = pl.program_id(0)
    last = pl.num_programs(0) - 1

    def row_copy(slots, b, j, k):
        src = pl.ds(pl.multiple_of(slots[k, j], ROW_TILE), ROW_TILE)
        dst = buf.at[b, k, pl.ds(pl.multiple_of(j * ROW_TILE, ROW_TILE), ROW_TILE)]
        return pltpu.make_async_copy(y_ref.at[src], dst, sem.at[b])

    def start_rows(slots, b):
        def body(j, c):
            for k in range(TOP_K):
                row_copy(slots, b, j, k).start(priority=k % 2)
            return c
        lax.fori_loop(0, tmc, body, 0)

    def wait_rows(b):
        def body(j, c):
            for k in range(TOP_K):
                row_copy(slot_ref, b, j, k).wait()
            return c
        lax.fori_loop(0, tmc, body, 0)

    def for_buffer(step, fn):
        for b in range(2):
            @pl.when(step % 2 == b)
            def _():
                fn(b)

    @pl.when(i == 0)
    def _():
        start_rows(slot_ref, 0)

    @pl.when(i < last)
    def _():
        for_buffer(i + 1, lambda b: start_rows(next_slot_ref, b))

    gu = jnp.dot(_load_row_tiled(h_ref, tmc).astype(BF16), wsgu_ref[...], preferred_element_type=F32)
    act = _silu(gu[:, :D_EXPERT]) * gu[:, D_EXPERT:]
    shared = jnp.dot(act.astype(BF16), wsd_ref[...], preferred_element_type=F32)
    w = w_ref[...]
    lane = lax.broadcasted_iota(jnp.int32, w.shape, 1)

    def finish(b):
        wait_rows(b)
        acc = shared
        for k in range(TOP_K):
            acc = acc + _lane_pick(w, lane, k) * _load_row_tiled(buf, tmc, lead=(b, k))
        y = x1_ref[...] + g2_ref[0] * acc
        o_ref[...] = _rms(y) * fw_ref[...]

    for_buffer(i, finish)


def _combine(slot, wts, h2, x1, g2, fw, wsgu, wsd, ys, T, tmc):
    M, D = x1.shape
    const = lambda shape: pl.BlockSpec(shape, lambda i: (0,) * len(shape))
    tile = lambda w: pl.BlockSpec((tmc, w), lambda i: (i, 0))
    row_tiled = pl.BlockSpec((tmc * ROW_TILE, LANES), lambda i: (i, 0))
    n_steps = M // tmc
    return pl.pallas_call(
        _combine_kernel,
        grid=(n_steps,),
        in_specs=[pl.BlockSpec((TOP_K, tmc), lambda i: (0, i), memory_space=pltpu.SMEM),
                  pl.BlockSpec((TOP_K, tmc), lambda i: (0, jnp.minimum(i + 1, n_steps - 1)), memory_space=pltpu.SMEM),
                  tile(GATE_LANES), row_tiled, tile(D),
                  pl.BlockSpec((1, 1, D), lambda i: (i // (T // tmc), 0, 0)),
                  const((1, D)), const(wsgu.shape), const(wsd.shape),
                  pl.BlockSpec(memory_space=pl.ANY)],
        out_specs=tile(D),
        out_shape=jax.ShapeDtypeStruct((M, D), F32),
        scratch_shapes=[pltpu.VMEM((2, TOP_K, tmc * ROW_TILE, LANES), F32), pltpu.SemaphoreType.DMA((2,))],
        compiler_params=pltpu.CompilerParams(dimension_semantics=("arbitrary",),
                                             vmem_limit_bytes=VMEM_LIMIT_BYTES),
        name="combine",
    )(slot, slot, wts, h2, x1, g2, fw, wsgu, wsd, ys)


def _pick_tile(n, want):
    t = min(n, want)
    assert n % t == 0 and t % CHUNK == 0, (n, want)
    return t


def kernel(x, c, w_ada, b_ada, norm1_w, w_in, conv_w, gdn_a_log, gdn_dt_bias, gdn_norm_w, hg_lb, hg_norm_w,
           w_out, norm2_w, w_router, router_bias, w_gate, w_up, w_down, ws_gate, ws_up, ws_down, final_norm_w):
    B, T, D = x.shape
    M = B * T
    depth = w_ada.shape[0]
    assert depth == 1 and T % CHUNK == 0 and B <= 8
    layer = 0
    tt = _pick_tile(T, 512)

    c_pad = jnp.pad(c, ((0, 8 - B), (0, 0)))
    mod = _ada(c_pad, w_ada[layer], b_ada[layer].reshape(1, -1))[:B]
    sh1, sc1, g1, sh2, sc2, g2 = (m.reshape(B, 1, D) for m in jnp.split(mod, 6, axis=-1))

    w = w_in[layer]
    qkv_w = 3 * GW
    sizes = (GW, HEADS, HEADS, GW, GW, GW, GW)
    offs = [qkv_w]
    for s in sizes:
        offs.append(offs[-1] + s)
    seg = lambda i: w[:, offs[i]:offs[i + 1]]
    small = jnp.pad(jnp.concatenate([seg(1), seg(2)], axis=1), ((0, 0), (0, GATE_LANES - 2 * HEADS)))
    w_all = jnp.concatenate([w[:, :qkv_w], seg(0), seg(3), seg(4), seg(5), seg(6), small], axis=1).astype(BF16)
    lane_pad = lambda v: jnp.pad(v.astype(F32).reshape(1, HEADS), ((0, 0), (HEADS, GATE_LANES - 2 * HEADS)))
    idx = jnp.arange(tt)
    tri = ((idx[:, None] >= idx[None, :]) & (idx[:, None] // CHUNK == idx[None, :] // CHUNK)).astype(BF16)

    qa, ka, va, ga, sm, bcum, kb, ib, qb, gb = _inproj(
        layer, x, norm1_w[layer].reshape(1, D), sc1, sh1, w_all, conv_w[layer].astype(F32),
        lane_pad(gdn_a_log[layer]), lane_pad(gdn_dt_bias[layer]), hg_lb.astype(F32), tri, tt)

    gct = sm[:, :, HEADS:2 * HEADS].transpose(0, 2, 1).reshape(B, HEADS, T // CHUNK, CHUNK)
    oa = _gdn(qa, ka, va, ga, sm, gct, gdn_norm_w[layer].reshape(1, DH), tt)
    ob = _hgrn(qb, kb, ib, bcum, gb, hg_norm_w[layer].reshape(1, GW), tt)

    wo = w_out[layer].astype(BF16)
    x1, h2, eidx, rank, wts, cnt = _outproj(oa, ob, x, wo[:GW], wo[GW:], g1, norm2_w[layer].reshape(1, D), sc2, sh2,
                                            w_router[layer].T, router_bias[layer].reshape(N_EXPERTS, 1), tt)

    counts = cnt[:, 0].astype(jnp.int32)
    padded = (counts + MOE_ROWS - 1) // MOE_ROWS * MOE_ROWS
    ends = jnp.cumsum(padded)
    offsets = ends - padded
    n_tiles = (M * TOP_K) // MOE_ROWS + N_EXPERTS
    n_used = (ends[-1] // MOE_ROWS).astype(jnp.int32)
    tile_ids = jnp.minimum(jnp.arange(n_tiles, dtype=jnp.int32), n_used - 1)
    tile_expert = jnp.sum(((ends // MOE_ROWS)[None, :] <= tile_ids[:, None]).astype(jnp.int32), axis=1)
    tile_expert = jnp.minimum(tile_expert, N_EXPERTS - 1)
    experts = jnp.arange(N_EXPERTS, dtype=jnp.int32)[:, None, None]
    slot = jnp.sum(jnp.where(eidx[None] == experts, offsets[:, None, None], 0), axis=0) + rank
    slot = slot * ROW_TILE
    zon = (padded > 0).astype(jnp.int32)
    zstart = (jnp.maximum(ends - MOE_ROWS, 0) * ROW_TILE).astype(jnp.int32)

    assert D == ROW_TILE * LANES
    h2f = h2.reshape(M * ROW_TILE, LANES)
    xs = _dispatch(slot, zstart, zon, h2f, n_tiles * MOE_ROWS, _pick_tile(M, 512))
    ys = _ffn(tile_expert.astype(jnp.int32), n_used.reshape(1), xs, w_gate[layer], w_up[layer], w_down[layer])
    wsgu = jnp.concatenate([ws_gate[layer], ws_up[layer]], axis=-1).astype(BF16)
    out = _combine(slot, wts.reshape(M, GATE_LANES), h2f, x1.reshape(M, D), g2, final_norm_w.reshape(1, D),
                   wsgu, ws_down[layer].astype(BF16), ys, T, _pick_tile(T, 128))
    return out.reshape(B, T, D)
```

```python
import functools

import jax
import jax.numpy as jnp
from jax import lax
from jax.experimental import pallas as pl
from jax.experimental.pallas import tpu as pltpu

F32 = jnp.float32
BF16 = jnp.bfloat16

EPS = 1e-6
CHUNK = 64
SUB = 8
HEADS = 4
DH = 128
GW = HEADS * DH
CONV_K = 4
N_EXPERTS = 64
N_GROUPS = 8
GROUP_SIZE = N_EXPERTS // N_GROUPS
TOPK_GROUPS = 4
TOP_K = 8
D_EXPERT = 256
ROUTE_SCALE = 2.5
GATE_LANES = 128
GDN_CHUNKS_PER_ITER = 2
MOE_ROWS = 256
ROW_TILE, LANES = 8, 128

VMEM_LIMIT_BYTES = 56 * 1024 * 1024

ACT = BF16


def _silu(x):
    return x * jax.nn.sigmoid(x)


def _dot(a, b):
    return jnp.dot(a.astype(BF16), b.astype(BF16), preferred_element_type=F32)


def _dot_nt(a, b):
    return lax.dot_general(a.astype(BF16), b.astype(BF16), (((1,), (1,)), ((), ())),
                           preferred_element_type=F32)


def _dot_tn(a, b):
    return lax.dot_general(a.astype(BF16), b.astype(BF16), (((0,), (0,)), ((), ())),
                           preferred_element_type=F32)


def _split2(x):
    hi = x.astype(BF16)
    lo = (x - hi.astype(F32)).astype(BF16)
    return hi, lo


def _dot3(a, b, dot=_dot):
    ah, al = _split2(a)
    bh, bl = _split2(b)
    return dot(ah, bh) + dot(ah, bl) + dot(al, bh)


def _cumsum_rows(tri, x):
    hi = x.astype(BF16)
    r = x - hi.astype(F32)
    mid = r.astype(BF16)
    lo = (r - mid.astype(F32)).astype(BF16)
    return (jnp.dot(tri, hi, preferred_element_type=F32)
            + jnp.dot(tri, mid, preferred_element_type=F32)
            + jnp.dot(tri, lo, preferred_element_type=F32))


def _lane_pick(tile, lane, idx):
    return jnp.sum(jnp.where(lane == idx, tile, 0.0), axis=1, keepdims=True)


def _rms(x):
    return x * lax.rsqrt(jnp.mean(x * x, axis=-1, keepdims=True) + EPS)


def _to_rows(x3):
    r = x3.shape[0]
    xt = jnp.swapaxes(x3.reshape(r // ROW_TILE, ROW_TILE, ROW_TILE, LANES), 1, 2)
    return jnp.concatenate([xt[:, s].reshape(r, LANES) for s in range(ROW_TILE)], axis=1)


def _to_row_tiles(x):
    r = x.shape[0]
    xt = jnp.stack([x[:, s * LANES:(s + 1) * LANES].reshape(r // ROW_TILE, ROW_TILE, LANES) for s in range(ROW_TILE)],
                   axis=1)
    return jnp.swapaxes(xt, 1, 2).reshape(r, ROW_TILE, LANES)


def _ada_kernel(c_ref, w_ref, b_ref, o_ref):
    ca = _silu(c_ref[...])
    o_ref[...] = _dot3(ca, w_ref[...]) + b_ref[...]


def _ada(c_pad, w, b):
    rows, d = c_pad.shape
    n = w.shape[1]
    tn = 1024
    return pl.pallas_call(
        _ada_kernel,
        grid=(n // tn,),
        in_specs=[pl.BlockSpec((rows, d), lambda j: (0, 0)),
                  pl.BlockSpec((d, tn), lambda j: (0, j)),
                  pl.BlockSpec((1, tn), lambda j: (0, j))],
        out_specs=pl.BlockSpec((rows, tn), lambda j: (0, j)),
        out_shape=jax.ShapeDtypeStruct((rows, n), F32),
        compiler_params=pltpu.CompilerParams(dimension_semantics=("arbitrary",),
                                             vmem_limit_bytes=VMEM_LIMIT_BYTES),
        name="ada",
    )(c_pad, w, b)


def _inproj_kernel(layer, x_ref, n1_ref, sc_ref, sh_ref, w_ref, cw_ref, alog_ref, dt_ref, lb_ref, tri_ref,
                   qa_ref, ka_ref, va_ref, ga_ref, sm_ref, b_ref, kb_ref, ib_ref, qb_ref, gb_ref,
                   pbuf):
    tt = x_ref.shape[1]
    t = pl.program_id(1)

    h = _rms(x_ref[0]) * n1_ref[...]
    h = h * (1.0 + sc_ref[0]) + sh_ref[0]
    hb = h.astype(BF16)

    def proj(g, width=GW):
        return jnp.dot(hb, w_ref[:, g * GW:g * GW + width], preferred_element_type=F32)

    @pl.when(t == 0)
    def _():
        pbuf[:, 0:8, :] = jnp.zeros((3, 8, GW), F32)

    for g, out_ref in enumerate((qa_ref, ka_ref, va_ref)):
        cols = slice(g * GW, (g + 1) * GW)
        p = proj(g)
        pbuf[g, 8:8 + tt, :] = p
        y = p * cw_ref[CONV_K - 1:CONV_K, cols]
        for j in range(1, CONV_K):
            y = y + pbuf[g, 8 - j:8 - j + tt, :] * cw_ref[CONV_K - 1 - j:CONV_K - j, cols]
        pbuf[g, 0:8, :] = pbuf[g, tt:tt + 8, :]
        y = _silu(y)
        if g == 2:
            out_ref[0] = y.astype(out_ref.dtype)
        else:
            scale = DH ** -0.5 if g == 0 else 1.0
            for hh in range(HEADS):
                hs = slice(hh * DH, (hh + 1) * DH)
                yh = y[:, hs]
                inv = lax.rsqrt(jnp.sum(yh * yh, axis=-1, keepdims=True) + EPS)
                out_ref[0, :, hs] = (yh * inv * scale).astype(out_ref.dtype)

    ga_ref[0] = _silu(proj(3)).astype(ga_ref.dtype)

    ps = proj(8, GATE_LANES)
    lane = lax.broadcasted_iota(jnp.int32, ps.shape, 1)
    beta = jax.nn.sigmoid(ps)
    z = ps + dt_ref[...]
    softplus = jnp.maximum(z, 0.0) + jnp.log1p(jnp.exp(-jnp.abs(z)))
    g_log = -jnp.exp(alog_ref[...]) * softplus
    tri = tri_ref[...]
    gc = _cumsum_rows(tri, jnp.where((lane >= HEADS) & (lane < 2 * HEADS), g_log, 0.0))
    sm_ref[0] = jnp.where(lane < HEADS, beta, gc)

    hl = lb_ref[...]
    e = jnp.exp(hl - jnp.max(hl, axis=0, keepdims=True))
    lb = jnp.sum(e[0:layer + 1], axis=0, keepdims=True) / jnp.sum(e, axis=0, keepdims=True)
    fr = proj(4)
    logf = jnp.log(lb + (1.0 - lb) * jax.nn.sigmoid(fr))
    b_ref[0] = _cumsum_rows(tri, logf)
    kb_ref[0] = ((1.0 - lb) * jax.nn.sigmoid(-fr)).astype(kb_ref.dtype)
    ib_ref[0] = proj(5).astype(ib_ref.dtype)
    qb_ref[0] = _silu(proj(6)).astype(qb_ref.dtype)
    gb_ref[0] = _silu(proj(7)).astype(gb_ref.dtype)


def _inproj(layer, x, n1, sc1, sh1, w_all, conv_w, alog_pad, dt_pad, hg_lb, tri, tt):
    B, T, D = x.shape
    const = lambda shape: pl.BlockSpec(shape, lambda b, t: (0,) * len(shape))
    act = lambda dt: jax.ShapeDtypeStruct((B, T, GW), dt)
    tile = lambda w: pl.BlockSpec((1, tt, w), lambda b, t: (b, t, 0))
    per_batch = pl.BlockSpec((1, 1, D), lambda b, t: (b, 0, 0))
    return pl.pallas_call(
        functools.partial(_inproj_kernel, layer),
        grid=(B, T // tt),
        in_specs=[tile(D), const((1, D)), per_batch, per_batch,
                  const(w_all.shape), const(conv_w.shape), const((1, GATE_LANES)), const((1, GATE_LANES)),
                  const(hg_lb.shape), const((tt, tt))],
        out_specs=[tile(GW), tile(GW), tile(GW), tile(GW), tile(GATE_LANES), tile(GW),
                   tile(GW), tile(GW), tile(GW), tile(GW)],
        out_shape=[act(ACT), act(ACT), act(ACT), act(ACT),
                   jax.ShapeDtypeStruct((B, T, GATE_LANES), F32), act(F32),
                   act(ACT), act(ACT), act(ACT), act(ACT)],
        scratch_shapes=[pltpu.VMEM((3, tt + 8, GW), F32)],
        compiler_params=pltpu.CompilerParams(dimension_semantics=("arbitrary", "arbitrary"),
                                             vmem_limit_bytes=VMEM_LIMIT_BYTES),
        name="inproj",
    )(x, n1, sc1, sh1, w_all, conv_w, alog_pad, dt_pad, hg_lb, tri)


def _gdn_prep_kernel(q_ref, k_ref, v_ref, sm_ref, gct_ref, o_ref, qt_ref, m_ref, n_ref):
    tt = q_ref.shape[1]
    nc = tt // CHUNK
    row = lax.broadcasted_iota(jnp.int32, (CHUNK, CHUNK), 0)
    col = lax.broadcasted_iota(jnp.int32, (CHUNK, CHUNK), 1)
    causal = row >= col
    strict = row > col
    eye = jnp.where(row == col, 1.0, 0.0)
    lane = lax.broadcasted_iota(jnp.int32, (CHUNK, GATE_LANES), 1)

    def body(i, carry):
        chains = [(GDN_CHUNKS_PER_ITER * i + j, hh) for j in range(GDN_CHUNKS_PER_ITER) for hh in range(HEADS)]
        rows = [pl.ds(pl.multiple_of(c * CHUNK, CHUNK), CHUNK) for c, _ in chains]
        hs = [slice(hh * DH, (hh + 1) * DH) for _, hh in chains]
        n = range(len(chains))
        sm = [sm_ref[0, rows[j], :] for j in n]
        q = [q_ref[0, rows[j], hs[j]].astype(F32) for j in n]
        k = [k_ref[0, rows[j], hs[j]].astype(F32) for j in n]
        v = [v_ref[0, rows[j], hs[j]].astype(F32) for j in n]
        beta = [_lane_pick(sm[j], lane, chains[j][1]) for j in n]
        gcol = [_lane_pick(sm[j], lane, HEADS + chains[j][1]) for j in n]
        grow = [gct_ref[0, hh, pl.ds(c, 1), :] for c, hh in chains]
        decay = [jnp.exp(jnp.where(causal, gcol[j] - grow[j], -jnp.inf)) for j in n]
        kb = [k[j] * beta[j] for j in n]
        L = [jnp.where(strict, _dot_nt(kb[j], k[j]) * decay[j], 0.0) for j in n]
        tinv = [eye - L[j] for j in n]
        pw = [_dot3(L[j], L[j]) for j in n]
        for _ in range(4):
            tinv = [tinv[j] + _dot3(tinv[j], pw[j]) for j in n]
            pw = [_dot3(pw[j], pw[j]) for j in n]
        tinv = [tinv[j] + _dot3(tinv[j], pw[j]) for j in n]
        eg = [jnp.exp(gcol[j]) for j in n]
        sol = [_dot(tinv[j], jnp.concatenate([v[j] * beta[j], kb[j] * eg[j]], axis=1)) for j in n]
        attn = [_dot_nt(q[j], k[j]) * decay[j] for j in n]
        k_tail = [k[j] * jnp.exp(gcol[j][CHUNK - 1:CHUNK, :] - gcol[j]) for j in n]
        au = [_dot(attn[j], sol[j]) for j in n]
        ku = [_dot_tn(k_tail[j], sol[j]) for j in n]
        for j, (c, hh) in enumerate(chains):
            o_ref[0, rows[j], hs[j]] = au[j][:, :DH]
            qt_ref[0, rows[j], hs[j]] = (q[j] * eg[j] - au[j][:, DH:]).astype(qt_ref.dtype)
            n_ref[0, hh, c] = ku[j][:, :DH].astype(n_ref.dtype)
            m_ref[0, hh, c] = (-ku[j][:, DH:]).astype(m_ref.dtype)
        return carry

    lax.fori_loop(0, nc // GDN_CHUNKS_PER_ITER, body, 0)


def _gdn_scan_kernel(o_ref, qt_ref, m_ref, n_ref, gct_ref, sg_ref, nw_ref, out_ref, s_ref):
    nb, tt = o_ref.shape[0], o_ref.shape[1]
    nc = tt // CHUNK

    @pl.when(pl.program_id(0) == 0)
    def _():
        s_ref[...] = jnp.zeros(s_ref.shape, F32)

    nw = nw_ref[...]

    def body(c, carry):
        rows = pl.ds(pl.multiple_of(c * CHUNK, CHUNK), CHUNK)
        for b in range(nb):
            for hh in range(HEADS):
                hs = slice(hh * DH, (hh + 1) * DH)
                S = s_ref[b, hh]
                Sb = S.astype(BF16)
                glast = gct_ref[b, hh, pl.ds(c, 1), :][:, CHUNK - 1:CHUNK]
                o = o_ref[b, rows, hs] + jnp.dot(qt_ref[b, rows, hs], Sb, preferred_element_type=F32)
                s_ref[b, hh] = (S * jnp.exp(glast) + jnp.dot(m_ref[b, hh, c], Sb, preferred_element_type=F32)
                                + n_ref[b, hh, c].astype(F32))
                o = _rms(o) * nw * sg_ref[b, rows, hs].astype(F32)
                out_ref[b, rows, hs] = o.astype(out_ref.dtype)
        return carry

    lax.fori_loop(0, nc, body, 0)


def _gdn(q, k, v, sg, sm, gct, nw, tt):
    B, T, _ = q.shape
    nc = tt // CHUNK
    assert nc % GDN_CHUNKS_PER_ITER == 0
    n_chunks = T // CHUNK
    tile = lambda w: pl.BlockSpec((1, tt, w), lambda b, t: (b, t, 0))
    mat = jax.ShapeDtypeStruct((B, HEADS, n_chunks, DH, DH), ACT)
    o_part, qt, m, n = pl.pallas_call(
        _gdn_prep_kernel,
        grid=(B, T // tt),
        in_specs=[tile(GW), tile(GW), tile(GW), tile(GATE_LANES),
                  pl.BlockSpec((1, HEADS, nc, CHUNK), lambda b, t: (b, 0, t, 0))],
        out_specs=[tile(GW), tile(GW),
                   pl.BlockSpec((1, HEADS, nc, DH, DH), lambda b, t: (b, 0, t, 0, 0)),
                   pl.BlockSpec((1, HEADS, nc, DH, DH), lambda b, t: (b, 0, t, 0, 0))],
        out_shape=[jax.ShapeDtypeStruct((B, T, GW), F32), jax.ShapeDtypeStruct((B, T, GW), ACT), mat, mat],
        compiler_params=pltpu.CompilerParams(dimension_semantics=("arbitrary", "arbitrary"),
                                             vmem_limit_bytes=VMEM_LIMIT_BYTES),
        name="gdn_prep",
    )(q, k, v, sm, gct)

    full = lambda w: pl.BlockSpec((B, tt, w), lambda t: (0, t, 0))
    mats = pl.BlockSpec((B, HEADS, nc, DH, DH), lambda t: (0, 0, t, 0, 0))
    return pl.pallas_call(
        _gdn_scan_kernel,
        grid=(T // tt,),
        in_specs=[full(GW), full(GW), mats, mats,
                  pl.BlockSpec((B, HEADS, nc, CHUNK), lambda t: (0, 0, t, 0)),
                  full(GW), pl.BlockSpec((1, DH), lambda t: (0, 0))],
        out_specs=full(GW),
        out_shape=jax.ShapeDtypeStruct((B, T, GW), ACT),
        scratch_shapes=[pltpu.VMEM((B, HEADS, DH, DH), F32)],
        compiler_params=pltpu.CompilerParams(dimension_semantics=("arbitrary",),
                                             vmem_limit_bytes=VMEM_LIMIT_BYTES),
        name="gdn_scan",
    )(o_part, qt, m, n, gct, sg, nw)


def _hgrn_kernel(q_ref, k_ref, v_ref, b_ref, sg_ref, nw_ref, o_ref, st_ref):
    tt = q_ref.shape[1]
    nc = tt // CHUNK

    @pl.when(pl.program_id(1) == 0)
    def _():
        st_ref[...] = jnp.zeros(st_ref.shape, F32)

    row = lax.broadcasted_iota(jnp.int32, (CHUNK, CHUNK), 0)
    col = lax.broadcasted_iota(jnp.int32, (CHUNK, CHUNK), 1)
    diag_block = ((col // SUB) == (row // SUB)) & (col <= row)
    heads = range(HEADS)
    hs = [slice(hh * DH, (hh + 1) * DH) for hh in heads]

    def body(c, carry):
        rows = pl.ds(pl.multiple_of(c * CHUNK, CHUNK), CHUNK)
        q = [q_ref[0, rows, hs[h]].astype(F32) for h in heads]
        k = [k_ref[0, rows, hs[h]].astype(F32) for h in heads]
        v = [v_ref[0, rows, hs[h]].astype(F32) for h in heads]
        b = [b_ref[0, rows, hs[h]] for h in heads]
        blast = [b[h][CHUNK - 1:CHUNK, :] for h in heads]
        st = [st_ref[h] for h in heads]
        o = [_dot_nt(q[h] * jnp.exp(b[h]), st[h]) for h in heads]
        k_tail = [k[h] * jnp.exp(blast[h] - b[h]) for h in heads]
        for h in heads:
            st_ref[h] = st[h] * jnp.exp(blast[h]) + _dot_tn(v[h], k_tail[h])

        blocks = [[jnp.zeros((SUB, CHUNK), F32)] for _ in heads]
        for i in range(1, CHUNK // SUB):
            lo, hi = i * SUB, (i + 1) * SUB
            for h in heads:
                r = b[h][lo:lo + 1, :]
                qi = q[h][lo:hi] * jnp.exp(b[h][lo:hi] - r)
                kj = k[h][:lo] * jnp.exp(jnp.minimum(r - b[h][:lo], 0.0))
                kj = jnp.concatenate([kj, jnp.zeros((CHUNK - lo, DH), F32)], axis=0)
                blocks[h].append(_dot_nt(qi, kj))
        a = []
        for h in heads:
            f = jnp.exp(jnp.minimum(b[h] - pltpu.roll(b[h], 1, 0), 0.0))
            e = None
            a_diag = jnp.zeros((CHUNK, CHUNK), F32)
            for delta in range(SUB):
                if delta == 0:
                    term = q[h] * k[h]
                else:
                    fsh = f if delta == 1 else pltpu.roll(f, delta - 1, 0)
                    e = fsh if e is None else e * fsh
                    term = q[h] * pltpu.roll(k[h], delta, 0) * e
                colv = jnp.sum(term, axis=1, keepdims=True)
                a_diag = jnp.where(row - col == delta, colv, a_diag)
            a.append(jnp.where(diag_block, a_diag, jnp.concatenate(blocks[h], axis=0)))

        o = [o[h] + _dot(a[h], v[h]) for h in heads]
        o = jnp.concatenate(o, axis=1)
        o = _rms(o) * nw_ref[...] * sg_ref[0, rows, :].astype(F32)
        o_ref[0, rows, :] = o.astype(o_ref.dtype)
        return carry

    lax.fori_loop(0, nc, body, 0)


def _hgrn(q, k, v, b, sg, nw, tt):
    B, T, _ = q.shape
    tile = pl.BlockSpec((1, tt, GW), lambda bi, t: (bi, t, 0))
    return pl.pallas_call(
        _hgrn_kernel,
        grid=(B, T // tt),
        in_specs=[tile, tile, tile, tile, tile, pl.BlockSpec((1, GW), lambda bi, t: (0, 0))],
        out_specs=tile,
        out_shape=jax.ShapeDtypeStruct((B, T, GW), ACT),
        scratch_shapes=[pltpu.VMEM((HEADS, DH, DH), F32)],
        compiler_params=pltpu.CompilerParams(dimension_semantics=("arbitrary", "arbitrary"),
                                             vmem_limit_bytes=VMEM_LIMIT_BYTES),
        name="hgrn",
    )(q, k, v, b, sg, nw)


def _outproj_kernel(oa_ref, ob_ref, x_ref, wa_ref, wb_ref, g1_ref, n2_ref, sc_ref, sh_ref, wr_ref, rb_ref,
                    x1_ref, h2_ref, eidx_ref, rank_ref, wts_ref, cnt_ref):
    tm = x_ref.shape[1]

    @pl.when((pl.program_id(0) == 0) & (pl.program_id(1) == 0))
    def _():
        cnt_ref[...] = jnp.zeros(cnt_ref.shape, F32)

    mix = (jnp.dot(oa_ref[0], wa_ref[...], preferred_element_type=F32)
           + jnp.dot(ob_ref[0], wb_ref[...], preferred_element_type=F32))
    x1 = x_ref[0] + g1_ref[0] * mix
    x1_ref[0] = x1
    h2 = _rms(x1) * n2_ref[...]
    h2 = h2 * (1.0 + sc_ref[0]) + sh_ref[0]
    h2_ref[...] = _to_row_tiles(h2)

    scores = jax.nn.sigmoid(_dot3(wr_ref[...], h2, dot=_dot_nt))
    sel = scores + rb_ref[...]
    sub = lax.broadcasted_iota(jnp.int32, (GROUP_SIZE, tm), 0)
    neg = -jnp.inf
    groups = range(N_GROUPS)

    def take_max(blk):
        m = jnp.max(blk, axis=0, keepdims=True)
        first = jnp.min(jnp.where(blk == m, sub, GROUP_SIZE), axis=0, keepdims=True)
        hit = sub == first
        return m, hit, jnp.where(hit, neg, blk)

    blk_of = lambda a, g: a[g * GROUP_SIZE:(g + 1) * GROUP_SIZE]
    sel_blk = [blk_of(sel, g) for g in groups]
    group_score = jnp.zeros((N_GROUPS, tm), F32)
    for g in groups:
        m1, _, rest = take_max(sel_blk[g])
        m2 = jnp.max(rest, axis=0, keepdims=True)
        group_score = jnp.where(sub == g, m1 + m2, group_score)
    group_on = jnp.zeros((N_GROUPS, tm), F32)
    for _ in range(TOPK_GROUPS):
        _, hit, group_score = take_max(group_score)
        group_on = jnp.where(hit, 1.0, group_on)

    cand = [jnp.where(group_on[g:g + 1] > 0.0, sel_blk[g], neg) for g in groups]
    picked = [jnp.zeros((GROUP_SIZE, tm), F32) for _ in groups]
    chosen = []
    for _ in range(TOP_K):
        m = jnp.max(functools.reduce(jnp.maximum, cand), axis=0, keepdims=True)
        first = functools.reduce(jnp.minimum, [jnp.where(cand[g] == m, sub + g * GROUP_SIZE, N_EXPERTS)
                                               for g in groups])
        first = jnp.min(first, axis=0, keepdims=True)
        chosen.append(first)
        for g in groups:
            hit = (sub + g * GROUP_SIZE) == first
            picked[g] = jnp.where(hit, 1.0, picked[g])
            cand[g] = jnp.where(hit, neg, cand[g])

    picked_all = jnp.concatenate(picked, axis=0)
    r_i = lax.broadcasted_iota(jnp.int32, (tm, tm), 0)
    c_i = lax.broadcasted_iota(jnp.int32, (tm, tm), 1)
    earlier = jnp.where(r_i < c_i, 1.0, 0.0).astype(BF16)
    before = jnp.dot(picked_all.astype(BF16), earlier, preferred_element_type=F32) + cnt_ref[:, 0:1]
    cnt_ref[...] = cnt_ref[...] + jnp.sum(picked_all, axis=1, keepdims=True)

    def pick_value(table, first):
        parts = [jnp.where((sub + g * GROUP_SIZE) == first, blk_of(table, g), 0.0) for g in groups]
        return jnp.sum(functools.reduce(jnp.add, parts), axis=0, keepdims=True)

    w_k = [pick_value(scores, f) for f in chosen]
    denom = functools.reduce(jnp.add, w_k)
    eidx = jnp.zeros((TOP_K, tm), jnp.int32)
    rank = jnp.zeros((TOP_K, tm), jnp.int32)
    wts = jnp.zeros((TOP_K, tm), F32)
    for k in range(TOP_K):
        eidx = jnp.where(sub == k, chosen[k], eidx)
        rank = jnp.where(sub == k, pick_value(before, chosen[k]).astype(jnp.int32), rank)
        wts = jnp.where(sub == k, w_k[k] / denom * ROUTE_SCALE, wts)
    eidx_ref[...] = eidx
    rank_ref[...] = rank
    pad = jnp.zeros((GATE_LANES - TOP_K, tm), F32)
    wts_ref[0] = jnp.concatenate([wts, pad], axis=0).T


def _outproj(oa, ob, x, wa, wb, g1, n2, sc2, sh2, wr_t, rb, tm):
    B, T, D = x.shape
    nt = T // tm
    const = lambda shape: pl.BlockSpec(shape, lambda b, t: (0,) * len(shape))
    tile = lambda w: pl.BlockSpec((1, tm, w), lambda b, t: (b, t, 0))
    per_batch = pl.BlockSpec((1, 1, D), lambda b, t: (b, 0, 0))
    picks = pl.BlockSpec((TOP_K, tm), lambda b, t: (0, b * nt + t))
    return pl.pallas_call(
        _outproj_kernel,
        grid=(B, nt),
        in_specs=[tile(GW), tile(GW), tile(D), const(wa.shape), const(wb.shape), per_batch,
                  const((1, D)), per_batch, per_batch, const(wr_t.shape), const(rb.shape)],
        out_specs=[tile(D), pl.BlockSpec((tm, ROW_TILE, LANES), lambda b, t: (b * nt + t, 0, 0)),
                   picks, picks, tile(GATE_LANES), const((N_EXPERTS, GATE_LANES))],
        out_shape=[jax.ShapeDtypeStruct((B, T, D), F32), jax.ShapeDtypeStruct((B * T, ROW_TILE, LANES), F32),
                   jax.ShapeDtypeStruct((TOP_K, B * T), jnp.int32), jax.ShapeDtypeStruct((TOP_K, B * T), jnp.int32),
                   jax.ShapeDtypeStruct((B, T, GATE_LANES), F32),
                   jax.ShapeDtypeStruct((N_EXPERTS, GATE_LANES), F32)],
        compiler_params=pltpu.CompilerParams(dimension_semantics=("arbitrary", "arbitrary"),
                                             vmem_limit_bytes=VMEM_LIMIT_BYTES),
        name="outproj",
    )(oa, ob, x, wa, wb, g1, n2, sc2, sh2, wr_t, rb)


def _dispatch_kernel(slot_ref, h_ref, xs_ref, sem):
    tmd = h_ref.shape[0]

    def start_rows(j, c):
        for k in range(TOP_K):
            pltpu.make_async_copy(h_ref.at[pl.ds(j, 1)], xs_ref.at[pl.ds(slot_ref[k, j], 1)], sem).start(
                priority=k % 2)
        return c

    lax.fori_loop(0, tmd, start_rows, 0)
    for _ in range(TOP_K):
        pltpu.make_async_copy(h_ref, xs_ref.at[pl.ds(0, tmd)], sem).wait()


def _dispatch(slot, h2, n_rows, tmd):
    M = h2.shape[0]
    return pl.pallas_call(
        _dispatch_kernel,
        grid=(M // tmd,),
        in_specs=[pl.BlockSpec((TOP_K, tmd), lambda i: (0, i), memory_space=pltpu.SMEM),
                  pl.BlockSpec((tmd, ROW_TILE, LANES), lambda i: (i, 0, 0))],
        out_specs=pl.BlockSpec(memory_space=pl.ANY),
        out_shape=jax.ShapeDtypeStruct((n_rows, ROW_TILE, LANES), F32),
        scratch_shapes=[pltpu.SemaphoreType.DMA],
        compiler_params=pltpu.CompilerParams(dimension_semantics=("arbitrary",),
                                             vmem_limit_bytes=VMEM_LIMIT_BYTES),
        name="dispatch",
    )(slot, h2)


def _ffn_kernel(te_ref, nu_ref, nv_ref, x_ref, wg_ref, wu_ref, wd_ref, y_ref, wgu_s, wd_s):
    i = pl.program_id(0)

    @pl.when(i < nu_ref[0])
    def _():
        @pl.when((i == 0) | (te_ref[i] != te_ref[jnp.maximum(i - 1, 0)]))
        def _():
            wgu_s[:, :D_EXPERT] = wg_ref[0].astype(BF16)
            wgu_s[:, D_EXPERT:] = wu_ref[0].astype(BF16)
            wd_s[...] = wd_ref[0].astype(BF16)

        valid = lax.broadcasted_iota(jnp.int32, x_ref.shape, 0) < nv_ref[i]
        x = _to_rows(jnp.where(valid, x_ref[...], 0.0))
        gu = jnp.dot(x.astype(BF16), wgu_s[...], preferred_element_type=F32)
        act = _silu(gu[:, :D_EXPERT]) * gu[:, D_EXPERT:]
        y = jnp.dot(act.astype(BF16), wd_s[...], preferred_element_type=F32)
        y_ref[...] = jnp.where(valid, _to_row_tiles(y), 0.0)


def _ffn(tile_expert, n_used, n_valid, xs, wg, wu, wd):
    D = wg.shape[1]
    n_tiles = xs.shape[0] // MOE_ROWS
    rows = pl.BlockSpec((MOE_ROWS, ROW_TILE, LANES), lambda i, te, nu, nv: (jnp.minimum(i, nu[0] - 1), 0, 0))
    return pl.pallas_call(
        _ffn_kernel,
        grid_spec=pltpu.PrefetchScalarGridSpec(
            num_scalar_prefetch=3,
            grid=(n_tiles,),
            in_specs=[rows,
                      pl.BlockSpec((1, D, D_EXPERT), lambda i, te, nu, nv: (te[i], 0, 0)),
                      pl.BlockSpec((1, D, D_EXPERT), lambda i, te, nu, nv: (te[i], 0, 0)),
                      pl.BlockSpec((1, D_EXPERT, D), lambda i, te, nu, nv: (te[i], 0, 0))],
            out_specs=rows,
            scratch_shapes=[pltpu.VMEM((D, 2 * D_EXPERT), BF16), pltpu.VMEM((D_EXPERT, D), BF16)]),
        out_shape=jax.ShapeDtypeStruct(xs.shape, F32),
        compiler_params=pltpu.CompilerParams(dimension_semantics=("arbitrary",),
                                             vmem_limit_bytes=VMEM_LIMIT_BYTES),
        name="ffn",
    )(tile_expert, n_used, n_valid, xs, wg, wu, wd)


def _combine_kernel(slot_ref, next_slot_ref, w_ref, h_ref, x1_ref, g2_ref, fw_ref, wsgu_ref, wsd_ref, y_ref,
                    o_ref, buf, sem):
    tmc = h_ref.shape[0]
    i = pl.program_id(0)
    last = pl.num_programs(0) - 1

    def start_rows(slots, b):
        def body(j, c):
            for k in range(TOP_K):
                pltpu.make_async_copy(y_ref.at[pl.ds(slots[k, j], 1)], buf.at[b, k, pl.ds(j, 1)], sem.at[b]).start(
                    priority=k % 2)
            return c
        lax.fori_loop(0, tmc, body, 0)

    def wait_rows(b):
        for k in range(TOP_K):
            pltpu.make_async_copy(y_ref.at[pl.ds(0, tmc)], buf.at[b, k], sem.at[b]).wait()

    def for_buffer(step, fn):
        for b in range(2):
            @pl.when(step % 2 == b)
            def _():
                fn(b)

    @pl.when(i == 0)
    def _():
        start_rows(slot_ref, 0)

    @pl.when(i < last)
    def _():
        for_buffer(i + 1, lambda b: start_rows(next_slot_ref, b))

    gu = jnp.dot(_to_rows(h_ref[...]).astype(BF16), wsgu_ref[...], preferred_element_type=F32)
    act = _silu(gu[:, :D_EXPERT]) * gu[:, D_EXPERT:]
    shared = jnp.dot(act.astype(BF16), wsd_ref[...], preferred_element_type=F32)
    w = w_ref[...]
    lane = lax.broadcasted_iota(jnp.int32, w.shape, 1)

    def finish(b):
        wait_rows(b)
        routed = jnp.zeros((tmc, ROW_TILE, LANES), F32)
        for k in range(TOP_K):
            wk = jnp.broadcast_to(_lane_pick(w, lane, k), (tmc, LANES))
            routed = routed + _to_row_tiles(jnp.concatenate([wk] * ROW_TILE, axis=1)) * buf[b, k]
        acc = shared + _to_rows(routed)
        y = x1_ref[...] + g2_ref[0] * acc
        o_ref[...] = _rms(y) * fw_ref[...]

    for_buffer(i, finish)


def _combine(slot, wts, h2, x1, g2, fw, wsgu, wsd, ys, T, tmc):
    M, D = x1.shape
    const = lambda shape: pl.BlockSpec(shape, lambda i: (0,) * len(shape))
    tile = lambda w: pl.BlockSpec((tmc, w), lambda i: (i, 0))
    row_tiled = pl.BlockSpec((tmc, ROW_TILE, LANES), lambda i: (i, 0, 0))
    n_steps = M // tmc
    return pl.pallas_call(
        _combine_kernel,
        grid=(n_steps,),
        in_specs=[pl.BlockSpec((TOP_K, tmc), lambda i: (0, i), memory_space=pltpu.SMEM),
                  pl.BlockSpec((TOP_K, tmc), lambda i: (0, jnp.minimum(i + 1, n_steps - 1)), memory_space=pltpu.SMEM),
                  tile(GATE_LANES), row_tiled, tile(D),
                  pl.BlockSpec((1, 1, D), lambda i: (i // (T // tmc), 0, 0)),
                  const((1, D)), const(wsgu.shape), const(wsd.shape),
                  pl.BlockSpec(memory_space=pl.ANY)],
        out_specs=tile(D),
        out_shape=jax.ShapeDtypeStruct((M, D), F32),
        scratch_shapes=[pltpu.VMEM((2, TOP_K, tmc, ROW_TILE, LANES), F32), pltpu.SemaphoreType.DMA((2,))],
        compiler_params=pltpu.CompilerParams(dimension_semantics=("arbitrary",),
                                             vmem_limit_bytes=VMEM_LIMIT_BYTES),
        name="combine",
    )(slot, slot, wts, h2, x1, g2, fw, wsgu, wsd, ys)


def _pick_tile(n, want):
    t = min(n, want)
    assert n % t == 0 and t % CHUNK == 0, (n, want)
    return t


def kernel(x, c, w_ada, b_ada, norm1_w, w_in, conv_w, gdn_a_log, gdn_dt_bias, gdn_norm_w, hg_lb, hg_norm_w,
           w_out, norm2_w, w_router, router_bias, w_gate, w_up, w_down, ws_gate, ws_up, ws_down, final_norm_w):
    B, T, D = x.shape
    M = B * T
    depth = w_ada.shape[0]
    assert depth == 1 and T % CHUNK == 0 and B <= 8
    layer = 0
    tt = _pick_tile(T, 512)

    c_pad = jnp.pad(c, ((0, 8 - B), (0, 0)))
    mod = _ada(c_pad, w_ada[layer], b_ada[layer].reshape(1, -1))[:B]
    sh1, sc1, g1, sh2, sc2, g2 = (m.reshape(B, 1, D) for m in jnp.split(mod, 6, axis=-1))

    w = w_in[layer]
    qkv_w = 3 * GW
    sizes = (GW, HEADS, HEADS, GW, GW, GW, GW)
    offs = [qkv_w]
    for s in sizes:
        offs.append(offs[-1] + s)
    seg = lambda i: w[:, offs[i]:offs[i + 1]]
    small = jnp.pad(jnp.concatenate([seg(1), seg(2)], axis=1), ((0, 0), (0, GATE_LANES - 2 * HEADS)))
    w_all = jnp.concatenate([w[:, :qkv_w], seg(0), seg(3), seg(4), seg(5), seg(6), small], axis=1).astype(BF16)
    lane_pad = lambda v: jnp.pad(v.astype(F32).reshape(1, HEADS), ((0, 0), (HEADS, GATE_LANES - 2 * HEADS)))
    idx = jnp.arange(tt)
    tri = ((idx[:, None] >= idx[None, :]) & (idx[:, None] // CHUNK == idx[None, :] // CHUNK)).astype(BF16)

    qa, ka, va, ga, sm, bcum, kb, ib, qb, gb = _inproj(
        layer, x, norm1_w[layer].reshape(1, D), sc1, sh1, w_all, conv_w[layer].astype(F32),
        lane_pad(gdn_a_log[layer]), lane_pad(gdn_dt_bias[layer]), hg_lb.astype(F32), tri, tt)

    gct = sm[:, :, HEADS:2 * HEADS].transpose(0, 2, 1).reshape(B, HEADS, T // CHUNK, CHUNK)
    oa = _gdn(qa, ka, va, ga, sm, gct, gdn_norm_w[layer].reshape(1, DH), tt)
    ob = _hgrn(qb, kb, ib, bcum, gb, hg_norm_w[layer].reshape(1, GW), tt)

    wo = w_out[layer].astype(BF16)
    x1, h2, eidx, rank, wts, cnt = _outproj(oa, ob, x, wo[:GW], wo[GW:], g1, norm2_w[layer].reshape(1, D), sc2, sh2,
                                            w_router[layer].T, router_bias[layer].reshape(N_EXPERTS, 1), tt)

    counts = cnt[:, 0].astype(jnp.int32)
    padded = (counts + MOE_ROWS - 1) // MOE_ROWS * MOE_ROWS
    ends = jnp.cumsum(padded)
    offsets = ends - padded
    n_tiles = (M * TOP_K) // MOE_ROWS + N_EXPERTS
    n_used = (ends[-1] // MOE_ROWS).astype(jnp.int32)
    tile_ids = jnp.minimum(jnp.arange(n_tiles, dtype=jnp.int32), n_used - 1)
    tile_expert = jnp.sum(((ends // MOE_ROWS)[None, :] <= tile_ids[:, None]).astype(jnp.int32), axis=1)
    tile_expert = jnp.minimum(tile_expert, N_EXPERTS - 1)
    experts = jnp.arange(N_EXPERTS, dtype=jnp.int32)[:, None, None]
    slot = jnp.sum(jnp.where(eidx[None] == experts, offsets[:, None, None], 0), axis=0) + rank
    first_tile = jnp.take(offsets, tile_expert) // MOE_ROWS
    n_valid = jnp.clip(jnp.take(counts, tile_expert) - (tile_ids - first_tile) * MOE_ROWS, 0, MOE_ROWS)

    assert D == ROW_TILE * LANES
    xs = _dispatch(slot, h2, n_tiles * MOE_ROWS, _pick_tile(M, 512))
    ys = _ffn(tile_expert.astype(jnp.int32), n_used.reshape(1), n_valid.astype(jnp.int32), xs,
              w_gate[layer], w_up[layer], w_down[layer])
    wsgu = jnp.concatenate([ws_gate[layer], ws_up[layer]], axis=-1).astype(BF16)
    out = _combine(slot, wts.reshape(M, GATE_LANES), h2, x1.reshape(M, D), g2, final_norm_w.reshape(1, D),
                   wsgu, ws_down[layer].astype(BF16), ys, T, _pick_tile(T, 128))
    return out.reshape(B, T, D)
```

```python
import functools

import jax
import jax.numpy as jnp
from jax import lax
from jax.experimental import pallas as pl
from jax.experimental.pallas import tpu as pltpu

F32 = jnp.float32
BF16 = jnp.bfloat16

EPS = 1e-6
CHUNK = 64
SUB = 8
HEADS = 4
DH = 128
GW = HEADS * DH
CONV_K = 4
N_EXPERTS = 64
N_GROUPS = 8
GROUP_SIZE = N_EXPERTS // N_GROUPS
TOPK_GROUPS = 4
TOP_K = 8
D_EXPERT = 256
ROUTE_SCALE = 2.5
GATE_LANES = 128
GDN_CHUNKS_PER_ITER = 2
MOE_ROWS = 1024
ROW_TILE, LANES = 8, 128

VMEM_LIMIT_BYTES = 56 * 1024 * 1024

ACT = BF16


def _silu(x):
    return x * jax.nn.sigmoid(x)


def _dot(a, b):
    return jnp.dot(a.astype(BF16), b.astype(BF16), preferred_element_type=F32)


def _dot_nt(a, b):
    return lax.dot_general(a.astype(BF16), b.astype(BF16), (((1,), (1,)), ((), ())),
                           preferred_element_type=F32)


def _dot_tn(a, b):
    return lax.dot_general(a.astype(BF16), b.astype(BF16), (((0,), (0,)), ((), ())),
                           preferred_element_type=F32)


def _split2(x):
    hi = x.astype(BF16)
    lo = (x - hi.astype(F32)).astype(BF16)
    return hi, lo


def _dot3(a, b, dot=_dot):
    ah, al = _split2(a)
    bh, bl = _split2(b)
    return dot(ah, bh) + dot(ah, bl) + dot(al, bh)


def _cumsum_rows(tri, x):
    hi = x.astype(BF16)
    r = x - hi.astype(F32)
    mid = r.astype(BF16)
    lo = (r - mid.astype(F32)).astype(BF16)
    return (jnp.dot(tri, hi, preferred_element_type=F32)
            + jnp.dot(tri, mid, preferred_element_type=F32)
            + jnp.dot(tri, lo, preferred_element_type=F32))


def _lane_pick(tile, lane, idx):
    return jnp.sum(jnp.where(lane == idx, tile, 0.0), axis=1, keepdims=True)


def _rms(x):
    return x * lax.rsqrt(jnp.mean(x * x, axis=-1, keepdims=True) + EPS)


def _to_rows(x3):
    r = x3.shape[0]
    xt = jnp.swapaxes(x3.reshape(r // ROW_TILE, ROW_TILE, ROW_TILE, LANES), 1, 2)
    return jnp.concatenate([xt[:, s].reshape(r, LANES) for s in range(ROW_TILE)], axis=1)


def _to_row_tiles(x):
    r = x.shape[0]
    xt = jnp.stack([x[:, s * LANES:(s + 1) * LANES].reshape(r // ROW_TILE, ROW_TILE, LANES) for s in range(ROW_TILE)],
                   axis=1)
    return jnp.swapaxes(xt, 1, 2).reshape(r, ROW_TILE, LANES)


def _ada_kernel(c_ref, w_ref, b_ref, o_ref):
    ca = _silu(c_ref[...])
    o_ref[...] = _dot3(ca, w_ref[...]) + b_ref[...]


def _ada(c_pad, w, b):
    rows, d = c_pad.shape
    n = w.shape[1]
    tn = 1024
    return pl.pallas_call(
        _ada_kernel,
        grid=(n // tn,),
        in_specs=[pl.BlockSpec((rows, d), lambda j: (0, 0)),
                  pl.BlockSpec((d, tn), lambda j: (0, j)),
                  pl.BlockSpec((1, tn), lambda j: (0, j))],
        out_specs=pl.BlockSpec((rows, tn), lambda j: (0, j)),
        out_shape=jax.ShapeDtypeStruct((rows, n), F32),
        compiler_params=pltpu.CompilerParams(dimension_semantics=("arbitrary",),
                                             vmem_limit_bytes=VMEM_LIMIT_BYTES),
        name="ada",
    )(c_pad, w, b)


def _inproj_kernel(layer, x_ref, n1_ref, sc_ref, sh_ref, w_ref, cw_ref, alog_ref, dt_ref, lb_ref, tri_ref,
                   qa_ref, ka_ref, va_ref, ga_ref, sm_ref, b_ref, kb_ref, ib_ref, qb_ref, gb_ref,
                   pbuf):
    tt = x_ref.shape[1]
    t = pl.program_id(1)

    h = _rms(x_ref[0]) * n1_ref[...]
    h = h * (1.0 + sc_ref[0]) + sh_ref[0]
    hb = h.astype(BF16)

    def proj(g, width=GW):
        return jnp.dot(hb, w_ref[:, g * GW:g * GW + width], preferred_element_type=F32)

    @pl.when(t == 0)
    def _():
        pbuf[:, 0:8, :] = jnp.zeros((3, 8, GW), F32)

    for g, out_ref in enumerate((qa_ref, ka_ref, va_ref)):
        cols = slice(g * GW, (g + 1) * GW)
        p = proj(g)
        pbuf[g, 8:8 + tt, :] = p
        y = p * cw_ref[CONV_K - 1:CONV_K, cols]
        for j in range(1, CONV_K):
            y = y + pbuf[g, 8 - j:8 - j + tt, :] * cw_ref[CONV_K - 1 - j:CONV_K - j, cols]
        pbuf[g, 0:8, :] = pbuf[g, tt:tt + 8, :]
        y = _silu(y)
        if g == 2:
            out_ref[0] = y.astype(out_ref.dtype)
        else:
            scale = DH ** -0.5 if g == 0 else 1.0
            for hh in range(HEADS):
                hs = slice(hh * DH, (hh + 1) * DH)
                yh = y[:, hs]
                inv = lax.rsqrt(jnp.sum(yh * yh, axis=-1, keepdims=True) + EPS)
                out_ref[0, :, hs] = (yh * inv * scale).astype(out_ref.dtype)

    ga_ref[0] = _silu(proj(3)).astype(ga_ref.dtype)

    ps = proj(8, GATE_LANES)
    lane = lax.broadcasted_iota(jnp.int32, ps.shape, 1)
    beta = jax.nn.sigmoid(ps)
    z = ps + dt_ref[...]
    softplus = jnp.maximum(z, 0.0) + jnp.log1p(jnp.exp(-jnp.abs(z)))
    g_log = -jnp.exp(alog_ref[...]) * softplus
    tri = tri_ref[...]
    gc = _cumsum_rows(tri, jnp.where((lane >= HEADS) & (lane < 2 * HEADS), g_log, 0.0))
    sm_ref[0] = jnp.where(lane < HEADS, beta, gc)

    hl = lb_ref[...]
    e = jnp.exp(hl - jnp.max(hl, axis=0, keepdims=True))
    lb = jnp.sum(e[0:layer + 1], axis=0, keepdims=True) / jnp.sum(e, axis=0, keepdims=True)
    fr = proj(4)
    logf = jnp.log(lb + (1.0 - lb) * jax.nn.sigmoid(fr))
    b_ref[0] = _cumsum_rows(tri, logf)
    kb_ref[0] = ((1.0 - lb) * jax.nn.sigmoid(-fr)).astype(kb_ref.dtype)
    ib_ref[0] = proj(5).astype(ib_ref.dtype)
    qb_ref[0] = _silu(proj(6)).astype(qb_ref.dtype)
    gb_ref[0] = _silu(proj(7)).astype(gb_ref.dtype)


def _inproj(layer, x, n1, sc1, sh1, w_all, conv_w, alog_pad, dt_pad, hg_lb, tri, tt):
    B, T, D = x.shape
    const = lambda shape: pl.BlockSpec(shape, lambda b, t: (0,) * len(shape))
    act = lambda dt: jax.ShapeDtypeStruct((B, T, GW), dt)
    tile = lambda w: pl.BlockSpec((1, tt, w), lambda b, t: (b, t, 0))
    per_batch = pl.BlockSpec((1, 1, D), lambda b, t: (b, 0, 0))
    return pl.pallas_call(
        functools.partial(_inproj_kernel, layer),
        grid=(B, T // tt),
        in_specs=[tile(D), const((1, D)), per_batch, per_batch,
                  const(w_all.shape), const(conv_w.shape), const((1, GATE_LANES)), const((1, GATE_LANES)),
                  const(hg_lb.shape), const((tt, tt))],
        out_specs=[tile(GW), tile(GW), tile(GW), tile(GW), tile(GATE_LANES), tile(GW),
                   tile(GW), tile(GW), tile(GW), tile(GW)],
        out_shape=[act(ACT), act(ACT), act(ACT), act(ACT),
                   jax.ShapeDtypeStruct((B, T, GATE_LANES), F32), act(F32),
                   act(ACT), act(ACT), act(ACT), act(ACT)],
        scratch_shapes=[pltpu.VMEM((3, tt + 8, GW), F32)],
        compiler_params=pltpu.CompilerParams(dimension_semantics=("arbitrary", "arbitrary"),
                                             vmem_limit_bytes=VMEM_LIMIT_BYTES),
        name="inproj",
    )(x, n1, sc1, sh1, w_all, conv_w, alog_pad, dt_pad, hg_lb, tri)


def _gdn_prep_kernel(q_ref, k_ref, v_ref, sm_ref, gct_ref, o_ref, qt_ref, m_ref, n_ref):
    tt = q_ref.shape[1]
    nc = tt // CHUNK
    row = lax.broadcasted_iota(jnp.int32, (CHUNK, CHUNK), 0)
    col = lax.broadcasted_iota(jnp.int32, (CHUNK, CHUNK), 1)
    causal = row >= col
    strict = row > col
    eye = jnp.where(row == col, 1.0, 0.0)
    lane = lax.broadcasted_iota(jnp.int32, (CHUNK, GATE_LANES), 1)

    def body(i, carry):
        chains = [(GDN_CHUNKS_PER_ITER * i + j, hh) for j in range(GDN_CHUNKS_PER_ITER) for hh in range(HEADS)]
        rows = [pl.ds(pl.multiple_of(c * CHUNK, CHUNK), CHUNK) for c, _ in chains]
        hs = [slice(hh * DH, (hh + 1) * DH) for _, hh in chains]
        n = range(len(chains))
        sm = [sm_ref[0, rows[j], :] for j in n]
        q = [q_ref[0, rows[j], hs[j]].astype(F32) for j in n]
        k = [k_ref[0, rows[j], hs[j]].astype(F32) for j in n]
        v = [v_ref[0, rows[j], hs[j]].astype(F32) for j in n]
        beta = [_lane_pick(sm[j], lane, chains[j][1]) for j in n]
        gcol = [_lane_pick(sm[j], lane, HEADS + chains[j][1]) for j in n]
        grow = [gct_ref[0, hh, pl.ds(c, 1), :] for c, hh in chains]
        decay = [jnp.exp(jnp.where(causal, gcol[j] - grow[j], -jnp.inf)) for j in n]
        kb = [k[j] * beta[j] for j in n]
        L = [jnp.where(strict, _dot_nt(kb[j], k[j]) * decay[j], 0.0) for j in n]
        tinv = [eye - L[j] for j in n]
        pw = [_dot3(L[j], L[j]) for j in n]
        for _ in range(4):
            tinv = [tinv[j] + _dot3(tinv[j], pw[j]) for j in n]
            pw = [_dot3(pw[j], pw[j]) for j in n]
        tinv = [tinv[j] + _dot3(tinv[j], pw[j]) for j in n]
        eg = [jnp.exp(gcol[j]) for j in n]
        sol = [_dot(tinv[j], jnp.concatenate([v[j] * beta[j], kb[j] * eg[j]], axis=1)) for j in n]
        attn = [_dot_nt(q[j], k[j]) * decay[j] for j in n]
        k_tail = [k[j] * jnp.exp(gcol[j][CHUNK - 1:CHUNK, :] - gcol[j]) for j in n]
        au = [_dot(attn[j], sol[j]) for j in n]
        ku = [_dot_tn(k_tail[j], sol[j]) for j in n]
        for j, (c, hh) in enumerate(chains):
            o_ref[0, rows[j], hs[j]] = au[j][:, :DH]
            qt_ref[0, rows[j], hs[j]] = (q[j] * eg[j] - au[j][:, DH:]).astype(qt_ref.dtype)
            n_ref[0, hh, c] = ku[j][:, :DH].astype(n_ref.dtype)
            m_ref[0, hh, c] = (-ku[j][:, DH:]).astype(m_ref.dtype)
        return carry

    lax.fori_loop(0, nc // GDN_CHUNKS_PER_ITER, body, 0)


def _gdn_scan_kernel(o_ref, qt_ref, m_ref, n_ref, gct_ref, sg_ref, nw_ref, out_ref, s_ref):
    nb, tt = o_ref.shape[0], o_ref.shape[1]
    nc = tt // CHUNK

    @pl.when(pl.program_id(0) == 0)
    def _():
        s_ref[...] = jnp.zeros(s_ref.shape, F32)

    nw = nw_ref[...]

    def body(c, carry):
        rows = pl.ds(pl.multiple_of(c * CHUNK, CHUNK), CHUNK)
        for b in range(nb):
            for hh in range(HEADS):
                hs = slice(hh * DH, (hh + 1) * DH)
                S = s_ref[b, hh]
                Sb = S.astype(BF16)
                glast = gct_ref[b, hh, pl.ds(c, 1), :][:, CHUNK - 1:CHUNK]
                o = o_ref[b, rows, hs] + jnp.dot(qt_ref[b, rows, hs], Sb, preferred_element_type=F32)
                s_ref[b, hh] = (S * jnp.exp(glast) + jnp.dot(m_ref[b, hh, c], Sb, preferred_element_type=F32)
                                + n_ref[b, hh, c].astype(F32))
                o = _rms(o) * nw * sg_ref[b, rows, hs].astype(F32)
                out_ref[b, rows, hs] = o.astype(out_ref.dtype)
        return carry

    lax.fori_loop(0, nc, body, 0)


def _gdn(q, k, v, sg, sm, gct, nw, tt):
    B, T, _ = q.shape
    nc = tt // CHUNK
    assert nc % GDN_CHUNKS_PER_ITER == 0
    n_chunks = T // CHUNK
    tile = lambda w: pl.BlockSpec((1, tt, w), lambda b, t: (b, t, 0))
    mat = jax.ShapeDtypeStruct((B, HEADS, n_chunks, DH, DH), ACT)
    o_part, qt, m, n = pl.pallas_call(
        _gdn_prep_kernel,
        grid=(B, T // tt),
        in_specs=[tile(GW), tile(GW), tile(GW), tile(GATE_LANES),
                  pl.BlockSpec((1, HEADS, nc, CHUNK), lambda b, t: (b, 0, t, 0))],
        out_specs=[tile(GW), tile(GW),
                   pl.BlockSpec((1, HEADS, nc, DH, DH), lambda b, t: (b, 0, t, 0, 0)),
                   pl.BlockSpec((1, HEADS, nc, DH, DH), lambda b, t: (b, 0, t, 0, 0))],
        out_shape=[jax.ShapeDtypeStruct((B, T, GW), F32), jax.ShapeDtypeStruct((B, T, GW), ACT), mat, mat],
        compiler_params=pltpu.CompilerParams(dimension_semantics=("arbitrary", "arbitrary"),
                                             vmem_limit_bytes=VMEM_LIMIT_BYTES),
        name="gdn_prep",
    )(q, k, v, sm, gct)

    full = lambda w: pl.BlockSpec((B, tt, w), lambda t: (0, t, 0))
    mats = pl.BlockSpec((B, HEADS, nc, DH, DH), lambda t: (0, 0, t, 0, 0))
    return pl.pallas_call(
        _gdn_scan_kernel,
        grid=(T // tt,),
        in_specs=[full(GW), full(GW), mats, mats,
                  pl.BlockSpec((B, HEADS, nc, CHUNK), lambda t: (0, 0, t, 0)),
                  full(GW), pl.BlockSpec((1, DH), lambda t: (0, 0))],
        out_specs=full(GW),
        out_shape=jax.ShapeDtypeStruct((B, T, GW), ACT),
        scratch_shapes=[pltpu.VMEM((B, HEADS, DH, DH), F32)],
        compiler_params=pltpu.CompilerParams(dimension_semantics=("arbitrary",),
                                             vmem_limit_bytes=VMEM_LIMIT_BYTES),
        name="gdn_scan",
    )(o_part, qt, m, n, gct, sg, nw)


def _hgrn_kernel(q_ref, k_ref, v_ref, b_ref, sg_ref, nw_ref, o_ref, st_ref):
    tt = q_ref.shape[1]
    nc = tt // CHUNK

    @pl.when(pl.program_id(1) == 0)
    def _():
        st_ref[...] = jnp.zeros(st_ref.shape, F32)

    row = lax.broadcasted_iota(jnp.int32, (CHUNK, CHUNK), 0)
    col = lax.broadcasted_iota(jnp.int32, (CHUNK, CHUNK), 1)
    diag_block = ((col // SUB) == (row // SUB)) & (col <= row)
    heads = range(HEADS)
    hs = [slice(hh * DH, (hh + 1) * DH) for hh in heads]

    def body(c, carry):
        rows = pl.ds(pl.multiple_of(c * CHUNK, CHUNK), CHUNK)
        q = [q_ref[0, rows, hs[h]].astype(F32) for h in heads]
        k = [k_ref[0, rows, hs[h]].astype(F32) for h in heads]
        v = [v_ref[0, rows, hs[h]].astype(F32) for h in heads]
        b = [b_ref[0, rows, hs[h]] for h in heads]
        blast = [b[h][CHUNK - 1:CHUNK, :] for h in heads]
        st = [st_ref[h] for h in heads]
        o = [_dot_nt(q[h] * jnp.exp(b[h]), st[h]) for h in heads]
        k_tail = [k[h] * jnp.exp(blast[h] - b[h]) for h in heads]
        for h in heads:
            st_ref[h] = st[h] * jnp.exp(blast[h]) + _dot_tn(v[h], k_tail[h])

        blocks = [[jnp.zeros((SUB, CHUNK), F32)] for _ in heads]
        for i in range(1, CHUNK // SUB):
            lo, hi = i * SUB, (i + 1) * SUB
            for h in heads:
                r = b[h][lo:lo + 1, :]
                qi = q[h][lo:hi] * jnp.exp(b[h][lo:hi] - r)
                kj = k[h][:lo] * jnp.exp(jnp.minimum(r - b[h][:lo], 0.0))
                kj = jnp.concatenate([kj, jnp.zeros((CHUNK - lo, DH), F32)], axis=0)
                blocks[h].append(_dot_nt(qi, kj))
        a = []
        for h in heads:
            f = jnp.exp(jnp.minimum(b[h] - pltpu.roll(b[h], 1, 0), 0.0))
            e = None
            a_diag = jnp.zeros((CHUNK, CHUNK), F32)
            for delta in range(SUB):
                if delta == 0:
                    term = q[h] * k[h]
                else:
                    fsh = f if delta == 1 else pltpu.roll(f, delta - 1, 0)
                    e = fsh if e is None else e * fsh
                    term = q[h] * pltpu.roll(k[h], delta, 0) * e
                colv = jnp.sum(term, axis=1, keepdims=True)
                a_diag = jnp.where(row - col == delta, colv, a_diag)
            a.append(jnp.where(diag_block, a_diag, jnp.concatenate(blocks[h], axis=0)))

        o = [o[h] + _dot(a[h], v[h]) for h in heads]
        o = jnp.concatenate(o, axis=1)
        o = _rms(o) * nw_ref[...] * sg_ref[0, rows, :].astype(F32)
        o_ref[0, rows, :] = o.astype(o_ref.dtype)
        return carry

    lax.fori_loop(0, nc, body, 0)


def _hgrn(q, k, v, b, sg, nw, tt):
    B, T, _ = q.shape
    tile = pl.BlockSpec((1, tt, GW), lambda bi, t: (bi, t, 0))
    return pl.pallas_call(
        _hgrn_kernel,
        grid=(B, T // tt),
        in_specs=[tile, tile, tile, tile, tile, pl.BlockSpec((1, GW), lambda bi, t: (0, 0))],
        out_specs=tile,
        out_shape=jax.ShapeDtypeStruct((B, T, GW), ACT),
        scratch_shapes=[pltpu.VMEM((HEADS, DH, DH), F32)],
        compiler_params=pltpu.CompilerParams(dimension_semantics=("arbitrary", "arbitrary"),
                                             vmem_limit_bytes=VMEM_LIMIT_BYTES),
        name="hgrn",
    )(q, k, v, b, sg, nw)


def _outproj_kernel(oa_ref, ob_ref, x_ref, wa_ref, wb_ref, g1_ref, n2_ref, sc_ref, sh_ref, wr_ref, rb_ref,
                    x1_ref, h2_ref, eidx_ref, rank_ref, wts_ref, cnt_ref):
    tm = x_ref.shape[1]

    @pl.when((pl.program_id(0) == 0) & (pl.program_id(1) == 0))
    def _():
        cnt_ref[...] = jnp.zeros(cnt_ref.shape, F32)

    mix = (jnp.dot(oa_ref[0], wa_ref[...], preferred_element_type=F32)
           + jnp.dot(ob_ref[0], wb_ref[...], preferred_element_type=F32))
    x1 = x_ref[0] + g1_ref[0] * mix
    x1_ref[0] = x1
    h2 = _rms(x1) * n2_ref[...]
    h2 = h2 * (1.0 + sc_ref[0]) + sh_ref[0]
    h2_ref[...] = _to_row_tiles(h2)

    scores = jax.nn.sigmoid(_dot3(wr_ref[...], h2, dot=_dot_nt))
    sel = scores + rb_ref[...]
    sub = lax.broadcasted_iota(jnp.int32, (GROUP_SIZE, tm), 0)
    neg = -jnp.inf
    groups = range(N_GROUPS)

    def take_max(blk):
        m = jnp.max(blk, axis=0, keepdims=True)
        first = jnp.min(jnp.where(blk == m, sub, GROUP_SIZE), axis=0, keepdims=True)
        hit = sub == first
        return m, hit, jnp.where(hit, neg, blk)

    blk_of = lambda a, g: a[g * GROUP_SIZE:(g + 1) * GROUP_SIZE]
    sel_blk = [blk_of(sel, g) for g in groups]
    group_score = jnp.zeros((N_GROUPS, tm), F32)
    for g in groups:
        m1, _, rest = take_max(sel_blk[g])
        m2 = jnp.max(rest, axis=0, keepdims=True)
        group_score = jnp.where(sub == g, m1 + m2, group_score)
    group_on = jnp.zeros((N_GROUPS, tm), F32)
    for _ in range(TOPK_GROUPS):
        _, hit, group_score = take_max(group_score)
        group_on = jnp.where(hit, 1.0, group_on)

    cand = [jnp.where(group_on[g:g + 1] > 0.0, sel_blk[g], neg) for g in groups]
    picked = [jnp.zeros((GROUP_SIZE, tm), F32) for _ in groups]
    chosen = []
    for _ in range(TOP_K):
        m = jnp.max(functools.reduce(jnp.maximum, cand), axis=0, keepdims=True)
        first = functools.reduce(jnp.minimum, [jnp.where(cand[g] == m, sub + g * GROUP_SIZE, N_EXPERTS)
                                               for g in groups])
        first = jnp.min(first, axis=0, keepdims=True)
        chosen.append(first)
        for g in groups:
            hit = (sub + g * GROUP_SIZE) == first
            picked[g] = jnp.where(hit, 1.0, picked[g])
            cand[g] = jnp.where(hit, neg, cand[g])

    picked_all = jnp.concatenate(picked, axis=0)
    r_i = lax.broadcasted_iota(jnp.int32, (tm, tm), 0)
    c_i = lax.broadcasted_iota(jnp.int32, (tm, tm), 1)
    earlier = jnp.where(r_i < c_i, 1.0, 0.0).astype(BF16)
    before = jnp.dot(picked_all.astype(BF16), earlier, preferred_element_type=F32) + cnt_ref[:, 0:1]
    cnt_ref[...] = cnt_ref[...] + jnp.sum(picked_all, axis=1, keepdims=True)

    def pick_value(table, first):
        parts = [jnp.where((sub + g * GROUP_SIZE) == first, blk_of(table, g), 0.0) for g in groups]
        return jnp.sum(functools.reduce(jnp.add, parts), axis=0, keepdims=True)

    w_k = [pick_value(scores, f) for f in chosen]
    denom = functools.reduce(jnp.add, w_k)
    eidx = jnp.zeros((TOP_K, tm), jnp.int32)
    rank = jnp.zeros((TOP_K, tm), jnp.int32)
    wts = jnp.zeros((TOP_K, tm), F32)
    for k in range(TOP_K):
        eidx = jnp.where(sub == k, chosen[k], eidx)
        rank = jnp.where(sub == k, pick_value(before, chosen[k]).astype(jnp.int32), rank)
        wts = jnp.where(sub == k, w_k[k] / denom * ROUTE_SCALE, wts)
    eidx_ref[...] = eidx
    rank_ref[...] = rank
    pad = jnp.zeros((GATE_LANES - TOP_K, tm), F32)
    wts_ref[0] = jnp.concatenate([wts, pad], axis=0).T


def _outproj(oa, ob, x, wa, wb, g1, n2, sc2, sh2, wr_t, rb, tm):
    B, T, D = x.shape
    nt = T // tm
    const = lambda shape: pl.BlockSpec(shape, lambda b, t: (0,) * len(shape))
    tile = lambda w: pl.BlockSpec((1, tm, w), lambda b, t: (b, t, 0))
    per_batch = pl.BlockSpec((1, 1, D), lambda b, t: (b, 0, 0))
    picks = pl.BlockSpec((TOP_K, tm), lambda b, t: (0, b * nt + t))
    return pl.pallas_call(
        _outproj_kernel,
        grid=(B, nt),
        in_specs=[tile(GW), tile(GW), tile(D), const(wa.shape), const(wb.shape), per_batch,
                  const((1, D)), per_batch, per_batch, const(wr_t.shape), const(rb.shape)],
        out_specs=[tile(D), pl.BlockSpec((tm, ROW_TILE, LANES), lambda b, t: (b * nt + t, 0, 0)),
                   picks, picks, tile(GATE_LANES), const((N_EXPERTS, GATE_LANES))],
        out_shape=[jax.ShapeDtypeStruct((B, T, D), F32), jax.ShapeDtypeStruct((B * T, ROW_TILE, LANES), F32),
                   jax.ShapeDtypeStruct((TOP_K, B * T), jnp.int32), jax.ShapeDtypeStruct((TOP_K, B * T), jnp.int32),
                   jax.ShapeDtypeStruct((B, T, GATE_LANES), F32),
                   jax.ShapeDtypeStruct((N_EXPERTS, GATE_LANES), F32)],
        compiler_params=pltpu.CompilerParams(dimension_semantics=("arbitrary", "arbitrary"),
                                             vmem_limit_bytes=VMEM_LIMIT_BYTES),
        name="outproj",
    )(oa, ob, x, wa, wb, g1, n2, sc2, sh2, wr_t, rb)


def _slots_kernel(off_ref, eidx_ref, rank_ref, slot_ref):
    eidx = eidx_ref[...]

    def add_expert(e, acc):
        return acc + jnp.where(eidx == e, off_ref[e], 0)

    slot_ref[...] = lax.fori_loop(0, N_EXPERTS, add_expert, rank_ref[...])


def _slots(offsets, eidx, rank):
    k, m = eidx.shape
    tile = pl.BlockSpec((k, m), lambda i: (0, 0))
    return pl.pallas_call(
        _slots_kernel,
        grid=(1,),
        in_specs=[pl.BlockSpec(memory_space=pltpu.SMEM), tile, tile],
        out_specs=tile,
        out_shape=jax.ShapeDtypeStruct((k, m), jnp.int32),
        compiler_params=pltpu.CompilerParams(dimension_semantics=("arbitrary",),
                                             vmem_limit_bytes=VMEM_LIMIT_BYTES),
        name="slots",
    )(offsets, eidx, rank)


def _dispatch_kernel(slot_ref, h_ref, xs_ref, sem):
    tmd = h_ref.shape[0]

    def start_rows(j, c):
        for k in range(TOP_K):
            pltpu.make_async_copy(h_ref.at[pl.ds(j, 1)], xs_ref.at[pl.ds(slot_ref[k, j], 1)], sem).start(
                priority=k % 2)
        return c

    lax.fori_loop(0, tmd, start_rows, 0)
    for _ in range(TOP_K):
        pltpu.make_async_copy(h_ref, xs_ref.at[pl.ds(0, tmd)], sem).wait()


def _dispatch(slot, h2, n_rows, tmd):
    M = h2.shape[0]
    return pl.pallas_call(
        _dispatch_kernel,
        grid=(M // tmd,),
        in_specs=[pl.BlockSpec((TOP_K, tmd), lambda i: (0, i), memory_space=pltpu.SMEM),
                  pl.BlockSpec((tmd, ROW_TILE, LANES), lambda i: (i, 0, 0))],
        out_specs=pl.BlockSpec(memory_space=pl.ANY),
        out_shape=jax.ShapeDtypeStruct((n_rows, ROW_TILE, LANES), F32),
        scratch_shapes=[pltpu.SemaphoreType.DMA],
        compiler_params=pltpu.CompilerParams(dimension_semantics=("arbitrary",),
                                             vmem_limit_bytes=VMEM_LIMIT_BYTES),
        name="dispatch",
    )(slot, h2)


def _ffn_kernel(te_ref, nu_ref, nv_ref, x_ref, wg_ref, wu_ref, wd_ref, y_ref, wgu_s, wd_s):
    i = pl.program_id(0)

    @pl.when(i < nu_ref[0])
    def _():
        @pl.when((i == 0) | (te_ref[i] != te_ref[jnp.maximum(i - 1, 0)]))
        def _():
            wgu_s[:, :D_EXPERT] = wg_ref[0].astype(BF16)
            wgu_s[:, D_EXPERT:] = wu_ref[0].astype(BF16)
            wd_s[...] = wd_ref[0].astype(BF16)

        valid = lax.broadcasted_iota(jnp.int32, x_ref.shape, 0) < nv_ref[i]
        x = _to_rows(jnp.where(valid, x_ref[...], 0.0))
        gu = jnp.dot(x.astype(BF16), wgu_s[...], preferred_element_type=F32)
        act = _silu(gu[:, :D_EXPERT]) * gu[:, D_EXPERT:]
        y = jnp.dot(act.astype(BF16), wd_s[...], preferred_element_type=F32)
        y_ref[...] = jnp.where(valid, _to_row_tiles(y), 0.0)


def _ffn(tile_expert, n_used, n_valid, xs, wg, wu, wd):
    D = wg.shape[1]
    n_tiles = xs.shape[0] // MOE_ROWS
    rows = pl.BlockSpec((MOE_ROWS, ROW_TILE, LANES), lambda i, te, nu, nv: (jnp.minimum(i, nu[0] - 1), 0, 0))
    return pl.pallas_call(
        _ffn_kernel,
        grid_spec=pltpu.PrefetchScalarGridSpec(
            num_scalar_prefetch=3,
            grid=(n_tiles,),
            in_specs=[rows,
                      pl.BlockSpec((1, D, D_EXPERT), lambda i, te, nu, nv: (te[i], 0, 0)),
                      pl.BlockSpec((1, D, D_EXPERT), lambda i, te, nu, nv: (te[i], 0, 0)),
                      pl.BlockSpec((1, D_EXPERT, D), lambda i, te, nu, nv: (te[i], 0, 0))],
            out_specs=rows,
            scratch_shapes=[pltpu.VMEM((D, 2 * D_EXPERT), BF16), pltpu.VMEM((D_EXPERT, D), BF16)]),
        out_shape=jax.ShapeDtypeStruct(xs.shape, F32),
        compiler_params=pltpu.CompilerParams(dimension_semantics=("arbitrary",),
                                             vmem_limit_bytes=VMEM_LIMIT_BYTES),
        name="ffn",
    )(tile_expert, n_used, n_valid, xs, wg, wu, wd)


def _combine_kernel(slot_ref, next_slot_ref, w_ref, h_ref, x1_ref, g2_ref, fw_ref, wsgu_ref, wsd_ref, y_ref,
                    o_ref, buf, sem):
    tmc = h_ref.shape[0]
    i = pl.program_id(0)
    last = pl.num_programs(0) - 1

    def start_rows(slots, b):
        def body(j, c):
            for k in range(TOP_K):
                pltpu.make_async_copy(y_ref.at[pl.ds(slots[k, j], 1)], buf.at[b, k, pl.ds(j, 1)], sem.at[b]).start(
                    priority=k % 2)
            return c
        lax.fori_loop(0, tmc, body, 0)

    def wait_rows(b):
        for k in range(TOP_K):
            pltpu.make_async_copy(y_ref.at[pl.ds(0, tmc)], buf.at[b, k], sem.at[b]).wait()

    def for_buffer(step, fn):
        for b in range(2):
            @pl.when(step % 2 == b)
            def _():
                fn(b)

    @pl.when(i == 0)
    def _():
        start_rows(slot_ref, 0)

    @pl.when(i < last)
    def _():
        for_buffer(i + 1, lambda b: start_rows(next_slot_ref, b))

    gu = jnp.dot(_to_rows(h_ref[...]).astype(BF16), wsgu_ref[...], preferred_element_type=F32)
    act = _silu(gu[:, :D_EXPERT]) * gu[:, D_EXPERT:]
    shared = jnp.dot(act.astype(BF16), wsd_ref[...], preferred_element_type=F32)
    w = w_ref[...]
    lane = lax.broadcasted_iota(jnp.int32, w.shape, 1)

    def finish(b):
        wait_rows(b)
        routed = jnp.zeros((tmc, ROW_TILE, LANES), F32)
        for k in range(TOP_K):
            wk = jnp.broadcast_to(_lane_pick(w, lane, k), (tmc, LANES))
            routed = routed + _to_row_tiles(jnp.concatenate([wk] * ROW_TILE, axis=1)) * buf[b, k]
        acc = shared + _to_rows(routed)
        y = x1_ref[...] + g2_ref[0] * acc
        o_ref[...] = _rms(y) * fw_ref[...]

    for_buffer(i, finish)


def _combine(slot, wts, h2, x1, g2, fw, wsgu, wsd, ys, T, tmc):
    M, D = x1.shape
    const = lambda shape: pl.BlockSpec(shape, lambda i: (0,) * len(shape))
    tile = lambda w: pl.BlockSpec((tmc, w), lambda i: (i, 0))
    row_tiled = pl.BlockSpec((tmc, ROW_TILE, LANES), lambda i: (i, 0, 0))
    n_steps = M // tmc
    return pl.pallas_call(
        _combine_kernel,
        grid=(n_steps,),
        in_specs=[pl.BlockSpec((TOP_K, tmc), lambda i: (0, i), memory_space=pltpu.SMEM),
                  pl.BlockSpec((TOP_K, tmc), lambda i: (0, jnp.minimum(i + 1, n_steps - 1)), memory_space=pltpu.SMEM),
                  tile(GATE_LANES), row_tiled, tile(D),
                  pl.BlockSpec((1, 1, D), lambda i: (i // (T // tmc), 0, 0)),
                  const((1, D)), const(wsgu.shape), const(wsd.shape),
                  pl.BlockSpec(memory_space=pl.ANY)],
        out_specs=tile(D),
        out_shape=jax.ShapeDtypeStruct((M, D), F32),
        scratch_shapes=[pltpu.VMEM((2, TOP_K, tmc, ROW_TILE, LANES), F32), pltpu.SemaphoreType.DMA((2,))],
        compiler_params=pltpu.CompilerParams(dimension_semantics=("arbitrary",),
                                             vmem_limit_bytes=VMEM_LIMIT_BYTES),
        name="combine",
    )(slot, slot, wts, h2, x1, g2, fw, wsgu, wsd, ys)


def _pick_tile(n, want):
    t = min(n, want)
    assert n % t == 0 and t % CHUNK == 0, (n, want)
    return t


def kernel(x, c, w_ada, b_ada, norm1_w, w_in, conv_w, gdn_a_log, gdn_dt_bias, gdn_norm_w, hg_lb, hg_norm_w,
           w_out, norm2_w, w_router, router_bias, w_gate, w_up, w_down, ws_gate, ws_up, ws_down, final_norm_w):
    B, T, D = x.shape
    M = B * T
    depth = w_ada.shape[0]
    assert depth == 1 and T % CHUNK == 0 and B <= 8
    layer = 0
    tt = _pick_tile(T, 512)

    c_pad = jnp.pad(c, ((0, 8 - B), (0, 0)))
    mod = _ada(c_pad, w_ada[layer], b_ada[layer].reshape(1, -1))[:B]
    sh1, sc1, g1, sh2, sc2, g2 = (m.reshape(B, 1, D) for m in jnp.split(mod, 6, axis=-1))

    w = w_in[layer]
    qkv_w = 3 * GW
    sizes = (GW, HEADS, HEADS, GW, GW, GW, GW)
    offs = [qkv_w]
    for s in sizes:
        offs.append(offs[-1] + s)
    seg = lambda i: w[:, offs[i]:offs[i + 1]]
    small = jnp.pad(jnp.concatenate([seg(1), seg(2)], axis=1), ((0, 0), (0, GATE_LANES - 2 * HEADS)))
    w_all = jnp.concatenate([w[:, :qkv_w], seg(0), seg(3), seg(4), seg(5), seg(6), small], axis=1).astype(BF16)
    lane_pad = lambda v: jnp.pad(v.astype(F32).reshape(1, HEADS), ((0, 0), (HEADS, GATE_LANES - 2 * HEADS)))
    idx = jnp.arange(tt)
    tri = ((idx[:, None] >= idx[None, :]) & (idx[:, None] // CHUNK == idx[None, :] // CHUNK)).astype(BF16)

    qa, ka, va, ga, sm, bcum, kb, ib, qb, gb = _inproj(
        layer, x, norm1_w[layer].reshape(1, D), sc1, sh1, w_all, conv_w[layer].astype(F32),
        lane_pad(gdn_a_log[layer]), lane_pad(gdn_dt_bias[layer]), hg_lb.astype(F32), tri, tt)

    gct = sm[:, :, HEADS:2 * HEADS].transpose(0, 2, 1).reshape(B, HEADS, T // CHUNK, CHUNK)
    oa = _gdn(qa, ka, va, ga, sm, gct, gdn_norm_w[layer].reshape(1, DH), tt)
    ob = _hgrn(qb, kb, ib, bcum, gb, hg_norm_w[layer].reshape(1, GW), tt)

    wo = w_out[layer].astype(BF16)
    x1, h2, eidx, rank, wts, cnt = _outproj(oa, ob, x, wo[:GW], wo[GW:], g1, norm2_w[layer].reshape(1, D), sc2, sh2,
                                            w_router[layer].T, router_bias[layer].reshape(N_EXPERTS, 1), tt)

    counts = cnt[:, 0].astype(jnp.int32)
    padded = (counts + MOE_ROWS - 1) // MOE_ROWS * MOE_ROWS
    ends = jnp.cumsum(padded)
    offsets = ends - padded
    n_tiles = (M * TOP_K) // MOE_ROWS + N_EXPERTS
    n_used = (ends[-1] // MOE_ROWS).astype(jnp.int32)
    tile_ids = jnp.minimum(jnp.arange(n_tiles, dtype=jnp.int32), n_used - 1)
    tile_expert = jnp.sum(((ends // MOE_ROWS)[None, :] <= tile_ids[:, None]).astype(jnp.int32), axis=1)
    tile_expert = jnp.minimum(tile_expert, N_EXPERTS - 1)
    slot = _slots(offsets.astype(jnp.int32), eidx, rank)
    first_tile = jnp.take(offsets, tile_expert) // MOE_ROWS
    n_valid = jnp.clip(jnp.take(counts, tile_expert) - (tile_ids - first_tile) * MOE_ROWS, 0, MOE_ROWS)

    assert D == ROW_TILE * LANES
    xs = _dispatch(slot, h2, n_tiles * MOE_ROWS, _pick_tile(M, 512))
    ys = _ffn(tile_expert.astype(jnp.int32), n_used.reshape(1), n_valid.astype(jnp.int32), xs,
              w_gate[layer], w_up[layer], w_down[layer])
    wsgu = jnp.concatenate([ws_gate[layer], ws_up[layer]], axis=-1).astype(BF16)
    out = _combine(slot, wts.reshape(M, GATE_LANES), h2, x1.reshape(M, D), g2, final_norm_w.reshape(1, D),
                   wsgu, ws_down[layer].astype(BF16), ys, T, _pick_tile(T, 128))
    return out.reshape(B, T, D)
```

```python
import functools

import jax
import jax.numpy as jnp
from jax import lax
from jax.experimental import pallas as pl
from jax.experimental.pallas import tpu as pltpu

F32 = jnp.float32
BF16 = jnp.bfloat16

EPS = 1e-6
CHUNK = 64
SUB = 8
HEADS = 4
DH = 128
GW = HEADS * DH
CONV_K = 4
N_EXPERTS = 64
N_GROUPS = 8
GROUP_SIZE = N_EXPERTS // N_GROUPS
TOPK_GROUPS = 4
TOP_K = 8
D_EXPERT = 256
ROUTE_SCALE = 2.5
GATE_LANES = 128
GDN_CHUNKS_PER_ITER = 2
MOE_ROWS = 1024
ROW_TILE, LANES = 8, 128

VMEM_LIMIT_BYTES = 56 * 1024 * 1024

ACT = BF16


def _silu(x):
    return x * jax.nn.sigmoid(x)


def _dot(a, b):
    return jnp.dot(a.astype(BF16), b.astype(BF16), preferred_element_type=F32)


def _dot_nt(a, b):
    return lax.dot_general(a.astype(BF16), b.astype(BF16), (((1,), (1,)), ((), ())),
                           preferred_element_type=F32)


def _dot_tn(a, b):
    return lax.dot_general(a.astype(BF16), b.astype(BF16), (((0,), (0,)), ((), ())),
                           preferred_element_type=F32)


def _split2(x):
    hi = x.astype(BF16)
    lo = (x - hi.astype(F32)).astype(BF16)
    return hi, lo


def _dot3(a, b, dot=_dot):
    ah, al = _split2(a)
    bh, bl = _split2(b)
    return dot(ah, bh) + dot(ah, bl) + dot(al, bh)


def _cumsum_rows(tri, x):
    hi = x.astype(BF16)
    r = x - hi.astype(F32)
    mid = r.astype(BF16)
    lo = (r - mid.astype(F32)).astype(BF16)
    return (jnp.dot(tri, hi, preferred_element_type=F32)
            + jnp.dot(tri, mid, preferred_element_type=F32)
            + jnp.dot(tri, lo, preferred_element_type=F32))


def _lane_pick(tile, lane, idx):
    return jnp.sum(jnp.where(lane == idx, tile, 0.0), axis=1, keepdims=True)


def _rms(x):
    return x * lax.rsqrt(jnp.mean(x * x, axis=-1, keepdims=True) + EPS)


def _to_rows(x3):
    r = x3.shape[0]
    xt = jnp.swapaxes(x3.reshape(r // ROW_TILE, ROW_TILE, ROW_TILE, LANES), 1, 2)
    return jnp.concatenate([xt[:, s].reshape(r, LANES) for s in range(ROW_TILE)], axis=1)


def _to_row_tiles(x):
    r = x.shape[0]
    xt = jnp.stack([x[:, s * LANES:(s + 1) * LANES].reshape(r // ROW_TILE, ROW_TILE, LANES) for s in range(ROW_TILE)],
                   axis=1)
    return jnp.swapaxes(xt, 1, 2).reshape(r, ROW_TILE, LANES)


def _ada_kernel(c_ref, w_ref, b_ref, o_ref):
    ca = _silu(c_ref[...])
    o_ref[...] = _dot3(ca, w_ref[...]) + b_ref[...]


def _ada(c_pad, w, b):
    rows, d = c_pad.shape
    n = w.shape[1]
    tn = 1024
    return pl.pallas_call(
        _ada_kernel,
        grid=(n // tn,),
        in_specs=[pl.BlockSpec((rows, d), lambda j: (0, 0)),
                  pl.BlockSpec((d, tn), lambda j: (0, j)),
                  pl.BlockSpec((1, tn), lambda j: (0, j))],
        out_specs=pl.BlockSpec((rows, tn), lambda j: (0, j)),
        out_shape=jax.ShapeDtypeStruct((rows, n), F32),
        compiler_params=pltpu.CompilerParams(dimension_semantics=("arbitrary",),
                                             vmem_limit_bytes=VMEM_LIMIT_BYTES),
        name="ada",
    )(c_pad, w, b)


def _inproj_kernel(layer, x_ref, n1_ref, sc_ref, sh_ref, w_ref, cw_ref, alog_ref, dt_ref, lb_ref, tri_ref,
                   qa_ref, ka_ref, va_ref, ga_ref, sm_ref, b_ref, kb_ref, ib_ref, qb_ref, gb_ref,
                   pbuf):
    tt = x_ref.shape[1]
    t = pl.program_id(1)

    h = _rms(x_ref[0]) * n1_ref[...]
    h = h * (1.0 + sc_ref[0]) + sh_ref[0]
    hb = h.astype(BF16)

    def proj(g, width=GW):
        return jnp.dot(hb, w_ref[:, g * GW:g * GW + width], preferred_element_type=F32)

    @pl.when(t == 0)
    def _():
        pbuf[:, 0:8, :] = jnp.zeros((3, 8, GW), F32)

    for g, out_ref in enumerate((qa_ref, ka_ref, va_ref)):
        cols = slice(g * GW, (g + 1) * GW)
        p = proj(g)
        pbuf[g, 8:8 + tt, :] = p
        y = p * cw_ref[CONV_K - 1:CONV_K, cols]
        for j in range(1, CONV_K):
            y = y + pbuf[g, 8 - j:8 - j + tt, :] * cw_ref[CONV_K - 1 - j:CONV_K - j, cols]
        pbuf[g, 0:8, :] = pbuf[g, tt:tt + 8, :]
        y = _silu(y)
        if g == 2:
            out_ref[0] = y.astype(out_ref.dtype)
        else:
            scale = DH ** -0.5 if g == 0 else 1.0
            for hh in range(HEADS):
                hs = slice(hh * DH, (hh + 1) * DH)
                yh = y[:, hs]
                inv = lax.rsqrt(jnp.sum(yh * yh, axis=-1, keepdims=True) + EPS)
                out_ref[0, :, hs] = (yh * inv * scale).astype(out_ref.dtype)

    ga_ref[0] = _silu(proj(3)).astype(ga_ref.dtype)

    ps = proj(8, GATE_LANES)
    lane = lax.broadcasted_iota(jnp.int32, ps.shape, 1)
    beta = jax.nn.sigmoid(ps)
    z = ps + dt_ref[...]
    softplus = jnp.maximum(z, 0.0) + jnp.log1p(jnp.exp(-jnp.abs(z)))
    g_log = -jnp.exp(alog_ref[...]) * softplus
    tri = tri_ref[...]
    gc = _cumsum_rows(tri, jnp.where((lane >= HEADS) & (lane < 2 * HEADS), g_log, 0.0))
    sm_ref[0] = jnp.where(lane < HEADS, beta, gc)

    hl = lb_ref[...]
    e = jnp.exp(hl - jnp.max(hl, axis=0, keepdims=True))
    lb = jnp.sum(e[0:layer + 1], axis=0, keepdims=True) / jnp.sum(e, axis=0, keepdims=True)
    fr = proj(4)
    logf = jnp.log(lb + (1.0 - lb) * jax.nn.sigmoid(fr))
    b_ref[0] = _cumsum_rows(tri, logf)
    kb_ref[0] = ((1.0 - lb) * jax.nn.sigmoid(-fr)).astype(kb_ref.dtype)
    ib_ref[0] = proj(5).astype(ib_ref.dtype)
    qb_ref[0] = _silu(proj(6)).astype(qb_ref.dtype)
    gb_ref[0] = _silu(proj(7)).astype(gb_ref.dtype)


def _inproj(layer, x, n1, sc1, sh1, w_all, conv_w, alog_pad, dt_pad, hg_lb, tri, tt):
    B, T, D = x.shape
    const = lambda shape: pl.BlockSpec(shape, lambda b, t: (0,) * len(shape))
    act = lambda dt: jax.ShapeDtypeStruct((B, T, GW), dt)
    tile = lambda w: pl.BlockSpec((1, tt, w), lambda b, t: (b, t, 0))
    per_batch = pl.BlockSpec((1, 1, D), lambda b, t: (b, 0, 0))
    return pl.pallas_call(
        functools.partial(_inproj_kernel, layer),
        grid=(B, T // tt),
        in_specs=[tile(D), const((1, D)), per_batch, per_batch,
                  const(w_all.shape), const(conv_w.shape), const((1, GATE_LANES)), const((1, GATE_LANES)),
                  const(hg_lb.shape), const((tt, tt))],
        out_specs=[tile(GW), tile(GW), tile(GW), tile(GW), tile(GATE_LANES), tile(GW),
                   tile(GW), tile(GW), tile(GW), tile(GW)],
        out_shape=[act(ACT), act(ACT), act(ACT), act(ACT),
                   jax.ShapeDtypeStruct((B, T, GATE_LANES), F32), act(F32),
                   act(ACT), act(ACT), act(ACT), act(ACT)],
        scratch_shapes=[pltpu.VMEM((3, tt + 8, GW), F32)],
        compiler_params=pltpu.CompilerParams(dimension_semantics=("arbitrary", "arbitrary"),
                                             vmem_limit_bytes=VMEM_LIMIT_BYTES),
        name="inproj",
    )(x, n1, sc1, sh1, w_all, conv_w, alog_pad, dt_pad, hg_lb, tri)


def _gdn_prep_kernel(q_ref, k_ref, v_ref, sm_ref, gct_ref, o_ref, qt_ref, m_ref, n_ref):
    tt = q_ref.shape[1]
    nc = tt // CHUNK
    row = lax.broadcasted_iota(jnp.int32, (CHUNK, CHUNK), 0)
    col = lax.broadcasted_iota(jnp.int32, (CHUNK, CHUNK), 1)
    causal = row >= col
    strict = row > col
    eye = jnp.where(row == col, 1.0, 0.0)
    lane = lax.broadcasted_iota(jnp.int32, (CHUNK, GATE_LANES), 1)

    def body(i, carry):
        chains = [(GDN_CHUNKS_PER_ITER * i + j, hh) for j in range(GDN_CHUNKS_PER_ITER) for hh in range(HEADS)]
        rows = [pl.ds(pl.multiple_of(c * CHUNK, CHUNK), CHUNK) for c, _ in chains]
        hs = [slice(hh * DH, (hh + 1) * DH) for _, hh in chains]
        n = range(len(chains))
        sm = [sm_ref[0, rows[j], :] for j in n]
        q = [q_ref[0, rows[j], hs[j]].astype(F32) for j in n]
        k = [k_ref[0, rows[j], hs[j]].astype(F32) for j in n]
        v = [v_ref[0, rows[j], hs[j]].astype(F32) for j in n]
        beta = [_lane_pick(sm[j], lane, chains[j][1]) for j in n]
        gcol = [_lane_pick(sm[j], lane, HEADS + chains[j][1]) for j in n]
        grow = [gct_ref[0, hh, pl.ds(c, 1), :] for c, hh in chains]
        decay = [jnp.exp(jnp.where(causal, gcol[j] - grow[j], -jnp.inf)) for j in n]
        kb = [k[j] * beta[j] for j in n]
        L = [jnp.where(strict, _dot_nt(kb[j], k[j]) * decay[j], 0.0) for j in n]
        tinv = [eye - L[j] for j in n]
        pw = [_dot3(L[j], L[j]) for j in n]
        for _ in range(4):
            tinv = [tinv[j] + _dot3(tinv[j], pw[j]) for j in n]
            pw = [_dot3(pw[j], pw[j]) for j in n]
        tinv = [tinv[j] + _dot3(tinv[j], pw[j]) for j in n]
        eg = [jnp.exp(gcol[j]) for j in n]
        sol = [_dot(tinv[j], jnp.concatenate([v[j] * beta[j], kb[j] * eg[j]], axis=1)) for j in n]
        attn = [_dot_nt(q[j], k[j]) * decay[j] for j in n]
        k_tail = [k[j] * jnp.exp(gcol[j][CHUNK - 1:CHUNK, :] - gcol[j]) for j in n]
        au = [_dot(attn[j], sol[j]) for j in n]
        ku = [_dot_tn(k_tail[j], sol[j]) for j in n]
        for j, (c, hh) in enumerate(chains):
            o_ref[0, rows[j], hs[j]] = au[j][:, :DH]
            qt_ref[0, rows[j], hs[j]] = (q[j] * eg[j] - au[j][:, DH:]).astype(qt_ref.dtype)
            n_ref[0, hh, c] = ku[j][:, :DH].astype(n_ref.dtype)
            m_ref[0, hh, c] = (-ku[j][:, DH:]).astype(m_ref.dtype)
        return carry

    lax.fori_loop(0, nc // GDN_CHUNKS_PER_ITER, body, 0)


def _gdn_scan_kernel(o_ref, qt_ref, m_ref, n_ref, gct_ref, sg_ref, nw_ref, out_ref, s_ref):
    nb, tt = o_ref.shape[0], o_ref.shape[1]
    nc = tt // CHUNK

    @pl.when(pl.program_id(0) == 0)
    def _():
        s_ref[...] = jnp.zeros(s_ref.shape, F32)

    nw = nw_ref[...]

    def body(c, carry):
        rows = pl.ds(pl.multiple_of(c * CHUNK, CHUNK), CHUNK)
        for b in range(nb):
            for hh in range(HEADS):
                hs = slice(hh * DH, (hh + 1) * DH)
                S = s_ref[b, hh]
                Sb = S.astype(BF16)
                glast = gct_ref[b, hh, pl.ds(c, 1), :][:, CHUNK - 1:CHUNK]
                o = o_ref[b, rows, hs] + jnp.dot(qt_ref[b, rows, hs], Sb, preferred_element_type=F32)
                s_ref[b, hh] = (S * jnp.exp(glast) + jnp.dot(m_ref[b, hh, c], Sb, preferred_element_type=F32)
                                + n_ref[b, hh, c].astype(F32))
                o = _rms(o) * nw * sg_ref[b, rows, hs].astype(F32)
                out_ref[b, rows, hs] = o.astype(out_ref.dtype)
        return carry

    lax.fori_loop(0, nc, body, 0)


def _gdn(q, k, v, sg, sm, gct, nw, tt):
    B, T, _ = q.shape
    nc = tt // CHUNK
    assert nc % GDN_CHUNKS_PER_ITER == 0
    n_chunks = T // CHUNK
    tile = lambda w: pl.BlockSpec((1, tt, w), lambda b, t: (b, t, 0))
    mat = jax.ShapeDtypeStruct((B, HEADS, n_chunks, DH, DH), ACT)
    o_part, qt, m, n = pl.pallas_call(
        _gdn_prep_kernel,
        grid=(B, T // tt),
        in_specs=[tile(GW), tile(GW), tile(GW), tile(GATE_LANES),
                  pl.BlockSpec((1, HEADS, nc, CHUNK), lambda b, t: (b, 0, t, 0))],
        out_specs=[tile(GW), tile(GW),
                   pl.BlockSpec((1, HEADS, nc, DH, DH), lambda b, t: (b, 0, t, 0, 0)),
                   pl.BlockSpec((1, HEADS, nc, DH, DH), lambda b, t: (b, 0, t, 0, 0))],
        out_shape=[jax.ShapeDtypeStruct((B, T, GW), F32), jax.ShapeDtypeStruct((B, T, GW), ACT), mat, mat],
        compiler_params=pltpu.CompilerParams(dimension_semantics=("arbitrary", "arbitrary"),
                                             vmem_limit_bytes=VMEM_LIMIT_BYTES),
        name="gdn_prep",
    )(q, k, v, sm, gct)

    full = lambda w: pl.BlockSpec((B, tt, w), lambda t: (0, t, 0))
    mats = pl.BlockSpec((B, HEADS, nc, DH, DH), lambda t: (0, 0, t, 0, 0))
    return pl.pallas_call(
        _gdn_scan_kernel,
        grid=(T // tt,),
        in_specs=[full(GW), full(GW), mats, mats,
                  pl.BlockSpec((B, HEADS, nc, CHUNK), lambda t: (0, 0, t, 0)),
                  full(GW), pl.BlockSpec((1, DH), lambda t: (0, 0))],
        out_specs=full(GW),
        out_shape=jax.ShapeDtypeStruct((B, T, GW), ACT),
        scratch_shapes=[pltpu.VMEM((B, HEADS, DH, DH), F32)],
        compiler_params=pltpu.CompilerParams(dimension_semantics=("arbitrary",),
                                             vmem_limit_bytes=VMEM_LIMIT_BYTES),
        name="gdn_scan",
    )(o_part, qt, m, n, gct, sg, nw)


def _hgrn_kernel(q_ref, k_ref, v_ref, b_ref, sg_ref, nw_ref, o_ref, st_ref):
    tt = q_ref.shape[1]
    nc = tt // CHUNK

    @pl.when(pl.program_id(1) == 0)
    def _():
        st_ref[...] = jnp.zeros(st_ref.shape, F32)

    row = lax.broadcasted_iota(jnp.int32, (CHUNK, CHUNK), 0)
    col = lax.broadcasted_iota(jnp.int32, (CHUNK, CHUNK), 1)
    diag_block = ((col // SUB) == (row // SUB)) & (col <= row)
    heads = range(HEADS)
    hs = [slice(hh * DH, (hh + 1) * DH) for hh in heads]

    def body(c, carry):
        rows = pl.ds(pl.multiple_of(c * CHUNK, CHUNK), CHUNK)
        q = [q_ref[0, rows, hs[h]].astype(F32) for h in heads]
        k = [k_ref[0, rows, hs[h]].astype(F32) for h in heads]
        v = [v_ref[0, rows, hs[h]].astype(F32) for h in heads]
        b = [b_ref[0, rows, hs[h]] for h in heads]
        blast = [b[h][CHUNK - 1:CHUNK, :] for h in heads]
        st = [st_ref[h] for h in heads]
        o = [_dot_nt(q[h] * jnp.exp(b[h]), st[h]) for h in heads]
        k_tail = [k[h] * jnp.exp(blast[h] - b[h]) for h in heads]
        for h in heads:
            st_ref[h] = st[h] * jnp.exp(blast[h]) + _dot_tn(v[h], k_tail[h])

        blocks = [[jnp.zeros((SUB, CHUNK), F32)] for _ in heads]
        for i in range(1, CHUNK // SUB):
            lo, hi = i * SUB, (i + 1) * SUB
            for h in heads:
                r = b[h][lo:lo + 1, :]
                qi = q[h][lo:hi] * jnp.exp(b[h][lo:hi] - r)
                kj = k[h][:lo] * jnp.exp(jnp.minimum(r - b[h][:lo], 0.0))
                kj = jnp.concatenate([kj, jnp.zeros((CHUNK - lo, DH), F32)], axis=0)
                blocks[h].append(_dot_nt(qi, kj))
        a = []
        for h in heads:
            f = jnp.exp(jnp.minimum(b[h] - pltpu.roll(b[h], 1, 0), 0.0))
            e = None
            a_diag = jnp.zeros((CHUNK, CHUNK), F32)
            for delta in range(SUB):
                if delta == 0:
                    term = q[h] * k[h]
                else:
                    fsh = f if delta == 1 else pltpu.roll(f, delta - 1, 0)
                    e = fsh if e is None else e * fsh
                    term = q[h] * pltpu.roll(k[h], delta, 0) * e
                colv = jnp.sum(term, axis=1, keepdims=True)
                a_diag = jnp.where(row - col == delta, colv, a_diag)
            a.append(jnp.where(diag_block, a_diag, jnp.concatenate(blocks[h], axis=0)))

        o = [o[h] + _dot(a[h], v[h]) for h in heads]
        o = jnp.concatenate(o, axis=1)
        o = _rms(o) * nw_ref[...] * sg_ref[0, rows, :].astype(F32)
        o_ref[0, rows, :] = o.astype(o_ref.dtype)
        return carry

    lax.fori_loop(0, nc, body, 0)


def _hgrn(q, k, v, b, sg, nw, tt):
    B, T, _ = q.shape
    tile = pl.BlockSpec((1, tt, GW), lambda bi, t: (bi, t, 0))
    return pl.pallas_call(
        _hgrn_kernel,
        grid=(B, T // tt),
        in_specs=[tile, tile, tile, tile, tile, pl.BlockSpec((1, GW), lambda bi, t: (0, 0))],
        out_specs=tile,
        out_shape=jax.ShapeDtypeStruct((B, T, GW), ACT),
        scratch_shapes=[pltpu.VMEM((HEADS, DH, DH), F32)],
        compiler_params=pltpu.CompilerParams(dimension_semantics=("arbitrary", "arbitrary"),
                                             vmem_limit_bytes=VMEM_LIMIT_BYTES),
        name="hgrn",
    )(q, k, v, b, sg, nw)


def _outproj_kernel(oa_ref, ob_ref, x_ref, wa_ref, wb_ref, g1_ref, n2_ref, sc_ref, sh_ref, wr_ref, rb_ref,
                    x1_ref, h2_ref, eidx_ref, rank_ref, wts_ref, cnt_ref):
    tm = x_ref.shape[1]

    @pl.when((pl.program_id(0) == 0) & (pl.program_id(1) == 0))
    def _():
        cnt_ref[...] = jnp.zeros(cnt_ref.shape, F32)

    mix = (jnp.dot(oa_ref[0], wa_ref[...], preferred_element_type=F32)
           + jnp.dot(ob_ref[0], wb_ref[...], preferred_element_type=F32))
    x1 = x_ref[0] + g1_ref[0] * mix
    x1_ref[0] = x1
    h2 = _rms(x1) * n2_ref[...]
    h2 = h2 * (1.0 + sc_ref[0]) + sh_ref[0]
    h2_ref[...] = _to_row_tiles(h2)

    scores = jax.nn.sigmoid(_dot3(wr_ref[...], h2, dot=_dot_nt))
    sel = scores + rb_ref[...]
    sub = lax.broadcasted_iota(jnp.int32, (GROUP_SIZE, tm), 0)
    neg = -jnp.inf
    groups = range(N_GROUPS)

    def take_max(blk):
        m = jnp.max(blk, axis=0, keepdims=True)
        first = jnp.min(jnp.where(blk == m, sub, GROUP_SIZE), axis=0, keepdims=True)
        hit = sub == first
        return m, hit, jnp.where(hit, neg, blk)

    blk_of = lambda a, g: a[g * GROUP_SIZE:(g + 1) * GROUP_SIZE]
    sel_blk = [blk_of(sel, g) for g in groups]
    group_score = jnp.zeros((N_GROUPS, tm), F32)
    for g in groups:
        m1, _, rest = take_max(sel_blk[g])
        m2 = jnp.max(rest, axis=0, keepdims=True)
        group_score = jnp.where(sub == g, m1 + m2, group_score)
    group_on = jnp.zeros((N_GROUPS, tm), F32)
    for _ in range(TOPK_GROUPS):
        _, hit, group_score = take_max(group_score)
        group_on = jnp.where(hit, 1.0, group_on)

    cand = [jnp.where(group_on[g:g + 1] > 0.0, sel_blk[g], neg) for g in groups]
    picked = [jnp.zeros((GROUP_SIZE, tm), F32) for _ in groups]
    chosen = []
    for _ in range(TOP_K):
        m = jnp.max(functools.reduce(jnp.maximum, cand), axis=0, keepdims=True)
        first = functools.reduce(jnp.minimum, [jnp.where(cand[g] == m, sub + g * GROUP_SIZE, N_EXPERTS)
                                               for g in groups])
        first = jnp.min(first, axis=0, keepdims=True)
        chosen.append(first)
        for g in groups:
            hit = (sub + g * GROUP_SIZE) == first
            picked[g] = jnp.where(hit, 1.0, picked[g])
            cand[g] = jnp.where(hit, neg, cand[g])

    picked_all = jnp.concatenate(picked, axis=0)
    r_i = lax.broadcasted_iota(jnp.int32, (tm, tm), 0)
    c_i = lax.broadcasted_iota(jnp.int32, (tm, tm), 1)
    earlier = jnp.where(r_i < c_i, 1.0, 0.0).astype(BF16)
    before = jnp.dot(picked_all.astype(BF16), earlier, preferred_element_type=F32) + cnt_ref[:, 0:1]
    cnt_ref[...] = cnt_ref[...] + jnp.sum(picked_all, axis=1, keepdims=True)

    def pick_value(table, first):
        parts = [jnp.where((sub + g * GROUP_SIZE) == first, blk_of(table, g), 0.0) for g in groups]
        return jnp.sum(functools.reduce(jnp.add, parts), axis=0, keepdims=True)

    w_k = [pick_value(scores, f) for f in chosen]
    denom = functools.reduce(jnp.add, w_k)
    eidx = jnp.zeros((TOP_K, tm), jnp.int32)
    rank = jnp.zeros((TOP_K, tm), jnp.int32)
    wts = jnp.zeros((TOP_K, tm), F32)
    for k in range(TOP_K):
        eidx = jnp.where(sub == k, chosen[k], eidx)
        rank = jnp.where(sub == k, pick_value(before, chosen[k]).astype(jnp.int32), rank)
        wts = jnp.where(sub == k, w_k[k] / denom * ROUTE_SCALE, wts)
    eidx_ref[...] = eidx
    rank_ref[...] = rank
    pad = jnp.zeros((GATE_LANES - TOP_K, tm), F32)
    wts_ref[0] = jnp.concatenate([wts, pad], axis=0).T


def _outproj(oa, ob, x, wa, wb, g1, n2, sc2, sh2, wr_t, rb, tm):
    B, T, D = x.shape
    nt = T // tm
    const = lambda shape: pl.BlockSpec(shape, lambda b, t: (0,) * len(shape))
    tile = lambda w: pl.BlockSpec((1, tm, w), lambda b, t: (b, t, 0))
    per_batch = pl.BlockSpec((1, 1, D), lambda b, t: (b, 0, 0))
    picks = pl.BlockSpec((TOP_K, tm), lambda b, t: (0, b * nt + t))
    return pl.pallas_call(
        _outproj_kernel,
        grid=(B, nt),
        in_specs=[tile(GW), tile(GW), tile(D), const(wa.shape), const(wb.shape), per_batch,
                  const((1, D)), per_batch, per_batch, const(wr_t.shape), const(rb.shape)],
        out_specs=[tile(D), pl.BlockSpec((tm, ROW_TILE, LANES), lambda b, t: (b * nt + t, 0, 0)),
                   picks, picks, tile(GATE_LANES), const((N_EXPERTS, GATE_LANES))],
        out_shape=[jax.ShapeDtypeStruct((B, T, D), F32), jax.ShapeDtypeStruct((B * T, ROW_TILE, LANES), F32),
                   jax.ShapeDtypeStruct((TOP_K, B * T), jnp.int32), jax.ShapeDtypeStruct((TOP_K, B * T), jnp.int32),
                   jax.ShapeDtypeStruct((B, T, GATE_LANES), F32),
                   jax.ShapeDtypeStruct((N_EXPERTS, GATE_LANES), F32)],
        compiler_params=pltpu.CompilerParams(dimension_semantics=("arbitrary", "arbitrary"),
                                             vmem_limit_bytes=VMEM_LIMIT_BYTES),
        name="outproj",
    )(oa, ob, x, wa, wb, g1, n2, sc2, sh2, wr_t, rb)


def _slots_kernel(off_ref, eidx_ref, rank_ref, slot_ref):
    eidx = eidx_ref[...]

    def add_expert(e, acc):
        return acc + jnp.where(eidx == e, off_ref[e], 0)

    slot_ref[...] = lax.fori_loop(0, N_EXPERTS, add_expert, rank_ref[...])


def _slots(offsets, eidx, rank):
    k, m = eidx.shape
    tile = pl.BlockSpec((k, m), lambda i: (0, 0))
    return pl.pallas_call(
        _slots_kernel,
        grid=(1,),
        in_specs=[pl.BlockSpec(memory_space=pltpu.SMEM), tile, tile],
        out_specs=tile,
        out_shape=jax.ShapeDtypeStruct((k, m), jnp.int32),
        compiler_params=pltpu.CompilerParams(dimension_semantics=("arbitrary",),
                                             vmem_limit_bytes=VMEM_LIMIT_BYTES),
        name="slots",
    )(offsets, eidx, rank)


def _dispatch_kernel(slot_ref, h_ref, xs_ref, sem):
    tmd = h_ref.shape[0]

    def start_rows(j, c):
        for k in range(TOP_K):
            pltpu.make_async_copy(h_ref.at[pl.ds(j, 1)], xs_ref.at[pl.ds(slot_ref[k, j], 1)], sem).start(
                priority=k % 2)
        return c

    lax.fori_loop(0, tmd, start_rows, 0)
    for _ in range(TOP_K):
        pltpu.make_async_copy(h_ref, xs_ref.at[pl.ds(0, tmd)], sem).wait()


def _dispatch(slot, h2, n_rows, tmd):
    M = h2.shape[0]
    return pl.pallas_call(
        _dispatch_kernel,
        grid=(M // tmd,),
        in_specs=[pl.BlockSpec((TOP_K, tmd), lambda i: (0, i), memory_space=pltpu.SMEM),
                  pl.BlockSpec((tmd, ROW_TILE, LANES), lambda i: (i, 0, 0))],
        out_specs=pl.BlockSpec(memory_space=pl.ANY),
        out_shape=jax.ShapeDtypeStruct((n_rows, ROW_TILE, LANES), F32),
        scratch_shapes=[pltpu.SemaphoreType.DMA],
        compiler_params=pltpu.CompilerParams(dimension_semantics=("arbitrary",),
                                             vmem_limit_bytes=VMEM_LIMIT_BYTES),
        name="dispatch",
    )(slot, h2)


FFN_PIPELINE_STEPS = 2


def _ffn_kernel(te_ref, nu_ref, nv_ref, x_ref, wg_ref, wu_ref, wd_ref, y_ref, wgu_s, wd_s, xstd, ystd):
    i = pl.program_id(0)
    row = lax.broadcasted_iota(jnp.int32, (MOE_ROWS, 1), 0)
    tile_mm = jnp.maximum(i - 1, 0)
    tile_out = jnp.maximum(i - 2, 0)

    @pl.when(i == 0)
    def _():
        xstd[...] = jnp.zeros(xstd.shape, BF16)
        ystd[...] = jnp.zeros(ystd.shape, F32)

    @pl.when(i < nu_ref[0] + FFN_PIPELINE_STEPS)
    def _():
        @pl.when((i == 0) | (te_ref[tile_mm] != te_ref[tile_out]))
        def _():
            wgu_s[:, :D_EXPERT] = wg_ref[0].astype(BF16)
            wgu_s[:, D_EXPERT:] = wu_ref[0].astype(BF16)
            wd_s[...] = wd_ref[0].astype(BF16)

        cur = i % 2
        xstd[cur] = jnp.where(row < nv_ref[i], _to_rows(x_ref[...]), 0.0).astype(BF16)
        y_ref[...] = _to_row_tiles(ystd[cur])
        gu = jnp.dot(xstd[1 - cur], wgu_s[...], preferred_element_type=F32)
        act = _silu(gu[:, :D_EXPERT]) * gu[:, D_EXPERT:]
        ystd[1 - cur] = jnp.dot(act.astype(BF16), wd_s[...], preferred_element_type=F32)


def _ffn(tile_expert, n_used, n_valid, xs, wg, wu, wd):
    D = wg.shape[1]
    n_tiles = xs.shape[0] // MOE_ROWS
    block = (MOE_ROWS, ROW_TILE, LANES)
    expert = lambda i, te, nu, nv: (te[jnp.maximum(i - 1, 0)], 0, 0)
    return pl.pallas_call(
        _ffn_kernel,
        grid_spec=pltpu.PrefetchScalarGridSpec(
            num_scalar_prefetch=3,
            grid=(n_tiles + FFN_PIPELINE_STEPS,),
            in_specs=[pl.BlockSpec(block, lambda i, te, nu, nv: (jnp.minimum(i, nu[0] - 1), 0, 0)),
                      pl.BlockSpec((1, D, D_EXPERT), expert),
                      pl.BlockSpec((1, D, D_EXPERT), expert),
                      pl.BlockSpec((1, D_EXPERT, D), expert)],
            out_specs=pl.BlockSpec(block, lambda i, te, nu, nv: (jnp.clip(i - 2, 0, nu[0] - 1), 0, 0)),
            scratch_shapes=[pltpu.VMEM((D, 2 * D_EXPERT), BF16), pltpu.VMEM((D_EXPERT, D), BF16),
                            pltpu.VMEM((2, MOE_ROWS, D), BF16), pltpu.VMEM((2, MOE_ROWS, D), F32)]),
        out_shape=jax.ShapeDtypeStruct(xs.shape, F32),
        compiler_params=pltpu.CompilerParams(dimension_semantics=("arbitrary",),
                                             vmem_limit_bytes=VMEM_LIMIT_BYTES),
        name="ffn",
    )(tile_expert, n_used, n_valid, xs, wg, wu, wd)


def _combine_kernel(slot_ref, next_slot_ref, w_ref, h_ref, x1_ref, g2_ref, fw_ref, wsgu_ref, wsd_ref, y_ref,
                    o_ref, buf, sem):
    tmc = h_ref.shape[0]
    i = pl.program_id(0)
    last = pl.num_programs(0) - 1

    def start_rows(slots, b):
        def body(j, c):
            for k in range(TOP_K):
                pltpu.make_async_copy(y_ref.at[pl.ds(slots[k, j], 1)], buf.at[b, k, pl.ds(j, 1)], sem.at[b]).start(
                    priority=k % 2)
            return c
        lax.fori_loop(0, tmc, body, 0)

    def wait_rows(b):
        for k in range(TOP_K):
            pltpu.make_async_copy(y_ref.at[pl.ds(0, tmc)], buf.at[b, k], sem.at[b]).wait()

    def for_buffer(step, fn):
        for b in range(2):
            @pl.when(step % 2 == b)
            def _():
                fn(b)

    @pl.when(i == 0)
    def _():
        start_rows(slot_ref, 0)

    @pl.when(i < last)
    def _():
        for_buffer(i + 1, lambda b: start_rows(next_slot_ref, b))

    gu = jnp.dot(_to_rows(h_ref[...]).astype(BF16), wsgu_ref[...], preferred_element_type=F32)
    act = _silu(gu[:, :D_EXPERT]) * gu[:, D_EXPERT:]
    shared = jnp.dot(act.astype(BF16), wsd_ref[...], preferred_element_type=F32)
    w = w_ref[...]
    lane = lax.broadcasted_iota(jnp.int32, w.shape, 1)

    def finish(b):
        wait_rows(b)
        routed = jnp.zeros((tmc, ROW_TILE, LANES), F32)
        for k in range(TOP_K):
            wk = jnp.broadcast_to(_lane_pick(w, lane, k), (tmc, LANES))
            routed = routed + _to_row_tiles(jnp.concatenate([wk] * ROW_TILE, axis=1)) * buf[b, k]
        acc = shared + _to_rows(routed)
        y = x1_ref[...] + g2_ref[0] * acc
        o_ref[...] = _rms(y) * fw_ref[...]

    for_buffer(i, finish)


def _combine(slot, wts, h2, x1, g2, fw, wsgu, wsd, ys, T, tmc):
    M, D = x1.shape
    const = lambda shape: pl.BlockSpec(shape, lambda i: (0,) * len(shape))
    tile = lambda w: pl.BlockSpec((tmc, w), lambda i: (i, 0))
    row_tiled = pl.BlockSpec((tmc, ROW_TILE, LANES), lambda i: (i, 0, 0))
    n_steps = M // tmc
    return pl.pallas_call(
        _combine_kernel,
        grid=(n_steps,),
        in_specs=[pl.BlockSpec((TOP_K, tmc), lambda i: (0, i), memory_space=pltpu.SMEM),
                  pl.BlockSpec((TOP_K, tmc), lambda i: (0, jnp.minimum(i + 1, n_steps - 1)), memory_space=pltpu.SMEM),
                  tile(GATE_LANES), row_tiled, tile(D),
                  pl.BlockSpec((1, 1, D), lambda i: (i // (T // tmc), 0, 0)),
                  const((1, D)), const(wsgu.shape), const(wsd.shape),
                  pl.BlockSpec(memory_space=pl.ANY)],
        out_specs=tile(D),
        out_shape=jax.ShapeDtypeStruct((M, D), F32),
        scratch_shapes=[pltpu.VMEM((2, TOP_K, tmc, ROW_TILE, LANES), F32), pltpu.SemaphoreType.DMA((2,))],
        compiler_params=pltpu.CompilerParams(dimension_semantics=("arbitrary",),
                                             vmem_limit_bytes=VMEM_LIMIT_BYTES),
        name="combine",
    )(slot, slot, wts, h2, x1, g2, fw, wsgu, wsd, ys)


def _pick_tile(n, want):
    t = min(n, want)
    assert n % t == 0 and t % CHUNK == 0, (n, want)
    return t


def kernel(x, c, w_ada, b_ada, norm1_w, w_in, conv_w, gdn_a_log, gdn_dt_bias, gdn_norm_w, hg_lb, hg_norm_w,
           w_out, norm2_w, w_router, router_bias, w_gate, w_up, w_down, ws_gate, ws_up, ws_down, final_norm_w):
    B, T, D = x.shape
    M = B * T
    depth = w_ada.shape[0]
    assert depth == 1 and T % CHUNK == 0 and B <= 8
    layer = 0
    tt = _pick_tile(T, 512)

    c_pad = jnp.pad(c, ((0, 8 - B), (0, 0)))
    mod = _ada(c_pad, w_ada[layer], b_ada[layer].reshape(1, -1))[:B]
    sh1, sc1, g1, sh2, sc2, g2 = (m.reshape(B, 1, D) for m in jnp.split(mod, 6, axis=-1))

    w = w_in[layer]
    qkv_w = 3 * GW
    sizes = (GW, HEADS, HEADS, GW, GW, GW, GW)
    offs = [qkv_w]
    for s in sizes:
        offs.append(offs[-1] + s)
    seg = lambda i: w[:, offs[i]:offs[i + 1]]
    small = jnp.pad(jnp.concatenate([seg(1), seg(2)], axis=1), ((0, 0), (0, GATE_LANES - 2 * HEADS)))
    w_all = jnp.concatenate([w[:, :qkv_w], seg(0), seg(3), seg(4), seg(5), seg(6), small], axis=1).astype(BF16)
    lane_pad = lambda v: jnp.pad(v.astype(F32).reshape(1, HEADS), ((0, 0), (HEADS, GATE_LANES - 2 * HEADS)))
    idx = jnp.arange(tt)
    tri = ((idx[:, None] >= idx[None, :]) & (idx[:, None] // CHUNK == idx[None, :] // CHUNK)).astype(BF16)

    qa, ka, va, ga, sm, bcum, kb, ib, qb, gb = _inproj(
        layer, x, norm1_w[layer].reshape(1, D), sc1, sh1, w_all, conv_w[layer].astype(F32),
        lane_pad(gdn_a_log[layer]), lane_pad(gdn_dt_bias[layer]), hg_lb.astype(F32), tri, tt)

    gct = sm[:, :, HEADS:2 * HEADS].transpose(0, 2, 1).reshape(B, HEADS, T // CHUNK, CHUNK)
    oa = _gdn(qa, ka, va, ga, sm, gct, gdn_norm_w[layer].reshape(1, DH), tt)
    ob = _hgrn(qb, kb, ib, bcum, gb, hg_norm_w[layer].reshape(1, GW), tt)

    wo = w_out[layer].astype(BF16)
    x1, h2, eidx, rank, wts, cnt = _outproj(oa, ob, x, wo[:GW], wo[GW:], g1, norm2_w[layer].reshape(1, D), sc2, sh2,
                                            w_router[layer].T, router_bias[layer].reshape(N_EXPERTS, 1), tt)

    counts = cnt[:, 0].astype(jnp.int32)
    padded = (counts + MOE_ROWS - 1) // MOE_ROWS * MOE_ROWS
    ends = jnp.cumsum(padded)
    offsets = ends - padded
    n_tiles = (M * TOP_K) // MOE_ROWS + N_EXPERTS
    n_used = (ends[-1] // MOE_ROWS).astype(jnp.int32)
    tile_ids = jnp.minimum(jnp.arange(n_tiles + FFN_PIPELINE_STEPS, dtype=jnp.int32), n_used - 1)
    tile_expert = jnp.sum(((ends // MOE_ROWS)[None, :] <= tile_ids[:, None]).astype(jnp.int32), axis=1)
    tile_expert = jnp.minimum(tile_expert, N_EXPERTS - 1)
    slot = _slots(offsets.astype(jnp.int32), eidx, rank)
    first_tile = jnp.take(offsets, tile_expert) // MOE_ROWS
    n_valid = jnp.clip(jnp.take(counts, tile_expert) - (tile_ids - first_tile) * MOE_ROWS, 0, MOE_ROWS)

    assert D == ROW_TILE * LANES
    xs = _dispatch(slot, h2, n_tiles * MOE_ROWS, _pick_tile(M, 512))
    ys = _ffn(tile_expert.astype(jnp.int32), n_used.reshape(1), n_valid.astype(jnp.int32), xs,
              w_gate[layer], w_up[layer], w_down[layer])
    wsgu = jnp.concatenate([ws_gate[layer], ws_up[layer]], axis=-1).astype(BF16)
    out = _combine(slot, wts.reshape(M, GATE_LANES), h2, x1.reshape(M, D), g2, final_norm_w.reshape(1, D),
                   wsgu, ws_down[layer].astype(BF16), ys, T, _pick_tile(T, 128))
    return out.reshape(B, T, D)
```

```python
import functools

import jax
import jax.numpy as jnp
from jax import lax
from jax.experimental import pallas as pl
from jax.experimental.pallas import tpu as pltpu

F32 = jnp.float32
BF16 = jnp.bfloat16

EPS = 1e-6
CHUNK = 64
SUB = 8
HEADS = 4
DH = 128
GW = HEADS * DH
CONV_K = 4
N_EXPERTS = 64
N_GROUPS = 8
GROUP_SIZE = N_EXPERTS // N_GROUPS
TOPK_GROUPS = 4
TOP_K = 8
D_EXPERT = 256
ROUTE_SCALE = 2.5
GATE_LANES = 128
GDN_CHUNKS_PER_ITER = 4
INV_BLOCK = 16
CUMSUM_ROWS = 128
MOE_ROWS = 1024
ROW_TILE, LANES = 8, 128

VMEM_LIMIT_BYTES = 56 * 1024 * 1024

ACT = BF16


def _silu(x):
    return x * jax.nn.sigmoid(x)


def _dot(a, b):
    return jnp.dot(a.astype(BF16), b.astype(BF16), preferred_element_type=F32)


def _dot_nt(a, b):
    return lax.dot_general(a.astype(BF16), b.astype(BF16), (((1,), (1,)), ((), ())),
                           preferred_element_type=F32)


def _dot_tn(a, b):
    return lax.dot_general(a.astype(BF16), b.astype(BF16), (((0,), (0,)), ((), ())),
                           preferred_element_type=F32)


def _split2(x):
    hi = x.astype(BF16)
    lo = (x - hi.astype(F32)).astype(BF16)
    return hi, lo


def _dot3(a, b, dot=_dot):
    ah, al = _split2(a)
    bh, bl = _split2(b)
    return dot(ah, bh) + dot(ah, bl) + dot(al, bh)


def _cumsum_rows(tri, x):
    hi = x.astype(BF16)
    r = x - hi.astype(F32)
    mid = r.astype(BF16)
    lo = (r - mid.astype(F32)).astype(BF16)
    g = tri.shape[0]
    groups = []
    for r0 in range(0, x.shape[0], g):
        rows = slice(r0, r0 + g)
        groups.append(jnp.dot(tri, hi[rows], preferred_element_type=F32)
                      + jnp.dot(tri, mid[rows], preferred_element_type=F32)
                      + jnp.dot(tri, lo[rows], preferred_element_type=F32))
    return jnp.concatenate(groups, axis=0)


def _lane_pick(tile, lane, idx):
    return jnp.sum(jnp.where(lane == idx, tile, 0.0), axis=1, keepdims=True)


def _rms(x):
    return x * lax.rsqrt(jnp.mean(x * x, axis=-1, keepdims=True) + EPS)


def _to_rows(x3):
    r = x3.shape[0]
    xt = jnp.swapaxes(x3.reshape(r // ROW_TILE, ROW_TILE, ROW_TILE, LANES), 1, 2)
    return jnp.concatenate([xt[:, s].reshape(r, LANES) for s in range(ROW_TILE)], axis=1)


def _to_row_tiles(x):
    r = x.shape[0]
    xt = jnp.stack([x[:, s * LANES:(s + 1) * LANES].reshape(r // ROW_TILE, ROW_TILE, LANES) for s in range(ROW_TILE)],
                   axis=1)
    return jnp.swapaxes(xt, 1, 2).reshape(r, ROW_TILE, LANES)


def _ada_kernel(c_ref, w_ref, b_ref, o_ref):
    ca = _silu(c_ref[...])
    o_ref[...] = _dot3(ca, w_ref[...]) + b_ref[...]


def _ada(c_pad, w, b):
    rows, d = c_pad.shape
    n = w.shape[1]
    tn = 1024
    return pl.pallas_call(
        _ada_kernel,
        grid=(n // tn,),
        in_specs=[pl.BlockSpec((rows, d), lambda j: (0, 0)),
                  pl.BlockSpec((d, tn), lambda j: (0, j)),
                  pl.BlockSpec((1, tn), lambda j: (0, j))],
        out_specs=pl.BlockSpec((rows, tn), lambda j: (0, j)),
        out_shape=jax.ShapeDtypeStruct((rows, n), F32),
        compiler_params=pltpu.CompilerParams(dimension_semantics=("arbitrary",),
                                             vmem_limit_bytes=VMEM_LIMIT_BYTES),
        name="ada",
    )(c_pad, w, b)


def _inproj_kernel(layer, x_ref, n1_ref, sc_ref, sh_ref, w_ref, cw_ref, alog_ref, dt_ref, lb_ref, tri_ref,
                   qa_ref, ka_ref, va_ref, ga_ref, sm_ref, b_ref, kb_ref, ib_ref, qb_ref, gb_ref,
                   pbuf):
    tt = x_ref.shape[1]
    t = pl.program_id(1)

    h = _rms(x_ref[0]) * n1_ref[...]
    h = h * (1.0 + sc_ref[0]) + sh_ref[0]
    hb = h.astype(BF16)

    def proj(g, width=GW):
        return jnp.dot(hb, w_ref[:, g * GW:g * GW + width], preferred_element_type=F32)

    @pl.when(t == 0)
    def _():
        pbuf[:, 0:8, :] = jnp.zeros((3, 8, GW), F32)

    for g, out_ref in enumerate((qa_ref, ka_ref, va_ref)):
        cols = slice(g * GW, (g + 1) * GW)
        p = proj(g)
        pbuf[g, 8:8 + tt, :] = p
        y = p * cw_ref[CONV_K - 1:CONV_K, cols]
        for j in range(1, CONV_K):
            y = y + pbuf[g, 8 - j:8 - j + tt, :] * cw_ref[CONV_K - 1 - j:CONV_K - j, cols]
        pbuf[g, 0:8, :] = pbuf[g, tt:tt + 8, :]
        y = _silu(y)
        if g == 2:
            out_ref[0] = y.astype(out_ref.dtype)
        else:
            scale = DH ** -0.5 if g == 0 else 1.0
            for hh in range(HEADS):
                hs = slice(hh * DH, (hh + 1) * DH)
                yh = y[:, hs]
                inv = lax.rsqrt(jnp.sum(yh * yh, axis=-1, keepdims=True) + EPS)
                out_ref[0, :, hs] = (yh * inv * scale).astype(out_ref.dtype)

    ga_ref[0] = _silu(proj(3)).astype(ga_ref.dtype)

    ps = proj(8, GATE_LANES)
    lane = lax.broadcasted_iota(jnp.int32, ps.shape, 1)
    beta = jax.nn.sigmoid(ps)
    z = ps + dt_ref[...]
    softplus = jnp.maximum(z, 0.0) + jnp.log1p(jnp.exp(-jnp.abs(z)))
    g_log = -jnp.exp(alog_ref[...]) * softplus
    tri = tri_ref[...]
    gc = _cumsum_rows(tri, jnp.where((lane >= HEADS) & (lane < 2 * HEADS), g_log, 0.0))
    sm_ref[0] = jnp.where(lane < HEADS, beta, gc)

    hl = lb_ref[...]
    e = jnp.exp(hl - jnp.max(hl, axis=0, keepdims=True))
    lb = jnp.sum(e[0:layer + 1], axis=0, keepdims=True) / jnp.sum(e, axis=0, keepdims=True)
    fr = proj(4)
    logf = jnp.log(lb + (1.0 - lb) * jax.nn.sigmoid(fr))
    b_ref[0] = _cumsum_rows(tri, logf)
    kb_ref[0] = ((1.0 - lb) * jax.nn.sigmoid(-fr)).astype(kb_ref.dtype)
    ib_ref[0] = proj(5).astype(ib_ref.dtype)
    qb_ref[0] = _silu(proj(6)).astype(qb_ref.dtype)
    gb_ref[0] = _silu(proj(7)).astype(gb_ref.dtype)


def _inproj(layer, x, n1, sc1, sh1, w_all, conv_w, alog_pad, dt_pad, hg_lb, tri, tt):
    B, T, D = x.shape
    const = lambda shape: pl.BlockSpec(shape, lambda b, t: (0,) * len(shape))
    act = lambda dt: jax.ShapeDtypeStruct((B, T, GW), dt)
    tile = lambda w: pl.BlockSpec((1, tt, w), lambda b, t: (b, t, 0))
    per_batch = pl.BlockSpec((1, 1, D), lambda b, t: (b, 0, 0))
    return pl.pallas_call(
        functools.partial(_inproj_kernel, layer),
        grid=(B, T // tt),
        in_specs=[tile(D), const((1, D)), per_batch, per_batch,
                  const(w_all.shape), const(conv_w.shape), const((1, GATE_LANES)), const((1, GATE_LANES)),
                  const(hg_lb.shape), const(tri.shape)],
        out_specs=[tile(GW), tile(GW), tile(GW), tile(GW), tile(GATE_LANES), tile(GW),
                   tile(GW), tile(GW), tile(GW), tile(GW)],
        out_shape=[act(ACT), act(ACT), act(ACT), act(ACT),
                   jax.ShapeDtypeStruct((B, T, GATE_LANES), F32), act(F32),
                   act(ACT), act(ACT), act(ACT), act(ACT)],
        scratch_shapes=[pltpu.VMEM((3, tt + 8, GW), F32)],
        compiler_params=pltpu.CompilerParams(dimension_semantics=("arbitrary", "arbitrary"),
                                             vmem_limit_bytes=VMEM_LIMIT_BYTES),
        name="inproj",
    )(x, n1, sc1, sh1, w_all, conv_w, alog_pad, dt_pad, hg_lb, tri)


def _gdn_prep_kernel(q_ref, k_ref, v_ref, sm_ref, gct_ref, o_ref, qt_ref, m_ref, n_ref):
    tt = q_ref.shape[1]
    nc = tt // CHUNK
    row = lax.broadcasted_iota(jnp.int32, (CHUNK, CHUNK), 0)
    col = lax.broadcasted_iota(jnp.int32, (CHUNK, CHUNK), 1)
    causal = row >= col
    diag_blk = (row > col) & (row // INV_BLOCK == col // INV_BLOCK)
    off_blk = row // INV_BLOCK > col // INV_BLOCK
    eye = jnp.where(row == col, 1.0, 0.0)
    lane = lax.broadcasted_iota(jnp.int32, (CHUNK, GATE_LANES), 1)
    assert INV_BLOCK == 16 and CHUNK == 4 * INV_BLOCK

    def body(i, carry):
        chains = [(GDN_CHUNKS_PER_ITER * i + j, hh) for j in range(GDN_CHUNKS_PER_ITER) for hh in range(HEADS)]
        rows = [pl.ds(pl.multiple_of(c * CHUNK, CHUNK), CHUNK) for c, _ in chains]
        hs = [slice(hh * DH, (hh + 1) * DH) for _, hh in chains]
        n = range(len(chains))
        sm = [sm_ref[0, rows[j], :] for j in n]
        q = [q_ref[0, rows[j], hs[j]].astype(F32) for j in n]
        k = [k_ref[0, rows[j], hs[j]].astype(F32) for j in n]
        v = [v_ref[0, rows[j], hs[j]].astype(F32) for j in n]
        beta = [_lane_pick(sm[j], lane, chains[j][1]) for j in n]
        gcol = [_lane_pick(sm[j], lane, HEADS + chains[j][1]) for j in n]
        grow = [gct_ref[0, hh, pl.ds(c, 1), :] for c, hh in chains]
        decay = [jnp.exp(jnp.where(causal, gcol[j] - grow[j], -jnp.inf)) for j in n]
        kb = [k[j] * beta[j] for j in n]
        L = [_dot_nt(kb[j], k[j]) * decay[j] for j in n]
        dg = [jnp.where(diag_blk, L[j], 0.0) for j in n]
        off = [jnp.where(off_blk, L[j], 0.0) for j in n]
        dinv = [eye - dg[j] for j in n]
        pw = [_dot3(dg[j], dg[j]) for j in n]
        for _ in range(2):
            dinv = [dinv[j] + _dot3(dinv[j], pw[j]) for j in n]
            pw = [_dot3(pw[j], pw[j]) for j in n]
        dinv = [dinv[j] + _dot3(dinv[j], pw[j]) for j in n]
        f1 = [_dot(dinv[j], off[j]) for j in n]
        f2 = [_dot(f1[j], f1[j]) for j in n]
        f3 = [_dot(f1[j], f2[j]) for j in n]
        tinv = [_dot(eye - f1[j] + f2[j] - f3[j], dinv[j]) for j in n]
        eg = [jnp.exp(gcol[j]) for j in n]
        sol = [_dot(tinv[j], jnp.concatenate([v[j] * beta[j], kb[j] * eg[j]], axis=1)) for j in n]
        attn = [_dot_nt(q[j], k[j]) * decay[j] for j in n]
        k_tail = [k[j] * jnp.exp(gcol[j][CHUNK - 1:CHUNK, :] - gcol[j]) for j in n]
        au = [_dot(attn[j], sol[j]) for j in n]
        ku = [_dot_tn(k_tail[j], sol[j]) for j in n]
        for j, (c, hh) in enumerate(chains):
            o_ref[0, rows[j], hs[j]] = au[j][:, :DH]
            qt_ref[0, rows[j], hs[j]] = (q[j] * eg[j] - au[j][:, DH:]).astype(qt_ref.dtype)
            n_ref[0, hh, c] = ku[j][:, :DH].astype(n_ref.dtype)
            m_ref[0, hh, c] = (-ku[j][:, DH:]).astype(m_ref.dtype)
        return carry

    lax.fori_loop(0, nc // GDN_CHUNKS_PER_ITER, body, 0)


def _gdn_scan_kernel(o_ref, qt_ref, m_ref, n_ref, gct_ref, sg_ref, nw_ref, out_ref, s_ref):
    nb, tt = o_ref.shape[0], o_ref.shape[1]
    nc = tt // CHUNK

    @pl.when(pl.program_id(0) == 0)
    def _():
        s_ref[...] = jnp.zeros(s_ref.shape, F32)

    nw = nw_ref[...]

    def body(c, carry):
        rows = pl.ds(pl.multiple_of(c * CHUNK, CHUNK), CHUNK)
        for b in range(nb):
            for hh in range(HEADS):
                hs = slice(hh * DH, (hh + 1) * DH)
                S = s_ref[b, hh]
                Sb = S.astype(BF16)
                glast = gct_ref[b, hh, pl.ds(c, 1), :][:, CHUNK - 1:CHUNK]
                o = o_ref[b, rows, hs] + jnp.dot(qt_ref[b, rows, hs], Sb, preferred_element_type=F32)
                s_ref[b, hh] = (S * jnp.exp(glast) + jnp.dot(m_ref[b, hh, c], Sb, preferred_element_type=F32)
                                + n_ref[b, hh, c].astype(F32))
                o = _rms(o) * nw * sg_ref[b, rows, hs].astype(F32)
                out_ref[b, rows, hs] = o.astype(out_ref.dtype)
        return carry

    lax.fori_loop(0, nc, body, 0)


def _gdn(q, k, v, sg, sm, gct, nw, tt):
    B, T, _ = q.shape
    nc = tt // CHUNK
    assert nc % GDN_CHUNKS_PER_ITER == 0
    n_chunks = T // CHUNK
    tile = lambda w: pl.BlockSpec((1, tt, w), lambda b, t: (b, t, 0))
    mat = jax.ShapeDtypeStruct((B, HEADS, n_chunks, DH, DH), ACT)
    o_part, qt, m, n = pl.pallas_call(
        _gdn_prep_kernel,
        grid=(B, T // tt),
        in_specs=[tile(GW), tile(GW), tile(GW), tile(GATE_LANES),
                  pl.BlockSpec((1, HEADS, nc, CHUNK), lambda b, t: (b, 0, t, 0))],
        out_specs=[tile(GW), tile(GW),
                   pl.BlockSpec((1, HEADS, nc, DH, DH), lambda b, t: (b, 0, t, 0, 0)),
                   pl.BlockSpec((1, HEADS, nc, DH, DH), lambda b, t: (b, 0, t, 0, 0))],
        out_shape=[jax.ShapeDtypeStruct((B, T, GW), F32), jax.ShapeDtypeStruct((B, T, GW), ACT), mat, mat],
        compiler_params=pltpu.CompilerParams(dimension_semantics=("arbitrary", "arbitrary"),
                                             vmem_limit_bytes=VMEM_LIMIT_BYTES),
        name="gdn_prep",
    )(q, k, v, sm, gct)

    full = lambda w: pl.BlockSpec((B, tt, w), lambda t: (0, t, 0))
    mats = pl.BlockSpec((B, HEADS, nc, DH, DH), lambda t: (0, 0, t, 0, 0))
    return pl.pallas_call(
        _gdn_scan_kernel,
        grid=(T // tt,),
        in_specs=[full(GW), full(GW), mats, mats,
                  pl.BlockSpec((B, HEADS, nc, CHUNK), lambda t: (0, 0, t, 0)),
                  full(GW), pl.BlockSpec((1, DH), lambda t: (0, 0))],
        out_specs=full(GW),
        out_shape=jax.ShapeDtypeStruct((B, T, GW), ACT),
        scratch_shapes=[pltpu.VMEM((B, HEADS, DH, DH), F32)],
        compiler_params=pltpu.CompilerParams(dimension_semantics=("arbitrary",),
                                             vmem_limit_bytes=VMEM_LIMIT_BYTES),
        name="gdn_scan",
    )(o_part, qt, m, n, gct, sg, nw)


def _hgrn_kernel(q_ref, k_ref, v_ref, b_ref, sg_ref, nw_ref, o_ref, st_ref):
    tt = q_ref.shape[1]
    nc = tt // CHUNK

    @pl.when(pl.program_id(1) == 0)
    def _():
        st_ref[...] = jnp.zeros(st_ref.shape, F32)

    row = lax.broadcasted_iota(jnp.int32, (CHUNK, CHUNK), 0)
    col = lax.broadcasted_iota(jnp.int32, (CHUNK, CHUNK), 1)
    diag_block = ((col // SUB) == (row // SUB)) & (col <= row)
    heads = range(HEADS)
    hs = [slice(hh * DH, (hh + 1) * DH) for hh in heads]

    def body(c, carry):
        rows = pl.ds(pl.multiple_of(c * CHUNK, CHUNK), CHUNK)
        q = [q_ref[0, rows, hs[h]].astype(F32) for h in heads]
        k = [k_ref[0, rows, hs[h]].astype(F32) for h in heads]
        v = [v_ref[0, rows, hs[h]].astype(F32) for h in heads]
        b = [b_ref[0, rows, hs[h]] for h in heads]
        blast = [b[h][CHUNK - 1:CHUNK, :] for h in heads]
        st = [st_ref[h] for h in heads]
        o = [_dot_nt(q[h] * jnp.exp(b[h]), st[h]) for h in heads]
        k_tail = [k[h] * jnp.exp(blast[h] - b[h]) for h in heads]
        for h in heads:
            st_ref[h] = st[h] * jnp.exp(blast[h]) + _dot_tn(v[h], k_tail[h])

        blocks = [[jnp.zeros((SUB, CHUNK), F32)] for _ in heads]
        for i in range(1, CHUNK // SUB):
            lo, hi = i * SUB, (i + 1) * SUB
            for h in heads:
                r = b[h][lo:lo + 1, :]
                qi = q[h][lo:hi] * jnp.exp(b[h][lo:hi] - r)
                kj = k[h][:lo] * jnp.exp(jnp.minimum(r - b[h][:lo], 0.0))
                kj = jnp.concatenate([kj, jnp.zeros((CHUNK - lo, DH), F32)], axis=0)
                blocks[h].append(_dot_nt(qi, kj))
        a = []
        for h in heads:
            f = jnp.exp(jnp.minimum(b[h] - pltpu.roll(b[h], 1, 0), 0.0))
            e = None
            a_diag = jnp.zeros((CHUNK, CHUNK), F32)
            for delta in range(SUB):
                if delta == 0:
                    term = q[h] * k[h]
                else:
                    fsh = f if delta == 1 else pltpu.roll(f, delta - 1, 0)
                    e = fsh if e is None else e * fsh
                    term = q[h] * pltpu.roll(k[h], delta, 0) * e
                colv = jnp.sum(term, axis=1, keepdims=True)
                a_diag = jnp.where(row - col == delta, colv, a_diag)
            a.append(jnp.where(diag_block, a_diag, jnp.concatenate(blocks[h], axis=0)))

        o = [o[h] + _dot(a[h], v[h]) for h in heads]
        o = jnp.concatenate(o, axis=1)
        o = _rms(o) * nw_ref[...] * sg_ref[0, rows, :].astype(F32)
        o_ref[0, rows, :] = o.astype(o_ref.dtype)
        return carry

    lax.fori_loop(0, nc, body, 0)


def _hgrn(q, k, v, b, sg, nw, tt):
    B, T, _ = q.shape
    tile = pl.BlockSpec((1, tt, GW), lambda bi, t: (bi, t, 0))
    return pl.pallas_call(
        _hgrn_kernel,
        grid=(B, T // tt),
        in_specs=[tile, tile, tile, tile, tile, pl.BlockSpec((1, GW), lambda bi, t: (0, 0))],
        out_specs=tile,
        out_shape=jax.ShapeDtypeStruct((B, T, GW), ACT),
        scratch_shapes=[pltpu.VMEM((HEADS, DH, DH), F32)],
        compiler_params=pltpu.CompilerParams(dimension_semantics=("arbitrary", "arbitrary"),
                                             vmem_limit_bytes=VMEM_LIMIT_BYTES),
        name="hgrn",
    )(q, k, v, b, sg, nw)


def _outproj_kernel(oa_ref, ob_ref, x_ref, wa_ref, wb_ref, g1_ref, n2_ref, sc_ref, sh_ref, wr_ref, rb_ref,
                    x1_ref, h2_ref, eidx_ref, rank_ref, wts_ref, cnt_ref):
    tm = x_ref.shape[1]

    @pl.when((pl.program_id(0) == 0) & (pl.program_id(1) == 0))
    def _():
        cnt_ref[...] = jnp.zeros(cnt_ref.shape, F32)

    mix = (jnp.dot(oa_ref[0], wa_ref[...], preferred_element_type=F32)
           + jnp.dot(ob_ref[0], wb_ref[...], preferred_element_type=F32))
    x1 = x_ref[0] + g1_ref[0] * mix
    x1_ref[0] = x1
    h2 = _rms(x1) * n2_ref[...]
    h2 = h2 * (1.0 + sc_ref[0]) + sh_ref[0]
    h2_ref[...] = _to_row_tiles(h2)

    scores = jax.nn.sigmoid(_dot3(wr_ref[...], h2, dot=_dot_nt))
    sel = scores + rb_ref[...]
    sub = lax.broadcasted_iota(jnp.int32, (GROUP_SIZE, tm), 0)
    neg = -jnp.inf
    groups = range(N_GROUPS)

    def take_max(blk):
        m = jnp.max(blk, axis=0, keepdims=True)
        first = jnp.min(jnp.where(blk == m, sub, GROUP_SIZE), axis=0, keepdims=True)
        hit = sub == first
        return m, hit, jnp.where(hit, neg, blk)

    blk_of = lambda a, g: a[g * GROUP_SIZE:(g + 1) * GROUP_SIZE]
    sel_blk = [blk_of(sel, g) for g in groups]
    group_score = jnp.zeros((N_GROUPS, tm), F32)
    for g in groups:
        m1, _, rest = take_max(sel_blk[g])
        m2 = jnp.max(rest, axis=0, keepdims=True)
        group_score = jnp.where(sub == g, m1 + m2, group_score)
    group_on = jnp.zeros((N_GROUPS, tm), F32)
    for _ in range(TOPK_GROUPS):
        _, hit, group_score = take_max(group_score)
        group_on = jnp.where(hit, 1.0, group_on)

    cand = [jnp.where(group_on[g:g + 1] > 0.0, sel_blk[g], neg) for g in groups]
    picked = [jnp.zeros((GROUP_SIZE, tm), F32) for _ in groups]
    chosen = []
    for _ in range(TOP_K):
        m = jnp.max(functools.reduce(jnp.maximum, cand), axis=0, keepdims=True)
        first = functools.reduce(jnp.minimum, [jnp.where(cand[g] == m, sub + g * GROUP_SIZE, N_EXPERTS)
                                               for g in groups])
        first = jnp.min(first, axis=0, keepdims=True)
        chosen.append(first)
        for g in groups:
            hit = (sub + g * GROUP_SIZE) == first
            picked[g] = jnp.where(hit, 1.0, picked[g])
            cand[g] = jnp.where(hit, neg, cand[g])

    picked_all = jnp.concatenate(picked, axis=0)
    r_i = lax.broadcasted_iota(jnp.int32, (tm, tm), 0)
    c_i = lax.broadcasted_iota(jnp.int32, (tm, tm), 1)
    earlier = jnp.where(r_i < c_i, 1.0, 0.0).astype(BF16)
    before = jnp.dot(picked_all.astype(BF16), earlier, preferred_element_type=F32) + cnt_ref[:, 0:1]
    cnt_ref[...] = cnt_ref[...] + jnp.sum(picked_all, axis=1, keepdims=True)

    def pick_value(table, first):
        parts = [jnp.where((sub + g * GROUP_SIZE) == first, blk_of(table, g), 0.0) for g in groups]
        return jnp.sum(functools.reduce(jnp.add, parts), axis=0, keepdims=True)

    w_k = [pick_value(scores, f) for f in chosen]
    denom = functools.reduce(jnp.add, w_k)
    eidx = jnp.zeros((TOP_K, tm), jnp.int32)
    rank = jnp.zeros((TOP_K, tm), jnp.int32)
    wts = jnp.zeros((TOP_K, tm), F32)
    for k in range(TOP_K):
        eidx = jnp.where(sub == k, chosen[k], eidx)
        rank = jnp.where(sub == k, pick_value(before, chosen[k]).astype(jnp.int32), rank)
        wts = jnp.where(sub == k, w_k[k] / denom * ROUTE_SCALE, wts)
    eidx_ref[...] = eidx
    rank_ref[...] = rank
    pad = jnp.zeros((GATE_LANES - TOP_K, tm), F32)
    wts_ref[0] = jnp.concatenate([wts, pad], axis=0).T


def _outproj(oa, ob, x, wa, wb, g1, n2, sc2, sh2, wr_t, rb, tm):
    B, T, D = x.shape
    nt = T // tm
    const = lambda shape: pl.BlockSpec(shape, lambda b, t: (0,) * len(shape))
    tile = lambda w: pl.BlockSpec((1, tm, w), lambda b, t: (b, t, 0))
    per_batch = pl.BlockSpec((1, 1, D), lambda b, t: (b, 0, 0))
    picks = pl.BlockSpec((TOP_K, tm), lambda b, t: (0, b * nt + t))
    return pl.pallas_call(
        _outproj_kernel,
        grid=(B, nt),
        in_specs=[tile(GW), tile(GW), tile(D), const(wa.shape), const(wb.shape), per_batch,
                  const((1, D)), per_batch, per_batch, const(wr_t.shape), const(rb.shape)],
        out_specs=[tile(D), pl.BlockSpec((tm, ROW_TILE, LANES), lambda b, t: (b * nt + t, 0, 0)),
                   picks, picks, tile(GATE_LANES), const((N_EXPERTS, GATE_LANES))],
        out_shape=[jax.ShapeDtypeStruct((B, T, D), F32), jax.ShapeDtypeStruct((B * T, ROW_TILE, LANES), F32),
                   jax.ShapeDtypeStruct((TOP_K, B * T), jnp.int32), jax.ShapeDtypeStruct((TOP_K, B * T), jnp.int32),
                   jax.ShapeDtypeStruct((B, T, GATE_LANES), F32),
                   jax.ShapeDtypeStruct((N_EXPERTS, GATE_LANES), F32)],
        compiler_params=pltpu.CompilerParams(dimension_semantics=("arbitrary", "arbitrary"),
                                             vmem_limit_bytes=VMEM_LIMIT_BYTES),
        name="outproj",
    )(oa, ob, x, wa, wb, g1, n2, sc2, sh2, wr_t, rb)


def _slots_kernel(off_ref, eidx_ref, rank_ref, slot_ref):
    eidx = eidx_ref[...]

    def add_expert(e, acc):
        return acc + jnp.where(eidx == e, off_ref[e], 0)

    slot_ref[...] = lax.fori_loop(0, N_EXPERTS, add_expert, rank_ref[...])


def _slots(offsets, eidx, rank):
    k, m = eidx.shape
    tile = pl.BlockSpec((k, m), lambda i: (0, 0))
    return pl.pallas_call(
        _slots_kernel,
        grid=(1,),
        in_specs=[pl.BlockSpec(memory_space=pltpu.SMEM), tile, tile],
        out_specs=tile,
        out_shape=jax.ShapeDtypeStruct((k, m), jnp.int32),
        compiler_params=pltpu.CompilerParams(dimension_semantics=("arbitrary",),
                                             vmem_limit_bytes=VMEM_LIMIT_BYTES),
        name="slots",
    )(offsets, eidx, rank)


def _dispatch_kernel(slot_ref, h_ref, xs_ref, sem):
    tmd = h_ref.shape[0]

    def start_rows(j, c):
        for k in range(TOP_K):
            pltpu.make_async_copy(h_ref.at[pl.ds(j, 1)], xs_ref.at[pl.ds(slot_ref[k, j], 1)], sem).start(
                priority=k % 2)
        return c

    lax.fori_loop(0, tmd, start_rows, 0)
    for _ in range(TOP_K):
        pltpu.make_async_copy(h_ref, xs_ref.at[pl.ds(0, tmd)], sem).wait()


def _dispatch(slot, h2, n_rows, tmd):
    M = h2.shape[0]
    return pl.pallas_call(
        _dispatch_kernel,
        grid=(M // tmd,),
        in_specs=[pl.BlockSpec((TOP_K, tmd), lambda i: (0, i), memory_space=pltpu.SMEM),
                  pl.BlockSpec((tmd, ROW_TILE, LANES), lambda i: (i, 0, 0))],
        out_specs=pl.BlockSpec(memory_space=pl.ANY),
        out_shape=jax.ShapeDtypeStruct((n_rows, ROW_TILE, LANES), F32),
        scratch_shapes=[pltpu.SemaphoreType.DMA],
        compiler_params=pltpu.CompilerParams(dimension_semantics=("arbitrary",),
                                             vmem_limit_bytes=VMEM_LIMIT_BYTES),
        name="dispatch",
    )(slot, h2)


FFN_PIPELINE_STEPS = 2


def _ffn_kernel(te_ref, nu_ref, nv_ref, x_ref, wg_ref, wu_ref, wd_ref, y_ref, wgu_s, wd_s, xstd, ystd):
    i = pl.program_id(0)
    row = lax.broadcasted_iota(jnp.int32, (MOE_ROWS, 1), 0)
    tile_mm = jnp.maximum(i - 1, 0)
    tile_out = jnp.maximum(i - 2, 0)

    @pl.when(i == 0)
    def _():
        xstd[...] = jnp.zeros(xstd.shape, BF16)
        ystd[...] = jnp.zeros(ystd.shape, F32)

    @pl.when(i < nu_ref[0] + FFN_PIPELINE_STEPS)
    def _():
        @pl.when((i == 0) | (te_ref[tile_mm] != te_ref[tile_out]))
        def _():
            wgu_s[:, :D_EXPERT] = wg_ref[0].astype(BF16)
            wgu_s[:, D_EXPERT:] = wu_ref[0].astype(BF16)
            wd_s[...] = wd_ref[0].astype(BF16)

        cur = i % 2
        xstd[cur] = jnp.where(row < nv_ref[i], _to_rows(x_ref[...]), 0.0).astype(BF16)
        y_ref[...] = _to_row_tiles(ystd[cur])
        gu = jnp.dot(xstd[1 - cur], wgu_s[...], preferred_element_type=F32)
        act = _silu(gu[:, :D_EXPERT]) * gu[:, D_EXPERT:]
        ystd[1 - cur] = jnp.dot(act.astype(BF16), wd_s[...], preferred_element_type=F32)


def _ffn(tile_expert, n_used, n_valid, xs, wg, wu, wd):
    D = wg.shape[1]
    n_tiles = xs.shape[0] // MOE_ROWS
    block = (MOE_ROWS, ROW_TILE, LANES)
    expert = lambda i, te, nu, nv: (te[jnp.maximum(i - 1, 0)], 0, 0)
    return pl.pallas_call(
        _ffn_kernel,
        grid_spec=pltpu.PrefetchScalarGridSpec(
            num_scalar_prefetch=3,
            grid=(n_tiles + FFN_PIPELINE_STEPS,),
            in_specs=[pl.BlockSpec(block, lambda i, te, nu, nv: (jnp.minimum(i, nu[0] - 1), 0, 0)),
                      pl.BlockSpec((1, D, D_EXPERT), expert),
                      pl.BlockSpec((1, D, D_EXPERT), expert),
                      pl.BlockSpec((1, D_EXPERT, D), expert)],
            out_specs=pl.BlockSpec(block, lambda i, te, nu, nv: (jnp.clip(i - 2, 0, nu[0] - 1), 0, 0)),
            scratch_shapes=[pltpu.VMEM((D, 2 * D_EXPERT), BF16), pltpu.VMEM((D_EXPERT, D), BF16),
                            pltpu.VMEM((2, MOE_ROWS, D), BF16), pltpu.VMEM((2, MOE_ROWS, D), F32)]),
        out_shape=jax.ShapeDtypeStruct(xs.shape, F32),
        compiler_params=pltpu.CompilerParams(dimension_semantics=("arbitrary",),
                                             vmem_limit_bytes=VMEM_LIMIT_BYTES),
        name="ffn",
    )(tile_expert, n_used, n_valid, xs, wg, wu, wd)


def _combine_kernel(slot_ref, next_slot_ref, w_ref, h_ref, x1_ref, g2_ref, fw_ref, wsgu_ref, wsd_ref, y_ref,
                    o_ref, buf, sem):
    tmc = h_ref.shape[0]
    i = pl.program_id(0)
    last = pl.num_programs(0) - 1

    def start_rows(slots, b):
        def body(j, c):
            for k in range(TOP_K):
                pltpu.make_async_copy(y_ref.at[pl.ds(slots[k, j], 1)], buf.at[b, k, pl.ds(j, 1)], sem.at[b]).start(
                    priority=k % 2)
            return c
        lax.fori_loop(0, tmc, body, 0)

    def wait_rows(b):
        for k in range(TOP_K):
            pltpu.make_async_copy(y_ref.at[pl.ds(0, tmc)], buf.at[b, k], sem.at[b]).wait()

    def for_buffer(step, fn):
        for b in range(2):
            @pl.when(step % 2 == b)
            def _():
                fn(b)

    @pl.when(i == 0)
    def _():
        start_rows(slot_ref, 0)

    @pl.when(i < last)
    def _():
        for_buffer(i + 1, lambda b: start_rows(next_slot_ref, b))

    gu = jnp.dot(_to_rows(h_ref[...]).astype(BF16), wsgu_ref[...], preferred_element_type=F32)
    act = _silu(gu[:, :D_EXPERT]) * gu[:, D_EXPERT:]
    shared = jnp.dot(act.astype(BF16), wsd_ref[...], preferred_element_type=F32)
    w = w_ref[...]
    lane = lax.broadcasted_iota(jnp.int32, w.shape, 1)

    def finish(b):
        wait_rows(b)
        routed = jnp.zeros((tmc, ROW_TILE, LANES), F32)
        for k in range(TOP_K):
            wk = jnp.broadcast_to(_lane_pick(w, lane, k), (tmc, LANES))
            routed = routed + _to_row_tiles(jnp.concatenate([wk] * ROW_TILE, axis=1)) * buf[b, k]
        acc = shared + _to_rows(routed)
        y = x1_ref[...] + g2_ref[0] * acc
        o_ref[...] = _rms(y) * fw_ref[...]

    for_buffer(i, finish)


def _combine(slot, wts, h2, x1, g2, fw, wsgu, wsd, ys, T, tmc):
    M, D = x1.shape
    const = lambda shape: pl.BlockSpec(shape, lambda i: (0,) * len(shape))
    tile = lambda w: pl.BlockSpec((tmc, w), lambda i: (i, 0))
    row_tiled = pl.BlockSpec((tmc, ROW_TILE, LANES), lambda i: (i, 0, 0))
    n_steps = M // tmc
    return pl.pallas_call(
        _combine_kernel,
        grid=(n_steps,),
        in_specs=[pl.BlockSpec((TOP_K, tmc), lambda i: (0, i), memory_space=pltpu.SMEM),
                  pl.BlockSpec((TOP_K, tmc), lambda i: (0, jnp.minimum(i + 1, n_steps - 1)), memory_space=pltpu.SMEM),
                  tile(GATE_LANES), row_tiled, tile(D),
                  pl.BlockSpec((1, 1, D), lambda i: (i // (T // tmc), 0, 0)),
                  const((1, D)), const(wsgu.shape), const(wsd.shape),
                  pl.BlockSpec(memory_space=pl.ANY)],
        out_specs=tile(D),
        out_shape=jax.ShapeDtypeStruct((M, D), F32),
        scratch_shapes=[pltpu.VMEM((2, TOP_K, tmc, ROW_TILE, LANES), F32), pltpu.SemaphoreType.DMA((2,))],
        compiler_params=pltpu.CompilerParams(dimension_semantics=("arbitrary",),
                                             vmem_limit_bytes=VMEM_LIMIT_BYTES),
        name="combine",
    )(slot, slot, wts, h2, x1, g2, fw, wsgu, wsd, ys)


def _pick_tile(n, want):
    t = min(n, want)
    assert n % t == 0 and t % CHUNK == 0, (n, want)
    return t


def kernel(x, c, w_ada, b_ada, norm1_w, w_in, conv_w, gdn_a_log, gdn_dt_bias, gdn_norm_w, hg_lb, hg_norm_w,
           w_out, norm2_w, w_router, router_bias, w_gate, w_up, w_down, ws_gate, ws_up, ws_down, final_norm_w):
    B, T, D = x.shape
    M = B * T
    depth = w_ada.shape[0]
    assert depth == 1 and T % CHUNK == 0 and B <= 8
    layer = 0
    tt = _pick_tile(T, 512)

    c_pad = jnp.pad(c, ((0, 8 - B), (0, 0)))
    mod = _ada(c_pad, w_ada[layer], b_ada[layer].reshape(1, -1))[:B]
    sh1, sc1, g1, sh2, sc2, g2 = (m.reshape(B, 1, D) for m in jnp.split(mod, 6, axis=-1))

    w = w_in[layer]
    qkv_w = 3 * GW
    sizes = (GW, HEADS, HEADS, GW, GW, GW, GW)
    offs = [qkv_w]
    for s in sizes:
        offs.append(offs[-1] + s)
    seg = lambda i: w[:, offs[i]:offs[i + 1]]
    small = jnp.pad(jnp.concatenate([seg(1), seg(2)], axis=1), ((0, 0), (0, GATE_LANES - 2 * HEADS)))
    w_all = jnp.concatenate([w[:, :qkv_w], seg(0), seg(3), seg(4), seg(5), seg(6), small], axis=1).astype(BF16)
    lane_pad = lambda v: jnp.pad(v.astype(F32).reshape(1, HEADS), ((0, 0), (HEADS, GATE_LANES - 2 * HEADS)))
    idx = jnp.arange(CUMSUM_ROWS)
    tri = ((idx[:, None] >= idx[None, :]) & (idx[:, None] // CHUNK == idx[None, :] // CHUNK)).astype(BF16)

    qa, ka, va, ga, sm, bcum, kb, ib, qb, gb = _inproj(
        layer, x, norm1_w[layer].reshape(1, D), sc1, sh1, w_all, conv_w[layer].astype(F32),
        lane_pad(gdn_a_log[layer]), lane_pad(gdn_dt_bias[layer]), hg_lb.astype(F32), tri, tt)

    gct = sm[:, :, HEADS:2 * HEADS].transpose(0, 2, 1).reshape(B, HEADS, T // CHUNK, CHUNK)
    oa = _gdn(qa, ka, va, ga, sm, gct, gdn_norm_w[layer].reshape(1, DH), tt)
    ob = _hgrn(qb, kb, ib, bcum, gb, hg_norm_w[layer].reshape(1, GW), tt)

    wo = w_out[layer].astype(BF16)
    x1, h2, eidx, rank, wts, cnt = _outproj(oa, ob, x, wo[:GW], wo[GW:], g1, norm2_w[layer].reshape(1, D), sc2, sh2,
                                            w_router[layer].T, router_bias[layer].reshape(N_EXPERTS, 1), tt)

    counts = cnt[:, 0].astype(jnp.int32)
    padded = (counts + MOE_ROWS - 1) // MOE_ROWS * MOE_ROWS
    ends = jnp.cumsum(padded)
    offsets = ends - padded
    n_tiles = (M * TOP_K) // MOE_ROWS + N_EXPERTS
    n_used = (ends[-1] // MOE_ROWS).astype(jnp.int32)
    tile_ids = jnp.minimum(jnp.arange(n_tiles + FFN_PIPELINE_STEPS, dtype=jnp.int32), n_used - 1)
    tile_expert = jnp.sum(((ends // MOE_ROWS)[None, :] <= tile_ids[:, None]).astype(jnp.int32), axis=1)
    tile_expert = jnp.minimum(tile_expert, N_EXPERTS - 1)
    slot = _slots(offsets.astype(jnp.int32), eidx, rank)
    first_tile = jnp.take(offsets, tile_expert) // MOE_ROWS
    n_valid = jnp.clip(jnp.take(counts, tile_expert) - (tile_ids - first_tile) * MOE_ROWS, 0, MOE_ROWS)

    assert D == ROW_TILE * LANES
    xs = _dispatch(slot, h2, n_tiles * MOE_ROWS, _pick_tile(M, 512))
    ys = _ffn(tile_expert.astype(jnp.int32), n_used.reshape(1), n_valid.astype(jnp.int32), xs,
              w_gate[layer], w_up[layer], w_down[layer])
    wsgu = jnp.concatenate([ws_gate[layer], ws_up[layer]], axis=-1).astype(BF16)
    out = _combine(slot, wts.reshape(M, GATE_LANES), h2, x1.reshape(M, D), g2, final_norm_w.reshape(1, D),
                   wsgu, ws_down[layer].astype(BF16), ys, T, _pick_tile(T, 128))
    return out.reshape(B, T, D)
```

```python
import functools

import jax
import jax.numpy as jnp
from jax import lax
from jax.experimental import pallas as pl
from jax.experimental.pallas import tpu as pltpu

F32 = jnp.float32
BF16 = jnp.bfloat16

EPS = 1e-6
CHUNK = 64
SUB = 8
HEADS = 4
DH = 128
GW = HEADS * DH
CONV_K = 4
N_EXPERTS = 64
N_GROUPS = 8
GROUP_SIZE = N_EXPERTS // N_GROUPS
TOPK_GROUPS = 4
TOP_K = 8
D_EXPERT = 256
ROUTE_SCALE = 2.5
GATE_LANES = 128
GDN_CHUNKS_PER_ITER = 4
INV_BLOCK = 16
CUMSUM_ROWS = 128
MOE_ROWS = 1024
ROW_TILE, LANES = 8, 128

VMEM_LIMIT_BYTES = 56 * 1024 * 1024

ACT = BF16


def _silu(x):
    return x * jax.nn.sigmoid(x)


def _dot(a, b):
    return jnp.dot(a.astype(BF16), b.astype(BF16), preferred_element_type=F32)


def _dot_nt(a, b):
    return lax.dot_general(a.astype(BF16), b.astype(BF16), (((1,), (1,)), ((), ())),
                           preferred_element_type=F32)


def _dot_tn(a, b):
    return lax.dot_general(a.astype(BF16), b.astype(BF16), (((0,), (0,)), ((), ())),
                           preferred_element_type=F32)


def _split2(x):
    hi = x.astype(BF16)
    lo = (x - hi.astype(F32)).astype(BF16)
    return hi, lo


def _dot3(a, b, dot=_dot):
    ah, al = _split2(a)
    bh, bl = _split2(b)
    return dot(ah, bh) + dot(ah, bl) + dot(al, bh)


def _cumsum_rows(tri, x):
    hi = x.astype(BF16)
    r = x - hi.astype(F32)
    mid = r.astype(BF16)
    lo = (r - mid.astype(F32)).astype(BF16)
    g = tri.shape[0]
    groups = []
    for r0 in range(0, x.shape[0], g):
        rows = slice(r0, r0 + g)
        groups.append(jnp.dot(tri, hi[rows], preferred_element_type=F32)
                      + jnp.dot(tri, mid[rows], preferred_element_type=F32)
                      + jnp.dot(tri, lo[rows], preferred_element_type=F32))
    return jnp.concatenate(groups, axis=0)


def _lane_pick(tile, lane, idx):
    return jnp.sum(jnp.where(lane == idx, tile, 0.0), axis=1, keepdims=True)


def _rms(x):
    return x * lax.rsqrt(jnp.mean(x * x, axis=-1, keepdims=True) + EPS)


def _to_rows(x3):
    r = x3.shape[0]
    xt = jnp.swapaxes(x3.reshape(r // ROW_TILE, ROW_TILE, ROW_TILE, LANES), 1, 2)
    return jnp.concatenate([xt[:, s].reshape(r, LANES) for s in range(ROW_TILE)], axis=1)


def _to_row_tiles(x):
    r = x.shape[0]
    xt = jnp.stack([x[:, s * LANES:(s + 1) * LANES].reshape(r // ROW_TILE, ROW_TILE, LANES) for s in range(ROW_TILE)],
                   axis=1)
    return jnp.swapaxes(xt, 1, 2).reshape(r, ROW_TILE, LANES)


def _ada_kernel(c_ref, w_ref, b_ref, o_ref):
    ca = _silu(c_ref[...])
    o_ref[...] = _dot3(ca, w_ref[...]) + b_ref[...]


def _ada(c_pad, w, b):
    rows, d = c_pad.shape
    n = w.shape[1]
    tn = 1024
    return pl.pallas_call(
        _ada_kernel,
        grid=(n // tn,),
        in_specs=[pl.BlockSpec((rows, d), lambda j: (0, 0)),
                  pl.BlockSpec((d, tn), lambda j: (0, j)),
                  pl.BlockSpec((1, tn), lambda j: (0, j))],
        out_specs=pl.BlockSpec((rows, tn), lambda j: (0, j)),
        out_shape=jax.ShapeDtypeStruct((rows, n), F32),
        compiler_params=pltpu.CompilerParams(dimension_semantics=("arbitrary",),
                                             vmem_limit_bytes=VMEM_LIMIT_BYTES),
        name="ada",
    )(c_pad, w, b)


def _inproj_kernel(layer, x_ref, n1_ref, sc_ref, sh_ref, w_ref, cw_ref, alog_ref, dt_ref, lb_ref, tri_ref,
                   qa_ref, ka_ref, va_ref, ga_ref, sm_ref, b_ref, kb_ref, ib_ref, qb_ref, gb_ref,
                   pbuf):
    tt = x_ref.shape[1]
    t = pl.program_id(1)

    h = _rms(x_ref[0]) * n1_ref[...]
    h = h * (1.0 + sc_ref[0]) + sh_ref[0]
    hb = h.astype(BF16)

    def proj(g, width=GW):
        return jnp.dot(hb, w_ref[:, g * GW:g * GW + width], preferred_element_type=F32)

    @pl.when(t == 0)
    def _():
        pbuf[:, 0:8, :] = jnp.zeros((3, 8, GW), F32)

    for g, out_ref in enumerate((qa_ref, ka_ref, va_ref)):
        cols = slice(g * GW, (g + 1) * GW)
        p = proj(g)
        pbuf[g, 8:8 + tt, :] = p
        y = p * cw_ref[CONV_K - 1:CONV_K, cols]
        for j in range(1, CONV_K):
            y = y + pbuf[g, 8 - j:8 - j + tt, :] * cw_ref[CONV_K - 1 - j:CONV_K - j, cols]
        pbuf[g, 0:8, :] = pbuf[g, tt:tt + 8, :]
        y = _silu(y)
        if g == 2:
            out_ref[0] = y.astype(out_ref.dtype)
        else:
            scale = DH ** -0.5 if g == 0 else 1.0
            for hh in range(HEADS):
                hs = slice(hh * DH, (hh + 1) * DH)
                yh = y[:, hs]
                inv = lax.rsqrt(jnp.sum(yh * yh, axis=-1, keepdims=True) + EPS)
                out_ref[0, :, hs] = (yh * inv * scale).astype(out_ref.dtype)

    ga_ref[0] = _silu(proj(3)).astype(ga_ref.dtype)

    ps = proj(8, GATE_LANES)
    lane = lax.broadcasted_iota(jnp.int32, ps.shape, 1)
    beta = jax.nn.sigmoid(ps)
    z = ps + dt_ref[...]
    softplus = jnp.maximum(z, 0.0) + jnp.log1p(jnp.exp(-jnp.abs(z)))
    g_log = -jnp.exp(alog_ref[...]) * softplus
    tri = tri_ref[...]
    gc = _cumsum_rows(tri, jnp.where((lane >= HEADS) & (lane < 2 * HEADS), g_log, 0.0))
    sm_ref[0] = jnp.where(lane < HEADS, beta, gc)

    hl = lb_ref[...]
    e = jnp.exp(hl - jnp.max(hl, axis=0, keepdims=True))
    lb = jnp.sum(e[0:layer + 1], axis=0, keepdims=True) / jnp.sum(e, axis=0, keepdims=True)
    fr = proj(4)
    logf = jnp.log(lb + (1.0 - lb) * jax.nn.sigmoid(fr))
    b_ref[0] = _cumsum_rows(tri, logf)
    kb_ref[0] = ((1.0 - lb) * jax.nn.sigmoid(-fr)).astype(kb_ref.dtype)
    ib_ref[0] = proj(5).astype(ib_ref.dtype)
    qb_ref[0] = _silu(proj(6)).astype(qb_ref.dtype)
    gb_ref[0] = _silu(proj(7)).astype(gb_ref.dtype)


def _inproj(layer, x, n1, sc1, sh1, w_all, conv_w, alog_pad, dt_pad, hg_lb, tri, tt):
    B, T, D = x.shape
    const = lambda shape: pl.BlockSpec(shape, lambda b, t: (0,) * len(shape))
    act = lambda dt: jax.ShapeDtypeStruct((B, T, GW), dt)
    tile = lambda w: pl.BlockSpec((1, tt, w), lambda b, t: (b, t, 0))
    per_batch = pl.BlockSpec((1, 1, D), lambda b, t: (b, 0, 0))
    return pl.pallas_call(
        functools.partial(_inproj_kernel, layer),
        grid=(B, T // tt),
        in_specs=[tile(D), const((1, D)), per_batch, per_batch,
                  const(w_all.shape), const(conv_w.shape), const((1, GATE_LANES)), const((1, GATE_LANES)),
                  const(hg_lb.shape), const(tri.shape)],
        out_specs=[tile(GW), tile(GW), tile(GW), tile(GW), tile(GATE_LANES), tile(GW),
                   tile(GW), tile(GW), tile(GW), tile(GW)],
        out_shape=[act(ACT), act(ACT), act(ACT), act(ACT),
                   jax.ShapeDtypeStruct((B, T, GATE_LANES), F32), act(F32),
                   act(ACT), act(ACT), act(ACT), act(ACT)],
        scratch_shapes=[pltpu.VMEM((3, tt + 8, GW), F32)],
        compiler_params=pltpu.CompilerParams(dimension_semantics=("arbitrary", "arbitrary"),
                                             vmem_limit_bytes=VMEM_LIMIT_BYTES),
        name="inproj",
    )(x, n1, sc1, sh1, w_all, conv_w, alog_pad, dt_pad, hg_lb, tri)


def _gdn_prep_kernel(q_ref, k_ref, v_ref, sm_ref, gct_ref, o_ref, qt_ref, m_ref, n_ref):
    tt = q_ref.shape[1]
    nc = tt // CHUNK
    row = lax.broadcasted_iota(jnp.int32, (CHUNK, CHUNK), 0)
    col = lax.broadcasted_iota(jnp.int32, (CHUNK, CHUNK), 1)
    causal = row >= col
    diag_blk = (row > col) & (row // INV_BLOCK == col // INV_BLOCK)
    off_blk = row // INV_BLOCK > col // INV_BLOCK
    eye = jnp.where(row == col, 1.0, 0.0)
    lane = lax.broadcasted_iota(jnp.int32, (CHUNK, GATE_LANES), 1)
    assert INV_BLOCK == 16 and CHUNK == 4 * INV_BLOCK

    def body(i, carry):
        chains = [(GDN_CHUNKS_PER_ITER * i + j, hh) for j in range(GDN_CHUNKS_PER_ITER) for hh in range(HEADS)]
        rows = [pl.ds(pl.multiple_of(c * CHUNK, CHUNK), CHUNK) for c, _ in chains]
        hs = [slice(hh * DH, (hh + 1) * DH) for _, hh in chains]
        n = range(len(chains))
        sm = [sm_ref[0, rows[j], :] for j in n]
        q = [q_ref[0, rows[j], hs[j]].astype(F32) for j in n]
        k = [k_ref[0, rows[j], hs[j]].astype(F32) for j in n]
        v = [v_ref[0, rows[j], hs[j]].astype(F32) for j in n]
        beta = [_lane_pick(sm[j], lane, chains[j][1]) for j in n]
        gcol = [_lane_pick(sm[j], lane, HEADS + chains[j][1]) for j in n]
        grow = [gct_ref[0, hh, pl.ds(c, 1), :] for c, hh in chains]
        decay = [jnp.exp(jnp.where(causal, gcol[j] - grow[j], -jnp.inf)) for j in n]
        kb = [k[j] * beta[j] for j in n]
        L = [_dot_nt(kb[j], k[j]) * decay[j] for j in n]
        dg = [jnp.where(diag_blk, L[j], 0.0) for j in n]
        off = [jnp.where(off_blk, L[j], 0.0) for j in n]
        dinv = [eye - dg[j] for j in n]
        pw = [_dot3(dg[j], dg[j]) for j in n]
        for _ in range(2):
            dinv = [dinv[j] + _dot3(dinv[j], pw[j]) for j in n]
            pw = [_dot3(pw[j], pw[j]) for j in n]
        dinv = [dinv[j] + _dot3(dinv[j], pw[j]) for j in n]
        f1 = [_dot(dinv[j], off[j]) for j in n]
        f2 = [_dot(f1[j], f1[j]) for j in n]
        f3 = [_dot(f1[j], f2[j]) for j in n]
        tinv = [_dot(eye - f1[j] + f2[j] - f3[j], dinv[j]) for j in n]
        eg = [jnp.exp(gcol[j]) for j in n]
        sol = [_dot(tinv[j], jnp.concatenate([v[j] * beta[j], kb[j] * eg[j]], axis=1)) for j in n]
        attn = [_dot_nt(q[j], k[j]) * decay[j] for j in n]
        k_tail = [k[j] * jnp.exp(gcol[j][CHUNK - 1:CHUNK, :] - gcol[j]) for j in n]
        au = [_dot(attn[j], sol[j]) for j in n]
        ku = [_dot_tn(k_tail[j], sol[j]) for j in n]
        for j, (c, hh) in enumerate(chains):
            o_ref[0, rows[j], hs[j]] = au[j][:, :DH]
            qt_ref[0, rows[j], hs[j]] = (q[j] * eg[j] - au[j][:, DH:]).astype(qt_ref.dtype)
            n_ref[0, hh, c] = ku[j][:, :DH].astype(n_ref.dtype)
            m_ref[0, hh, c] = (-ku[j][:, DH:]).astype(m_ref.dtype)
        return carry

    lax.fori_loop(0, nc // GDN_CHUNKS_PER_ITER, body, 0)


def _gdn_scan_kernel(o_ref, qt_ref, m_ref, n_ref, gct_ref, sg_ref, nw_ref, out_ref, s_ref):
    nb, tt = o_ref.shape[0], o_ref.shape[1]
    nc = tt // CHUNK

    @pl.when(pl.program_id(0) == 0)
    def _():
        s_ref[...] = jnp.zeros(s_ref.shape, F32)

    nw = nw_ref[...]

    def body(c, carry):
        rows = pl.ds(pl.multiple_of(c * CHUNK, CHUNK), CHUNK)
        for b in range(nb):
            for hh in range(HEADS):
                hs = slice(hh * DH, (hh + 1) * DH)
                S = s_ref[b, hh]
                Sb = S.astype(BF16)
                glast = gct_ref[b, hh, pl.ds(c, 1), :][:, CHUNK - 1:CHUNK]
                o = o_ref[b, rows, hs] + jnp.dot(qt_ref[b, rows, hs], Sb, preferred_element_type=F32)
                s_ref[b, hh] = (S * jnp.exp(glast) + jnp.dot(m_ref[b, hh, c], Sb, preferred_element_type=F32)
                                + n_ref[b, hh, c].astype(F32))
                o = _rms(o) * nw * sg_ref[b, rows, hs].astype(F32)
                out_ref[b, rows, hs] = o.astype(out_ref.dtype)
        return carry

    lax.fori_loop(0, nc, body, 0)


def _gdn(q, k, v, sg, sm, gct, nw, tt):
    B, T, _ = q.shape
    nc = tt // CHUNK
    assert nc % GDN_CHUNKS_PER_ITER == 0
    n_chunks = T // CHUNK
    tile = lambda w: pl.BlockSpec((1, tt, w), lambda b, t: (b, t, 0))
    mat = jax.ShapeDtypeStruct((B, HEADS, n_chunks, DH, DH), ACT)
    o_part, qt, m, n = pl.pallas_call(
        _gdn_prep_kernel,
        grid=(B, T // tt),
        in_specs=[tile(GW), tile(GW), tile(GW), tile(GATE_LANES),
                  pl.BlockSpec((1, HEADS, nc, CHUNK), lambda b, t: (b, 0, t, 0))],
        out_specs=[tile(GW), tile(GW),
                   pl.BlockSpec((1, HEADS, nc, DH, DH), lambda b, t: (b, 0, t, 0, 0)),
                   pl.BlockSpec((1, HEADS, nc, DH, DH), lambda b, t: (b, 0, t, 0, 0))],
        out_shape=[jax.ShapeDtypeStruct((B, T, GW), F32), jax.ShapeDtypeStruct((B, T, GW), ACT), mat, mat],
        compiler_params=pltpu.CompilerParams(dimension_semantics=("arbitrary", "arbitrary"),
                                             vmem_limit_bytes=VMEM_LIMIT_BYTES),
        name="gdn_prep",
    )(q, k, v, sm, gct)

    full = lambda w: pl.BlockSpec((B, tt, w), lambda t: (0, t, 0))
    mats = pl.BlockSpec((B, HEADS, nc, DH, DH), lambda t: (0, 0, t, 0, 0))
    return pl.pallas_call(
        _gdn_scan_kernel,
        grid=(T // tt,),
        in_specs=[full(GW), full(GW), mats, mats,
                  pl.BlockSpec((B, HEADS, nc, CHUNK), lambda t: (0, 0, t, 0)),
                  full(GW), pl.BlockSpec((1, DH), lambda t: (0, 0))],
        out_specs=full(GW),
        out_shape=jax.ShapeDtypeStruct((B, T, GW), ACT),
        scratch_shapes=[pltpu.VMEM((B, HEADS, DH, DH), F32)],
        compiler_params=pltpu.CompilerParams(dimension_semantics=("arbitrary",),
                                             vmem_limit_bytes=VMEM_LIMIT_BYTES),
        name="gdn_scan",
    )(o_part, qt, m, n, gct, sg, nw)


def _hgrn_kernel(q_ref, k_ref, v_ref, b_ref, sg_ref, nw_ref, o_ref, st_ref):
    tt = q_ref.shape[1]
    nc = tt // CHUNK

    @pl.when(pl.program_id(1) == 0)
    def _():
        st_ref[...] = jnp.zeros(st_ref.shape, F32)

    row = lax.broadcasted_iota(jnp.int32, (CHUNK, CHUNK), 0)
    col = lax.broadcasted_iota(jnp.int32, (CHUNK, CHUNK), 1)
    diag_block = ((col // SUB) == (row // SUB)) & (col <= row)
    heads = range(HEADS)
    hs = [slice(hh * DH, (hh + 1) * DH) for hh in heads]

    def body(c, carry):
        rows = pl.ds(pl.multiple_of(c * CHUNK, CHUNK), CHUNK)
        q = [q_ref[0, rows, hs[h]].astype(F32) for h in heads]
        k = [k_ref[0, rows, hs[h]].astype(F32) for h in heads]
        v = [v_ref[0, rows, hs[h]].astype(F32) for h in heads]
        b = [b_ref[0, rows, hs[h]] for h in heads]
        blast = [b[h][CHUNK - 1:CHUNK, :] for h in heads]
        st = [st_ref[h] for h in heads]
        o = [_dot_nt(q[h] * jnp.exp(b[h]), st[h]) for h in heads]
        k_tail = [k[h] * jnp.exp(blast[h] - b[h]) for h in heads]
        for h in heads:
            st_ref[h] = st[h] * jnp.exp(blast[h]) + _dot_tn(v[h], k_tail[h])

        blocks = [[jnp.zeros((SUB, CHUNK), F32)] for _ in heads]
        for i in range(1, CHUNK // SUB):
            lo, hi = i * SUB, (i + 1) * SUB
            for h in heads:
                r = b[h][lo:lo + 1, :]
                qi = q[h][lo:hi] * jnp.exp(b[h][lo:hi] - r)
                kj = k[h][:lo] * jnp.exp(jnp.minimum(r - b[h][:lo], 0.0))
                kj = jnp.concatenate([kj, jnp.zeros((CHUNK - lo, DH), F32)], axis=0)
                blocks[h].append(_dot_nt(qi, kj))
        a = []
        for h in heads:
            f = jnp.exp(jnp.minimum(b[h] - pltpu.roll(b[h], 1, 0), 0.0))
            e = None
            a_diag = jnp.zeros((CHUNK, CHUNK), F32)
            for delta in range(SUB):
                if delta == 0:
                    term = q[h] * k[h]
                else:
                    fsh = f if delta == 1 else pltpu.roll(f, delta - 1, 0)
                    e = fsh if e is None else e * fsh
                    term = q[h] * pltpu.roll(k[h], delta, 0) * e
                colv = jnp.sum(term, axis=1, keepdims=True)
                a_diag = jnp.where(row - col == delta, colv, a_diag)
            a.append(jnp.where(diag_block, a_diag, jnp.concatenate(blocks[h], axis=0)))

        o = [o[h] + _dot(a[h], v[h]) for h in heads]
        o = jnp.concatenate(o, axis=1)
        o = _rms(o) * nw_ref[...] * sg_ref[0, rows, :].astype(F32)
        o_ref[0, rows, :] = o.astype(o_ref.dtype)
        return carry

    lax.fori_loop(0, nc, body, 0)


def _hgrn(q, k, v, b, sg, nw, tt):
    B, T, _ = q.shape
    tile = pl.BlockSpec((1, tt, GW), lambda bi, t: (bi, t, 0))
    return pl.pallas_call(
        _hgrn_kernel,
        grid=(B, T // tt),
        in_specs=[tile, tile, tile, tile, tile, pl.BlockSpec((1, GW), lambda bi, t: (0, 0))],
        out_specs=tile,
        out_shape=jax.ShapeDtypeStruct((B, T, GW), ACT),
        scratch_shapes=[pltpu.VMEM((HEADS, DH, DH), F32)],
        compiler_params=pltpu.CompilerParams(dimension_semantics=("arbitrary", "arbitrary"),
                                             vmem_limit_bytes=VMEM_LIMIT_BYTES),
        name="hgrn",
    )(q, k, v, b, sg, nw)


def _outproj_kernel(oa_ref, ob_ref, x_ref, wa_ref, wb_ref, g1_ref, n2_ref, sc_ref, sh_ref, wr_ref, rb_ref,
                    x1_ref, h2_ref, eidx_ref, rank_ref, wts_ref, cnt_ref):
    tm = x_ref.shape[1]

    @pl.when((pl.program_id(0) == 0) & (pl.program_id(1) == 0))
    def _():
        cnt_ref[...] = jnp.zeros(cnt_ref.shape, F32)

    mix = (jnp.dot(oa_ref[0], wa_ref[...], preferred_element_type=F32)
           + jnp.dot(ob_ref[0], wb_ref[...], preferred_element_type=F32))
    x1 = x_ref[0] + g1_ref[0] * mix
    x1_ref[0] = x1
    h2 = _rms(x1) * n2_ref[...]
    h2 = h2 * (1.0 + sc_ref[0]) + sh_ref[0]
    h2_ref[...] = _to_row_tiles(h2).astype(h2_ref.dtype)

    scores = jax.nn.sigmoid(_dot3(wr_ref[...], h2, dot=_dot_nt))
    sel = scores + rb_ref[...]
    sub = lax.broadcasted_iota(jnp.int32, (GROUP_SIZE, tm), 0)
    neg = -jnp.inf
    groups = range(N_GROUPS)

    def take_max(blk):
        m = jnp.max(blk, axis=0, keepdims=True)
        first = jnp.min(jnp.where(blk == m, sub, GROUP_SIZE), axis=0, keepdims=True)
        hit = sub == first
        return m, hit, jnp.where(hit, neg, blk)

    blk_of = lambda a, g: a[g * GROUP_SIZE:(g + 1) * GROUP_SIZE]
    sel_blk = [blk_of(sel, g) for g in groups]
    group_score = jnp.zeros((N_GROUPS, tm), F32)
    for g in groups:
        m1, _, rest = take_max(sel_blk[g])
        m2 = jnp.max(rest, axis=0, keepdims=True)
        group_score = jnp.where(sub == g, m1 + m2, group_score)
    group_on = jnp.zeros((N_GROUPS, tm), F32)
    for _ in range(TOPK_GROUPS):
        _, hit, group_score = take_max(group_score)
        group_on = jnp.where(hit, 1.0, group_on)

    cand = [jnp.where(group_on[g:g + 1] > 0.0, sel_blk[g], neg) for g in groups]
    picked = [jnp.zeros((GROUP_SIZE, tm), F32) for _ in groups]
    chosen = []
    for _ in range(TOP_K):
        m = jnp.max(functools.reduce(jnp.maximum, cand), axis=0, keepdims=True)
        first = functools.reduce(jnp.minimum, [jnp.where(cand[g] == m, sub + g * GROUP_SIZE, N_EXPERTS)
                                               for g in groups])
        first = jnp.min(first, axis=0, keepdims=True)
        chosen.append(first)
        for g in groups:
            hit = (sub + g * GROUP_SIZE) == first
            picked[g] = jnp.where(hit, 1.0, picked[g])
            cand[g] = jnp.where(hit, neg, cand[g])

    picked_all = jnp.concatenate(picked, axis=0)
    r_i = lax.broadcasted_iota(jnp.int32, (tm, tm), 0)
    c_i = lax.broadcasted_iota(jnp.int32, (tm, tm), 1)
    earlier = jnp.where(r_i < c_i, 1.0, 0.0).astype(BF16)
    before = jnp.dot(picked_all.astype(BF16), earlier, preferred_element_type=F32) + cnt_ref[:, 0:1]
    cnt_ref[...] = cnt_ref[...] + jnp.sum(picked_all, axis=1, keepdims=True)

    def pick_value(table, first):
        parts = [jnp.where((sub + g * GROUP_SIZE) == first, blk_of(table, g), 0.0) for g in groups]
        return jnp.sum(functools.reduce(jnp.add, parts), axis=0, keepdims=True)

    w_k = [pick_value(scores, f) for f in chosen]
    denom = functools.reduce(jnp.add, w_k)
    eidx = jnp.zeros((TOP_K, tm), jnp.int32)
    rank = jnp.zeros((TOP_K, tm), jnp.int32)
    wts = jnp.zeros((TOP_K, tm), F32)
    for k in range(TOP_K):
        eidx = jnp.where(sub == k, chosen[k], eidx)
        rank = jnp.where(sub == k, pick_value(before, chosen[k]).astype(jnp.int32), rank)
        wts = jnp.where(sub == k, w_k[k] / denom * ROUTE_SCALE, wts)
    eidx_ref[...] = eidx
    rank_ref[...] = rank
    pad = jnp.zeros((GATE_LANES - TOP_K, tm), F32)
    wts_ref[0] = jnp.concatenate([wts, pad], axis=0).T


def _outproj(oa, ob, x, wa, wb, g1, n2, sc2, sh2, wr_t, rb, tm):
    B, T, D = x.shape
    nt = T // tm
    const = lambda shape: pl.BlockSpec(shape, lambda b, t: (0,) * len(shape))
    tile = lambda w: pl.BlockSpec((1, tm, w), lambda b, t: (b, t, 0))
    per_batch = pl.BlockSpec((1, 1, D), lambda b, t: (b, 0, 0))
    picks = pl.BlockSpec((TOP_K, tm), lambda b, t: (0, b * nt + t))
    return pl.pallas_call(
        _outproj_kernel,
        grid=(B, nt),
        in_specs=[tile(GW), tile(GW), tile(D), const(wa.shape), const(wb.shape), per_batch,
                  const((1, D)), per_batch, per_batch, const(wr_t.shape), const(rb.shape)],
        out_specs=[tile(D), pl.BlockSpec((tm, ROW_TILE, LANES), lambda b, t: (b * nt + t, 0, 0)),
                   picks, picks, tile(GATE_LANES), const((N_EXPERTS, GATE_LANES))],
        out_shape=[jax.ShapeDtypeStruct((B, T, D), F32), jax.ShapeDtypeStruct((B * T, ROW_TILE, LANES), ACT),
                   jax.ShapeDtypeStruct((TOP_K, B * T), jnp.int32), jax.ShapeDtypeStruct((TOP_K, B * T), jnp.int32),
                   jax.ShapeDtypeStruct((B, T, GATE_LANES), F32),
                   jax.ShapeDtypeStruct((N_EXPERTS, GATE_LANES), F32)],
        compiler_params=pltpu.CompilerParams(dimension_semantics=("arbitrary", "arbitrary"),
                                             vmem_limit_bytes=VMEM_LIMIT_BYTES),
        name="outproj",
    )(oa, ob, x, wa, wb, g1, n2, sc2, sh2, wr_t, rb)


def _slots_kernel(off_ref, eidx_ref, rank_ref, slot_ref):
    eidx = eidx_ref[...]

    def add_expert(e, acc):
        return acc + jnp.where(eidx == e, off_ref[e], 0)

    slot_ref[...] = lax.fori_loop(0, N_EXPERTS, add_expert, rank_ref[...])


def _slots(offsets, eidx, rank):
    k, m = eidx.shape
    tile = pl.BlockSpec((k, m), lambda i: (0, 0))
    return pl.pallas_call(
        _slots_kernel,
        grid=(1,),
        in_specs=[pl.BlockSpec(memory_space=pltpu.SMEM), tile, tile],
        out_specs=tile,
        out_shape=jax.ShapeDtypeStruct((k, m), jnp.int32),
        compiler_params=pltpu.CompilerParams(dimension_semantics=("arbitrary",),
                                             vmem_limit_bytes=VMEM_LIMIT_BYTES),
        name="slots",
    )(offsets, eidx, rank)


def _dispatch_kernel(slot_ref, h_ref, xs_ref, sem):
    tmd = h_ref.shape[0]

    def start_rows(j, c):
        for k in range(TOP_K):
            pltpu.make_async_copy(h_ref.at[pl.ds(j, 1)], xs_ref.at[pl.ds(slot_ref[k, j], 1)], sem).start(
                priority=k % 2)
        return c

    lax.fori_loop(0, tmd, start_rows, 0)
    for _ in range(TOP_K):
        pltpu.make_async_copy(h_ref, xs_ref.at[pl.ds(0, tmd)], sem).wait()


def _dispatch(slot, h2, n_rows, tmd):
    M = h2.shape[0]
    return pl.pallas_call(
        _dispatch_kernel,
        grid=(M // tmd,),
        in_specs=[pl.BlockSpec((TOP_K, tmd), lambda i: (0, i), memory_space=pltpu.SMEM),
                  pl.BlockSpec((tmd, ROW_TILE, LANES), lambda i: (i, 0, 0))],
        out_specs=pl.BlockSpec(memory_space=pl.ANY),
        out_shape=jax.ShapeDtypeStruct((n_rows, ROW_TILE, LANES), h2.dtype),
        scratch_shapes=[pltpu.SemaphoreType.DMA],
        compiler_params=pltpu.CompilerParams(dimension_semantics=("arbitrary",),
                                             vmem_limit_bytes=VMEM_LIMIT_BYTES),
        name="dispatch",
    )(slot, h2)


FFN_PIPELINE_STEPS = 2


def _ffn_kernel(te_ref, nu_ref, nv_ref, x_ref, wg_ref, wu_ref, wd_ref, y_ref, wgu_s, wd_s, xstd, ystd):
    i = pl.program_id(0)
    row = lax.broadcasted_iota(jnp.int32, (MOE_ROWS, 1), 0)
    tile_mm = jnp.maximum(i - 1, 0)
    tile_out = jnp.maximum(i - 2, 0)

    @pl.when(i == 0)
    def _():
        xstd[...] = jnp.zeros(xstd.shape, BF16)
        ystd[...] = jnp.zeros(ystd.shape, F32)

    @pl.when(i < nu_ref[0] + FFN_PIPELINE_STEPS)
    def _():
        @pl.when((i == 0) | (te_ref[tile_mm] != te_ref[tile_out]))
        def _():
            wgu_s[:, :D_EXPERT] = wg_ref[0].astype(BF16)
            wgu_s[:, D_EXPERT:] = wu_ref[0].astype(BF16)
            wd_s[...] = wd_ref[0].astype(BF16)

        cur = i % 2
        xstd[cur] = jnp.where(row < nv_ref[i], _to_rows(x_ref[...].astype(F32)), 0.0).astype(BF16)
        y_ref[...] = _to_row_tiles(ystd[cur]).astype(y_ref.dtype)
        gu = jnp.dot(xstd[1 - cur], wgu_s[...], preferred_element_type=F32)
        act = _silu(gu[:, :D_EXPERT]) * gu[:, D_EXPERT:]
        ystd[1 - cur] = jnp.dot(act.astype(BF16), wd_s[...], preferred_element_type=F32)


def _ffn(tile_expert, n_used, n_valid, xs, wg, wu, wd):
    D = wg.shape[1]
    n_tiles = xs.shape[0] // MOE_ROWS
    block = (MOE_ROWS, ROW_TILE, LANES)
    expert = lambda i, te, nu, nv: (te[jnp.maximum(i - 1, 0)], 0, 0)
    return pl.pallas_call(
        _ffn_kernel,
        grid_spec=pltpu.PrefetchScalarGridSpec(
            num_scalar_prefetch=3,
            grid=(n_tiles + FFN_PIPELINE_STEPS,),
            in_specs=[pl.BlockSpec(block, lambda i, te, nu, nv: (jnp.minimum(i, nu[0] - 1), 0, 0)),
                      pl.BlockSpec((1, D, D_EXPERT), expert),
                      pl.BlockSpec((1, D, D_EXPERT), expert),
                      pl.BlockSpec((1, D_EXPERT, D), expert)],
            out_specs=pl.BlockSpec(block, lambda i, te, nu, nv: (jnp.clip(i - 2, 0, nu[0] - 1), 0, 0)),
            scratch_shapes=[pltpu.VMEM((D, 2 * D_EXPERT), BF16), pltpu.VMEM((D_EXPERT, D), BF16),
                            pltpu.VMEM((2, MOE_ROWS, D), BF16), pltpu.VMEM((2, MOE_ROWS, D), F32)]),
        out_shape=jax.ShapeDtypeStruct(xs.shape, xs.dtype),
        compiler_params=pltpu.CompilerParams(dimension_semantics=("arbitrary",),
                                             vmem_limit_bytes=VMEM_LIMIT_BYTES),
        name="ffn",
    )(tile_expert, n_used, n_valid, xs, wg, wu, wd)


def _combine_kernel(slot_ref, next_slot_ref, w_ref, h_ref, x1_ref, g2_ref, fw_ref, wsgu_ref, wsd_ref, y_ref,
                    o_ref, buf, sem):
    tmc = h_ref.shape[0]
    i = pl.program_id(0)
    last = pl.num_programs(0) - 1

    def start_rows(slots, b):
        def body(j, c):
            for k in range(TOP_K):
                pltpu.make_async_copy(y_ref.at[pl.ds(slots[k, j], 1)], buf.at[b, k, pl.ds(j, 1)], sem.at[b]).start(
                    priority=k % 2)
            return c
        lax.fori_loop(0, tmc, body, 0)

    def wait_rows(b):
        for k in range(TOP_K):
            pltpu.make_async_copy(y_ref.at[pl.ds(0, tmc)], buf.at[b, k], sem.at[b]).wait()

    def for_buffer(step, fn):
        for b in range(2):
            @pl.when(step % 2 == b)
            def _():
                fn(b)

    @pl.when(i == 0)
    def _():
        start_rows(slot_ref, 0)

    @pl.when(i < last)
    def _():
        for_buffer(i + 1, lambda b: start_rows(next_slot_ref, b))

    gu = jnp.dot(_to_rows(h_ref[...].astype(F32)).astype(BF16), wsgu_ref[...], preferred_element_type=F32)
    act = _silu(gu[:, :D_EXPERT]) * gu[:, D_EXPERT:]
    shared = jnp.dot(act.astype(BF16), wsd_ref[...], preferred_element_type=F32)
    w = w_ref[...]
    lane = lax.broadcasted_iota(jnp.int32, w.shape, 1)

    def finish(b):
        wait_rows(b)
        routed = jnp.zeros((tmc, ROW_TILE, LANES), F32)
        for k in range(TOP_K):
            wk = jnp.broadcast_to(_lane_pick(w, lane, k), (tmc, LANES))
            routed = routed + _to_row_tiles(jnp.concatenate([wk] * ROW_TILE, axis=1)) * buf[b, k].astype(F32)
        acc = shared + _to_rows(routed)
        y = x1_ref[...] + g2_ref[0] * acc
        o_ref[...] = _rms(y) * fw_ref[...]

    for_buffer(i, finish)


def _combine(slot, wts, h2, x1, g2, fw, wsgu, wsd, ys, T, tmc):
    M, D = x1.shape
    const = lambda shape: pl.BlockSpec(shape, lambda i: (0,) * len(shape))
    tile = lambda w: pl.BlockSpec((tmc, w), lambda i: (i, 0))
    row_tiled = pl.BlockSpec((tmc, ROW_TILE, LANES), lambda i: (i, 0, 0))
    n_steps = M // tmc
    return pl.pallas_call(
        _combine_kernel,
        grid=(n_steps,),
        in_specs=[pl.BlockSpec((TOP_K, tmc), lambda i: (0, i), memory_space=pltpu.SMEM),
                  pl.BlockSpec((TOP_K, tmc), lambda i: (0, jnp.minimum(i + 1, n_steps - 1)), memory_space=pltpu.SMEM),
                  tile(GATE_LANES), row_tiled, tile(D),
                  pl.BlockSpec((1, 1, D), lambda i: (i // (T // tmc), 0, 0)),
                  const((1, D)), const(wsgu.shape), const(wsd.shape),
                  pl.BlockSpec(memory_space=pl.ANY)],
        out_specs=tile(D),
        out_shape=jax.ShapeDtypeStruct((M, D), F32),
        scratch_shapes=[pltpu.VMEM((2, TOP_K, tmc, ROW_TILE, LANES), ys.dtype), pltpu.SemaphoreType.DMA((2,))],
        compiler_params=pltpu.CompilerParams(dimension_semantics=("arbitrary",),
                                             vmem_limit_bytes=VMEM_LIMIT_BYTES),
        name="combine",
    )(slot, slot, wts, h2, x1, g2, fw, wsgu, wsd, ys)


def _pick_tile(n, want):
    t = min(n, want)
    assert n % t == 0 and t % CHUNK == 0, (n, want)
    return t


def kernel(x, c, w_ada, b_ada, norm1_w, w_in, conv_w, gdn_a_log, gdn_dt_bias, gdn_norm_w, hg_lb, hg_norm_w,
           w_out, norm2_w, w_router, router_bias, w_gate, w_up, w_down, ws_gate, ws_up, ws_down, final_norm_w):
    B, T, D = x.shape
    M = B * T
    depth = w_ada.shape[0]
    assert depth == 1 and T % CHUNK == 0 and B <= 8
    layer = 0
    tt = _pick_tile(T, 512)

    c_pad = jnp.pad(c, ((0, 8 - B), (0, 0)))
    mod = _ada(c_pad, w_ada[layer], b_ada[layer].reshape(1, -1))[:B]
    sh1, sc1, g1, sh2, sc2, g2 = (m.reshape(B, 1, D) for m in jnp.split(mod, 6, axis=-1))

    w = w_in[layer]
    qkv_w = 3 * GW
    sizes = (GW, HEADS, HEADS, GW, GW, GW, GW)
    offs = [qkv_w]
    for s in sizes:
        offs.append(offs[-1] + s)
    seg = lambda i: w[:, offs[i]:offs[i + 1]]
    small = jnp.pad(jnp.concatenate([seg(1), seg(2)], axis=1), ((0, 0), (0, GATE_LANES - 2 * HEADS)))
    w_all = jnp.concatenate([w[:, :qkv_w], seg(0), seg(3), seg(4), seg(5), seg(6), small], axis=1).astype(BF16)
    lane_pad = lambda v: jnp.pad(v.astype(F32).reshape(1, HEADS), ((0, 0), (HEADS, GATE_LANES - 2 * HEADS)))
    idx = jnp.arange(CUMSUM_ROWS)
    tri = ((idx[:, None] >= idx[None, :]) & (idx[:, None] // CHUNK == idx[None, :] // CHUNK)).astype(BF16)

    qa, ka, va, ga, sm, bcum, kb, ib, qb, gb = _inproj(
        layer, x, norm1_w[layer].reshape(1, D), sc1, sh1, w_all, conv_w[layer].astype(F32),
        lane_pad(gdn_a_log[layer]), lane_pad(gdn_dt_bias[layer]), hg_lb.astype(F32), tri, tt)

    gct = sm[:, :, HEADS:2 * HEADS].transpose(0, 2, 1).reshape(B, HEADS, T // CHUNK, CHUNK)
    oa = _gdn(qa, ka, va, ga, sm, gct, gdn_norm_w[layer].reshape(1, DH), tt)
    ob = _hgrn(qb, kb, ib, bcum, gb, hg_norm_w[layer].reshape(1, GW), tt)

    wo = w_out[layer].astype(BF16)
    x1, h2, eidx, rank, wts, cnt = _outproj(oa, ob, x, wo[:GW], wo[GW:], g1, norm2_w[layer].reshape(1, D), sc2, sh2,
                                            w_router[layer].T, router_bias[layer].reshape(N_EXPERTS, 1), tt)

    counts = cnt[:, 0].astype(jnp.int32)
    padded = (counts + MOE_ROWS - 1) // MOE_ROWS * MOE_ROWS
    ends = jnp.cumsum(padded)
    offsets = ends - padded
    n_tiles = (M * TOP_K) // MOE_ROWS + N_EXPERTS
    n_used = (ends[-1] // MOE_ROWS).astype(jnp.int32)
    tile_ids = jnp.minimum(jnp.arange(n_tiles + FFN_PIPELINE_STEPS, dtype=jnp.int32), n_used - 1)
    tile_expert = jnp.sum(((ends // MOE_ROWS)[None, :] <= tile_ids[:, None]).astype(jnp.int32), axis=1)
    tile_expert = jnp.minimum(tile_expert, N_EXPERTS - 1)
    slot = _slots(offsets.astype(jnp.int32), eidx, rank)
    first_tile = jnp.take(offsets, tile_expert) // MOE_ROWS
    n_valid = jnp.clip(jnp.take(counts, tile_expert) - (tile_ids - first_tile) * MOE_ROWS, 0, MOE_ROWS)

    assert D == ROW_TILE * LANES
    xs = _dispatch(slot, h2, n_tiles * MOE_ROWS, _pick_tile(M, 512))
    ys = _ffn(tile_expert.astype(jnp.int32), n_used.reshape(1), n_valid.astype(jnp.int32), xs,
              w_gate[layer], w_up[layer], w_down[layer])
    wsgu = jnp.concatenate([ws_gate[layer], ws_up[layer]], axis=-1).astype(BF16)
    out = _combine(slot, wts.reshape(M, GATE_LANES), h2, x1.reshape(M, D), g2, final_norm_w.reshape(1, D),
                   wsgu, ws_down[layer].astype(BF16), ys, T, _pick_tile(T, 128))
    return out.reshape(B, T, D)
```

```python
import functools

import jax
import jax.numpy as jnp
from jax import lax
from jax.experimental import pallas as pl
from jax.experimental.pallas import tpu as pltpu

F32 = jnp.float32
BF16 = jnp.bfloat16

EPS = 1e-6
CHUNK = 64
SUB = 8
HEADS = 4
DH = 128
GW = HEADS * DH
CONV_K = 4
N_EXPERTS = 64
N_GROUPS = 8
GROUP_SIZE = N_EXPERTS // N_GROUPS
TOPK_GROUPS = 4
TOP_K = 8
D_EXPERT = 256
ROUTE_SCALE = 2.5
GATE_LANES = 128
GDN_CHUNKS_PER_ITER = 4
INV_BLOCK = 16
CUMSUM_ROWS = 128
MOE_ROWS = 1024
ROW_TILE, LANES = 8, 128

VMEM_LIMIT_BYTES = 56 * 1024 * 1024

ACT = BF16
ROW_DTYPE = F32


def _silu(x):
    return x * jax.nn.sigmoid(x)


def _dot(a, b):
    return jnp.dot(a.astype(BF16), b.astype(BF16), preferred_element_type=F32)


def _dot_nt(a, b):
    return lax.dot_general(a.astype(BF16), b.astype(BF16), (((1,), (1,)), ((), ())),
                           preferred_element_type=F32)


def _dot_tn(a, b):
    return lax.dot_general(a.astype(BF16), b.astype(BF16), (((0,), (0,)), ((), ())),
                           preferred_element_type=F32)


def _split2(x):
    hi = x.astype(BF16)
    lo = (x - hi.astype(F32)).astype(BF16)
    return hi, lo


def _dot3(a, b, dot=_dot):
    ah, al = _split2(a)
    bh, bl = _split2(b)
    return dot(ah, bh) + dot(ah, bl) + dot(al, bh)


def _cumsum_rows(tri, x):
    hi = x.astype(BF16)
    r = x - hi.astype(F32)
    mid = r.astype(BF16)
    lo = (r - mid.astype(F32)).astype(BF16)
    g = tri.shape[0]
    groups = []
    for r0 in range(0, x.shape[0], g):
        rows = slice(r0, r0 + g)
        groups.append(jnp.dot(tri, hi[rows], preferred_element_type=F32)
                      + jnp.dot(tri, mid[rows], preferred_element_type=F32)
                      + jnp.dot(tri, lo[rows], preferred_element_type=F32))
    return jnp.concatenate(groups, axis=0)


def _lane_pick(tile, lane, idx):
    return jnp.sum(jnp.where(lane == idx, tile, 0.0), axis=1, keepdims=True)


def _rms(x):
    return x * lax.rsqrt(jnp.mean(x * x, axis=-1, keepdims=True) + EPS)


def _to_rows(x3):
    r = x3.shape[0]
    xt = jnp.swapaxes(x3.reshape(r // ROW_TILE, ROW_TILE, ROW_TILE, LANES), 1, 2)
    return jnp.concatenate([xt[:, s].reshape(r, LANES) for s in range(ROW_TILE)], axis=1)


def _to_row_tiles(x):
    r = x.shape[0]
    xt = jnp.stack([x[:, s * LANES:(s + 1) * LANES].reshape(r // ROW_TILE, ROW_TILE, LANES) for s in range(ROW_TILE)],
                   axis=1)
    return jnp.swapaxes(xt, 1, 2).reshape(r, ROW_TILE, LANES)


def _ada_kernel(c_ref, w_ref, b_ref, o_ref):
    ca = _silu(c_ref[...])
    o_ref[...] = _dot3(ca, w_ref[...]) + b_ref[...]


def _ada(c_pad, w, b):
    rows, d = c_pad.shape
    n = w.shape[1]
    tn = 1024
    return pl.pallas_call(
        _ada_kernel,
        grid=(n // tn,),
        in_specs=[pl.BlockSpec((rows, d), lambda j: (0, 0)),
                  pl.BlockSpec((d, tn), lambda j: (0, j)),
                  pl.BlockSpec((1, tn), lambda j: (0, j))],
        out_specs=pl.BlockSpec((rows, tn), lambda j: (0, j)),
        out_shape=jax.ShapeDtypeStruct((rows, n), F32),
        compiler_params=pltpu.CompilerParams(dimension_semantics=("arbitrary",),
                                             vmem_limit_bytes=VMEM_LIMIT_BYTES),
        name="ada",
    )(c_pad, w, b)


def _inproj_kernel(layer, x_ref, n1_ref, sc_ref, sh_ref, w_ref, cw_ref, alog_ref, dt_ref, lb_ref, tri_ref,
                   qa_ref, ka_ref, va_ref, ga_ref, sm_ref, b_ref, kb_ref, ib_ref, qb_ref, gb_ref,
                   pbuf):
    tt = x_ref.shape[1]
    t = pl.program_id(1)

    h = _rms(x_ref[0]) * n1_ref[...]
    h = h * (1.0 + sc_ref[0]) + sh_ref[0]
    hb = h.astype(BF16)

    def proj(g, width=GW):
        return jnp.dot(hb, w_ref[:, g * GW:g * GW + width], preferred_element_type=F32)

    @pl.when(t == 0)
    def _():
        pbuf[:, 0:8, :] = jnp.zeros((3, 8, GW), F32)

    for g, out_ref in enumerate((qa_ref, ka_ref, va_ref)):
        cols = slice(g * GW, (g + 1) * GW)
        p = proj(g)
        pbuf[g, 8:8 + tt, :] = p
        y = p * cw_ref[CONV_K - 1:CONV_K, cols]
        for j in range(1, CONV_K):
            y = y + pbuf[g, 8 - j:8 - j + tt, :] * cw_ref[CONV_K - 1 - j:CONV_K - j, cols]
        pbuf[g, 0:8, :] = pbuf[g, tt:tt + 8, :]
        y = _silu(y)
        if g == 2:
            out_ref[0] = y.astype(out_ref.dtype)
        else:
            scale = DH ** -0.5 if g == 0 else 1.0
            for hh in range(HEADS):
                hs = slice(hh * DH, (hh + 1) * DH)
                yh = y[:, hs]
                inv = lax.rsqrt(jnp.sum(yh * yh, axis=-1, keepdims=True) + EPS)
                out_ref[0, :, hs] = (yh * inv * scale).astype(out_ref.dtype)

    ga_ref[0] = _silu(proj(3)).astype(ga_ref.dtype)

    ps = proj(8, GATE_LANES)
    lane = lax.broadcasted_iota(jnp.int32, ps.shape, 1)
    beta = jax.nn.sigmoid(ps)
    z = ps + dt_ref[...]
    softplus = jnp.maximum(z, 0.0) + jnp.log1p(jnp.exp(-jnp.abs(z)))
    g_log = -jnp.exp(alog_ref[...]) * softplus
    tri = tri_ref[...]
    gc = _cumsum_rows(tri, jnp.where((lane >= HEADS) & (lane < 2 * HEADS), g_log, 0.0))
    sm_ref[0] = jnp.where(lane < HEADS, beta, gc)

    hl = lb_ref[...]
    e = jnp.exp(hl - jnp.max(hl, axis=0, keepdims=True))
    lb = jnp.sum(e[0:layer + 1], axis=0, keepdims=True) / jnp.sum(e, axis=0, keepdims=True)
    fr = proj(4)
    logf = jnp.log(lb + (1.0 - lb) * jax.nn.sigmoid(fr))
    b_ref[0] = _cumsum_rows(tri, logf)
    kb_ref[0] = ((1.0 - lb) * jax.nn.sigmoid(-fr)).astype(kb_ref.dtype)
    ib_ref[0] = proj(5).astype(ib_ref.dtype)
    qb_ref[0] = _silu(proj(6)).astype(qb_ref.dtype)
    gb_ref[0] = _silu(proj(7)).astype(gb_ref.dtype)


def _inproj(layer, x, n1, sc1, sh1, w_all, conv_w, alog_pad, dt_pad, hg_lb, tri, tt):
    B, T, D = x.shape
    const = lambda shape: pl.BlockSpec(shape, lambda b, t: (0,) * len(shape))
    act = lambda dt: jax.ShapeDtypeStruct((B, T, GW), dt)
    tile = lambda w: pl.BlockSpec((1, tt, w), lambda b, t: (b, t, 0))
    per_batch = pl.BlockSpec((1, 1, D), lambda b, t: (b, 0, 0))
    return pl.pallas_call(
        functools.partial(_inproj_kernel, layer),
        grid=(B, T // tt),
        in_specs=[tile(D), const((1, D)), per_batch, per_batch,
                  const(w_all.shape), const(conv_w.shape), const((1, GATE_LANES)), const((1, GATE_LANES)),
                  const(hg_lb.shape), const(tri.shape)],
        out_specs=[tile(GW), tile(GW), tile(GW), tile(GW), tile(GATE_LANES), tile(GW),
                   tile(GW), tile(GW), tile(GW), tile(GW)],
        out_shape=[act(ACT), act(ACT), act(ACT), act(ACT),
                   jax.ShapeDtypeStruct((B, T, GATE_LANES), F32), act(F32),
                   act(ACT), act(ACT), act(ACT), act(ACT)],
        scratch_shapes=[pltpu.VMEM((3, tt + 8, GW), F32)],
        compiler_params=pltpu.CompilerParams(dimension_semantics=("arbitrary", "arbitrary"),
                                             vmem_limit_bytes=VMEM_LIMIT_BYTES),
        name="inproj",
    )(x, n1, sc1, sh1, w_all, conv_w, alog_pad, dt_pad, hg_lb, tri)


def _gdn_prep_kernel(q_ref, k_ref, v_ref, sm_ref, gct_ref, o_ref, qt_ref, m_ref, n_ref):
    tt = q_ref.shape[1]
    nc = tt // CHUNK
    row = lax.broadcasted_iota(jnp.int32, (CHUNK, CHUNK), 0)
    col = lax.broadcasted_iota(jnp.int32, (CHUNK, CHUNK), 1)
    causal = row >= col
    diag_blk = (row > col) & (row // INV_BLOCK == col // INV_BLOCK)
    off_blk = row // INV_BLOCK > col // INV_BLOCK
    eye = jnp.where(row == col, 1.0, 0.0)
    lane = lax.broadcasted_iota(jnp.int32, (CHUNK, GATE_LANES), 1)
    assert INV_BLOCK == 16 and CHUNK == 4 * INV_BLOCK

    def body(i, carry):
        chains = [(GDN_CHUNKS_PER_ITER * i + j, hh) for j in range(GDN_CHUNKS_PER_ITER) for hh in range(HEADS)]
        rows = [pl.ds(pl.multiple_of(c * CHUNK, CHUNK), CHUNK) for c, _ in chains]
        hs = [slice(hh * DH, (hh + 1) * DH) for _, hh in chains]
        n = range(len(chains))
        sm = [sm_ref[0, rows[j], :] for j in n]
        q = [q_ref[0, rows[j], hs[j]].astype(F32) for j in n]
        k = [k_ref[0, rows[j], hs[j]].astype(F32) for j in n]
        v = [v_ref[0, rows[j], hs[j]].astype(F32) for j in n]
        beta = [_lane_pick(sm[j], lane, chains[j][1]) for j in n]
        gcol = [_lane_pick(sm[j], lane, HEADS + chains[j][1]) for j in n]
        grow = [gct_ref[0, hh, pl.ds(c, 1), :] for c, hh in chains]
        decay = [jnp.exp(jnp.where(causal, gcol[j] - grow[j], -jnp.inf)) for j in n]
        kb = [k[j] * beta[j] for j in n]
        L = [_dot_nt(kb[j], k[j]) * decay[j] for j in n]
        dg = [jnp.where(diag_blk, L[j], 0.0) for j in n]
        off = [jnp.where(off_blk, L[j], 0.0) for j in n]
        dinv = [eye - dg[j] for j in n]
        pw = [_dot3(dg[j], dg[j]) for j in n]
        for _ in range(2):
            dinv = [dinv[j] + _dot3(dinv[j], pw[j]) for j in n]
            pw = [_dot3(pw[j], pw[j]) for j in n]
        dinv = [dinv[j] + _dot3(dinv[j], pw[j]) for j in n]
        f1 = [_dot(dinv[j], off[j]) for j in n]
        f2 = [_dot(f1[j], f1[j]) for j in n]
        f3 = [_dot(f1[j], f2[j]) for j in n]
        tinv = [_dot(eye - f1[j] + f2[j] - f3[j], dinv[j]) for j in n]
        eg = [jnp.exp(gcol[j]) for j in n]
        sol = [_dot(tinv[j], jnp.concatenate([v[j] * beta[j], kb[j] * eg[j]], axis=1)) for j in n]
        attn = [_dot_nt(q[j], k[j]) * decay[j] for j in n]
        k_tail = [k[j] * jnp.exp(gcol[j][CHUNK - 1:CHUNK, :] - gcol[j]) for j in n]
        au = [_dot(attn[j], sol[j]) for j in n]
        ku = [_dot_tn(k_tail[j], sol[j]) for j in n]
        for j, (c, hh) in enumerate(chains):
            o_ref[0, rows[j], hs[j]] = au[j][:, :DH]
            qt_ref[0, rows[j], hs[j]] = (q[j] * eg[j] - au[j][:, DH:]).astype(qt_ref.dtype)
            n_ref[0, hh, c] = ku[j][:, :DH].astype(n_ref.dtype)
            m_ref[0, hh, c] = (-ku[j][:, DH:]).astype(m_ref.dtype)
        return carry

    lax.fori_loop(0, nc // GDN_CHUNKS_PER_ITER, body, 0)


def _gdn_scan_kernel(o_ref, qt_ref, m_ref, n_ref, gct_ref, sg_ref, nw_ref, out_ref, s_ref):
    nb, tt = o_ref.shape[0], o_ref.shape[1]
    nc = tt // CHUNK

    @pl.when(pl.program_id(0) == 0)
    def _():
        s_ref[...] = jnp.zeros(s_ref.shape, F32)

    nw = nw_ref[...]

    def body(c, carry):
        rows = pl.ds(pl.multiple_of(c * CHUNK, CHUNK), CHUNK)
        for b in range(nb):
            for hh in range(HEADS):
                hs = slice(hh * DH, (hh + 1) * DH)
                S = s_ref[b, hh]
                Sb = S.astype(BF16)
                glast = gct_ref[b, hh, pl.ds(c, 1), :][:, CHUNK - 1:CHUNK]
                o = o_ref[b, rows, hs] + jnp.dot(qt_ref[b, rows, hs], Sb, preferred_element_type=F32)
                s_ref[b, hh] = (S * jnp.exp(glast) + jnp.dot(m_ref[b, hh, c], Sb, preferred_element_type=F32)
                                + n_ref[b, hh, c].astype(F32))
                o = _rms(o) * nw * sg_ref[b, rows, hs].astype(F32)
                out_ref[b, rows, hs] = o.astype(out_ref.dtype)
        return carry

    lax.fori_loop(0, nc, body, 0)


def _gdn(q, k, v, sg, sm, gct, nw, tt):
    B, T, _ = q.shape
    nc = tt // CHUNK
    assert nc % GDN_CHUNKS_PER_ITER == 0
    n_chunks = T // CHUNK
    tile = lambda w: pl.BlockSpec((1, tt, w), lambda b, t: (b, t, 0))
    mat = jax.ShapeDtypeStruct((B, HEADS, n_chunks, DH, DH), ACT)
    o_part, qt, m, n = pl.pallas_call(
        _gdn_prep_kernel,
        grid=(B, T // tt),
        in_specs=[tile(GW), tile(GW), tile(GW), tile(GATE_LANES),
                  pl.BlockSpec((1, HEADS, nc, CHUNK), lambda b, t: (b, 0, t, 0))],
        out_specs=[tile(GW), tile(GW),
                   pl.BlockSpec((1, HEADS, nc, DH, DH), lambda b, t: (b, 0, t, 0, 0)),
                   pl.BlockSpec((1, HEADS, nc, DH, DH), lambda b, t: (b, 0, t, 0, 0))],
        out_shape=[jax.ShapeDtypeStruct((B, T, GW), F32), jax.ShapeDtypeStruct((B, T, GW), ACT), mat, mat],
        compiler_params=pltpu.CompilerParams(dimension_semantics=("arbitrary", "arbitrary"),
                                             vmem_limit_bytes=VMEM_LIMIT_BYTES),
        name="gdn_prep",
    )(q, k, v, sm, gct)

    full = lambda w: pl.BlockSpec((B, tt, w), lambda t: (0, t, 0))
    mats = pl.BlockSpec((B, HEADS, nc, DH, DH), lambda t: (0, 0, t, 0, 0))
    return pl.pallas_call(
        _gdn_scan_kernel,
        grid=(T // tt,),
        in_specs=[full(GW), full(GW), mats, mats,
                  pl.BlockSpec((B, HEADS, nc, CHUNK), lambda t: (0, 0, t, 0)),
                  full(GW), pl.BlockSpec((1, DH), lambda t: (0, 0))],
        out_specs=full(GW),
        out_shape=jax.ShapeDtypeStruct((B, T, GW), ACT),
        scratch_shapes=[pltpu.VMEM((B, HEADS, DH, DH), F32)],
        compiler_params=pltpu.CompilerParams(dimension_semantics=("arbitrary",),
                                             vmem_limit_bytes=VMEM_LIMIT_BYTES),
        name="gdn_scan",
    )(o_part, qt, m, n, gct, sg, nw)


def _hgrn_kernel(q_ref, k_ref, v_ref, b_ref, sg_ref, nw_ref, o_ref, st_ref):
    tt = q_ref.shape[1]
    nc = tt // CHUNK

    @pl.when(pl.program_id(1) == 0)
    def _():
        st_ref[...] = jnp.zeros(st_ref.shape, F32)

    row = lax.broadcasted_iota(jnp.int32, (CHUNK, CHUNK), 0)
    col = lax.broadcasted_iota(jnp.int32, (CHUNK, CHUNK), 1)
    diag_block = ((col // SUB) == (row // SUB)) & (col <= row)
    heads = range(HEADS)
    hs = [slice(hh * DH, (hh + 1) * DH) for hh in heads]

    def body(c, carry):
        rows = pl.ds(pl.multiple_of(c * CHUNK, CHUNK), CHUNK)
        q = [q_ref[0, rows, hs[h]].astype(F32) for h in heads]
        k = [k_ref[0, rows, hs[h]].astype(F32) for h in heads]
        v = [v_ref[0, rows, hs[h]].astype(F32) for h in heads]
        b = [b_ref[0, rows, hs[h]] for h in heads]
        blast = [b[h][CHUNK - 1:CHUNK, :] for h in heads]
        st = [st_ref[h] for h in heads]
        o = [_dot_nt(q[h] * jnp.exp(b[h]), st[h]) for h in heads]
        k_tail = [k[h] * jnp.exp(blast[h] - b[h]) for h in heads]
        for h in heads:
            st_ref[h] = st[h] * jnp.exp(blast[h]) + _dot_tn(v[h], k_tail[h])

        blocks = [[jnp.zeros((SUB, CHUNK), F32)] for _ in heads]
        for i in range(1, CHUNK // SUB):
            lo, hi = i * SUB, (i + 1) * SUB
            for h in heads:
                r = b[h][lo:lo + 1, :]
                qi = q[h][lo:hi] * jnp.exp(b[h][lo:hi] - r)
                kj = k[h][:lo] * jnp.exp(jnp.minimum(r - b[h][:lo], 0.0))
                kj = jnp.concatenate([kj, jnp.zeros((CHUNK - lo, DH), F32)], axis=0)
                blocks[h].append(_dot_nt(qi, kj))
        a = []
        for h in heads:
            f = jnp.exp(jnp.minimum(b[h] - pltpu.roll(b[h], 1, 0), 0.0))
            e = None
            a_diag = jnp.zeros((CHUNK, CHUNK), F32)
            for delta in range(SUB):
                if delta == 0:
                    term = q[h] * k[h]
                else:
                    fsh = f if delta == 1 else pltpu.roll(f, delta - 1, 0)
                    e = fsh if e is None else e * fsh
                    term = q[h] * pltpu.roll(k[h], delta, 0) * e
                colv = jnp.sum(term, axis=1, keepdims=True)
                a_diag = jnp.where(row - col == delta, colv, a_diag)
            a.append(jnp.where(diag_block, a_diag, jnp.concatenate(blocks[h], axis=0)))

        o = [o[h] + _dot(a[h], v[h]) for h in heads]
        o = jnp.concatenate(o, axis=1)
        o = _rms(o) * nw_ref[...] * sg_ref[0, rows, :].astype(F32)
        o_ref[0, rows, :] = o.astype(o_ref.dtype)
        return carry

    lax.fori_loop(0, nc, body, 0)


def _hgrn(q, k, v, b, sg, nw, tt):
    B, T, _ = q.shape
    tile = pl.BlockSpec((1, tt, GW), lambda bi, t: (bi, t, 0))
    return pl.pallas_call(
        _hgrn_kernel,
        grid=(B, T // tt),
        in_specs=[tile, tile, tile, tile, tile, pl.BlockSpec((1, GW), lambda bi, t: (0, 0))],
        out_specs=tile,
        out_shape=jax.ShapeDtypeStruct((B, T, GW), ACT),
        scratch_shapes=[pltpu.VMEM((HEADS, DH, DH), F32)],
        compiler_params=pltpu.CompilerParams(dimension_semantics=("arbitrary", "arbitrary"),
                                             vmem_limit_bytes=VMEM_LIMIT_BYTES),
        name="hgrn",
    )(q, k, v, b, sg, nw)


def _outproj_kernel(oa_ref, ob_ref, x_ref, wa_ref, wb_ref, g1_ref, n2_ref, sc_ref, sh_ref, wr_ref, rb_ref,
                    x1_ref, h2_ref, eidx_ref, rank_ref, wts_ref, cnt_ref):
    tm = x_ref.shape[1]

    @pl.when((pl.program_id(0) == 0) & (pl.program_id(1) == 0))
    def _():
        cnt_ref[...] = jnp.zeros(cnt_ref.shape, F32)

    mix = (jnp.dot(oa_ref[0], wa_ref[...], preferred_element_type=F32)
           + jnp.dot(ob_ref[0], wb_ref[...], preferred_element_type=F32))
    x1 = x_ref[0] + g1_ref[0] * mix
    x1_ref[0] = x1
    h2 = _rms(x1) * n2_ref[...]
    h2 = h2 * (1.0 + sc_ref[0]) + sh_ref[0]
    h2_ref[...] = _to_row_tiles(h2).astype(h2_ref.dtype)

    scores = jax.nn.sigmoid(_dot3(wr_ref[...], h2, dot=_dot_nt))
    sel = scores + rb_ref[...]
    sub = lax.broadcasted_iota(jnp.int32, (GROUP_SIZE, tm), 0)
    neg = -jnp.inf
    groups = range(N_GROUPS)

    def take_max(blk):
        m = jnp.max(blk, axis=0, keepdims=True)
        first = jnp.min(jnp.where(blk == m, sub, GROUP_SIZE), axis=0, keepdims=True)
        hit = sub == first
        return m, hit, jnp.where(hit, neg, blk)

    blk_of = lambda a, g: a[g * GROUP_SIZE:(g + 1) * GROUP_SIZE]
    sel_blk = [blk_of(sel, g) for g in groups]
    group_score = jnp.zeros((N_GROUPS, tm), F32)
    for g in groups:
        m1, _, rest = take_max(sel_blk[g])
        m2 = jnp.max(rest, axis=0, keepdims=True)
        group_score = jnp.where(sub == g, m1 + m2, group_score)
    group_on = jnp.zeros((N_GROUPS, tm), F32)
    for _ in range(TOPK_GROUPS):
        _, hit, group_score = take_max(group_score)
        group_on = jnp.where(hit, 1.0, group_on)

    cand = [jnp.where(group_on[g:g + 1] > 0.0, sel_blk[g], neg) for g in groups]
    picked = [jnp.zeros((GROUP_SIZE, tm), F32) for _ in groups]
    chosen = []
    for _ in range(TOP_K):
        m = jnp.max(functools.reduce(jnp.maximum, cand), axis=0, keepdims=True)
        first = functools.reduce(jnp.minimum, [jnp.where(cand[g] == m, sub + g * GROUP_SIZE, N_EXPERTS)
                                               for g in groups])
        first = jnp.min(first, axis=0, keepdims=True)
        chosen.append(first)
        for g in groups:
            hit = (sub + g * GROUP_SIZE) == first
            picked[g] = jnp.where(hit, 1.0, picked[g])
            cand[g] = jnp.where(hit, neg, cand[g])

    picked_all = jnp.concatenate(picked, axis=0)
    r_i = lax.broadcasted_iota(jnp.int32, (tm, tm), 0)
    c_i = lax.broadcasted_iota(jnp.int32, (tm, tm), 1)
    earlier = jnp.where(r_i < c_i, 1.0, 0.0).astype(BF16)
    before = jnp.dot(picked_all.astype(BF16), earlier, preferred_element_type=F32) + cnt_ref[:, 0:1]
    cnt_ref[...] = cnt_ref[...] + jnp.sum(picked_all, axis=1, keepdims=True)

    def pick_value(table, first):
        parts = [jnp.where((sub + g * GROUP_SIZE) == first, blk_of(table, g), 0.0) for g in groups]
        return jnp.sum(functools.reduce(jnp.add, parts), axis=0, keepdims=True)

    w_k = [pick_value(scores, f) for f in chosen]
    denom = functools.reduce(jnp.add, w_k)
    eidx = jnp.zeros((TOP_K, tm), jnp.int32)
    rank = jnp.zeros((TOP_K, tm), jnp.int32)
    wts = jnp.zeros((TOP_K, tm), F32)
    for k in range(TOP_K):
        eidx = jnp.where(sub == k, chosen[k], eidx)
        rank = jnp.where(sub == k, pick_value(before, chosen[k]).astype(jnp.int32), rank)
        wts = jnp.where(sub == k, w_k[k] / denom * ROUTE_SCALE, wts)
    eidx_ref[...] = eidx
    rank_ref[...] = rank
    pad = jnp.zeros((GATE_LANES - TOP_K, tm), F32)
    wts_ref[0] = jnp.concatenate([wts, pad], axis=0).T


def _outproj(oa, ob, x, wa, wb, g1, n2, sc2, sh2, wr_t, rb, tm):
    B, T, D = x.shape
    nt = T // tm
    const = lambda shape: pl.BlockSpec(shape, lambda b, t: (0,) * len(shape))
    tile = lambda w: pl.BlockSpec((1, tm, w), lambda b, t: (b, t, 0))
    per_batch = pl.BlockSpec((1, 1, D), lambda b, t: (b, 0, 0))
    picks = pl.BlockSpec((TOP_K, tm), lambda b, t: (0, b * nt + t))
    return pl.pallas_call(
        _outproj_kernel,
        grid=(B, nt),
        in_specs=[tile(GW), tile(GW), tile(D), const(wa.shape), const(wb.shape), per_batch,
                  const((1, D)), per_batch, per_batch, const(wr_t.shape), const(rb.shape)],
        out_specs=[tile(D), pl.BlockSpec((tm, ROW_TILE, LANES), lambda b, t: (b * nt + t, 0, 0)),
                   picks, picks, tile(GATE_LANES), const((N_EXPERTS, GATE_LANES))],
        out_shape=[jax.ShapeDtypeStruct((B, T, D), F32), jax.ShapeDtypeStruct((B * T, ROW_TILE, LANES), ROW_DTYPE),
                   jax.ShapeDtypeStruct((TOP_K, B * T), jnp.int32), jax.ShapeDtypeStruct((TOP_K, B * T), jnp.int32),
                   jax.ShapeDtypeStruct((B, T, GATE_LANES), F32),
                   jax.ShapeDtypeStruct((N_EXPERTS, GATE_LANES), F32)],
        compiler_params=pltpu.CompilerParams(dimension_semantics=("arbitrary", "arbitrary"),
                                             vmem_limit_bytes=VMEM_LIMIT_BYTES),
        name="outproj",
    )(oa, ob, x, wa, wb, g1, n2, sc2, sh2, wr_t, rb)


def _slots_kernel(off_ref, eidx_ref, rank_ref, slot_ref):
    eidx = eidx_ref[...]

    def add_expert(e, acc):
        return acc + jnp.where(eidx == e, off_ref[e], 0)

    slot_ref[...] = lax.fori_loop(0, N_EXPERTS, add_expert, rank_ref[...])


def _slots(offsets, eidx, rank):
    k, m = eidx.shape
    tile = pl.BlockSpec((k, m), lambda i: (0, 0))
    return pl.pallas_call(
        _slots_kernel,
        grid=(1,),
        in_specs=[pl.BlockSpec(memory_space=pltpu.SMEM), tile, tile],
        out_specs=tile,
        out_shape=jax.ShapeDtypeStruct((k, m), jnp.int32),
        compiler_params=pltpu.CompilerParams(dimension_semantics=("arbitrary",),
                                             vmem_limit_bytes=VMEM_LIMIT_BYTES),
        name="slots",
    )(offsets, eidx, rank)


def _dispatch_kernel(slot_ref, h_ref, xs_ref, sem):
    tmd = h_ref.shape[0]

    def start_rows(j, c):
        for k in range(TOP_K):
            pltpu.make_async_copy(h_ref.at[pl.ds(j, 1)], xs_ref.at[pl.ds(slot_ref[k, j], 1)], sem).start(
                priority=k % 2)
        return c

    lax.fori_loop(0, tmd, start_rows, 0)
    for _ in range(TOP_K):
        pltpu.make_async_copy(h_ref, xs_ref.at[pl.ds(0, tmd)], sem).wait()


def _dispatch(slot, h2, n_rows, tmd):
    M = h2.shape[0]
    return pl.pallas_call(
        _dispatch_kernel,
        grid=(M // tmd,),
        in_specs=[pl.BlockSpec((TOP_K, tmd), lambda i: (0, i), memory_space=pltpu.SMEM),
                  pl.BlockSpec((tmd, ROW_TILE, LANES), lambda i: (i, 0, 0))],
        out_specs=pl.BlockSpec(memory_space=pl.ANY),
        out_shape=jax.ShapeDtypeStruct((n_rows, ROW_TILE, LANES), h2.dtype),
        scratch_shapes=[pltpu.SemaphoreType.DMA],
        compiler_params=pltpu.CompilerParams(dimension_semantics=("arbitrary",),
                                             vmem_limit_bytes=VMEM_LIMIT_BYTES),
        name="dispatch",
    )(slot, h2)


FFN_PIPELINE_STEPS = 2


def _ffn_kernel(te_ref, nu_ref, nv_ref, x_ref, wg_ref, wu_ref, wd_ref, y_ref, wgu_s, wd_s, xstd, ystd):
    i = pl.program_id(0)
    row = lax.broadcasted_iota(jnp.int32, (MOE_ROWS, 1), 0)
    tile_mm = jnp.maximum(i - 1, 0)
    tile_out = jnp.maximum(i - 2, 0)

    @pl.when(i == 0)
    def _():
        xstd[...] = jnp.zeros(xstd.shape, BF16)
        ystd[...] = jnp.zeros(ystd.shape, F32)

    @pl.when(i < nu_ref[0] + FFN_PIPELINE_STEPS)
    def _():
        @pl.when((i == 0) | (te_ref[tile_mm] != te_ref[tile_out]))
        def _():
            wgu_s[:, :D_EXPERT] = wg_ref[0].astype(BF16)
            wgu_s[:, D_EXPERT:] = wu_ref[0].astype(BF16)
            wd_s[...] = wd_ref[0].astype(BF16)

        cur = i % 2
        xstd[cur] = jnp.where(row < nv_ref[i], _to_rows(x_ref[...].astype(F32)), 0.0).astype(BF16)
        y_ref[...] = _to_row_tiles(ystd[cur]).astype(y_ref.dtype)
        gu = jnp.dot(xstd[1 - cur], wgu_s[...], preferred_element_type=F32)
        act = _silu(gu[:, :D_EXPERT]) * gu[:, D_EXPERT:]
        ystd[1 - cur] = jnp.dot(act.astype(BF16), wd_s[...], preferred_element_type=F32)


def _ffn(tile_expert, n_used, n_valid, xs, wg, wu, wd):
    D = wg.shape[1]
    n_tiles = xs.shape[0] // MOE_ROWS
    block = (MOE_ROWS, ROW_TILE, LANES)
    expert = lambda i, te, nu, nv: (te[jnp.maximum(i - 1, 0)], 0, 0)
    return pl.pallas_call(
        _ffn_kernel,
        grid_spec=pltpu.PrefetchScalarGridSpec(
            num_scalar_prefetch=3,
            grid=(n_tiles + FFN_PIPELINE_STEPS,),
            in_specs=[pl.BlockSpec(block, lambda i, te, nu, nv: (jnp.minimum(i, nu[0] - 1), 0, 0)),
                      pl.BlockSpec((1, D, D_EXPERT), expert),
                      pl.BlockSpec((1, D, D_EXPERT), expert),
                      pl.BlockSpec((1, D_EXPERT, D), expert)],
            out_specs=pl.BlockSpec(block, lambda i, te, nu, nv: (jnp.clip(i - 2, 0, nu[0] - 1), 0, 0)),
            scratch_shapes=[pltpu.VMEM((D, 2 * D_EXPERT), BF16), pltpu.VMEM((D_EXPERT, D), BF16),
                            pltpu.VMEM((2, MOE_ROWS, D), BF16), pltpu.VMEM((2, MOE_ROWS, D), F32)]),
        out_shape=jax.ShapeDtypeStruct(xs.shape, xs.dtype),
        compiler_params=pltpu.CompilerParams(dimension_semantics=("arbitrary",),
                                             vmem_limit_bytes=VMEM_LIMIT_BYTES),
        name="ffn",
    )(tile_expert, n_used, n_valid, xs, wg, wu, wd)


def _combine_kernel(slot_ref, next_slot_ref, w_ref, h_ref, x1_ref, g2_ref, fw_ref, wsgu_ref, wsd_ref, y_ref,
                    o_ref, buf, sem):
    tmc = h_ref.shape[0]
    i = pl.program_id(0)
    last = pl.num_programs(0) - 1

    def start_rows(slots, b):
        def body(j, c):
            for k in range(TOP_K):
                pltpu.make_async_copy(y_ref.at[pl.ds(slots[k, j], 1)], buf.at[b, k, pl.ds(j, 1)], sem.at[b]).start(
                    priority=k % 2)
            return c
        lax.fori_loop(0, tmc, body, 0)

    def wait_rows(b):
        for k in range(TOP_K):
            pltpu.make_async_copy(y_ref.at[pl.ds(0, tmc)], buf.at[b, k], sem.at[b]).wait()

    def for_buffer(step, fn):
        for b in range(2):
            @pl.when(step % 2 == b)
            def _():
                fn(b)

    @pl.when(i == 0)
    def _():
        start_rows(slot_ref, 0)

    @pl.when(i < last)
    def _():
        for_buffer(i + 1, lambda b: start_rows(next_slot_ref, b))

    gu = jnp.dot(_to_rows(h_ref[...].astype(F32)).astype(BF16), wsgu_ref[...], preferred_element_type=F32)
    act = _silu(gu[:, :D_EXPERT]) * gu[:, D_EXPERT:]
    shared = jnp.dot(act.astype(BF16), wsd_ref[...], preferred_element_type=F32)
    w = w_ref[...]
    lane = lax.broadcasted_iota(jnp.int32, w.shape, 1)

    def finish(b):
        wait_rows(b)
        routed = jnp.zeros((tmc, ROW_TILE, LANES), F32)
        for k in range(TOP_K):
            wk = jnp.broadcast_to(_lane_pick(w, lane, k), (tmc, LANES))
            routed = routed + _to_row_tiles(jnp.concatenate([wk] * ROW_TILE, axis=1)) * buf[b, k].astype(F32)
        acc = shared + _to_rows(routed)
        y = x1_ref[...] + g2_ref[0] * acc
        o_ref[...] = _rms(y) * fw_ref[...]

    for_buffer(i, finish)


def _combine(slot, wts, h2, x1, g2, fw, wsgu, wsd, ys, T, tmc):
    M, D = x1.shape
    const = lambda shape: pl.BlockSpec(shape, lambda i: (0,) * len(shape))
    tile = lambda w: pl.BlockSpec((tmc, w), lambda i: (i, 0))
    row_tiled = pl.BlockSpec((tmc, ROW_TILE, LANES), lambda i: (i, 0, 0))
    n_steps = M // tmc
    return pl.pallas_call(
        _combine_kernel,
        grid=(n_steps,),
        in_specs=[pl.BlockSpec((TOP_K, tmc), lambda i: (0, i), memory_space=pltpu.SMEM),
                  pl.BlockSpec((TOP_K, tmc), lambda i: (0, jnp.minimum(i + 1, n_steps - 1)), memory_space=pltpu.SMEM),
                  tile(GATE_LANES), row_tiled, tile(D),
                  pl.BlockSpec((1, 1, D), lambda i: (i // (T // tmc), 0, 0)),
                  const((1, D)), const(wsgu.shape), const(wsd.shape),
                  pl.BlockSpec(memory_space=pl.ANY)],
        out_specs=tile(D),
        out_shape=jax.ShapeDtypeStruct((M, D), F32),
        scratch_shapes=[pltpu.VMEM((2, TOP_K, tmc, ROW_TILE, LANES), ys.dtype), pltpu.SemaphoreType.DMA((2,))],
        compiler_params=pltpu.CompilerParams(dimension_semantics=("arbitrary",),
                                             vmem_limit_bytes=VMEM_LIMIT_BYTES),
        name="combine",
    )(slot, slot, wts, h2, x1, g2, fw, wsgu, wsd, ys)


def _pick_tile(n, want):
    t = min(n, want)
    assert n % t == 0 and t % CHUNK == 0, (n, want)
    return t


def kernel(x, c, w_ada, b_ada, norm1_w, w_in, conv_w, gdn_a_log, gdn_dt_bias, gdn_norm_w, hg_lb, hg_norm_w,
           w_out, norm2_w, w_router, router_bias, w_gate, w_up, w_down, ws_gate, ws_up, ws_down, final_norm_w):
    B, T, D = x.shape
    M = B * T
    depth = w_ada.shape[0]
    assert depth == 1 and T % CHUNK == 0 and B <= 8
    layer = 0
    tt = _pick_tile(T, 512)

    c_pad = jnp.pad(c, ((0, 8 - B), (0, 0)))
    mod = _ada(c_pad, w_ada[layer], b_ada[layer].reshape(1, -1))[:B]
    sh1, sc1, g1, sh2, sc2, g2 = (m.reshape(B, 1, D) for m in jnp.split(mod, 6, axis=-1))

    w = w_in[layer]
    qkv_w = 3 * GW
    sizes = (GW, HEADS, HEADS, GW, GW, GW, GW)
    offs = [qkv_w]
    for s in sizes:
        offs.append(offs[-1] + s)
    seg = lambda i: w[:, offs[i]:offs[i + 1]]
    small = jnp.pad(jnp.concatenate([seg(1), seg(2)], axis=1), ((0, 0), (0, GATE_LANES - 2 * HEADS)))
    w_all = jnp.concatenate([w[:, :qkv_w], seg(0), seg(3), seg(4), seg(5), seg(6), small], axis=1).astype(BF16)
    lane_pad = lambda v: jnp.pad(v.astype(F32).reshape(1, HEADS), ((0, 0), (HEADS, GATE_LANES - 2 * HEADS)))
    idx = jnp.arange(CUMSUM_ROWS)
    tri = ((idx[:, None] >= idx[None, :]) & (idx[:, None] // CHUNK == idx[None, :] // CHUNK)).astype(BF16)

    qa, ka, va, ga, sm, bcum, kb, ib, qb, gb = _inproj(
        layer, x, norm1_w[layer].reshape(1, D), sc1, sh1, w_all, conv_w[layer].astype(F32),
        lane_pad(gdn_a_log[layer]), lane_pad(gdn_dt_bias[layer]), hg_lb.astype(F32), tri, tt)

    gct = sm[:, :, HEADS:2 * HEADS].transpose(0, 2, 1).reshape(B, HEADS, T // CHUNK, CHUNK)
    oa = _gdn(qa, ka, va, ga, sm, gct, gdn_norm_w[layer].reshape(1, DH), tt)
    ob = _hgrn(qb, kb, ib, bcum, gb, hg_norm_w[layer].reshape(1, GW), tt)

    wo = w_out[layer].astype(BF16)
    x1, h2, eidx, rank, wts, cnt = _outproj(oa, ob, x, wo[:GW], wo[GW:], g1, norm2_w[layer].reshape(1, D), sc2, sh2,
                                            w_router[layer].T, router_bias[layer].reshape(N_EXPERTS, 1), tt)

    counts = cnt[:, 0].astype(jnp.int32)
    padded = (counts + MOE_ROWS - 1) // MOE_ROWS * MOE_ROWS
    ends = jnp.cumsum(padded)
    offsets = ends - padded
    n_tiles = (M * TOP_K) // MOE_ROWS + N_EXPERTS
    n_used = (ends[-1] // MOE_ROWS).astype(jnp.int32)
    tile_ids = jnp.minimum(jnp.arange(n_tiles + FFN_PIPELINE_STEPS, dtype=jnp.int32), n_used - 1)
    tile_expert = jnp.sum(((ends // MOE_ROWS)[None, :] <= tile_ids[:, None]).astype(jnp.int32), axis=1)
    tile_expert = jnp.minimum(tile_expert, N_EXPERTS - 1)
    slot = _slots(offsets.astype(jnp.int32), eidx, rank)
    first_tile = jnp.take(offsets, tile_expert) // MOE_ROWS
    n_valid = jnp.clip(jnp.take(counts, tile_expert) - (tile_ids - first_tile) * MOE_ROWS, 0, MOE_ROWS)

    assert D == ROW_TILE * LANES
    xs = _dispatch(slot, h2, n_tiles * MOE_ROWS, _pick_tile(M, 1024))
    ys = _ffn(tile_expert.astype(jnp.int32), n_used.reshape(1), n_valid.astype(jnp.int32), xs,
              w_gate[layer], w_up[layer], w_down[layer])
    wsgu = jnp.concatenate([ws_gate[layer], ws_up[layer]], axis=-1).astype(BF16)
    out = _combine(slot, wts.reshape(M, GATE_LANES), h2, x1.reshape(M, D), g2, final_norm_w.reshape(1, D),
                   wsgu, ws_down[layer].astype(BF16), ys, T, _pick_tile(T, 256))
    return out.reshape(B, T, D)
```

```python
import functools

import jax
import jax.numpy as jnp
from jax import lax
from jax.experimental import pallas as pl
from jax.experimental.pallas import tpu as pltpu

F32 = jnp.float32
BF16 = jnp.bfloat16

EPS = 1e-6
CHUNK = 64
SUB = 8
HEADS = 4
DH = 128
GW = HEADS * DH
CONV_K = 4
N_EXPERTS = 64
N_GROUPS = 8
GROUP_SIZE = N_EXPERTS // N_GROUPS
TOPK_GROUPS = 4
TOP_K = 8
D_EXPERT = 256
ROUTE_SCALE = 2.5
GATE_LANES = 128
GDN_CHUNKS_PER_ITER = 4
INV_BLOCK = 16
CUMSUM_ROWS = 128
MOE_ROWS = 1024
ROW_TILE, LANES = 8, 128

VMEM_LIMIT_BYTES = 56 * 1024 * 1024

ACT = BF16
ROW_DTYPE = F32


def _silu(x):
    return x * jax.nn.sigmoid(x)


def _dot(a, b):
    return jnp.dot(a.astype(BF16), b.astype(BF16), preferred_element_type=F32)


def _dot_nt(a, b):
    return lax.dot_general(a.astype(BF16), b.astype(BF16), (((1,), (1,)), ((), ())),
                           preferred_element_type=F32)


def _dot_tn(a, b):
    return lax.dot_general(a.astype(BF16), b.astype(BF16), (((0,), (0,)), ((), ())),
                           preferred_element_type=F32)


def _split2(x):
    hi = x.astype(BF16)
    lo = (x - hi.astype(F32)).astype(BF16)
    return hi, lo


def _dot3(a, b, dot=_dot):
    ah, al = _split2(a)
    bh, bl = _split2(b)
    return dot(ah, bh) + dot(ah, bl) + dot(al, bh)


def _cumsum_rows(tri, x):
    hi = x.astype(BF16)
    r = x - hi.astype(F32)
    mid = r.astype(BF16)
    lo = (r - mid.astype(F32)).astype(BF16)
    g = tri.shape[0]
    groups = []
    for r0 in range(0, x.shape[0], g):
        rows = slice(r0, r0 + g)
        groups.append(jnp.dot(tri, hi[rows], preferred_element_type=F32)
                      + jnp.dot(tri, mid[rows], preferred_element_type=F32)
                      + jnp.dot(tri, lo[rows], preferred_element_type=F32))
    return jnp.concatenate(groups, axis=0)


def _lane_pick(tile, lane, idx):
    return jnp.sum(jnp.where(lane == idx, tile, 0.0), axis=1, keepdims=True)


def _rms(x):
    return x * lax.rsqrt(jnp.mean(x * x, axis=-1, keepdims=True) + EPS)


def _to_rows(x3):
    r = x3.shape[0]
    xt = jnp.swapaxes(x3.reshape(r // ROW_TILE, ROW_TILE, ROW_TILE, LANES), 1, 2)
    return jnp.concatenate([xt[:, s].reshape(r, LANES) for s in range(ROW_TILE)], axis=1)


def _to_row_tiles(x):
    r = x.shape[0]
    xt = jnp.stack([x[:, s * LANES:(s + 1) * LANES].reshape(r // ROW_TILE, ROW_TILE, LANES) for s in range(ROW_TILE)],
                   axis=1)
    return jnp.swapaxes(xt, 1, 2).reshape(r, ROW_TILE, LANES)


def _ada_kernel(c_ref, w_ref, b_ref, o_ref):
    ca = _silu(c_ref[...])
    o_ref[...] = _dot3(ca, w_ref[...]) + b_ref[...]


def _ada(c_pad, w, b):
    rows, d = c_pad.shape
    n = w.shape[1]
    tn = 1024
    return pl.pallas_call(
        _ada_kernel,
        grid=(n // tn,),
        in_specs=[pl.BlockSpec((rows, d), lambda j: (0, 0)),
                  pl.BlockSpec((d, tn), lambda j: (0, j)),
                  pl.BlockSpec((1, tn), lambda j: (0, j))],
        out_specs=pl.BlockSpec((rows, tn), lambda j: (0, j)),
        out_shape=jax.ShapeDtypeStruct((rows, n), F32),
        compiler_params=pltpu.CompilerParams(dimension_semantics=("arbitrary",),
                                             vmem_limit_bytes=VMEM_LIMIT_BYTES),
        name="ada",
    )(c_pad, w, b)


def _inproj_kernel(layer, x_ref, n1_ref, sc_ref, sh_ref, w_ref, cw_ref, alog_ref, dt_ref, lb_ref, tri_ref,
                   qa_ref, ka_ref, va_ref, ga_ref, sm_ref, b_ref, kb_ref, ib_ref, qb_ref, gb_ref,
                   pbuf):
    tt = x_ref.shape[1]
    t = pl.program_id(1)

    h = _rms(x_ref[0]) * n1_ref[...]
    h = h * (1.0 + sc_ref[0]) + sh_ref[0]
    hb = h.astype(BF16)

    def proj(g, width=GW):
        return jnp.dot(hb, w_ref[:, g * GW:g * GW + width], preferred_element_type=F32)

    @pl.when(t == 0)
    def _():
        pbuf[:, 0:8, :] = jnp.zeros((3, 8, GW), F32)

    for g, out_ref in enumerate((qa_ref, ka_ref, va_ref)):
        cols = slice(g * GW, (g + 1) * GW)
        p = proj(g)
        pbuf[g, 8:8 + tt, :] = p
        y = p * cw_ref[CONV_K - 1:CONV_K, cols]
        for j in range(1, CONV_K):
            y = y + pbuf[g, 8 - j:8 - j + tt, :] * cw_ref[CONV_K - 1 - j:CONV_K - j, cols]
        pbuf[g, 0:8, :] = pbuf[g, tt:tt + 8, :]
        y = _silu(y)
        if g == 2:
            out_ref[0] = y.astype(out_ref.dtype)
        else:
            scale = DH ** -0.5 if g == 0 else 1.0
            for hh in range(HEADS):
                hs = slice(hh * DH, (hh + 1) * DH)
                yh = y[:, hs]
                inv = lax.rsqrt(jnp.sum(yh * yh, axis=-1, keepdims=True) + EPS)
                out_ref[0, :, hs] = (yh * inv * scale).astype(out_ref.dtype)

    ga_ref[0] = _silu(proj(3)).astype(ga_ref.dtype)

    ps = proj(8, GATE_LANES)
    lane = lax.broadcasted_iota(jnp.int32, ps.shape, 1)
    beta = jax.nn.sigmoid(ps)
    z = ps + dt_ref[...]
    softplus = jnp.maximum(z, 0.0) + jnp.log1p(jnp.exp(-jnp.abs(z)))
    g_log = -jnp.exp(alog_ref[...]) * softplus
    tri = tri_ref[...]
    gc = _cumsum_rows(tri, jnp.where((lane >= HEADS) & (lane < 2 * HEADS), g_log, 0.0))
    sm_ref[0] = jnp.where(lane < HEADS, beta, gc)

    hl = lb_ref[...]
    e = jnp.exp(hl - jnp.max(hl, axis=0, keepdims=True))
    lb = jnp.sum(e[0:layer + 1], axis=0, keepdims=True) / jnp.sum(e, axis=0, keepdims=True)
    fr = proj(4)
    logf = jnp.log(lb + (1.0 - lb) * jax.nn.sigmoid(fr))
    b_ref[0] = _cumsum_rows(tri, logf)
    kb_ref[0] = ((1.0 - lb) * jax.nn.sigmoid(-fr)).astype(kb_ref.dtype)
    ib_ref[0] = proj(5).astype(ib_ref.dtype)
    qb_ref[0] = _silu(proj(6)).astype(qb_ref.dtype)
    gb_ref[0] = _silu(proj(7)).astype(gb_ref.dtype)


def _inproj(layer, x, n1, sc1, sh1, w_all, conv_w, alog_pad, dt_pad, hg_lb, tri, tt):
    B, T, D = x.shape
    const = lambda shape: pl.BlockSpec(shape, lambda b, t: (0,) * len(shape), pipeline_mode=pl.Buffered(1))
    act = lambda dt: jax.ShapeDtypeStruct((B, T, GW), dt)
    tile = lambda w: pl.BlockSpec((1, tt, w), lambda b, t: (b, t, 0))
    per_batch = pl.BlockSpec((1, 1, D), lambda b, t: (b, 0, 0))
    return pl.pallas_call(
        functools.partial(_inproj_kernel, layer),
        grid=(B, T // tt),
        in_specs=[tile(D), const((1, D)), per_batch, per_batch,
                  const(w_all.shape), const(conv_w.shape), const((1, GATE_LANES)), const((1, GATE_LANES)),
                  const(hg_lb.shape), const(tri.shape)],
        out_specs=[tile(GW), tile(GW), tile(GW), tile(GW), tile(GATE_LANES), tile(GW),
                   tile(GW), tile(GW), tile(GW), tile(GW)],
        out_shape=[act(ACT), act(ACT), act(ACT), act(ACT),
                   jax.ShapeDtypeStruct((B, T, GATE_LANES), F32), act(F32),
                   act(ACT), act(ACT), act(ACT), act(ACT)],
        scratch_shapes=[pltpu.VMEM((3, tt + 8, GW), F32)],
        compiler_params=pltpu.CompilerParams(dimension_semantics=("arbitrary", "arbitrary"),
                                             vmem_limit_bytes=VMEM_LIMIT_BYTES),
        name="inproj",
    )(x, n1, sc1, sh1, w_all, conv_w, alog_pad, dt_pad, hg_lb, tri)


def _gdn_prep_kernel(q_ref, k_ref, v_ref, sm_ref, gct_ref, o_ref, qt_ref, m_ref, n_ref):
    tt = q_ref.shape[1]
    nc = tt // CHUNK
    row = lax.broadcasted_iota(jnp.int32, (CHUNK, CHUNK), 0)
    col = lax.broadcasted_iota(jnp.int32, (CHUNK, CHUNK), 1)
    causal = row >= col
    diag_blk = (row > col) & (row // INV_BLOCK == col // INV_BLOCK)
    off_blk = row // INV_BLOCK > col // INV_BLOCK
    eye = jnp.where(row == col, 1.0, 0.0)
    lane = lax.broadcasted_iota(jnp.int32, (CHUNK, GATE_LANES), 1)
    assert INV_BLOCK == 16 and CHUNK == 4 * INV_BLOCK

    def body(i, carry):
        chains = [(GDN_CHUNKS_PER_ITER * i + j, hh) for j in range(GDN_CHUNKS_PER_ITER) for hh in range(HEADS)]
        rows = [pl.ds(pl.multiple_of(c * CHUNK, CHUNK), CHUNK) for c, _ in chains]
        hs = [slice(hh * DH, (hh + 1) * DH) for _, hh in chains]
        n = range(len(chains))
        sm = [sm_ref[0, rows[j], :] for j in n]
        q = [q_ref[0, rows[j], hs[j]].astype(F32) for j in n]
        k = [k_ref[0, rows[j], hs[j]].astype(F32) for j in n]
        v = [v_ref[0, rows[j], hs[j]].astype(F32) for j in n]
        beta = [_lane_pick(sm[j], lane, chains[j][1]) for j in n]
        gcol = [_lane_pick(sm[j], lane, HEADS + chains[j][1]) for j in n]
        grow = [gct_ref[0, hh, pl.ds(c, 1), :] for c, hh in chains]
        decay = [jnp.exp(jnp.where(causal, gcol[j] - grow[j], -jnp.inf)) for j in n]
        kb = [k[j] * beta[j] for j in n]
        L = [_dot_nt(kb[j], k[j]) * decay[j] for j in n]
        dg = [jnp.where(diag_blk, L[j], 0.0) for j in n]
        off = [jnp.where(off_blk, L[j], 0.0) for j in n]
        dinv = [eye - dg[j] for j in n]
        pw = [_dot3(dg[j], dg[j]) for j in n]
        for _ in range(2):
            dinv = [dinv[j] + _dot3(dinv[j], pw[j]) for j in n]
            pw = [_dot3(pw[j], pw[j]) for j in n]
        dinv = [dinv[j] + _dot3(dinv[j], pw[j]) for j in n]
        f1 = [_dot(dinv[j], off[j]) for j in n]
        f2 = [_dot(f1[j], f1[j]) for j in n]
        f3 = [_dot(f1[j], f2[j]) for j in n]
        tinv = [_dot(eye - f1[j] + f2[j] - f3[j], dinv[j]) for j in n]
        eg = [jnp.exp(gcol[j]) for j in n]
        sol = [_dot(tinv[j], jnp.concatenate([v[j] * beta[j], kb[j] * eg[j]], axis=1)) for j in n]
        attn = [_dot_nt(q[j], k[j]) * decay[j] for j in n]
        k_tail = [k[j] * jnp.exp(gcol[j][CHUNK - 1:CHUNK, :] - gcol[j]) for j in n]
        au = [_dot(attn[j], sol[j]) for j in n]
        ku = [_dot_tn(k_tail[j], sol[j]) for j in n]
        for j, (c, hh) in enumerate(chains):
            o_ref[0, rows[j], hs[j]] = au[j][:, :DH]
            qt_ref[0, rows[j], hs[j]] = (q[j] * eg[j] - au[j][:, DH:]).astype(qt_ref.dtype)
            n_ref[0, hh, c] = ku[j][:, :DH].astype(n_ref.dtype)
            m_ref[0, hh, c] = (-ku[j][:, DH:]).astype(m_ref.dtype)
        return carry

    lax.fori_loop(0, nc // GDN_CHUNKS_PER_ITER, body, 0)


def _gdn_scan_kernel(o_ref, qt_ref, m_ref, n_ref, gct_ref, sg_ref, nw_ref, out_ref, s_ref):
    nb, tt = o_ref.shape[0], o_ref.shape[1]
    nc = tt // CHUNK

    @pl.when(pl.program_id(0) == 0)
    def _():
        s_ref[...] = jnp.zeros(s_ref.shape, F32)

    nw = nw_ref[...]

    def body(c, carry):
        rows = pl.ds(pl.multiple_of(c * CHUNK, CHUNK), CHUNK)
        for b in range(nb):
            for hh in range(HEADS):
                hs = slice(hh * DH, (hh + 1) * DH)
                S = s_ref[b, hh]
                Sb = S.astype(BF16)
                glast = gct_ref[b, hh, pl.ds(c, 1), :][:, CHUNK - 1:CHUNK]
                o = o_ref[b, rows, hs] + jnp.dot(qt_ref[b, rows, hs], Sb, preferred_element_type=F32)
                s_ref[b, hh] = (S * jnp.exp(glast) + jnp.dot(m_ref[b, hh, c], Sb, preferred_element_type=F32)
                                + n_ref[b, hh, c].astype(F32))
                o = _rms(o) * nw * sg_ref[b, rows, hs].astype(F32)
                out_ref[b, rows, hs] = o.astype(out_ref.dtype)
        return carry

    lax.fori_loop(0, nc, body, 0)


def _gdn(q, k, v, sg, sm, gct, nw, tt):
    B, T, _ = q.shape
    nc = tt // CHUNK
    assert nc % GDN_CHUNKS_PER_ITER == 0
    n_chunks = T // CHUNK
    tile = lambda w: pl.BlockSpec((1, tt, w), lambda b, t: (b, t, 0))
    mat = jax.ShapeDtypeStruct((B, HEADS, n_chunks, DH, DH), ACT)
    o_part, qt, m, n = pl.pallas_call(
        _gdn_prep_kernel,
        grid=(B, T // tt),
        in_specs=[tile(GW), tile(GW), tile(GW), tile(GATE_LANES),
                  pl.BlockSpec((1, HEADS, nc, CHUNK), lambda b, t: (b, 0, t, 0))],
        out_specs=[tile(GW), tile(GW),
                   pl.BlockSpec((1, HEADS, nc, DH, DH), lambda b, t: (b, 0, t, 0, 0)),
                   pl.BlockSpec((1, HEADS, nc, DH, DH), lambda b, t: (b, 0, t, 0, 0))],
        out_shape=[jax.ShapeDtypeStruct((B, T, GW), F32), jax.ShapeDtypeStruct((B, T, GW), ACT), mat, mat],
        compiler_params=pltpu.CompilerParams(dimension_semantics=("arbitrary", "arbitrary"),
                                             vmem_limit_bytes=VMEM_LIMIT_BYTES),
        name="gdn_prep",
    )(q, k, v, sm, gct)

    full = lambda w: pl.BlockSpec((B, tt, w), lambda t: (0, t, 0))
    mats = pl.BlockSpec((B, HEADS, nc, DH, DH), lambda t: (0, 0, t, 0, 0))
    return pl.pallas_call(
        _gdn_scan_kernel,
        grid=(T // tt,),
        in_specs=[full(GW), full(GW), mats, mats,
                  pl.BlockSpec((B, HEADS, nc, CHUNK), lambda t: (0, 0, t, 0)),
                  full(GW), pl.BlockSpec((1, DH), lambda t: (0, 0))],
        out_specs=full(GW),
        out_shape=jax.ShapeDtypeStruct((B, T, GW), ACT),
        scratch_shapes=[pltpu.VMEM((B, HEADS, DH, DH), F32)],
        compiler_params=pltpu.CompilerParams(dimension_semantics=("arbitrary",),
                                             vmem_limit_bytes=VMEM_LIMIT_BYTES),
        name="gdn_scan",
    )(o_part, qt, m, n, gct, sg, nw)


def _hgrn_kernel(q_ref, k_ref, v_ref, b_ref, sg_ref, nw_ref, o_ref, st_ref):
    tt = q_ref.shape[1]
    nc = tt // CHUNK

    @pl.when(pl.program_id(1) == 0)
    def _():
        st_ref[...] = jnp.zeros(st_ref.shape, F32)

    row = lax.broadcasted_iota(jnp.int32, (CHUNK, CHUNK), 0)
    col = lax.broadcasted_iota(jnp.int32, (CHUNK, CHUNK), 1)
    diag_block = ((col // SUB) == (row // SUB)) & (col <= row)
    heads = range(HEADS)
    hs = [slice(hh * DH, (hh + 1) * DH) for hh in heads]

    def body(c, carry):
        rows = pl.ds(pl.multiple_of(c * CHUNK, CHUNK), CHUNK)
        q = [q_ref[0, rows, hs[h]].astype(F32) for h in heads]
        k = [k_ref[0, rows, hs[h]].astype(F32) for h in heads]
        v = [v_ref[0, rows, hs[h]].astype(F32) for h in heads]
        b = [b_ref[0, rows, hs[h]] for h in heads]
        blast = [b[h][CHUNK - 1:CHUNK, :] for h in heads]
        st = [st_ref[h] for h in heads]
        o = [_dot_nt(q[h] * jnp.exp(b[h]), st[h]) for h in heads]
        k_tail = [k[h] * jnp.exp(blast[h] - b[h]) for h in heads]
        for h in heads:
            st_ref[h] = st[h] * jnp.exp(blast[h]) + _dot_tn(v[h], k_tail[h])

        blocks = [[jnp.zeros((SUB, CHUNK), F32)] for _ in heads]
        for i in range(1, CHUNK // SUB):
            lo, hi = i * SUB, (i + 1) * SUB
            for h in heads:
                r = b[h][lo:lo + 1, :]
                qi = q[h][lo:hi] * jnp.exp(b[h][lo:hi] - r)
                kj = k[h][:lo] * jnp.exp(jnp.minimum(r - b[h][:lo], 0.0))
                kj = jnp.concatenate([kj, jnp.zeros((CHUNK - lo, DH), F32)], axis=0)
                blocks[h].append(_dot_nt(qi, kj))
        a = []
        for h in heads:
            f = jnp.exp(jnp.minimum(b[h] - pltpu.roll(b[h], 1, 0), 0.0))
            e = None
            a_diag = jnp.zeros((CHUNK, CHUNK), F32)
            for delta in range(SUB):
                if delta == 0:
                    term = q[h] * k[h]
                else:
                    fsh = f if delta == 1 else pltpu.roll(f, delta - 1, 0)
                    e = fsh if e is None else e * fsh
                    term = q[h] * pltpu.roll(k[h], delta, 0) * e
                colv = jnp.sum(term, axis=1, keepdims=True)
                a_diag = jnp.where(row - col == delta, colv, a_diag)
            a.append(jnp.where(diag_block, a_diag, jnp.concatenate(blocks[h], axis=0)))

        o = [o[h] + _dot(a[h], v[h]) for h in heads]
        o = jnp.concatenate(o, axis=1)
        o = _rms(o) * nw_ref[...] * sg_ref[0, rows, :].astype(F32)
        o_ref[0, rows, :] = o.astype(o_ref.dtype)
        return carry

    lax.fori_loop(0, nc, body, 0)


def _hgrn(q, k, v, b, sg, nw, tt):
    B, T, _ = q.shape
    tile = pl.BlockSpec((1, tt, GW), lambda bi, t: (bi, t, 0))
    return pl.pallas_call(
        _hgrn_kernel,
        grid=(B, T // tt),
        in_specs=[tile, tile, tile, tile, tile, pl.BlockSpec((1, GW), lambda bi, t: (0, 0))],
        out_specs=tile,
        out_shape=jax.ShapeDtypeStruct((B, T, GW), ACT),
        scratch_shapes=[pltpu.VMEM((HEADS, DH, DH), F32)],
        compiler_params=pltpu.CompilerParams(dimension_semantics=("arbitrary", "arbitrary"),
                                             vmem_limit_bytes=VMEM_LIMIT_BYTES),
        name="hgrn",
    )(q, k, v, b, sg, nw)


def _outproj_kernel(oa_ref, ob_ref, x_ref, wa_ref, wb_ref, g1_ref, n2_ref, sc_ref, sh_ref, wr_ref, rb_ref,
                    x1_ref, h2_ref, eidx_ref, rank_ref, wts_ref, cnt_ref):
    tm = x_ref.shape[1]

    @pl.when((pl.program_id(0) == 0) & (pl.program_id(1) == 0))
    def _():
        cnt_ref[...] = jnp.zeros(cnt_ref.shape, F32)

    mix = (jnp.dot(oa_ref[0], wa_ref[...], preferred_element_type=F32)
           + jnp.dot(ob_ref[0], wb_ref[...], preferred_element_type=F32))
    x1 = x_ref[0] + g1_ref[0] * mix
    x1_ref[0] = x1
    h2 = _rms(x1) * n2_ref[...]
    h2 = h2 * (1.0 + sc_ref[0]) + sh_ref[0]
    h2_ref[...] = _to_row_tiles(h2).astype(h2_ref.dtype)

    scores = jax.nn.sigmoid(_dot3(wr_ref[...], h2, dot=_dot_nt))
    sel = scores + rb_ref[...]
    sub = lax.broadcasted_iota(jnp.int32, (GROUP_SIZE, tm), 0)
    neg = -jnp.inf
    groups = range(N_GROUPS)

    def take_max(blk):
        m = jnp.max(blk, axis=0, keepdims=True)
        first = jnp.min(jnp.where(blk == m, sub, GROUP_SIZE), axis=0, keepdims=True)
        hit = sub == first
        return m, hit, jnp.where(hit, neg, blk)

    blk_of = lambda a, g: a[g * GROUP_SIZE:(g + 1) * GROUP_SIZE]
    sel_blk = [blk_of(sel, g) for g in groups]
    group_score = jnp.zeros((N_GROUPS, tm), F32)
    for g in groups:
        m1, _, rest = take_max(sel_blk[g])
        m2 = jnp.max(rest, axis=0, keepdims=True)
        group_score = jnp.where(sub == g, m1 + m2, group_score)
    group_on = jnp.zeros((N_GROUPS, tm), F32)
    for _ in range(TOPK_GROUPS):
        _, hit, group_score = take_max(group_score)
        group_on = jnp.where(hit, 1.0, group_on)

    cand = [jnp.where(group_on[g:g + 1] > 0.0, sel_blk[g], neg) for g in groups]
    picked = [jnp.zeros((GROUP_SIZE, tm), F32) for _ in groups]
    chosen = []
    for _ in range(TOP_K):
        m = jnp.max(functools.reduce(jnp.maximum, cand), axis=0, keepdims=True)
        first = functools.reduce(jnp.minimum, [jnp.where(cand[g] == m, sub + g * GROUP_SIZE, N_EXPERTS)
                                               for g in groups])
        first = jnp.min(first, axis=0, keepdims=True)
        chosen.append(first)
        for g in groups:
            hit = (sub + g * GROUP_SIZE) == first
            picked[g] = jnp.where(hit, 1.0, picked[g])
            cand[g] = jnp.where(hit, neg, cand[g])

    picked_all = jnp.concatenate(picked, axis=0)
    r_i = lax.broadcasted_iota(jnp.int32, (tm, tm), 0)
    c_i = lax.broadcasted_iota(jnp.int32, (tm, tm), 1)
    earlier = jnp.where(r_i < c_i, 1.0, 0.0).astype(BF16)
    before = jnp.dot(picked_all.astype(BF16), earlier, preferred_element_type=F32) + cnt_ref[:, 0:1]
    cnt_ref[...] = cnt_ref[...] + jnp.sum(picked_all, axis=1, keepdims=True)

    def pick_value(table, first):
        parts = [jnp.where((sub + g * GROUP_SIZE) == first, blk_of(table, g), 0.0) for g in groups]
        return jnp.sum(functools.reduce(jnp.add, parts), axis=0, keepdims=True)

    w_k = [pick_value(scores, f) for f in chosen]
    denom = functools.reduce(jnp.add, w_k)
    eidx = jnp.zeros((TOP_K, tm), jnp.int32)
    rank = jnp.zeros((TOP_K, tm), jnp.int32)
    wts = jnp.zeros((TOP_K, tm), F32)
    for k in range(TOP_K):
        eidx = jnp.where(sub == k, chosen[k], eidx)
        rank = jnp.where(sub == k, pick_value(before, chosen[k]).astype(jnp.int32), rank)
        wts = jnp.where(sub == k, w_k[k] / denom * ROUTE_SCALE, wts)
    eidx_ref[...] = eidx
    rank_ref[...] = rank
    pad = jnp.zeros((GATE_LANES - TOP_K, tm), F32)
    wts_ref[0] = jnp.concatenate([wts, pad], axis=0).T


def _outproj(oa, ob, x, wa, wb, g1, n2, sc2, sh2, wr_t, rb, tm):
    B, T, D = x.shape
    nt = T // tm
    const = lambda shape: pl.BlockSpec(shape, lambda b, t: (0,) * len(shape))
    tile = lambda w: pl.BlockSpec((1, tm, w), lambda b, t: (b, t, 0))
    per_batch = pl.BlockSpec((1, 1, D), lambda b, t: (b, 0, 0))
    picks = pl.BlockSpec((TOP_K, tm), lambda b, t: (0, b * nt + t))
    return pl.pallas_call(
        _outproj_kernel,
        grid=(B, nt),
        in_specs=[tile(GW), tile(GW), tile(D), const(wa.shape), const(wb.shape), per_batch,
                  const((1, D)), per_batch, per_batch, const(wr_t.shape), const(rb.shape)],
        out_specs=[tile(D), pl.BlockSpec((tm, ROW_TILE, LANES), lambda b, t: (b * nt + t, 0, 0)),
                   picks, picks, tile(GATE_LANES), const((N_EXPERTS, GATE_LANES))],
        out_shape=[jax.ShapeDtypeStruct((B, T, D), F32), jax.ShapeDtypeStruct((B * T, ROW_TILE, LANES), ROW_DTYPE),
                   jax.ShapeDtypeStruct((TOP_K, B * T), jnp.int32), jax.ShapeDtypeStruct((TOP_K, B * T), jnp.int32),
                   jax.ShapeDtypeStruct((B, T, GATE_LANES), F32),
                   jax.ShapeDtypeStruct((N_EXPERTS, GATE_LANES), F32)],
        compiler_params=pltpu.CompilerParams(dimension_semantics=("arbitrary", "arbitrary"),
                                             vmem_limit_bytes=VMEM_LIMIT_BYTES),
        name="outproj",
    )(oa, ob, x, wa, wb, g1, n2, sc2, sh2, wr_t, rb)


def _slots_kernel(off_ref, eidx_ref, rank_ref, slot_ref):
    eidx = eidx_ref[...]

    def add_expert(e, acc):
        return acc + jnp.where(eidx == e, off_ref[e], 0)

    slot_ref[...] = lax.fori_loop(0, N_EXPERTS, add_expert, rank_ref[...])


def _slots(offsets, eidx, rank):
    k, m = eidx.shape
    tile = pl.BlockSpec((k, m), lambda i: (0, 0))
    return pl.pallas_call(
        _slots_kernel,
        grid=(1,),
        in_specs=[pl.BlockSpec(memory_space=pltpu.SMEM), tile, tile],
        out_specs=tile,
        out_shape=jax.ShapeDtypeStruct((k, m), jnp.int32),
        compiler_params=pltpu.CompilerParams(dimension_semantics=("arbitrary",),
                                             vmem_limit_bytes=VMEM_LIMIT_BYTES),
        name="slots",
    )(offsets, eidx, rank)


def _dispatch_kernel(slot_ref, h_ref, xs_ref, sem):
    tmd = h_ref.shape[0]

    def start_rows(j, c):
        for k in range(TOP_K):
            pltpu.make_async_copy(h_ref.at[pl.ds(j, 1)], xs_ref.at[pl.ds(slot_ref[k, j], 1)], sem).start(
                priority=k % 2)
        return c

    lax.fori_loop(0, tmd, start_rows, 0)
    for _ in range(TOP_K):
        pltpu.make_async_copy(h_ref, xs_ref.at[pl.ds(0, tmd)], sem).wait()


def _dispatch(slot, h2, n_rows, tmd):
    M = h2.shape[0]
    return pl.pallas_call(
        _dispatch_kernel,
        grid=(M // tmd,),
        in_specs=[pl.BlockSpec((TOP_K, tmd), lambda i: (0, i), memory_space=pltpu.SMEM),
                  pl.BlockSpec((tmd, ROW_TILE, LANES), lambda i: (i, 0, 0))],
        out_specs=pl.BlockSpec(memory_space=pl.ANY),
        out_shape=jax.ShapeDtypeStruct((n_rows, ROW_TILE, LANES), h2.dtype),
        scratch_shapes=[pltpu.SemaphoreType.DMA],
        compiler_params=pltpu.CompilerParams(dimension_semantics=("arbitrary",),
                                             vmem_limit_bytes=VMEM_LIMIT_BYTES),
        name="dispatch",
    )(slot, h2)


FFN_PIPELINE_STEPS = 2


def _ffn_kernel(te_ref, nu_ref, nv_ref, x_ref, wg_ref, wu_ref, wd_ref, y_ref, wgu_s, wd_s, xstd, ystd):
    i = pl.program_id(0)
    row = lax.broadcasted_iota(jnp.int32, (MOE_ROWS, 1), 0)
    tile_mm = jnp.maximum(i - 1, 0)
    tile_out = jnp.maximum(i - 2, 0)

    @pl.when(i == 0)
    def _():
        xstd[...] = jnp.zeros(xstd.shape, BF16)
        ystd[...] = jnp.zeros(ystd.shape, F32)

    @pl.when(i < nu_ref[0] + FFN_PIPELINE_STEPS)
    def _():
        @pl.when((i == 0) | (te_ref[tile_mm] != te_ref[tile_out]))
        def _():
            wgu_s[:, :D_EXPERT] = wg_ref[0].astype(BF16)
            wgu_s[:, D_EXPERT:] = wu_ref[0].astype(BF16)
            wd_s[...] = wd_ref[0].astype(BF16)

        cur = i % 2
        xstd[cur] = jnp.where(row < nv_ref[i], _to_rows(x_ref[...].astype(F32)), 0.0).astype(BF16)
        y_ref[...] = _to_row_tiles(ystd[cur]).astype(y_ref.dtype)
        gu = jnp.dot(xstd[1 - cur], wgu_s[...], preferred_element_type=F32)
        act = _silu(gu[:, :D_EXPERT]) * gu[:, D_EXPERT:]
        ystd[1 - cur] = jnp.dot(act.astype(BF16), wd_s[...], preferred_element_type=F32)


def _ffn(tile_expert, n_used, n_valid, xs, wg, wu, wd):
    D = wg.shape[1]
    n_tiles = xs.shape[0] // MOE_ROWS
    block = (MOE_ROWS, ROW_TILE, LANES)
    expert = lambda i, te, nu, nv: (te[jnp.maximum(i - 1, 0)], 0, 0)
    return pl.pallas_call(
        _ffn_kernel,
        grid_spec=pltpu.PrefetchScalarGridSpec(
            num_scalar_prefetch=3,
            grid=(n_tiles + FFN_PIPELINE_STEPS,),
            in_specs=[pl.BlockSpec(block, lambda i, te, nu, nv: (jnp.minimum(i, nu[0] - 1), 0, 0)),
                      pl.BlockSpec((1, D, D_EXPERT), expert),
                      pl.BlockSpec((1, D, D_EXPERT), expert),
                      pl.BlockSpec((1, D_EXPERT, D), expert)],
            out_specs=pl.BlockSpec(block, lambda i, te, nu, nv: (jnp.clip(i - 2, 0, nu[0] - 1), 0, 0)),
            scratch_shapes=[pltpu.VMEM((D, 2 * D_EXPERT), BF16), pltpu.VMEM((D_EXPERT, D), BF16),
                            pltpu.VMEM((2, MOE_ROWS, D), BF16), pltpu.VMEM((2, MOE_ROWS, D), F32)]),
        out_shape=jax.ShapeDtypeStruct(xs.shape, xs.dtype),
        compiler_params=pltpu.CompilerParams(dimension_semantics=("arbitrary",),
                                             vmem_limit_bytes=VMEM_LIMIT_BYTES),
        name="ffn",
    )(tile_expert, n_used, n_valid, xs, wg, wu, wd)


def _combine_kernel(slot_ref, next_slot_ref, w_ref, h_ref, x1_ref, g2_ref, fw_ref, wsgu_ref, wsd_ref, y_ref,
                    o_ref, buf, sem):
    tmc = h_ref.shape[0]
    i = pl.program_id(0)
    last = pl.num_programs(0) - 1

    def start_rows(slots, b):
        def body(j, c):
            for k in range(TOP_K):
                pltpu.make_async_copy(y_ref.at[pl.ds(slots[k, j], 1)], buf.at[b, k, pl.ds(j, 1)], sem.at[b]).start(
                    priority=k % 2)
            return c
        lax.fori_loop(0, tmc, body, 0)

    def wait_rows(b):
        for k in range(TOP_K):
            pltpu.make_async_copy(y_ref.at[pl.ds(0, tmc)], buf.at[b, k], sem.at[b]).wait()

    def for_buffer(step, fn):
        for b in range(2):
            @pl.when(step % 2 == b)
            def _():
                fn(b)

    @pl.when(i == 0)
    def _():
        start_rows(slot_ref, 0)

    @pl.when(i < last)
    def _():
        for_buffer(i + 1, lambda b: start_rows(next_slot_ref, b))

    gu = jnp.dot(_to_rows(h_ref[...].astype(F32)).astype(BF16), wsgu_ref[...], preferred_element_type=F32)
    act = _silu(gu[:, :D_EXPERT]) * gu[:, D_EXPERT:]
    shared = jnp.dot(act.astype(BF16), wsd_ref[...], preferred_element_type=F32)
    w = w_ref[...]
    lane = lax.broadcasted_iota(jnp.int32, w.shape, 1)

    def finish(b):
        wait_rows(b)
        routed = jnp.zeros((tmc, ROW_TILE, LANES), F32)
        for k in range(TOP_K):
            wk = jnp.broadcast_to(_lane_pick(w, lane, k), (tmc, LANES))
            routed = routed + _to_row_tiles(jnp.concatenate([wk] * ROW_TILE, axis=1)) * buf[b, k].astype(F32)
        acc = shared + _to_rows(routed)
        y = x1_ref[...] + g2_ref[0] * acc
        o_ref[...] = _rms(y) * fw_ref[...]

    for_buffer(i, finish)


def _combine(slot, wts, h2, x1, g2, fw, wsgu, wsd, ys, T, tmc):
    M, D = x1.shape
    const = lambda shape: pl.BlockSpec(shape, lambda i: (0,) * len(shape))
    tile = lambda w: pl.BlockSpec((tmc, w), lambda i: (i, 0))
    row_tiled = pl.BlockSpec((tmc, ROW_TILE, LANES), lambda i: (i, 0, 0))
    n_steps = M // tmc
    return pl.pallas_call(
        _combine_kernel,
        grid=(n_steps,),
        in_specs=[pl.BlockSpec((TOP_K, tmc), lambda i: (0, i), memory_space=pltpu.SMEM),
                  pl.BlockSpec((TOP_K, tmc), lambda i: (0, jnp.minimum(i + 1, n_steps - 1)), memory_space=pltpu.SMEM),
                  tile(GATE_LANES), row_tiled, tile(D),
                  pl.BlockSpec((1, 1, D), lambda i: (i // (T // tmc), 0, 0)),
                  const((1, D)), const(wsgu.shape), const(wsd.shape),
                  pl.BlockSpec(memory_space=pl.ANY)],
        out_specs=tile(D),
        out_shape=jax.ShapeDtypeStruct((M, D), F32),
        scratch_shapes=[pltpu.VMEM((2, TOP_K, tmc, ROW_TILE, LANES), ys.dtype), pltpu.SemaphoreType.DMA((2,))],
        compiler_params=pltpu.CompilerParams(dimension_semantics=("arbitrary",),
                                             vmem_limit_bytes=VMEM_LIMIT_BYTES),
        name="combine",
    )(slot, slot, wts, h2, x1, g2, fw, wsgu, wsd, ys)


def _pick_tile(n, want):
    t = min(n, want)
    assert n % t == 0 and t % CHUNK == 0, (n, want)
    return t


def kernel(x, c, w_ada, b_ada, norm1_w, w_in, conv_w, gdn_a_log, gdn_dt_bias, gdn_norm_w, hg_lb, hg_norm_w,
           w_out, norm2_w, w_router, router_bias, w_gate, w_up, w_down, ws_gate, ws_up, ws_down, final_norm_w):
    B, T, D = x.shape
    M = B * T
    depth = w_ada.shape[0]
    assert depth == 1 and T % CHUNK == 0 and B <= 8
    layer = 0
    tt = _pick_tile(T, 512)

    c_pad = jnp.pad(c, ((0, 8 - B), (0, 0)))
    mod = _ada(c_pad, w_ada[layer], b_ada[layer].reshape(1, -1))[:B]
    sh1, sc1, g1, sh2, sc2, g2 = (m.reshape(B, 1, D) for m in jnp.split(mod, 6, axis=-1))

    w = w_in[layer]
    qkv_w = 3 * GW
    sizes = (GW, HEADS, HEADS, GW, GW, GW, GW)
    offs = [qkv_w]
    for s in sizes:
        offs.append(offs[-1] + s)
    seg = lambda i: w[:, offs[i]:offs[i + 1]]
    small = jnp.pad(jnp.concatenate([seg(1), seg(2)], axis=1), ((0, 0), (0, GATE_LANES - 2 * HEADS)))
    w_all = jnp.concatenate([w[:, :qkv_w], seg(0), seg(3), seg(4), seg(5), seg(6), small], axis=1).astype(BF16)
    lane_pad = lambda v: jnp.pad(v.astype(F32).reshape(1, HEADS), ((0, 0), (HEADS, GATE_LANES - 2 * HEADS)))
    idx = jnp.arange(CUMSUM_ROWS)
    tri = ((idx[:, None] >= idx[None, :]) & (idx[:, None] // CHUNK == idx[None, :] // CHUNK)).astype(BF16)

    qa, ka, va, ga, sm, bcum, kb, ib, qb, gb = _inproj(
        layer, x, norm1_w[layer].reshape(1, D), sc1, sh1, w_all, conv_w[layer].astype(F32),
        lane_pad(gdn_a_log[layer]), lane_pad(gdn_dt_bias[layer]), hg_lb.astype(F32), tri, _pick_tile(T, 1024))

    gct = sm[:, :, HEADS:2 * HEADS].transpose(0, 2, 1).reshape(B, HEADS, T // CHUNK, CHUNK)
    oa = _gdn(qa, ka, va, ga, sm, gct, gdn_norm_w[layer].reshape(1, DH), tt)
    ob = _hgrn(qb, kb, ib, bcum, gb, hg_norm_w[layer].reshape(1, GW), tt)

    wo = w_out[layer].astype(BF16)
    x1, h2, eidx, rank, wts, cnt = _outproj(oa, ob, x, wo[:GW], wo[GW:], g1, norm2_w[layer].reshape(1, D), sc2, sh2,
                                            w_router[layer].T, router_bias[layer].reshape(N_EXPERTS, 1), tt)

    counts = cnt[:, 0].astype(jnp.int32)
    padded = (counts + MOE_ROWS - 1) // MOE_ROWS * MOE_ROWS
    ends = jnp.cumsum(padded)
    offsets = ends - padded
    n_tiles = (M * TOP_K) // MOE_ROWS + N_EXPERTS
    n_used = (ends[-1] // MOE_ROWS).astype(jnp.int32)
    tile_ids = jnp.minimum(jnp.arange(n_tiles + FFN_PIPELINE_STEPS, dtype=jnp.int32), n_used - 1)
    tile_expert = jnp.sum(((ends // MOE_ROWS)[None, :] <= tile_ids[:, None]).astype(jnp.int32), axis=1)
    tile_expert = jnp.minimum(tile_expert, N_EXPERTS - 1)
    slot = _slots(offsets.astype(jnp.int32), eidx, rank)
    first_tile = jnp.take(offsets, tile_expert) // MOE_ROWS
    n_valid = jnp.clip(jnp.take(counts, tile_expert) - (tile_ids - first_tile) * MOE_ROWS, 0, MOE_ROWS)

    assert D == ROW_TILE * LANES
    xs = _dispatch(slot, h2, n_tiles * MOE_ROWS, _pick_tile(M, 1024))
    ys = _ffn(tile_expert.astype(jnp.int32), n_used.reshape(1), n_valid.astype(jnp.int32), xs,
              w_gate[layer], w_up[layer], w_down[layer])
    wsgu = jnp.concatenate([ws_gate[layer], ws_up[layer]], axis=-1).astype(BF16)
    out = _combine(slot, wts.reshape(M, GATE_LANES), h2, x1.reshape(M, D), g2, final_norm_w.reshape(1, D),
                   wsgu, ws_down[layer].astype(BF16), ys, T, _pick_tile(T, 256))
    return out.reshape(B, T, D)
```

```python
import functools

import jax
import jax.numpy as jnp
from jax import lax
from jax.experimental import pallas as pl
from jax.experimental.pallas import tpu as pltpu

F32 = jnp.float32
BF16 = jnp.bfloat16

EPS = 1e-6
CHUNK = 64
SUB = 8
HEADS = 4
DH = 128
GW = HEADS * DH
CONV_K = 4
N_EXPERTS = 64
N_GROUPS = 8
GROUP_SIZE = N_EXPERTS // N_GROUPS
TOPK_GROUPS = 4
TOP_K = 8
D_EXPERT = 256
ROUTE_SCALE = 2.5
GATE_LANES = 128
GDN_CHUNKS_PER_ITER = 8
INV_BLOCK = 16
CUMSUM_ROWS = 128
MOE_ROWS = 1024
ROW_TILE, LANES = 8, 128

VMEM_LIMIT_BYTES = 56 * 1024 * 1024

ACT = BF16
ROW_DTYPE = F32


def _silu(x):
    return x * jax.nn.sigmoid(x)


def _dot(a, b):
    return jnp.dot(a.astype(BF16), b.astype(BF16), preferred_element_type=F32)


def _dot_nt(a, b):
    return lax.dot_general(a.astype(BF16), b.astype(BF16), (((1,), (1,)), ((), ())),
                           preferred_element_type=F32)


def _dot_tn(a, b):
    return lax.dot_general(a.astype(BF16), b.astype(BF16), (((0,), (0,)), ((), ())),
                           preferred_element_type=F32)


def _split2(x):
    hi = x.astype(BF16)
    lo = (x - hi.astype(F32)).astype(BF16)
    return hi, lo


def _dot3(a, b, dot=_dot):
    ah, al = _split2(a)
    bh, bl = _split2(b)
    return dot(ah, bh) + dot(ah, bl) + dot(al, bh)


def _cumsum_rows(tri, x):
    hi = x.astype(BF16)
    r = x - hi.astype(F32)
    mid = r.astype(BF16)
    lo = (r - mid.astype(F32)).astype(BF16)
    g = tri.shape[0]
    groups = []
    for r0 in range(0, x.shape[0], g):
        rows = slice(r0, r0 + g)
        groups.append(jnp.dot(tri, hi[rows], preferred_element_type=F32)
                      + jnp.dot(tri, mid[rows], preferred_element_type=F32)
                      + jnp.dot(tri, lo[rows], preferred_element_type=F32))
    return jnp.concatenate(groups, axis=0)


def _lane_pick(tile, lane, idx):
    return jnp.sum(jnp.where(lane == idx, tile, 0.0), axis=1, keepdims=True)


def _rms(x):
    return x * lax.rsqrt(jnp.mean(x * x, axis=-1, keepdims=True) + EPS)


def _to_rows(x3):
    r = x3.shape[0]
    xt = jnp.swapaxes(x3.reshape(r // ROW_TILE, ROW_TILE, ROW_TILE, LANES), 1, 2)
    return jnp.concatenate([xt[:, s].reshape(r, LANES) for s in range(ROW_TILE)], axis=1)


def _to_row_tiles(x):
    r = x.shape[0]
    xt = jnp.stack([x[:, s * LANES:(s + 1) * LANES].reshape(r // ROW_TILE, ROW_TILE, LANES) for s in range(ROW_TILE)],
                   axis=1)
    return jnp.swapaxes(xt, 1, 2).reshape(r, ROW_TILE, LANES)


def _ada_kernel(c_ref, w_ref, b_ref, o_ref):
    ca = _silu(c_ref[...])
    o_ref[...] = _dot3(ca, w_ref[...]) + b_ref[...]


def _ada(c_pad, w, b):
    rows, d = c_pad.shape
    n = w.shape[1]
    tn = 1024
    return pl.pallas_call(
        _ada_kernel,
        grid=(n // tn,),
        in_specs=[pl.BlockSpec((rows, d), lambda j: (0, 0)),
                  pl.BlockSpec((d, tn), lambda j: (0, j)),
                  pl.BlockSpec((1, tn), lambda j: (0, j))],
        out_specs=pl.BlockSpec((rows, tn), lambda j: (0, j)),
        out_shape=jax.ShapeDtypeStruct((rows, n), F32),
        compiler_params=pltpu.CompilerParams(dimension_semantics=("arbitrary",),
                                             vmem_limit_bytes=VMEM_LIMIT_BYTES),
        name="ada",
    )(c_pad, w, b)


def _inproj_kernel(layer, x_ref, n1_ref, sc_ref, sh_ref, w_ref, cw_ref, alog_ref, dt_ref, lb_ref, tri_ref,
                   qa_ref, ka_ref, va_ref, ga_ref, sm_ref, b_ref, kb_ref, ib_ref, qb_ref, gb_ref,
                   pbuf):
    tt = x_ref.shape[1]
    t = pl.program_id(1)

    h = _rms(x_ref[0]) * n1_ref[...]
    h = h * (1.0 + sc_ref[0]) + sh_ref[0]
    hb = h.astype(BF16)

    def proj(g, width=GW):
        return jnp.dot(hb, w_ref[:, g * GW:g * GW + width], preferred_element_type=F32)

    @pl.when(t == 0)
    def _():
        pbuf[:, 0:8, :] = jnp.zeros((3, 8, GW), F32)

    for g, out_ref in enumerate((qa_ref, ka_ref, va_ref)):
        cols = slice(g * GW, (g + 1) * GW)
        p = proj(g)
        pbuf[g, 8:8 + tt, :] = p
        y = p * cw_ref[CONV_K - 1:CONV_K, cols]
        for j in range(1, CONV_K):
            y = y + pbuf[g, 8 - j:8 - j + tt, :] * cw_ref[CONV_K - 1 - j:CONV_K - j, cols]
        pbuf[g, 0:8, :] = pbuf[g, tt:tt + 8, :]
        y = _silu(y)
        if g == 2:
            out_ref[0] = y.astype(out_ref.dtype)
        else:
            scale = DH ** -0.5 if g == 0 else 1.0
            for hh in range(HEADS):
                hs = slice(hh * DH, (hh + 1) * DH)
                yh = y[:, hs]
                inv = lax.rsqrt(jnp.sum(yh * yh, axis=-1, keepdims=True) + EPS)
                out_ref[0, :, hs] = (yh * inv * scale).astype(out_ref.dtype)

    ga_ref[0] = _silu(proj(3)).astype(ga_ref.dtype)

    ps = proj(8, GATE_LANES)
    lane = lax.broadcasted_iota(jnp.int32, ps.shape, 1)
    beta = jax.nn.sigmoid(ps)
    z = ps + dt_ref[...]
    softplus = jnp.maximum(z, 0.0) + jnp.log1p(jnp.exp(-jnp.abs(z)))
    g_log = -jnp.exp(alog_ref[...]) * softplus
    tri = tri_ref[...]
    gc = _cumsum_rows(tri, jnp.where((lane >= HEADS) & (lane < 2 * HEADS), g_log, 0.0))
    sm_ref[0] = jnp.where(lane < HEADS, beta, gc)

    hl = lb_ref[...]
    e = jnp.exp(hl - jnp.max(hl, axis=0, keepdims=True))
    lb = jnp.sum(e[0:layer + 1], axis=0, keepdims=True) / jnp.sum(e, axis=0, keepdims=True)
    fr = proj(4)
    logf = jnp.log(lb + (1.0 - lb) * jax.nn.sigmoid(fr))
    b_ref[0] = _cumsum_rows(tri, logf)
    kb_ref[0] = ((1.0 - lb) * jax.nn.sigmoid(-fr)).astype(kb_ref.dtype)
    ib_ref[0] = proj(5).astype(ib_ref.dtype)
    qb_ref[0] = _silu(proj(6)).astype(qb_ref.dtype)
    gb_ref[0] = _silu(proj(7)).astype(gb_ref.dtype)


def _inproj(layer, x, n1, sc1, sh1, w_all, conv_w, alog_pad, dt_pad, hg_lb, tri, tt):
    B, T, D = x.shape
    const = lambda shape: pl.BlockSpec(shape, lambda b, t: (0,) * len(shape), pipeline_mode=pl.Buffered(1))
    act = lambda dt: jax.ShapeDtypeStruct((B, T, GW), dt)
    tile = lambda w: pl.BlockSpec((1, tt, w), lambda b, t: (b, t, 0))
    per_batch = pl.BlockSpec((1, 1, D), lambda b, t: (b, 0, 0))
    return pl.pallas_call(
        functools.partial(_inproj_kernel, layer),
        grid=(B, T // tt),
        in_specs=[tile(D), const((1, D)), per_batch, per_batch,
                  const(w_all.shape), const(conv_w.shape), const((1, GATE_LANES)), const((1, GATE_LANES)),
                  const(hg_lb.shape), const(tri.shape)],
        out_specs=[tile(GW), tile(GW), tile(GW), tile(GW), tile(GATE_LANES), tile(GW),
                   tile(GW), tile(GW), tile(GW), tile(GW)],
        out_shape=[act(ACT), act(ACT), act(ACT), act(ACT),
                   jax.ShapeDtypeStruct((B, T, GATE_LANES), F32), act(F32),
                   act(ACT), act(ACT), act(ACT), act(ACT)],
        scratch_shapes=[pltpu.VMEM((3, tt + 8, GW), F32)],
        compiler_params=pltpu.CompilerParams(dimension_semantics=("arbitrary", "arbitrary"),
                                             vmem_limit_bytes=VMEM_LIMIT_BYTES),
        name="inproj",
    )(x, n1, sc1, sh1, w_all, conv_w, alog_pad, dt_pad, hg_lb, tri)


def _gdn_prep_kernel(q_ref, k_ref, v_ref, sm_ref, gct_ref, o_ref, qt_ref, m_ref, n_ref):
    tt = q_ref.shape[1]
    nc = tt // CHUNK
    row = lax.broadcasted_iota(jnp.int32, (CHUNK, CHUNK), 0)
    col = lax.broadcasted_iota(jnp.int32, (CHUNK, CHUNK), 1)
    causal = row >= col
    diag_blk = (row > col) & (row // INV_BLOCK == col // INV_BLOCK)
    off_blk = row // INV_BLOCK > col // INV_BLOCK
    eye = jnp.where(row == col, 1.0, 0.0)
    lane = lax.broadcasted_iota(jnp.int32, (CHUNK, GATE_LANES), 1)
    assert INV_BLOCK == 16 and CHUNK == 4 * INV_BLOCK

    def body(i, carry):
        chains = [(GDN_CHUNKS_PER_ITER * i + j, hh) for j in range(GDN_CHUNKS_PER_ITER) for hh in range(HEADS)]
        rows = [pl.ds(pl.multiple_of(c * CHUNK, CHUNK), CHUNK) for c, _ in chains]
        hs = [slice(hh * DH, (hh + 1) * DH) for _, hh in chains]
        n = range(len(chains))
        sm = [sm_ref[0, rows[j], :] for j in n]
        q = [q_ref[0, rows[j], hs[j]].astype(F32) for j in n]
        k = [k_ref[0, rows[j], hs[j]].astype(F32) for j in n]
        v = [v_ref[0, rows[j], hs[j]].astype(F32) for j in n]
        beta = [_lane_pick(sm[j], lane, chains[j][1]) for j in n]
        gcol = [_lane_pick(sm[j], lane, HEADS + chains[j][1]) for j in n]
        grow = [gct_ref[0, hh, pl.ds(c, 1), :] for c, hh in chains]
        decay = [jnp.exp(jnp.where(causal, gcol[j] - grow[j], -jnp.inf)) for j in n]
        kb = [k[j] * beta[j] for j in n]
        L = [_dot_nt(kb[j], k[j]) * decay[j] for j in n]
        dg = [jnp.where(diag_blk, L[j], 0.0) for j in n]
        off = [jnp.where(off_blk, L[j], 0.0) for j in n]
        dinv = [eye - dg[j] for j in n]
        pw = [_dot3(dg[j], dg[j]) for j in n]
        for _ in range(2):
            dinv = [dinv[j] + _dot3(dinv[j], pw[j]) for j in n]
            pw = [_dot3(pw[j], pw[j]) for j in n]
        dinv = [dinv[j] + _dot3(dinv[j], pw[j]) for j in n]
        f1 = [_dot(dinv[j], off[j]) for j in n]
        f2 = [_dot(f1[j], f1[j]) for j in n]
        f3 = [_dot(f1[j], f2[j]) for j in n]
        tinv = [_dot(eye - f1[j] + f2[j] - f3[j], dinv[j]) for j in n]
        eg = [jnp.exp(gcol[j]) for j in n]
        sol = [_dot(tinv[j], jnp.concatenate([v[j] * beta[j], kb[j] * eg[j]], axis=1)) for j in n]
        attn = [_dot_nt(q[j], k[j]) * decay[j] for j in n]
        k_tail = [k[j] * jnp.exp(gcol[j][CHUNK - 1:CHUNK, :] - gcol[j]) for j in n]
        au = [_dot(attn[j], sol[j]) for j in n]
        ku = [_dot_tn(k_tail[j], sol[j]) for j in n]
        for j, (c, hh) in enumerate(chains):
            o_ref[0, rows[j], hs[j]] = au[j][:, :DH]
            qt_ref[0, rows[j], hs[j]] = (q[j] * eg[j] - au[j][:, DH:]).astype(qt_ref.dtype)
            n_ref[0, hh, c] = ku[j][:, :DH].astype(n_ref.dtype)
            m_ref[0, hh, c] = (-ku[j][:, DH:]).astype(m_ref.dtype)
        return carry

    lax.fori_loop(0, nc // GDN_CHUNKS_PER_ITER, body, 0)


def _gdn_scan_kernel(o_ref, qt_ref, m_ref, n_ref, gct_ref, sg_ref, nw_ref, out_ref, s_ref):
    nb, tt = o_ref.shape[0], o_ref.shape[1]
    nc = tt // CHUNK

    @pl.when(pl.program_id(0) == 0)
    def _():
        s_ref[...] = jnp.zeros(s_ref.shape, F32)

    nw = nw_ref[...]

    def body(c, carry):
        rows = pl.ds(pl.multiple_of(c * CHUNK, CHUNK), CHUNK)
        for b in range(nb):
            for hh in range(HEADS):
                hs = slice(hh * DH, (hh + 1) * DH)
                S = s_ref[b, hh]
                Sb = S.astype(BF16)
                glast = gct_ref[b, hh, pl.ds(c, 1), :][:, CHUNK - 1:CHUNK]
                o = o_ref[b, rows, hs] + jnp.dot(qt_ref[b, rows, hs], Sb, preferred_element_type=F32)
                s_ref[b, hh] = (S * jnp.exp(glast) + jnp.dot(m_ref[b, hh, c], Sb, preferred_element_type=F32)
                                + n_ref[b, hh, c].astype(F32))
                o = _rms(o) * nw * sg_ref[b, rows, hs].astype(F32)
                out_ref[b, rows, hs] = o.astype(out_ref.dtype)
        return carry

    lax.fori_loop(0, nc, body, 0)


def _gdn(q, k, v, sg, sm, gct, nw, tt):
    B, T, _ = q.shape
    nc = tt // CHUNK
    assert nc % GDN_CHUNKS_PER_ITER == 0
    n_chunks = T // CHUNK
    tile = lambda w: pl.BlockSpec((1, tt, w), lambda b, t: (b, t, 0))
    mat = jax.ShapeDtypeStruct((B, HEADS, n_chunks, DH, DH), ACT)
    o_part, qt, m, n = pl.pallas_call(
        _gdn_prep_kernel,
        grid=(B, T // tt),
        in_specs=[tile(GW), tile(GW), tile(GW), tile(GATE_LANES),
                  pl.BlockSpec((1, HEADS, nc, CHUNK), lambda b, t: (b, 0, t, 0))],
        out_specs=[tile(GW), tile(GW),
                   pl.BlockSpec((1, HEADS, nc, DH, DH), lambda b, t: (b, 0, t, 0, 0)),
                   pl.BlockSpec((1, HEADS, nc, DH, DH), lambda b, t: (b, 0, t, 0, 0))],
        out_shape=[jax.ShapeDtypeStruct((B, T, GW), F32), jax.ShapeDtypeStruct((B, T, GW), ACT), mat, mat],
        compiler_params=pltpu.CompilerParams(dimension_semantics=("arbitrary", "arbitrary"),
                                             vmem_limit_bytes=VMEM_LIMIT_BYTES),
        name="gdn_prep",
    )(q, k, v, sm, gct)

    full = lambda w: pl.BlockSpec((B, tt, w), lambda t: (0, t, 0))
    mats = pl.BlockSpec((B, HEADS, nc, DH, DH), lambda t: (0, 0, t, 0, 0))
    return pl.pallas_call(
        _gdn_scan_kernel,
        grid=(T // tt,),
        in_specs=[full(GW), full(GW), mats, mats,
                  pl.BlockSpec((B, HEADS, nc, CHUNK), lambda t: (0, 0, t, 0)),
                  full(GW), pl.BlockSpec((1, DH), lambda t: (0, 0))],
        out_specs=full(GW),
        out_shape=jax.ShapeDtypeStruct((B, T, GW), ACT),
        scratch_shapes=[pltpu.VMEM((B, HEADS, DH, DH), F32)],
        compiler_params=pltpu.CompilerParams(dimension_semantics=("arbitrary",),
                                             vmem_limit_bytes=VMEM_LIMIT_BYTES),
        name="gdn_scan",
    )(o_part, qt, m, n, gct, sg, nw)


def _hgrn_kernel(q_ref, k_ref, v_ref, b_ref, sg_ref, nw_ref, o_ref, st_ref):
    nb, tt = q_ref.shape[0], q_ref.shape[1]
    nc = tt // CHUNK

    @pl.when(pl.program_id(0) == 0)
    def _():
        st_ref[...] = jnp.zeros(st_ref.shape, F32)

    row = lax.broadcasted_iota(jnp.int32, (CHUNK, CHUNK), 0)
    col = lax.broadcasted_iota(jnp.int32, (CHUNK, CHUNK), 1)
    diag_block = ((col // SUB) == (row // SUB)) & (col <= row)
    chains = [(bi, hh) for bi in range(nb) for hh in range(HEADS)]
    n = range(len(chains))
    hs = [slice(hh * DH, (hh + 1) * DH) for _, hh in chains]

    def body(c, carry):
        rows = pl.ds(pl.multiple_of(c * CHUNK, CHUNK), CHUNK)
        q = [q_ref[bi, rows, hs[j]].astype(F32) for j, (bi, _) in enumerate(chains)]
        k = [k_ref[bi, rows, hs[j]].astype(F32) for j, (bi, _) in enumerate(chains)]
        v = [v_ref[bi, rows, hs[j]].astype(F32) for j, (bi, _) in enumerate(chains)]
        b = [b_ref[bi, rows, hs[j]] for j, (bi, _) in enumerate(chains)]
        blast = [b[j][CHUNK - 1:CHUNK, :] for j in n]
        st = [st_ref[bi, hh] for bi, hh in chains]
        o = [_dot_nt(q[j] * jnp.exp(b[j]), st[j]) for j in n]
        k_tail = [k[j] * jnp.exp(blast[j] - b[j]) for j in n]
        for j, (bi, hh) in enumerate(chains):
            st_ref[bi, hh] = st[j] * jnp.exp(blast[j]) + _dot_tn(v[j], k_tail[j])

        blocks = [[jnp.zeros((SUB, CHUNK), F32)] for _ in n]
        for i in range(1, CHUNK // SUB):
            lo, hi = i * SUB, (i + 1) * SUB
            for j in n:
                r = b[j][lo:lo + 1, :]
                qi = q[j][lo:hi] * jnp.exp(b[j][lo:hi] - r)
                kj = k[j][:lo] * jnp.exp(jnp.minimum(r - b[j][:lo], 0.0))
                kj = jnp.concatenate([kj, jnp.zeros((CHUNK - lo, DH), F32)], axis=0)
                blocks[j].append(_dot_nt(qi, kj))
        a = []
        for j in n:
            f = jnp.exp(jnp.minimum(b[j] - pltpu.roll(b[j], 1, 0), 0.0))
            e = None
            a_diag = jnp.zeros((CHUNK, CHUNK), F32)
            for delta in range(SUB):
                if delta == 0:
                    term = q[j] * k[j]
                else:
                    fsh = f if delta == 1 else pltpu.roll(f, delta - 1, 0)
                    e = fsh if e is None else e * fsh
                    term = q[j] * pltpu.roll(k[j], delta, 0) * e
                colv = jnp.sum(term, axis=1, keepdims=True)
                a_diag = jnp.where(row - col == delta, colv, a_diag)
            a.append(jnp.where(diag_block, a_diag, jnp.concatenate(blocks[j], axis=0)))

        o = [o[j] + _dot(a[j], v[j]) for j in n]
        for bi in range(nb):
            ob = jnp.concatenate(o[bi * HEADS:(bi + 1) * HEADS], axis=1)
            ob = _rms(ob) * nw_ref[...] * sg_ref[bi, rows, :].astype(F32)
            o_ref[bi, rows, :] = ob.astype(o_ref.dtype)
        return carry

    lax.fori_loop(0, nc, body, 0)


def _hgrn(q, k, v, b, sg, nw, tt):
    B, T, _ = q.shape
    tile = pl.BlockSpec((B, tt, GW), lambda t: (0, t, 0))
    return pl.pallas_call(
        _hgrn_kernel,
        grid=(T // tt,),
        in_specs=[tile, tile, tile, tile, tile, pl.BlockSpec((1, GW), lambda t: (0, 0))],
        out_specs=tile,
        out_shape=jax.ShapeDtypeStruct((B, T, GW), ACT),
        scratch_shapes=[pltpu.VMEM((B, HEADS, DH, DH), F32)],
        compiler_params=pltpu.CompilerParams(dimension_semantics=("arbitrary",),
                                             vmem_limit_bytes=VMEM_LIMIT_BYTES),
        name="hgrn",
    )(q, k, v, b, sg, nw)


def _outproj_kernel(oa_ref, ob_ref, x_ref, wa_ref, wb_ref, g1_ref, n2_ref, sc_ref, sh_ref, wr_ref, rb_ref,
                    x1_ref, h2_ref, eidx_ref, rank_ref, wts_ref, cnt_ref):
    tm = x_ref.shape[1]

    @pl.when((pl.program_id(0) == 0) & (pl.program_id(1) == 0))
    def _():
        cnt_ref[...] = jnp.zeros(cnt_ref.shape, F32)

    mix = (jnp.dot(oa_ref[0], wa_ref[...], preferred_element_type=F32)
           + jnp.dot(ob_ref[0], wb_ref[...], preferred_element_type=F32))
    x1 = x_ref[0] + g1_ref[0] * mix
    x1_ref[0] = x1
    h2 = _rms(x1) * n2_ref[...]
    h2 = h2 * (1.0 + sc_ref[0]) + sh_ref[0]
    h2_ref[...] = _to_row_tiles(h2).astype(h2_ref.dtype)

    scores = jax.nn.sigmoid(_dot3(wr_ref[...], h2, dot=_dot_nt))
    sel = scores + rb_ref[...]
    sub = lax.broadcasted_iota(jnp.int32, (GROUP_SIZE, tm), 0)
    neg = -jnp.inf
    groups = range(N_GROUPS)

    def take_max(blk):
        m = jnp.max(blk, axis=0, keepdims=True)
        first = jnp.min(jnp.where(blk == m, sub, GROUP_SIZE), axis=0, keepdims=True)
        hit = sub == first
        return m, hit, jnp.where(hit, neg, blk)

    blk_of = lambda a, g: a[g * GROUP_SIZE:(g + 1) * GROUP_SIZE]
    sel_blk = [blk_of(sel, g) for g in groups]
    group_score = jnp.zeros((N_GROUPS, tm), F32)
    for g in groups:
        m1, _, rest = take_max(sel_blk[g])
        m2 = jnp.max(rest, axis=0, keepdims=True)
        group_score = jnp.where(sub == g, m1 + m2, group_score)
    group_on = jnp.zeros((N_GROUPS, tm), F32)
    for _ in range(TOPK_GROUPS):
        _, hit, group_score = take_max(group_score)
        group_on = jnp.where(hit, 1.0, group_on)

    cand = [jnp.where(group_on[g:g + 1] > 0.0, sel_blk[g], neg) for g in groups]
    picked = [jnp.zeros((GROUP_SIZE, tm), F32) for _ in groups]
    chosen = []
    for _ in range(TOP_K):
        m = jnp.max(functools.reduce(jnp.maximum, cand), axis=0, keepdims=True)
        first = functools.reduce(jnp.minimum, [jnp.where(cand[g] == m, sub + g * GROUP_SIZE, N_EXPERTS)
                                               for g in groups])
        first = jnp.min(first, axis=0, keepdims=True)
        chosen.append(first)
        for g in groups:
            hit = (sub + g * GROUP_SIZE) == first
            picked[g] = jnp.where(hit, 1.0, picked[g])
            cand[g] = jnp.where(hit, neg, cand[g])

    picked_all = jnp.concatenate(picked, axis=0)
    r_i = lax.broadcasted_iota(jnp.int32, (tm, tm), 0)
    c_i = lax.broadcasted_iota(jnp.int32, (tm, tm), 1)
    earlier = jnp.where(r_i < c_i, 1.0, 0.0).astype(BF16)
    before = jnp.dot(picked_all.astype(BF16), earlier, preferred_element_type=F32) + cnt_ref[:, 0:1]
    cnt_ref[...] = cnt_ref[...] + jnp.sum(picked_all, axis=1, keepdims=True)

    def pick_value(table, first):
        parts = [jnp.where((sub + g * GROUP_SIZE) == first, blk_of(table, g), 0.0) for g in groups]
        return jnp.sum(functools.reduce(jnp.add, parts), axis=0, keepdims=True)

    w_k = [pick_value(scores, f) for f in chosen]
    denom = functools.reduce(jnp.add, w_k)
    eidx = jnp.zeros((TOP_K, tm), jnp.int32)
    rank = jnp.zeros((TOP_K, tm), jnp.int32)
    wts = jnp.zeros((TOP_K, tm), F32)
    for k in range(TOP_K):
        eidx = jnp.where(sub == k, chosen[k], eidx)
        rank = jnp.where(sub == k, pick_value(before, chosen[k]).astype(jnp.int32), rank)
        wts = jnp.where(sub == k, w_k[k] / denom * ROUTE_SCALE, wts)
    eidx_ref[...] = eidx
    rank_ref[...] = rank
    pad = jnp.zeros((GATE_LANES - TOP_K, tm), F32)
    wts_ref[0] = jnp.concatenate([wts, pad], axis=0).T


def _outproj(oa, ob, x, wa, wb, g1, n2, sc2, sh2, wr_t, rb, tm):
    B, T, D = x.shape
    nt = T // tm
    const = lambda shape: pl.BlockSpec(shape, lambda b, t: (0,) * len(shape))
    tile = lambda w: pl.BlockSpec((1, tm, w), lambda b, t: (b, t, 0))
    per_batch = pl.BlockSpec((1, 1, D), lambda b, t: (b, 0, 0))
    picks = pl.BlockSpec((TOP_K, tm), lambda b, t: (0, b * nt + t))
    return pl.pallas_call(
        _outproj_kernel,
        grid=(B, nt),
        in_specs=[tile(GW), tile(GW), tile(D), const(wa.shape), const(wb.shape), per_batch,
                  const((1, D)), per_batch, per_batch, const(wr_t.shape), const(rb.shape)],
        out_specs=[tile(D), pl.BlockSpec((tm, ROW_TILE, LANES), lambda b, t: (b * nt + t, 0, 0)),
                   picks, picks, tile(GATE_LANES), const((N_EXPERTS, GATE_LANES))],
        out_shape=[jax.ShapeDtypeStruct((B, T, D), F32), jax.ShapeDtypeStruct((B * T, ROW_TILE, LANES), ROW_DTYPE),
                   jax.ShapeDtypeStruct((TOP_K, B * T), jnp.int32), jax.ShapeDtypeStruct((TOP_K, B * T), jnp.int32),
                   jax.ShapeDtypeStruct((B, T, GATE_LANES), F32),
                   jax.ShapeDtypeStruct((N_EXPERTS, GATE_LANES), F32)],
        compiler_params=pltpu.CompilerParams(dimension_semantics=("arbitrary", "arbitrary"),
                                             vmem_limit_bytes=VMEM_LIMIT_BYTES),
        name="outproj",
    )(oa, ob, x, wa, wb, g1, n2, sc2, sh2, wr_t, rb)


def _slots_kernel(off_ref, eidx_ref, rank_ref, slot_ref):
    eidx = eidx_ref[...]

    def add_expert(e, acc):
        return acc + jnp.where(eidx == e, off_ref[e], 0)

    slot_ref[...] = lax.fori_loop(0, N_EXPERTS, add_expert, rank_ref[...])


def _slots(offsets, eidx, rank):
    k, m = eidx.shape
    tile = pl.BlockSpec((k, m), lambda i: (0, 0))
    return pl.pallas_call(
        _slots_kernel,
        grid=(1,),
        in_specs=[pl.BlockSpec(memory_space=pltpu.SMEM), tile, tile],
        out_specs=tile,
        out_shape=jax.ShapeDtypeStruct((k, m), jnp.int32),
        compiler_params=pltpu.CompilerParams(dimension_semantics=("arbitrary",),
                                             vmem_limit_bytes=VMEM_LIMIT_BYTES),
        name="slots",
    )(offsets, eidx, rank)


def _dispatch_kernel(slot_ref, h_ref, xs_ref, sem):
    tmd = h_ref.shape[0]

    def start_rows(j, c):
        for k in range(TOP_K):
            pltpu.make_async_copy(h_ref.at[pl.ds(j, 1)], xs_ref.at[pl.ds(slot_ref[k, j], 1)], sem).start(
                priority=k % 2)
        return c

    lax.fori_loop(0, tmd, start_rows, 0)
    for _ in range(TOP_K):
        pltpu.make_async_copy(h_ref, xs_ref.at[pl.ds(0, tmd)], sem).wait()


def _dispatch(slot, h2, n_rows, tmd):
    M = h2.shape[0]
    return pl.pallas_call(
        _dispatch_kernel,
        grid=(M // tmd,),
        in_specs=[pl.BlockSpec((TOP_K, tmd), lambda i: (0, i), memory_space=pltpu.SMEM),
                  pl.BlockSpec((tmd, ROW_TILE, LANES), lambda i: (i, 0, 0))],
        out_specs=pl.BlockSpec(memory_space=pl.ANY),
        out_shape=jax.ShapeDtypeStruct((n_rows, ROW_TILE, LANES), h2.dtype),
        scratch_shapes=[pltpu.SemaphoreType.DMA],
        compiler_params=pltpu.CompilerParams(dimension_semantics=("arbitrary",),
                                             vmem_limit_bytes=VMEM_LIMIT_BYTES),
        name="dispatch",
    )(slot, h2)


FFN_PIPELINE_STEPS = 2


def _ffn_kernel(te_ref, nu_ref, nv_ref, x_ref, wg_ref, wu_ref, wd_ref, y_ref, wgu_s, wd_s, xstd, ystd):
    i = pl.program_id(0)
    row = lax.broadcasted_iota(jnp.int32, (MOE_ROWS, 1), 0)
    tile_mm = jnp.maximum(i - 1, 0)
    tile_out = jnp.maximum(i - 2, 0)

    @pl.when(i == 0)
    def _():
        xstd[...] = jnp.zeros(xstd.shape, BF16)
        ystd[...] = jnp.zeros(ystd.shape, F32)

    @pl.when(i < nu_ref[0] + FFN_PIPELINE_STEPS)
    def _():
        @pl.when((i == 0) | (te_ref[tile_mm] != te_ref[tile_out]))
        def _():
            wgu_s[:, :D_EXPERT] = wg_ref[0].astype(BF16)
            wgu_s[:, D_EXPERT:] = wu_ref[0].astype(BF16)
            wd_s[...] = wd_ref[0].astype(BF16)

        cur = i % 2
        xstd[cur] = jnp.where(row < nv_ref[i], _to_rows(x_ref[...].astype(F32)), 0.0).astype(BF16)
        y_ref[...] = _to_row_tiles(ystd[cur]).astype(y_ref.dtype)
        gu = jnp.dot(xstd[1 - cur], wgu_s[...], preferred_element_type=F32)
        act = _silu(gu[:, :D_EXPERT]) * gu[:, D_EXPERT:]
        ystd[1 - cur] = jnp.dot(act.astype(BF16), wd_s[...], preferred_element_type=F32)


def _ffn(tile_expert, n_used, n_valid, xs, wg, wu, wd):
    D = wg.shape[1]
    n_tiles = xs.shape[0] // MOE_ROWS
    block = (MOE_ROWS, ROW_TILE, LANES)
    expert = lambda i, te, nu, nv: (te[jnp.maximum(i - 1, 0)], 0, 0)
    return pl.pallas_call(
        _ffn_kernel,
        grid_spec=pltpu.PrefetchScalarGridSpec(
            num_scalar_prefetch=3,
            grid=(n_tiles + FFN_PIPELINE_STEPS,),
            in_specs=[pl.BlockSpec(block, lambda i, te, nu, nv: (jnp.minimum(i, nu[0] - 1), 0, 0)),
                      pl.BlockSpec((1, D, D_EXPERT), expert),
                      pl.BlockSpec((1, D, D_EXPERT), expert),
                      pl.BlockSpec((1, D_EXPERT, D), expert)],
            out_specs=pl.BlockSpec(block, lambda i, te, nu, nv: (jnp.clip(i - 2, 0, nu[0] - 1), 0, 0)),
            scratch_shapes=[pltpu.VMEM((D, 2 * D_EXPERT), BF16), pltpu.VMEM((D_EXPERT, D), BF16),
                            pltpu.VMEM((2, MOE_ROWS, D), BF16), pltpu.VMEM((2, MOE_ROWS, D), F32)]),
        out_shape=jax.ShapeDtypeStruct(xs.shape, xs.dtype),
        compiler_params=pltpu.CompilerParams(dimension_semantics=("arbitrary",),
                                             vmem_limit_bytes=VMEM_LIMIT_BYTES),
        name="ffn",
    )(tile_expert, n_used, n_valid, xs, wg, wu, wd)


def _combine_kernel(slot_ref, next_slot_ref, w_ref, h_ref, x1_ref, g2_ref, fw_ref, wsgu_ref, wsd_ref, y_ref,
                    o_ref, buf, sem):
    tmc = h_ref.shape[0]
    i = pl.program_id(0)
    last = pl.num_programs(0) - 1

    def start_rows(slots, b):
        def body(j, c):
            for k in range(TOP_K):
                pltpu.make_async_copy(y_ref.at[pl.ds(slots[k, j], 1)], buf.at[b, k, pl.ds(j, 1)], sem.at[b]).start(
                    priority=k % 2)
            return c
        lax.fori_loop(0, tmc, body, 0)

    def wait_rows(b):
        for k in range(TOP_K):
            pltpu.make_async_copy(y_ref.at[pl.ds(0, tmc)], buf.at[b, k], sem.at[b]).wait()

    def for_buffer(step, fn):
        for b in range(2):
            @pl.when(step % 2 == b)
            def _():
                fn(b)

    @pl.when(i == 0)
    def _():
        start_rows(slot_ref, 0)

    @pl.when(i < last)
    def _():
        for_buffer(i + 1, lambda b: start_rows(next_slot_ref, b))

    gu = jnp.dot(_to_rows(h_ref[...].astype(F32)).astype(BF16), wsgu_ref[...], preferred_element_type=F32)
    act = _silu(gu[:, :D_EXPERT]) * gu[:, D_EXPERT:]
    shared = jnp.dot(act.astype(BF16), wsd_ref[...], preferred_element_type=F32)
    w = w_ref[...]
    lane = lax.broadcasted_iota(jnp.int32, w.shape, 1)

    def finish(b):
        wait_rows(b)
        routed = jnp.zeros((tmc, ROW_TILE, LANES), F32)
        for k in range(TOP_K):
            wk = jnp.broadcast_to(_lane_pick(w, lane, k), (tmc, LANES))
            routed = routed + _to_row_tiles(jnp.concatenate([wk] * ROW_TILE, axis=1)) * buf[b, k].astype(F32)
        acc = shared + _to_rows(routed)
        y = x1_ref[...] + g2_ref[0] * acc
        o_ref[...] = _rms(y) * fw_ref[...]

    for_buffer(i, finish)


def _combine(slot, wts, h2, x1, g2, fw, wsgu, wsd, ys, T, tmc):
    M, D = x1.shape
    const = lambda shape: pl.BlockSpec(shape, lambda i: (0,) * len(shape))
    tile = lambda w: pl.BlockSpec((tmc, w), lambda i: (i, 0))
    row_tiled = pl.BlockSpec((tmc, ROW_TILE, LANES), lambda i: (i, 0, 0))
    n_steps = M // tmc
    return pl.pallas_call(
        _combine_kernel,
        grid=(n_steps,),
        in_specs=[pl.BlockSpec((TOP_K, tmc), lambda i: (0, i), memory_space=pltpu.SMEM),
                  pl.BlockSpec((TOP_K, tmc), lambda i: (0, jnp.minimum(i + 1, n_steps - 1)), memory_space=pltpu.SMEM),
                  tile(GATE_LANES), row_tiled, tile(D),
                  pl.BlockSpec((1, 1, D), lambda i: (i // (T // tmc), 0, 0)),
                  const((1, D)), const(wsgu.shape), const(wsd.shape),
                  pl.BlockSpec(memory_space=pl.ANY)],
        out_specs=tile(D),
        out_shape=jax.ShapeDtypeStruct((M, D), F32),
        scratch_shapes=[pltpu.VMEM((2, TOP_K, tmc, ROW_TILE, LANES), ys.dtype), pltpu.SemaphoreType.DMA((2,))],
        compiler_params=pltpu.CompilerParams(dimension_semantics=("arbitrary",),
                                             vmem_limit_bytes=VMEM_LIMIT_BYTES),
        name="combine",
    )(slot, slot, wts, h2, x1, g2, fw, wsgu, wsd, ys)


def _pick_tile(n, want):
    t = min(n, want)
    assert n % t == 0 and t % CHUNK == 0, (n, want)
    return t


def kernel(x, c, w_ada, b_ada, norm1_w, w_in, conv_w, gdn_a_log, gdn_dt_bias, gdn_norm_w, hg_lb, hg_norm_w,
           w_out, norm2_w, w_router, router_bias, w_gate, w_up, w_down, ws_gate, ws_up, ws_down, final_norm_w):
    B, T, D = x.shape
    M = B * T
    depth = w_ada.shape[0]
    assert depth == 1 and T % CHUNK == 0 and B <= 8
    layer = 0
    tt = _pick_tile(T, 512)

    c_pad = jnp.pad(c, ((0, 8 - B), (0, 0)))
    mod = _ada(c_pad, w_ada[layer], b_ada[layer].reshape(1, -1))[:B]
    sh1, sc1, g1, sh2, sc2, g2 = (m.reshape(B, 1, D) for m in jnp.split(mod, 6, axis=-1))

    w = w_in[layer]
    qkv_w = 3 * GW
    sizes = (GW, HEADS, HEADS, GW, GW, GW, GW)
    offs = [qkv_w]
    for s in sizes:
        offs.append(offs[-1] + s)
    seg = lambda i: w[:, offs[i]:offs[i + 1]]
    small = jnp.pad(jnp.concatenate([seg(1), seg(2)], axis=1), ((0, 0), (0, GATE_LANES - 2 * HEADS)))
    w_all = jnp.concatenate([w[:, :qkv_w], seg(0), seg(3), seg(4), seg(5), seg(6), small], axis=1).astype(BF16)
    lane_pad = lambda v: jnp.pad(v.astype(F32).reshape(1, HEADS), ((0, 0), (HEADS, GATE_LANES - 2 * HEADS)))
    idx = jnp.arange(CUMSUM_ROWS)
    tri = ((idx[:, None] >= idx[None, :]) & (idx[:, None] // CHUNK == idx[None, :] // CHUNK)).astype(BF16)

    qa, ka, va, ga, sm, bcum, kb, ib, qb, gb = _inproj(
        layer, x, norm1_w[layer].reshape(1, D), sc1, sh1, w_all, conv_w[layer].astype(F32),
        lane_pad(gdn_a_log[layer]), lane_pad(gdn_dt_bias[layer]), hg_lb.astype(F32), tri, _pick_tile(T, 1024))

    gct = sm[:, :, HEADS:2 * HEADS].transpose(0, 2, 1).reshape(B, HEADS, T // CHUNK, CHUNK)
    oa = _gdn(qa, ka, va, ga, sm, gct, gdn_norm_w[layer].reshape(1, DH), tt)
    ob = _hgrn(qb, kb, ib, bcum, gb, hg_norm_w[layer].reshape(1, GW), tt)

    wo = w_out[layer].astype(BF16)
    x1, h2, eidx, rank, wts, cnt = _outproj(oa, ob, x, wo[:GW], wo[GW:], g1, norm2_w[layer].reshape(1, D), sc2, sh2,
                                            w_router[layer].T, router_bias[layer].reshape(N_EXPERTS, 1), tt)

    counts = cnt[:, 0].astype(jnp.int32)
    padded = (counts + MOE_ROWS - 1) // MOE_ROWS * MOE_ROWS
    ends = jnp.cumsum(padded)
    offsets = ends - padded
    n_tiles = (M * TOP_K) // MOE_ROWS + N_EXPERTS
    n_used = (ends[-1] // MOE_ROWS).astype(jnp.int32)
    tile_ids = jnp.minimum(jnp.arange(n_tiles + FFN_PIPELINE_STEPS, dtype=jnp.int32), n_used - 1)
    tile_expert = jnp.sum(((ends // MOE_ROWS)[None, :] <= tile_ids[:, None]).astype(jnp.int32), axis=1)
    tile_expert = jnp.minimum(tile_expert, N_EXPERTS - 1)
    slot = _slots(offsets.astype(jnp.int32), eidx, rank)
    first_tile = jnp.take(offsets, tile_expert) // MOE_ROWS
    n_valid = jnp.clip(jnp.take(counts, tile_expert) - (tile_ids - first_tile) * MOE_ROWS, 0, MOE_ROWS)

    assert D == ROW_TILE * LANES
    xs = _dispatch(slot, h2, n_tiles * MOE_ROWS, _pick_tile(M, 1024))
    ys = _ffn(tile_expert.astype(jnp.int32), n_used.reshape(1), n_valid.astype(jnp.int32), xs,
              w_gate[layer], w_up[layer], w_down[layer])
    wsgu = jnp.concatenate([ws_gate[layer], ws_up[layer]], axis=-1).astype(BF16)
    out = _combine(slot, wts.reshape(M, GATE_LANES), h2, x1.reshape(M, D), g2, final_norm_w.reshape(1, D),
                   wsgu, ws_down[layer].astype(BF16), ys, T, _pick_tile(T, 256))
    return out.reshape(B, T, D)
```

```python
import functools

import jax
import jax.numpy as jnp
from jax import lax
from jax.experimental import pallas as pl
from jax.experimental.pallas import tpu as pltpu

F32 = jnp.float32
BF16 = jnp.bfloat16

EPS = 1e-6
CHUNK = 64
SUB = 8
HEADS = 4
DH = 128
GW = HEADS * DH
CONV_K = 4
N_EXPERTS = 64
N_GROUPS = 8
GROUP_SIZE = N_EXPERTS // N_GROUPS
TOPK_GROUPS = 4
TOP_K = 8
D_EXPERT = 256
ROUTE_SCALE = 2.5
GATE_LANES = 128
GDN_CHUNKS_PER_ITER = 8
INV_BLOCK = 16
CUMSUM_ROWS = 128
MOE_ROWS = 1024
ROW_TILE, LANES = 8, 128

VMEM_LIMIT_BYTES = 56 * 1024 * 1024

ACT = BF16
ROW_DTYPE = F32


def _silu(x):
    return x * jax.nn.sigmoid(x)


def _dot(a, b):
    return jnp.dot(a.astype(BF16), b.astype(BF16), preferred_element_type=F32)


def _dot_nt(a, b):
    return lax.dot_general(a.astype(BF16), b.astype(BF16), (((1,), (1,)), ((), ())),
                           preferred_element_type=F32)


def _dot_tn(a, b):
    return lax.dot_general(a.astype(BF16), b.astype(BF16), (((0,), (0,)), ((), ())),
                           preferred_element_type=F32)


def _split2(x):
    hi = x.astype(BF16)
    lo = (x - hi.astype(F32)).astype(BF16)
    return hi, lo


def _dot3(a, b, dot=_dot):
    ah, al = _split2(a)
    bh, bl = _split2(b)
    return dot(ah, bh) + dot(ah, bl) + dot(al, bh)


def _cumsum_rows(tri, x):
    hi = x.astype(BF16)
    r = x - hi.astype(F32)
    mid = r.astype(BF16)
    lo = (r - mid.astype(F32)).astype(BF16)
    g = tri.shape[0]
    groups = []
    for r0 in range(0, x.shape[0], g):
        rows = slice(r0, r0 + g)
        groups.append(jnp.dot(tri, hi[rows], preferred_element_type=F32)
                      + jnp.dot(tri, mid[rows], preferred_element_type=F32)
                      + jnp.dot(tri, lo[rows], preferred_element_type=F32))
    return jnp.concatenate(groups, axis=0)


def _lane_pick(tile, lane, idx):
    return jnp.sum(jnp.where(lane == idx, tile, 0.0), axis=1, keepdims=True)


def _rms(x):
    return x * lax.rsqrt(jnp.mean(x * x, axis=-1, keepdims=True) + EPS)


def _to_rows(x3):
    r = x3.shape[0]
    xt = jnp.swapaxes(x3.reshape(r // ROW_TILE, ROW_TILE, ROW_TILE, LANES), 1, 2)
    return jnp.concatenate([xt[:, s].reshape(r, LANES) for s in range(ROW_TILE)], axis=1)


def _to_row_tiles(x):
    r = x.shape[0]
    xt = jnp.stack([x[:, s * LANES:(s + 1) * LANES].reshape(r // ROW_TILE, ROW_TILE, LANES) for s in range(ROW_TILE)],
                   axis=1)
    return jnp.swapaxes(xt, 1, 2).reshape(r, ROW_TILE, LANES)


def _ada_kernel(c_ref, w_ref, b_ref, o_ref):
    ca = _silu(c_ref[...])
    o_ref[...] = _dot3(ca, w_ref[...]) + b_ref[...]


def _ada(c_pad, w, b):
    rows, d = c_pad.shape
    n = w.shape[1]
    tn = 1024
    return pl.pallas_call(
        _ada_kernel,
        grid=(n // tn,),
        in_specs=[pl.BlockSpec((rows, d), lambda j: (0, 0)),
                  pl.BlockSpec((d, tn), lambda j: (0, j)),
                  pl.BlockSpec((1, tn), lambda j: (0, j))],
        out_specs=pl.BlockSpec((rows, tn), lambda j: (0, j)),
        out_shape=jax.ShapeDtypeStruct((rows, n), F32),
        compiler_params=pltpu.CompilerParams(dimension_semantics=("arbitrary",),
                                             vmem_limit_bytes=VMEM_LIMIT_BYTES),
        name="ada",
    )(c_pad, w, b)


def _inproj_kernel(layer, x_ref, n1_ref, sc_ref, sh_ref, w_ref, cw_ref, alog_ref, dt_ref, lb_ref, tri_ref,
                   qa_ref, ka_ref, va_ref, ga_ref, sm_ref, b_ref, kb_ref, ib_ref, qb_ref, gb_ref,
                   pbuf):
    tt = x_ref.shape[1]
    t = pl.program_id(1)

    h = _rms(x_ref[0]) * n1_ref[...]
    h = h * (1.0 + sc_ref[0]) + sh_ref[0]
    hb = h.astype(BF16)

    def proj(g, width=GW):
        return jnp.dot(hb, w_ref[:, g * GW:g * GW + width], preferred_element_type=F32)

    @pl.when(t == 0)
    def _():
        pbuf[:, 0:8, :] = jnp.zeros((3, 8, GW), F32)

    for g, out_ref in enumerate((qa_ref, ka_ref, va_ref)):
        cols = slice(g * GW, (g + 1) * GW)
        p = proj(g)
        pbuf[g, 8:8 + tt, :] = p
        y = p * cw_ref[CONV_K - 1:CONV_K, cols]
        for j in range(1, CONV_K):
            y = y + pbuf[g, 8 - j:8 - j + tt, :] * cw_ref[CONV_K - 1 - j:CONV_K - j, cols]
        pbuf[g, 0:8, :] = pbuf[g, tt:tt + 8, :]
        y = _silu(y)
        if g == 2:
            out_ref[0] = y.astype(out_ref.dtype)
        else:
            scale = DH ** -0.5 if g == 0 else 1.0
            for hh in range(HEADS):
                hs = slice(hh * DH, (hh + 1) * DH)
                yh = y[:, hs]
                inv = lax.rsqrt(jnp.sum(yh * yh, axis=-1, keepdims=True) + EPS)
                out_ref[0, :, hs] = (yh * inv * scale).astype(out_ref.dtype)

    ga_ref[0] = _silu(proj(3)).astype(ga_ref.dtype)

    ps = proj(8, GATE_LANES)
    lane = lax.broadcasted_iota(jnp.int32, ps.shape, 1)
    beta = jax.nn.sigmoid(ps)
    z = ps + dt_ref[...]
    softplus = jnp.maximum(z, 0.0) + jnp.log1p(jnp.exp(-jnp.abs(z)))
    g_log = -jnp.exp(alog_ref[...]) * softplus
    tri = tri_ref[...]
    gc = _cumsum_rows(tri, jnp.where((lane >= HEADS) & (lane < 2 * HEADS), g_log, 0.0))
    sm_ref[0] = jnp.where(lane < HEADS, beta, gc)

    hl = lb_ref[...]
    e = jnp.exp(hl - jnp.max(hl, axis=0, keepdims=True))
    lb = jnp.sum(e[0:layer + 1], axis=0, keepdims=True) / jnp.sum(e, axis=0, keepdims=True)
    fr = proj(4)
    logf = jnp.log(lb + (1.0 - lb) * jax.nn.sigmoid(fr))
    b_ref[0] = _cumsum_rows(tri, logf)
    kb_ref[0] = ((1.0 - lb) * jax.nn.sigmoid(-fr)).astype(kb_ref.dtype)
    ib_ref[0] = proj(5).astype(ib_ref.dtype)
    qb_ref[0] = _silu(proj(6)).astype(qb_ref.dtype)
    gb_ref[0] = _silu(proj(7)).astype(gb_ref.dtype)


def _inproj(layer, x, n1, sc1, sh1, w_all, conv_w, alog_pad, dt_pad, hg_lb, tri, tt):
    B, T, D = x.shape
    const = lambda shape: pl.BlockSpec(shape, lambda b, t: (0,) * len(shape), pipeline_mode=pl.Buffered(1))
    act = lambda dt: jax.ShapeDtypeStruct((B, T, GW), dt)
    tile = lambda w: pl.BlockSpec((1, tt, w), lambda b, t: (b, t, 0))
    per_batch = pl.BlockSpec((1, 1, D), lambda b, t: (b, 0, 0))
    return pl.pallas_call(
        functools.partial(_inproj_kernel, layer),
        grid=(B, T // tt),
        in_specs=[tile(D), const((1, D)), per_batch, per_batch,
                  const(w_all.shape), const(conv_w.shape), const((1, GATE_LANES)), const((1, GATE_LANES)),
                  const(hg_lb.shape), const(tri.shape)],
        out_specs=[tile(GW), tile(GW), tile(GW), tile(GW), tile(GATE_LANES), tile(GW),
                   tile(GW), tile(GW), tile(GW), tile(GW)],
        out_shape=[act(ACT), act(ACT), act(ACT), act(ACT),
                   jax.ShapeDtypeStruct((B, T, GATE_LANES), F32), act(F32),
                   act(ACT), act(ACT), act(ACT), act(ACT)],
        scratch_shapes=[pltpu.VMEM((3, tt + 8, GW), F32)],
        compiler_params=pltpu.CompilerParams(dimension_semantics=("arbitrary", "arbitrary"),
                                             vmem_limit_bytes=VMEM_LIMIT_BYTES),
        name="inproj",
    )(x, n1, sc1, sh1, w_all, conv_w, alog_pad, dt_pad, hg_lb, tri)


def _gdn_prep_kernel(q_ref, k_ref, v_ref, sm_ref, gct_ref, o_ref, qt_ref, m_ref, n_ref):
    tt = q_ref.shape[1]
    nc = tt // CHUNK
    row = lax.broadcasted_iota(jnp.int32, (CHUNK, CHUNK), 0)
    col = lax.broadcasted_iota(jnp.int32, (CHUNK, CHUNK), 1)
    causal = row >= col
    diag_blk = (row > col) & (row // INV_BLOCK == col // INV_BLOCK)
    off_blk = row // INV_BLOCK > col // INV_BLOCK
    eye = jnp.where(row == col, 1.0, 0.0)
    lane = lax.broadcasted_iota(jnp.int32, (CHUNK, GATE_LANES), 1)
    assert INV_BLOCK == 16 and CHUNK == 4 * INV_BLOCK

    def body(i, carry):
        chains = [(GDN_CHUNKS_PER_ITER * i + j, hh) for j in range(GDN_CHUNKS_PER_ITER) for hh in range(HEADS)]
        rows = [pl.ds(pl.multiple_of(c * CHUNK, CHUNK), CHUNK) for c, _ in chains]
        hs = [slice(hh * DH, (hh + 1) * DH) for _, hh in chains]
        n = range(len(chains))
        sm = [sm_ref[0, rows[j], :] for j in n]
        q = [q_ref[0, rows[j], hs[j]].astype(F32) for j in n]
        k = [k_ref[0, rows[j], hs[j]].astype(F32) for j in n]
        v = [v_ref[0, rows[j], hs[j]].astype(F32) for j in n]
        beta = [_lane_pick(sm[j], lane, chains[j][1]) for j in n]
        gcol = [_lane_pick(sm[j], lane, HEADS + chains[j][1]) for j in n]
        grow = [gct_ref[0, hh, pl.ds(c, 1), :] for c, hh in chains]
        decay = [jnp.exp(jnp.where(causal, gcol[j] - grow[j], -jnp.inf)) for j in n]
        kb = [k[j] * beta[j] for j in n]
        L = [_dot_nt(kb[j], k[j]) * decay[j] for j in n]
        dg = [jnp.where(diag_blk, L[j], 0.0) for j in n]
        off = [jnp.where(off_blk, L[j], 0.0) for j in n]
        dinv = [eye - dg[j] for j in n]
        pw = [_dot3(dg[j], dg[j]) for j in n]
        for _ in range(2):
            dinv = [dinv[j] + _dot3(dinv[j], pw[j]) for j in n]
            pw = [_dot3(pw[j], pw[j]) for j in n]
        dinv = [dinv[j] + _dot3(dinv[j], pw[j]) for j in n]
        f1 = [_dot(dinv[j], off[j]) for j in n]
        f2 = [_dot(f1[j], f1[j]) for j in n]
        f3 = [_dot(f1[j], f2[j]) for j in n]
        tinv = [_dot(eye - f1[j] + f2[j] - f3[j], dinv[j]) for j in n]
        eg = [jnp.exp(gcol[j]) for j in n]
        sol = [_dot(tinv[j], jnp.concatenate([v[j] * beta[j], kb[j] * eg[j]], axis=1)) for j in n]
        attn = [_dot_nt(q[j], k[j]) * decay[j] for j in n]
        k_tail = [k[j] * jnp.exp(gcol[j][CHUNK - 1:CHUNK, :] - gcol[j]) for j in n]
        au = [_dot(attn[j], sol[j]) for j in n]
        ku = [_dot_tn(k_tail[j], sol[j]) for j in n]
        for j, (c, hh) in enumerate(chains):
            o_ref[0, rows[j], hs[j]] = au[j][:, :DH]
            qt_ref[0, rows[j], hs[j]] = (q[j] * eg[j] - au[j][:, DH:]).astype(qt_ref.dtype)
            n_ref[0, hh, c] = ku[j][:, :DH].astype(n_ref.dtype)
            m_ref[0, hh, c] = (-ku[j][:, DH:]).astype(m_ref.dtype)
        return carry

    lax.fori_loop(0, nc // GDN_CHUNKS_PER_ITER, body, 0)


def _gdn_scan_kernel(o_ref, qt_ref, m_ref, n_ref, gct_ref, sg_ref, nw_ref, out_ref, s_ref):
    nb, tt = o_ref.shape[0], o_ref.shape[1]
    nc = tt // CHUNK

    @pl.when(pl.program_id(0) == 0)
    def _():
        s_ref[...] = jnp.zeros(s_ref.shape, F32)

    nw = nw_ref[...]

    def body(c, carry):
        rows = pl.ds(pl.multiple_of(c * CHUNK, CHUNK), CHUNK)
        for b in range(nb):
            for hh in range(HEADS):
                hs = slice(hh * DH, (hh + 1) * DH)
                S = s_ref[b, hh]
                Sb = S.astype(BF16)
                glast = gct_ref[b, hh, pl.ds(c, 1), :][:, CHUNK - 1:CHUNK]
                o = o_ref[b, rows, hs] + jnp.dot(qt_ref[b, rows, hs], Sb, preferred_element_type=F32)
                s_ref[b, hh] = (S * jnp.exp(glast) + jnp.dot(m_ref[b, hh, c], Sb, preferred_element_type=F32)
                                + n_ref[b, hh, c].astype(F32))
                o = _rms(o) * nw * sg_ref[b, rows, hs].astype(F32)
                out_ref[b, rows, hs] = o.astype(out_ref.dtype)
        return carry

    lax.fori_loop(0, nc, body, 0)


def _gdn(q, k, v, sg, sm, gct, nw, tt):
    B, T, _ = q.shape
    nc = tt // CHUNK
    assert nc % GDN_CHUNKS_PER_ITER == 0
    n_chunks = T // CHUNK
    tile = lambda w: pl.BlockSpec((1, tt, w), lambda b, t: (b, t, 0))
    mat = jax.ShapeDtypeStruct((B, HEADS, n_chunks, DH, DH), ACT)
    o_part, qt, m, n = pl.pallas_call(
        _gdn_prep_kernel,
        grid=(B, T // tt),
        in_specs=[tile(GW), tile(GW), tile(GW), tile(GATE_LANES),
                  pl.BlockSpec((1, HEADS, nc, CHUNK), lambda b, t: (b, 0, t, 0))],
        out_specs=[tile(GW), tile(GW),
                   pl.BlockSpec((1, HEADS, nc, DH, DH), lambda b, t: (b, 0, t, 0, 0)),
                   pl.BlockSpec((1, HEADS, nc, DH, DH), lambda b, t: (b, 0, t, 0, 0))],
        out_shape=[jax.ShapeDtypeStruct((B, T, GW), F32), jax.ShapeDtypeStruct((B, T, GW), ACT), mat, mat],
        compiler_params=pltpu.CompilerParams(dimension_semantics=("arbitrary", "arbitrary"),
                                             vmem_limit_bytes=VMEM_LIMIT_BYTES),
        name="gdn_prep",
    )(q, k, v, sm, gct)

    full = lambda w: pl.BlockSpec((B, tt, w), lambda t: (0, t, 0))
    mats = pl.BlockSpec((B, HEADS, nc, DH, DH), lambda t: (0, 0, t, 0, 0))
    return pl.pallas_call(
        _gdn_scan_kernel,
        grid=(T // tt,),
        in_specs=[full(GW), full(GW), mats, mats,
                  pl.BlockSpec((B, HEADS, nc, CHUNK), lambda t: (0, 0, t, 0)),
                  full(GW), pl.BlockSpec((1, DH), lambda t: (0, 0))],
        out_specs=full(GW),
        out_shape=jax.ShapeDtypeStruct((B, T, GW), ACT),
        scratch_shapes=[pltpu.VMEM((B, HEADS, DH, DH), F32)],
        compiler_params=pltpu.CompilerParams(dimension_semantics=("arbitrary",),
                                             vmem_limit_bytes=VMEM_LIMIT_BYTES),
        name="gdn_scan",
    )(o_part, qt, m, n, gct, sg, nw)


def _hgrn_kernel(q_ref, k_ref, v_ref, b_ref, sg_ref, nw_ref, o_ref, st_ref):
    nb, tt = q_ref.shape[0], q_ref.shape[1]
    nc = tt // CHUNK

    @pl.when(pl.program_id(0) == 0)
    def _():
        st_ref[...] = jnp.zeros(st_ref.shape, F32)

    row = lax.broadcasted_iota(jnp.int32, (CHUNK, CHUNK), 0)
    col = lax.broadcasted_iota(jnp.int32, (CHUNK, CHUNK), 1)
    diag_block = ((col // SUB) == (row // SUB)) & (col <= row)
    chains = [(bi, hh) for bi in range(nb) for hh in range(HEADS)]
    n = range(len(chains))
    hs = [slice(hh * DH, (hh + 1) * DH) for _, hh in chains]

    def body(c, carry):
        rows = pl.ds(pl.multiple_of(c * CHUNK, CHUNK), CHUNK)
        q = [q_ref[bi, rows, hs[j]].astype(F32) for j, (bi, _) in enumerate(chains)]
        k = [k_ref[bi, rows, hs[j]].astype(F32) for j, (bi, _) in enumerate(chains)]
        v = [v_ref[bi, rows, hs[j]].astype(F32) for j, (bi, _) in enumerate(chains)]
        b = [b_ref[bi, rows, hs[j]] for j, (bi, _) in enumerate(chains)]
        blast = [b[j][CHUNK - 1:CHUNK, :] for j in n]
        st = [st_ref[bi, hh] for bi, hh in chains]
        o = [_dot_nt(q[j] * jnp.exp(b[j]), st[j]) for j in n]
        k_tail = [k[j] * jnp.exp(blast[j] - b[j]) for j in n]
        for j, (bi, hh) in enumerate(chains):
            st_ref[bi, hh] = st[j] * jnp.exp(blast[j]) + _dot_tn(v[j], k_tail[j])

        blocks = [[jnp.zeros((SUB, CHUNK), F32)] for _ in n]
        for i in range(1, CHUNK // SUB):
            lo, hi = i * SUB, (i + 1) * SUB
            for j in n:
                r = b[j][lo:lo + 1, :]
                qi = q[j][lo:hi] * jnp.exp(b[j][lo:hi] - r)
                kj = k[j][:lo] * jnp.exp(jnp.minimum(r - b[j][:lo], 0.0))
                kj = jnp.concatenate([kj, jnp.zeros((CHUNK - lo, DH), F32)], axis=0)
                blocks[j].append(_dot_nt(qi, kj))
        a = []
        for j in n:
            f = jnp.exp(jnp.minimum(b[j] - pltpu.roll(b[j], 1, 0), 0.0))
            e = None
            a_diag = jnp.zeros((CHUNK, CHUNK), F32)
            for delta in range(SUB):
                if delta == 0:
                    term = q[j] * k[j]
                else:
                    fsh = f if delta == 1 else pltpu.roll(f, delta - 1, 0)
                    e = fsh if e is None else e * fsh
                    term = q[j] * pltpu.roll(k[j], delta, 0) * e
                colv = jnp.sum(term, axis=1, keepdims=True)
                a_diag = jnp.where(row - col == delta, colv, a_diag)
            a.append(jnp.where(diag_block, a_diag, jnp.concatenate(blocks[j], axis=0)))

        o = [o[j] + _dot(a[j], v[j]) for j in n]
        for bi in range(nb):
            ob = jnp.concatenate(o[bi * HEADS:(bi + 1) * HEADS], axis=1)
            ob = _rms(ob) * nw_ref[...] * sg_ref[bi, rows, :].astype(F32)
            o_ref[bi, rows, :] = ob.astype(o_ref.dtype)
        return carry

    lax.fori_loop(0, nc, body, 0)


def _hgrn(q, k, v, b, sg, nw, tt):
    B, T, _ = q.shape
    tile = pl.BlockSpec((B, tt, GW), lambda t: (0, t, 0))
    return pl.pallas_call(
        _hgrn_kernel,
        grid=(T // tt,),
        in_specs=[tile, tile, tile, tile, tile, pl.BlockSpec((1, GW), lambda t: (0, 0))],
        out_specs=tile,
        out_shape=jax.ShapeDtypeStruct((B, T, GW), ACT),
        scratch_shapes=[pltpu.VMEM((B, HEADS, DH, DH), F32)],
        compiler_params=pltpu.CompilerParams(dimension_semantics=("arbitrary",),
                                             vmem_limit_bytes=VMEM_LIMIT_BYTES),
        name="hgrn",
    )(q, k, v, b, sg, nw)


def _outproj_kernel(oa_ref, ob_ref, x_ref, wa_ref, wb_ref, g1_ref, n2_ref, sc_ref, sh_ref, wr_ref, rb_ref,
                    x1_ref, h2_ref, eidx_ref, rank_ref, wts_ref, cnt_ref):
    tm = x_ref.shape[1]

    @pl.when((pl.program_id(0) == 0) & (pl.program_id(1) == 0))
    def _():
        cnt_ref[...] = jnp.zeros(cnt_ref.shape, F32)

    mix = (jnp.dot(oa_ref[0], wa_ref[...], preferred_element_type=F32)
           + jnp.dot(ob_ref[0], wb_ref[...], preferred_element_type=F32))
    x1 = x_ref[0] + g1_ref[0] * mix
    x1_ref[0] = x1
    h2 = _rms(x1) * n2_ref[...]
    h2 = h2 * (1.0 + sc_ref[0]) + sh_ref[0]
    h2_ref[...] = _to_row_tiles(h2).astype(h2_ref.dtype)

    scores = jax.nn.sigmoid(_dot3(wr_ref[...], h2, dot=_dot_nt))
    sel = scores + rb_ref[...]
    sub = lax.broadcasted_iota(jnp.int32, (GROUP_SIZE, tm), 0)
    neg = -jnp.inf
    groups = range(N_GROUPS)

    def take_max(blk):
        m = jnp.max(blk, axis=0, keepdims=True)
        first = jnp.min(jnp.where(blk == m, sub, GROUP_SIZE), axis=0, keepdims=True)
        hit = sub == first
        return m, hit, jnp.where(hit, neg, blk)

    blk_of = lambda a, g: a[g * GROUP_SIZE:(g + 1) * GROUP_SIZE]
    sel_blk = [blk_of(sel, g) for g in groups]
    group_score = jnp.zeros((N_GROUPS, tm), F32)
    for g in groups:
        m1, _, rest = take_max(sel_blk[g])
        m2 = jnp.max(rest, axis=0, keepdims=True)
        group_score = jnp.where(sub == g, m1 + m2, group_score)
    group_on = jnp.zeros((N_GROUPS, tm), F32)
    for _ in range(TOPK_GROUPS):
        _, hit, group_score = take_max(group_score)
        group_on = jnp.where(hit, 1.0, group_on)

    cand = [jnp.where(group_on[g:g + 1] > 0.0, sel_blk[g], neg) for g in groups]
    picked = [jnp.zeros((GROUP_SIZE, tm), F32) for _ in groups]
    chosen = []
    for _ in range(TOP_K):
        m = jnp.max(functools.reduce(jnp.maximum, cand), axis=0, keepdims=True)
        first = functools.reduce(jnp.minimum, [jnp.where(cand[g] == m, sub + g * GROUP_SIZE, N_EXPERTS)
                                               for g in groups])
        first = jnp.min(first, axis=0, keepdims=True)
        chosen.append(first)
        for g in groups:
            hit = (sub + g * GROUP_SIZE) == first
            picked[g] = jnp.where(hit, 1.0, picked[g])
            cand[g] = jnp.where(hit, neg, cand[g])

    picked_all = jnp.concatenate(picked, axis=0)
    r_i = lax.broadcasted_iota(jnp.int32, (tm, tm), 0)
    c_i = lax.broadcasted_iota(jnp.int32, (tm, tm), 1)
    earlier = jnp.where(r_i < c_i, 1.0, 0.0).astype(BF16)
    before = jnp.dot(picked_all.astype(BF16), earlier, preferred_element_type=F32) + cnt_ref[:, 0:1]
    cnt_ref[...] = cnt_ref[...] + jnp.sum(picked_all, axis=1, keepdims=True)

    def pick_value(table, first):
        parts = [jnp.where((sub + g * GROUP_SIZE) == first, blk_of(table, g), 0.0) for g in groups]
        return jnp.sum(functools.reduce(jnp.add, parts), axis=0, keepdims=True)

    w_k = [pick_value(scores, f) for f in chosen]
    denom = functools.reduce(jnp.add, w_k)
    eidx = jnp.zeros((TOP_K, tm), jnp.int32)
    rank = jnp.zeros((TOP_K, tm), jnp.int32)
    wts = jnp.zeros((TOP_K, tm), F32)
    for k in range(TOP_K):
        eidx = jnp.where(sub == k, chosen[k], eidx)
        rank = jnp.where(sub == k, pick_value(before, chosen[k]).astype(jnp.int32), rank)
        wts = jnp.where(sub == k, w_k[k] / denom * ROUTE_SCALE, wts)
    eidx_ref[...] = eidx
    rank_ref[...] = rank
    pad = jnp.zeros((GATE_LANES - TOP_K, tm), F32)
    wts_ref[0] = jnp.concatenate([wts, pad], axis=0).T


def _outproj(oa, ob, x, wa, wb, g1, n2, sc2, sh2, wr_t, rb, tm):
    B, T, D = x.shape
    nt = T // tm
    const = lambda shape: pl.BlockSpec(shape, lambda b, t: (0,) * len(shape))
    tile = lambda w: pl.BlockSpec((1, tm, w), lambda b, t: (b, t, 0))
    per_batch = pl.BlockSpec((1, 1, D), lambda b, t: (b, 0, 0))
    picks = pl.BlockSpec((TOP_K, tm), lambda b, t: (0, b * nt + t))
    return pl.pallas_call(
        _outproj_kernel,
        grid=(B, nt),
        in_specs=[tile(GW), tile(GW), tile(D), const(wa.shape), const(wb.shape), per_batch,
                  const((1, D)), per_batch, per_batch, const(wr_t.shape), const(rb.shape)],
        out_specs=[tile(D), pl.BlockSpec((tm, ROW_TILE, LANES), lambda b, t: (b * nt + t, 0, 0)),
                   picks, picks, tile(GATE_LANES), const((N_EXPERTS, GATE_LANES))],
        out_shape=[jax.ShapeDtypeStruct((B, T, D), F32), jax.ShapeDtypeStruct((B * T, ROW_TILE, LANES), ROW_DTYPE),
                   jax.ShapeDtypeStruct((TOP_K, B * T), jnp.int32), jax.ShapeDtypeStruct((TOP_K, B * T), jnp.int32),
                   jax.ShapeDtypeStruct((B, T, GATE_LANES), F32),
                   jax.ShapeDtypeStruct((N_EXPERTS, GATE_LANES), F32)],
        compiler_params=pltpu.CompilerParams(dimension_semantics=("arbitrary", "arbitrary"),
                                             vmem_limit_bytes=VMEM_LIMIT_BYTES),
        name="outproj",
    )(oa, ob, x, wa, wb, g1, n2, sc2, sh2, wr_t, rb)


def _slots_kernel(off_ref, eidx_ref, rank_ref, slot_ref):
    eidx = eidx_ref[...]

    def add_expert(e, acc):
        return acc + jnp.where(eidx == e, off_ref[e], 0)

    slot_ref[...] = lax.fori_loop(0, N_EXPERTS, add_expert, rank_ref[...])


def _slots(offsets, eidx, rank):
    k, m = eidx.shape
    tile = pl.BlockSpec((k, m), lambda i: (0, 0))
    return pl.pallas_call(
        _slots_kernel,
        grid=(1,),
        in_specs=[pl.BlockSpec(memory_space=pltpu.SMEM), tile, tile],
        out_specs=tile,
        out_shape=jax.ShapeDtypeStruct((k, m), jnp.int32),
        compiler_params=pltpu.CompilerParams(dimension_semantics=("arbitrary",),
                                             vmem_limit_bytes=VMEM_LIMIT_BYTES),
        name="slots",
    )(offsets, eidx, rank)


def _dispatch_kernel(slot_ref, h_ref, xs_ref, sem):
    tmd = h_ref.shape[0]

    def start_rows(j, c):
        for k in range(TOP_K):
            pltpu.make_async_copy(h_ref.at[pl.ds(j, 1)], xs_ref.at[pl.ds(slot_ref[k, j], 1)], sem).start(
                priority=k % 2)
        return c

    lax.fori_loop(0, tmd, start_rows, 0)
    for _ in range(TOP_K):
        pltpu.make_async_copy(h_ref, xs_ref.at[pl.ds(0, tmd)], sem).wait()


def _dispatch(slot, h2, n_rows, tmd):
    M = h2.shape[0]
    return pl.pallas_call(
        _dispatch_kernel,
        grid=(M // tmd,),
        in_specs=[pl.BlockSpec((TOP_K, tmd), lambda i: (0, i), memory_space=pltpu.SMEM),
                  pl.BlockSpec((tmd, ROW_TILE, LANES), lambda i: (i, 0, 0))],
        out_specs=pl.BlockSpec(memory_space=pl.ANY),
        out_shape=jax.ShapeDtypeStruct((n_rows, ROW_TILE, LANES), h2.dtype),
        scratch_shapes=[pltpu.SemaphoreType.DMA],
        compiler_params=pltpu.CompilerParams(dimension_semantics=("arbitrary",),
                                             vmem_limit_bytes=VMEM_LIMIT_BYTES),
        name="dispatch",
    )(slot, h2)


FFN_PIPELINE_STEPS = 2


def _ffn_kernel(te_ref, nu_ref, nv_ref, x_ref, wg_ref, wu_ref, wd_ref, y_ref, wgu_s, wd_s, xstd, ystd):
    i = pl.program_id(0)
    row = lax.broadcasted_iota(jnp.int32, (MOE_ROWS, 1), 0)
    tile_mm = jnp.maximum(i - 1, 0)
    tile_out = jnp.maximum(i - 2, 0)

    @pl.when(i == 0)
    def _():
        xstd[...] = jnp.zeros(xstd.shape, BF16)
        ystd[...] = jnp.zeros(ystd.shape, F32)

    @pl.when(i < nu_ref[0] + FFN_PIPELINE_STEPS)
    def _():
        @pl.when((i == 0) | (te_ref[tile_mm] != te_ref[tile_out]))
        def _():
            wgu_s[:, :D_EXPERT] = wg_ref[0].astype(BF16)
            wgu_s[:, D_EXPERT:] = wu_ref[0].astype(BF16)
            wd_s[...] = wd_ref[0].astype(BF16)

        cur = i % 2
        xstd[cur] = jnp.where(row < nv_ref[i], _to_rows(x_ref[...].astype(F32)), 0.0).astype(BF16)
        y_ref[...] = _to_row_tiles(ystd[cur]).astype(y_ref.dtype)
        gu = jnp.dot(xstd[1 - cur], wgu_s[...], preferred_element_type=F32)
        act = _silu(gu[:, :D_EXPERT]) * gu[:, D_EXPERT:]
        ystd[1 - cur] = jnp.dot(act.astype(BF16), wd_s[...], preferred_element_type=F32)


def _ffn(tile_expert, n_used, n_valid, xs, wg, wu, wd):
    D = wg.shape[1]
    n_tiles = xs.shape[0] // MOE_ROWS
    block = (MOE_ROWS, ROW_TILE, LANES)
    expert = lambda i, te, nu, nv: (te[jnp.maximum(i - 1, 0)], 0, 0)
    return pl.pallas_call(
        _ffn_kernel,
        grid_spec=pltpu.PrefetchScalarGridSpec(
            num_scalar_prefetch=3,
            grid=(n_tiles + FFN_PIPELINE_STEPS,),
            in_specs=[pl.BlockSpec(block, lambda i, te, nu, nv: (jnp.minimum(i, nu[0] - 1), 0, 0)),
                      pl.BlockSpec((1, D, D_EXPERT), expert),
                      pl.BlockSpec((1, D, D_EXPERT), expert),
                      pl.BlockSpec((1, D_EXPERT, D), expert)],
            out_specs=pl.BlockSpec(block, lambda i, te, nu, nv: (jnp.clip(i - 2, 0, nu[0] - 1), 0, 0)),
            scratch_shapes=[pltpu.VMEM((D, 2 * D_EXPERT), BF16), pltpu.VMEM((D_EXPERT, D), BF16),
                            pltpu.VMEM((2, MOE_ROWS, D), BF16), pltpu.VMEM((2, MOE_ROWS, D), F32)]),
        out_shape=jax.ShapeDtypeStruct(xs.shape, xs.dtype),
        compiler_params=pltpu.CompilerParams(dimension_semantics=("arbitrary",),
                                             vmem_limit_bytes=VMEM_LIMIT_BYTES),
        name="ffn",
    )(tile_expert, n_used, n_valid, xs, wg, wu, wd)


def _combine_kernel(slot_ref, next_slot_ref, w_ref, h_ref, x1_ref, g2_ref, fw_ref, wsgu_ref, wsd_ref, y_ref,
                    o_ref, buf, sem):
    tmc = h_ref.shape[0]
    i = pl.program_id(0)
    last = pl.num_programs(0) - 1

    def start_rows(slots, b):
        def body(j, c):
            for k in range(TOP_K):
                pltpu.make_async_copy(y_ref.at[pl.ds(slots[k, j], 1)], buf.at[b, k, pl.ds(j, 1)], sem.at[b]).start(
                    priority=k % 2)
            return c
        lax.fori_loop(0, tmc, body, 0)

    def wait_rows(b):
        for k in range(TOP_K):
            pltpu.make_async_copy(y_ref.at[pl.ds(0, tmc)], buf.at[b, k], sem.at[b]).wait()

    def for_buffer(step, fn):
        for b in range(2):
            @pl.when(step % 2 == b)
            def _():
                fn(b)

    @pl.when(i == 0)
    def _():
        start_rows(slot_ref, 0)

    @pl.when(i < last)
    def _():
        for_buffer(i + 1, lambda b: start_rows(next_slot_ref, b))

    gu = jnp.dot(_to_rows(h_ref[...].astype(F32)).astype(BF16), wsgu_ref[...], preferred_element_type=F32)
    act = _silu(gu[:, :D_EXPERT]) * gu[:, D_EXPERT:]
    shared = jnp.dot(act.astype(BF16), wsd_ref[...], preferred_element_type=F32)
    w = w_ref[...]
    lane = lax.broadcasted_iota(jnp.int32, w.shape, 1)

    def finish(b):
        wait_rows(b)
        routed = jnp.zeros((tmc, ROW_TILE, LANES), F32)
        for k in range(TOP_K):
            wk = jnp.broadcast_to(_lane_pick(w, lane, k), (tmc, LANES))
            routed = routed + _to_row_tiles(jnp.concatenate([wk] * ROW_TILE, axis=1)) * buf[b, k].astype(F32)
        acc = shared + _to_rows(routed)
        y = x1_ref[...] + g2_ref[0] * acc
        o_ref[...] = _rms(y) * fw_ref[...]

    for_buffer(i, finish)


def _combine(slot, wts, h2, x1, g2, fw, wsgu, wsd, ys, T, tmc):
    M, D = x1.shape
    const = lambda shape: pl.BlockSpec(shape, lambda i: (0,) * len(shape))
    tile = lambda w: pl.BlockSpec((tmc, w), lambda i: (i, 0))
    row_tiled = pl.BlockSpec((tmc, ROW_TILE, LANES), lambda i: (i, 0, 0))
    n_steps = M // tmc
    return pl.pallas_call(
        _combine_kernel,
        grid=(n_steps,),
        in_specs=[pl.BlockSpec((TOP_K, tmc), lambda i: (0, i), memory_space=pltpu.SMEM),
                  pl.BlockSpec((TOP_K, tmc), lambda i: (0, jnp.minimum(i + 1, n_steps - 1)), memory_space=pltpu.SMEM),
                  tile(GATE_LANES), row_tiled, tile(D),
                  pl.BlockSpec((1, 1, D), lambda i: (i // (T // tmc), 0, 0)),
                  const((1, D)), const(wsgu.shape), const(wsd.shape),
                  pl.BlockSpec(memory_space=pl.ANY)],
        out_specs=tile(D),
        out_shape=jax.ShapeDtypeStruct((M, D), F32),
        scratch_shapes=[pltpu.VMEM((2, TOP_K, tmc, ROW_TILE, LANES), ys.dtype), pltpu.SemaphoreType.DMA((2,))],
        compiler_params=pltpu.CompilerParams(dimension_semantics=("arbitrary",),
                                             vmem_limit_bytes=VMEM_LIMIT_BYTES),
        name="combine",
    )(slot, slot, wts, h2, x1, g2, fw, wsgu, wsd, ys)


def _pick_tile(n, want):
    t = min(n, want)
    assert n % t == 0 and t % CHUNK == 0, (n, want)
    return t


def kernel(x, c, w_ada, b_ada, norm1_w, w_in, conv_w, gdn_a_log, gdn_dt_bias, gdn_norm_w, hg_lb, hg_norm_w,
           w_out, norm2_w, w_router, router_bias, w_gate, w_up, w_down, ws_gate, ws_up, ws_down, final_norm_w):
    B, T, D = x.shape
    M = B * T
    depth = w_ada.shape[0]
    assert depth == 1 and T % CHUNK == 0 and B <= 8
    layer = 0
    tt = _pick_tile(T, 512)

    c_pad = jnp.pad(c, ((0, 8 - B), (0, 0)))
    mod = _ada(c_pad, w_ada[layer], b_ada[layer].reshape(1, -1))[:B]
    sh1, sc1, g1, sh2, sc2, g2 = (m.reshape(B, 1, D) for m in jnp.split(mod, 6, axis=-1))

    w = w_in[layer]
    qkv_w = 3 * GW
    sizes = (GW, HEADS, HEADS, GW, GW, GW, GW)
    offs = [qkv_w]
    for s in sizes:
        offs.append(offs[-1] + s)
    seg = lambda i: w[:, offs[i]:offs[i + 1]]
    small = jnp.pad(jnp.concatenate([seg(1), seg(2)], axis=1), ((0, 0), (0, GATE_LANES - 2 * HEADS)))
    w_all = jnp.concatenate([w[:, :qkv_w], seg(0), seg(3), seg(4), seg(5), seg(6), small], axis=1).astype(BF16)
    lane_pad = lambda v: jnp.pad(v.astype(F32).reshape(1, HEADS), ((0, 0), (HEADS, GATE_LANES - 2 * HEADS)))
    idx = jnp.arange(CUMSUM_ROWS)
    tri = ((idx[:, None] >= idx[None, :]) & (idx[:, None] // CHUNK == idx[None, :] // CHUNK)).astype(BF16)

    qa, ka, va, ga, sm, bcum, kb, ib, qb, gb = _inproj(
        layer, x, norm1_w[layer].reshape(1, D), sc1, sh1, w_all, conv_w[layer].astype(F32),
        lane_pad(gdn_a_log[layer]), lane_pad(gdn_dt_bias[layer]), hg_lb.astype(F32), tri, _pick_tile(T, 1024))

    gct = sm[:, :, HEADS:2 * HEADS].transpose(0, 2, 1).reshape(B, HEADS, T // CHUNK, CHUNK)
    oa = _gdn(qa, ka, va, ga, sm, gct, gdn_norm_w[layer].reshape(1, DH), tt)
    ob = _hgrn(qb, kb, ib, bcum, gb, hg_norm_w[layer].reshape(1, GW), tt)

    wo = w_out[layer].astype(BF16)
    x1, h2, eidx, rank, wts, cnt = _outproj(oa, ob, x, wo[:GW], wo[GW:], g1, norm2_w[layer].reshape(1, D), sc2, sh2,
                                            w_router[layer].T, router_bias[layer].reshape(N_EXPERTS, 1),
                                            _pick_tile(T, 1024))

    counts = cnt[:, 0].astype(jnp.int32)
    padded = (counts + MOE_ROWS - 1) // MOE_ROWS * MOE_ROWS
    ends = jnp.cumsum(padded)
    offsets = ends - padded
    n_tiles = (M * TOP_K) // MOE_ROWS + N_EXPERTS
    n_used = (ends[-1] // MOE_ROWS).astype(jnp.int32)
    tile_ids = jnp.minimum(jnp.arange(n_tiles + FFN_PIPELINE_STEPS, dtype=jnp.int32), n_used - 1)
    tile_expert = jnp.sum(((ends // MOE_ROWS)[None, :] <= tile_ids[:, None]).astype(jnp.int32), axis=1)
    tile_expert = jnp.minimum(tile_expert, N_EXPERTS - 1)
    slot = _slots(offsets.astype(jnp.int32), eidx, rank)
    first_tile = jnp.take(offsets, tile_expert) // MOE_ROWS
    n_valid = jnp.clip(jnp.take(counts, tile_expert) - (tile_ids - first_tile) * MOE_ROWS, 0, MOE_ROWS)

    assert D == ROW_TILE * LANES
    xs = _dispatch(slot, h2, n_tiles * MOE_ROWS, _pick_tile(M, 1024))
    ys = _ffn(tile_expert.astype(jnp.int32), n_used.reshape(1), n_valid.astype(jnp.int32), xs,
              w_gate[layer], w_up[layer], w_down[layer])
    wsgu = jnp.concatenate([ws_gate[layer], ws_up[layer]], axis=-1).astype(BF16)
    out = _combine(slot, wts.reshape(M, GATE_LANES), h2, x1.reshape(M, D), g2, final_norm_w.reshape(1, D),
                   wsgu, ws_down[layer].astype(BF16), ys, T, _pick_tile(T, 256))
    return out.reshape(B, T, D)
```

```python
import functools

import jax
import jax.numpy as jnp
from jax import lax
from jax.experimental import pallas as pl
from jax.experimental.pallas import tpu as pltpu

F32 = jnp.float32
BF16 = jnp.bfloat16

EPS = 1e-6
CHUNK = 64
SUB = 8
HEADS = 4
DH = 128
GW = HEADS * DH
CONV_K = 4
N_EXPERTS = 64
N_GROUPS = 8
GROUP_SIZE = N_EXPERTS // N_GROUPS
TOPK_GROUPS = 4
TOP_K = 8
D_EXPERT = 256
ROUTE_SCALE = 2.5
GATE_LANES = 128
GDN_CHUNKS_PER_ITER = 8
INV_BLOCK = 16
CUMSUM_ROWS = 128
MOE_ROWS = 512
ROW_TILE, LANES = 8, 128

VMEM_LIMIT_BYTES = 56 * 1024 * 1024

ACT = BF16
ROW_DTYPE = F32


def _silu(x):
    return x * jax.nn.sigmoid(x)


def _dot(a, b):
    return jnp.dot(a.astype(BF16), b.astype(BF16), preferred_element_type=F32)


def _dot_nt(a, b):
    return lax.dot_general(a.astype(BF16), b.astype(BF16), (((1,), (1,)), ((), ())),
                           preferred_element_type=F32)


def _dot_tn(a, b):
    return lax.dot_general(a.astype(BF16), b.astype(BF16), (((0,), (0,)), ((), ())),
                           preferred_element_type=F32)


def _split2(x):
    hi = x.astype(BF16)
    lo = (x - hi.astype(F32)).astype(BF16)
    return hi, lo


def _dot3(a, b, dot=_dot):
    ah, al = _split2(a)
    bh, bl = _split2(b)
    return dot(ah, bh) + dot(ah, bl) + dot(al, bh)


def _cumsum_rows(tri, x):
    hi = x.astype(BF16)
    r = x - hi.astype(F32)
    mid = r.astype(BF16)
    lo = (r - mid.astype(F32)).astype(BF16)
    g = tri.shape[0]
    groups = []
    for r0 in range(0, x.shape[0], g):
        rows = slice(r0, r0 + g)
        groups.append(jnp.dot(tri, hi[rows], preferred_element_type=F32)
                      + jnp.dot(tri, mid[rows], preferred_element_type=F32)
                      + jnp.dot(tri, lo[rows], preferred_element_type=F32))
    return jnp.concatenate(groups, axis=0)


def _lane_pick(tile, lane, idx):
    return jnp.sum(jnp.where(lane == idx, tile, 0.0), axis=1, keepdims=True)


def _rms(x):
    return x * lax.rsqrt(jnp.mean(x * x, axis=-1, keepdims=True) + EPS)


def _to_rows(x3):
    r = x3.shape[0]
    xt = jnp.swapaxes(x3.reshape(r // ROW_TILE, ROW_TILE, ROW_TILE, LANES), 1, 2)
    return jnp.concatenate([xt[:, s].reshape(r, LANES) for s in range(ROW_TILE)], axis=1)


def _to_row_tiles(x):
    r = x.shape[0]
    xt = jnp.stack([x[:, s * LANES:(s + 1) * LANES].reshape(r // ROW_TILE, ROW_TILE, LANES) for s in range(ROW_TILE)],
                   axis=1)
    return jnp.swapaxes(xt, 1, 2).reshape(r, ROW_TILE, LANES)


def _ada_kernel(c_ref, w_ref, b_ref, o_ref):
    ca = _silu(c_ref[...])
    o_ref[...] = _dot3(ca, w_ref[...]) + b_ref[...]


def _ada(c_pad, w, b):
    rows, d = c_pad.shape
    n = w.shape[1]
    tn = 1024
    return pl.pallas_call(
        _ada_kernel,
        grid=(n // tn,),
        in_specs=[pl.BlockSpec((rows, d), lambda j: (0, 0)),
                  pl.BlockSpec((d, tn), lambda j: (0, j)),
                  pl.BlockSpec((1, tn), lambda j: (0, j))],
        out_specs=pl.BlockSpec((rows, tn), lambda j: (0, j)),
        out_shape=jax.ShapeDtypeStruct((rows, n), F32),
        compiler_params=pltpu.CompilerParams(dimension_semantics=("arbitrary",),
                                             vmem_limit_bytes=VMEM_LIMIT_BYTES),
        name="ada",
    )(c_pad, w, b)


def _inproj_kernel(layer, x_ref, n1_ref, sc_ref, sh_ref, w_ref, cw_ref, alog_ref, dt_ref, lb_ref, tri_ref,
                   qa_ref, ka_ref, va_ref, ga_ref, sm_ref, b_ref, kb_ref, ib_ref, qb_ref, gb_ref,
                   pbuf):
    tt = x_ref.shape[1]
    t = pl.program_id(1)

    h = _rms(x_ref[0]) * n1_ref[...]
    h = h * (1.0 + sc_ref[0]) + sh_ref[0]
    hb = h.astype(BF16)

    def proj(g, width=GW):
        return jnp.dot(hb, w_ref[:, g * GW:g * GW + width], preferred_element_type=F32)

    @pl.when(t == 0)
    def _():
        pbuf[:, 0:8, :] = jnp.zeros((3, 8, GW), F32)

    for g, out_ref in enumerate((qa_ref, ka_ref, va_ref)):
        cols = slice(g * GW, (g + 1) * GW)
        p = proj(g)
        pbuf[g, 8:8 + tt, :] = p
        y = p * cw_ref[CONV_K - 1:CONV_K, cols]
        for j in range(1, CONV_K):
            y = y + pbuf[g, 8 - j:8 - j + tt, :] * cw_ref[CONV_K - 1 - j:CONV_K - j, cols]
        pbuf[g, 0:8, :] = pbuf[g, tt:tt + 8, :]
        y = _silu(y)
        if g == 2:
            out_ref[0] = y.astype(out_ref.dtype)
        else:
            scale = DH ** -0.5 if g == 0 else 1.0
            for hh in range(HEADS):
                hs = slice(hh * DH, (hh + 1) * DH)
                yh = y[:, hs]
                inv = lax.rsqrt(jnp.sum(yh * yh, axis=-1, keepdims=True) + EPS)
                out_ref[0, :, hs] = (yh * inv * scale).astype(out_ref.dtype)

    ga_ref[0] = _silu(proj(3)).astype(ga_ref.dtype)

    ps = proj(8, GATE_LANES)
    lane = lax.broadcasted_iota(jnp.int32, ps.shape, 1)
    beta = jax.nn.sigmoid(ps)
    z = ps + dt_ref[...]
    softplus = jnp.maximum(z, 0.0) + jnp.log1p(jnp.exp(-jnp.abs(z)))
    g_log = -jnp.exp(alog_ref[...]) * softplus
    tri = tri_ref[...]
    gc = _cumsum_rows(tri, jnp.where((lane >= HEADS) & (lane < 2 * HEADS), g_log, 0.0))
    sm_ref[0] = jnp.where(lane < HEADS, beta, gc)

    hl = lb_ref[...]
    e = jnp.exp(hl - jnp.max(hl, axis=0, keepdims=True))
    lb = jnp.sum(e[0:layer + 1], axis=0, keepdims=True) / jnp.sum(e, axis=0, keepdims=True)
    fr = proj(4)
    logf = jnp.log(lb + (1.0 - lb) * jax.nn.sigmoid(fr))
    b_ref[0] = _cumsum_rows(tri, logf)
    kb_ref[0] = ((1.0 - lb) * jax.nn.sigmoid(-fr)).astype(kb_ref.dtype)
    ib_ref[0] = proj(5).astype(ib_ref.dtype)
    qb_ref[0] = _silu(proj(6)).astype(qb_ref.dtype)
    gb_ref[0] = _silu(proj(7)).astype(gb_ref.dtype)


def _inproj(layer, x, n1, sc1, sh1, w_all, conv_w, alog_pad, dt_pad, hg_lb, tri, tt):
    B, T, D = x.shape
    const = lambda shape: pl.BlockSpec(shape, lambda b, t: (0,) * len(shape), pipeline_mode=pl.Buffered(1))
    act = lambda dt: jax.ShapeDtypeStruct((B, T, GW), dt)
    tile = lambda w: pl.BlockSpec((1, tt, w), lambda b, t: (b, t, 0))
    per_batch = pl.BlockSpec((1, 1, D), lambda b, t: (b, 0, 0))
    return pl.pallas_call(
        functools.partial(_inproj_kernel, layer),
        grid=(B, T // tt),
        in_specs=[tile(D), const((1, D)), per_batch, per_batch,
                  const(w_all.shape), const(conv_w.shape), const((1, GATE_LANES)), const((1, GATE_LANES)),
                  const(hg_lb.shape), const(tri.shape)],
        out_specs=[tile(GW), tile(GW), tile(GW), tile(GW), tile(GATE_LANES), tile(GW),
                   tile(GW), tile(GW), tile(GW), tile(GW)],
        out_shape=[act(ACT), act(ACT), act(ACT), act(ACT),
                   jax.ShapeDtypeStruct((B, T, GATE_LANES), F32), act(F32),
                   act(ACT), act(ACT), act(ACT), act(ACT)],
        scratch_shapes=[pltpu.VMEM((3, tt + 8, GW), F32)],
        compiler_params=pltpu.CompilerParams(dimension_semantics=("arbitrary", "arbitrary"),
                                             vmem_limit_bytes=VMEM_LIMIT_BYTES),
        name="inproj",
    )(x, n1, sc1, sh1, w_all, conv_w, alog_pad, dt_pad, hg_lb, tri)


def _gdn_prep_kernel(q_ref, k_ref, v_ref, sm_ref, gct_ref, o_ref, qt_ref, m_ref, n_ref):
    tt = q_ref.shape[1]
    nc = tt // CHUNK
    row = lax.broadcasted_iota(jnp.int32, (CHUNK, CHUNK), 0)
    col = lax.broadcasted_iota(jnp.int32, (CHUNK, CHUNK), 1)
    causal = row >= col
    diag_blk = (row > col) & (row // INV_BLOCK == col // INV_BLOCK)
    off_blk = row // INV_BLOCK > col // INV_BLOCK
    eye = jnp.where(row == col, 1.0, 0.0)
    lane = lax.broadcasted_iota(jnp.int32, (CHUNK, GATE_LANES), 1)
    assert INV_BLOCK == 16 and CHUNK == 4 * INV_BLOCK

    def body(i, carry):
        chains = [(GDN_CHUNKS_PER_ITER * i + j, hh) for j in range(GDN_CHUNKS_PER_ITER) for hh in range(HEADS)]
        rows = [pl.ds(pl.multiple_of(c * CHUNK, CHUNK), CHUNK) for c, _ in chains]
        hs = [slice(hh * DH, (hh + 1) * DH) for _, hh in chains]
        n = range(len(chains))
        sm = [sm_ref[0, rows[j], :] for j in n]
        q = [q_ref[0, rows[j], hs[j]].astype(F32) for j in n]
        k = [k_ref[0, rows[j], hs[j]].astype(F32) for j in n]
        v = [v_ref[0, rows[j], hs[j]].astype(F32) for j in n]
        beta = [_lane_pick(sm[j], lane, chains[j][1]) for j in n]
        gcol = [_lane_pick(sm[j], lane, HEADS + chains[j][1]) for j in n]
        grow = [gct_ref[0, hh, pl.ds(c, 1), :] for c, hh in chains]
        decay = [jnp.exp(jnp.where(causal, gcol[j] - grow[j], -jnp.inf)) for j in n]
        kb = [k[j] * beta[j] for j in n]
        L = [_dot_nt(kb[j], k[j]) * decay[j] for j in n]
        dg = [jnp.where(diag_blk, L[j], 0.0) for j in n]
        off = [jnp.where(off_blk, L[j], 0.0) for j in n]
        dinv = [eye - dg[j] for j in n]
        pw = [_dot3(dg[j], dg[j]) for j in n]
        for _ in range(2):
            dinv = [dinv[j] + _dot3(dinv[j], pw[j]) for j in n]
            pw = [_dot3(pw[j], pw[j]) for j in n]
        dinv = [dinv[j] + _dot3(dinv[j], pw[j]) for j in n]
        f1 = [_dot(dinv[j], off[j]) for j in n]
        f2 = [_dot(f1[j], f1[j]) for j in n]
        f3 = [_dot(f1[j], f2[j]) for j in n]
        tinv = [_dot(eye - f1[j] + f2[j] - f3[j], dinv[j]) for j in n]
        eg = [jnp.exp(gcol[j]) for j in n]
        sol = [_dot(tinv[j], jnp.concatenate([v[j] * beta[j], kb[j] * eg[j]], axis=1)) for j in n]
        attn = [_dot_nt(q[j], k[j]) * decay[j] for j in n]
        k_tail = [k[j] * jnp.exp(gcol[j][CHUNK - 1:CHUNK, :] - gcol[j]) for j in n]
        au = [_dot(attn[j], sol[j]) for j in n]
        ku = [_dot_tn(k_tail[j], sol[j]) for j in n]
        for j, (c, hh) in enumerate(chains):
            o_ref[0, rows[j], hs[j]] = au[j][:, :DH]
            qt_ref[0, rows[j], hs[j]] = (q[j] * eg[j] - au[j][:, DH:]).astype(qt_ref.dtype)
            n_ref[0, hh, c] = ku[j][:, :DH].astype(n_ref.dtype)
            m_ref[0, hh, c] = (-ku[j][:, DH:]).astype(m_ref.dtype)
        return carry

    lax.fori_loop(0, nc // GDN_CHUNKS_PER_ITER, body, 0)


def _gdn_scan_kernel(o_ref, qt_ref, m_ref, n_ref, gct_ref, sg_ref, nw_ref, out_ref, s_ref):
    nb, tt = o_ref.shape[0], o_ref.shape[1]
    nc = tt // CHUNK

    @pl.when(pl.program_id(0) == 0)
    def _():
        s_ref[...] = jnp.zeros(s_ref.shape, F32)

    nw = nw_ref[...]

    def body(c, carry):
        rows = pl.ds(pl.multiple_of(c * CHUNK, CHUNK), CHUNK)
        for b in range(nb):
            for hh in range(HEADS):
                hs = slice(hh * DH, (hh + 1) * DH)
                S = s_ref[b, hh]
                Sb = S.astype(BF16)
                glast = gct_ref[b, hh, pl.ds(c, 1), :][:, CHUNK - 1:CHUNK]
                o = o_ref[b, rows, hs] + jnp.dot(qt_ref[b, rows, hs], Sb, preferred_element_type=F32)
                s_ref[b, hh] = (S * jnp.exp(glast) + jnp.dot(m_ref[b, hh, c], Sb, preferred_element_type=F32)
                                + n_ref[b, hh, c].astype(F32))
                o = _rms(o) * nw * sg_ref[b, rows, hs].astype(F32)
                out_ref[b, rows, hs] = o.astype(out_ref.dtype)
        return carry

    lax.fori_loop(0, nc, body, 0)


def _gdn(q, k, v, sg, sm, gct, nw, tt):
    B, T, _ = q.shape
    nc = tt // CHUNK
    assert nc % GDN_CHUNKS_PER_ITER == 0
    n_chunks = T // CHUNK
    tile = lambda w: pl.BlockSpec((1, tt, w), lambda b, t: (b, t, 0))
    mat = jax.ShapeDtypeStruct((B, HEADS, n_chunks, DH, DH), ACT)
    o_part, qt, m, n = pl.pallas_call(
        _gdn_prep_kernel,
        grid=(B, T // tt),
        in_specs=[tile(GW), tile(GW), tile(GW), tile(GATE_LANES),
                  pl.BlockSpec((1, HEADS, nc, CHUNK), lambda b, t: (b, 0, t, 0))],
        out_specs=[tile(GW), tile(GW),
                   pl.BlockSpec((1, HEADS, nc, DH, DH), lambda b, t: (b, 0, t, 0, 0)),
                   pl.BlockSpec((1, HEADS, nc, DH, DH), lambda b, t: (b, 0, t, 0, 0))],
        out_shape=[jax.ShapeDtypeStruct((B, T, GW), F32), jax.ShapeDtypeStruct((B, T, GW), ACT), mat, mat],
        compiler_params=pltpu.CompilerParams(dimension_semantics=("arbitrary", "arbitrary"),
                                             vmem_limit_bytes=VMEM_LIMIT_BYTES),
        name="gdn_prep",
    )(q, k, v, sm, gct)

    full = lambda w: pl.BlockSpec((B, tt, w), lambda t: (0, t, 0))
    mats = pl.BlockSpec((B, HEADS, nc, DH, DH), lambda t: (0, 0, t, 0, 0))
    return pl.pallas_call(
        _gdn_scan_kernel,
        grid=(T // tt,),
        in_specs=[full(GW), full(GW), mats, mats,
                  pl.BlockSpec((B, HEADS, nc, CHUNK), lambda t: (0, 0, t, 0)),
                  full(GW), pl.BlockSpec((1, DH), lambda t: (0, 0))],
        out_specs=full(GW),
        out_shape=jax.ShapeDtypeStruct((B, T, GW), ACT),
        scratch_shapes=[pltpu.VMEM((B, HEADS, DH, DH), F32)],
        compiler_params=pltpu.CompilerParams(dimension_semantics=("arbitrary",),
                                             vmem_limit_bytes=VMEM_LIMIT_BYTES),
        name="gdn_scan",
    )(o_part, qt, m, n, gct, sg, nw)


def _hgrn_kernel(q_ref, k_ref, v_ref, b_ref, sg_ref, nw_ref, o_ref, st_ref):
    nb, tt = q_ref.shape[0], q_ref.shape[1]
    nc = tt // CHUNK

    @pl.when(pl.program_id(0) == 0)
    def _():
        st_ref[...] = jnp.zeros(st_ref.shape, F32)

    row = lax.broadcasted_iota(jnp.int32, (CHUNK, CHUNK), 0)
    col = lax.broadcasted_iota(jnp.int32, (CHUNK, CHUNK), 1)
    diag_block = ((col // SUB) == (row // SUB)) & (col <= row)
    chains = [(bi, hh) for bi in range(nb) for hh in range(HEADS)]
    n = range(len(chains))
    hs = [slice(hh * DH, (hh + 1) * DH) for _, hh in chains]

    def body(c, carry):
        rows = pl.ds(pl.multiple_of(c * CHUNK, CHUNK), CHUNK)
        q = [q_ref[bi, rows, hs[j]].astype(F32) for j, (bi, _) in enumerate(chains)]
        k = [k_ref[bi, rows, hs[j]].astype(F32) for j, (bi, _) in enumerate(chains)]
        v = [v_ref[bi, rows, hs[j]].astype(F32) for j, (bi, _) in enumerate(chains)]
        b = [b_ref[bi, rows, hs[j]] for j, (bi, _) in enumerate(chains)]
        blast = [b[j][CHUNK - 1:CHUNK, :] for j in n]
        st = [st_ref[bi, hh] for bi, hh in chains]
        o = [_dot_nt(q[j] * jnp.exp(b[j]), st[j]) for j in n]
        k_tail = [k[j] * jnp.exp(blast[j] - b[j]) for j in n]
        for j, (bi, hh) in enumerate(chains):
            st_ref[bi, hh] = st[j] * jnp.exp(blast[j]) + _dot_tn(v[j], k_tail[j])

        blocks = [[jnp.zeros((SUB, CHUNK), F32)] for _ in n]
        for i in range(1, CHUNK // SUB):
            lo, hi = i * SUB, (i + 1) * SUB
            for j in n:
                r = b[j][lo:lo + 1, :]
                qi = q[j][lo:hi] * jnp.exp(b[j][lo:hi] - r)
                kj = k[j][:lo] * jnp.exp(jnp.minimum(r - b[j][:lo], 0.0))
                kj = jnp.concatenate([kj, jnp.zeros((CHUNK - lo, DH), F32)], axis=0)
                blocks[j].append(_dot_nt(qi, kj))
        a = []
        for j in n:
            f = jnp.exp(jnp.minimum(b[j] - pltpu.roll(b[j], 1, 0), 0.0))
            e = None
            a_diag = jnp.zeros((CHUNK, CHUNK), F32)
            for delta in range(SUB):
                if delta == 0:
                    term = q[j] * k[j]
                else:
                    fsh = f if delta == 1 else pltpu.roll(f, delta - 1, 0)
                    e = fsh if e is None else e * fsh
                    term = q[j] * pltpu.roll(k[j], delta, 0) * e
                colv = jnp.sum(term, axis=1, keepdims=True)
                a_diag = jnp.where(row - col == delta, colv, a_diag)
            a.append(jnp.where(diag_block, a_diag, jnp.concatenate(blocks[j], axis=0)))

        o = [o[j] + _dot(a[j], v[j]) for j in n]
        for bi in range(nb):
            ob = jnp.concatenate(o[bi * HEADS:(bi + 1) * HEADS], axis=1)
            ob = _rms(ob) * nw_ref[...] * sg_ref[bi, rows, :].astype(F32)
            o_ref[bi, rows, :] = ob.astype(o_ref.dtype)
        return carry

    lax.fori_loop(0, nc, body, 0)


def _hgrn(q, k, v, b, sg, nw, tt):
    B, T, _ = q.shape
    tile = pl.BlockSpec((B, tt, GW), lambda t: (0, t, 0))
    return pl.pallas_call(
        _hgrn_kernel,
        grid=(T // tt,),
        in_specs=[tile, tile, tile, tile, tile, pl.BlockSpec((1, GW), lambda t: (0, 0))],
        out_specs=tile,
        out_shape=jax.ShapeDtypeStruct((B, T, GW), ACT),
        scratch_shapes=[pltpu.VMEM((B, HEADS, DH, DH), F32)],
        compiler_params=pltpu.CompilerParams(dimension_semantics=("arbitrary",),
                                             vmem_limit_bytes=VMEM_LIMIT_BYTES),
        name="hgrn",
    )(q, k, v, b, sg, nw)


def _outproj_kernel(oa_ref, ob_ref, x_ref, wa_ref, wb_ref, g1_ref, n2_ref, sc_ref, sh_ref, wr_ref, rb_ref,
                    x1_ref, h2_ref, eidx_ref, rank_ref, wts_ref, cnt_ref):
    tm = x_ref.shape[1]

    @pl.when((pl.program_id(0) == 0) & (pl.program_id(1) == 0))
    def _():
        cnt_ref[...] = jnp.zeros(cnt_ref.shape, F32)

    mix = (jnp.dot(oa_ref[0], wa_ref[...], preferred_element_type=F32)
           + jnp.dot(ob_ref[0], wb_ref[...], preferred_element_type=F32))
    x1 = x_ref[0] + g1_ref[0] * mix
    x1_ref[0] = x1
    h2 = _rms(x1) * n2_ref[...]
    h2 = h2 * (1.0 + sc_ref[0]) + sh_ref[0]
    h2_ref[...] = _to_row_tiles(h2).astype(h2_ref.dtype)

    scores = jax.nn.sigmoid(_dot3(wr_ref[...], h2, dot=_dot_nt))
    sel = scores + rb_ref[...]
    sub = lax.broadcasted_iota(jnp.int32, (GROUP_SIZE, tm), 0)
    neg = -jnp.inf
    groups = range(N_GROUPS)

    def take_max(blk):
        m = jnp.max(blk, axis=0, keepdims=True)
        first = jnp.min(jnp.where(blk == m, sub, GROUP_SIZE), axis=0, keepdims=True)
        hit = sub == first
        return m, hit, jnp.where(hit, neg, blk)

    blk_of = lambda a, g: a[g * GROUP_SIZE:(g + 1) * GROUP_SIZE]
    sel_blk = [blk_of(sel, g) for g in groups]
    group_score = jnp.zeros((N_GROUPS, tm), F32)
    for g in groups:
        m1, _, rest = take_max(sel_blk[g])
        m2 = jnp.max(rest, axis=0, keepdims=True)
        group_score = jnp.where(sub == g, m1 + m2, group_score)
    group_on = jnp.zeros((N_GROUPS, tm), F32)
    for _ in range(TOPK_GROUPS):
        _, hit, group_score = take_max(group_score)
        group_on = jnp.where(hit, 1.0, group_on)

    cand = [jnp.where(group_on[g:g + 1] > 0.0, sel_blk[g], neg) for g in groups]
    picked = [jnp.zeros((GROUP_SIZE, tm), F32) for _ in groups]
    chosen = []
    for _ in range(TOP_K):
        m = jnp.max(functools.reduce(jnp.maximum, cand), axis=0, keepdims=True)
        first = functools.reduce(jnp.minimum, [jnp.where(cand[g] == m, sub + g * GROUP_SIZE, N_EXPERTS)
                                               for g in groups])
        first = jnp.min(first, axis=0, keepdims=True)
        chosen.append(first)
        for g in groups:
            hit = (sub + g * GROUP_SIZE) == first
            picked[g] = jnp.where(hit, 1.0, picked[g])
            cand[g] = jnp.where(hit, neg, cand[g])

    picked_all = jnp.concatenate(picked, axis=0)
    r_i = lax.broadcasted_iota(jnp.int32, (tm, tm), 0)
    c_i = lax.broadcasted_iota(jnp.int32, (tm, tm), 1)
    earlier = jnp.where(r_i < c_i, 1.0, 0.0).astype(BF16)
    before = jnp.dot(picked_all.astype(BF16), earlier, preferred_element_type=F32) + cnt_ref[:, 0:1]
    cnt_ref[...] = cnt_ref[...] + jnp.sum(picked_all, axis=1, keepdims=True)

    def pick_value(table, first):
        parts = [jnp.where((sub + g * GROUP_SIZE) == first, blk_of(table, g), 0.0) for g in groups]
        return jnp.sum(functools.reduce(jnp.add, parts), axis=0, keepdims=True)

    w_k = [pick_value(scores, f) for f in chosen]
    denom = functools.reduce(jnp.add, w_k)
    eidx = jnp.zeros((TOP_K, tm), jnp.int32)
    rank = jnp.zeros((TOP_K, tm), jnp.int32)
    wts = jnp.zeros((TOP_K, tm), F32)
    for k in range(TOP_K):
        eidx = jnp.where(sub == k, chosen[k], eidx)
        rank = jnp.where(sub == k, pick_value(before, chosen[k]).astype(jnp.int32), rank)
        wts = jnp.where(sub == k, w_k[k] / denom * ROUTE_SCALE, wts)
    eidx_ref[...] = eidx
    rank_ref[...] = rank
    pad = jnp.zeros((GATE_LANES - TOP_K, tm), F32)
    wts_ref[0] = jnp.concatenate([wts, pad], axis=0).T


def _outproj(oa, ob, x, wa, wb, g1, n2, sc2, sh2, wr_t, rb, tm):
    B, T, D = x.shape
    nt = T // tm
    const = lambda shape: pl.BlockSpec(shape, lambda b, t: (0,) * len(shape))
    tile = lambda w: pl.BlockSpec((1, tm, w), lambda b, t: (b, t, 0))
    per_batch = pl.BlockSpec((1, 1, D), lambda b, t: (b, 0, 0))
    picks = pl.BlockSpec((TOP_K, tm), lambda b, t: (0, b * nt + t))
    return pl.pallas_call(
        _outproj_kernel,
        grid=(B, nt),
        in_specs=[tile(GW), tile(GW), tile(D), const(wa.shape), const(wb.shape), per_batch,
                  const((1, D)), per_batch, per_batch, const(wr_t.shape), const(rb.shape)],
        out_specs=[tile(D), pl.BlockSpec((tm, ROW_TILE, LANES), lambda b, t: (b * nt + t, 0, 0)),
                   picks, picks, tile(GATE_LANES), const((N_EXPERTS, GATE_LANES))],
        out_shape=[jax.ShapeDtypeStruct((B, T, D), F32), jax.ShapeDtypeStruct((B * T, ROW_TILE, LANES), ROW_DTYPE),
                   jax.ShapeDtypeStruct((TOP_K, B * T), jnp.int32), jax.ShapeDtypeStruct((TOP_K, B * T), jnp.int32),
                   jax.ShapeDtypeStruct((B, T, GATE_LANES), F32),
                   jax.ShapeDtypeStruct((N_EXPERTS, GATE_LANES), F32)],
        compiler_params=pltpu.CompilerParams(dimension_semantics=("arbitrary", "arbitrary"),
                                             vmem_limit_bytes=VMEM_LIMIT_BYTES),
        name="outproj",
    )(oa, ob, x, wa, wb, g1, n2, sc2, sh2, wr_t, rb)


def _slots_kernel(off_ref, eidx_ref, rank_ref, slot_ref):
    eidx = eidx_ref[...]

    def add_expert(e, acc):
        return acc + jnp.where(eidx == e, off_ref[e], 0)

    slot_ref[...] = lax.fori_loop(0, N_EXPERTS, add_expert, rank_ref[...])


def _slots(offsets, eidx, rank):
    k, m = eidx.shape
    tile = pl.BlockSpec((k, m), lambda i: (0, 0))
    return pl.pallas_call(
        _slots_kernel,
        grid=(1,),
        in_specs=[pl.BlockSpec(memory_space=pltpu.SMEM), tile, tile],
        out_specs=tile,
        out_shape=jax.ShapeDtypeStruct((k, m), jnp.int32),
        compiler_params=pltpu.CompilerParams(dimension_semantics=("arbitrary",),
                                             vmem_limit_bytes=VMEM_LIMIT_BYTES),
        name="slots",
    )(offsets, eidx, rank)


def _dispatch_kernel(slot_ref, h_ref, xs_ref, sem):
    tmd = h_ref.shape[0]

    def start_rows(j, c):
        for k in range(TOP_K):
            pltpu.make_async_copy(h_ref.at[pl.ds(j, 1)], xs_ref.at[pl.ds(slot_ref[k, j], 1)], sem).start(
                priority=k % 2)
        return c

    lax.fori_loop(0, tmd, start_rows, 0)
    for _ in range(TOP_K):
        pltpu.make_async_copy(h_ref, xs_ref.at[pl.ds(0, tmd)], sem).wait()


def _dispatch(slot, h2, n_rows, tmd):
    M = h2.shape[0]
    return pl.pallas_call(
        _dispatch_kernel,
        grid=(M // tmd,),
        in_specs=[pl.BlockSpec((TOP_K, tmd), lambda i: (0, i), memory_space=pltpu.SMEM),
                  pl.BlockSpec((tmd, ROW_TILE, LANES), lambda i: (i, 0, 0))],
        out_specs=pl.BlockSpec(memory_space=pl.ANY),
        out_shape=jax.ShapeDtypeStruct((n_rows, ROW_TILE, LANES), h2.dtype),
        scratch_shapes=[pltpu.SemaphoreType.DMA],
        compiler_params=pltpu.CompilerParams(dimension_semantics=("arbitrary",),
                                             vmem_limit_bytes=VMEM_LIMIT_BYTES),
        name="dispatch",
    )(slot, h2)


FFN_PIPELINE_STEPS = 2


def _ffn_kernel(te_ref, nu_ref, nv_ref, x_ref, wg_ref, wu_ref, wd_ref, y_ref, wgu_s, wd_s, xstd, ystd):
    i = pl.program_id(0)
    row = lax.broadcasted_iota(jnp.int32, (MOE_ROWS, 1), 0)
    tile_mm = jnp.maximum(i - 1, 0)
    tile_out = jnp.maximum(i - 2, 0)

    @pl.when(i == 0)
    def _():
        xstd[...] = jnp.zeros(xstd.shape, BF16)
        ystd[...] = jnp.zeros(ystd.shape, F32)

    @pl.when(i < nu_ref[0] + FFN_PIPELINE_STEPS)
    def _():
        @pl.when((i == 0) | (te_ref[tile_mm] != te_ref[tile_out]))
        def _():
            wgu_s[:, :D_EXPERT] = wg_ref[0].astype(BF16)
            wgu_s[:, D_EXPERT:] = wu_ref[0].astype(BF16)
            wd_s[...] = wd_ref[0].astype(BF16)

        cur = i % 2
        xstd[cur] = jnp.where(row < nv_ref[i], _to_rows(x_ref[...].astype(F32)), 0.0).astype(BF16)
        y_ref[...] = _to_row_tiles(ystd[cur]).astype(y_ref.dtype)
        gu = jnp.dot(xstd[1 - cur], wgu_s[...], preferred_element_type=F32)
        act = _silu(gu[:, :D_EXPERT]) * gu[:, D_EXPERT:]
        ystd[1 - cur] = jnp.dot(act.astype(BF16), wd_s[...], preferred_element_type=F32)


def _ffn(tile_expert, n_used, n_valid, xs, wg, wu, wd):
    D = wg.shape[1]
    n_tiles = xs.shape[0] // MOE_ROWS
    block = (MOE_ROWS, ROW_TILE, LANES)
    expert = lambda i, te, nu, nv: (te[jnp.maximum(i - 1, 0)], 0, 0)
    return pl.pallas_call(
        _ffn_kernel,
        grid_spec=pltpu.PrefetchScalarGridSpec(
            num_scalar_prefetch=3,
            grid=(n_tiles + FFN_PIPELINE_STEPS,),
            in_specs=[pl.BlockSpec(block, lambda i, te, nu, nv: (jnp.minimum(i, nu[0] - 1), 0, 0)),
                      pl.BlockSpec((1, D, D_EXPERT), expert),
                      pl.BlockSpec((1, D, D_EXPERT), expert),
                      pl.BlockSpec((1, D_EXPERT, D), expert)],
            out_specs=pl.BlockSpec(block, lambda i, te, nu, nv: (jnp.clip(i - 2, 0, nu[0] - 1), 0, 0)),
            scratch_shapes=[pltpu.VMEM((D, 2 * D_EXPERT), BF16), pltpu.VMEM((D_EXPERT, D), BF16),
                            pltpu.VMEM((2, MOE_ROWS, D), BF16), pltpu.VMEM((2, MOE_ROWS, D), F32)]),
        out_shape=jax.ShapeDtypeStruct(xs.shape, xs.dtype),
        compiler_params=pltpu.CompilerParams(dimension_semantics=("arbitrary",),
                                             vmem_limit_bytes=VMEM_LIMIT_BYTES),
        name="ffn",
    )(tile_expert, n_used, n_valid, xs, wg, wu, wd)


def _combine_kernel(slot_ref, next_slot_ref, w_ref, h_ref, x1_ref, g2_ref, fw_ref, wsgu_ref, wsd_ref, y_ref,
                    o_ref, buf, sem):
    tmc = h_ref.shape[0]
    i = pl.program_id(0)
    last = pl.num_programs(0) - 1

    def start_rows(slots, b):
        def body(j, c):
            for k in range(TOP_K):
                pltpu.make_async_copy(y_ref.at[pl.ds(slots[k, j], 1)], buf.at[b, k, pl.ds(j, 1)], sem.at[b]).start(
                    priority=k % 2)
            return c
        lax.fori_loop(0, tmc, body, 0)

    def wait_rows(b):
        for k in range(TOP_K):
            pltpu.make_async_copy(y_ref.at[pl.ds(0, tmc)], buf.at[b, k], sem.at[b]).wait()

    def for_buffer(step, fn):
        for b in range(2):
            @pl.when(step % 2 == b)
            def _():
                fn(b)

    @pl.when(i == 0)
    def _():
        start_rows(slot_ref, 0)

    @pl.when(i < last)
    def _():
        for_buffer(i + 1, lambda b: start_rows(next_slot_ref, b))

    gu = jnp.dot(_to_rows(h_ref[...].astype(F32)).astype(BF16), wsgu_ref[...], preferred_element_type=F32)
    act = _silu(gu[:, :D_EXPERT]) * gu[:, D_EXPERT:]
    shared = jnp.dot(act.astype(BF16), wsd_ref[...], preferred_element_type=F32)
    w = w_ref[...]
    lane = lax.broadcasted_iota(jnp.int32, w.shape, 1)

    def finish(b):
        wait_rows(b)
        routed = jnp.zeros((tmc, ROW_TILE, LANES), F32)
        for k in range(TOP_K):
            wk = jnp.broadcast_to(_lane_pick(w, lane, k), (tmc, LANES))
            routed = routed + _to_row_tiles(jnp.concatenate([wk] * ROW_TILE, axis=1)) * buf[b, k].astype(F32)
        acc = shared + _to_rows(routed)
        y = x1_ref[...] + g2_ref[0] * acc
        o_ref[...] = _rms(y) * fw_ref[...]

    for_buffer(i, finish)


def _combine(slot, wts, h2, x1, g2, fw, wsgu, wsd, ys, T, tmc):
    M, D = x1.shape
    const = lambda shape: pl.BlockSpec(shape, lambda i: (0,) * len(shape))
    tile = lambda w: pl.BlockSpec((tmc, w), lambda i: (i, 0))
    row_tiled = pl.BlockSpec((tmc, ROW_TILE, LANES), lambda i: (i, 0, 0))
    n_steps = M // tmc
    return pl.pallas_call(
        _combine_kernel,
        grid=(n_steps,),
        in_specs=[pl.BlockSpec((TOP_K, tmc), lambda i: (0, i), memory_space=pltpu.SMEM),
                  pl.BlockSpec((TOP_K, tmc), lambda i: (0, jnp.minimum(i + 1, n_steps - 1)), memory_space=pltpu.SMEM),
                  tile(GATE_LANES), row_tiled, tile(D),
                  pl.BlockSpec((1, 1, D), lambda i: (i // (T // tmc), 0, 0)),
                  const((1, D)), const(wsgu.shape), const(wsd.shape),
                  pl.BlockSpec(memory_space=pl.ANY)],
        out_specs=tile(D),
        out_shape=jax.ShapeDtypeStruct((M, D), F32),
        scratch_shapes=[pltpu.VMEM((2, TOP_K, tmc, ROW_TILE, LANES), ys.dtype), pltpu.SemaphoreType.DMA((2,))],
        compiler_params=pltpu.CompilerParams(dimension_semantics=("arbitrary",),
                                             vmem_limit_bytes=VMEM_LIMIT_BYTES),
        name="combine",
    )(slot, slot, wts, h2, x1, g2, fw, wsgu, wsd, ys)


def _pick_tile(n, want):
    t = min(n, want)
    assert n % t == 0 and t % CHUNK == 0, (n, want)
    return t


def kernel(x, c, w_ada, b_ada, norm1_w, w_in, conv_w, gdn_a_log, gdn_dt_bias, gdn_norm_w, hg_lb, hg_norm_w,
           w_out, norm2_w, w_router, router_bias, w_gate, w_up, w_down, ws_gate, ws_up, ws_down, final_norm_w):
    B, T, D = x.shape
    M = B * T
    depth = w_ada.shape[0]
    assert depth == 1 and T % CHUNK == 0 and B <= 8
    layer = 0
    tt = _pick_tile(T, 512)

    c_pad = jnp.pad(c, ((0, 8 - B), (0, 0)))
    mod = _ada(c_pad, w_ada[layer], b_ada[layer].reshape(1, -1))[:B]
    sh1, sc1, g1, sh2, sc2, g2 = (m.reshape(B, 1, D) for m in jnp.split(mod, 6, axis=-1))

    w = w_in[layer]
    qkv_w = 3 * GW
    sizes = (GW, HEADS, HEADS, GW, GW, GW, GW)
    offs = [qkv_w]
    for s in sizes:
        offs.append(offs[-1] + s)
    seg = lambda i: w[:, offs[i]:offs[i + 1]]
    small = jnp.pad(jnp.concatenate([seg(1), seg(2)], axis=1), ((0, 0), (0, GATE_LANES - 2 * HEADS)))
    w_all = jnp.concatenate([w[:, :qkv_w], seg(0), seg(3), seg(4), seg(5), seg(6), small], axis=1).astype(BF16)
    lane_pad = lambda v: jnp.pad(v.astype(F32).reshape(1, HEADS), ((0, 0), (HEADS, GATE_LANES - 2 * HEADS)))
    idx = jnp.arange(CUMSUM_ROWS)
    tri = ((idx[:, None] >= idx[None, :]) & (idx[:, None] // CHUNK == idx[None, :] // CHUNK)).astype(BF16)

    qa, ka, va, ga, sm, bcum, kb, ib, qb, gb = _inproj(
        layer, x, norm1_w[layer].reshape(1, D), sc1, sh1, w_all, conv_w[layer].astype(F32),
        lane_pad(gdn_a_log[layer]), lane_pad(gdn_dt_bias[layer]), hg_lb.astype(F32), tri, _pick_tile(T, 1024))

    gct = sm[:, :, HEADS:2 * HEADS].transpose(0, 2, 1).reshape(B, HEADS, T // CHUNK, CHUNK)
    oa = _gdn(qa, ka, va, ga, sm, gct, gdn_norm_w[layer].reshape(1, DH), tt)
    ob = _hgrn(qb, kb, ib, bcum, gb, hg_norm_w[layer].reshape(1, GW), tt)

    wo = w_out[layer].astype(BF16)
    x1, h2, eidx, rank, wts, cnt = _outproj(oa, ob, x, wo[:GW], wo[GW:], g1, norm2_w[layer].reshape(1, D), sc2, sh2,
                                            w_router[layer].T, router_bias[layer].reshape(N_EXPERTS, 1),
                                            _pick_tile(T, 1024))

    counts = cnt[:, 0].astype(jnp.int32)
    padded = (counts + MOE_ROWS - 1) // MOE_ROWS * MOE_ROWS
    ends = jnp.cumsum(padded)
    offsets = ends - padded
    n_tiles = (M * TOP_K) // MOE_ROWS + N_EXPERTS
    n_used = (ends[-1] // MOE_ROWS).astype(jnp.int32)
    tile_ids = jnp.minimum(jnp.arange(n_tiles + FFN_PIPELINE_STEPS, dtype=jnp.int32), n_used - 1)
    tile_expert = jnp.sum(((ends // MOE_ROWS)[None, :] <= tile_ids[:, None]).astype(jnp.int32), axis=1)
    tile_expert = jnp.minimum(tile_expert, N_EXPERTS - 1)
    slot = _slots(offsets.astype(jnp.int32), eidx, rank)
    first_tile = jnp.take(offsets, tile_expert) // MOE_ROWS
    n_valid = jnp.clip(jnp.take(counts, tile_expert) - (tile_ids - first_tile) * MOE_ROWS, 0, MOE_ROWS)

    assert D == ROW_TILE * LANES
    xs = _dispatch(slot, h2, n_tiles * MOE_ROWS, _pick_tile(M, 1024))
    ys = _ffn(tile_expert.astype(jnp.int32), n_used.reshape(1), n_valid.astype(jnp.int32), xs,
              w_gate[layer], w_up[layer], w_down[layer])
    wsgu = jnp.concatenate([ws_gate[layer], ws_up[layer]], axis=-1).astype(BF16)
    out = _combine(slot, wts.reshape(M, GATE_LANES), h2, x1.reshape(M, D), g2, final_norm_w.reshape(1, D),
                   wsgu, ws_down[layer].astype(BF16), ys, T, _pick_tile(T, 256))
    return out.reshape(B, T, D)
```

```python
import functools

import jax
import jax.numpy as jnp
from jax import lax
from jax.experimental import pallas as pl
from jax.experimental.pallas import tpu as pltpu

F32 = jnp.float32
BF16 = jnp.bfloat16

EPS = 1e-6
CHUNK = 64
SUB = 8
HEADS = 4
DH = 128
GW = HEADS * DH
CONV_K = 4
N_EXPERTS = 64
N_GROUPS = 8
GROUP_SIZE = N_EXPERTS // N_GROUPS
TOPK_GROUPS = 4
TOP_K = 8
D_EXPERT = 256
ROUTE_SCALE = 2.5
GATE_LANES = 128
GDN_CHUNKS_PER_ITER = 8
INV_BLOCK = 16
CUMSUM_ROWS = 128
MOE_ROWS = 1024
ROW_TILE, LANES = 8, 128

VMEM_LIMIT_BYTES = 56 * 1024 * 1024

ACT = BF16
ROW_DTYPE = F32


def _silu(x):
    return x * jax.nn.sigmoid(x)


def _dot(a, b):
    return jnp.dot(a.astype(BF16), b.astype(BF16), preferred_element_type=F32)


def _dot_nt(a, b):
    return lax.dot_general(a.astype(BF16), b.astype(BF16), (((1,), (1,)), ((), ())),
                           preferred_element_type=F32)


def _dot_tn(a, b):
    return lax.dot_general(a.astype(BF16), b.astype(BF16), (((0,), (0,)), ((), ())),
                           preferred_element_type=F32)


def _split2(x):
    hi = x.astype(BF16)
    lo = (x - hi.astype(F32)).astype(BF16)
    return hi, lo


def _dot3(a, b, dot=_dot):
    ah, al = _split2(a)
    bh, bl = _split2(b)
    return dot(ah, bh) + dot(ah, bl) + dot(al, bh)


def _cumsum_rows(tri, x):
    hi = x.astype(BF16)
    r = x - hi.astype(F32)
    mid = r.astype(BF16)
    lo = (r - mid.astype(F32)).astype(BF16)
    g = tri.shape[0]
    groups = []
    for r0 in range(0, x.shape[0], g):
        rows = slice(r0, r0 + g)
        groups.append(jnp.dot(tri, hi[rows], preferred_element_type=F32)
                      + jnp.dot(tri, mid[rows], preferred_element_type=F32)
                      + jnp.dot(tri, lo[rows], preferred_element_type=F32))
    return jnp.concatenate(groups, axis=0)


def _lane_pick(tile, lane, idx):
    return jnp.sum(jnp.where(lane == idx, tile, 0.0), axis=1, keepdims=True)


def _rms(x):
    return x * lax.rsqrt(jnp.mean(x * x, axis=-1, keepdims=True) + EPS)


def _to_rows(x3):
    r = x3.shape[0]
    xt = jnp.swapaxes(x3.reshape(r // ROW_TILE, ROW_TILE, ROW_TILE, LANES), 1, 2)
    return jnp.concatenate([xt[:, s].reshape(r, LANES) for s in range(ROW_TILE)], axis=1)


def _to_row_tiles(x):
    r = x.shape[0]
    xt = jnp.stack([x[:, s * LANES:(s + 1) * LANES].reshape(r // ROW_TILE, ROW_TILE, LANES) for s in range(ROW_TILE)],
                   axis=1)
    return jnp.swapaxes(xt, 1, 2).reshape(r, ROW_TILE, LANES)


def _ada_kernel(c_ref, w_ref, b_ref, o_ref):
    ca = _silu(c_ref[...])
    o_ref[...] = _dot3(ca, w_ref[...]) + b_ref[...]


def _ada(c_pad, w, b):
    rows, d = c_pad.shape
    n = w.shape[1]
    tn = 1024
    return pl.pallas_call(
        _ada_kernel,
        grid=(n // tn,),
        in_specs=[pl.BlockSpec((rows, d), lambda j: (0, 0)),
                  pl.BlockSpec((d, tn), lambda j: (0, j)),
                  pl.BlockSpec((1, tn), lambda j: (0, j))],
        out_specs=pl.BlockSpec((rows, tn), lambda j: (0, j)),
        out_shape=jax.ShapeDtypeStruct((rows, n), F32),
        compiler_params=pltpu.CompilerParams(dimension_semantics=("arbitrary",),
                                             vmem_limit_bytes=VMEM_LIMIT_BYTES),
        name="ada",
    )(c_pad, w, b)


def _inproj_kernel(layer, x_ref, n1_ref, sc_ref, sh_ref, w_ref, cw_ref, alog_ref, dt_ref, lb_ref, tri_ref,
                   qa_ref, ka_ref, va_ref, ga_ref, sm_ref, b_ref, kb_ref, ib_ref, qb_ref, gb_ref,
                   pbuf):
    tt = x_ref.shape[1]
    t = pl.program_id(1)

    h = _rms(x_ref[0]) * n1_ref[...]
    h = h * (1.0 + sc_ref[0]) + sh_ref[0]
    hb = h.astype(BF16)

    def proj(g, width=GW):
        return jnp.dot(hb, w_ref[:, g * GW:g * GW + width], preferred_element_type=F32)

    @pl.when(t == 0)
    def _():
        pbuf[:, 0:8, :] = jnp.zeros((3, 8, GW), F32)

    for g, out_ref in enumerate((qa_ref, ka_ref, va_ref)):
        cols = slice(g * GW, (g + 1) * GW)
        p = proj(g)
        pbuf[g, 8:8 + tt, :] = p
        y = p * cw_ref[CONV_K - 1:CONV_K, cols]
        for j in range(1, CONV_K):
            y = y + pbuf[g, 8 - j:8 - j + tt, :] * cw_ref[CONV_K - 1 - j:CONV_K - j, cols]
        pbuf[g, 0:8, :] = pbuf[g, tt:tt + 8, :]
        y = _silu(y)
        if g == 2:
            out_ref[0] = y.astype(out_ref.dtype)
        else:
            scale = DH ** -0.5 if g == 0 else 1.0
            for hh in range(HEADS):
                hs = slice(hh * DH, (hh + 1) * DH)
                yh = y[:, hs]
                inv = lax.rsqrt(jnp.sum(yh * yh, axis=-1, keepdims=True) + EPS)
                out_ref[0, :, hs] = (yh * inv * scale).astype(out_ref.dtype)

    ga_ref[0] = _silu(proj(3)).astype(ga_ref.dtype)

    ps = proj(8, GATE_LANES)
    lane = lax.broadcasted_iota(jnp.int32, ps.shape, 1)
    beta = jax.nn.sigmoid(ps)
    z = ps + dt_ref[...]
    softplus = jnp.maximum(z, 0.0) + jnp.log1p(jnp.exp(-jnp.abs(z)))
    g_log = -jnp.exp(alog_ref[...]) * softplus
    tri = tri_ref[...]
    gc = _cumsum_rows(tri, jnp.where((lane >= HEADS) & (lane < 2 * HEADS), g_log, 0.0))
    sm_ref[0] = jnp.where(lane < HEADS, beta, gc)

    hl = lb_ref[...]
    e = jnp.exp(hl - jnp.max(hl, axis=0, keepdims=True))
    lb = jnp.sum(e[0:layer + 1], axis=0, keepdims=True) / jnp.sum(e, axis=0, keepdims=True)
    fr = proj(4)
    logf = jnp.log(lb + (1.0 - lb) * jax.nn.sigmoid(fr))
    b_ref[0] = _cumsum_rows(tri, logf)
    kb_ref[0] = ((1.0 - lb) * jax.nn.sigmoid(-fr)).astype(kb_ref.dtype)
    ib_ref[0] = proj(5).astype(ib_ref.dtype)
    qb_ref[0] = _silu(proj(6)).astype(qb_ref.dtype)
    gb_ref[0] = _silu(proj(7)).astype(gb_ref.dtype)


def _inproj(layer, x, n1, sc1, sh1, w_all, conv_w, alog_pad, dt_pad, hg_lb, tri, tt):
    B, T, D = x.shape
    const = lambda shape: pl.BlockSpec(shape, lambda b, t: (0,) * len(shape), pipeline_mode=pl.Buffered(1))
    act = lambda dt: jax.ShapeDtypeStruct((B, T, GW), dt)
    tile = lambda w: pl.BlockSpec((1, tt, w), lambda b, t: (b, t, 0))
    per_batch = pl.BlockSpec((1, 1, D), lambda b, t: (b, 0, 0))
    return pl.pallas_call(
        functools.partial(_inproj_kernel, layer),
        grid=(B, T // tt),
        in_specs=[tile(D), const((1, D)), per_batch, per_batch,
                  const(w_all.shape), const(conv_w.shape), const((1, GATE_LANES)), const((1, GATE_LANES)),
                  const(hg_lb.shape), const(tri.shape)],
        out_specs=[tile(GW), tile(GW), tile(GW), tile(GW), tile(GATE_LANES), tile(GW),
                   tile(GW), tile(GW), tile(GW), tile(GW)],
        out_shape=[act(ACT), act(ACT), act(ACT), act(ACT),
                   jax.ShapeDtypeStruct((B, T, GATE_LANES), F32), act(F32),
                   act(ACT), act(ACT), act(ACT), act(ACT)],
        scratch_shapes=[pltpu.VMEM((3, tt + 8, GW), F32)],
        compiler_params=pltpu.CompilerParams(dimension_semantics=("arbitrary", "arbitrary"),
                                             vmem_limit_bytes=VMEM_LIMIT_BYTES),
        name="inproj",
    )(x, n1, sc1, sh1, w_all, conv_w, alog_pad, dt_pad, hg_lb, tri)


def _gdn_prep_kernel(q_ref, k_ref, v_ref, sm_ref, gct_ref, o_ref, qt_ref, m_ref, n_ref):
    tt = q_ref.shape[1]
    nc = tt // CHUNK
    row = lax.broadcasted_iota(jnp.int32, (CHUNK, CHUNK), 0)
    col = lax.broadcasted_iota(jnp.int32, (CHUNK, CHUNK), 1)
    causal = row >= col
    diag_blk = (row > col) & (row // INV_BLOCK == col // INV_BLOCK)
    off_blk = row // INV_BLOCK > col // INV_BLOCK
    eye = jnp.where(row == col, 1.0, 0.0)
    lane = lax.broadcasted_iota(jnp.int32, (CHUNK, GATE_LANES), 1)
    assert INV_BLOCK == 16 and CHUNK == 4 * INV_BLOCK

    def body(i, carry):
        chains = [(GDN_CHUNKS_PER_ITER * i + j, hh) for j in range(GDN_CHUNKS_PER_ITER) for hh in range(HEADS)]
        rows = [pl.ds(pl.multiple_of(c * CHUNK, CHUNK), CHUNK) for c, _ in chains]
        hs = [slice(hh * DH, (hh + 1) * DH) for _, hh in chains]
        n = range(len(chains))
        sm = [sm_ref[0, rows[j], :] for j in n]
        q = [q_ref[0, rows[j], hs[j]].astype(F32) for j in n]
        k = [k_ref[0, rows[j], hs[j]].astype(F32) for j in n]
        v = [v_ref[0, rows[j], hs[j]].astype(F32) for j in n]
        beta = [_lane_pick(sm[j], lane, chains[j][1]) for j in n]
        gcol = [_lane_pick(sm[j], lane, HEADS + chains[j][1]) for j in n]
        grow = [gct_ref[0, hh, pl.ds(c, 1), :] for c, hh in chains]
        decay = [jnp.exp(jnp.where(causal, gcol[j] - grow[j], -jnp.inf)) for j in n]
        kb = [k[j] * beta[j] for j in n]
        L = [_dot_nt(kb[j], k[j]) * decay[j] for j in n]
        dg = [jnp.where(diag_blk, L[j], 0.0) for j in n]
        off = [jnp.where(off_blk, L[j], 0.0) for j in n]
        dinv = [eye - dg[j] for j in n]
        pw = [_dot3(dg[j], dg[j]) for j in n]
        for _ in range(2):
            dinv = [dinv[j] + _dot3(dinv[j], pw[j]) for j in n]
            pw = [_dot3(pw[j], pw[j]) for j in n]
        dinv = [dinv[j] + _dot3(dinv[j], pw[j]) for j in n]
        f1 = [_dot(dinv[j], off[j]) for j in n]
        f2 = [_dot(f1[j], f1[j]) for j in n]
        f3 = [_dot(f1[j], f2[j]) for j in n]
        tinv = [_dot(eye - f1[j] + f2[j] - f3[j], dinv[j]) for j in n]
        eg = [jnp.exp(gcol[j]) for j in n]
        sol = [_dot(tinv[j], jnp.concatenate([v[j] * beta[j], kb[j] * eg[j]], axis=1)) for j in n]
        attn = [_dot_nt(q[j], k[j]) * decay[j] for j in n]
        k_tail = [k[j] * jnp.exp(gcol[j][CHUNK - 1:CHUNK, :] - gcol[j]) for j in n]
        au = [_dot(attn[j], sol[j]) for j in n]
        ku = [_dot_tn(k_tail[j], sol[j]) for j in n]
        for j, (c, hh) in enumerate(chains):
            o_ref[0, rows[j], hs[j]] = au[j][:, :DH]
            qt_ref[0, rows[j], hs[j]] = (q[j] * eg[j] - au[j][:, DH:]).astype(qt_ref.dtype)
            n_ref[0, hh, c] = ku[j][:, :DH].astype(n_ref.dtype)
            m_ref[0, hh, c] = (-ku[j][:, DH:]).astype(m_ref.dtype)
        return carry

    lax.fori_loop(0, nc // GDN_CHUNKS_PER_ITER, body, 0)


def _gdn_scan_kernel(o_ref, qt_ref, m_ref, n_ref, gct_ref, sg_ref, nw_ref, out_ref, s_ref):
    nb, tt = o_ref.shape[0], o_ref.shape[1]
    nc = tt // CHUNK

    @pl.when(pl.program_id(0) == 0)
    def _():
        s_ref[...] = jnp.zeros(s_ref.shape, F32)

    nw = nw_ref[...]

    def body(c, carry):
        rows = pl.ds(pl.multiple_of(c * CHUNK, CHUNK), CHUNK)
        for b in range(nb):
            for hh in range(HEADS):
                hs = slice(hh * DH, (hh + 1) * DH)
                S = s_ref[b, hh]
                Sb = S.astype(BF16)
                glast = gct_ref[b, hh, pl.ds(c, 1), :][:, CHUNK - 1:CHUNK]
                o = o_ref[b, rows, hs] + jnp.dot(qt_ref[b, rows, hs], Sb, preferred_element_type=F32)
                s_ref[b, hh] = (S * jnp.exp(glast) + jnp.dot(m_ref[b, hh, c], Sb, preferred_element_type=F32)
                                + n_ref[b, hh, c].astype(F32))
                o = _rms(o) * nw * sg_ref[b, rows, hs].astype(F32)
                out_ref[b, rows, hs] = o.astype(out_ref.dtype)
        return carry

    lax.fori_loop(0, nc, body, 0)


def _gdn(q, k, v, sg, sm, gct, nw, tt):
    B, T, _ = q.shape
    nc = tt // CHUNK
    assert nc % GDN_CHUNKS_PER_ITER == 0
    n_chunks = T // CHUNK
    tile = lambda w: pl.BlockSpec((1, tt, w), lambda b, t: (b, t, 0))
    mat = jax.ShapeDtypeStruct((B, HEADS, n_chunks, DH, DH), ACT)
    o_part, qt, m, n = pl.pallas_call(
        _gdn_prep_kernel,
        grid=(B, T // tt),
        in_specs=[tile(GW), tile(GW), tile(GW), tile(GATE_LANES),
                  pl.BlockSpec((1, HEADS, nc, CHUNK), lambda b, t: (b, 0, t, 0))],
        out_specs=[tile(GW), tile(GW),
                   pl.BlockSpec((1, HEADS, nc, DH, DH), lambda b, t: (b, 0, t, 0, 0)),
                   pl.BlockSpec((1, HEADS, nc, DH, DH), lambda b, t: (b, 0, t, 0, 0))],
        out_shape=[jax.ShapeDtypeStruct((B, T, GW), F32), jax.ShapeDtypeStruct((B, T, GW), ACT), mat, mat],
        compiler_params=pltpu.CompilerParams(dimension_semantics=("arbitrary", "arbitrary"),
                                             vmem_limit_bytes=VMEM_LIMIT_BYTES),
        name="gdn_prep",
    )(q, k, v, sm, gct)

    full = lambda w: pl.BlockSpec((B, tt, w), lambda t: (0, t, 0))
    mats = pl.BlockSpec((B, HEADS, nc, DH, DH), lambda t: (0, 0, t, 0, 0))
    return pl.pallas_call(
        _gdn_scan_kernel,
        grid=(T // tt,),
        in_specs=[full(GW), full(GW), mats, mats,
                  pl.BlockSpec((B, HEADS, nc, CHUNK), lambda t: (0, 0, t, 0)),
                  full(GW), pl.BlockSpec((1, DH), lambda t: (0, 0))],
        out_specs=full(GW),
        out_shape=jax.ShapeDtypeStruct((B, T, GW), ACT),
        scratch_shapes=[pltpu.VMEM((B, HEADS, DH, DH), F32)],
        compiler_params=pltpu.CompilerParams(dimension_semantics=("arbitrary",),
                                             vmem_limit_bytes=VMEM_LIMIT_BYTES),
        name="gdn_scan",
    )(o_part, qt, m, n, gct, sg, nw)


def _hgrn_kernel(q_ref, k_ref, v_ref, b_ref, sg_ref, nw_ref, o_ref, st_ref):
    nb, tt = q_ref.shape[0], q_ref.shape[1]
    nc = tt // CHUNK

    @pl.when(pl.program_id(0) == 0)
    def _():
        st_ref[...] = jnp.zeros(st_ref.shape, F32)

    row = lax.broadcasted_iota(jnp.int32, (CHUNK, CHUNK), 0)
    col = lax.broadcasted_iota(jnp.int32, (CHUNK, CHUNK), 1)
    diag_block = ((col // SUB) == (row // SUB)) & (col <= row)
    chains = [(bi, hh) for bi in range(nb) for hh in range(HEADS)]
    n = range(len(chains))
    hs = [slice(hh * DH, (hh + 1) * DH) for _, hh in chains]

    def body(c, carry):
        rows = pl.ds(pl.multiple_of(c * CHUNK, CHUNK), CHUNK)
        q = [q_ref[bi, rows, hs[j]].astype(F32) for j, (bi, _) in enumerate(chains)]
        k = [k_ref[bi, rows, hs[j]].astype(F32) for j, (bi, _) in enumerate(chains)]
        v = [v_ref[bi, rows, hs[j]].astype(F32) for j, (bi, _) in enumerate(chains)]
        b = [b_ref[bi, rows, hs[j]] for j, (bi, _) in enumerate(chains)]
        blast = [b[j][CHUNK - 1:CHUNK, :] for j in n]
        st = [st_ref[bi, hh] for bi, hh in chains]
        o = [_dot_nt(q[j] * jnp.exp(b[j]), st[j]) for j in n]
        k_tail = [k[j] * jnp.exp(blast[j] - b[j]) for j in n]
        for j, (bi, hh) in enumerate(chains):
            st_ref[bi, hh] = st[j] * jnp.exp(blast[j]) + _dot_tn(v[j], k_tail[j])

        blocks = [[jnp.zeros((SUB, CHUNK), F32)] for _ in n]
        for i in range(1, CHUNK // SUB):
            lo, hi = i * SUB, (i + 1) * SUB
            for j in n:
                r = b[j][lo:lo + 1, :]
                qi = q[j][lo:hi] * jnp.exp(b[j][lo:hi] - r)
                kj = k[j][:lo] * jnp.exp(jnp.minimum(r - b[j][:lo], 0.0))
                kj = jnp.concatenate([kj, jnp.zeros((CHUNK - lo, DH), F32)], axis=0)
                blocks[j].append(_dot_nt(qi, kj))
        a = []
        for j in n:
            f = jnp.exp(jnp.minimum(b[j] - pltpu.roll(b[j], 1, 0), 0.0))
            e = None
            a_diag = jnp.zeros((CHUNK, CHUNK), F32)
            for delta in range(SUB):
                if delta == 0:
                    term = q[j] * k[j]
                else:
                    fsh = f if delta == 1 else pltpu.roll(f, delta - 1, 0)
                    e = fsh if e is None else e * fsh
                    term = q[j] * pltpu.roll(k[j], delta, 0) * e
                colv = jnp.sum(term, axis=1, keepdims=True)
                a_diag = jnp.where(row - col == delta, colv, a_diag)
            a.append(jnp.where(diag_block, a_diag, jnp.concatenate(blocks[j], axis=0)))

        o = [o[j] + _dot(a[j], v[j]) for j in n]
        for bi in range(nb):
            ob = jnp.concatenate(o[bi * HEADS:(bi + 1) * HEADS], axis=1)
            ob = _rms(ob) * nw_ref[...] * sg_ref[bi, rows, :].astype(F32)
            o_ref[bi, rows, :] = ob.astype(o_ref.dtype)
        return carry

    lax.fori_loop(0, nc, body, 0)


def _hgrn(q, k, v, b, sg, nw, tt):
    B, T, _ = q.shape
    tile = pl.BlockSpec((B, tt, GW), lambda t: (0, t, 0))
    return pl.pallas_call(
        _hgrn_kernel,
        grid=(T // tt,),
        in_specs=[tile, tile, tile, tile, tile, pl.BlockSpec((1, GW), lambda t: (0, 0))],
        out_specs=tile,
        out_shape=jax.ShapeDtypeStruct((B, T, GW), ACT),
        scratch_shapes=[pltpu.VMEM((B, HEADS, DH, DH), F32)],
        compiler_params=pltpu.CompilerParams(dimension_semantics=("arbitrary",),
                                             vmem_limit_bytes=VMEM_LIMIT_BYTES),
        name="hgrn",
    )(q, k, v, b, sg, nw)


def _outproj_kernel(oa_ref, ob_ref, x_ref, wa_ref, wb_ref, g1_ref, n2_ref, sc_ref, sh_ref, wr_ref, rb_ref,
                    x1_ref, h2_ref, eidx_ref, rank_ref, wts_ref, cnt_ref):
    tm = x_ref.shape[1]

    @pl.when((pl.program_id(0) == 0) & (pl.program_id(1) == 0))
    def _():
        cnt_ref[...] = jnp.zeros(cnt_ref.shape, F32)

    mix = (jnp.dot(oa_ref[0], wa_ref[...], preferred_element_type=F32)
           + jnp.dot(ob_ref[0], wb_ref[...], preferred_element_type=F32))
    x1 = x_ref[0] + g1_ref[0] * mix
    x1_ref[0] = x1
    h2 = _rms(x1) * n2_ref[...]
    h2 = h2 * (1.0 + sc_ref[0]) + sh_ref[0]
    h2_ref[...] = _to_row_tiles(h2).astype(h2_ref.dtype)

    scores = jax.nn.sigmoid(_dot3(wr_ref[...], h2, dot=_dot_nt))
    sel = scores + rb_ref[...]
    sub = lax.broadcasted_iota(jnp.int32, (GROUP_SIZE, tm), 0)
    neg = -jnp.inf
    groups = range(N_GROUPS)

    def take_max(blk):
        m = jnp.max(blk, axis=0, keepdims=True)
        first = jnp.min(jnp.where(blk == m, sub, GROUP_SIZE), axis=0, keepdims=True)
        hit = sub == first
        return m, hit, jnp.where(hit, neg, blk)

    blk_of = lambda a, g: a[g * GROUP_SIZE:(g + 1) * GROUP_SIZE]
    sel_blk = [blk_of(sel, g) for g in groups]
    group_score = jnp.zeros((N_GROUPS, tm), F32)
    for g in groups:
        m1, _, rest = take_max(sel_blk[g])
        m2 = jnp.max(rest, axis=0, keepdims=True)
        group_score = jnp.where(sub == g, m1 + m2, group_score)
    group_on = jnp.zeros((N_GROUPS, tm), F32)
    for _ in range(TOPK_GROUPS):
        _, hit, group_score = take_max(group_score)
        group_on = jnp.where(hit, 1.0, group_on)

    cand = [jnp.where(group_on[g:g + 1] > 0.0, sel_blk[g], neg) for g in groups]
    picked = [jnp.zeros((GROUP_SIZE, tm), F32) for _ in groups]
    chosen = []
    for _ in range(TOP_K):
        m = jnp.max(functools.reduce(jnp.maximum, cand), axis=0, keepdims=True)
        first = functools.reduce(jnp.minimum, [jnp.where(cand[g] == m, sub + g * GROUP_SIZE, N_EXPERTS)
                                               for g in groups])
        first = jnp.min(first, axis=0, keepdims=True)
        chosen.append(first)
        for g in groups:
            hit = (sub + g * GROUP_SIZE) == first
            picked[g] = jnp.where(hit, 1.0, picked[g])
            cand[g] = jnp.where(hit, neg, cand[g])

    picked_all = jnp.concatenate(picked, axis=0)
    r_i = lax.broadcasted_iota(jnp.int32, (tm, tm), 0)
    c_i = lax.broadcasted_iota(jnp.int32, (tm, tm), 1)
    earlier = jnp.where(r_i < c_i, 1.0, 0.0).astype(BF16)
    before = jnp.dot(picked_all.astype(BF16), earlier, preferred_element_type=F32) + cnt_ref[:, 0:1]
    cnt_ref[...] = cnt_ref[...] + jnp.sum(picked_all, axis=1, keepdims=True)

    def pick_value(table, first):
        parts = [jnp.where((sub + g * GROUP_SIZE) == first, blk_of(table, g), 0.0) for g in groups]
        return jnp.sum(functools.reduce(jnp.add, parts), axis=0, keepdims=True)

    w_k = [pick_value(scores, f) for f in chosen]
    denom = functools.reduce(jnp.add, w_k)
    eidx = jnp.zeros((TOP_K, tm), jnp.int32)
    rank = jnp.zeros((TOP_K, tm), jnp.int32)
    wts = jnp.zeros((TOP_K, tm), F32)
    for k in range(TOP_K):
        eidx = jnp.where(sub == k, chosen[k], eidx)
        rank = jnp.where(sub == k, pick_value(before, chosen[k]).astype(jnp.int32), rank)
        wts = jnp.where(sub == k, w_k[k] / denom * ROUTE_SCALE, wts)
    eidx_ref[...] = eidx
    rank_ref[...] = rank
    pad = jnp.zeros((GATE_LANES - TOP_K, tm), F32)
    wts_ref[0] = jnp.concatenate([wts, pad], axis=0).T


def _outproj(oa, ob, x, wa, wb, g1, n2, sc2, sh2, wr_t, rb, tm):
    B, T, D = x.shape
    nt = T // tm
    const = lambda shape: pl.BlockSpec(shape, lambda b, t: (0,) * len(shape))
    tile = lambda w: pl.BlockSpec((1, tm, w), lambda b, t: (b, t, 0))
    per_batch = pl.BlockSpec((1, 1, D), lambda b, t: (b, 0, 0))
    picks = pl.BlockSpec((TOP_K, tm), lambda b, t: (0, b * nt + t))
    return pl.pallas_call(
        _outproj_kernel,
        grid=(B, nt),
        in_specs=[tile(GW), tile(GW), tile(D), const(wa.shape), const(wb.shape), per_batch,
                  const((1, D)), per_batch, per_batch, const(wr_t.shape), const(rb.shape)],
        out_specs=[tile(D), pl.BlockSpec((tm, ROW_TILE, LANES), lambda b, t: (b * nt + t, 0, 0)),
                   picks, picks, tile(GATE_LANES), const((N_EXPERTS, GATE_LANES))],
        out_shape=[jax.ShapeDtypeStruct((B, T, D), F32), jax.ShapeDtypeStruct((B * T, ROW_TILE, LANES), ROW_DTYPE),
                   jax.ShapeDtypeStruct((TOP_K, B * T), jnp.int32), jax.ShapeDtypeStruct((TOP_K, B * T), jnp.int32),
                   jax.ShapeDtypeStruct((B, T, GATE_LANES), F32),
                   jax.ShapeDtypeStruct((N_EXPERTS, GATE_LANES), F32)],
        compiler_params=pltpu.CompilerParams(dimension_semantics=("arbitrary", "arbitrary"),
                                             vmem_limit_bytes=VMEM_LIMIT_BYTES),
        name="outproj",
    )(oa, ob, x, wa, wb, g1, n2, sc2, sh2, wr_t, rb)


def _slots_kernel(off_ref, eidx_ref, rank_ref, slot_ref):
    eidx = eidx_ref[...]

    def add_expert(e, acc):
        return acc + jnp.where(eidx == e, off_ref[e], 0)

    slot_ref[...] = lax.fori_loop(0, N_EXPERTS, add_expert, rank_ref[...])


def _slots(offsets, eidx, rank):
    k, m = eidx.shape
    tile = pl.BlockSpec((k, m), lambda i: (0, 0))
    return pl.pallas_call(
        _slots_kernel,
        grid=(1,),
        in_specs=[pl.BlockSpec(memory_space=pltpu.SMEM), tile, tile],
        out_specs=tile,
        out_shape=jax.ShapeDtypeStruct((k, m), jnp.int32),
        compiler_params=pltpu.CompilerParams(dimension_semantics=("arbitrary",),
                                             vmem_limit_bytes=VMEM_LIMIT_BYTES),
        name="slots",
    )(offsets, eidx, rank)


def _dispatch_kernel(slot_ref, h_ref, xs_ref, sem):
    tmd = h_ref.shape[0]

    def start_rows(j, c):
        for k in range(TOP_K):
            pltpu.make_async_copy(h_ref.at[pl.ds(j, 1)], xs_ref.at[pl.ds(slot_ref[k, j], 1)], sem).start(
                priority=k % 2)
        return c

    lax.fori_loop(0, tmd, start_rows, 0)
    for _ in range(TOP_K):
        pltpu.make_async_copy(h_ref, xs_ref.at[pl.ds(0, tmd)], sem).wait()


def _dispatch(slot, h2, n_rows, tmd):
    M = h2.shape[0]
    return pl.pallas_call(
        _dispatch_kernel,
        grid=(M // tmd,),
        in_specs=[pl.BlockSpec((TOP_K, tmd), lambda i: (0, i), memory_space=pltpu.SMEM),
                  pl.BlockSpec((tmd, ROW_TILE, LANES), lambda i: (i, 0, 0))],
        out_specs=pl.BlockSpec(memory_space=pl.ANY),
        out_shape=jax.ShapeDtypeStruct((n_rows, ROW_TILE, LANES), h2.dtype),
        scratch_shapes=[pltpu.SemaphoreType.DMA],
        compiler_params=pltpu.CompilerParams(dimension_semantics=("arbitrary",),
                                             vmem_limit_bytes=VMEM_LIMIT_BYTES),
        name="dispatch",
    )(slot, h2)


FFN_PIPELINE_STEPS = 2


def _ffn_kernel(te_ref, nu_ref, nv_ref, x_ref, wg_ref, wu_ref, wd_ref, y_ref, wgu_s, wd_s, xstd, ystd):
    i = pl.program_id(0)
    row = lax.broadcasted_iota(jnp.int32, (MOE_ROWS, 1), 0)
    tile_mm = jnp.maximum(i - 1, 0)
    tile_out = jnp.maximum(i - 2, 0)

    @pl.when(i == 0)
    def _():
        xstd[...] = jnp.zeros(xstd.shape, BF16)
        ystd[...] = jnp.zeros(ystd.shape, F32)

    @pl.when(i < nu_ref[0] + FFN_PIPELINE_STEPS)
    def _():
        @pl.when((i == 0) | (te_ref[tile_mm] != te_ref[tile_out]))
        def _():
            wgu_s[:, :D_EXPERT] = wg_ref[0].astype(BF16)
            wgu_s[:, D_EXPERT:] = wu_ref[0].astype(BF16)
            wd_s[...] = wd_ref[0].astype(BF16)

        cur = i % 2
        xstd[cur] = jnp.where(row < nv_ref[i], _to_rows(x_ref[...].astype(F32)), 0.0).astype(BF16)
        y_ref[...] = _to_row_tiles(ystd[cur]).astype(y_ref.dtype)
        gu = jnp.dot(xstd[1 - cur], wgu_s[...], preferred_element_type=F32)
        act = _silu(gu[:, :D_EXPERT]) * gu[:, D_EXPERT:]
        ystd[1 - cur] = jnp.dot(act.astype(BF16), wd_s[...], preferred_element_type=F32)


def _ffn(tile_expert, n_used, n_valid, xs, wg, wu, wd):
    D = wg.shape[1]
    n_tiles = xs.shape[0] // MOE_ROWS
    block = (MOE_ROWS, ROW_TILE, LANES)
    expert = lambda i, te, nu, nv: (te[jnp.maximum(i - 1, 0)], 0, 0)
    return pl.pallas_call(
        _ffn_kernel,
        grid_spec=pltpu.PrefetchScalarGridSpec(
            num_scalar_prefetch=3,
            grid=(n_tiles + FFN_PIPELINE_STEPS,),
            in_specs=[pl.BlockSpec(block, lambda i, te, nu, nv: (jnp.minimum(i, nu[0] - 1), 0, 0)),
                      pl.BlockSpec((1, D, D_EXPERT), expert),
                      pl.BlockSpec((1, D, D_EXPERT), expert),
                      pl.BlockSpec((1, D_EXPERT, D), expert)],
            out_specs=pl.BlockSpec(block, lambda i, te, nu, nv: (jnp.clip(i - 2, 0, nu[0] - 1), 0, 0)),
            scratch_shapes=[pltpu.VMEM((D, 2 * D_EXPERT), BF16), pltpu.VMEM((D_EXPERT, D), BF16),
                            pltpu.VMEM((2, MOE_ROWS, D), BF16), pltpu.VMEM((2, MOE_ROWS, D), F32)]),
        out_shape=jax.ShapeDtypeStruct(xs.shape, xs.dtype),
        compiler_params=pltpu.CompilerParams(dimension_semantics=("arbitrary",),
                                             vmem_limit_bytes=VMEM_LIMIT_BYTES),
        name="ffn",
    )(tile_expert, n_used, n_valid, xs, wg, wu, wd)


def _combine_kernel(slot_ref, next_slot_ref, w_ref, h_ref, x1_ref, g2_ref, fw_ref, wsgu_ref, wsd_ref, y_ref,
                    o_ref, buf, sem):
    tmc = h_ref.shape[0]
    i = pl.program_id(0)
    last = pl.num_programs(0) - 1

    def start_rows(slots, b):
        def body(j, c):
            for k in range(TOP_K):
                pltpu.make_async_copy(y_ref.at[pl.ds(slots[k, j], 1)], buf.at[b, k, pl.ds(j, 1)], sem.at[b]).start(
                    priority=k % 2)
            return c
        lax.fori_loop(0, tmc, body, 0)

    def wait_rows(b):
        for k in range(TOP_K):
            pltpu.make_async_copy(y_ref.at[pl.ds(0, tmc)], buf.at[b, k], sem.at[b]).wait()

    def for_buffer(step, fn):
        for b in range(2):
            @pl.when(step % 2 == b)
            def _():
                fn(b)

    @pl.when(i == 0)
    def _():
        start_rows(slot_ref, 0)

    @pl.when(i < last)
    def _():
        for_buffer(i + 1, lambda b: start_rows(next_slot_ref, b))

    gu = jnp.dot(_to_rows(h_ref[...].astype(F32)).astype(BF16), wsgu_ref[...], preferred_element_type=F32)
    act = _silu(gu[:, :D_EXPERT]) * gu[:, D_EXPERT:]
    shared = jnp.dot(act.astype(BF16), wsd_ref[...], preferred_element_type=F32)
    w = w_ref[...]
    lane = lax.broadcasted_iota(jnp.int32, w.shape, 1)

    def finish(b):
        wait_rows(b)
        w3 = _to_row_tiles(jnp.concatenate(
            [jnp.broadcast_to(_lane_pick(w, lane, k), (tmc, LANES)) for k in range(TOP_K)], axis=1))
        routed = jnp.zeros((tmc, ROW_TILE, LANES), F32)
        for k in range(TOP_K):
            routed = routed + jnp.broadcast_to(w3[:, k:k + 1, :], w3.shape) * buf[b, k].astype(F32)
        acc = shared + _to_rows(routed)
        y = x1_ref[...] + g2_ref[0] * acc
        o_ref[...] = _rms(y) * fw_ref[...]

    for_buffer(i, finish)


def _combine(slot, wts, h2, x1, g2, fw, wsgu, wsd, ys, T, tmc):
    M, D = x1.shape
    const = lambda shape: pl.BlockSpec(shape, lambda i: (0,) * len(shape))
    tile = lambda w: pl.BlockSpec((tmc, w), lambda i: (i, 0))
    row_tiled = pl.BlockSpec((tmc, ROW_TILE, LANES), lambda i: (i, 0, 0))
    n_steps = M // tmc
    return pl.pallas_call(
        _combine_kernel,
        grid=(n_steps,),
        in_specs=[pl.BlockSpec((TOP_K, tmc), lambda i: (0, i), memory_space=pltpu.SMEM),
                  pl.BlockSpec((TOP_K, tmc), lambda i: (0, jnp.minimum(i + 1, n_steps - 1)), memory_space=pltpu.SMEM),
                  tile(GATE_LANES), row_tiled, tile(D),
                  pl.BlockSpec((1, 1, D), lambda i: (i // (T // tmc), 0, 0)),
                  const((1, D)), const(wsgu.shape), const(wsd.shape),
                  pl.BlockSpec(memory_space=pl.ANY)],
        out_specs=tile(D),
        out_shape=jax.ShapeDtypeStruct((M, D), F32),
        scratch_shapes=[pltpu.VMEM((2, TOP_K, tmc, ROW_TILE, LANES), ys.dtype), pltpu.SemaphoreType.DMA((2,))],
        compiler_params=pltpu.CompilerParams(dimension_semantics=("arbitrary",),
                                             vmem_limit_bytes=VMEM_LIMIT_BYTES),
        name="combine",
    )(slot, slot, wts, h2, x1, g2, fw, wsgu, wsd, ys)


def _pick_tile(n, want):
    t = min(n, want)
    assert n % t == 0 and t % CHUNK == 0, (n, want)
    return t


def kernel(x, c, w_ada, b_ada, norm1_w, w_in, conv_w, gdn_a_log, gdn_dt_bias, gdn_norm_w, hg_lb, hg_norm_w,
           w_out, norm2_w, w_router, router_bias, w_gate, w_up, w_down, ws_gate, ws_up, ws_down, final_norm_w):
    B, T, D = x.shape
    M = B * T
    depth = w_ada.shape[0]
    assert depth == 1 and T % CHUNK == 0 and B <= 8
    layer = 0
    tt = _pick_tile(T, 512)

    c_pad = jnp.pad(c, ((0, 8 - B), (0, 0)))
    mod = _ada(c_pad, w_ada[layer], b_ada[layer].reshape(1, -1))[:B]
    sh1, sc1, g1, sh2, sc2, g2 = (m.reshape(B, 1, D) for m in jnp.split(mod, 6, axis=-1))

    w = w_in[layer]
    qkv_w = 3 * GW
    sizes = (GW, HEADS, HEADS, GW, GW, GW, GW)
    offs = [qkv_w]
    for s in sizes:
        offs.append(offs[-1] + s)
    seg = lambda i: w[:, offs[i]:offs[i + 1]]
    small = jnp.pad(jnp.concatenate([seg(1), seg(2)], axis=1), ((0, 0), (0, GATE_LANES - 2 * HEADS)))
    w_all = jnp.concatenate([w[:, :qkv_w], seg(0), seg(3), seg(4), seg(5), seg(6), small], axis=1).astype(BF16)
    lane_pad = lambda v: jnp.pad(v.astype(F32).reshape(1, HEADS), ((0, 0), (HEADS, GATE_LANES - 2 * HEADS)))
    idx = jnp.arange(CUMSUM_ROWS)
    tri = ((idx[:, None] >= idx[None, :]) & (idx[:, None] // CHUNK == idx[None, :] // CHUNK)).astype(BF16)

    qa, ka, va, ga, sm, bcum, kb, ib, qb, gb = _inproj(
        layer, x, norm1_w[layer].reshape(1, D), sc1, sh1, w_all, conv_w[layer].astype(F32),
        lane_pad(gdn_a_log[layer]), lane_pad(gdn_dt_bias[layer]), hg_lb.astype(F32), tri, _pick_tile(T, 1024))

    gct = sm[:, :, HEADS:2 * HEADS].transpose(0, 2, 1).reshape(B, HEADS, T // CHUNK, CHUNK)
    oa = _gdn(qa, ka, va, ga, sm, gct, gdn_norm_w[layer].reshape(1, DH), tt)
    ob = _hgrn(qb, kb, ib, bcum, gb, hg_norm_w[layer].reshape(1, GW), tt)

    wo = w_out[layer].astype(BF16)
    x1, h2, eidx, rank, wts, cnt = _outproj(oa, ob, x, wo[:GW], wo[GW:], g1, norm2_w[layer].reshape(1, D), sc2, sh2,
                                            w_router[layer].T, router_bias[layer].reshape(N_EXPERTS, 1),
                                            _pick_tile(T, 1024))

    counts = cnt[:, 0].astype(jnp.int32)
    padded = (counts + MOE_ROWS - 1) // MOE_ROWS * MOE_ROWS
    ends = jnp.cumsum(padded)
    offsets = ends - padded
    n_tiles = (M * TOP_K) // MOE_ROWS + N_EXPERTS
    n_used = (ends[-1] // MOE_ROWS).astype(jnp.int32)
    tile_ids = jnp.minimum(jnp.arange(n_tiles + FFN_PIPELINE_STEPS, dtype=jnp.int32), n_used - 1)
    tile_expert = jnp.sum(((ends // MOE_ROWS)[None, :] <= tile_ids[:, None]).astype(jnp.int32), axis=1)
    tile_expert = jnp.minimum(tile_expert, N_EXPERTS - 1)
    slot = _slots(offsets.astype(jnp.int32), eidx, rank)
    first_tile = jnp.take(offsets, tile_expert) // MOE_ROWS
    n_valid = jnp.clip(jnp.take(counts, tile_expert) - (tile_ids - first_tile) * MOE_ROWS, 0, MOE_ROWS)

    assert D == ROW_TILE * LANES
    xs = _dispatch(slot, h2, n_tiles * MOE_ROWS, _pick_tile(M, 1024))
    ys = _ffn(tile_expert.astype(jnp.int32), n_used.reshape(1), n_valid.astype(jnp.int32), xs,
              w_gate[layer], w_up[layer], w_down[layer])
    wsgu = jnp.concatenate([ws_gate[layer], ws_up[layer]], axis=-1).astype(BF16)
    out = _combine(slot, wts.reshape(M, GATE_LANES), h2, x1.reshape(M, D), g2, final_norm_w.reshape(1, D),
                   wsgu, ws_down[layer].astype(BF16), ys, T, _pick_tile(T, 256))
    return out.reshape(B, T, D)
```

```python
import functools

import jax
import jax.numpy as jnp
from jax import lax
from jax.experimental import pallas as pl
from jax.experimental.pallas import tpu as pltpu

F32 = jnp.float32
BF16 = jnp.bfloat16

EPS = 1e-6
CHUNK = 64
SUB = 8
HEADS = 4
DH = 128
GW = HEADS * DH
CONV_K = 4
N_EXPERTS = 64
N_GROUPS = 8
GROUP_SIZE = N_EXPERTS // N_GROUPS
TOPK_GROUPS = 4
TOP_K = 8
D_EXPERT = 256
ROUTE_SCALE = 2.5
GATE_LANES = 128
GDN_CHUNKS_PER_ITER = 8
INV_BLOCK = 16
CUMSUM_ROWS = 128
MOE_ROWS = 1024
ROW_TILE, LANES = 8, 128

VMEM_LIMIT_BYTES = 56 * 1024 * 1024

ACT = BF16
ROW_DTYPE = F32


def _silu(x):
    return x * jax.nn.sigmoid(x)


def _dot(a, b):
    return jnp.dot(a.astype(BF16), b.astype(BF16), preferred_element_type=F32)


def _dot_nt(a, b):
    return lax.dot_general(a.astype(BF16), b.astype(BF16), (((1,), (1,)), ((), ())),
                           preferred_element_type=F32)


def _dot_tn(a, b):
    return lax.dot_general(a.astype(BF16), b.astype(BF16), (((0,), (0,)), ((), ())),
                           preferred_element_type=F32)


def _split2(x):
    hi = x.astype(BF16)
    lo = (x - hi.astype(F32)).astype(BF16)
    return hi, lo


def _dot3(a, b, dot=_dot):
    ah, al = _split2(a)
    bh, bl = _split2(b)
    return dot(ah, bh) + dot(ah, bl) + dot(al, bh)


def _cumsum_rows(tri, x):
    hi = x.astype(BF16)
    r = x - hi.astype(F32)
    mid = r.astype(BF16)
    lo = (r - mid.astype(F32)).astype(BF16)
    g = tri.shape[0]
    groups = []
    for r0 in range(0, x.shape[0], g):
        rows = slice(r0, r0 + g)
        groups.append(jnp.dot(tri, hi[rows], preferred_element_type=F32)
                      + jnp.dot(tri, mid[rows], preferred_element_type=F32)
                      + jnp.dot(tri, lo[rows], preferred_element_type=F32))
    return jnp.concatenate(groups, axis=0)


def _lane_pick(tile, lane, idx):
    return jnp.sum(jnp.where(lane == idx, tile, 0.0), axis=1, keepdims=True)


def _rms(x):
    return x * lax.rsqrt(jnp.mean(x * x, axis=-1, keepdims=True) + EPS)


def _to_rows(x3):
    r = x3.shape[0]
    xt = jnp.swapaxes(x3.reshape(r // ROW_TILE, ROW_TILE, ROW_TILE, LANES), 1, 2)
    return jnp.concatenate([xt[:, s].reshape(r, LANES) for s in range(ROW_TILE)], axis=1)


def _to_row_tiles(x):
    r = x.shape[0]
    xt = jnp.stack([x[:, s * LANES:(s + 1) * LANES].reshape(r // ROW_TILE, ROW_TILE, LANES) for s in range(ROW_TILE)],
                   axis=1)
    return jnp.swapaxes(xt, 1, 2).reshape(r, ROW_TILE, LANES)


def _ada_kernel(c_ref, w_ref, b_ref, o_ref):
    ca = _silu(c_ref[...])
    o_ref[...] = _dot3(ca, w_ref[...]) + b_ref[...]


def _ada(c_pad, w, b):
    rows, d = c_pad.shape
    n = w.shape[1]
    tn = 1024
    return pl.pallas_call(
        _ada_kernel,
        grid=(n // tn,),
        in_specs=[pl.BlockSpec((rows, d), lambda j: (0, 0)),
                  pl.BlockSpec((d, tn), lambda j: (0, j)),
                  pl.BlockSpec((1, tn), lambda j: (0, j))],
        out_specs=pl.BlockSpec((rows, tn), lambda j: (0, j)),
        out_shape=jax.ShapeDtypeStruct((rows, n), F32),
        compiler_params=pltpu.CompilerParams(dimension_semantics=("arbitrary",),
                                             vmem_limit_bytes=VMEM_LIMIT_BYTES),
        name="ada",
    )(c_pad, w, b)


def _inproj_kernel(layer, x_ref, n1_ref, sc_ref, sh_ref, w_ref, cw_ref, alog_ref, dt_ref, lb_ref, tri_ref,
                   qa_ref, ka_ref, va_ref, ga_ref, sm_ref, b_ref, kb_ref, ib_ref, qb_ref, gb_ref,
                   pbuf):
    tt = x_ref.shape[1]
    t = pl.program_id(1)

    h = _rms(x_ref[0]) * n1_ref[...]
    h = h * (1.0 + sc_ref[0]) + sh_ref[0]
    hb = h.astype(BF16)

    def proj(g, width=GW):
        return jnp.dot(hb, w_ref[:, g * GW:g * GW + width], preferred_element_type=F32)

    @pl.when(t == 0)
    def _():
        pbuf[:, 0:8, :] = jnp.zeros((3, 8, GW), F32)

    for g, out_ref in enumerate((qa_ref, ka_ref, va_ref)):
        cols = slice(g * GW, (g + 1) * GW)
        p = proj(g)
        pbuf[g, 8:8 + tt, :] = p
        y = p * cw_ref[CONV_K - 1:CONV_K, cols]
        for j in range(1, CONV_K):
            y = y + pbuf[g, 8 - j:8 - j + tt, :] * cw_ref[CONV_K - 1 - j:CONV_K - j, cols]
        pbuf[g, 0:8, :] = pbuf[g, tt:tt + 8, :]
        y = _silu(y)
        if g == 2:
            out_ref[0] = y.astype(out_ref.dtype)
        else:
            scale = DH ** -0.5 if g == 0 else 1.0
            for hh in range(HEADS):
                hs = slice(hh * DH, (hh + 1) * DH)
                yh = y[:, hs]
                inv = lax.rsqrt(jnp.sum(yh * yh, axis=-1, keepdims=True) + EPS)
                out_ref[0, :, hs] = (yh * inv * scale).astype(out_ref.dtype)

    ga_ref[0] = _silu(proj(3)).astype(ga_ref.dtype)

    ps = proj(8, GATE_LANES)
    lane = lax.broadcasted_iota(jnp.int32, ps.shape, 1)
    beta = jax.nn.sigmoid(ps)
    z = ps + dt_ref[...]
    softplus = jnp.maximum(z, 0.0) + jnp.log1p(jnp.exp(-jnp.abs(z)))
    g_log = -jnp.exp(alog_ref[...]) * softplus
    tri = tri_ref[...]
    gc = _cumsum_rows(tri, jnp.where((lane >= HEADS) & (lane < 2 * HEADS), g_log, 0.0))
    sm_ref[0] = jnp.where(lane < HEADS, beta, gc)

    hl = lb_ref[...]
    e = jnp.exp(hl - jnp.max(hl, axis=0, keepdims=True))
    lb = jnp.sum(e[0:layer + 1], axis=0, keepdims=True) / jnp.sum(e, axis=0, keepdims=True)
    fr = proj(4)
    logf = jnp.log(lb + (1.0 - lb) * jax.nn.sigmoid(fr))
    b_ref[0] = _cumsum_rows(tri, logf)
    kb_ref[0] = ((1.0 - lb) * jax.nn.sigmoid(-fr)).astype(kb_ref.dtype)
    ib_ref[0] = proj(5).astype(ib_ref.dtype)
    qb_ref[0] = _silu(proj(6)).astype(qb_ref.dtype)
    gb_ref[0] = _silu(proj(7)).astype(gb_ref.dtype)


def _inproj(layer, x, n1, sc1, sh1, w_all, conv_w, alog_pad, dt_pad, hg_lb, tri, tt):
    B, T, D = x.shape
    const = lambda shape: pl.BlockSpec(shape, lambda b, t: (0,) * len(shape), pipeline_mode=pl.Buffered(1))
    act = lambda dt: jax.ShapeDtypeStruct((B, T, GW), dt)
    tile = lambda w: pl.BlockSpec((1, tt, w), lambda b, t: (b, t, 0))
    per_batch = pl.BlockSpec((1, 1, D), lambda b, t: (b, 0, 0))
    return pl.pallas_call(
        functools.partial(_inproj_kernel, layer),
        grid=(B, T // tt),
        in_specs=[tile(D), const((1, D)), per_batch, per_batch,
                  const(w_all.shape), const(conv_w.shape), const((1, GATE_LANES)), const((1, GATE_LANES)),
                  const(hg_lb.shape), const(tri.shape)],
        out_specs=[tile(GW), tile(GW), tile(GW), tile(GW), tile(GATE_LANES), tile(GW),
                   tile(GW), tile(GW), tile(GW), tile(GW)],
        out_shape=[act(ACT), act(ACT), act(ACT), act(ACT),
                   jax.ShapeDtypeStruct((B, T, GATE_LANES), F32), act(F32),
                   act(ACT), act(ACT), act(ACT), act(ACT)],
        scratch_shapes=[pltpu.VMEM((3, tt + 8, GW), F32)],
        compiler_params=pltpu.CompilerParams(dimension_semantics=("arbitrary", "arbitrary"),
                                             vmem_limit_bytes=VMEM_LIMIT_BYTES),
        name="inproj",
    )(x, n1, sc1, sh1, w_all, conv_w, alog_pad, dt_pad, hg_lb, tri)


def _gdn_prep_kernel(q_ref, k_ref, v_ref, sm_ref, gct_ref, o_ref, qt_ref, m_ref, n_ref):
    tt = q_ref.shape[1]
    nc = tt // CHUNK
    row = lax.broadcasted_iota(jnp.int32, (CHUNK, CHUNK), 0)
    col = lax.broadcasted_iota(jnp.int32, (CHUNK, CHUNK), 1)
    causal = row >= col
    diag_blk = (row > col) & (row // INV_BLOCK == col // INV_BLOCK)
    off_blk = row // INV_BLOCK > col // INV_BLOCK
    eye = jnp.where(row == col, 1.0, 0.0)
    lane = lax.broadcasted_iota(jnp.int32, (CHUNK, GATE_LANES), 1)
    assert INV_BLOCK == 16 and CHUNK == 4 * INV_BLOCK

    def body(i, carry):
        chains = [(GDN_CHUNKS_PER_ITER * i + j, hh) for j in range(GDN_CHUNKS_PER_ITER) for hh in range(HEADS)]
        rows = [pl.ds(pl.multiple_of(c * CHUNK, CHUNK), CHUNK) for c, _ in chains]
        hs = [slice(hh * DH, (hh + 1) * DH) for _, hh in chains]
        n = range(len(chains))
        sm = [sm_ref[0, rows[j], :] for j in n]
        q = [q_ref[0, rows[j], hs[j]].astype(F32) for j in n]
        k = [k_ref[0, rows[j], hs[j]].astype(F32) for j in n]
        v = [v_ref[0, rows[j], hs[j]].astype(F32) for j in n]
        beta = [_lane_pick(sm[j], lane, chains[j][1]) for j in n]
        gcol = [_lane_pick(sm[j], lane, HEADS + chains[j][1]) for j in n]
        grow = [gct_ref[0, hh, pl.ds(c, 1), :] for c, hh in chains]
        decay = [jnp.exp(jnp.where(causal, gcol[j] - grow[j], -jnp.inf)) for j in n]
        kb = [k[j] * beta[j] for j in n]
        L = [_dot_nt(kb[j], k[j]) * decay[j] for j in n]
        dg = [jnp.where(diag_blk, L[j], 0.0) for j in n]
        off = [jnp.where(off_blk, L[j], 0.0) for j in n]
        dinv = [eye - dg[j] for j in n]
        pw = [_dot3(dg[j], dg[j]) for j in n]
        for _ in range(2):
            dinv = [dinv[j] + _dot3(dinv[j], pw[j]) for j in n]
            pw = [_dot3(pw[j], pw[j]) for j in n]
        dinv = [dinv[j] + _dot3(dinv[j], pw[j]) for j in n]
        f1 = [_dot(dinv[j], off[j]) for j in n]
        f2 = [_dot(f1[j], f1[j]) for j in n]
        f3 = [_dot(f1[j], f2[j]) for j in n]
        tinv = [_dot(eye - f1[j] + f2[j] - f3[j], dinv[j]) for j in n]
        eg = [jnp.exp(gcol[j]) for j in n]
        sol = [_dot(tinv[j], jnp.concatenate([v[j] * beta[j], kb[j] * eg[j]], axis=1)) for j in n]
        attn = [_dot_nt(q[j], k[j]) * decay[j] for j in n]
        k_tail = [k[j] * jnp.exp(gcol[j][CHUNK - 1:CHUNK, :] - gcol[j]) for j in n]
        au = [_dot(attn[j], sol[j]) for j in n]
        ku = [_dot_tn(k_tail[j], sol[j]) for j in n]
        for j, (c, hh) in enumerate(chains):
            o_ref[0, rows[j], hs[j]] = au[j][:, :DH]
            qt_ref[0, rows[j], hs[j]] = (q[j] * eg[j] - au[j][:, DH:]).astype(qt_ref.dtype)
            n_ref[0, hh, c] = ku[j][:, :DH].astype(n_ref.dtype)
            m_ref[0, hh, c] = (-ku[j][:, DH:]).astype(m_ref.dtype)
        return carry

    lax.fori_loop(0, nc // GDN_CHUNKS_PER_ITER, body, 0)


def _gdn_scan_kernel(o_ref, qt_ref, m_ref, n_ref, gct_ref, sg_ref, nw_ref, out_ref, s_ref):
    nb, tt = o_ref.shape[0], o_ref.shape[1]
    nc = tt // CHUNK

    @pl.when(pl.program_id(0) == 0)
    def _():
        s_ref[...] = jnp.zeros(s_ref.shape, F32)

    nw = nw_ref[...]

    def body(c, carry):
        rows = pl.ds(pl.multiple_of(c * CHUNK, CHUNK), CHUNK)
        for b in range(nb):
            for hh in range(HEADS):
                hs = slice(hh * DH, (hh + 1) * DH)
                S = s_ref[b, hh]
                Sb = S.astype(BF16)
                glast = gct_ref[b, hh, pl.ds(c, 1), :][:, CHUNK - 1:CHUNK]
                o = o_ref[b, rows, hs] + jnp.dot(qt_ref[b, rows, hs], Sb, preferred_element_type=F32)
                s_ref[b, hh] = (S * jnp.exp(glast) + jnp.dot(m_ref[b, hh, c], Sb, preferred_element_type=F32)
                                + n_ref[b, hh, c].astype(F32))
                o = _rms(o) * nw * sg_ref[b, rows, hs].astype(F32)
                out_ref[b, rows, hs] = o.astype(out_ref.dtype)
        return carry

    lax.fori_loop(0, nc, body, 0)


def _gdn(q, k, v, sg, sm, gct, nw, tt):
    B, T, _ = q.shape
    nc = tt // CHUNK
    assert nc % GDN_CHUNKS_PER_ITER == 0
    n_chunks = T // CHUNK
    tile = lambda w: pl.BlockSpec((1, tt, w), lambda b, t: (b, t, 0))
    mat = jax.ShapeDtypeStruct((B, HEADS, n_chunks, DH, DH), ACT)
    o_part, qt, m, n = pl.pallas_call(
        _gdn_prep_kernel,
        grid=(B, T // tt),
        in_specs=[tile(GW), tile(GW), tile(GW), tile(GATE_LANES),
                  pl.BlockSpec((1, HEADS, nc, CHUNK), lambda b, t: (b, 0, t, 0))],
        out_specs=[tile(GW), tile(GW),
                   pl.BlockSpec((1, HEADS, nc, DH, DH), lambda b, t: (b, 0, t, 0, 0)),
                   pl.BlockSpec((1, HEADS, nc, DH, DH), lambda b, t: (b, 0, t, 0, 0))],
        out_shape=[jax.ShapeDtypeStruct((B, T, GW), F32), jax.ShapeDtypeStruct((B, T, GW), ACT), mat, mat],
        compiler_params=pltpu.CompilerParams(dimension_semantics=("arbitrary", "arbitrary"),
                                             vmem_limit_bytes=VMEM_LIMIT_BYTES),
        name="gdn_prep",
    )(q, k, v, sm, gct)

    full = lambda w: pl.BlockSpec((B, tt, w), lambda t: (0, t, 0))
    mats = pl.BlockSpec((B, HEADS, nc, DH, DH), lambda t: (0, 0, t, 0, 0))
    return pl.pallas_call(
        _gdn_scan_kernel,
        grid=(T // tt,),
        in_specs=[full(GW), full(GW), mats, mats,
                  pl.BlockSpec((B, HEADS, nc, CHUNK), lambda t: (0, 0, t, 0)),
                  full(GW), pl.BlockSpec((1, DH), lambda t: (0, 0))],
        out_specs=full(GW),
        out_shape=jax.ShapeDtypeStruct((B, T, GW), ACT),
        scratch_shapes=[pltpu.VMEM((B, HEADS, DH, DH), F32)],
        compiler_params=pltpu.CompilerParams(dimension_semantics=("arbitrary",),
                                             vmem_limit_bytes=VMEM_LIMIT_BYTES),
        name="gdn_scan",
    )(o_part, qt, m, n, gct, sg, nw)


def _hgrn_kernel(q_ref, k_ref, v_ref, b_ref, sg_ref, nw_ref, o_ref, st_ref):
    nb, tt = q_ref.shape[0], q_ref.shape[1]
    nc = tt // CHUNK

    @pl.when(pl.program_id(0) == 0)
    def _():
        st_ref[...] = jnp.zeros(st_ref.shape, F32)

    row = lax.broadcasted_iota(jnp.int32, (CHUNK, CHUNK), 0)
    col = lax.broadcasted_iota(jnp.int32, (CHUNK, CHUNK), 1)
    diag_block = ((col // SUB) == (row // SUB)) & (col <= row)
    chains = [(bi, hh) for bi in range(nb) for hh in range(HEADS)]
    n = range(len(chains))
    hs = [slice(hh * DH, (hh + 1) * DH) for _, hh in chains]

    def body(c, carry):
        rows = pl.ds(pl.multiple_of(c * CHUNK, CHUNK), CHUNK)
        q = [q_ref[bi, rows, hs[j]].astype(F32) for j, (bi, _) in enumerate(chains)]
        k = [k_ref[bi, rows, hs[j]].astype(F32) for j, (bi, _) in enumerate(chains)]
        v = [v_ref[bi, rows, hs[j]].astype(F32) for j, (bi, _) in enumerate(chains)]
        b = [b_ref[bi, rows, hs[j]] for j, (bi, _) in enumerate(chains)]
        blast = [b[j][CHUNK - 1:CHUNK, :] for j in n]
        st = [st_ref[bi, hh] for bi, hh in chains]
        o = [_dot_nt(q[j] * jnp.exp(b[j]), st[j]) for j in n]
        k_tail = [k[j] * jnp.exp(blast[j] - b[j]) for j in n]
        for j, (bi, hh) in enumerate(chains):
            st_ref[bi, hh] = st[j] * jnp.exp(blast[j]) + _dot_tn(v[j], k_tail[j])

        blocks = [[jnp.zeros((SUB, CHUNK), F32)] for _ in n]
        for i in range(1, CHUNK // SUB):
            lo, hi = i * SUB, (i + 1) * SUB
            for j in n:
                r = b[j][lo:lo + 1, :]
                qi = q[j][lo:hi] * jnp.exp(b[j][lo:hi] - r)
                kj = k[j][:lo] * jnp.exp(jnp.minimum(r - b[j][:lo], 0.0))
                kj = jnp.concatenate([kj, jnp.zeros((CHUNK - lo, DH), F32)], axis=0)
                blocks[j].append(_dot_nt(qi, kj))
        a = []
        for j in n:
            f = jnp.exp(jnp.minimum(b[j] - pltpu.roll(b[j], 1, 0), 0.0))
            e = None
            a_diag = jnp.zeros((CHUNK, CHUNK), F32)
            for delta in range(SUB):
                if delta == 0:
                    term = q[j] * k[j]
                else:
                    fsh = f if delta == 1 else pltpu.roll(f, delta - 1, 0)
                    e = fsh if e is None else e * fsh
                    term = q[j] * pltpu.roll(k[j], delta, 0) * e
                colv = jnp.sum(term, axis=1, keepdims=True)
                a_diag = jnp.where(row - col == delta, colv, a_diag)
            a.append(jnp.where(diag_block, a_diag, jnp.concatenate(blocks[j], axis=0)))

        o = [o[j] + _dot(a[j], v[j]) for j in n]
        for bi in range(nb):
            ob = jnp.concatenate(o[bi * HEADS:(bi + 1) * HEADS], axis=1)
            ob = _rms(ob) * nw_ref[...] * sg_ref[bi, rows, :].astype(F32)
            o_ref[bi, rows, :] = ob.astype(o_ref.dtype)
        return carry

    lax.fori_loop(0, nc, body, 0)


def _hgrn(q, k, v, b, sg, nw, tt):
    B, T, _ = q.shape
    tile = pl.BlockSpec((B, tt, GW), lambda t: (0, t, 0))
    return pl.pallas_call(
        _hgrn_kernel,
        grid=(T // tt,),
        in_specs=[tile, tile, tile, tile, tile, pl.BlockSpec((1, GW), lambda t: (0, 0))],
        out_specs=tile,
        out_shape=jax.ShapeDtypeStruct((B, T, GW), ACT),
        scratch_shapes=[pltpu.VMEM((B, HEADS, DH, DH), F32)],
        compiler_params=pltpu.CompilerParams(dimension_semantics=("arbitrary",),
                                             vmem_limit_bytes=VMEM_LIMIT_BYTES),
        name="hgrn",
    )(q, k, v, b, sg, nw)


def _outproj_kernel(oa_ref, ob_ref, x_ref, wa_ref, wb_ref, g1_ref, n2_ref, sc_ref, sh_ref, wr_ref, rb_ref,
                    x1_ref, h2_ref, eidx_ref, rank_ref, wts_ref, cnt_ref):
    tm = x_ref.shape[1]

    @pl.when((pl.program_id(0) == 0) & (pl.program_id(1) == 0))
    def _():
        cnt_ref[...] = jnp.zeros(cnt_ref.shape, F32)

    mix = (jnp.dot(oa_ref[0], wa_ref[...], preferred_element_type=F32)
           + jnp.dot(ob_ref[0], wb_ref[...], preferred_element_type=F32))
    x1 = x_ref[0] + g1_ref[0] * mix
    x1_ref[0] = x1
    h2 = _rms(x1) * n2_ref[...]
    h2 = h2 * (1.0 + sc_ref[0]) + sh_ref[0]
    h2_ref[...] = _to_row_tiles(h2).astype(h2_ref.dtype)

    scores = jax.nn.sigmoid(_dot3(wr_ref[...], h2, dot=_dot_nt))
    sel = scores + rb_ref[...]
    sub = lax.broadcasted_iota(jnp.int32, (GROUP_SIZE, tm), 0)
    neg = -jnp.inf
    groups = range(N_GROUPS)

    def take_max(blk):
        m = jnp.max(blk, axis=0, keepdims=True)
        first = jnp.min(jnp.where(blk == m, sub, GROUP_SIZE), axis=0, keepdims=True)
        hit = sub == first
        return m, hit, jnp.where(hit, neg, blk)

    blk_of = lambda a, g: a[g * GROUP_SIZE:(g + 1) * GROUP_SIZE]
    sel_blk = [blk_of(sel, g) for g in groups]
    group_score = jnp.zeros((N_GROUPS, tm), F32)
    for g in groups:
        m1, _, rest = take_max(sel_blk[g])
        m2 = jnp.max(rest, axis=0, keepdims=True)
        group_score = jnp.where(sub == g, m1 + m2, group_score)
    group_on = jnp.zeros((N_GROUPS, tm), F32)
    for _ in range(TOPK_GROUPS):
        _, hit, group_score = take_max(group_score)
        group_on = jnp.where(hit, 1.0, group_on)

    cand = [jnp.where(group_on[g:g + 1] > 0.0, sel_blk[g], neg) for g in groups]
    picked = [jnp.zeros((GROUP_SIZE, tm), F32) for _ in groups]
    chosen = []
    for _ in range(TOP_K):
        m = jnp.max(functools.reduce(jnp.maximum, cand), axis=0, keepdims=True)
        first = functools.reduce(jnp.minimum, [jnp.where(cand[g] == m, sub + g * GROUP_SIZE, N_EXPERTS)
                                               for g in groups])
        first = jnp.min(first, axis=0, keepdims=True)
        chosen.append(first)
        for g in groups:
            hit = (sub + g * GROUP_SIZE) == first
            picked[g] = jnp.where(hit, 1.0, picked[g])
            cand[g] = jnp.where(hit, neg, cand[g])

    picked_all = jnp.concatenate(picked, axis=0)
    r_i = lax.broadcasted_iota(jnp.int32, (tm, tm), 0)
    c_i = lax.broadcasted_iota(jnp.int32, (tm, tm), 1)
    earlier = jnp.where(r_i < c_i, 1.0, 0.0).astype(BF16)
    before = jnp.dot(picked_all.astype(BF16), earlier, preferred_element_type=F32) + cnt_ref[:, 0:1]
    cnt_ref[...] = cnt_ref[...] + jnp.sum(picked_all, axis=1, keepdims=True)

    def pick_value(table, first):
        parts = [jnp.where((sub + g * GROUP_SIZE) == first, blk_of(table, g), 0.0) for g in groups]
        return jnp.sum(functools.reduce(jnp.add, parts), axis=0, keepdims=True)

    w_k = [pick_value(scores, f) for f in chosen]
    denom = functools.reduce(jnp.add, w_k)
    eidx = jnp.zeros((TOP_K, tm), jnp.int32)
    rank = jnp.zeros((TOP_K, tm), jnp.int32)
    wts = jnp.zeros((TOP_K, tm), F32)
    for k in range(TOP_K):
        eidx = jnp.where(sub == k, chosen[k], eidx)
        rank = jnp.where(sub == k, pick_value(before, chosen[k]).astype(jnp.int32), rank)
        wts = jnp.where(sub == k, w_k[k] / denom * ROUTE_SCALE, wts)
    eidx_ref[...] = eidx
    rank_ref[...] = rank
    pad = jnp.zeros((GATE_LANES - TOP_K, tm), F32)
    wts_ref[0] = jnp.concatenate([wts, pad], axis=0).T


def _outproj(oa, ob, x, wa, wb, g1, n2, sc2, sh2, wr_t, rb, tm):
    B, T, D = x.shape
    nt = T // tm
    const = lambda shape: pl.BlockSpec(shape, lambda b, t: (0,) * len(shape))
    tile = lambda w: pl.BlockSpec((1, tm, w), lambda b, t: (b, t, 0))
    per_batch = pl.BlockSpec((1, 1, D), lambda b, t: (b, 0, 0))
    picks = pl.BlockSpec((TOP_K, tm), lambda b, t: (0, b * nt + t))
    return pl.pallas_call(
        _outproj_kernel,
        grid=(B, nt),
        in_specs=[tile(GW), tile(GW), tile(D), const(wa.shape), const(wb.shape), per_batch,
                  const((1, D)), per_batch, per_batch, const(wr_t.shape), const(rb.shape)],
        out_specs=[tile(D), pl.BlockSpec((tm, ROW_TILE, LANES), lambda b, t: (b * nt + t, 0, 0)),
                   picks, picks, tile(GATE_LANES), const((N_EXPERTS, GATE_LANES))],
        out_shape=[jax.ShapeDtypeStruct((B, T, D), F32), jax.ShapeDtypeStruct((B * T, ROW_TILE, LANES), ROW_DTYPE),
                   jax.ShapeDtypeStruct((TOP_K, B * T), jnp.int32), jax.ShapeDtypeStruct((TOP_K, B * T), jnp.int32),
                   jax.ShapeDtypeStruct((B, T, GATE_LANES), F32),
                   jax.ShapeDtypeStruct((N_EXPERTS, GATE_LANES), F32)],
        compiler_params=pltpu.CompilerParams(dimension_semantics=("arbitrary", "arbitrary"),
                                             vmem_limit_bytes=VMEM_LIMIT_BYTES),
        name="outproj",
    )(oa, ob, x, wa, wb, g1, n2, sc2, sh2, wr_t, rb)


def _slots_kernel(off_ref, eidx_ref, rank_ref, slot_ref):
    eidx = eidx_ref[...]

    def add_expert(e, acc):
        return acc + jnp.where(eidx == e, off_ref[e], 0)

    slot_ref[...] = lax.fori_loop(0, N_EXPERTS, add_expert, rank_ref[...])


def _slots(offsets, eidx, rank):
    k, m = eidx.shape
    tile = pl.BlockSpec((k, m), lambda i: (0, 0))
    return pl.pallas_call(
        _slots_kernel,
        grid=(1,),
        in_specs=[pl.BlockSpec(memory_space=pltpu.SMEM), tile, tile],
        out_specs=tile,
        out_shape=jax.ShapeDtypeStruct((k, m), jnp.int32),
        compiler_params=pltpu.CompilerParams(dimension_semantics=("arbitrary",),
                                             vmem_limit_bytes=VMEM_LIMIT_BYTES),
        name="slots",
    )(offsets, eidx, rank)


def _dispatch_kernel(slot_ref, h_ref, xs_ref, sem):
    tmd = h_ref.shape[0]

    def start_rows(j, c):
        for k in range(TOP_K):
            pltpu.make_async_copy(h_ref.at[pl.ds(j, 1)], xs_ref.at[pl.ds(slot_ref[k, j], 1)], sem).start(
                priority=k % 2)
        return c

    lax.fori_loop(0, tmd, start_rows, 0)
    for _ in range(TOP_K):
        pltpu.make_async_copy(h_ref, xs_ref.at[pl.ds(0, tmd)], sem).wait()


def _dispatch(slot, h2, n_rows, tmd):
    M = h2.shape[0]
    return pl.pallas_call(
        _dispatch_kernel,
        grid=(M // tmd,),
        in_specs=[pl.BlockSpec((TOP_K, tmd), lambda i: (0, i), memory_space=pltpu.SMEM),
                  pl.BlockSpec((tmd, ROW_TILE, LANES), lambda i: (i, 0, 0))],
        out_specs=pl.BlockSpec(memory_space=pl.ANY),
        out_shape=jax.ShapeDtypeStruct((n_rows, ROW_TILE, LANES), h2.dtype),
        scratch_shapes=[pltpu.SemaphoreType.DMA],
        compiler_params=pltpu.CompilerParams(dimension_semantics=("arbitrary",),
                                             vmem_limit_bytes=VMEM_LIMIT_BYTES),
        name="dispatch",
    )(slot, h2)


FFN_PIPELINE_STEPS = 2


def _ffn_kernel(te_ref, nu_ref, nv_ref, x_ref, wg_ref, wu_ref, wd_ref, y_ref, wgu_s, wd_s, xstd, ystd):
    i = pl.program_id(0)
    row = lax.broadcasted_iota(jnp.int32, (MOE_ROWS, 1), 0)
    tile_mm = jnp.maximum(i - 1, 0)
    tile_out = jnp.maximum(i - 2, 0)

    @pl.when(i == 0)
    def _():
        xstd[...] = jnp.zeros(xstd.shape, BF16)
        ystd[...] = jnp.zeros(ystd.shape, F32)

    @pl.when(i < nu_ref[0] + FFN_PIPELINE_STEPS)
    def _():
        @pl.when((i == 0) | (te_ref[tile_mm] != te_ref[tile_out]))
        def _():
            wgu_s[:, :D_EXPERT] = wg_ref[0].astype(BF16)
            wgu_s[:, D_EXPERT:] = wu_ref[0].astype(BF16)
            wd_s[...] = wd_ref[0].astype(BF16)

        cur = i % 2
        xstd[cur] = jnp.where(row < nv_ref[i], _to_rows(x_ref[...].astype(F32)), 0.0).astype(BF16)
        y_ref[...] = _to_row_tiles(ystd[cur]).astype(y_ref.dtype)
        gu = jnp.dot(xstd[1 - cur], wgu_s[...], preferred_element_type=F32)
        act = _silu(gu[:, :D_EXPERT]) * gu[:, D_EXPERT:]
        ystd[1 - cur] = jnp.dot(act.astype(BF16), wd_s[...], preferred_element_type=F32)


def _ffn(tile_expert, n_used, n_valid, xs, wg, wu, wd):
    D = wg.shape[1]
    n_tiles = xs.shape[0] // MOE_ROWS
    block = (MOE_ROWS, ROW_TILE, LANES)
    expert = lambda i, te, nu, nv: (te[jnp.maximum(i - 1, 0)], 0, 0)
    return pl.pallas_call(
        _ffn_kernel,
        grid_spec=pltpu.PrefetchScalarGridSpec(
            num_scalar_prefetch=3,
            grid=(n_tiles + FFN_PIPELINE_STEPS,),
            in_specs=[pl.BlockSpec(block, lambda i, te, nu, nv: (jnp.minimum(i, nu[0] - 1), 0, 0)),
                      pl.BlockSpec((1, D, D_EXPERT), expert),
                      pl.BlockSpec((1, D, D_EXPERT), expert),
                      pl.BlockSpec((1, D_EXPERT, D), expert)],
            out_specs=pl.BlockSpec(block, lambda i, te, nu, nv: (jnp.clip(i - 2, 0, nu[0] - 1), 0, 0)),
            scratch_shapes=[pltpu.VMEM((D, 2 * D_EXPERT), BF16), pltpu.VMEM((D_EXPERT, D), BF16),
                            pltpu.VMEM((2, MOE_ROWS, D), BF16), pltpu.VMEM((2, MOE_ROWS, D), F32)]),
        out_shape=jax.ShapeDtypeStruct(xs.shape, xs.dtype),
        compiler_params=pltpu.CompilerParams(dimension_semantics=("arbitrary",),
                                             vmem_limit_bytes=VMEM_LIMIT_BYTES),
        name="ffn",
    )(tile_expert, n_used, n_valid, xs, wg, wu, wd)


def _combine_kernel(slot_ref, next_slot_ref, w_ref, h_ref, x1_ref, g2_ref, fw_ref, wsgu_ref, wsd_ref, y_ref,
                    o_ref, buf, sem):
    tmc = h_ref.shape[0]
    i = pl.program_id(0)
    last = pl.num_programs(0) - 1
    assert TOP_K == ROW_TILE and tmc % 2 == 0

    def start_pick(slots, b, k):
        def body(jj, c):
            for p in range(2):
                j = 2 * jj + p
                pltpu.make_async_copy(y_ref.at[pl.ds(slots[k, j], 1)], buf.at[b, k, pl.ds(j, 1)], sem.at[b]).start(
                    priority=p)
            return c
        lax.fori_loop(0, tmc // 2, body, 0)

    def wait_rows(b):
        for k in range(TOP_K):
            pltpu.make_async_copy(y_ref.at[pl.ds(0, tmc)], buf.at[b, k], sem.at[b]).wait()

    @pl.when(i == 0)
    def _():
        for k in range(TOP_K):
            start_pick(slot_ref, 0, k)

    gu = jnp.dot(_to_rows(h_ref[...].astype(F32)).astype(BF16), wsgu_ref[...], preferred_element_type=F32)
    act = _silu(gu[:, :D_EXPERT]) * gu[:, D_EXPERT:]
    shared = jnp.dot(act.astype(BF16), wsd_ref[...], preferred_element_type=F32)
    w = w_ref[...]
    lane = lax.broadcasted_iota(jnp.int32, w.shape, 1)
    w3 = _to_row_tiles(jnp.concatenate(
        [jnp.broadcast_to(_lane_pick(w, lane, k), (tmc, LANES)) for k in range(TOP_K)], axis=1))

    def step(b):
        wait_rows(b)
        routed = jnp.zeros((tmc, ROW_TILE, LANES), F32)
        for k in range(TOP_K):
            if 2 * k < TOP_K:
                @pl.when(i < last)
                def _():
                    start_pick(next_slot_ref, 1 - b, 2 * k)
                    start_pick(next_slot_ref, 1 - b, 2 * k + 1)
            routed = routed + jnp.broadcast_to(w3[:, k:k + 1, :], w3.shape) * buf[b, k].astype(F32)
        acc = shared + _to_rows(routed)
        y = x1_ref[...] + g2_ref[0] * acc
        o_ref[...] = _rms(y) * fw_ref[...]

    for b in range(2):
        @pl.when(i % 2 == b)
        def _():
            step(b)


def _combine(slot, wts, h2, x1, g2, fw, wsgu, wsd, ys, T, tmc):
    M, D = x1.shape
    const = lambda shape: pl.BlockSpec(shape, lambda i: (0,) * len(shape))
    tile = lambda w: pl.BlockSpec((tmc, w), lambda i: (i, 0))
    row_tiled = pl.BlockSpec((tmc, ROW_TILE, LANES), lambda i: (i, 0, 0))
    n_steps = M // tmc
    return pl.pallas_call(
        _combine_kernel,
        grid=(n_steps,),
        in_specs=[pl.BlockSpec((TOP_K, tmc), lambda i: (0, i), memory_space=pltpu.SMEM),
                  pl.BlockSpec((TOP_K, tmc), lambda i: (0, jnp.minimum(i + 1, n_steps - 1)), memory_space=pltpu.SMEM),
                  tile(GATE_LANES), row_tiled, tile(D),
                  pl.BlockSpec((1, 1, D), lambda i: (i // (T // tmc), 0, 0)),
                  const((1, D)), const(wsgu.shape), const(wsd.shape),
                  pl.BlockSpec(memory_space=pl.ANY)],
        out_specs=tile(D),
        out_shape=jax.ShapeDtypeStruct((M, D), F32),
        scratch_shapes=[pltpu.VMEM((2, TOP_K, tmc, ROW_TILE, LANES), ys.dtype), pltpu.SemaphoreType.DMA((2,))],
        compiler_params=pltpu.CompilerParams(dimension_semantics=("arbitrary",),
                                             vmem_limit_bytes=VMEM_LIMIT_BYTES),
        name="combine",
    )(slot, slot, wts, h2, x1, g2, fw, wsgu, wsd, ys)


def _pick_tile(n, want):
    t = min(n, want)
    assert n % t == 0 and t % CHUNK == 0, (n, want)
    return t


def kernel(x, c, w_ada, b_ada, norm1_w, w_in, conv_w, gdn_a_log, gdn_dt_bias, gdn_norm_w, hg_lb, hg_norm_w,
           w_out, norm2_w, w_router, router_bias, w_gate, w_up, w_down, ws_gate, ws_up, ws_down, final_norm_w):
    B, T, D = x.shape
    M = B * T
    depth = w_ada.shape[0]
    assert depth == 1 and T % CHUNK == 0 and B <= 8
    layer = 0
    tt = _pick_tile(T, 512)

    c_pad = jnp.pad(c, ((0, 8 - B), (0, 0)))
    mod = _ada(c_pad, w_ada[layer], b_ada[layer].reshape(1, -1))[:B]
    sh1, sc1, g1, sh2, sc2, g2 = (m.reshape(B, 1, D) for m in jnp.split(mod, 6, axis=-1))

    w = w_in[layer]
    qkv_w = 3 * GW
    sizes = (GW, HEADS, HEADS, GW, GW, GW, GW)
    offs = [qkv_w]
    for s in sizes:
        offs.append(offs[-1] + s)
    seg = lambda i: w[:, offs[i]:offs[i + 1]]
    small = jnp.pad(jnp.concatenate([seg(1), seg(2)], axis=1), ((0, 0), (0, GATE_LANES - 2 * HEADS)))
    w_all = jnp.concatenate([w[:, :qkv_w], seg(0), seg(3), seg(4), seg(5), seg(6), small], axis=1).astype(BF16)
    lane_pad = lambda v: jnp.pad(v.astype(F32).reshape(1, HEADS), ((0, 0), (HEADS, GATE_LANES - 2 * HEADS)))
    idx = jnp.arange(CUMSUM_ROWS)
    tri = ((idx[:, None] >= idx[None, :]) & (idx[:, None] // CHUNK == idx[None, :] // CHUNK)).astype(BF16)

    qa, ka, va, ga, sm, bcum, kb, ib, qb, gb = _inproj(
        layer, x, norm1_w[layer].reshape(1, D), sc1, sh1, w_all, conv_w[layer].astype(F32),
        lane_pad(gdn_a_log[layer]), lane_pad(gdn_dt_bias[layer]), hg_lb.astype(F32), tri, _pick_tile(T, 1024))

    gct = sm[:, :, HEADS:2 * HEADS].transpose(0, 2, 1).reshape(B, HEADS, T // CHUNK, CHUNK)
    oa = _gdn(qa, ka, va, ga, sm, gct, gdn_norm_w[layer].reshape(1, DH), tt)
    ob = _hgrn(qb, kb, ib, bcum, gb, hg_norm_w[layer].reshape(1, GW), tt)

    wo = w_out[layer].astype(BF16)
    x1, h2, eidx, rank, wts, cnt = _outproj(oa, ob, x, wo[:GW], wo[GW:], g1, norm2_w[layer].reshape(1, D), sc2, sh2,
                                            w_router[layer].T, router_bias[layer].reshape(N_EXPERTS, 1),
                                            _pick_tile(T, 1024))

    counts = cnt[:, 0].astype(jnp.int32)
    padded = (counts + MOE_ROWS - 1) // MOE_ROWS * MOE_ROWS
    ends = jnp.cumsum(padded)
    offsets = ends - padded
    n_tiles = (M * TOP_K) // MOE_ROWS + N_EXPERTS
    n_used = (ends[-1] // MOE_ROWS).astype(jnp.int32)
    tile_ids = jnp.minimum(jnp.arange(n_tiles + FFN_PIPELINE_STEPS, dtype=jnp.int32), n_used - 1)
    tile_expert = jnp.sum(((ends // MOE_ROWS)[None, :] <= tile_ids[:, None]).astype(jnp.int32), axis=1)
    tile_expert = jnp.minimum(tile_expert, N_EXPERTS - 1)
    slot = _slots(offsets.astype(jnp.int32), eidx, rank)
    first_tile = jnp.take(offsets, tile_expert) // MOE_ROWS
    n_valid = jnp.clip(jnp.take(counts, tile_expert) - (tile_ids - first_tile) * MOE_ROWS, 0, MOE_ROWS)

    assert D == ROW_TILE * LANES
    xs = _dispatch(slot, h2, n_tiles * MOE_ROWS, _pick_tile(M, 1024))
    ys = _ffn(tile_expert.astype(jnp.int32), n_used.reshape(1), n_valid.astype(jnp.int32), xs,
              w_gate[layer], w_up[layer], w_down[layer])
    wsgu = jnp.concatenate([ws_gate[layer], ws_up[layer]], axis=-1).astype(BF16)
    out = _combine(slot, wts.reshape(M, GATE_LANES), h2, x1.reshape(M, D), g2, final_norm_w.reshape(1, D),
                   wsgu, ws_down[layer].astype(BF16), ys, T, _pick_tile(T, 256))
    return out.reshape(B, T, D)
```

```python
import functools

import jax
import jax.numpy as jnp
from jax import lax
from jax.experimental import pallas as pl
from jax.experimental.pallas import tpu as pltpu

F32 = jnp.float32
BF16 = jnp.bfloat16

EPS = 1e-6
CHUNK = 64
SUB = 8
HEADS = 4
DH = 128
GW = HEADS * DH
CONV_K = 4
N_EXPERTS = 64
N_GROUPS = 8
GROUP_SIZE = N_EXPERTS // N_GROUPS
TOPK_GROUPS = 4
TOP_K = 8
D_EXPERT = 256
ROUTE_SCALE = 2.5
GATE_LANES = 128
GDN_CHUNKS_PER_ITER = 8
INV_BLOCK = 16
CUMSUM_ROWS = 128
MOE_ROWS = 1024
ROW_TILE, LANES = 8, 128

VMEM_LIMIT_BYTES = 56 * 1024 * 1024

ACT = BF16
ROW_DTYPE = F32


def _silu(x):
    return x * jax.nn.sigmoid(x)


def _dot(a, b):
    return jnp.dot(a.astype(BF16), b.astype(BF16), preferred_element_type=F32)


def _dot_nt(a, b):
    return lax.dot_general(a.astype(BF16), b.astype(BF16), (((1,), (1,)), ((), ())),
                           preferred_element_type=F32)


def _dot_tn(a, b):
    return lax.dot_general(a.astype(BF16), b.astype(BF16), (((0,), (0,)), ((), ())),
                           preferred_element_type=F32)


def _split2(x):
    hi = x.astype(BF16)
    lo = (x - hi.astype(F32)).astype(BF16)
    return hi, lo


def _dot3(a, b, dot=_dot):
    ah, al = _split2(a)
    bh, bl = _split2(b)
    return dot(ah, bh) + dot(ah, bl) + dot(al, bh)


def _cumsum_rows(tri, x):
    hi = x.astype(BF16)
    r = x - hi.astype(F32)
    mid = r.astype(BF16)
    lo = (r - mid.astype(F32)).astype(BF16)
    g = tri.shape[0]
    groups = []
    for r0 in range(0, x.shape[0], g):
        rows = slice(r0, r0 + g)
        groups.append(jnp.dot(tri, hi[rows], preferred_element_type=F32)
                      + jnp.dot(tri, mid[rows], preferred_element_type=F32)
                      + jnp.dot(tri, lo[rows], preferred_element_type=F32))
    return jnp.concatenate(groups, axis=0)


def _lane_pick(tile, lane, idx):
    return jnp.sum(jnp.where(lane == idx, tile, 0.0), axis=1, keepdims=True)


def _rms(x):
    return x * lax.rsqrt(jnp.mean(x * x, axis=-1, keepdims=True) + EPS)


def _to_rows(x3):
    r = x3.shape[0]
    xt = jnp.swapaxes(x3.reshape(r // ROW_TILE, ROW_TILE, ROW_TILE, LANES), 1, 2)
    return jnp.concatenate([xt[:, s].reshape(r, LANES) for s in range(ROW_TILE)], axis=1)


def _to_row_tiles(x):
    r = x.shape[0]
    xt = jnp.stack([x[:, s * LANES:(s + 1) * LANES].reshape(r // ROW_TILE, ROW_TILE, LANES) for s in range(ROW_TILE)],
                   axis=1)
    return jnp.swapaxes(xt, 1, 2).reshape(r, ROW_TILE, LANES)


def _ada_kernel(c_ref, w_ref, b_ref, o_ref):
    ca = _silu(c_ref[...])
    o_ref[...] = _dot3(ca, w_ref[...]) + b_ref[...]


def _ada(c_pad, w, b):
    rows, d = c_pad.shape
    n = w.shape[1]
    tn = 1024
    return pl.pallas_call(
        _ada_kernel,
        grid=(n // tn,),
        in_specs=[pl.BlockSpec((rows, d), lambda j: (0, 0)),
                  pl.BlockSpec((d, tn), lambda j: (0, j)),
                  pl.BlockSpec((1, tn), lambda j: (0, j))],
        out_specs=pl.BlockSpec((rows, tn), lambda j: (0, j)),
        out_shape=jax.ShapeDtypeStruct((rows, n), F32),
        compiler_params=pltpu.CompilerParams(dimension_semantics=("arbitrary",),
                                             vmem_limit_bytes=VMEM_LIMIT_BYTES),
        name="ada",
    )(c_pad, w, b)


def _inproj_kernel(layer, x_ref, n1_ref, sc_ref, sh_ref, w_ref, cw_ref, alog_ref, dt_ref, lb_ref, tri_ref,
                   qa_ref, ka_ref, va_ref, ga_ref, sm_ref, b_ref, kb_ref, ib_ref, qb_ref, gb_ref,
                   pbuf):
    tt = x_ref.shape[1]
    t = pl.program_id(1)

    h = _rms(x_ref[0]) * n1_ref[...]
    h = h * (1.0 + sc_ref[0]) + sh_ref[0]
    hb = h.astype(BF16)

    def proj(g, width=GW):
        return jnp.dot(hb, w_ref[:, g * GW:g * GW + width], preferred_element_type=F32)

    @pl.when(t == 0)
    def _():
        pbuf[:, 0:8, :] = jnp.zeros((3, 8, GW), F32)

    for g, out_ref in enumerate((qa_ref, ka_ref, va_ref)):
        cols = slice(g * GW, (g + 1) * GW)
        p = proj(g)
        pbuf[g, 8:8 + tt, :] = p
        y = p * cw_ref[CONV_K - 1:CONV_K, cols]
        for j in range(1, CONV_K):
            y = y + pbuf[g, 8 - j:8 - j + tt, :] * cw_ref[CONV_K - 1 - j:CONV_K - j, cols]
        pbuf[g, 0:8, :] = pbuf[g, tt:tt + 8, :]
        y = _silu(y)
        if g == 2:
            out_ref[0] = y.astype(out_ref.dtype)
        else:
            scale = DH ** -0.5 if g == 0 else 1.0
            for hh in range(HEADS):
                hs = slice(hh * DH, (hh + 1) * DH)
                yh = y[:, hs]
                inv = lax.rsqrt(jnp.sum(yh * yh, axis=-1, keepdims=True) + EPS)
                out_ref[0, :, hs] = (yh * inv * scale).astype(out_ref.dtype)

    ga_ref[0] = _silu(proj(3)).astype(ga_ref.dtype)

    ps = proj(8, GATE_LANES)
    lane = lax.broadcasted_iota(jnp.int32, ps.shape, 1)
    beta = jax.nn.sigmoid(ps)
    z = ps + dt_ref[...]
    softplus = jnp.maximum(z, 0.0) + jnp.log1p(jnp.exp(-jnp.abs(z)))
    g_log = -jnp.exp(alog_ref[...]) * softplus
    tri = tri_ref[...]
    gc = _cumsum_rows(tri, jnp.where((lane >= HEADS) & (lane < 2 * HEADS), g_log, 0.0))
    sm_ref[0] = jnp.where(lane < HEADS, beta, gc)

    hl = lb_ref[...]
    e = jnp.exp(hl - jnp.max(hl, axis=0, keepdims=True))
    lb = jnp.sum(e[0:layer + 1], axis=0, keepdims=True) / jnp.sum(e, axis=0, keepdims=True)
    fr = proj(4)
    logf = jnp.log(lb + (1.0 - lb) * jax.nn.sigmoid(fr))
    b_ref[0] = _cumsum_rows(tri, logf)
    kb_ref[0] = ((1.0 - lb) * jax.nn.sigmoid(-fr)).astype(kb_ref.dtype)
    ib_ref[0] = proj(5).astype(ib_ref.dtype)
    qb_ref[0] = _silu(proj(6)).astype(qb_ref.dtype)
    gb_ref[0] = _silu(proj(7)).astype(gb_ref.dtype)


def _inproj(layer, x, n1, sc1, sh1, w_all, conv_w, alog_pad, dt_pad, hg_lb, tri, tt):
    B, T, D = x.shape
    const = lambda shape: pl.BlockSpec(shape, lambda b, t: (0,) * len(shape), pipeline_mode=pl.Buffered(1))
    act = lambda dt: jax.ShapeDtypeStruct((B, T, GW), dt)
    tile = lambda w: pl.BlockSpec((1, tt, w), lambda b, t: (b, t, 0))
    per_batch = pl.BlockSpec((1, 1, D), lambda b, t: (b, 0, 0))
    return pl.pallas_call(
        functools.partial(_inproj_kernel, layer),
        grid=(B, T // tt),
        in_specs=[tile(D), const((1, D)), per_batch, per_batch,
                  const(w_all.shape), const(conv_w.shape), const((1, GATE_LANES)), const((1, GATE_LANES)),
                  const(hg_lb.shape), const(tri.shape)],
        out_specs=[tile(GW), tile(GW), tile(GW), tile(GW), tile(GATE_LANES), tile(GW),
                   tile(GW), tile(GW), tile(GW), tile(GW)],
        out_shape=[act(ACT), act(ACT), act(ACT), act(ACT),
                   jax.ShapeDtypeStruct((B, T, GATE_LANES), F32), act(F32),
                   act(ACT), act(ACT), act(ACT), act(ACT)],
        scratch_shapes=[pltpu.VMEM((3, tt + 8, GW), F32)],
        compiler_params=pltpu.CompilerParams(dimension_semantics=("arbitrary", "arbitrary"),
                                             vmem_limit_bytes=VMEM_LIMIT_BYTES),
        name="inproj",
    )(x, n1, sc1, sh1, w_all, conv_w, alog_pad, dt_pad, hg_lb, tri)


def _gdn_prep_kernel(q_ref, k_ref, v_ref, sm_ref, gct_ref, o_ref, qt_ref, m_ref, n_ref):
    tt = q_ref.shape[1]
    nc = tt // CHUNK
    row = lax.broadcasted_iota(jnp.int32, (CHUNK, CHUNK), 0)
    col = lax.broadcasted_iota(jnp.int32, (CHUNK, CHUNK), 1)
    causal = row >= col
    diag_blk = (row > col) & (row // INV_BLOCK == col // INV_BLOCK)
    off_blk = row // INV_BLOCK > col // INV_BLOCK
    eye = jnp.where(row == col, 1.0, 0.0)
    lane = lax.broadcasted_iota(jnp.int32, (CHUNK, GATE_LANES), 1)
    assert INV_BLOCK == 16 and CHUNK == 4 * INV_BLOCK

    def body(i, carry):
        chains = [(GDN_CHUNKS_PER_ITER * i + j, hh) for j in range(GDN_CHUNKS_PER_ITER) for hh in range(HEADS)]
        rows = [pl.ds(pl.multiple_of(c * CHUNK, CHUNK), CHUNK) for c, _ in chains]
        hs = [slice(hh * DH, (hh + 1) * DH) for _, hh in chains]
        n = range(len(chains))
        sm = [sm_ref[0, rows[j], :] for j in n]
        q = [q_ref[0, rows[j], hs[j]].astype(F32) for j in n]
        k = [k_ref[0, rows[j], hs[j]].astype(F32) for j in n]
        v = [v_ref[0, rows[j], hs[j]].astype(F32) for j in n]
        beta = [_lane_pick(sm[j], lane, chains[j][1]) for j in n]
        gcol = [_lane_pick(sm[j], lane, HEADS + chains[j][1]) for j in n]
        grow = [gct_ref[0, hh, pl.ds(c, 1), :] for c, hh in chains]
        decay = [jnp.exp(jnp.where(causal, gcol[j] - grow[j], -jnp.inf)) for j in n]
        kb = [k[j] * beta[j] for j in n]
        L = [_dot_nt(kb[j], k[j]) * decay[j] for j in n]
        dg = [jnp.where(diag_blk, L[j], 0.0) for j in n]
        off = [jnp.where(off_blk, L[j], 0.0) for j in n]
        dinv = [eye - dg[j] for j in n]
        pw = [_dot3(dg[j], dg[j]) for j in n]
        for _ in range(2):
            dinv = [dinv[j] + _dot3(dinv[j], pw[j]) for j in n]
            pw = [_dot3(pw[j], pw[j]) for j in n]
        dinv = [dinv[j] + _dot3(dinv[j], pw[j]) for j in n]
        f1 = [_dot(dinv[j], off[j]) for j in n]
        f2 = [_dot(f1[j], f1[j]) for j in n]
        f3 = [_dot(f1[j], f2[j]) for j in n]
        tinv = [_dot(eye - f1[j] + f2[j] - f3[j], dinv[j]) for j in n]
        eg = [jnp.exp(gcol[j]) for j in n]
        sol = [_dot(tinv[j], jnp.concatenate([v[j] * beta[j], kb[j] * eg[j]], axis=1)) for j in n]
        attn = [_dot_nt(q[j], k[j]) * decay[j] for j in n]
        k_tail = [k[j] * jnp.exp(gcol[j][CHUNK - 1:CHUNK, :] - gcol[j]) for j in n]
        au = [_dot(attn[j], sol[j]) for j in n]
        ku = [_dot_tn(k_tail[j], sol[j]) for j in n]
        for j, (c, hh) in enumerate(chains):
            o_ref[0, rows[j], hs[j]] = au[j][:, :DH]
            qt_ref[0, rows[j], hs[j]] = (q[j] * eg[j] - au[j][:, DH:]).astype(qt_ref.dtype)
            n_ref[0, hh, c] = ku[j][:, :DH].astype(n_ref.dtype)
            m_ref[0, hh, c] = (-ku[j][:, DH:]).astype(m_ref.dtype)
        return carry

    lax.fori_loop(0, nc // GDN_CHUNKS_PER_ITER, body, 0)


def _gdn_scan_kernel(o_ref, qt_ref, m_ref, n_ref, gct_ref, sg_ref, nw_ref, out_ref, s_ref):
    nb, tt = o_ref.shape[0], o_ref.shape[1]
    nc = tt // CHUNK

    @pl.when(pl.program_id(0) == 0)
    def _():
        s_ref[...] = jnp.zeros(s_ref.shape, F32)

    nw = nw_ref[...]

    def body(c, carry):
        rows = pl.ds(pl.multiple_of(c * CHUNK, CHUNK), CHUNK)
        for b in range(nb):
            for hh in range(HEADS):
                hs = slice(hh * DH, (hh + 1) * DH)
                S = s_ref[b, hh]
                Sb = S.astype(BF16)
                glast = gct_ref[b, hh, pl.ds(c, 1), :][:, CHUNK - 1:CHUNK]
                o = o_ref[b, rows, hs] + jnp.dot(qt_ref[b, rows, hs], Sb, preferred_element_type=F32)
                s_ref[b, hh] = (S * jnp.exp(glast) + jnp.dot(m_ref[b, hh, c], Sb, preferred_element_type=F32)
                                + n_ref[b, hh, c].astype(F32))
                o = _rms(o) * nw * sg_ref[b, rows, hs].astype(F32)
                out_ref[b, rows, hs] = o.astype(out_ref.dtype)
        return carry

    lax.fori_loop(0, nc, body, 0)


def _gdn(q, k, v, sg, sm, gct, nw, tt):
    B, T, _ = q.shape
    nc = tt // CHUNK
    assert nc % GDN_CHUNKS_PER_ITER == 0
    n_chunks = T // CHUNK
    tile = lambda w: pl.BlockSpec((1, tt, w), lambda b, t: (b, t, 0))
    mat = jax.ShapeDtypeStruct((B, HEADS, n_chunks, DH, DH), ACT)
    o_part, qt, m, n = pl.pallas_call(
        _gdn_prep_kernel,
        grid=(B, T // tt),
        in_specs=[tile(GW), tile(GW), tile(GW), tile(GATE_LANES),
                  pl.BlockSpec((1, HEADS, nc, CHUNK), lambda b, t: (b, 0, t, 0))],
        out_specs=[tile(GW), tile(GW),
                   pl.BlockSpec((1, HEADS, nc, DH, DH), lambda b, t: (b, 0, t, 0, 0)),
                   pl.BlockSpec((1, HEADS, nc, DH, DH), lambda b, t: (b, 0, t, 0, 0))],
        out_shape=[jax.ShapeDtypeStruct((B, T, GW), F32), jax.ShapeDtypeStruct((B, T, GW), ACT), mat, mat],
        compiler_params=pltpu.CompilerParams(dimension_semantics=("arbitrary", "arbitrary"),
                                             vmem_limit_bytes=VMEM_LIMIT_BYTES),
        name="gdn_prep",
    )(q, k, v, sm, gct)

    full = lambda w: pl.BlockSpec((B, tt, w), lambda t: (0, t, 0))
    mats = pl.BlockSpec((B, HEADS, nc, DH, DH), lambda t: (0, 0, t, 0, 0))
    return pl.pallas_call(
        _gdn_scan_kernel,
        grid=(T // tt,),
        in_specs=[full(GW), full(GW), mats, mats,
                  pl.BlockSpec((B, HEADS, nc, CHUNK), lambda t: (0, 0, t, 0)),
                  full(GW), pl.BlockSpec((1, DH), lambda t: (0, 0))],
        out_specs=full(GW),
        out_shape=jax.ShapeDtypeStruct((B, T, GW), ACT),
        scratch_shapes=[pltpu.VMEM((B, HEADS, DH, DH), F32)],
        compiler_params=pltpu.CompilerParams(dimension_semantics=("arbitrary",),
                                             vmem_limit_bytes=VMEM_LIMIT_BYTES),
        name="gdn_scan",
    )(o_part, qt, m, n, gct, sg, nw)


def _hgrn_kernel(q_ref, k_ref, v_ref, b_ref, sg_ref, nw_ref, o_ref, st_ref):
    nb, tt = q_ref.shape[0], q_ref.shape[1]
    nc = tt // CHUNK

    @pl.when(pl.program_id(0) == 0)
    def _():
        st_ref[...] = jnp.zeros(st_ref.shape, F32)

    row = lax.broadcasted_iota(jnp.int32, (CHUNK, CHUNK), 0)
    col = lax.broadcasted_iota(jnp.int32, (CHUNK, CHUNK), 1)
    diag_block = ((col // SUB) == (row // SUB)) & (col <= row)
    chains = [(bi, hh) for bi in range(nb) for hh in range(HEADS)]
    n = range(len(chains))
    hs = [slice(hh * DH, (hh + 1) * DH) for _, hh in chains]

    def body(c, carry):
        rows = pl.ds(pl.multiple_of(c * CHUNK, CHUNK), CHUNK)
        q = [q_ref[bi, rows, hs[j]].astype(F32) for j, (bi, _) in enumerate(chains)]
        k = [k_ref[bi, rows, hs[j]].astype(F32) for j, (bi, _) in enumerate(chains)]
        v = [v_ref[bi, rows, hs[j]].astype(F32) for j, (bi, _) in enumerate(chains)]
        b = [b_ref[bi, rows, hs[j]] for j, (bi, _) in enumerate(chains)]
        blast = [b[j][CHUNK - 1:CHUNK, :] for j in n]
        st = [st_ref[bi, hh] for bi, hh in chains]
        o = [_dot_nt(q[j] * jnp.exp(b[j]), st[j]) for j in n]
        k_tail = [k[j] * jnp.exp(blast[j] - b[j]) for j in n]
        for j, (bi, hh) in enumerate(chains):
            st_ref[bi, hh] = st[j] * jnp.exp(blast[j]) + _dot_tn(v[j], k_tail[j])

        blocks = [[jnp.zeros((SUB, CHUNK), F32)] for _ in n]
        for i in range(1, CHUNK // SUB):
            lo, hi = i * SUB, (i + 1) * SUB
            for j in n:
                r = b[j][lo:lo + 1, :]
                qi = q[j][lo:hi] * jnp.exp(b[j][lo:hi] - r)
                kj = k[j][:lo] * jnp.exp(jnp.minimum(r - b[j][:lo], 0.0))
                kj = jnp.concatenate([kj, jnp.zeros((CHUNK - lo, DH), F32)], axis=0)
                blocks[j].append(_dot_nt(qi, kj))
        a = []
        for j in n:
            f = jnp.exp(jnp.minimum(b[j] - pltpu.roll(b[j], 1, 0), 0.0))
            e = None
            a_diag = jnp.zeros((CHUNK, CHUNK), F32)
            for delta in range(SUB):
                if delta == 0:
                    term = q[j] * k[j]
                else:
                    fsh = f if delta == 1 else pltpu.roll(f, delta - 1, 0)
                    e = fsh if e is None else e * fsh
                    term = q[j] * pltpu.roll(k[j], delta, 0) * e
                colv = jnp.sum(term, axis=1, keepdims=True)
                a_diag = jnp.where(row - col == delta, colv, a_diag)
            a.append(jnp.where(diag_block, a_diag, jnp.concatenate(blocks[j], axis=0)))

        o = [o[j] + _dot(a[j], v[j]) for j in n]
        for bi in range(nb):
            ob = jnp.concatenate(o[bi * HEADS:(bi + 1) * HEADS], axis=1)
            ob = _rms(ob) * nw_ref[...] * sg_ref[bi, rows, :].astype(F32)
            o_ref[bi, rows, :] = ob.astype(o_ref.dtype)
        return carry

    lax.fori_loop(0, nc, body, 0)


def _hgrn(q, k, v, b, sg, nw, tt):
    B, T, _ = q.shape
    tile = pl.BlockSpec((B, tt, GW), lambda t: (0, t, 0))
    return pl.pallas_call(
        _hgrn_kernel,
        grid=(T // tt,),
        in_specs=[tile, tile, tile, tile, tile, pl.BlockSpec((1, GW), lambda t: (0, 0))],
        out_specs=tile,
        out_shape=jax.ShapeDtypeStruct((B, T, GW), ACT),
        scratch_shapes=[pltpu.VMEM((B, HEADS, DH, DH), F32)],
        compiler_params=pltpu.CompilerParams(dimension_semantics=("arbitrary",),
                                             vmem_limit_bytes=VMEM_LIMIT_BYTES),
        name="hgrn",
    )(q, k, v, b, sg, nw)


def _outproj_kernel(oa_ref, ob_ref, x_ref, wa_ref, wb_ref, g1_ref, n2_ref, sc_ref, sh_ref, wr_ref, rb_ref,
                    x1_ref, h2_ref, eidx_ref, rank_ref, wts_ref, cnt_ref):
    tm = x_ref.shape[1]

    @pl.when((pl.program_id(0) == 0) & (pl.program_id(1) == 0))
    def _():
        cnt_ref[...] = jnp.zeros(cnt_ref.shape, F32)

    mix = (jnp.dot(oa_ref[0], wa_ref[...], preferred_element_type=F32)
           + jnp.dot(ob_ref[0], wb_ref[...], preferred_element_type=F32))
    x1 = x_ref[0] + g1_ref[0] * mix
    x1_ref[0] = x1
    h2 = _rms(x1) * n2_ref[...]
    h2 = h2 * (1.0 + sc_ref[0]) + sh_ref[0]
    h2_ref[...] = _to_row_tiles(h2).astype(h2_ref.dtype)

    scores = jax.nn.sigmoid(_dot3(wr_ref[...], h2, dot=_dot_nt))
    sel = scores + rb_ref[...]
    sub = lax.broadcasted_iota(jnp.int32, (GROUP_SIZE, tm), 0)
    neg = -jnp.inf
    groups = range(N_GROUPS)

    def take_max(blk):
        m = jnp.max(blk, axis=0, keepdims=True)
        first = jnp.min(jnp.where(blk == m, sub, GROUP_SIZE), axis=0, keepdims=True)
        hit = sub == first
        return m, hit, jnp.where(hit, neg, blk)

    blk_of = lambda a, g: a[g * GROUP_SIZE:(g + 1) * GROUP_SIZE]
    sel_blk = [blk_of(sel, g) for g in groups]
    group_score = jnp.zeros((N_GROUPS, tm), F32)
    for g in groups:
        m1, _, rest = take_max(sel_blk[g])
        m2 = jnp.max(rest, axis=0, keepdims=True)
        group_score = jnp.where(sub == g, m1 + m2, group_score)
    group_on = jnp.zeros((N_GROUPS, tm), F32)
    for _ in range(TOPK_GROUPS):
        _, hit, group_score = take_max(group_score)
        group_on = jnp.where(hit, 1.0, group_on)

    cand = [jnp.where(group_on[g:g + 1] > 0.0, sel_blk[g], neg) for g in groups]
    picked = [jnp.zeros((GROUP_SIZE, tm), F32) for _ in groups]
    chosen = []
    for _ in range(TOP_K):
        m = jnp.max(functools.reduce(jnp.maximum, cand), axis=0, keepdims=True)
        first = functools.reduce(jnp.minimum, [jnp.where(cand[g] == m, sub + g * GROUP_SIZE, N_EXPERTS)
                                               for g in groups])
        first = jnp.min(first, axis=0, keepdims=True)
        chosen.append(first)
        for g in groups:
            hit = (sub + g * GROUP_SIZE) == first
            picked[g] = jnp.where(hit, 1.0, picked[g])
            cand[g] = jnp.where(hit, neg, cand[g])

    picked_all = jnp.concatenate(picked, axis=0)
    r_i = lax.broadcasted_iota(jnp.int32, (tm, tm), 0)
    c_i = lax.broadcasted_iota(jnp.int32, (tm, tm), 1)
    earlier = jnp.where(r_i < c_i, 1.0, 0.0).astype(BF16)
    before = jnp.dot(picked_all.astype(BF16), earlier, preferred_element_type=F32) + cnt_ref[:, 0:1]
    cnt_ref[...] = cnt_ref[...] + jnp.sum(picked_all, axis=1, keepdims=True)

    def pick_value(table, first):
        parts = [jnp.where((sub + g * GROUP_SIZE) == first, blk_of(table, g), 0.0) for g in groups]
        return jnp.sum(functools.reduce(jnp.add, parts), axis=0, keepdims=True)

    w_k = [pick_value(scores, f) for f in chosen]
    denom = functools.reduce(jnp.add, w_k)
    eidx = jnp.zeros((TOP_K, tm), jnp.int32)
    rank = jnp.zeros((TOP_K, tm), jnp.int32)
    wts = jnp.zeros((TOP_K, tm), F32)
    for k in range(TOP_K):
        eidx = jnp.where(sub == k, chosen[k], eidx)
        rank = jnp.where(sub == k, pick_value(before, chosen[k]).astype(jnp.int32), rank)
        wts = jnp.where(sub == k, w_k[k] / denom * ROUTE_SCALE, wts)
    eidx_ref[...] = eidx
    rank_ref[...] = rank
    pad = jnp.zeros((GATE_LANES - TOP_K, tm), F32)
    wts_ref[0] = jnp.concatenate([wts, pad], axis=0).T


def _outproj(oa, ob, x, wa, wb, g1, n2, sc2, sh2, wr_t, rb, tm):
    B, T, D = x.shape
    nt = T // tm
    const = lambda shape: pl.BlockSpec(shape, lambda b, t: (0,) * len(shape))
    tile = lambda w: pl.BlockSpec((1, tm, w), lambda b, t: (b, t, 0))
    per_batch = pl.BlockSpec((1, 1, D), lambda b, t: (b, 0, 0))
    picks = pl.BlockSpec((TOP_K, tm), lambda b, t: (0, b * nt + t))
    return pl.pallas_call(
        _outproj_kernel,
        grid=(B, nt),
        in_specs=[tile(GW), tile(GW), tile(D), const(wa.shape), const(wb.shape), per_batch,
                  const((1, D)), per_batch, per_batch, const(wr_t.shape), const(rb.shape)],
        out_specs=[tile(D), pl.BlockSpec((tm, ROW_TILE, LANES), lambda b, t: (b * nt + t, 0, 0)),
                   picks, picks, tile(GATE_LANES), const((N_EXPERTS, GATE_LANES))],
        out_shape=[jax.ShapeDtypeStruct((B, T, D), F32), jax.ShapeDtypeStruct((B * T, ROW_TILE, LANES), ROW_DTYPE),
                   jax.ShapeDtypeStruct((TOP_K, B * T), jnp.int32), jax.ShapeDtypeStruct((TOP_K, B * T), jnp.int32),
                   jax.ShapeDtypeStruct((B, T, GATE_LANES), F32),
                   jax.ShapeDtypeStruct((N_EXPERTS, GATE_LANES), F32)],
        compiler_params=pltpu.CompilerParams(dimension_semantics=("arbitrary", "arbitrary"),
                                             vmem_limit_bytes=VMEM_LIMIT_BYTES),
        name="outproj",
    )(oa, ob, x, wa, wb, g1, n2, sc2, sh2, wr_t, rb)


def _slots_kernel(off_ref, eidx_ref, rank_ref, slot_ref):
    eidx = eidx_ref[...]

    def add_expert(e, acc):
        return acc + jnp.where(eidx == e, off_ref[e], 0)

    slot_ref[...] = lax.fori_loop(0, N_EXPERTS, add_expert, rank_ref[...])


def _slots(offsets, eidx, rank):
    k, m = eidx.shape
    tile = pl.BlockSpec((k, m), lambda i: (0, 0))
    return pl.pallas_call(
        _slots_kernel,
        grid=(1,),
        in_specs=[pl.BlockSpec(memory_space=pltpu.SMEM), tile, tile],
        out_specs=tile,
        out_shape=jax.ShapeDtypeStruct((k, m), jnp.int32),
        compiler_params=pltpu.CompilerParams(dimension_semantics=("arbitrary",),
                                             vmem_limit_bytes=VMEM_LIMIT_BYTES),
        name="slots",
    )(offsets, eidx, rank)


def _dispatch_kernel(slot_ref, h_ref, xs_ref, sem):
    tmd = h_ref.shape[0]

    def start_rows(j, c):
        for k in range(TOP_K):
            pltpu.make_async_copy(h_ref.at[pl.ds(j, 1)], xs_ref.at[pl.ds(slot_ref[k, j], 1)], sem).start(
                priority=k % 2)
        return c

    lax.fori_loop(0, tmd, start_rows, 0)
    for _ in range(TOP_K):
        pltpu.make_async_copy(h_ref, xs_ref.at[pl.ds(0, tmd)], sem).wait()


def _dispatch(slot, h2, n_rows, tmd):
    M = h2.shape[0]
    return pl.pallas_call(
        _dispatch_kernel,
        grid=(M // tmd,),
        in_specs=[pl.BlockSpec((TOP_K, tmd), lambda i: (0, i), memory_space=pltpu.SMEM),
                  pl.BlockSpec((tmd, ROW_TILE, LANES), lambda i: (i, 0, 0))],
        out_specs=pl.BlockSpec(memory_space=pl.ANY),
        out_shape=jax.ShapeDtypeStruct((n_rows, ROW_TILE, LANES), h2.dtype),
        scratch_shapes=[pltpu.SemaphoreType.DMA],
        compiler_params=pltpu.CompilerParams(dimension_semantics=("arbitrary",),
                                             vmem_limit_bytes=VMEM_LIMIT_BYTES),
        name="dispatch",
    )(slot, h2)


FFN_PIPELINE_STEPS = 2


def _ffn_kernel(te_ref, nu_ref, nv_ref, x_ref, wg_ref, wu_ref, wd_ref, y_ref, wgu_s, wd_s, xstd, ystd):
    i = pl.program_id(0)
    row = lax.broadcasted_iota(jnp.int32, (MOE_ROWS, 1), 0)
    tile_mm = jnp.maximum(i - 1, 0)
    tile_out = jnp.maximum(i - 2, 0)

    @pl.when(i == 0)
    def _():
        xstd[...] = jnp.zeros(xstd.shape, BF16)
        ystd[...] = jnp.zeros(ystd.shape, F32)

    @pl.when(i < nu_ref[0] + FFN_PIPELINE_STEPS)
    def _():
        @pl.when((i == 0) | (te_ref[tile_mm] != te_ref[tile_out]))
        def _():
            wgu_s[:, :D_EXPERT] = wg_ref[0].astype(BF16)
            wgu_s[:, D_EXPERT:] = wu_ref[0].astype(BF16)
            wd_s[...] = wd_ref[0].astype(BF16)

        cur = i % 2
        xstd[cur] = jnp.where(row < nv_ref[i], _to_rows(x_ref[...].astype(F32)), 0.0).astype(BF16)
        y_ref[...] = _to_row_tiles(ystd[cur]).astype(y_ref.dtype)
        gu = jnp.dot(xstd[1 - cur], wgu_s[...], preferred_element_type=F32)
        act = _silu(gu[:, :D_EXPERT]) * gu[:, D_EXPERT:]
        ystd[1 - cur] = jnp.dot(act.astype(BF16), wd_s[...], preferred_element_type=F32)


def _ffn(tile_expert, n_used, n_valid, xs, wg, wu, wd):
    D = wg.shape[1]
    n_tiles = xs.shape[0] // MOE_ROWS
    block = (MOE_ROWS, ROW_TILE, LANES)
    expert = lambda i, te, nu, nv: (te[jnp.maximum(i - 1, 0)], 0, 0)
    return pl.pallas_call(
        _ffn_kernel,
        grid_spec=pltpu.PrefetchScalarGridSpec(
            num_scalar_prefetch=3,
            grid=(n_tiles + FFN_PIPELINE_STEPS,),
            in_specs=[pl.BlockSpec(block, lambda i, te, nu, nv: (jnp.minimum(i, nu[0] - 1), 0, 0)),
                      pl.BlockSpec((1, D, D_EXPERT), expert),
                      pl.BlockSpec((1, D, D_EXPERT), expert),
                      pl.BlockSpec((1, D_EXPERT, D), expert)],
            out_specs=pl.BlockSpec(block, lambda i, te, nu, nv: (jnp.clip(i - 2, 0, nu[0] - 1), 0, 0)),
            scratch_shapes=[pltpu.VMEM((D, 2 * D_EXPERT), BF16), pltpu.VMEM((D_EXPERT, D), BF16),
                            pltpu.VMEM((2, MOE_ROWS, D), BF16), pltpu.VMEM((2, MOE_ROWS, D), F32)]),
        out_shape=jax.ShapeDtypeStruct(xs.shape, xs.dtype),
        compiler_params=pltpu.CompilerParams(dimension_semantics=("arbitrary",),
                                             vmem_limit_bytes=VMEM_LIMIT_BYTES),
        name="ffn",
    )(tile_expert, n_used, n_valid, xs, wg, wu, wd)


def _combine_kernel(slot_ref, next_slot_ref, w_ref, h_ref, x1_ref, g2_ref, fw_ref, wsgu_ref, wsd_ref, y_ref,
                    o_ref, buf, sem):
    tmc = h_ref.shape[0]
    i = pl.program_id(0)
    last = pl.num_programs(0) - 1
    bursts = TOP_K // 2
    assert TOP_K == ROW_TILE and tmc % bursts == 0

    def start_rows(slots, b, burst):
        def body(j, c):
            for k in range(TOP_K):
                pltpu.make_async_copy(y_ref.at[pl.ds(slots[k, j], 1)], buf.at[b, k, pl.ds(j, 1)], sem.at[b]).start(
                    priority=k % 2)
            return c
        lax.fori_loop(burst * (tmc // bursts), (burst + 1) * (tmc // bursts), body, 0)

    def wait_rows(b):
        for k in range(TOP_K):
            pltpu.make_async_copy(y_ref.at[pl.ds(0, tmc)], buf.at[b, k], sem.at[b]).wait()

    @pl.when(i == 0)
    def _():
        for burst in range(bursts):
            start_rows(slot_ref, 0, burst)

    gu = jnp.dot(_to_rows(h_ref[...].astype(F32)).astype(BF16), wsgu_ref[...], preferred_element_type=F32)
    act = _silu(gu[:, :D_EXPERT]) * gu[:, D_EXPERT:]
    shared = jnp.dot(act.astype(BF16), wsd_ref[...], preferred_element_type=F32)
    w = w_ref[...]
    lane = lax.broadcasted_iota(jnp.int32, w.shape, 1)
    w3 = _to_row_tiles(jnp.concatenate(
        [jnp.broadcast_to(_lane_pick(w, lane, k), (tmc, LANES)) for k in range(TOP_K)], axis=1))

    def step(b):
        wait_rows(b)
        routed = jnp.zeros((tmc, ROW_TILE, LANES), F32)
        for k in range(TOP_K):
            if k < bursts:
                @pl.when(i < last)
                def _():
                    start_rows(next_slot_ref, 1 - b, k)
            routed = routed + jnp.broadcast_to(w3[:, k:k + 1, :], w3.shape) * buf[b, k].astype(F32)
        acc = shared + _to_rows(routed)
        y = x1_ref[...] + g2_ref[0] * acc
        o_ref[...] = _rms(y) * fw_ref[...]

    for b in range(2):
        @pl.when(i % 2 == b)
        def _():
            step(b)


def _combine(slot, wts, h2, x1, g2, fw, wsgu, wsd, ys, T, tmc):
    M, D = x1.shape
    const = lambda shape: pl.BlockSpec(shape, lambda i: (0,) * len(shape))
    tile = lambda w: pl.BlockSpec((tmc, w), lambda i: (i, 0))
    row_tiled = pl.BlockSpec((tmc, ROW_TILE, LANES), lambda i: (i, 0, 0))
    n_steps = M // tmc
    return pl.pallas_call(
        _combine_kernel,
        grid=(n_steps,),
        in_specs=[pl.BlockSpec((TOP_K, tmc), lambda i: (0, i), memory_space=pltpu.SMEM),
                  pl.BlockSpec((TOP_K, tmc), lambda i: (0, jnp.minimum(i + 1, n_steps - 1)), memory_space=pltpu.SMEM),
                  tile(GATE_LANES), row_tiled, tile(D),
                  pl.BlockSpec((1, 1, D), lambda i: (i // (T // tmc), 0, 0)),
                  const((1, D)), const(wsgu.shape), const(wsd.shape),
                  pl.BlockSpec(memory_space=pl.ANY)],
        out_specs=tile(D),
        out_shape=jax.ShapeDtypeStruct((M, D), F32),
        scratch_shapes=[pltpu.VMEM((2, TOP_K, tmc, ROW_TILE, LANES), ys.dtype), pltpu.SemaphoreType.DMA((2,))],
        compiler_params=pltpu.CompilerParams(dimension_semantics=("arbitrary",),
                                             vmem_limit_bytes=VMEM_LIMIT_BYTES),
        name="combine",
    )(slot, slot, wts, h2, x1, g2, fw, wsgu, wsd, ys)


def _pick_tile(n, want):
    t = min(n, want)
    assert n % t == 0 and t % CHUNK == 0, (n, want)
    return t


def kernel(x, c, w_ada, b_ada, norm1_w, w_in, conv_w, gdn_a_log, gdn_dt_bias, gdn_norm_w, hg_lb, hg_norm_w,
           w_out, norm2_w, w_router, router_bias, w_gate, w_up, w_down, ws_gate, ws_up, ws_down, final_norm_w):
    B, T, D = x.shape
    M = B * T
    depth = w_ada.shape[0]
    assert depth == 1 and T % CHUNK == 0 and B <= 8
    layer = 0
    tt = _pick_tile(T, 512)

    c_pad = jnp.pad(c, ((0, 8 - B), (0, 0)))
    mod = _ada(c_pad, w_ada[layer], b_ada[layer].reshape(1, -1))[:B]
    sh1, sc1, g1, sh2, sc2, g2 = (m.reshape(B, 1, D) for m in jnp.split(mod, 6, axis=-1))

    w = w_in[layer]
    qkv_w = 3 * GW
    sizes = (GW, HEADS, HEADS, GW, GW, GW, GW)
    offs = [qkv_w]
    for s in sizes:
        offs.append(offs[-1] + s)
    seg = lambda i: w[:, offs[i]:offs[i + 1]]
    small = jnp.pad(jnp.concatenate([seg(1), seg(2)], axis=1), ((0, 0), (0, GATE_LANES - 2 * HEADS)))
    w_all = jnp.concatenate([w[:, :qkv_w], seg(0), seg(3), seg(4), seg(5), seg(6), small], axis=1).astype(BF16)
    lane_pad = lambda v: jnp.pad(v.astype(F32).reshape(1, HEADS), ((0, 0), (HEADS, GATE_LANES - 2 * HEADS)))
    idx = jnp.arange(CUMSUM_ROWS)
    tri = ((idx[:, None] >= idx[None, :]) & (idx[:, None] // CHUNK == idx[None, :] // CHUNK)).astype(BF16)

    qa, ka, va, ga, sm, bcum, kb, ib, qb, gb = _inproj(
        layer, x, norm1_w[layer].reshape(1, D), sc1, sh1, w_all, conv_w[layer].astype(F32),
        lane_pad(gdn_a_log[layer]), lane_pad(gdn_dt_bias[layer]), hg_lb.astype(F32), tri, _pick_tile(T, 1024))

    gct = sm[:, :, HEADS:2 * HEADS].transpose(0, 2, 1).reshape(B, HEADS, T // CHUNK, CHUNK)
    oa = _gdn(qa, ka, va, ga, sm, gct, gdn_norm_w[layer].reshape(1, DH), tt)
    ob = _hgrn(qb, kb, ib, bcum, gb, hg_norm_w[layer].reshape(1, GW), tt)

    wo = w_out[layer].astype(BF16)
    x1, h2, eidx, rank, wts, cnt = _outproj(oa, ob, x, wo[:GW], wo[GW:], g1, norm2_w[layer].reshape(1, D), sc2, sh2,
                                            w_router[layer].T, router_bias[layer].reshape(N_EXPERTS, 1),
                                            _pick_tile(T, 1024))

    counts = cnt[:, 0].astype(jnp.int32)
    padded = (counts + MOE_ROWS - 1) // MOE_ROWS * MOE_ROWS
    ends = jnp.cumsum(padded)
    offsets = ends - padded
    n_tiles = (M * TOP_K) // MOE_ROWS + N_EXPERTS
    n_used = (ends[-1] // MOE_ROWS).astype(jnp.int32)
    tile_ids = jnp.minimum(jnp.arange(n_tiles + FFN_PIPELINE_STEPS, dtype=jnp.int32), n_used - 1)
    tile_expert = jnp.sum(((ends // MOE_ROWS)[None, :] <= tile_ids[:, None]).astype(jnp.int32), axis=1)
    tile_expert = jnp.minimum(tile_expert, N_EXPERTS - 1)
    slot = _slots(offsets.astype(jnp.int32), eidx, rank)
    first_tile = jnp.take(offsets, tile_expert) // MOE_ROWS
    n_valid = jnp.clip(jnp.take(counts, tile_expert) - (tile_ids - first_tile) * MOE_ROWS, 0, MOE_ROWS)

    assert D == ROW_TILE * LANES
    xs = _dispatch(slot, h2, n_tiles * MOE_ROWS, _pick_tile(M, 1024))
    ys = _ffn(tile_expert.astype(jnp.int32), n_used.reshape(1), n_valid.astype(jnp.int32), xs,
              w_gate[layer], w_up[layer], w_down[layer])
    wsgu = jnp.concatenate([ws_gate[layer], ws_up[layer]], axis=-1).astype(BF16)
    out = _combine(slot, wts.reshape(M, GATE_LANES), h2, x1.reshape(M, D), g2, final_norm_w.reshape(1, D),
                   wsgu, ws_down[layer].astype(BF16), ys, T, _pick_tile(T, 256))
    return out.reshape(B, T, D)
```

```python
import functools

import jax
import jax.numpy as jnp
from jax import lax
from jax.experimental import pallas as pl
from jax.experimental.pallas import tpu as pltpu

F32 = jnp.float32
BF16 = jnp.bfloat16

EPS = 1e-6
CHUNK = 64
SUB = 8
HEADS = 4
DH = 128
GW = HEADS * DH
CONV_K = 4
N_EXPERTS = 64
N_GROUPS = 8
GROUP_SIZE = N_EXPERTS // N_GROUPS
TOPK_GROUPS = 4
TOP_K = 8
D_EXPERT = 256
ROUTE_SCALE = 2.5
GATE_LANES = 128
GDN_CHUNKS_PER_ITER = 8
INV_BLOCK = 16
CUMSUM_ROWS = 128
MOE_ROWS = 1024
ROW_TILE, LANES = 8, 128

TILE_INPROJ = 1024
TILE_RECURRENCE = 512
TILE_OUTPROJ = 1024
TILE_DISPATCH = 1024
TILE_COMBINE = 256
ADA_COLS = 1024

VMEM_LIMIT_BYTES = 56 * 1024 * 1024

ACT = BF16
ROW_DTYPE = F32


def _silu(x):
    return x * jax.nn.sigmoid(x)


def _dot(a, b):
    return jnp.dot(a.astype(BF16), b.astype(BF16), preferred_element_type=F32)


def _dot_nt(a, b):
    return lax.dot_general(a.astype(BF16), b.astype(BF16), (((1,), (1,)), ((), ())),
                           preferred_element_type=F32)


def _dot_tn(a, b):
    return lax.dot_general(a.astype(BF16), b.astype(BF16), (((0,), (0,)), ((), ())),
                           preferred_element_type=F32)


def _split2(x):
    hi = x.astype(BF16)
    lo = (x - hi.astype(F32)).astype(BF16)
    return hi, lo


def _dot3(a, b, dot=_dot):
    ah, al = _split2(a)
    bh, bl = _split2(b)
    return dot(ah, bh) + dot(ah, bl) + dot(al, bh)


def _cumsum_rows(tri, x):
    hi = x.astype(BF16)
    r = x - hi.astype(F32)
    mid = r.astype(BF16)
    lo = (r - mid.astype(F32)).astype(BF16)
    g = tri.shape[0]
    groups = []
    for r0 in range(0, x.shape[0], g):
        rows = slice(r0, r0 + g)
        groups.append(jnp.dot(tri, hi[rows], preferred_element_type=F32)
                      + jnp.dot(tri, mid[rows], preferred_element_type=F32)
                      + jnp.dot(tri, lo[rows], preferred_element_type=F32))
    return jnp.concatenate(groups, axis=0)


def _lane_pick(tile, lane, idx):
    return jnp.sum(jnp.where(lane == idx, tile, 0.0), axis=1, keepdims=True)


def _rms(x):
    return x * lax.rsqrt(jnp.mean(x * x, axis=-1, keepdims=True) + EPS)


def _to_rows(x3):
    r = x3.shape[0]
    xt = jnp.swapaxes(x3.reshape(r // ROW_TILE, ROW_TILE, ROW_TILE, LANES), 1, 2)
    return jnp.concatenate([xt[:, s].reshape(r, LANES) for s in range(ROW_TILE)], axis=1)


def _to_row_tiles(x):
    r = x.shape[0]
    xt = jnp.stack([x[:, s * LANES:(s + 1) * LANES].reshape(r // ROW_TILE, ROW_TILE, LANES) for s in range(ROW_TILE)],
                   axis=1)
    return jnp.swapaxes(xt, 1, 2).reshape(r, ROW_TILE, LANES)


def _ada_kernel(c_ref, w_ref, b_ref, o_ref):
    ca = _silu(c_ref[...])
    o_ref[...] = _dot3(ca, w_ref[...]) + b_ref[...]


def _ada(c_pad, w, b):
    rows, d = c_pad.shape
    n = w.shape[1]
    tn = ADA_COLS
    return pl.pallas_call(
        _ada_kernel,
        grid=(n // tn,),
        in_specs=[pl.BlockSpec((rows, d), lambda j: (0, 0)),
                  pl.BlockSpec((d, tn), lambda j: (0, j)),
                  pl.BlockSpec((1, tn), lambda j: (0, j))],
        out_specs=pl.BlockSpec((rows, tn), lambda j: (0, j)),
        out_shape=jax.ShapeDtypeStruct((rows, n), F32),
        compiler_params=pltpu.CompilerParams(dimension_semantics=("arbitrary",),
                                             vmem_limit_bytes=VMEM_LIMIT_BYTES),
        name="ada",
    )(c_pad, w, b)


def _inproj_kernel(layer, x_ref, n1_ref, sc_ref, sh_ref, w_ref, cw_ref, alog_ref, dt_ref, lb_ref, tri_ref,
                   qa_ref, ka_ref, va_ref, ga_ref, sm_ref, b_ref, kb_ref, ib_ref, qb_ref, gb_ref,
                   pbuf):
    tt = x_ref.shape[1]
    t = pl.program_id(1)

    h = _rms(x_ref[0]) * n1_ref[...]
    h = h * (1.0 + sc_ref[0]) + sh_ref[0]
    hb = h.astype(BF16)

    def proj(g, width=GW):
        return jnp.dot(hb, w_ref[:, g * GW:g * GW + width], preferred_element_type=F32)

    @pl.when(t == 0)
    def _():
        pbuf[:, 0:8, :] = jnp.zeros((3, 8, GW), F32)

    for g, out_ref in enumerate((qa_ref, ka_ref, va_ref)):
        cols = slice(g * GW, (g + 1) * GW)
        p = proj(g)
        pbuf[g, 8:8 + tt, :] = p
        y = p * cw_ref[CONV_K - 1:CONV_K, cols]
        for j in range(1, CONV_K):
            y = y + pbuf[g, 8 - j:8 - j + tt, :] * cw_ref[CONV_K - 1 - j:CONV_K - j, cols]
        pbuf[g, 0:8, :] = pbuf[g, tt:tt + 8, :]
        y = _silu(y)
        if g == 2:
            out_ref[0] = y.astype(out_ref.dtype)
        else:
            scale = DH ** -0.5 if g == 0 else 1.0
            for hh in range(HEADS):
                hs = slice(hh * DH, (hh + 1) * DH)
                yh = y[:, hs]
                inv = lax.rsqrt(jnp.sum(yh * yh, axis=-1, keepdims=True) + EPS)
                out_ref[0, :, hs] = (yh * inv * scale).astype(out_ref.dtype)

    ga_ref[0] = _silu(proj(3)).astype(ga_ref.dtype)

    ps = proj(8, GATE_LANES)
    lane = lax.broadcasted_iota(jnp.int32, ps.shape, 1)
    beta = jax.nn.sigmoid(ps)
    z = ps + dt_ref[...]
    softplus = jnp.maximum(z, 0.0) + jnp.log1p(jnp.exp(-jnp.abs(z)))
    g_log = -jnp.exp(alog_ref[...]) * softplus
    tri = tri_ref[...]
    gc = _cumsum_rows(tri, jnp.where((lane >= HEADS) & (lane < 2 * HEADS), g_log, 0.0))
    sm_ref[0] = jnp.where(lane < HEADS, beta, gc)

    hl = lb_ref[...]
    e = jnp.exp(hl - jnp.max(hl, axis=0, keepdims=True))
    lb = jnp.sum(e[0:layer + 1], axis=0, keepdims=True) / jnp.sum(e, axis=0, keepdims=True)
    fr = proj(4)
    logf = jnp.log(lb + (1.0 - lb) * jax.nn.sigmoid(fr))
    b_ref[0] = _cumsum_rows(tri, logf)
    kb_ref[0] = ((1.0 - lb) * jax.nn.sigmoid(-fr)).astype(kb_ref.dtype)
    ib_ref[0] = proj(5).astype(ib_ref.dtype)
    qb_ref[0] = _silu(proj(6)).astype(qb_ref.dtype)
    gb_ref[0] = _silu(proj(7)).astype(gb_ref.dtype)


def _inproj(layer, x, n1, sc1, sh1, w_all, conv_w, alog_pad, dt_pad, hg_lb, tri, tt):
    B, T, D = x.shape
    const = lambda shape: pl.BlockSpec(shape, lambda b, t: (0,) * len(shape), pipeline_mode=pl.Buffered(1))
    act = lambda dt: jax.ShapeDtypeStruct((B, T, GW), dt)
    tile = lambda w: pl.BlockSpec((1, tt, w), lambda b, t: (b, t, 0))
    per_batch = pl.BlockSpec((1, 1, D), lambda b, t: (b, 0, 0))
    return pl.pallas_call(
        functools.partial(_inproj_kernel, layer),
        grid=(B, T // tt),
        in_specs=[tile(D), const((1, D)), per_batch, per_batch,
                  const(w_all.shape), const(conv_w.shape), const((1, GATE_LANES)), const((1, GATE_LANES)),
                  const(hg_lb.shape), const(tri.shape)],
        out_specs=[tile(GW), tile(GW), tile(GW), tile(GW), tile(GATE_LANES), tile(GW),
                   tile(GW), tile(GW), tile(GW), tile(GW)],
        out_shape=[act(ACT), act(ACT), act(ACT), act(ACT),
                   jax.ShapeDtypeStruct((B, T, GATE_LANES), F32), act(F32),
                   act(ACT), act(ACT), act(ACT), act(ACT)],
        scratch_shapes=[pltpu.VMEM((3, tt + 8, GW), F32)],
        compiler_params=pltpu.CompilerParams(dimension_semantics=("arbitrary", "arbitrary"),
                                             vmem_limit_bytes=VMEM_LIMIT_BYTES),
        name="inproj",
    )(x, n1, sc1, sh1, w_all, conv_w, alog_pad, dt_pad, hg_lb, tri)


def _gdn_prep_kernel(q_ref, k_ref, v_ref, sm_ref, gct_ref, o_ref, qt_ref, m_ref, n_ref):
    tt = q_ref.shape[1]
    nc = tt // CHUNK
    row = lax.broadcasted_iota(jnp.int32, (CHUNK, CHUNK), 0)
    col = lax.broadcasted_iota(jnp.int32, (CHUNK, CHUNK), 1)
    causal = row >= col
    diag_blk = (row > col) & (row // INV_BLOCK == col // INV_BLOCK)
    off_blk = row // INV_BLOCK > col // INV_BLOCK
    eye = jnp.where(row == col, 1.0, 0.0)
    lane = lax.broadcasted_iota(jnp.int32, (CHUNK, GATE_LANES), 1)
    assert INV_BLOCK == 16 and CHUNK == 4 * INV_BLOCK

    def body(i, carry):
        chains = [(GDN_CHUNKS_PER_ITER * i + j, hh) for j in range(GDN_CHUNKS_PER_ITER) for hh in range(HEADS)]
        rows = [pl.ds(pl.multiple_of(c * CHUNK, CHUNK), CHUNK) for c, _ in chains]
        hs = [slice(hh * DH, (hh + 1) * DH) for _, hh in chains]
        n = range(len(chains))
        sm = [sm_ref[0, rows[j], :] for j in n]
        q = [q_ref[0, rows[j], hs[j]].astype(F32) for j in n]
        k = [k_ref[0, rows[j], hs[j]].astype(F32) for j in n]
        v = [v_ref[0, rows[j], hs[j]].astype(F32) for j in n]
        beta = [_lane_pick(sm[j], lane, chains[j][1]) for j in n]
        gcol = [_lane_pick(sm[j], lane, HEADS + chains[j][1]) for j in n]
        grow = [gct_ref[0, hh, pl.ds(c, 1), :] for c, hh in chains]
        decay = [jnp.exp(jnp.where(causal, gcol[j] - grow[j], -jnp.inf)) for j in n]
        kb = [k[j] * beta[j] for j in n]
        L = [_dot_nt(kb[j], k[j]) * decay[j] for j in n]
        dg = [jnp.where(diag_blk, L[j], 0.0) for j in n]
        off = [jnp.where(off_blk, L[j], 0.0) for j in n]
        dinv = [eye - dg[j] for j in n]
        pw = [_dot3(dg[j], dg[j]) for j in n]
        for _ in range(2):
            dinv = [dinv[j] + _dot3(dinv[j], pw[j]) for j in n]
            pw = [_dot3(pw[j], pw[j]) for j in n]
        dinv = [dinv[j] + _dot3(dinv[j], pw[j]) for j in n]
        f1 = [_dot(dinv[j], off[j]) for j in n]
        f2 = [_dot(f1[j], f1[j]) for j in n]
        f3 = [_dot(f1[j], f2[j]) for j in n]
        tinv = [_dot(eye - f1[j] + f2[j] - f3[j], dinv[j]) for j in n]
        eg = [jnp.exp(gcol[j]) for j in n]
        sol = [_dot(tinv[j], jnp.concatenate([v[j] * beta[j], kb[j] * eg[j]], axis=1)) for j in n]
        attn = [_dot_nt(q[j], k[j]) * decay[j] for j in n]
        k_tail = [k[j] * jnp.exp(gcol[j][CHUNK - 1:CHUNK, :] - gcol[j]) for j in n]
        au = [_dot(attn[j], sol[j]) for j in n]
        ku = [_dot_tn(k_tail[j], sol[j]) for j in n]
        for j, (c, hh) in enumerate(chains):
            o_ref[0, rows[j], hs[j]] = au[j][:, :DH]
            qt_ref[0, rows[j], hs[j]] = (q[j] * eg[j] - au[j][:, DH:]).astype(qt_ref.dtype)
            n_ref[0, hh, c] = ku[j][:, :DH].astype(n_ref.dtype)
            m_ref[0, hh, c] = (-ku[j][:, DH:]).astype(m_ref.dtype)
        return carry

    lax.fori_loop(0, nc // GDN_CHUNKS_PER_ITER, body, 0)


def _gdn_scan_kernel(o_ref, qt_ref, m_ref, n_ref, gct_ref, sg_ref, nw_ref, out_ref, s_ref):
    nb, tt = o_ref.shape[0], o_ref.shape[1]
    nc = tt // CHUNK

    @pl.when(pl.program_id(0) == 0)
    def _():
        s_ref[...] = jnp.zeros(s_ref.shape, F32)

    nw = nw_ref[...]

    def body(c, carry):
        rows = pl.ds(pl.multiple_of(c * CHUNK, CHUNK), CHUNK)
        for b in range(nb):
            for hh in range(HEADS):
                hs = slice(hh * DH, (hh + 1) * DH)
                S = s_ref[b, hh]
                Sb = S.astype(BF16)
                glast = gct_ref[b, hh, pl.ds(c, 1), :][:, CHUNK - 1:CHUNK]
                o = o_ref[b, rows, hs] + jnp.dot(qt_ref[b, rows, hs], Sb, preferred_element_type=F32)
                s_ref[b, hh] = (S * jnp.exp(glast) + jnp.dot(m_ref[b, hh, c], Sb, preferred_element_type=F32)
                                + n_ref[b, hh, c].astype(F32))
                o = _rms(o) * nw * sg_ref[b, rows, hs].astype(F32)
                out_ref[b, rows, hs] = o.astype(out_ref.dtype)
        return carry

    lax.fori_loop(0, nc, body, 0)


def _gdn(q, k, v, sg, sm, gct, nw, tt):
    B, T, _ = q.shape
    nc = tt // CHUNK
    assert nc % GDN_CHUNKS_PER_ITER == 0
    n_chunks = T // CHUNK
    tile = lambda w: pl.BlockSpec((1, tt, w), lambda b, t: (b, t, 0))
    mat = jax.ShapeDtypeStruct((B, HEADS, n_chunks, DH, DH), ACT)
    o_part, qt, m, n = pl.pallas_call(
        _gdn_prep_kernel,
        grid=(B, T // tt),
        in_specs=[tile(GW), tile(GW), tile(GW), tile(GATE_LANES),
                  pl.BlockSpec((1, HEADS, nc, CHUNK), lambda b, t: (b, 0, t, 0))],
        out_specs=[tile(GW), tile(GW),
                   pl.BlockSpec((1, HEADS, nc, DH, DH), lambda b, t: (b, 0, t, 0, 0)),
                   pl.BlockSpec((1, HEADS, nc, DH, DH), lambda b, t: (b, 0, t, 0, 0))],
        out_shape=[jax.ShapeDtypeStruct((B, T, GW), F32), jax.ShapeDtypeStruct((B, T, GW), ACT), mat, mat],
        compiler_params=pltpu.CompilerParams(dimension_semantics=("arbitrary", "arbitrary"),
                                             vmem_limit_bytes=VMEM_LIMIT_BYTES),
        name="gdn_prep",
    )(q, k, v, sm, gct)

    full = lambda w: pl.BlockSpec((B, tt, w), lambda t: (0, t, 0))
    mats = pl.BlockSpec((B, HEADS, nc, DH, DH), lambda t: (0, 0, t, 0, 0))
    return pl.pallas_call(
        _gdn_scan_kernel,
        grid=(T // tt,),
        in_specs=[full(GW), full(GW), mats, mats,
                  pl.BlockSpec((B, HEADS, nc, CHUNK), lambda t: (0, 0, t, 0)),
                  full(GW), pl.BlockSpec((1, DH), lambda t: (0, 0))],
        out_specs=full(GW),
        out_shape=jax.ShapeDtypeStruct((B, T, GW), ACT),
        scratch_shapes=[pltpu.VMEM((B, HEADS, DH, DH), F32)],
        compiler_params=pltpu.CompilerParams(dimension_semantics=("arbitrary",),
                                             vmem_limit_bytes=VMEM_LIMIT_BYTES),
        name="gdn_scan",
    )(o_part, qt, m, n, gct, sg, nw)


def _hgrn_kernel(q_ref, k_ref, v_ref, b_ref, sg_ref, nw_ref, o_ref, st_ref):
    nb, tt = q_ref.shape[0], q_ref.shape[1]
    nc = tt // CHUNK

    @pl.when(pl.program_id(0) == 0)
    def _():
        st_ref[...] = jnp.zeros(st_ref.shape, F32)

    row = lax.broadcasted_iota(jnp.int32, (CHUNK, CHUNK), 0)
    col = lax.broadcasted_iota(jnp.int32, (CHUNK, CHUNK), 1)
    diag_block = ((col // SUB) == (row // SUB)) & (col <= row)
    chains = [(bi, hh) for bi in range(nb) for hh in range(HEADS)]
    n = range(len(chains))
    hs = [slice(hh * DH, (hh + 1) * DH) for _, hh in chains]

    def body(c, carry):
        rows = pl.ds(pl.multiple_of(c * CHUNK, CHUNK), CHUNK)
        q = [q_ref[bi, rows, hs[j]].astype(F32) for j, (bi, _) in enumerate(chains)]
        k = [k_ref[bi, rows, hs[j]].astype(F32) for j, (bi, _) in enumerate(chains)]
        v = [v_ref[bi, rows, hs[j]].astype(F32) for j, (bi, _) in enumerate(chains)]
        b = [b_ref[bi, rows, hs[j]] for j, (bi, _) in enumerate(chains)]
        blast = [b[j][CHUNK - 1:CHUNK, :] for j in n]
        st = [st_ref[bi, hh] for bi, hh in chains]
        o = [_dot_nt(q[j] * jnp.exp(b[j]), st[j]) for j in n]
        k_tail = [k[j] * jnp.exp(blast[j] - b[j]) for j in n]
        for j, (bi, hh) in enumerate(chains):
            st_ref[bi, hh] = st[j] * jnp.exp(blast[j]) + _dot_tn(v[j], k_tail[j])

        blocks = [[jnp.zeros((SUB, CHUNK), F32)] for _ in n]
        for i in range(1, CHUNK // SUB):
            lo, hi = i * SUB, (i + 1) * SUB
            for j in n:
                r = b[j][lo:lo + 1, :]
                qi = q[j][lo:hi] * jnp.exp(b[j][lo:hi] - r)
                kj = k[j][:lo] * jnp.exp(jnp.minimum(r - b[j][:lo], 0.0))
                kj = jnp.concatenate([kj, jnp.zeros((CHUNK - lo, DH), F32)], axis=0)
                blocks[j].append(_dot_nt(qi, kj))
        a = []
        for j in n:
            f = jnp.exp(jnp.minimum(b[j] - pltpu.roll(b[j], 1, 0), 0.0))
            e = None
            a_diag = jnp.zeros((CHUNK, CHUNK), F32)
            for delta in range(SUB):
                if delta == 0:
                    term = q[j] * k[j]
                else:
                    fsh = f if delta == 1 else pltpu.roll(f, delta - 1, 0)
                    e = fsh if e is None else e * fsh
                    term = q[j] * pltpu.roll(k[j], delta, 0) * e
                colv = jnp.sum(term, axis=1, keepdims=True)
                a_diag = jnp.where(row - col == delta, colv, a_diag)
            a.append(jnp.where(diag_block, a_diag, jnp.concatenate(blocks[j], axis=0)))

        o = [o[j] + _dot(a[j], v[j]) for j in n]
        for bi in range(nb):
            ob = jnp.concatenate(o[bi * HEADS:(bi + 1) * HEADS], axis=1)
            ob = _rms(ob) * nw_ref[...] * sg_ref[bi, rows, :].astype(F32)
            o_ref[bi, rows, :] = ob.astype(o_ref.dtype)
        return carry

    lax.fori_loop(0, nc, body, 0)


def _hgrn(q, k, v, b, sg, nw, tt):
    B, T, _ = q.shape
    tile = pl.BlockSpec((B, tt, GW), lambda t: (0, t, 0))
    return pl.pallas_call(
        _hgrn_kernel,
        grid=(T // tt,),
        in_specs=[tile, tile, tile, tile, tile, pl.BlockSpec((1, GW), lambda t: (0, 0))],
        out_specs=tile,
        out_shape=jax.ShapeDtypeStruct((B, T, GW), ACT),
        scratch_shapes=[pltpu.VMEM((B, HEADS, DH, DH), F32)],
        compiler_params=pltpu.CompilerParams(dimension_semantics=("arbitrary",),
                                             vmem_limit_bytes=VMEM_LIMIT_BYTES),
        name="hgrn",
    )(q, k, v, b, sg, nw)


def _outproj_kernel(oa_ref, ob_ref, x_ref, wa_ref, wb_ref, g1_ref, n2_ref, sc_ref, sh_ref, wr_ref, rb_ref,
                    x1_ref, h2_ref, eidx_ref, rank_ref, wts_ref, cnt_ref):
    tm = x_ref.shape[1]

    @pl.when((pl.program_id(0) == 0) & (pl.program_id(1) == 0))
    def _():
        cnt_ref[...] = jnp.zeros(cnt_ref.shape, F32)

    mix = (jnp.dot(oa_ref[0], wa_ref[...], preferred_element_type=F32)
           + jnp.dot(ob_ref[0], wb_ref[...], preferred_element_type=F32))
    x1 = x_ref[0] + g1_ref[0] * mix
    x1_ref[0] = x1
    h2 = _rms(x1) * n2_ref[...]
    h2 = h2 * (1.0 + sc_ref[0]) + sh_ref[0]
    h2_ref[...] = _to_row_tiles(h2).astype(h2_ref.dtype)

    scores = jax.nn.sigmoid(_dot3(wr_ref[...], h2, dot=_dot_nt))
    sel = scores + rb_ref[...]
    sub = lax.broadcasted_iota(jnp.int32, (GROUP_SIZE, tm), 0)
    neg = -jnp.inf
    groups = range(N_GROUPS)

    def take_max(blk):
        m = jnp.max(blk, axis=0, keepdims=True)
        first = jnp.min(jnp.where(blk == m, sub, GROUP_SIZE), axis=0, keepdims=True)
        hit = sub == first
        return m, hit, jnp.where(hit, neg, blk)

    blk_of = lambda a, g: a[g * GROUP_SIZE:(g + 1) * GROUP_SIZE]
    sel_blk = [blk_of(sel, g) for g in groups]
    group_score = jnp.zeros((N_GROUPS, tm), F32)
    for g in groups:
        m1, _, rest = take_max(sel_blk[g])
        m2 = jnp.max(rest, axis=0, keepdims=True)
        group_score = jnp.where(sub == g, m1 + m2, group_score)
    group_on = jnp.zeros((N_GROUPS, tm), F32)
    for _ in range(TOPK_GROUPS):
        _, hit, group_score = take_max(group_score)
        group_on = jnp.where(hit, 1.0, group_on)

    cand = [jnp.where(group_on[g:g + 1] > 0.0, sel_blk[g], neg) for g in groups]
    picked = [jnp.zeros((GROUP_SIZE, tm), F32) for _ in groups]
    chosen = []
    for _ in range(TOP_K):
        m = jnp.max(functools.reduce(jnp.maximum, cand), axis=0, keepdims=True)
        first = functools.reduce(jnp.minimum, [jnp.where(cand[g] == m, sub + g * GROUP_SIZE, N_EXPERTS)
                                               for g in groups])
        first = jnp.min(first, axis=0, keepdims=True)
        chosen.append(first)
        for g in groups:
            hit = (sub + g * GROUP_SIZE) == first
            picked[g] = jnp.where(hit, 1.0, picked[g])
            cand[g] = jnp.where(hit, neg, cand[g])

    picked_all = jnp.concatenate(picked, axis=0)
    r_i = lax.broadcasted_iota(jnp.int32, (tm, tm), 0)
    c_i = lax.broadcasted_iota(jnp.int32, (tm, tm), 1)
    earlier = jnp.where(r_i < c_i, 1.0, 0.0).astype(BF16)
    before = jnp.dot(picked_all.astype(BF16), earlier, preferred_element_type=F32) + cnt_ref[:, 0:1]
    cnt_ref[...] = cnt_ref[...] + jnp.sum(picked_all, axis=1, keepdims=True)

    def pick_value(table, first):
        parts = [jnp.where((sub + g * GROUP_SIZE) == first, blk_of(table, g), 0.0) for g in groups]
        return jnp.sum(functools.reduce(jnp.add, parts), axis=0, keepdims=True)

    w_k = [pick_value(scores, f) for f in chosen]
    denom = functools.reduce(jnp.add, w_k)
    eidx = jnp.zeros((TOP_K, tm), jnp.int32)
    rank = jnp.zeros((TOP_K, tm), jnp.int32)
    wts = jnp.zeros((TOP_K, tm), F32)
    for k in range(TOP_K):
        eidx = jnp.where(sub == k, chosen[k], eidx)
        rank = jnp.where(sub == k, pick_value(before, chosen[k]).astype(jnp.int32), rank)
        wts = jnp.where(sub == k, w_k[k] / denom * ROUTE_SCALE, wts)
    eidx_ref[...] = eidx
    rank_ref[...] = rank
    pad = jnp.zeros((GATE_LANES - TOP_K, tm), F32)
    wts_ref[0] = jnp.concatenate([wts, pad], axis=0).T


def _outproj(oa, ob, x, wa, wb, g1, n2, sc2, sh2, wr_t, rb, tm):
    B, T, D = x.shape
    nt = T // tm
    const = lambda shape: pl.BlockSpec(shape, lambda b, t: (0,) * len(shape))
    tile = lambda w: pl.BlockSpec((1, tm, w), lambda b, t: (b, t, 0))
    per_batch = pl.BlockSpec((1, 1, D), lambda b, t: (b, 0, 0))
    picks = pl.BlockSpec((TOP_K, tm), lambda b, t: (0, b * nt + t))
    return pl.pallas_call(
        _outproj_kernel,
        grid=(B, nt),
        in_specs=[tile(GW), tile(GW), tile(D), const(wa.shape), const(wb.shape), per_batch,
                  const((1, D)), per_batch, per_batch, const(wr_t.shape), const(rb.shape)],
        out_specs=[tile(D), pl.BlockSpec((tm, ROW_TILE, LANES), lambda b, t: (b * nt + t, 0, 0)),
                   picks, picks, tile(GATE_LANES), const((N_EXPERTS, GATE_LANES))],
        out_shape=[jax.ShapeDtypeStruct((B, T, D), F32), jax.ShapeDtypeStruct((B * T, ROW_TILE, LANES), ROW_DTYPE),
                   jax.ShapeDtypeStruct((TOP_K, B * T), jnp.int32), jax.ShapeDtypeStruct((TOP_K, B * T), jnp.int32),
                   jax.ShapeDtypeStruct((B, T, GATE_LANES), F32),
                   jax.ShapeDtypeStruct((N_EXPERTS, GATE_LANES), F32)],
        compiler_params=pltpu.CompilerParams(dimension_semantics=("arbitrary", "arbitrary"),
                                             vmem_limit_bytes=VMEM_LIMIT_BYTES),
        name="outproj",
    )(oa, ob, x, wa, wb, g1, n2, sc2, sh2, wr_t, rb)


def _slots_kernel(off_ref, eidx_ref, rank_ref, slot_ref):
    eidx = eidx_ref[...]

    def add_expert(e, acc):
        return acc + jnp.where(eidx == e, off_ref[e], 0)

    slot_ref[...] = lax.fori_loop(0, N_EXPERTS, add_expert, rank_ref[...])


def _slots(offsets, eidx, rank):
    k, m = eidx.shape
    tile = pl.BlockSpec((k, m), lambda i: (0, 0))
    return pl.pallas_call(
        _slots_kernel,
        grid=(1,),
        in_specs=[pl.BlockSpec(memory_space=pltpu.SMEM), tile, tile],
        out_specs=tile,
        out_shape=jax.ShapeDtypeStruct((k, m), jnp.int32),
        compiler_params=pltpu.CompilerParams(dimension_semantics=("arbitrary",),
                                             vmem_limit_bytes=VMEM_LIMIT_BYTES),
        name="slots",
    )(offsets, eidx, rank)


def _dispatch_kernel(slot_ref, h_ref, xs_ref, sem):
    tmd = h_ref.shape[0]

    def start_rows(j, c):
        for k in range(TOP_K):
            pltpu.make_async_copy(h_ref.at[pl.ds(j, 1)], xs_ref.at[pl.ds(slot_ref[k, j], 1)], sem).start(
                priority=k % 2)
        return c

    lax.fori_loop(0, tmd, start_rows, 0)
    for _ in range(TOP_K):
        pltpu.make_async_copy(h_ref, xs_ref.at[pl.ds(0, tmd)], sem).wait()


def _dispatch(slot, h2, n_rows, tmd):
    M = h2.shape[0]
    return pl.pallas_call(
        _dispatch_kernel,
        grid=(M // tmd,),
        in_specs=[pl.BlockSpec((TOP_K, tmd), lambda i: (0, i), memory_space=pltpu.SMEM),
                  pl.BlockSpec((tmd, ROW_TILE, LANES), lambda i: (i, 0, 0))],
        out_specs=pl.BlockSpec(memory_space=pl.ANY),
        out_shape=jax.ShapeDtypeStruct((n_rows, ROW_TILE, LANES), h2.dtype),
        scratch_shapes=[pltpu.SemaphoreType.DMA],
        compiler_params=pltpu.CompilerParams(dimension_semantics=("arbitrary",),
                                             vmem_limit_bytes=VMEM_LIMIT_BYTES),
        name="dispatch",
    )(slot, h2)


FFN_PIPELINE_STEPS = 2


def _ffn_kernel(te_ref, nu_ref, nv_ref, x_ref, wg_ref, wu_ref, wd_ref, y_ref, wgu_s, wd_s, xstd, ystd):
    i = pl.program_id(0)
    row = lax.broadcasted_iota(jnp.int32, (MOE_ROWS, 1), 0)
    tile_mm = jnp.maximum(i - 1, 0)
    tile_out = jnp.maximum(i - 2, 0)

    @pl.when(i == 0)
    def _():
        xstd[...] = jnp.zeros(xstd.shape, BF16)
        ystd[...] = jnp.zeros(ystd.shape, F32)

    @pl.when(i < nu_ref[0] + FFN_PIPELINE_STEPS)
    def _():
        @pl.when((i == 0) | (te_ref[tile_mm] != te_ref[tile_out]))
        def _():
            wgu_s[:, :D_EXPERT] = wg_ref[0].astype(BF16)
            wgu_s[:, D_EXPERT:] = wu_ref[0].astype(BF16)
            wd_s[...] = wd_ref[0].astype(BF16)

        cur = i % 2
        xstd[cur] = jnp.where(row < nv_ref[i], _to_rows(x_ref[...].astype(F32)), 0.0).astype(BF16)
        y_ref[...] = _to_row_tiles(ystd[cur]).astype(y_ref.dtype)
        gu = jnp.dot(xstd[1 - cur], wgu_s[...], preferred_element_type=F32)
        act = _silu(gu[:, :D_EXPERT]) * gu[:, D_EXPERT:]
        ystd[1 - cur] = jnp.dot(act.astype(BF16), wd_s[...], preferred_element_type=F32)


def _ffn(tile_expert, n_used, n_valid, xs, wg, wu, wd):
    D = wg.shape[1]
    n_tiles = xs.shape[0] // MOE_ROWS
    block = (MOE_ROWS, ROW_TILE, LANES)
    expert = lambda i, te, nu, nv: (te[jnp.maximum(i - 1, 0)], 0, 0)
    return pl.pallas_call(
        _ffn_kernel,
        grid_spec=pltpu.PrefetchScalarGridSpec(
            num_scalar_prefetch=3,
            grid=(n_tiles + FFN_PIPELINE_STEPS,),
            in_specs=[pl.BlockSpec(block, lambda i, te, nu, nv: (jnp.minimum(i, nu[0] - 1), 0, 0)),
                      pl.BlockSpec((1, D, D_EXPERT), expert),
                      pl.BlockSpec((1, D, D_EXPERT), expert),
                      pl.BlockSpec((1, D_EXPERT, D), expert)],
            out_specs=pl.BlockSpec(block, lambda i, te, nu, nv: (jnp.clip(i - 2, 0, nu[0] - 1), 0, 0)),
            scratch_shapes=[pltpu.VMEM((D, 2 * D_EXPERT), BF16), pltpu.VMEM((D_EXPERT, D), BF16),
                            pltpu.VMEM((2, MOE_ROWS, D), BF16), pltpu.VMEM((2, MOE_ROWS, D), F32)]),
        out_shape=jax.ShapeDtypeStruct(xs.shape, xs.dtype),
        compiler_params=pltpu.CompilerParams(dimension_semantics=("arbitrary",),
                                             vmem_limit_bytes=VMEM_LIMIT_BYTES),
        name="ffn",
    )(tile_expert, n_used, n_valid, xs, wg, wu, wd)


def _combine_kernel(slot_ref, next_slot_ref, w_ref, h_ref, x1_ref, g2_ref, fw_ref, wsgu_ref, wsd_ref, y_ref,
                    o_ref, buf, sem):
    tmc = h_ref.shape[0]
    i = pl.program_id(0)
    last = pl.num_programs(0) - 1
    assert TOP_K == ROW_TILE

    def start_rows(slots, b):
        def body(j, c):
            for k in range(TOP_K):
                pltpu.make_async_copy(y_ref.at[pl.ds(slots[k, j], 1)], buf.at[b, k, pl.ds(j, 1)], sem.at[b]).start(
                    priority=k % 2)
            return c
        lax.fori_loop(0, tmc, body, 0)

    def wait_rows(b):
        for k in range(TOP_K):
            pltpu.make_async_copy(y_ref.at[pl.ds(0, tmc)], buf.at[b, k], sem.at[b]).wait()

    def for_buffer(step, fn):
        for b in range(2):
            @pl.when(step % 2 == b)
            def _():
                fn(b)

    @pl.when(i == 0)
    def _():
        start_rows(slot_ref, 0)

    @pl.when(i < last)
    def _():
        for_buffer(i + 1, lambda b: start_rows(next_slot_ref, b))

    gu = jnp.dot(_to_rows(h_ref[...].astype(F32)).astype(BF16), wsgu_ref[...], preferred_element_type=F32)
    act = _silu(gu[:, :D_EXPERT]) * gu[:, D_EXPERT:]
    shared = jnp.dot(act.astype(BF16), wsd_ref[...], preferred_element_type=F32)
    w = w_ref[...]
    lane = lax.broadcasted_iota(jnp.int32, w.shape, 1)

    def finish(b):
        wait_rows(b)
        w3 = _to_row_tiles(jnp.concatenate(
            [jnp.broadcast_to(_lane_pick(w, lane, k), (tmc, LANES)) for k in range(TOP_K)], axis=1))
        routed = jnp.zeros((tmc, ROW_TILE, LANES), F32)
        for k in range(TOP_K):
            routed = routed + jnp.broadcast_to(w3[:, k:k + 1, :], w3.shape) * buf[b, k].astype(F32)
        acc = shared + _to_rows(routed)
        y = x1_ref[...] + g2_ref[0] * acc
        o_ref[...] = _rms(y) * fw_ref[...]

    for_buffer(i, finish)


def _combine(slot, wts, h2, x1, g2, fw, wsgu, wsd, ys, T, tmc):
    M, D = x1.shape
    const = lambda shape: pl.BlockSpec(shape, lambda i: (0,) * len(shape))
    tile = lambda w: pl.BlockSpec((tmc, w), lambda i: (i, 0))
    row_tiled = pl.BlockSpec((tmc, ROW_TILE, LANES), lambda i: (i, 0, 0))
    n_steps = M // tmc
    return pl.pallas_call(
        _combine_kernel,
        grid=(n_steps,),
        in_specs=[pl.BlockSpec((TOP_K, tmc), lambda i: (0, i), memory_space=pltpu.SMEM),
                  pl.BlockSpec((TOP_K, tmc), lambda i: (0, jnp.minimum(i + 1, n_steps - 1)), memory_space=pltpu.SMEM),
                  tile(GATE_LANES), row_tiled, tile(D),
                  pl.BlockSpec((1, 1, D), lambda i: (i // (T // tmc), 0, 0)),
                  const((1, D)), const(wsgu.shape), const(wsd.shape),
                  pl.BlockSpec(memory_space=pl.ANY)],
        out_specs=tile(D),
        out_shape=jax.ShapeDtypeStruct((M, D), F32),
        scratch_shapes=[pltpu.VMEM((2, TOP_K, tmc, ROW_TILE, LANES), ys.dtype), pltpu.SemaphoreType.DMA((2,))],
        compiler_params=pltpu.CompilerParams(dimension_semantics=("arbitrary",),
                                             vmem_limit_bytes=VMEM_LIMIT_BYTES),
        name="combine",
    )(slot, slot, wts, h2, x1, g2, fw, wsgu, wsd, ys)


def _pick_tile(n, want):
    t = min(n, want)
    assert n % t == 0 and t % CHUNK == 0, (n, want)
    return t


def kernel(x, c, w_ada, b_ada, norm1_w, w_in, conv_w, gdn_a_log, gdn_dt_bias, gdn_norm_w, hg_lb, hg_norm_w,
           w_out, norm2_w, w_router, router_bias, w_gate, w_up, w_down, ws_gate, ws_up, ws_down, final_norm_w):
    B, T, D = x.shape
    M = B * T
    depth = w_ada.shape[0]
    assert depth == 1 and T % CHUNK == 0 and B <= 8
    layer = 0
    tt = _pick_tile(T, TILE_RECURRENCE)

    c_pad = jnp.pad(c, ((0, 8 - B), (0, 0)))
    mod = _ada(c_pad, w_ada[layer], b_ada[layer].reshape(1, -1))[:B]
    sh1, sc1, g1, sh2, sc2, g2 = (m.reshape(B, 1, D) for m in jnp.split(mod, 6, axis=-1))

    w = w_in[layer]
    qkv_w = 3 * GW
    sizes = (GW, HEADS, HEADS, GW, GW, GW, GW)
    offs = [qkv_w]
    for s in sizes:
        offs.append(offs[-1] + s)
    seg = lambda i: w[:, offs[i]:offs[i + 1]]
    small = jnp.pad(jnp.concatenate([seg(1), seg(2)], axis=1), ((0, 0), (0, GATE_LANES - 2 * HEADS)))
    w_all = jnp.concatenate([w[:, :qkv_w], seg(0), seg(3), seg(4), seg(5), seg(6), small], axis=1).astype(BF16)
    lane_pad = lambda v: jnp.pad(v.astype(F32).reshape(1, HEADS), ((0, 0), (HEADS, GATE_LANES - 2 * HEADS)))
    idx = jnp.arange(CUMSUM_ROWS)
    tri = ((idx[:, None] >= idx[None, :]) & (idx[:, None] // CHUNK == idx[None, :] // CHUNK)).astype(BF16)

    qa, ka, va, ga, sm, bcum, kb, ib, qb, gb = _inproj(
        layer, x, norm1_w[layer].reshape(1, D), sc1, sh1, w_all, conv_w[layer].astype(F32),
        lane_pad(gdn_a_log[layer]), lane_pad(gdn_dt_bias[layer]), hg_lb.astype(F32), tri,
        _pick_tile(T, TILE_INPROJ))

    gct = sm[:, :, HEADS:2 * HEADS].transpose(0, 2, 1).reshape(B, HEADS, T // CHUNK, CHUNK)
    oa = _gdn(qa, ka, va, ga, sm, gct, gdn_norm_w[layer].reshape(1, DH), tt)
    ob = _hgrn(qb, kb, ib, bcum, gb, hg_norm_w[layer].reshape(1, GW), tt)

    wo = w_out[layer].astype(BF16)
    x1, h2, eidx, rank, wts, cnt = _outproj(oa, ob, x, wo[:GW], wo[GW:], g1, norm2_w[layer].reshape(1, D), sc2, sh2,
                                            w_router[layer].T, router_bias[layer].reshape(N_EXPERTS, 1),
                                            _pick_tile(T, TILE_OUTPROJ))

    counts = cnt[:, 0].astype(jnp.int32)
    padded = (counts + MOE_ROWS - 1) // MOE_ROWS * MOE_ROWS
    ends = jnp.cumsum(padded)
    offsets = ends - padded
    n_tiles = (M * TOP_K) // MOE_ROWS + N_EXPERTS
    n_used = (ends[-1] // MOE_ROWS).astype(jnp.int32)
    tile_ids = jnp.minimum(jnp.arange(n_tiles + FFN_PIPELINE_STEPS, dtype=jnp.int32), n_used - 1)
    tile_expert = jnp.sum(((ends // MOE_ROWS)[None, :] <= tile_ids[:, None]).astype(jnp.int32), axis=1)
    tile_expert = jnp.minimum(tile_expert, N_EXPERTS - 1)
    slot = _slots(offsets.astype(jnp.int32), eidx, rank)
    first_tile = jnp.take(offsets, tile_expert) // MOE_ROWS
    n_valid = jnp.clip(jnp.take(counts, tile_expert) - (tile_ids - first_tile) * MOE_ROWS, 0, MOE_ROWS)

    assert D == ROW_TILE * LANES
    xs = _dispatch(slot, h2, n_tiles * MOE_ROWS, _pick_tile(M, TILE_DISPATCH))
    ys = _ffn(tile_expert.astype(jnp.int32), n_used.reshape(1), n_valid.astype(jnp.int32), xs,
              w_gate[layer], w_up[layer], w_down[layer])
    wsgu = jnp.concatenate([ws_gate[layer], ws_up[layer]], axis=-1).astype(BF16)
    out = _combine(slot, wts.reshape(M, GATE_LANES), h2, x1.reshape(M, D), g2, final_norm_w.reshape(1, D),
                   wsgu, ws_down[layer].astype(BF16), ys, T, _pick_tile(T, TILE_COMBINE))
    return out.reshape(B, T, D)
```

```python
import functools

import jax
import jax.numpy as jnp
from jax import lax
from jax.experimental import pallas as pl
from jax.experimental.pallas import tpu as pltpu

F32 = jnp.float32
BF16 = jnp.bfloat16

EPS = 1e-6
CHUNK = 64
SUB = 8
HEADS = 4
DH = 128
GW = HEADS * DH
CONV_K = 4
N_EXPERTS = 64
N_GROUPS = 8
GROUP_SIZE = N_EXPERTS // N_GROUPS
TOPK_GROUPS = 4
TOP_K = 8
D_EXPERT = 256
ROUTE_SCALE = 2.5
GATE_LANES = 128
GDN_CHUNKS_PER_ITER = 8
INV_BLOCK = 16
CUMSUM_ROWS = 128
MOE_ROWS = 1024
ROW_TILE, LANES = 8, 128

TILE_INPROJ = 1024
TILE_RECURRENCE = 1024
TILE_OUTPROJ = 1024
TILE_DISPATCH = 1024
TILE_COMBINE = 256
ADA_COLS = 1024

VMEM_LIMIT_BYTES = 56 * 1024 * 1024

ACT = BF16
ROW_DTYPE = F32


def _silu(x):
    return x * jax.nn.sigmoid(x)


def _dot(a, b):
    return jnp.dot(a.astype(BF16), b.astype(BF16), preferred_element_type=F32)


def _dot_nt(a, b):
    return lax.dot_general(a.astype(BF16), b.astype(BF16), (((1,), (1,)), ((), ())),
                           preferred_element_type=F32)


def _dot_tn(a, b):
    return lax.dot_general(a.astype(BF16), b.astype(BF16), (((0,), (0,)), ((), ())),
                           preferred_element_type=F32)


def _split2(x):
    hi = x.astype(BF16)
    lo = (x - hi.astype(F32)).astype(BF16)
    return hi, lo


def _dot3(a, b, dot=_dot):
    ah, al = _split2(a)
    bh, bl = _split2(b)
    return dot(ah, bh) + dot(ah, bl) + dot(al, bh)


def _cumsum_rows(tri, x):
    hi = x.astype(BF16)
    r = x - hi.astype(F32)
    mid = r.astype(BF16)
    lo = (r - mid.astype(F32)).astype(BF16)
    g = tri.shape[0]
    groups = []
    for r0 in range(0, x.shape[0], g):
        rows = slice(r0, r0 + g)
        groups.append(jnp.dot(tri, hi[rows], preferred_element_type=F32)
                      + jnp.dot(tri, mid[rows], preferred_element_type=F32)
                      + jnp.dot(tri, lo[rows], preferred_element_type=F32))
    return jnp.concatenate(groups, axis=0)


def _lane_pick(tile, lane, idx):
    return jnp.sum(jnp.where(lane == idx, tile, 0.0), axis=1, keepdims=True)


def _rms(x):
    return x * lax.rsqrt(jnp.mean(x * x, axis=-1, keepdims=True) + EPS)


def _to_rows(x3):
    r = x3.shape[0]
    xt = jnp.swapaxes(x3.reshape(r // ROW_TILE, ROW_TILE, ROW_TILE, LANES), 1, 2)
    return jnp.concatenate([xt[:, s].reshape(r, LANES) for s in range(ROW_TILE)], axis=1)


def _to_row_tiles(x):
    r = x.shape[0]
    xt = jnp.stack([x[:, s * LANES:(s + 1) * LANES].reshape(r // ROW_TILE, ROW_TILE, LANES) for s in range(ROW_TILE)],
                   axis=1)
    return jnp.swapaxes(xt, 1, 2).reshape(r, ROW_TILE, LANES)


def _ada_kernel(c_ref, w_ref, b_ref, o_ref):
    ca = _silu(c_ref[...])
    o_ref[...] = _dot3(ca, w_ref[...]) + b_ref[...]


def _ada(c_pad, w, b):
    rows, d = c_pad.shape
    n = w.shape[1]
    tn = ADA_COLS
    return pl.pallas_call(
        _ada_kernel,
        grid=(n // tn,),
        in_specs=[pl.BlockSpec((rows, d), lambda j: (0, 0)),
                  pl.BlockSpec((d, tn), lambda j: (0, j)),
                  pl.BlockSpec((1, tn), lambda j: (0, j))],
        out_specs=pl.BlockSpec((rows, tn), lambda j: (0, j)),
        out_shape=jax.ShapeDtypeStruct((rows, n), F32),
        compiler_params=pltpu.CompilerParams(dimension_semantics=("arbitrary",),
                                             vmem_limit_bytes=VMEM_LIMIT_BYTES),
        name="ada",
    )(c_pad, w, b)


def _inproj_kernel(layer, x_ref, n1_ref, sc_ref, sh_ref, w_ref, cw_ref, alog_ref, dt_ref, lb_ref, tri_ref,
                   qa_ref, ka_ref, va_ref, ga_ref, sm_ref, b_ref, kb_ref, ib_ref, qb_ref, gb_ref,
                   pbuf):
    tt = x_ref.shape[1]
    t = pl.program_id(1)

    h = _rms(x_ref[0]) * n1_ref[...]
    h = h * (1.0 + sc_ref[0]) + sh_ref[0]
    hb = h.astype(BF16)

    def proj(g, width=GW):
        return jnp.dot(hb, w_ref[:, g * GW:g * GW + width], preferred_element_type=F32)

    @pl.when(t == 0)
    def _():
        pbuf[:, 0:8, :] = jnp.zeros((3, 8, GW), F32)

    for g, out_ref in enumerate((qa_ref, ka_ref, va_ref)):
        cols = slice(g * GW, (g + 1) * GW)
        p = proj(g)
        pbuf[g, 8:8 + tt, :] = p
        y = p * cw_ref[CONV_K - 1:CONV_K, cols]
        for j in range(1, CONV_K):
            y = y + pbuf[g, 8 - j:8 - j + tt, :] * cw_ref[CONV_K - 1 - j:CONV_K - j, cols]
        pbuf[g, 0:8, :] = pbuf[g, tt:tt + 8, :]
        y = _silu(y)
        if g == 2:
            out_ref[0] = y.astype(out_ref.dtype)
        else:
            scale = DH ** -0.5 if g == 0 else 1.0
            for hh in range(HEADS):
                hs = slice(hh * DH, (hh + 1) * DH)
                yh = y[:, hs]
                inv = lax.rsqrt(jnp.sum(yh * yh, axis=-1, keepdims=True) + EPS)
                out_ref[0, :, hs] = (yh * inv * scale).astype(out_ref.dtype)

    ga_ref[0] = _silu(proj(3)).astype(ga_ref.dtype)

    ps = proj(8, GATE_LANES)
    lane = lax.broadcasted_iota(jnp.int32, ps.shape, 1)
    beta = jax.nn.sigmoid(ps)
    z = ps + dt_ref[...]
    softplus = jnp.maximum(z, 0.0) + jnp.log1p(jnp.exp(-jnp.abs(z)))
    g_log = -jnp.exp(alog_ref[...]) * softplus
    tri = tri_ref[...]
    gc = _cumsum_rows(tri, jnp.where((lane >= HEADS) & (lane < 2 * HEADS), g_log, 0.0))
    sm_ref[0] = jnp.where(lane < HEADS, beta, gc)

    hl = lb_ref[...]
    e = jnp.exp(hl - jnp.max(hl, axis=0, keepdims=True))
    lb = jnp.sum(e[0:layer + 1], axis=0, keepdims=True) / jnp.sum(e, axis=0, keepdims=True)
    fr = proj(4)
    logf = jnp.log(lb + (1.0 - lb) * jax.nn.sigmoid(fr))
    b_ref[0] = _cumsum_rows(tri, logf)
    kb_ref[0] = ((1.0 - lb) * jax.nn.sigmoid(-fr)).astype(kb_ref.dtype)
    ib_ref[0] = proj(5).astype(ib_ref.dtype)
    qb_ref[0] = _silu(proj(6)).astype(qb_ref.dtype)
    gb_ref[0] = _silu(proj(7)).astype(gb_ref.dtype)


def _inproj(layer, x, n1, sc1, sh1, w_all, conv_w, alog_pad, dt_pad, hg_lb, tri, tt):
    B, T, D = x.shape
    const = lambda shape: pl.BlockSpec(shape, lambda b, t: (0,) * len(shape), pipeline_mode=pl.Buffered(1))
    act = lambda dt: jax.ShapeDtypeStruct((B, T, GW), dt)
    tile = lambda w: pl.BlockSpec((1, tt, w), lambda b, t: (b, t, 0))
    per_batch = pl.BlockSpec((1, 1, D), lambda b, t: (b, 0, 0))
    return pl.pallas_call(
        functools.partial(_inproj_kernel, layer),
        grid=(B, T // tt),
        in_specs=[tile(D), const((1, D)), per_batch, per_batch,
                  const(w_all.shape), const(conv_w.shape), const((1, GATE_LANES)), const((1, GATE_LANES)),
                  const(hg_lb.shape), const(tri.shape)],
        out_specs=[tile(GW), tile(GW), tile(GW), tile(GW), tile(GATE_LANES), tile(GW),
                   tile(GW), tile(GW), tile(GW), tile(GW)],
        out_shape=[act(ACT), act(ACT), act(ACT), act(ACT),
                   jax.ShapeDtypeStruct((B, T, GATE_LANES), F32), act(F32),
                   act(ACT), act(ACT), act(ACT), act(ACT)],
        scratch_shapes=[pltpu.VMEM((3, tt + 8, GW), F32)],
        compiler_params=pltpu.CompilerParams(dimension_semantics=("arbitrary", "arbitrary"),
                                             vmem_limit_bytes=VMEM_LIMIT_BYTES),
        name="inproj",
    )(x, n1, sc1, sh1, w_all, conv_w, alog_pad, dt_pad, hg_lb, tri)


def _gdn_prep_kernel(q_ref, k_ref, v_ref, sm_ref, gct_ref, o_ref, qt_ref, m_ref, n_ref):
    tt = q_ref.shape[1]
    nc = tt // CHUNK
    row = lax.broadcasted_iota(jnp.int32, (CHUNK, CHUNK), 0)
    col = lax.broadcasted_iota(jnp.int32, (CHUNK, CHUNK), 1)
    causal = row >= col
    diag_blk = (row > col) & (row // INV_BLOCK == col // INV_BLOCK)
    off_blk = row // INV_BLOCK > col // INV_BLOCK
    eye = jnp.where(row == col, 1.0, 0.0)
    lane = lax.broadcasted_iota(jnp.int32, (CHUNK, GATE_LANES), 1)
    assert INV_BLOCK == 16 and CHUNK == 4 * INV_BLOCK

    def body(i, carry):
        chains = [(GDN_CHUNKS_PER_ITER * i + j, hh) for j in range(GDN_CHUNKS_PER_ITER) for hh in range(HEADS)]
        rows = [pl.ds(pl.multiple_of(c * CHUNK, CHUNK), CHUNK) for c, _ in chains]
        hs = [slice(hh * DH, (hh + 1) * DH) for _, hh in chains]
        n = range(len(chains))
        sm = [sm_ref[0, rows[j], :] for j in n]
        q = [q_ref[0, rows[j], hs[j]].astype(F32) for j in n]
        k = [k_ref[0, rows[j], hs[j]].astype(F32) for j in n]
        v = [v_ref[0, rows[j], hs[j]].astype(F32) for j in n]
        beta = [_lane_pick(sm[j], lane, chains[j][1]) for j in n]
        gcol = [_lane_pick(sm[j], lane, HEADS + chains[j][1]) for j in n]
        grow = [gct_ref[0, hh, pl.ds(c, 1), :] for c, hh in chains]
        decay = [jnp.exp(jnp.where(causal, gcol[j] - grow[j], -jnp.inf)) for j in n]
        kb = [k[j] * beta[j] for j in n]
        L = [_dot_nt(kb[j], k[j]) * decay[j] for j in n]
        dg = [jnp.where(diag_blk, L[j], 0.0) for j in n]
        off = [jnp.where(off_blk, L[j], 0.0) for j in n]
        dinv = [eye - dg[j] for j in n]
        pw = [_dot3(dg[j], dg[j]) for j in n]
        for _ in range(2):
            dinv = [dinv[j] + _dot3(dinv[j], pw[j]) for j in n]
            pw = [_dot3(pw[j], pw[j]) for j in n]
        dinv = [dinv[j] + _dot3(dinv[j], pw[j]) for j in n]
        f1 = [_dot(dinv[j], off[j]) for j in n]
        f2 = [_dot(f1[j], f1[j]) for j in n]
        f3 = [_dot(f1[j], f2[j]) for j in n]
        tinv = [_dot(eye - f1[j] + f2[j] - f3[j], dinv[j]) for j in n]
        eg = [jnp.exp(gcol[j]) for j in n]
        sol = [_dot(tinv[j], jnp.concatenate([v[j] * beta[j], kb[j] * eg[j]], axis=1)) for j in n]
        attn = [_dot_nt(q[j], k[j]) * decay[j] for j in n]
        k_tail = [k[j] * jnp.exp(gcol[j][CHUNK - 1:CHUNK, :] - gcol[j]) for j in n]
        au = [_dot(attn[j], sol[j]) for j in n]
        ku = [_dot_tn(k_tail[j], sol[j]) for j in n]
        for j, (c, hh) in enumerate(chains):
            o_ref[0, rows[j], hs[j]] = au[j][:, :DH]
            qt_ref[0, rows[j], hs[j]] = (q[j] * eg[j] - au[j][:, DH:]).astype(qt_ref.dtype)
            n_ref[0, hh, c] = ku[j][:, :DH].astype(n_ref.dtype)
            m_ref[0, hh, c] = (-ku[j][:, DH:]).astype(m_ref.dtype)
        return carry

    lax.fori_loop(0, nc // GDN_CHUNKS_PER_ITER, body, 0)


def _gdn_scan_kernel(o_ref, qt_ref, m_ref, n_ref, gct_ref, sg_ref, nw_ref, out_ref, s_ref):
    nb, tt = o_ref.shape[0], o_ref.shape[1]
    nc = tt // CHUNK

    @pl.when(pl.program_id(0) == 0)
    def _():
        s_ref[...] = jnp.zeros(s_ref.shape, F32)

    nw = nw_ref[...]

    def body(c, carry):
        rows = pl.ds(pl.multiple_of(c * CHUNK, CHUNK), CHUNK)
        for b in range(nb):
            for hh in range(HEADS):
                hs = slice(hh * DH, (hh + 1) * DH)
                S = s_ref[b, hh]
                Sb = S.astype(BF16)
                glast = gct_ref[b, hh, pl.ds(c, 1), :][:, CHUNK - 1:CHUNK]
                o = o_ref[b, rows, hs] + jnp.dot(qt_ref[b, rows, hs], Sb, preferred_element_type=F32)
                s_ref[b, hh] = (S * jnp.exp(glast) + jnp.dot(m_ref[b, hh, c], Sb, preferred_element_type=F32)
                                + n_ref[b, hh, c].astype(F32))
                o = _rms(o) * nw * sg_ref[b, rows, hs].astype(F32)
                out_ref[b, rows, hs] = o.astype(out_ref.dtype)
        return carry

    lax.fori_loop(0, nc, body, 0)


def _gdn(q, k, v, sg, sm, gct, nw, tt):
    B, T, _ = q.shape
    nc = tt // CHUNK
    assert nc % GDN_CHUNKS_PER_ITER == 0
    n_chunks = T // CHUNK
    tile = lambda w: pl.BlockSpec((1, tt, w), lambda b, t: (b, t, 0))
    mat = jax.ShapeDtypeStruct((B, HEADS, n_chunks, DH, DH), ACT)
    o_part, qt, m, n = pl.pallas_call(
        _gdn_prep_kernel,
        grid=(B, T // tt),
        in_specs=[tile(GW), tile(GW), tile(GW), tile(GATE_LANES),
                  pl.BlockSpec((1, HEADS, nc, CHUNK), lambda b, t: (b, 0, t, 0))],
        out_specs=[tile(GW), tile(GW),
                   pl.BlockSpec((1, HEADS, nc, DH, DH), lambda b, t: (b, 0, t, 0, 0)),
                   pl.BlockSpec((1, HEADS, nc, DH, DH), lambda b, t: (b, 0, t, 0, 0))],
        out_shape=[jax.ShapeDtypeStruct((B, T, GW), F32), jax.ShapeDtypeStruct((B, T, GW), ACT), mat, mat],
        compiler_params=pltpu.CompilerParams(dimension_semantics=("arbitrary", "arbitrary"),
                                             vmem_limit_bytes=VMEM_LIMIT_BYTES),
        name="gdn_prep",
    )(q, k, v, sm, gct)

    full = lambda w: pl.BlockSpec((B, tt, w), lambda t: (0, t, 0))
    mats = pl.BlockSpec((B, HEADS, nc, DH, DH), lambda t: (0, 0, t, 0, 0))
    return pl.pallas_call(
        _gdn_scan_kernel,
        grid=(T // tt,),
        in_specs=[full(GW), full(GW), mats, mats,
                  pl.BlockSpec((B, HEADS, nc, CHUNK), lambda t: (0, 0, t, 0)),
                  full(GW), pl.BlockSpec((1, DH), lambda t: (0, 0))],
        out_specs=full(GW),
        out_shape=jax.ShapeDtypeStruct((B, T, GW), ACT),
        scratch_shapes=[pltpu.VMEM((B, HEADS, DH, DH), F32)],
        compiler_params=pltpu.CompilerParams(dimension_semantics=("arbitrary",),
                                             vmem_limit_bytes=VMEM_LIMIT_BYTES),
        name="gdn_scan",
    )(o_part, qt, m, n, gct, sg, nw)


def _hgrn_kernel(q_ref, k_ref, v_ref, b_ref, sg_ref, nw_ref, o_ref, st_ref):
    nb, tt = q_ref.shape[0], q_ref.shape[1]
    nc = tt // CHUNK

    @pl.when(pl.program_id(0) == 0)
    def _():
        st_ref[...] = jnp.zeros(st_ref.shape, F32)

    row = lax.broadcasted_iota(jnp.int32, (CHUNK, CHUNK), 0)
    col = lax.broadcasted_iota(jnp.int32, (CHUNK, CHUNK), 1)
    diag_block = ((col // SUB) == (row // SUB)) & (col <= row)
    chains = [(bi, hh) for bi in range(nb) for hh in range(HEADS)]
    n = range(len(chains))
    hs = [slice(hh * DH, (hh + 1) * DH) for _, hh in chains]

    def body(c, carry):
        rows = pl.ds(pl.multiple_of(c * CHUNK, CHUNK), CHUNK)
        q = [q_ref[bi, rows, hs[j]].astype(F32) for j, (bi, _) in enumerate(chains)]
        k = [k_ref[bi, rows, hs[j]].astype(F32) for j, (bi, _) in enumerate(chains)]
        v = [v_ref[bi, rows, hs[j]].astype(F32) for j, (bi, _) in enumerate(chains)]
        b = [b_ref[bi, rows, hs[j]] for j, (bi, _) in enumerate(chains)]
        blast = [b[j][CHUNK - 1:CHUNK, :] for j in n]
        st = [st_ref[bi, hh] for bi, hh in chains]
        o = [_dot_nt(q[j] * jnp.exp(b[j]), st[j]) for j in n]
        k_tail = [k[j] * jnp.exp(blast[j] - b[j]) for j in n]
        for j, (bi, hh) in enumerate(chains):
            st_ref[bi, hh] = st[j] * jnp.exp(blast[j]) + _dot_tn(v[j], k_tail[j])

        blocks = [[jnp.zeros((SUB, CHUNK), F32)] for _ in n]
        for i in range(1, CHUNK // SUB):
            lo, hi = i * SUB, (i + 1) * SUB
            for j in n:
                r = b[j][lo:lo + 1, :]
                qi = q[j][lo:hi] * jnp.exp(b[j][lo:hi] - r)
                kj = k[j][:lo] * jnp.exp(jnp.minimum(r - b[j][:lo], 0.0))
                kj = jnp.concatenate([kj, jnp.zeros((CHUNK - lo, DH), F32)], axis=0)
                blocks[j].append(_dot_nt(qi, kj))
        a = []
        for j in n:
            f = jnp.exp(jnp.minimum(b[j] - pltpu.roll(b[j], 1, 0), 0.0))
            e = None
            a_diag = jnp.zeros((CHUNK, CHUNK), F32)
            for delta in range(SUB):
                if delta == 0:
                    term = q[j] * k[j]
                else:
                    fsh = f if delta == 1 else pltpu.roll(f, delta - 1, 0)
                    e = fsh if e is None else e * fsh
                    term = q[j] * pltpu.roll(k[j], delta, 0) * e
                colv = jnp.sum(term, axis=1, keepdims=True)
                a_diag = jnp.where(row - col == delta, colv, a_diag)
            a.append(jnp.where(diag_block, a_diag, jnp.concatenate(blocks[j], axis=0)))

        o = [o[j] + _dot(a[j], v[j]) for j in n]
        for bi in range(nb):
            ob = jnp.concatenate(o[bi * HEADS:(bi + 1) * HEADS], axis=1)
            ob = _rms(ob) * nw_ref[...] * sg_ref[bi, rows, :].astype(F32)
            o_ref[bi, rows, :] = ob.astype(o_ref.dtype)
        return carry

    lax.fori_loop(0, nc, body, 0)


def _hgrn(q, k, v, b, sg, nw, tt):
    B, T, _ = q.shape
    tile = pl.BlockSpec((B, tt, GW), lambda t: (0, t, 0))
    return pl.pallas_call(
        _hgrn_kernel,
        grid=(T // tt,),
        in_specs=[tile, tile, tile, tile, tile, pl.BlockSpec((1, GW), lambda t: (0, 0))],
        out_specs=tile,
        out_shape=jax.ShapeDtypeStruct((B, T, GW), ACT),
        scratch_shapes=[pltpu.VMEM((B, HEADS, DH, DH), F32)],
        compiler_params=pltpu.CompilerParams(dimension_semantics=("arbitrary",),
                                             vmem_limit_bytes=VMEM_LIMIT_BYTES),
        name="hgrn",
    )(q, k, v, b, sg, nw)


def _outproj_kernel(oa_ref, ob_ref, x_ref, wa_ref, wb_ref, g1_ref, n2_ref, sc_ref, sh_ref, wr_ref, rb_ref,
                    x1_ref, h2_ref, eidx_ref, rank_ref, wts_ref, cnt_ref):
    tm = x_ref.shape[1]

    @pl.when((pl.program_id(0) == 0) & (pl.program_id(1) == 0))
    def _():
        cnt_ref[...] = jnp.zeros(cnt_ref.shape, F32)

    mix = (jnp.dot(oa_ref[0], wa_ref[...], preferred_element_type=F32)
           + jnp.dot(ob_ref[0], wb_ref[...], preferred_element_type=F32))
    x1 = x_ref[0] + g1_ref[0] * mix
    x1_ref[0] = x1
    h2 = _rms(x1) * n2_ref[...]
    h2 = h2 * (1.0 + sc_ref[0]) + sh_ref[0]
    h2_ref[...] = _to_row_tiles(h2).astype(h2_ref.dtype)

    scores = jax.nn.sigmoid(_dot3(wr_ref[...], h2, dot=_dot_nt))
    sel = scores + rb_ref[...]
    sub = lax.broadcasted_iota(jnp.int32, (GROUP_SIZE, tm), 0)
    neg = -jnp.inf
    groups = range(N_GROUPS)

    def take_max(blk):
        m = jnp.max(blk, axis=0, keepdims=True)
        first = jnp.min(jnp.where(blk == m, sub, GROUP_SIZE), axis=0, keepdims=True)
        hit = sub == first
        return m, hit, jnp.where(hit, neg, blk)

    blk_of = lambda a, g: a[g * GROUP_SIZE:(g + 1) * GROUP_SIZE]
    sel_blk = [blk_of(sel, g) for g in groups]
    group_score = jnp.zeros((N_GROUPS, tm), F32)
    for g in groups:
        m1, _, rest = take_max(sel_blk[g])
        m2 = jnp.max(rest, axis=0, keepdims=True)
        group_score = jnp.where(sub == g, m1 + m2, group_score)
    group_on = jnp.zeros((N_GROUPS, tm), F32)
    for _ in range(TOPK_GROUPS):
        _, hit, group_score = take_max(group_score)
        group_on = jnp.where(hit, 1.0, group_on)

    cand = [jnp.where(group_on[g:g + 1] > 0.0, sel_blk[g], neg) for g in groups]
    picked = [jnp.zeros((GROUP_SIZE, tm), F32) for _ in groups]
    chosen = []
    for _ in range(TOP_K):
        m = jnp.max(functools.reduce(jnp.maximum, cand), axis=0, keepdims=True)
        first = functools.reduce(jnp.minimum, [jnp.where(cand[g] == m, sub + g * GROUP_SIZE, N_EXPERTS)
                                               for g in groups])
        first = jnp.min(first, axis=0, keepdims=True)
        chosen.append(first)
        for g in groups:
            hit = (sub + g * GROUP_SIZE) == first
            picked[g] = jnp.where(hit, 1.0, picked[g])
            cand[g] = jnp.where(hit, neg, cand[g])

    picked_all = jnp.concatenate(picked, axis=0)
    r_i = lax.broadcasted_iota(jnp.int32, (tm, tm), 0)
    c_i = lax.broadcasted_iota(jnp.int32, (tm, tm), 1)
    earlier = jnp.where(r_i < c_i, 1.0, 0.0).astype(BF16)
    before = jnp.dot(picked_all.astype(BF16), earlier, preferred_element_type=F32) + cnt_ref[:, 0:1]
    cnt_ref[...] = cnt_ref[...] + jnp.sum(picked_all, axis=1, keepdims=True)

    def pick_value(table, first):
        parts = [jnp.where((sub + g * GROUP_SIZE) == first, blk_of(table, g), 0.0) for g in groups]
        return jnp.sum(functools.reduce(jnp.add, parts), axis=0, keepdims=True)

    w_k = [pick_value(scores, f) for f in chosen]
    denom = functools.reduce(jnp.add, w_k)
    eidx = jnp.zeros((TOP_K, tm), jnp.int32)
    rank = jnp.zeros((TOP_K, tm), jnp.int32)
    wts = jnp.zeros((TOP_K, tm), F32)
    for k in range(TOP_K):
        eidx = jnp.where(sub == k, chosen[k], eidx)
        rank = jnp.where(sub == k, pick_value(before, chosen[k]).astype(jnp.int32), rank)
        wts = jnp.where(sub == k, w_k[k] / denom * ROUTE_SCALE, wts)
    eidx_ref[...] = eidx
    rank_ref[...] = rank
    pad = jnp.zeros((GATE_LANES - TOP_K, tm), F32)
    wts_ref[0] = jnp.concatenate([wts, pad], axis=0).T


def _outproj(oa, ob, x, wa, wb, g1, n2, sc2, sh2, wr_t, rb, tm):
    B, T, D = x.shape
    nt = T // tm
    const = lambda shape: pl.BlockSpec(shape, lambda b, t: (0,) * len(shape))
    tile = lambda w: pl.BlockSpec((1, tm, w), lambda b, t: (b, t, 0))
    per_batch = pl.BlockSpec((1, 1, D), lambda b, t: (b, 0, 0))
    picks = pl.BlockSpec((TOP_K, tm), lambda b, t: (0, b * nt + t))
    return pl.pallas_call(
        _outproj_kernel,
        grid=(B, nt),
        in_specs=[tile(GW), tile(GW), tile(D), const(wa.shape), const(wb.shape), per_batch,
                  const((1, D)), per_batch, per_batch, const(wr_t.shape), const(rb.shape)],
        out_specs=[tile(D), pl.BlockSpec((tm, ROW_TILE, LANES), lambda b, t: (b * nt + t, 0, 0)),
                   picks, picks, tile(GATE_LANES), const((N_EXPERTS, GATE_LANES))],
        out_shape=[jax.ShapeDtypeStruct((B, T, D), F32), jax.ShapeDtypeStruct((B * T, ROW_TILE, LANES), ROW_DTYPE),
                   jax.ShapeDtypeStruct((TOP_K, B * T), jnp.int32), jax.ShapeDtypeStruct((TOP_K, B * T), jnp.int32),
                   jax.ShapeDtypeStruct((B, T, GATE_LANES), F32),
                   jax.ShapeDtypeStruct((N_EXPERTS, GATE_LANES), F32)],
        compiler_params=pltpu.CompilerParams(dimension_semantics=("arbitrary", "arbitrary"),
                                             vmem_limit_bytes=VMEM_LIMIT_BYTES),
        name="outproj",
    )(oa, ob, x, wa, wb, g1, n2, sc2, sh2, wr_t, rb)


def _slots_kernel(off_ref, eidx_ref, rank_ref, slot_ref):
    eidx = eidx_ref[...]

    def add_expert(e, acc):
        return acc + jnp.where(eidx == e, off_ref[e], 0)

    slot_ref[...] = lax.fori_loop(0, N_EXPERTS, add_expert, rank_ref[...])


def _slots(offsets, eidx, rank):
    k, m = eidx.shape
    tile = pl.BlockSpec((k, m), lambda i: (0, 0))
    return pl.pallas_call(
        _slots_kernel,
        grid=(1,),
        in_specs=[pl.BlockSpec(memory_space=pltpu.SMEM), tile, tile],
        out_specs=tile,
        out_shape=jax.ShapeDtypeStruct((k, m), jnp.int32),
        compiler_params=pltpu.CompilerParams(dimension_semantics=("arbitrary",),
                                             vmem_limit_bytes=VMEM_LIMIT_BYTES),
        name="slots",
    )(offsets, eidx, rank)


def _dispatch_kernel(slot_ref, h_ref, xs_ref, sem):
    tmd = h_ref.shape[0]

    def start_rows(j, c):
        for k in range(TOP_K):
            pltpu.make_async_copy(h_ref.at[pl.ds(j, 1)], xs_ref.at[pl.ds(slot_ref[k, j], 1)], sem).start(
                priority=k % 2)
        return c

    lax.fori_loop(0, tmd, start_rows, 0)
    for _ in range(TOP_K):
        pltpu.make_async_copy(h_ref, xs_ref.at[pl.ds(0, tmd)], sem).wait()


def _dispatch(slot, h2, n_rows, tmd):
    M = h2.shape[0]
    return pl.pallas_call(
        _dispatch_kernel,
        grid=(M // tmd,),
        in_specs=[pl.BlockSpec((TOP_K, tmd), lambda i: (0, i), memory_space=pltpu.SMEM),
                  pl.BlockSpec((tmd, ROW_TILE, LANES), lambda i: (i, 0, 0))],
        out_specs=pl.BlockSpec(memory_space=pl.ANY),
        out_shape=jax.ShapeDtypeStruct((n_rows, ROW_TILE, LANES), h2.dtype),
        scratch_shapes=[pltpu.SemaphoreType.DMA],
        compiler_params=pltpu.CompilerParams(dimension_semantics=("arbitrary",),
                                             vmem_limit_bytes=VMEM_LIMIT_BYTES),
        name="dispatch",
    )(slot, h2)


FFN_PIPELINE_STEPS = 2


def _ffn_kernel(te_ref, nu_ref, nv_ref, x_ref, wg_ref, wu_ref, wd_ref, y_ref, wgu_s, wd_s, xstd, ystd):
    i = pl.program_id(0)
    row = lax.broadcasted_iota(jnp.int32, (MOE_ROWS, 1), 0)
    tile_mm = jnp.maximum(i - 1, 0)
    tile_out = jnp.maximum(i - 2, 0)

    @pl.when(i == 0)
    def _():
        xstd[...] = jnp.zeros(xstd.shape, BF16)
        ystd[...] = jnp.zeros(ystd.shape, F32)

    @pl.when(i < nu_ref[0] + FFN_PIPELINE_STEPS)
    def _():
        @pl.when((i == 0) | (te_ref[tile_mm] != te_ref[tile_out]))
        def _():
            wgu_s[:, :D_EXPERT] = wg_ref[0].astype(BF16)
            wgu_s[:, D_EXPERT:] = wu_ref[0].astype(BF16)
            wd_s[...] = wd_ref[0].astype(BF16)

        cur = i % 2
        xstd[cur] = jnp.where(row < nv_ref[i], _to_rows(x_ref[...].astype(F32)), 0.0).astype(BF16)
        y_ref[...] = _to_row_tiles(ystd[cur]).astype(y_ref.dtype)
        gu = jnp.dot(xstd[1 - cur], wgu_s[...], preferred_element_type=F32)
        act = _silu(gu[:, :D_EXPERT]) * gu[:, D_EXPERT:]
        ystd[1 - cur] = jnp.dot(act.astype(BF16), wd_s[...], preferred_element_type=F32)


def _ffn(tile_expert, n_used, n_valid, xs, wg, wu, wd):
    D = wg.shape[1]
    n_tiles = xs.shape[0] // MOE_ROWS
    block = (MOE_ROWS, ROW_TILE, LANES)
    expert = lambda i, te, nu, nv: (te[jnp.maximum(i - 1, 0)], 0, 0)
    return pl.pallas_call(
        _ffn_kernel,
        grid_spec=pltpu.PrefetchScalarGridSpec(
            num_scalar_prefetch=3,
            grid=(n_tiles + FFN_PIPELINE_STEPS,),
            in_specs=[pl.BlockSpec(block, lambda i, te, nu, nv: (jnp.minimum(i, nu[0] - 1), 0, 0)),
                      pl.BlockSpec((1, D, D_EXPERT), expert),
                      pl.BlockSpec((1, D, D_EXPERT), expert),
                      pl.BlockSpec((1, D_EXPERT, D), expert)],
            out_specs=pl.BlockSpec(block, lambda i, te, nu, nv: (jnp.clip(i - 2, 0, nu[0] - 1), 0, 0)),
            scratch_shapes=[pltpu.VMEM((D, 2 * D_EXPERT), BF16), pltpu.VMEM((D_EXPERT, D), BF16),
                            pltpu.VMEM((2, MOE_ROWS, D), BF16), pltpu.VMEM((2, MOE_ROWS, D), F32)]),
        out_shape=jax.ShapeDtypeStruct(xs.shape, xs.dtype),
        compiler_params=pltpu.CompilerParams(dimension_semantics=("arbitrary",),
                                             vmem_limit_bytes=VMEM_LIMIT_BYTES),
        name="ffn",
    )(tile_expert, n_used, n_valid, xs, wg, wu, wd)


def _combine_kernel(slot_ref, next_slot_ref, w_ref, h_ref, x1_ref, g2_ref, fw_ref, wsgu_ref, wsd_ref, y_ref,
                    o_ref, buf, sem):
    tmc = h_ref.shape[0]
    i = pl.program_id(0)
    last = pl.num_programs(0) - 1
    assert TOP_K == ROW_TILE

    def start_rows(slots, b):
        def body(j, c):
            for k in range(TOP_K):
                pltpu.make_async_copy(y_ref.at[pl.ds(slots[k, j], 1)], buf.at[b, k, pl.ds(j, 1)], sem.at[b]).start(
                    priority=k % 2)
            return c
        lax.fori_loop(0, tmc, body, 0)

    def wait_rows(b):
        for k in range(TOP_K):
            pltpu.make_async_copy(y_ref.at[pl.ds(0, tmc)], buf.at[b, k], sem.at[b]).wait()

    def for_buffer(step, fn):
        for b in range(2):
            @pl.when(step % 2 == b)
            def _():
                fn(b)

    @pl.when(i == 0)
    def _():
        start_rows(slot_ref, 0)

    @pl.when(i < last)
    def _():
        for_buffer(i + 1, lambda b: start_rows(next_slot_ref, b))

    gu = jnp.dot(_to_rows(h_ref[...].astype(F32)).astype(BF16), wsgu_ref[...], preferred_element_type=F32)
    act = _silu(gu[:, :D_EXPERT]) * gu[:, D_EXPERT:]
    shared = jnp.dot(act.astype(BF16), wsd_ref[...], preferred_element_type=F32)
    w = w_ref[...]
    lane = lax.broadcasted_iota(jnp.int32, w.shape, 1)

    def finish(b):
        wait_rows(b)
        w3 = _to_row_tiles(jnp.concatenate(
            [jnp.broadcast_to(_lane_pick(w, lane, k), (tmc, LANES)) for k in range(TOP_K)], axis=1))
        routed = jnp.zeros((tmc, ROW_TILE, LANES), F32)
        for k in range(TOP_K):
            routed = routed + jnp.broadcast_to(w3[:, k:k + 1, :], w3.shape) * buf[b, k].astype(F32)
        acc = shared + _to_rows(routed)
        y = x1_ref[...] + g2_ref[0] * acc
        o_ref[...] = _rms(y) * fw_ref[...]

    for_buffer(i, finish)


def _combine(slot, wts, h2, x1, g2, fw, wsgu, wsd, ys, T, tmc):
    M, D = x1.shape
    const = lambda shape: pl.BlockSpec(shape, lambda i: (0,) * len(shape))
    tile = lambda w: pl.BlockSpec((tmc, w), lambda i: (i, 0))
    row_tiled = pl.BlockSpec((tmc, ROW_TILE, LANES), lambda i: (i, 0, 0))
    n_steps = M // tmc
    return pl.pallas_call(
        _combine_kernel,
        grid=(n_steps,),
        in_specs=[pl.BlockSpec((TOP_K, tmc), lambda i: (0, i), memory_space=pltpu.SMEM),
                  pl.BlockSpec((TOP_K, tmc), lambda i: (0, jnp.minimum(i + 1, n_steps - 1)), memory_space=pltpu.SMEM),
                  tile(GATE_LANES), row_tiled, tile(D),
                  pl.BlockSpec((1, 1, D), lambda i: (i // (T // tmc), 0, 0)),
                  const((1, D)), const(wsgu.shape), const(wsd.shape),
                  pl.BlockSpec(memory_space=pl.ANY)],
        out_specs=tile(D),
        out_shape=jax.ShapeDtypeStruct((M, D), F32),
        scratch_shapes=[pltpu.VMEM((2, TOP_K, tmc, ROW_TILE, LANES), ys.dtype), pltpu.SemaphoreType.DMA((2,))],
        compiler_params=pltpu.CompilerParams(dimension_semantics=("arbitrary",),
                                             vmem_limit_bytes=VMEM_LIMIT_BYTES),
        name="combine",
    )(slot, slot, wts, h2, x1, g2, fw, wsgu, wsd, ys)


def _pick_tile(n, want):
    t = min(n, want)
    assert n % t == 0 and t % CHUNK == 0, (n, want)
    return t


def kernel(x, c, w_ada, b_ada, norm1_w, w_in, conv_w, gdn_a_log, gdn_dt_bias, gdn_norm_w, hg_lb, hg_norm_w,
           w_out, norm2_w, w_router, router_bias, w_gate, w_up, w_down, ws_gate, ws_up, ws_down, final_norm_w):
    B, T, D = x.shape
    M = B * T
    depth = w_ada.shape[0]
    assert depth == 1 and T % CHUNK == 0 and B <= 8
    layer = 0
    tt = _pick_tile(T, TILE_RECURRENCE)

    c_pad = jnp.pad(c, ((0, 8 - B), (0, 0)))
    mod = _ada(c_pad, w_ada[layer], b_ada[layer].reshape(1, -1))[:B]
    sh1, sc1, g1, sh2, sc2, g2 = (m.reshape(B, 1, D) for m in jnp.split(mod, 6, axis=-1))

    w = w_in[layer]
    qkv_w = 3 * GW
    sizes = (GW, HEADS, HEADS, GW, GW, GW, GW)
    offs = [qkv_w]
    for s in sizes:
        offs.append(offs[-1] + s)
    seg = lambda i: w[:, offs[i]:offs[i + 1]]
    small = jnp.pad(jnp.concatenate([seg(1), seg(2)], axis=1), ((0, 0), (0, GATE_LANES - 2 * HEADS)))
    w_all = jnp.concatenate([w[:, :qkv_w], seg(0), seg(3), seg(4), seg(5), seg(6), small], axis=1).astype(BF16)
    lane_pad = lambda v: jnp.pad(v.astype(F32).reshape(1, HEADS), ((0, 0), (HEADS, GATE_LANES - 2 * HEADS)))
    idx = jnp.arange(CUMSUM_ROWS)
    tri = ((idx[:, None] >= idx[None, :]) & (idx[:, None] // CHUNK == idx[None, :] // CHUNK)).astype(BF16)

    qa, ka, va, ga, sm, bcum, kb, ib, qb, gb = _inproj(
        layer, x, norm1_w[layer].reshape(1, D), sc1, sh1, w_all, conv_w[layer].astype(F32),
        lane_pad(gdn_a_log[layer]), lane_pad(gdn_dt_bias[layer]), hg_lb.astype(F32), tri,
        _pick_tile(T, TILE_INPROJ))

    gct = sm[:, :, HEADS:2 * HEADS].transpose(0, 2, 1).reshape(B, HEADS, T // CHUNK, CHUNK)
    oa = _gdn(qa, ka, va, ga, sm, gct, gdn_norm_w[layer].reshape(1, DH), tt)
    ob = _hgrn(qb, kb, ib, bcum, gb, hg_norm_w[layer].reshape(1, GW), tt)

    wo = w_out[layer].astype(BF16)
    x1, h2, eidx, rank, wts, cnt = _outproj(oa, ob, x, wo[:GW], wo[GW:], g1, norm2_w[layer].reshape(1, D), sc2, sh2,
                                            w_router[layer].T, router_bias[layer].reshape(N_EXPERTS, 1),
                                            _pick_tile(T, TILE_OUTPROJ))

    counts = cnt[:, 0].astype(jnp.int32)
    padded = (counts + MOE_ROWS - 1) // MOE_ROWS * MOE_ROWS
    ends = jnp.cumsum(padded)
    offsets = ends - padded
    n_tiles = (M * TOP_K) // MOE_ROWS + N_EXPERTS
    n_used = (ends[-1] // MOE_ROWS).astype(jnp.int32)
    tile_ids = jnp.minimum(jnp.arange(n_tiles + FFN_PIPELINE_STEPS, dtype=jnp.int32), n_used - 1)
    tile_expert = jnp.sum(((ends // MOE_ROWS)[None, :] <= tile_ids[:, None]).astype(jnp.int32), axis=1)
    tile_expert = jnp.minimum(tile_expert, N_EXPERTS - 1)
    slot = _slots(offsets.astype(jnp.int32), eidx, rank)
    first_tile = jnp.take(offsets, tile_expert) // MOE_ROWS
    n_valid = jnp.clip(jnp.take(counts, tile_expert) - (tile_ids - first_tile) * MOE_ROWS, 0, MOE_ROWS)

    assert D == ROW_TILE * LANES
    xs = _dispatch(slot, h2, n_tiles * MOE_ROWS, _pick_tile(M, TILE_DISPATCH))
    ys = _ffn(tile_expert.astype(jnp.int32), n_used.reshape(1), n_valid.astype(jnp.int32), xs,
              w_gate[layer], w_up[layer], w_down[layer])
    wsgu = jnp.concatenate([ws_gate[layer], ws_up[layer]], axis=-1).astype(BF16)
    out = _combine(slot, wts.reshape(M, GATE_LANES), h2, x1.reshape(M, D), g2, final_norm_w.reshape(1, D),
                   wsgu, ws_down[layer].astype(BF16), ys, T, _pick_tile(T, TILE_COMBINE))
    return out.reshape(B, T, D)
```

```python
import functools

import jax
import jax.numpy as jnp
from jax import lax
from jax.experimental import pallas as pl
from jax.experimental.pallas import tpu as pltpu

F32 = jnp.float32
BF16 = jnp.bfloat16

EPS = 1e-6
CHUNK = 64
SUB = 8
HEADS = 4
DH = 128
GW = HEADS * DH
CONV_K = 4
N_EXPERTS = 64
N_GROUPS = 8
GROUP_SIZE = N_EXPERTS // N_GROUPS
TOPK_GROUPS = 4
TOP_K = 8
D_EXPERT = 256
ROUTE_SCALE = 2.5
GATE_LANES = 128
GDN_CHUNKS_PER_ITER = 8
INV_BLOCK = 16
CUMSUM_ROWS = 128
MOE_ROWS = 1024
ROW_TILE, LANES = 8, 128

TILE_INPROJ = 1024
TILE_RECURRENCE = 512
TILE_OUTPROJ = 1024
TILE_DISPATCH = 1024
TILE_COMBINE = 256
ADA_COLS = 1024

VMEM_LIMIT_BYTES = 56 * 1024 * 1024

ACT = BF16
ROW_DTYPE = F32


def _silu(x):
    return x * jax.nn.sigmoid(x)


def _dot(a, b):
    return jnp.dot(a.astype(BF16), b.astype(BF16), preferred_element_type=F32)


def _dot_nt(a, b):
    return lax.dot_general(a.astype(BF16), b.astype(BF16), (((1,), (1,)), ((), ())),
                           preferred_element_type=F32)


def _dot_tn(a, b):
    return lax.dot_general(a.astype(BF16), b.astype(BF16), (((0,), (0,)), ((), ())),
                           preferred_element_type=F32)


def _split2(x):
    hi = x.astype(BF16)
    lo = (x - hi.astype(F32)).astype(BF16)
    return hi, lo


def _dot3(a, b, dot=_dot):
    ah, al = _split2(a)
    bh, bl = _split2(b)
    return dot(ah, bh) + dot(ah, bl) + dot(al, bh)


def _cumsum_rows(tri, x):
    hi = x.astype(BF16)
    r = x - hi.astype(F32)
    mid = r.astype(BF16)
    lo = (r - mid.astype(F32)).astype(BF16)
    g = tri.shape[0]
    groups = []
    for r0 in range(0, x.shape[0], g):
        rows = slice(r0, r0 + g)
        groups.append(jnp.dot(tri, hi[rows], preferred_element_type=F32)
                      + jnp.dot(tri, mid[rows], preferred_element_type=F32)
                      + jnp.dot(tri, lo[rows], preferred_element_type=F32))
    return jnp.concatenate(groups, axis=0)


def _lane_pick(tile, lane, idx):
    return jnp.sum(jnp.where(lane == idx, tile, 0.0), axis=1, keepdims=True)


def _rms(x):
    return x * lax.rsqrt(jnp.mean(x * x, axis=-1, keepdims=True) + EPS)


def _to_rows(x3):
    r = x3.shape[0]
    xt = jnp.swapaxes(x3.reshape(r // ROW_TILE, ROW_TILE, ROW_TILE, LANES), 1, 2)
    return jnp.concatenate([xt[:, s].reshape(r, LANES) for s in range(ROW_TILE)], axis=1)


def _to_row_tiles(x):
    r = x.shape[0]
    xt = jnp.stack([x[:, s * LANES:(s + 1) * LANES].reshape(r // ROW_TILE, ROW_TILE, LANES) for s in range(ROW_TILE)],
                   axis=1)
    return jnp.swapaxes(xt, 1, 2).reshape(r, ROW_TILE, LANES)


def _ada_kernel(c_ref, w_ref, b_ref, o_ref):
    ca = _silu(c_ref[...])
    o_ref[...] = _dot3(ca, w_ref[...]) + b_ref[...]


def _ada(c_pad, w, b):
    rows, d = c_pad.shape
    n = w.shape[1]
    tn = ADA_COLS
    return pl.pallas_call(
        _ada_kernel,
        grid=(n // tn,),
        in_specs=[pl.BlockSpec((rows, d), lambda j: (0, 0)),
                  pl.BlockSpec((d, tn), lambda j: (0, j)),
                  pl.BlockSpec((1, tn), lambda j: (0, j))],
        out_specs=pl.BlockSpec((rows, tn), lambda j: (0, j)),
        out_shape=jax.ShapeDtypeStruct((rows, n), F32),
        compiler_params=pltpu.CompilerParams(dimension_semantics=("arbitrary",),
                                             vmem_limit_bytes=VMEM_LIMIT_BYTES),
        name="ada",
    )(c_pad, w, b)


def _inproj_kernel(layer, x_ref, n1_ref, sc_ref, sh_ref, w_ref, cw_ref, alog_ref, dt_ref, lb_ref, tri_ref,
                   qa_ref, ka_ref, va_ref, ga_ref, sm_ref, b_ref, kb_ref, ib_ref, qb_ref, gb_ref,
                   pbuf):
    tt = x_ref.shape[1]
    t = pl.program_id(1)

    h = _rms(x_ref[0]) * n1_ref[...]
    h = h * (1.0 + sc_ref[0]) + sh_ref[0]
    hb = h.astype(BF16)

    def proj(g, width=GW):
        return jnp.dot(hb, w_ref[:, g * GW:g * GW + width], preferred_element_type=F32)

    @pl.when(t == 0)
    def _():
        pbuf[:, 0:8, :] = jnp.zeros((3, 8, GW), F32)

    for g, out_ref in enumerate((qa_ref, ka_ref, va_ref)):
        cols = slice(g * GW, (g + 1) * GW)
        p = proj(g)
        pbuf[g, 8:8 + tt, :] = p
        y = p * cw_ref[CONV_K - 1:CONV_K, cols]
        for j in range(1, CONV_K):
            y = y + pbuf[g, 8 - j:8 - j + tt, :] * cw_ref[CONV_K - 1 - j:CONV_K - j, cols]
        pbuf[g, 0:8, :] = pbuf[g, tt:tt + 8, :]
        y = _silu(y)
        if g == 2:
            out_ref[0] = y.astype(out_ref.dtype)
        else:
            scale = DH ** -0.5 if g == 0 else 1.0
            for hh in range(HEADS):
                hs = slice(hh * DH, (hh + 1) * DH)
                yh = y[:, hs]
                inv = lax.rsqrt(jnp.sum(yh * yh, axis=-1, keepdims=True) + EPS)
                out_ref[0, :, hs] = (yh * inv * scale).astype(out_ref.dtype)

    ga_ref[0] = _silu(proj(3)).astype(ga_ref.dtype)

    ps = proj(8, GATE_LANES)
    lane = lax.broadcasted_iota(jnp.int32, ps.shape, 1)
    beta = jax.nn.sigmoid(ps)
    z = ps + dt_ref[...]
    softplus = jnp.maximum(z, 0.0) + jnp.log1p(jnp.exp(-jnp.abs(z)))
    g_log = -jnp.exp(alog_ref[...]) * softplus
    tri = tri_ref[...]
    gc = _cumsum_rows(tri, jnp.where((lane >= HEADS) & (lane < 2 * HEADS), g_log, 0.0))
    sm_ref[0] = jnp.where(lane < HEADS, beta, gc)

    hl = lb_ref[...]
    e = jnp.exp(hl - jnp.max(hl, axis=0, keepdims=True))
    lb = jnp.sum(e[0:layer + 1], axis=0, keepdims=True) / jnp.sum(e, axis=0, keepdims=True)
    fr = proj(4)
    logf = jnp.log(lb + (1.0 - lb) * jax.nn.sigmoid(fr))
    b_ref[0] = _cumsum_rows(tri, logf)
    kb_ref[0] = ((1.0 - lb) * jax.nn.sigmoid(-fr)).astype(kb_ref.dtype)
    ib_ref[0] = proj(5).astype(ib_ref.dtype)
    qb_ref[0] = _silu(proj(6)).astype(qb_ref.dtype)
    gb_ref[0] = _silu(proj(7)).astype(gb_ref.dtype)


def _inproj(layer, x, n1, sc1, sh1, w_all, conv_w, alog_pad, dt_pad, hg_lb, tri, tt):
    B, T, D = x.shape
    const = lambda shape: pl.BlockSpec(shape, lambda b, t: (0,) * len(shape), pipeline_mode=pl.Buffered(1))
    act = lambda dt: jax.ShapeDtypeStruct((B, T, GW), dt)
    tile = lambda w: pl.BlockSpec((1, tt, w), lambda b, t: (b, t, 0))
    per_batch = pl.BlockSpec((1, 1, D), lambda b, t: (b, 0, 0))
    return pl.pallas_call(
        functools.partial(_inproj_kernel, layer),
        grid=(B, T // tt),
        in_specs=[tile(D), const((1, D)), per_batch, per_batch,
                  const(w_all.shape), const(conv_w.shape), const((1, GATE_LANES)), const((1, GATE_LANES)),
                  const(hg_lb.shape), const(tri.shape)],
        out_specs=[tile(GW), tile(GW), tile(GW), tile(GW), tile(GATE_LANES), tile(GW),
                   tile(GW), tile(GW), tile(GW), tile(GW)],
        out_shape=[act(ACT), act(ACT), act(ACT), act(ACT),
                   jax.ShapeDtypeStruct((B, T, GATE_LANES), F32), act(F32),
                   act(ACT), act(ACT), act(ACT), act(ACT)],
        scratch_shapes=[pltpu.VMEM((3, tt + 8, GW), F32)],
        compiler_params=pltpu.CompilerParams(dimension_semantics=("arbitrary", "arbitrary"),
                                             vmem_limit_bytes=VMEM_LIMIT_BYTES),
        name="inproj",
    )(x, n1, sc1, sh1, w_all, conv_w, alog_pad, dt_pad, hg_lb, tri)


def _gdn_prep_kernel(q_ref, k_ref, v_ref, sm_ref, gct_ref, o_ref, qt_ref, m_ref, n_ref):
    tt = q_ref.shape[1]
    nc = tt // CHUNK
    row = lax.broadcasted_iota(jnp.int32, (CHUNK, CHUNK), 0)
    col = lax.broadcasted_iota(jnp.int32, (CHUNK, CHUNK), 1)
    causal = row >= col
    diag_blk = (row > col) & (row // INV_BLOCK == col // INV_BLOCK)
    off_blk = row // INV_BLOCK > col // INV_BLOCK
    eye = jnp.where(row == col, 1.0, 0.0)
    lane = lax.broadcasted_iota(jnp.int32, (CHUNK, GATE_LANES), 1)
    assert INV_BLOCK == 16 and CHUNK == 4 * INV_BLOCK

    def body(i, carry):
        chains = [(GDN_CHUNKS_PER_ITER * i + j, hh) for j in range(GDN_CHUNKS_PER_ITER) for hh in range(HEADS)]
        rows = [pl.ds(pl.multiple_of(c * CHUNK, CHUNK), CHUNK) for c, _ in chains]
        hs = [slice(hh * DH, (hh + 1) * DH) for _, hh in chains]
        n = range(len(chains))
        sm = [sm_ref[0, rows[j], :] for j in n]
        q = [q_ref[0, rows[j], hs[j]].astype(F32) for j in n]
        k = [k_ref[0, rows[j], hs[j]].astype(F32) for j in n]
        v = [v_ref[0, rows[j], hs[j]].astype(F32) for j in n]
        beta = [_lane_pick(sm[j], lane, chains[j][1]) for j in n]
        gcol = [_lane_pick(sm[j], lane, HEADS + chains[j][1]) for j in n]
        grow = [gct_ref[0, hh, pl.ds(c, 1), :] for c, hh in chains]
        decay = [jnp.exp(jnp.where(causal, gcol[j] - grow[j], -jnp.inf)) for j in n]
        kb = [k[j] * beta[j] for j in n]
        L = [_dot_nt(kb[j], k[j]) * decay[j] for j in n]
        dg = [jnp.where(diag_blk, L[j], 0.0) for j in n]
        off = [jnp.where(off_blk, L[j], 0.0) for j in n]
        dinv = [eye - dg[j] for j in n]
        pw = [_dot3(dg[j], dg[j]) for j in n]
        for _ in range(2):
            dinv = [dinv[j] + _dot3(dinv[j], pw[j]) for j in n]
            pw = [_dot3(pw[j], pw[j]) for j in n]
        dinv = [dinv[j] + _dot3(dinv[j], pw[j]) for j in n]
        f1 = [_dot(dinv[j], off[j]) for j in n]
        f2 = [_dot(f1[j], f1[j]) for j in n]
        f3 = [_dot(f1[j], f2[j]) for j in n]
        tinv = [_dot(eye - f1[j] + f2[j] - f3[j], dinv[j]) for j in n]
        eg = [jnp.exp(gcol[j]) for j in n]
        sol = [_dot(tinv[j], jnp.concatenate([v[j] * beta[j], kb[j] * eg[j]], axis=1)) for j in n]
        attn = [_dot_nt(q[j], k[j]) * decay[j] for j in n]
        k_tail = [k[j] * jnp.exp(gcol[j][CHUNK - 1:CHUNK, :] - gcol[j]) for j in n]
        au = [_dot(attn[j], sol[j]) for j in n]
        ku = [_dot_tn(k_tail[j], sol[j]) for j in n]
        for j, (c, hh) in enumerate(chains):
            o_ref[0, rows[j], hs[j]] = au[j][:, :DH]
            qt_ref[0, rows[j], hs[j]] = (q[j] * eg[j] - au[j][:, DH:]).astype(qt_ref.dtype)
            n_ref[0, hh, c] = ku[j][:, :DH].astype(n_ref.dtype)
            m_ref[0, hh, c] = (-ku[j][:, DH:]).astype(m_ref.dtype)
        return carry

    lax.fori_loop(0, nc // GDN_CHUNKS_PER_ITER, body, 0)


def _gdn_scan_kernel(o_ref, qt_ref, m_ref, n_ref, gct_ref, sg_ref, nw_ref, out_ref, s_ref):
    nb, tt = o_ref.shape[0], o_ref.shape[1]
    nc = tt // CHUNK

    @pl.when(pl.program_id(0) == 0)
    def _():
        s_ref[...] = jnp.zeros(s_ref.shape, F32)

    nw = nw_ref[...]

    def body(c, carry):
        rows = pl.ds(pl.multiple_of(c * CHUNK, CHUNK), CHUNK)
        for b in range(nb):
            for hh in range(HEADS):
                hs = slice(hh * DH, (hh + 1) * DH)
                S = s_ref[b, hh]
                Sb = S.astype(BF16)
                glast = gct_ref[b, hh, pl.ds(c, 1), :][:, CHUNK - 1:CHUNK]
                o = o_ref[b, rows, hs] + jnp.dot(qt_ref[b, rows, hs], Sb, preferred_element_type=F32)
                s_ref[b, hh] = (S * jnp.exp(glast) + jnp.dot(m_ref[b, hh, c], Sb, preferred_element_type=F32)
                                + n_ref[b, hh, c].astype(F32))
                o = _rms(o) * nw * sg_ref[b, rows, hs].astype(F32)
                out_ref[b, rows, hs] = o.astype(out_ref.dtype)
        return carry

    lax.fori_loop(0, nc, body, 0)


def _gdn(q, k, v, sg, sm, gct, nw, tt):
    B, T, _ = q.shape
    nc = tt // CHUNK
    assert nc % GDN_CHUNKS_PER_ITER == 0
    n_chunks = T // CHUNK
    tile = lambda w: pl.BlockSpec((1, tt, w), lambda b, t: (b, t, 0))
    mat = jax.ShapeDtypeStruct((B, HEADS, n_chunks, DH, DH), ACT)
    o_part, qt, m, n = pl.pallas_call(
        _gdn_prep_kernel,
        grid=(B, T // tt),
        in_specs=[tile(GW), tile(GW), tile(GW), tile(GATE_LANES),
                  pl.BlockSpec((1, HEADS, nc, CHUNK), lambda b, t: (b, 0, t, 0))],
        out_specs=[tile(GW), tile(GW),
                   pl.BlockSpec((1, HEADS, nc, DH, DH), lambda b, t: (b, 0, t, 0, 0)),
                   pl.BlockSpec((1, HEADS, nc, DH, DH), lambda b, t: (b, 0, t, 0, 0))],
        out_shape=[jax.ShapeDtypeStruct((B, T, GW), F32), jax.ShapeDtypeStruct((B, T, GW), ACT), mat, mat],
        compiler_params=pltpu.CompilerParams(dimension_semantics=("arbitrary", "arbitrary"),
                                             vmem_limit_bytes=VMEM_LIMIT_BYTES),
        name="gdn_prep",
    )(q, k, v, sm, gct)

    full = lambda w: pl.BlockSpec((B, tt, w), lambda t: (0, t, 0))
    mats = pl.BlockSpec((B, HEADS, nc, DH, DH), lambda t: (0, 0, t, 0, 0))
    return pl.pallas_call(
        _gdn_scan_kernel,
        grid=(T // tt,),
        in_specs=[full(GW), full(GW), mats, mats,
                  pl.BlockSpec((B, HEADS, nc, CHUNK), lambda t: (0, 0, t, 0)),
                  full(GW), pl.BlockSpec((1, DH), lambda t: (0, 0))],
        out_specs=full(GW),
        out_shape=jax.ShapeDtypeStruct((B, T, GW), ACT),
        scratch_shapes=[pltpu.VMEM((B, HEADS, DH, DH), F32)],
        compiler_params=pltpu.CompilerParams(dimension_semantics=("arbitrary",),
                                             vmem_limit_bytes=VMEM_LIMIT_BYTES),
        name="gdn_scan",
    )(o_part, qt, m, n, gct, sg, nw)


def _hgrn_kernel(q_ref, k_ref, v_ref, b_ref, sg_ref, nw_ref, o_ref, st_ref):
    nb, tt = q_ref.shape[0], q_ref.shape[1]
    nc = tt // CHUNK

    @pl.when(pl.program_id(0) == 0)
    def _():
        st_ref[...] = jnp.zeros(st_ref.shape, F32)

    row = lax.broadcasted_iota(jnp.int32, (CHUNK, CHUNK), 0)
    col = lax.broadcasted_iota(jnp.int32, (CHUNK, CHUNK), 1)
    diag_block = ((col // SUB) == (row // SUB)) & (col <= row)
    chains = [(bi, hh) for bi in range(nb) for hh in range(HEADS)]
    n = range(len(chains))
    hs = [slice(hh * DH, (hh + 1) * DH) for _, hh in chains]

    def body(c, carry):
        rows = pl.ds(pl.multiple_of(c * CHUNK, CHUNK), CHUNK)
        q = [q_ref[bi, rows, hs[j]].astype(F32) for j, (bi, _) in enumerate(chains)]
        k = [k_ref[bi, rows, hs[j]].astype(F32) for j, (bi, _) in enumerate(chains)]
        v = [v_ref[bi, rows, hs[j]].astype(F32) for j, (bi, _) in enumerate(chains)]
        b = [b_ref[bi, rows, hs[j]] for j, (bi, _) in enumerate(chains)]
        blast = [b[j][CHUNK - 1:CHUNK, :] for j in n]
        st = [st_ref[bi, hh] for bi, hh in chains]
        o = [_dot_nt(q[j] * jnp.exp(b[j]), st[j]) for j in n]
        k_tail = [k[j] * jnp.exp(blast[j] - b[j]) for j in n]
        for j, (bi, hh) in enumerate(chains):
            st_ref[bi, hh] = st[j] * jnp.exp(blast[j]) + _dot_tn(v[j], k_tail[j])

        blocks = [[jnp.zeros((SUB, CHUNK), F32)] for _ in n]
        for i in range(1, CHUNK // SUB):
            lo, hi = i * SUB, (i + 1) * SUB
            for j in n:
                r = b[j][lo:lo + 1, :]
                qi = q[j][lo:hi] * jnp.exp(b[j][lo:hi] - r)
                kj = k[j][:lo] * jnp.exp(jnp.minimum(r - b[j][:lo], 0.0))
                kj = jnp.concatenate([kj, jnp.zeros((CHUNK - lo, DH), F32)], axis=0)
                blocks[j].append(_dot_nt(qi, kj))
        a = []
        for j in n:
            f = jnp.exp(jnp.minimum(b[j] - pltpu.roll(b[j], 1, 0), 0.0))
            e = None
            a_diag = jnp.zeros((CHUNK, CHUNK), F32)
            for delta in range(SUB):
                if delta == 0:
                    term = q[j] * k[j]
                else:
                    fsh = f if delta == 1 else pltpu.roll(f, delta - 1, 0)
                    e = fsh if e is None else e * fsh
                    term = q[j] * pltpu.roll(k[j], delta, 0) * e
                colv = jnp.sum(term, axis=1, keepdims=True)
                a_diag = jnp.where(row - col == delta, colv, a_diag)
            a.append(jnp.where(diag_block, a_diag, jnp.concatenate(blocks[j], axis=0)))

        o = [o[j] + _dot(a[j], v[j]) for j in n]
        for bi in range(nb):
            ob = jnp.concatenate(o[bi * HEADS:(bi + 1) * HEADS], axis=1)
            ob = _rms(ob) * nw_ref[...] * sg_ref[bi, rows, :].astype(F32)
            o_ref[bi, rows, :] = ob.astype(o_ref.dtype)
        return carry

    lax.fori_loop(0, nc, body, 0)


def _hgrn(q, k, v, b, sg, nw, tt):
    B, T, _ = q.shape
    tile = pl.BlockSpec((B, tt, GW), lambda t: (0, t, 0))
    return pl.pallas_call(
        _hgrn_kernel,
        grid=(T // tt,),
        in_specs=[tile, tile, tile, tile, tile, pl.BlockSpec((1, GW), lambda t: (0, 0))],
        out_specs=tile,
        out_shape=jax.ShapeDtypeStruct((B, T, GW), ACT),
        scratch_shapes=[pltpu.VMEM((B, HEADS, DH, DH), F32)],
        compiler_params=pltpu.CompilerParams(dimension_semantics=("arbitrary",),
                                             vmem_limit_bytes=VMEM_LIMIT_BYTES),
        name="hgrn",
    )(q, k, v, b, sg, nw)


def _outproj_kernel(oa_ref, ob_ref, x_ref, wa_ref, wb_ref, g1_ref, n2_ref, sc_ref, sh_ref, wr_ref, rb_ref,
                    x1_ref, h2_ref, eidx_ref, rank_ref, wts_ref, cnt_ref):
    tm = x_ref.shape[1]

    @pl.when((pl.program_id(0) == 0) & (pl.program_id(1) == 0))
    def _():
        cnt_ref[...] = jnp.zeros(cnt_ref.shape, F32)

    mix = (jnp.dot(oa_ref[0], wa_ref[...], preferred_element_type=F32)
           + jnp.dot(ob_ref[0], wb_ref[...], preferred_element_type=F32))
    x1 = x_ref[0] + g1_ref[0] * mix
    x1_ref[0] = x1
    h2 = _rms(x1) * n2_ref[...]
    h2 = h2 * (1.0 + sc_ref[0]) + sh_ref[0]
    h2_ref[...] = _to_row_tiles(h2).astype(h2_ref.dtype)

    scores = jax.nn.sigmoid(_dot3(wr_ref[...], h2, dot=_dot_nt))
    sel = scores + rb_ref[...]
    sub = lax.broadcasted_iota(jnp.int32, (GROUP_SIZE, tm), 0)
    neg = -jnp.inf
    groups = range(N_GROUPS)

    def take_max(blk):
        m = jnp.max(blk, axis=0, keepdims=True)
        first = jnp.min(jnp.where(blk == m, sub, GROUP_SIZE), axis=0, keepdims=True)
        hit = sub == first
        return m, hit, jnp.where(hit, neg, blk)

    blk_of = lambda a, g: a[g * GROUP_SIZE:(g + 1) * GROUP_SIZE]
    sel_blk = [blk_of(sel, g) for g in groups]
    group_score = jnp.zeros((N_GROUPS, tm), F32)
    for g in groups:
        m1, _, rest = take_max(sel_blk[g])
        m2 = jnp.max(rest, axis=0, keepdims=True)
        group_score = jnp.where(sub == g, m1 + m2, group_score)
    group_on = jnp.zeros((N_GROUPS, tm), F32)
    for _ in range(TOPK_GROUPS):
        _, hit, group_score = take_max(group_score)
        group_on = jnp.where(hit, 1.0, group_on)

    cand = [jnp.where(group_on[g:g + 1] > 0.0, sel_blk[g], neg) for g in groups]
    picked = [jnp.zeros((GROUP_SIZE, tm), F32) for _ in groups]
    chosen = []
    for _ in range(TOP_K):
        m = jnp.max(functools.reduce(jnp.maximum, cand), axis=0, keepdims=True)
        first = functools.reduce(jnp.minimum, [jnp.where(cand[g] == m, sub + g * GROUP_SIZE, N_EXPERTS)
                                               for g in groups])
        first = jnp.min(first, axis=0, keepdims=True)
        chosen.append(first)
        for g in groups:
            hit = (sub + g * GROUP_SIZE) == first
            picked[g] = jnp.where(hit, 1.0, picked[g])
            cand[g] = jnp.where(hit, neg, cand[g])

    picked_all = jnp.concatenate(picked, axis=0)
    r_i = lax.broadcasted_iota(jnp.int32, (tm, tm), 0)
    c_i = lax.broadcasted_iota(jnp.int32, (tm, tm), 1)
    earlier = jnp.where(r_i < c_i, 1.0, 0.0).astype(BF16)
    before = jnp.dot(picked_all.astype(BF16), earlier, preferred_element_type=F32) + cnt_ref[:, 0:1]
    cnt_ref[...] = cnt_ref[...] + jnp.sum(picked_all, axis=1, keepdims=True)

    def pick_value(table, first):
        parts = [jnp.where((sub + g * GROUP_SIZE) == first, blk_of(table, g), 0.0) for g in groups]
        return jnp.sum(functools.reduce(jnp.add, parts), axis=0, keepdims=True)

    w_k = [pick_value(scores, f) for f in chosen]
    denom = functools.reduce(jnp.add, w_k)
    eidx = jnp.zeros((TOP_K, tm), jnp.int32)
    rank = jnp.zeros((TOP_K, tm), jnp.int32)
    wts = jnp.zeros((TOP_K, tm), F32)
    for k in range(TOP_K):
        eidx = jnp.where(sub == k, chosen[k], eidx)
        rank = jnp.where(sub == k, pick_value(before, chosen[k]).astype(jnp.int32), rank)
        wts = jnp.where(sub == k, w_k[k] / denom * ROUTE_SCALE, wts)
    eidx_ref[...] = eidx
    rank_ref[...] = rank
    pad = jnp.zeros((GATE_LANES - TOP_K, tm), F32)
    wts_ref[0] = jnp.concatenate([wts, pad], axis=0).T


def _outproj(oa, ob, x, wa, wb, g1, n2, sc2, sh2, wr_t, rb, tm):
    B, T, D = x.shape
    nt = T // tm
    const = lambda shape: pl.BlockSpec(shape, lambda b, t: (0,) * len(shape))
    tile = lambda w: pl.BlockSpec((1, tm, w), lambda b, t: (b, t, 0))
    per_batch = pl.BlockSpec((1, 1, D), lambda b, t: (b, 0, 0))
    picks = pl.BlockSpec((TOP_K, tm), lambda b, t: (0, b * nt + t))
    return pl.pallas_call(
        _outproj_kernel,
        grid=(B, nt),
        in_specs=[tile(GW), tile(GW), tile(D), const(wa.shape), const(wb.shape), per_batch,
                  const((1, D)), per_batch, per_batch, const(wr_t.shape), const(rb.shape)],
        out_specs=[tile(D), pl.BlockSpec((tm, ROW_TILE, LANES), lambda b, t: (b * nt + t, 0, 0)),
                   picks, picks, tile(GATE_LANES), const((N_EXPERTS, GATE_LANES))],
        out_shape=[jax.ShapeDtypeStruct((B, T, D), F32), jax.ShapeDtypeStruct((B * T, ROW_TILE, LANES), ROW_DTYPE),
                   jax.ShapeDtypeStruct((TOP_K, B * T), jnp.int32), jax.ShapeDtypeStruct((TOP_K, B * T), jnp.int32),
                   jax.ShapeDtypeStruct((B, T, GATE_LANES), F32),
                   jax.ShapeDtypeStruct((N_EXPERTS, GATE_LANES), F32)],
        compiler_params=pltpu.CompilerParams(dimension_semantics=("arbitrary", "arbitrary"),
                                             vmem_limit_bytes=VMEM_LIMIT_BYTES),
        name="outproj",
    )(oa, ob, x, wa, wb, g1, n2, sc2, sh2, wr_t, rb)


def _slots_kernel(off_ref, eidx_ref, rank_ref, slot_ref):
    eidx = eidx_ref[...]

    def add_expert(e, acc):
        return acc + jnp.where(eidx == e, off_ref[e], 0)

    slot_ref[...] = lax.fori_loop(0, N_EXPERTS, add_expert, rank_ref[...])


def _slots(offsets, eidx, rank):
    k, m = eidx.shape
    tile = pl.BlockSpec((k, m), lambda i: (0, 0))
    return pl.pallas_call(
        _slots_kernel,
        grid=(1,),
        in_specs=[pl.BlockSpec(memory_space=pltpu.SMEM), tile, tile],
        out_specs=tile,
        out_shape=jax.ShapeDtypeStruct((k, m), jnp.int32),
        compiler_params=pltpu.CompilerParams(dimension_semantics=("arbitrary",),
                                             vmem_limit_bytes=VMEM_LIMIT_BYTES),
        name="slots",
    )(offsets, eidx, rank)


def _dispatch_kernel(slot_ref, h_ref, xs_ref, sem):
    tmd = h_ref.shape[0]

    def start_rows(j, c):
        for k in range(TOP_K):
            pltpu.make_async_copy(h_ref.at[pl.ds(j, 1)], xs_ref.at[pl.ds(slot_ref[k, j], 1)], sem).start(
                priority=k % 2)
        return c

    lax.fori_loop(0, tmd, start_rows, 0)
    for _ in range(TOP_K):
        pltpu.make_async_copy(h_ref, xs_ref.at[pl.ds(0, tmd)], sem).wait()


def _dispatch(slot, h2, n_rows, tmd):
    M = h2.shape[0]
    return pl.pallas_call(
        _dispatch_kernel,
        grid=(M // tmd,),
        in_specs=[pl.BlockSpec((TOP_K, tmd), lambda i: (0, i), memory_space=pltpu.SMEM),
                  pl.BlockSpec((tmd, ROW_TILE, LANES), lambda i: (i, 0, 0))],
        out_specs=pl.BlockSpec(memory_space=pl.ANY),
        out_shape=jax.ShapeDtypeStruct((n_rows, ROW_TILE, LANES), h2.dtype),
        scratch_shapes=[pltpu.SemaphoreType.DMA],
        compiler_params=pltpu.CompilerParams(dimension_semantics=("arbitrary",),
                                             vmem_limit_bytes=VMEM_LIMIT_BYTES),
        name="dispatch",
    )(slot, h2)


FFN_PIPELINE_STEPS = 2


def _ffn_kernel(te_ref, nu_ref, nv_ref, x_ref, wg_ref, wu_ref, wd_ref, y_ref, wgu_s, wd_s, xstd, ystd):
    i = pl.program_id(0)
    row = lax.broadcasted_iota(jnp.int32, (MOE_ROWS, 1), 0)
    tile_mm = jnp.maximum(i - 1, 0)
    tile_out = jnp.maximum(i - 2, 0)

    @pl.when(i == 0)
    def _():
        xstd[...] = jnp.zeros(xstd.shape, BF16)
        ystd[...] = jnp.zeros(ystd.shape, F32)

    @pl.when(i < nu_ref[0] + FFN_PIPELINE_STEPS)
    def _():
        @pl.when((i == 0) | (te_ref[tile_mm] != te_ref[tile_out]))
        def _():
            wgu_s[:, :D_EXPERT] = wg_ref[0].astype(BF16)
            wgu_s[:, D_EXPERT:] = wu_ref[0].astype(BF16)
            wd_s[...] = wd_ref[0].astype(BF16)

        cur = i % 2
        xstd[cur] = jnp.where(row < nv_ref[i], _to_rows(x_ref[...].astype(F32)), 0.0).astype(BF16)
        y_ref[...] = _to_row_tiles(ystd[cur]).astype(y_ref.dtype)
        gu = jnp.dot(xstd[1 - cur], wgu_s[...], preferred_element_type=F32)
        act = _silu(gu[:, :D_EXPERT]) * gu[:, D_EXPERT:]
        ystd[1 - cur] = jnp.dot(act.astype(BF16), wd_s[...], preferred_element_type=F32)


def _ffn(tile_expert, n_used, n_valid, xs, wg, wu, wd):
    D = wg.shape[1]
    n_tiles = xs.shape[0] // MOE_ROWS
    block = (MOE_ROWS, ROW_TILE, LANES)
    expert = lambda i, te, nu, nv: (te[jnp.maximum(i - 1, 0)], 0, 0)
    return pl.pallas_call(
        _ffn_kernel,
        grid_spec=pltpu.PrefetchScalarGridSpec(
            num_scalar_prefetch=3,
            grid=(n_tiles + FFN_PIPELINE_STEPS,),
            in_specs=[pl.BlockSpec(block, lambda i, te, nu, nv: (jnp.minimum(i, nu[0] - 1), 0, 0)),
                      pl.BlockSpec((1, D, D_EXPERT), expert),
                      pl.BlockSpec((1, D, D_EXPERT), expert),
                      pl.BlockSpec((1, D_EXPERT, D), expert)],
            out_specs=pl.BlockSpec(block, lambda i, te, nu, nv: (jnp.clip(i - 2, 0, nu[0] - 1), 0, 0)),
            scratch_shapes=[pltpu.VMEM((D, 2 * D_EXPERT), BF16), pltpu.VMEM((D_EXPERT, D), BF16),
                            pltpu.VMEM((2, MOE_ROWS, D), BF16), pltpu.VMEM((2, MOE_ROWS, D), F32)]),
        out_shape=jax.ShapeDtypeStruct(xs.shape, xs.dtype),
        compiler_params=pltpu.CompilerParams(dimension_semantics=("arbitrary",),
                                             vmem_limit_bytes=VMEM_LIMIT_BYTES),
        name="ffn",
    )(tile_expert, n_used, n_valid, xs, wg, wu, wd)


def _combine_kernel(slot_ref, next_slot_ref, w_ref, h_ref, x1_ref, g2_ref, fw_ref, wsgu_ref, wsd_ref, y_ref,
                    o_ref, buf0, buf1, sem):
    tmc = h_ref.shape[0]
    i = pl.program_id(0)
    last = pl.num_programs(0) - 1
    bufs = (buf0, buf1)
    assert TOP_K == ROW_TILE

    def start_rows(slots, b):
        for j in range(tmc):
            for k in range(TOP_K):
                pltpu.make_async_copy(y_ref.at[pl.ds(slots[k, j], 1)], bufs[b].at[k, pl.ds(j, 1)], sem.at[b]).start(
                    priority=k % 2)

    def wait_rows(b):
        for k in range(TOP_K):
            pltpu.make_async_copy(y_ref.at[pl.ds(0, tmc)], bufs[b].at[k], sem.at[b]).wait()

    @pl.when(i == 0)
    def _():
        start_rows(slot_ref, 0)

    def step(b):
        wait_rows(b)
        start_rows(next_slot_ref, 1 - b)
        gu = jnp.dot(_to_rows(h_ref[...].astype(F32)).astype(BF16), wsgu_ref[...], preferred_element_type=F32)
        act = _silu(gu[:, :D_EXPERT]) * gu[:, D_EXPERT:]
        shared = jnp.dot(act.astype(BF16), wsd_ref[...], preferred_element_type=F32)
        w = w_ref[...]
        lane = lax.broadcasted_iota(jnp.int32, w.shape, 1)
        w3 = _to_row_tiles(jnp.concatenate(
            [jnp.broadcast_to(_lane_pick(w, lane, k), (tmc, LANES)) for k in range(TOP_K)], axis=1))
        routed = jnp.zeros((tmc, ROW_TILE, LANES), F32)
        for k in range(TOP_K):
            routed = routed + jnp.broadcast_to(w3[:, k:k + 1, :], w3.shape) * bufs[b][k].astype(F32)
        acc = shared + _to_rows(routed)
        y = x1_ref[...] + g2_ref[0] * acc
        o_ref[...] = _rms(y) * fw_ref[...]

        @pl.when(i == last)
        def _():
            wait_rows(1 - b)

    for b in range(2):
        @pl.when(i % 2 == b)
        def _():
            step(b)


def _combine(slot, wts, h2, x1, g2, fw, wsgu, wsd, ys, T, tmc):
    M, D = x1.shape
    const = lambda shape: pl.BlockSpec(shape, lambda i: (0,) * len(shape))
    tile = lambda w: pl.BlockSpec((tmc, w), lambda i: (i, 0))
    row_tiled = pl.BlockSpec((tmc, ROW_TILE, LANES), lambda i: (i, 0, 0))
    n_steps = M // tmc
    return pl.pallas_call(
        _combine_kernel,
        grid=(n_steps,),
        in_specs=[pl.BlockSpec((TOP_K, tmc), lambda i: (0, i), memory_space=pltpu.SMEM),
                  pl.BlockSpec((TOP_K, tmc), lambda i: (0, jnp.minimum(i + 1, n_steps - 1)), memory_space=pltpu.SMEM),
                  tile(GATE_LANES), row_tiled, tile(D),
                  pl.BlockSpec((1, 1, D), lambda i: (i // (T // tmc), 0, 0)),
                  const((1, D)), const(wsgu.shape), const(wsd.shape),
                  pl.BlockSpec(memory_space=pl.ANY)],
        out_specs=tile(D),
        out_shape=jax.ShapeDtypeStruct((M, D), F32),
        scratch_shapes=[pltpu.VMEM((TOP_K, tmc, ROW_TILE, LANES), ys.dtype),
                        pltpu.VMEM((TOP_K, tmc, ROW_TILE, LANES), ys.dtype), pltpu.SemaphoreType.DMA((2,))],
        compiler_params=pltpu.CompilerParams(dimension_semantics=("arbitrary",),
                                             vmem_limit_bytes=VMEM_LIMIT_BYTES),
        name="combine",
    )(slot, slot, wts, h2, x1, g2, fw, wsgu, wsd, ys)


def _pick_tile(n, want):
    t = min(n, want)
    assert n % t == 0 and t % CHUNK == 0, (n, want)
    return t


def kernel(x, c, w_ada, b_ada, norm1_w, w_in, conv_w, gdn_a_log, gdn_dt_bias, gdn_norm_w, hg_lb, hg_norm_w,
           w_out, norm2_w, w_router, router_bias, w_gate, w_up, w_down, ws_gate, ws_up, ws_down, final_norm_w):
    B, T, D = x.shape
    M = B * T
    depth = w_ada.shape[0]
    assert depth == 1 and T % CHUNK == 0 and B <= 8
    layer = 0
    tt = _pick_tile(T, TILE_RECURRENCE)

    c_pad = jnp.pad(c, ((0, 8 - B), (0, 0)))
    mod = _ada(c_pad, w_ada[layer], b_ada[layer].reshape(1, -1))[:B]
    sh1, sc1, g1, sh2, sc2, g2 = (m.reshape(B, 1, D) for m in jnp.split(mod, 6, axis=-1))

    w = w_in[layer]
    qkv_w = 3 * GW
    sizes = (GW, HEADS, HEADS, GW, GW, GW, GW)
    offs = [qkv_w]
    for s in sizes:
        offs.append(offs[-1] + s)
    seg = lambda i: w[:, offs[i]:offs[i + 1]]
    small = jnp.pad(jnp.concatenate([seg(1), seg(2)], axis=1), ((0, 0), (0, GATE_LANES - 2 * HEADS)))
    w_all = jnp.concatenate([w[:, :qkv_w], seg(0), seg(3), seg(4), seg(5), seg(6), small], axis=1).astype(BF16)
    lane_pad = lambda v: jnp.pad(v.astype(F32).reshape(1, HEADS), ((0, 0), (HEADS, GATE_LANES - 2 * HEADS)))
    idx = jnp.arange(CUMSUM_ROWS)
    tri = ((idx[:, None] >= idx[None, :]) & (idx[:, None] // CHUNK == idx[None, :] // CHUNK)).astype(BF16)

    qa, ka, va, ga, sm, bcum, kb, ib, qb, gb = _inproj(
        layer, x, norm1_w[layer].reshape(1, D), sc1, sh1, w_all, conv_w[layer].astype(F32),
        lane_pad(gdn_a_log[layer]), lane_pad(gdn_dt_bias[layer]), hg_lb.astype(F32), tri,
        _pick_tile(T, TILE_INPROJ))

    gct = sm[:, :, HEADS:2 * HEADS].transpose(0, 2, 1).reshape(B, HEADS, T // CHUNK, CHUNK)
    oa = _gdn(qa, ka, va, ga, sm, gct, gdn_norm_w[layer].reshape(1, DH), tt)
    ob = _hgrn(qb, kb, ib, bcum, gb, hg_norm_w[layer].reshape(1, GW), tt)

    wo = w_out[layer].astype(BF16)
    x1, h2, eidx, rank, wts, cnt = _outproj(oa, ob, x, wo[:GW], wo[GW:], g1, norm2_w[layer].reshape(1, D), sc2, sh2,
                                            w_router[layer].T, router_bias[layer].reshape(N_EXPERTS, 1),
                                            _pick_tile(T, TILE_OUTPROJ))

    counts = cnt[:, 0].astype(jnp.int32)
    padded = (counts + MOE_ROWS - 1) // MOE_ROWS * MOE_ROWS
    ends = jnp.cumsum(padded)
    offsets = ends - padded
    n_tiles = (M * TOP_K) // MOE_ROWS + N_EXPERTS
    n_used = (ends[-1] // MOE_ROWS).astype(jnp.int32)
    tile_ids = jnp.minimum(jnp.arange(n_tiles + FFN_PIPELINE_STEPS, dtype=jnp.int32), n_used - 1)
    tile_expert = jnp.sum(((ends // MOE_ROWS)[None, :] <= tile_ids[:, None]).astype(jnp.int32), axis=1)
    tile_expert = jnp.minimum(tile_expert, N_EXPERTS - 1)
    slot = _slots(offsets.astype(jnp.int32), eidx, rank)
    first_tile = jnp.take(offsets, tile_expert) // MOE_ROWS
    n_valid = jnp.clip(jnp.take(counts, tile_expert) - (tile_ids - first_tile) * MOE_ROWS, 0, MOE_ROWS)

    assert D == ROW_TILE * LANES
    xs = _dispatch(slot, h2, n_tiles * MOE_ROWS, _pick_tile(M, TILE_DISPATCH))
    ys = _ffn(tile_expert.astype(jnp.int32), n_used.reshape(1), n_valid.astype(jnp.int32), xs,
              w_gate[layer], w_up[layer], w_down[layer])
    wsgu = jnp.concatenate([ws_gate[layer], ws_up[layer]], axis=-1).astype(BF16)
    out = _combine(slot, wts.reshape(M, GATE_LANES), h2, x1.reshape(M, D), g2, final_norm_w.reshape(1, D),
                   wsgu, ws_down[layer].astype(BF16), ys, T, _pick_tile(T, TILE_COMBINE))
    return out.reshape(B, T, D)
```

```python
import functools

import jax
import jax.numpy as jnp
from jax import lax
from jax.experimental import pallas as pl
from jax.experimental.pallas import tpu as pltpu

F32 = jnp.float32
BF16 = jnp.bfloat16

EPS = 1e-6
CHUNK = 64
SUB = 8
HEADS = 4
DH = 128
GW = HEADS * DH
CONV_K = 4
N_EXPERTS = 64
N_GROUPS = 8
GROUP_SIZE = N_EXPERTS // N_GROUPS
TOPK_GROUPS = 4
TOP_K = 8
D_EXPERT = 256
ROUTE_SCALE = 2.5
GATE_LANES = 128
GDN_CHUNKS_PER_ITER = 8
INV_BLOCK = 16
CUMSUM_ROWS = 128
MOE_ROWS = 1024
ROW_TILE, LANES = 8, 128

TILE_INPROJ = 1024
TILE_RECURRENCE = 512
TILE_OUTPROJ = 1024
TILE_DISPATCH = 256
TILE_COMBINE = 256
ADA_COLS = 1024

VMEM_LIMIT_BYTES = 56 * 1024 * 1024

ACT = BF16
ROW_DTYPE = F32


def _silu(x):
    return x * jax.nn.sigmoid(x)


def _dot(a, b):
    return jnp.dot(a.astype(BF16), b.astype(BF16), preferred_element_type=F32)


def _dot_nt(a, b):
    return lax.dot_general(a.astype(BF16), b.astype(BF16), (((1,), (1,)), ((), ())),
                           preferred_element_type=F32)


def _dot_tn(a, b):
    return lax.dot_general(a.astype(BF16), b.astype(BF16), (((0,), (0,)), ((), ())),
                           preferred_element_type=F32)


def _split2(x):
    hi = x.astype(BF16)
    lo = (x - hi.astype(F32)).astype(BF16)
    return hi, lo


def _dot3(a, b, dot=_dot):
    ah, al = _split2(a)
    bh, bl = _split2(b)
    return dot(ah, bh) + dot(ah, bl) + dot(al, bh)


def _cumsum_rows(tri, x):
    hi = x.astype(BF16)
    r = x - hi.astype(F32)
    mid = r.astype(BF16)
    lo = (r - mid.astype(F32)).astype(BF16)
    g = tri.shape[0]
    groups = []
    for r0 in range(0, x.shape[0], g):
        rows = slice(r0, r0 + g)
        groups.append(jnp.dot(tri, hi[rows], preferred_element_type=F32)
                      + jnp.dot(tri, mid[rows], preferred_element_type=F32)
                      + jnp.dot(tri, lo[rows], preferred_element_type=F32))
    return jnp.concatenate(groups, axis=0)


def _lane_pick(tile, lane, idx):
    return jnp.sum(jnp.where(lane == idx, tile, 0.0), axis=1, keepdims=True)


def _rms(x):
    return x * lax.rsqrt(jnp.mean(x * x, axis=-1, keepdims=True) + EPS)


def _to_rows(x3):
    r = x3.shape[0]
    xt = jnp.swapaxes(x3.reshape(r // ROW_TILE, ROW_TILE, ROW_TILE, LANES), 1, 2)
    return jnp.concatenate([xt[:, s].reshape(r, LANES) for s in range(ROW_TILE)], axis=1)


def _to_row_tiles(x):
    r = x.shape[0]
    xt = jnp.stack([x[:, s * LANES:(s + 1) * LANES].reshape(r // ROW_TILE, ROW_TILE, LANES) for s in range(ROW_TILE)],
                   axis=1)
    return jnp.swapaxes(xt, 1, 2).reshape(r, ROW_TILE, LANES)


def _ada_kernel(c_ref, w_ref, b_ref, o_ref):
    ca = _silu(c_ref[...])
    o_ref[...] = _dot3(ca, w_ref[...]) + b_ref[...]


def _ada(c_pad, w, b):
    rows, d = c_pad.shape
    n = w.shape[1]
    tn = ADA_COLS
    return pl.pallas_call(
        _ada_kernel,
        grid=(n // tn,),
        in_specs=[pl.BlockSpec((rows, d), lambda j: (0, 0)),
                  pl.BlockSpec((d, tn), lambda j: (0, j)),
                  pl.BlockSpec((1, tn), lambda j: (0, j))],
        out_specs=pl.BlockSpec((rows, tn), lambda j: (0, j)),
        out_shape=jax.ShapeDtypeStruct((rows, n), F32),
        compiler_params=pltpu.CompilerParams(dimension_semantics=("arbitrary",),
                                             vmem_limit_bytes=VMEM_LIMIT_BYTES),
        name="ada",
    )(c_pad, w, b)


def _inproj_kernel(layer, x_ref, n1_ref, sc_ref, sh_ref, w_ref, cw_ref, alog_ref, dt_ref, lb_ref, tri_ref,
                   qa_ref, ka_ref, va_ref, ga_ref, sm_ref, b_ref, kb_ref, ib_ref, qb_ref, gb_ref,
                   pbuf):
    tt = x_ref.shape[1]
    t = pl.program_id(1)

    h = _rms(x_ref[0]) * n1_ref[...]
    h = h * (1.0 + sc_ref[0]) + sh_ref[0]
    hb = h.astype(BF16)

    def proj(g, width=GW):
        return jnp.dot(hb, w_ref[:, g * GW:g * GW + width], preferred_element_type=F32)

    @pl.when(t == 0)
    def _():
        pbuf[:, 0:8, :] = jnp.zeros((3, 8, GW), F32)

    for g, out_ref in enumerate((qa_ref, ka_ref, va_ref)):
        cols = slice(g * GW, (g + 1) * GW)
        p = proj(g)
        pbuf[g, 8:8 + tt, :] = p
        y = p * cw_ref[CONV_K - 1:CONV_K, cols]
        for j in range(1, CONV_K):
            y = y + pbuf[g, 8 - j:8 - j + tt, :] * cw_ref[CONV_K - 1 - j:CONV_K - j, cols]
        pbuf[g, 0:8, :] = pbuf[g, tt:tt + 8, :]
        y = _silu(y)
        if g == 2:
            out_ref[0] = y.astype(out_ref.dtype)
        else:
            scale = DH ** -0.5 if g == 0 else 1.0
            for hh in range(HEADS):
                hs = slice(hh * DH, (hh + 1) * DH)
                yh = y[:, hs]
                inv = lax.rsqrt(jnp.sum(yh * yh, axis=-1, keepdims=True) + EPS)
                out_ref[0, :, hs] = (yh * inv * scale).astype(out_ref.dtype)

    ga_ref[0] = _silu(proj(3)).astype(ga_ref.dtype)

    ps = proj(8, GATE_LANES)
    lane = lax.broadcasted_iota(jnp.int32, ps.shape, 1)
    beta = jax.nn.sigmoid(ps)
    z = ps + dt_ref[...]
    softplus = jnp.maximum(z, 0.0) + jnp.log1p(jnp.exp(-jnp.abs(z)))
    g_log = -jnp.exp(alog_ref[...]) * softplus
    tri = tri_ref[...]
    gc = _cumsum_rows(tri, jnp.where((lane >= HEADS) & (lane < 2 * HEADS), g_log, 0.0))
    sm_ref[0] = jnp.where(lane < HEADS, beta, gc)

    hl = lb_ref[...]
    e = jnp.exp(hl - jnp.max(hl, axis=0, keepdims=True))
    lb = jnp.sum(e[0:layer + 1], axis=0, keepdims=True) / jnp.sum(e, axis=0, keepdims=True)
    fr = proj(4)
    logf = jnp.log(lb + (1.0 - lb) * jax.nn.sigmoid(fr))
    b_ref[0] = _cumsum_rows(tri, logf)
    kb_ref[0] = ((1.0 - lb) * jax.nn.sigmoid(-fr)).astype(kb_ref.dtype)
    ib_ref[0] = proj(5).astype(ib_ref.dtype)
    qb_ref[0] = _silu(proj(6)).astype(qb_ref.dtype)
    gb_ref[0] = _silu(proj(7)).astype(gb_ref.dtype)


def _inproj(layer, x, n1, sc1, sh1, w_all, conv_w, alog_pad, dt_pad, hg_lb, tri, tt):
    B, T, D = x.shape
    const = lambda shape: pl.BlockSpec(shape, lambda b, t: (0,) * len(shape), pipeline_mode=pl.Buffered(1))
    act = lambda dt: jax.ShapeDtypeStruct((B, T, GW), dt)
    tile = lambda w: pl.BlockSpec((1, tt, w), lambda b, t: (b, t, 0))
    per_batch = pl.BlockSpec((1, 1, D), lambda b, t: (b, 0, 0))
    return pl.pallas_call(
        functools.partial(_inproj_kernel, layer),
        grid=(B, T // tt),
        in_specs=[tile(D), const((1, D)), per_batch, per_batch,
                  const(w_all.shape), const(conv_w.shape), const((1, GATE_LANES)), const((1, GATE_LANES)),
                  const(hg_lb.shape), const(tri.shape)],
        out_specs=[tile(GW), tile(GW), tile(GW), tile(GW), tile(GATE_LANES), tile(GW),
                   tile(GW), tile(GW), tile(GW), tile(GW)],
        out_shape=[act(ACT), act(ACT), act(ACT), act(ACT),
                   jax.ShapeDtypeStruct((B, T, GATE_LANES), F32), act(F32),
                   act(ACT), act(ACT), act(ACT), act(ACT)],
        scratch_shapes=[pltpu.VMEM((3, tt + 8, GW), F32)],
        compiler_params=pltpu.CompilerParams(dimension_semantics=("arbitrary", "arbitrary"),
                                             vmem_limit_bytes=VMEM_LIMIT_BYTES),
        name="inproj",
    )(x, n1, sc1, sh1, w_all, conv_w, alog_pad, dt_pad, hg_lb, tri)


def _gdn_prep_kernel(q_ref, k_ref, v_ref, sm_ref, gct_ref, o_ref, qt_ref, m_ref, n_ref):
    tt = q_ref.shape[1]
    nc = tt // CHUNK
    row = lax.broadcasted_iota(jnp.int32, (CHUNK, CHUNK), 0)
    col = lax.broadcasted_iota(jnp.int32, (CHUNK, CHUNK), 1)
    causal = row >= col
    diag_blk = (row > col) & (row // INV_BLOCK == col // INV_BLOCK)
    off_blk = row // INV_BLOCK > col // INV_BLOCK
    eye = jnp.where(row == col, 1.0, 0.0)
    lane = lax.broadcasted_iota(jnp.int32, (CHUNK, GATE_LANES), 1)
    assert INV_BLOCK == 16 and CHUNK == 4 * INV_BLOCK

    def body(i, carry):
        chains = [(GDN_CHUNKS_PER_ITER * i + j, hh) for j in range(GDN_CHUNKS_PER_ITER) for hh in range(HEADS)]
        rows = [pl.ds(pl.multiple_of(c * CHUNK, CHUNK), CHUNK) for c, _ in chains]
        hs = [slice(hh * DH, (hh + 1) * DH) for _, hh in chains]
        n = range(len(chains))
        sm = [sm_ref[0, rows[j], :] for j in n]
        q = [q_ref[0, rows[j], hs[j]].astype(F32) for j in n]
        k = [k_ref[0, rows[j], hs[j]].astype(F32) for j in n]
        v = [v_ref[0, rows[j], hs[j]].astype(F32) for j in n]
        beta = [_lane_pick(sm[j], lane, chains[j][1]) for j in n]
        gcol = [_lane_pick(sm[j], lane, HEADS + chains[j][1]) for j in n]
        grow = [gct_ref[0, hh, pl.ds(c, 1), :] for c, hh in chains]
        decay = [jnp.exp(jnp.where(causal, gcol[j] - grow[j], -jnp.inf)) for j in n]
        kb = [k[j] * beta[j] for j in n]
        L = [_dot_nt(kb[j], k[j]) * decay[j] for j in n]
        dg = [jnp.where(diag_blk, L[j], 0.0) for j in n]
        off = [jnp.where(off_blk, L[j], 0.0) for j in n]
        dinv = [eye - dg[j] for j in n]
        pw = [_dot3(dg[j], dg[j]) for j in n]
        for _ in range(2):
            dinv = [dinv[j] + _dot3(dinv[j], pw[j]) for j in n]
            pw = [_dot3(pw[j], pw[j]) for j in n]
        dinv = [dinv[j] + _dot3(dinv[j], pw[j]) for j in n]
        f1 = [_dot(dinv[j], off[j]) for j in n]
        f2 = [_dot(f1[j], f1[j]) for j in n]
        f3 = [_dot(f1[j], f2[j]) for j in n]
        tinv = [_dot(eye - f1[j] + f2[j] - f3[j], dinv[j]) for j in n]
        eg = [jnp.exp(gcol[j]) for j in n]
        sol = [_dot(tinv[j], jnp.concatenate([v[j] * beta[j], kb[j] * eg[j]], axis=1)) for j in n]
        attn = [_dot_nt(q[j], k[j]) * decay[j] for j in n]
        k_tail = [k[j] * jnp.exp(gcol[j][CHUNK - 1:CHUNK, :] - gcol[j]) for j in n]
        au = [_dot(attn[j], sol[j]) for j in n]
        ku = [_dot_tn(k_tail[j], sol[j]) for j in n]
        for j, (c, hh) in enumerate(chains):
            o_ref[0, rows[j], hs[j]] = au[j][:, :DH]
            qt_ref[0, rows[j], hs[j]] = (q[j] * eg[j] - au[j][:, DH:]).astype(qt_ref.dtype)
            n_ref[0, hh, c] = ku[j][:, :DH].astype(n_ref.dtype)
            m_ref[0, hh, c] = (-ku[j][:, DH:]).astype(m_ref.dtype)
        return carry

    lax.fori_loop(0, nc // GDN_CHUNKS_PER_ITER, body, 0)


def _gdn_scan_kernel(o_ref, qt_ref, m_ref, n_ref, gct_ref, sg_ref, nw_ref, out_ref, s_ref):
    nb, tt = o_ref.shape[0], o_ref.shape[1]
    nc = tt // CHUNK

    @pl.when(pl.program_id(0) == 0)
    def _():
        s_ref[...] = jnp.zeros(s_ref.shape, F32)

    nw = nw_ref[...]

    def body(c, carry):
        rows = pl.ds(pl.multiple_of(c * CHUNK, CHUNK), CHUNK)
        for b in range(nb):
            for hh in range(HEADS):
                hs = slice(hh * DH, (hh + 1) * DH)
                S = s_ref[b, hh]
                Sb = S.astype(BF16)
                glast = gct_ref[b, hh, pl.ds(c, 1), :][:, CHUNK - 1:CHUNK]
                o = o_ref[b, rows, hs] + jnp.dot(qt_ref[b, rows, hs], Sb, preferred_element_type=F32)
                s_ref[b, hh] = (S * jnp.exp(glast) + jnp.dot(m_ref[b, hh, c], Sb, preferred_element_type=F32)
                                + n_ref[b, hh, c].astype(F32))
                o = _rms(o) * nw * sg_ref[b, rows, hs].astype(F32)
                out_ref[b, rows, hs] = o.astype(out_ref.dtype)
        return carry

    lax.fori_loop(0, nc, body, 0)


def _gdn(q, k, v, sg, sm, gct, nw, tt):
    B, T, _ = q.shape
    nc = tt // CHUNK
    assert nc % GDN_CHUNKS_PER_ITER == 0
    n_chunks = T // CHUNK
    tile = lambda w: pl.BlockSpec((1, tt, w), lambda b, t: (b, t, 0))
    mat = jax.ShapeDtypeStruct((B, HEADS, n_chunks, DH, DH), ACT)
    o_part, qt, m, n = pl.pallas_call(
        _gdn_prep_kernel,
        grid=(B, T // tt),
        in_specs=[tile(GW), tile(GW), tile(GW), tile(GATE_LANES),
                  pl.BlockSpec((1, HEADS, nc, CHUNK), lambda b, t: (b, 0, t, 0))],
        out_specs=[tile(GW), tile(GW),
                   pl.BlockSpec((1, HEADS, nc, DH, DH), lambda b, t: (b, 0, t, 0, 0)),
                   pl.BlockSpec((1, HEADS, nc, DH, DH), lambda b, t: (b, 0, t, 0, 0))],
        out_shape=[jax.ShapeDtypeStruct((B, T, GW), F32), jax.ShapeDtypeStruct((B, T, GW), ACT), mat, mat],
        compiler_params=pltpu.CompilerParams(dimension_semantics=("arbitrary", "arbitrary"),
                                             vmem_limit_bytes=VMEM_LIMIT_BYTES),
        name="gdn_prep",
    )(q, k, v, sm, gct)

    full = lambda w: pl.BlockSpec((B, tt, w), lambda t: (0, t, 0))
    mats = pl.BlockSpec((B, HEADS, nc, DH, DH), lambda t: (0, 0, t, 0, 0))
    return pl.pallas_call(
        _gdn_scan_kernel,
        grid=(T // tt,),
        in_specs=[full(GW), full(GW), mats, mats,
                  pl.BlockSpec((B, HEADS, nc, CHUNK), lambda t: (0, 0, t, 0)),
                  full(GW), pl.BlockSpec((1, DH), lambda t: (0, 0))],
        out_specs=full(GW),
        out_shape=jax.ShapeDtypeStruct((B, T, GW), ACT),
        scratch_shapes=[pltpu.VMEM((B, HEADS, DH, DH), F32)],
        compiler_params=pltpu.CompilerParams(dimension_semantics=("arbitrary",),
                                             vmem_limit_bytes=VMEM_LIMIT_BYTES),
        name="gdn_scan",
    )(o_part, qt, m, n, gct, sg, nw)


def _hgrn_kernel(q_ref, k_ref, v_ref, b_ref, sg_ref, nw_ref, o_ref, st_ref):
    nb, tt = q_ref.shape[0], q_ref.shape[1]
    nc = tt // CHUNK

    @pl.when(pl.program_id(0) == 0)
    def _():
        st_ref[...] = jnp.zeros(st_ref.shape, F32)

    row = lax.broadcasted_iota(jnp.int32, (CHUNK, CHUNK), 0)
    col = lax.broadcasted_iota(jnp.int32, (CHUNK, CHUNK), 1)
    diag_block = ((col // SUB) == (row // SUB)) & (col <= row)
    chains = [(bi, hh) for bi in range(nb) for hh in range(HEADS)]
    n = range(len(chains))
    hs = [slice(hh * DH, (hh + 1) * DH) for _, hh in chains]

    def body(c, carry):
        rows = pl.ds(pl.multiple_of(c * CHUNK, CHUNK), CHUNK)
        q = [q_ref[bi, rows, hs[j]].astype(F32) for j, (bi, _) in enumerate(chains)]
        k = [k_ref[bi, rows, hs[j]].astype(F32) for j, (bi, _) in enumerate(chains)]
        v = [v_ref[bi, rows, hs[j]].astype(F32) for j, (bi, _) in enumerate(chains)]
        b = [b_ref[bi, rows, hs[j]] for j, (bi, _) in enumerate(chains)]
        blast = [b[j][CHUNK - 1:CHUNK, :] for j in n]
        st = [st_ref[bi, hh] for bi, hh in chains]
        o = [_dot_nt(q[j] * jnp.exp(b[j]), st[j]) for j in n]
        k_tail = [k[j] * jnp.exp(blast[j] - b[j]) for j in n]
        for j, (bi, hh) in enumerate(chains):
            st_ref[bi, hh] = st[j] * jnp.exp(blast[j]) + _dot_tn(v[j], k_tail[j])

        blocks = [[jnp.zeros((SUB, CHUNK), F32)] for _ in n]
        for i in range(1, CHUNK // SUB):
            lo, hi = i * SUB, (i + 1) * SUB
            for j in n:
                r = b[j][lo:lo + 1, :]
                qi = q[j][lo:hi] * jnp.exp(b[j][lo:hi] - r)
                kj = k[j][:lo] * jnp.exp(jnp.minimum(r - b[j][:lo], 0.0))
                kj = jnp.concatenate([kj, jnp.zeros((CHUNK - lo, DH), F32)], axis=0)
                blocks[j].append(_dot_nt(qi, kj))
        a = []
        for j in n:
            f = jnp.exp(jnp.minimum(b[j] - pltpu.roll(b[j], 1, 0), 0.0))
            e = None
            a_diag = jnp.zeros((CHUNK, CHUNK), F32)
            for delta in range(SUB):
                if delta == 0:
                    term = q[j] * k[j]
                else:
                    fsh = f if delta == 1 else pltpu.roll(f, delta - 1, 0)
                    e = fsh if e is None else e * fsh
                    term = q[j] * pltpu.roll(k[j], delta, 0) * e
                colv = jnp.sum(term, axis=1, keepdims=True)
                a_diag = jnp.where(row - col == delta, colv, a_diag)
            a.append(jnp.where(diag_block, a_diag, jnp.concatenate(blocks[j], axis=0)))

        o = [o[j] + _dot(a[j], v[j]) for j in n]
        for bi in range(nb):
            ob = jnp.concatenate(o[bi * HEADS:(bi + 1) * HEADS], axis=1)
            ob = _rms(ob) * nw_ref[...] * sg_ref[bi, rows, :].astype(F32)
            o_ref[bi, rows, :] = ob.astype(o_ref.dtype)
        return carry

    lax.fori_loop(0, nc, body, 0)


def _hgrn(q, k, v, b, sg, nw, tt):
    B, T, _ = q.shape
    tile = pl.BlockSpec((B, tt, GW), lambda t: (0, t, 0))
    return pl.pallas_call(
        _hgrn_kernel,
        grid=(T // tt,),
        in_specs=[tile, tile, tile, tile, tile, pl.BlockSpec((1, GW), lambda t: (0, 0))],
        out_specs=tile,
        out_shape=jax.ShapeDtypeStruct((B, T, GW), ACT),
        scratch_shapes=[pltpu.VMEM((B, HEADS, DH, DH), F32)],
        compiler_params=pltpu.CompilerParams(dimension_semantics=("arbitrary",),
                                             vmem_limit_bytes=VMEM_LIMIT_BYTES),
        name="hgrn",
    )(q, k, v, b, sg, nw)


def _outproj_kernel(oa_ref, ob_ref, x_ref, wa_ref, wb_ref, g1_ref, n2_ref, sc_ref, sh_ref, wr_ref, rb_ref,
                    x1_ref, h2_ref, eidx_ref, rank_ref, wts_ref, cnt_ref):
    tm = x_ref.shape[1]

    @pl.when((pl.program_id(0) == 0) & (pl.program_id(1) == 0))
    def _():
        cnt_ref[...] = jnp.zeros(cnt_ref.shape, F32)

    mix = (jnp.dot(oa_ref[0], wa_ref[...], preferred_element_type=F32)
           + jnp.dot(ob_ref[0], wb_ref[...], preferred_element_type=F32))
    x1 = x_ref[0] + g1_ref[0] * mix
    x1_ref[0] = x1
    h2 = _rms(x1) * n2_ref[...]
    h2 = h2 * (1.0 + sc_ref[0]) + sh_ref[0]
    h2_ref[...] = _to_row_tiles(h2).astype(h2_ref.dtype)

    scores = jax.nn.sigmoid(_dot3(wr_ref[...], h2, dot=_dot_nt))
    sel = scores + rb_ref[...]
    sub = lax.broadcasted_iota(jnp.int32, (GROUP_SIZE, tm), 0)
    neg = -jnp.inf
    groups = range(N_GROUPS)

    def take_max(blk):
        m = jnp.max(blk, axis=0, keepdims=True)
        first = jnp.min(jnp.where(blk == m, sub, GROUP_SIZE), axis=0, keepdims=True)
        hit = sub == first
        return m, hit, jnp.where(hit, neg, blk)

    blk_of = lambda a, g: a[g * GROUP_SIZE:(g + 1) * GROUP_SIZE]
    sel_blk = [blk_of(sel, g) for g in groups]
    group_score = jnp.zeros((N_GROUPS, tm), F32)
    for g in groups:
        m1, _, rest = take_max(sel_blk[g])
        m2 = jnp.max(rest, axis=0, keepdims=True)
        group_score = jnp.where(sub == g, m1 + m2, group_score)
    group_on = jnp.zeros((N_GROUPS, tm), F32)
    for _ in range(TOPK_GROUPS):
        _, hit, group_score = take_max(group_score)
        group_on = jnp.where(hit, 1.0, group_on)

    cand = [jnp.where(group_on[g:g + 1] > 0.0, sel_blk[g], neg) for g in groups]
    picked = [jnp.zeros((GROUP_SIZE, tm), F32) for _ in groups]
    chosen = []
    for _ in range(TOP_K):
        m = jnp.max(functools.reduce(jnp.maximum, cand), axis=0, keepdims=True)
        first = functools.reduce(jnp.minimum, [jnp.where(cand[g] == m, sub + g * GROUP_SIZE, N_EXPERTS)
                                               for g in groups])
        first = jnp.min(first, axis=0, keepdims=True)
        chosen.append(first)
        for g in groups:
            hit = (sub + g * GROUP_SIZE) == first
            picked[g] = jnp.where(hit, 1.0, picked[g])
            cand[g] = jnp.where(hit, neg, cand[g])

    picked_all = jnp.concatenate(picked, axis=0)
    r_i = lax.broadcasted_iota(jnp.int32, (tm, tm), 0)
    c_i = lax.broadcasted_iota(jnp.int32, (tm, tm), 1)
    earlier = jnp.where(r_i < c_i, 1.0, 0.0).astype(BF16)
    before = jnp.dot(picked_all.astype(BF16), earlier, preferred_element_type=F32) + cnt_ref[:, 0:1]
    cnt_ref[...] = cnt_ref[...] + jnp.sum(picked_all, axis=1, keepdims=True)

    def pick_value(table, first):
        parts = [jnp.where((sub + g * GROUP_SIZE) == first, blk_of(table, g), 0.0) for g in groups]
        return jnp.sum(functools.reduce(jnp.add, parts), axis=0, keepdims=True)

    w_k = [pick_value(scores, f) for f in chosen]
    denom = functools.reduce(jnp.add, w_k)
    eidx = jnp.zeros((TOP_K, tm), jnp.int32)
    rank = jnp.zeros((TOP_K, tm), jnp.int32)
    wts = jnp.zeros((TOP_K, tm), F32)
    for k in range(TOP_K):
        eidx = jnp.where(sub == k, chosen[k], eidx)
        rank = jnp.where(sub == k, pick_value(before, chosen[k]).astype(jnp.int32), rank)
        wts = jnp.where(sub == k, w_k[k] / denom * ROUTE_SCALE, wts)
    eidx_ref[...] = eidx
    rank_ref[...] = rank
    pad = jnp.zeros((GATE_LANES - TOP_K, tm), F32)
    wts_ref[0] = jnp.concatenate([wts, pad], axis=0).T


def _outproj(oa, ob, x, wa, wb, g1, n2, sc2, sh2, wr_t, rb, tm):
    B, T, D = x.shape
    nt = T // tm
    const = lambda shape: pl.BlockSpec(shape, lambda b, t: (0,) * len(shape))
    tile = lambda w: pl.BlockSpec((1, tm, w), lambda b, t: (b, t, 0))
    per_batch = pl.BlockSpec((1, 1, D), lambda b, t: (b, 0, 0))
    picks = pl.BlockSpec((TOP_K, tm), lambda b, t: (0, b * nt + t))
    return pl.pallas_call(
        _outproj_kernel,
        grid=(B, nt),
        in_specs=[tile(GW), tile(GW), tile(D), const(wa.shape), const(wb.shape), per_batch,
                  const((1, D)), per_batch, per_batch, const(wr_t.shape), const(rb.shape)],
        out_specs=[tile(D), pl.BlockSpec((tm, ROW_TILE, LANES), lambda b, t: (b * nt + t, 0, 0)),
                   picks, picks, tile(GATE_LANES), const((N_EXPERTS, GATE_LANES))],
        out_shape=[jax.ShapeDtypeStruct((B, T, D), F32), jax.ShapeDtypeStruct((B * T, ROW_TILE, LANES), ROW_DTYPE),
                   jax.ShapeDtypeStruct((TOP_K, B * T), jnp.int32), jax.ShapeDtypeStruct((TOP_K, B * T), jnp.int32),
                   jax.ShapeDtypeStruct((B, T, GATE_LANES), F32),
                   jax.ShapeDtypeStruct((N_EXPERTS, GATE_LANES), F32)],
        compiler_params=pltpu.CompilerParams(dimension_semantics=("arbitrary", "arbitrary"),
                                             vmem_limit_bytes=VMEM_LIMIT_BYTES),
        name="outproj",
    )(oa, ob, x, wa, wb, g1, n2, sc2, sh2, wr_t, rb)


def _slots_kernel(off_ref, eidx_ref, rank_ref, slot_ref):
    eidx = eidx_ref[...]

    def add_expert(e, acc):
        return acc + jnp.where(eidx == e, off_ref[e], 0)

    slot_ref[...] = lax.fori_loop(0, N_EXPERTS, add_expert, rank_ref[...])


def _slots(offsets, eidx, rank):
    k, m = eidx.shape
    tile = pl.BlockSpec((k, m), lambda i: (0, 0))
    return pl.pallas_call(
        _slots_kernel,
        grid=(1,),
        in_specs=[pl.BlockSpec(memory_space=pltpu.SMEM), tile, tile],
        out_specs=tile,
        out_shape=jax.ShapeDtypeStruct((k, m), jnp.int32),
        compiler_params=pltpu.CompilerParams(dimension_semantics=("arbitrary",),
                                             vmem_limit_bytes=VMEM_LIMIT_BYTES),
        name="slots",
    )(offsets, eidx, rank)


def _dispatch_kernel(slot_ref, h_ref, xs_ref, sem):
    tmd = h_ref.shape[0]

    for j in range(tmd):
        for k in range(TOP_K):
            pltpu.make_async_copy(h_ref.at[pl.ds(j, 1)], xs_ref.at[pl.ds(slot_ref[k, j], 1)], sem).start(
                priority=k % 2)
    for _ in range(TOP_K):
        pltpu.make_async_copy(h_ref, xs_ref.at[pl.ds(0, tmd)], sem).wait()


def _dispatch(slot, h2, n_rows, tmd):
    M = h2.shape[0]
    return pl.pallas_call(
        _dispatch_kernel,
        grid=(M // tmd,),
        in_specs=[pl.BlockSpec((TOP_K, tmd), lambda i: (0, i), memory_space=pltpu.SMEM),
                  pl.BlockSpec((tmd, ROW_TILE, LANES), lambda i: (i, 0, 0))],
        out_specs=pl.BlockSpec(memory_space=pl.ANY),
        out_shape=jax.ShapeDtypeStruct((n_rows, ROW_TILE, LANES), h2.dtype),
        scratch_shapes=[pltpu.SemaphoreType.DMA],
        compiler_params=pltpu.CompilerParams(dimension_semantics=("arbitrary",),
                                             vmem_limit_bytes=VMEM_LIMIT_BYTES),
        name="dispatch",
    )(slot, h2)


FFN_PIPELINE_STEPS = 2


def _ffn_kernel(te_ref, nu_ref, nv_ref, x_ref, wg_ref, wu_ref, wd_ref, y_ref, wgu_s, wd_s, xstd, ystd):
    i = pl.program_id(0)
    row = lax.broadcasted_iota(jnp.int32, (MOE_ROWS, 1), 0)
    tile_mm = jnp.maximum(i - 1, 0)
    tile_out = jnp.maximum(i - 2, 0)

    @pl.when(i == 0)
    def _():
        xstd[...] = jnp.zeros(xstd.shape, BF16)
        ystd[...] = jnp.zeros(ystd.shape, F32)

    @pl.when(i < nu_ref[0] + FFN_PIPELINE_STEPS)
    def _():
        @pl.when((i == 0) | (te_ref[tile_mm] != te_ref[tile_out]))
        def _():
            wgu_s[:, :D_EXPERT] = wg_ref[0].astype(BF16)
            wgu_s[:, D_EXPERT:] = wu_ref[0].astype(BF16)
            wd_s[...] = wd_ref[0].astype(BF16)

        cur = i % 2
        xstd[cur] = jnp.where(row < nv_ref[i], _to_rows(x_ref[...].astype(F32)), 0.0).astype(BF16)
        y_ref[...] = _to_row_tiles(ystd[cur]).astype(y_ref.dtype)
        gu = jnp.dot(xstd[1 - cur], wgu_s[...], preferred_element_type=F32)
        act = _silu(gu[:, :D_EXPERT]) * gu[:, D_EXPERT:]
        ystd[1 - cur] = jnp.dot(act.astype(BF16), wd_s[...], preferred_element_type=F32)


def _ffn(tile_expert, n_used, n_valid, xs, wg, wu, wd):
    D = wg.shape[1]
    n_tiles = xs.shape[0] // MOE_ROWS
    block = (MOE_ROWS, ROW_TILE, LANES)
    expert = lambda i, te, nu, nv: (te[jnp.maximum(i - 1, 0)], 0, 0)
    return pl.pallas_call(
        _ffn_kernel,
        grid_spec=pltpu.PrefetchScalarGridSpec(
            num_scalar_prefetch=3,
            grid=(n_tiles + FFN_PIPELINE_STEPS,),
            in_specs=[pl.BlockSpec(block, lambda i, te, nu, nv: (jnp.minimum(i, nu[0] - 1), 0, 0)),
                      pl.BlockSpec((1, D, D_EXPERT), expert),
                      pl.BlockSpec((1, D, D_EXPERT), expert),
                      pl.BlockSpec((1, D_EXPERT, D), expert)],
            out_specs=pl.BlockSpec(block, lambda i, te, nu, nv: (jnp.clip(i - 2, 0, nu[0] - 1), 0, 0)),
            scratch_shapes=[pltpu.VMEM((D, 2 * D_EXPERT), BF16), pltpu.VMEM((D_EXPERT, D), BF16),
                            pltpu.VMEM((2, MOE_ROWS, D), BF16), pltpu.VMEM((2, MOE_ROWS, D), F32)]),
        out_shape=jax.ShapeDtypeStruct(xs.shape, xs.dtype),
        compiler_params=pltpu.CompilerParams(dimension_semantics=("arbitrary",),
                                             vmem_limit_bytes=VMEM_LIMIT_BYTES),
        name="ffn",
    )(tile_expert, n_used, n_valid, xs, wg, wu, wd)


def _combine_kernel(slot_ref, next_slot_ref, w_ref, h_ref, x1_ref, g2_ref, fw_ref, wsgu_ref, wsd_ref, y_ref,
                    o_ref, buf0, buf1, sem):
    tmc = h_ref.shape[0]
    i = pl.program_id(0)
    last = pl.num_programs(0) - 1
    bufs = (buf0, buf1)
    assert TOP_K == ROW_TILE

    def start_rows(slots, b):
        for j in range(tmc):
            for k in range(TOP_K):
                pltpu.make_async_copy(y_ref.at[pl.ds(slots[k, j], 1)], bufs[b].at[k, pl.ds(j, 1)], sem.at[b]).start(
                    priority=k % 2)

    def wait_rows(b):
        for k in range(TOP_K):
            pltpu.make_async_copy(y_ref.at[pl.ds(0, tmc)], bufs[b].at[k], sem.at[b]).wait()

    @pl.when(i == 0)
    def _():
        start_rows(slot_ref, 0)

    def step(b):
        wait_rows(b)
        start_rows(next_slot_ref, 1 - b)
        gu = jnp.dot(_to_rows(h_ref[...].astype(F32)).astype(BF16), wsgu_ref[...], preferred_element_type=F32)
        act = _silu(gu[:, :D_EXPERT]) * gu[:, D_EXPERT:]
        shared = jnp.dot(act.astype(BF16), wsd_ref[...], preferred_element_type=F32)
        w = w_ref[...]
        lane = lax.broadcasted_iota(jnp.int32, w.shape, 1)
        w3 = _to_row_tiles(jnp.concatenate(
            [jnp.broadcast_to(_lane_pick(w, lane, k), (tmc, LANES)) for k in range(TOP_K)], axis=1))
        routed = jnp.zeros((tmc, ROW_TILE, LANES), F32)
        for k in range(TOP_K):
            routed = routed + jnp.broadcast_to(w3[:, k:k + 1, :], w3.shape) * bufs[b][k].astype(F32)
        acc = shared + _to_rows(routed)
        y = x1_ref[...] + g2_ref[0] * acc
        o_ref[...] = _rms(y) * fw_ref[...]

        @pl.when(i == last)
        def _():
            wait_rows(1 - b)

    for b in range(2):
        @pl.when(i % 2 == b)
        def _():
            step(b)


def _combine(slot, wts, h2, x1, g2, fw, wsgu, wsd, ys, T, tmc):
    M, D = x1.shape
    const = lambda shape: pl.BlockSpec(shape, lambda i: (0,) * len(shape))
    tile = lambda w: pl.BlockSpec((tmc, w), lambda i: (i, 0))
    row_tiled = pl.BlockSpec((tmc, ROW_TILE, LANES), lambda i: (i, 0, 0))
    n_steps = M // tmc
    return pl.pallas_call(
        _combine_kernel,
        grid=(n_steps,),
        in_specs=[pl.BlockSpec((TOP_K, tmc), lambda i: (0, i), memory_space=pltpu.SMEM),
                  pl.BlockSpec((TOP_K, tmc), lambda i: (0, jnp.minimum(i + 1, n_steps - 1)), memory_space=pltpu.SMEM),
                  tile(GATE_LANES), row_tiled, tile(D),
                  pl.BlockSpec((1, 1, D), lambda i: (i // (T // tmc), 0, 0)),
                  const((1, D)), const(wsgu.shape), const(wsd.shape),
                  pl.BlockSpec(memory_space=pl.ANY)],
        out_specs=tile(D),
        out_shape=jax.ShapeDtypeStruct((M, D), F32),
        scratch_shapes=[pltpu.VMEM((TOP_K, tmc, ROW_TILE, LANES), ys.dtype),
                        pltpu.VMEM((TOP_K, tmc, ROW_TILE, LANES), ys.dtype), pltpu.SemaphoreType.DMA((2,))],
        compiler_params=pltpu.CompilerParams(dimension_semantics=("arbitrary",),
                                             vmem_limit_bytes=VMEM_LIMIT_BYTES),
        name="combine",
    )(slot, slot, wts, h2, x1, g2, fw, wsgu, wsd, ys)


def _pick_tile(n, want):
    t = min(n, want)
    assert n % t == 0 and t % CHUNK == 0, (n, want)
    return t


def kernel(x, c, w_ada, b_ada, norm1_w, w_in, conv_w, gdn_a_log, gdn_dt_bias, gdn_norm_w, hg_lb, hg_norm_w,
           w_out, norm2_w, w_router, router_bias, w_gate, w_up, w_down, ws_gate, ws_up, ws_down, final_norm_w):
    B, T, D = x.shape
    M = B * T
    depth = w_ada.shape[0]
    assert depth == 1 and T % CHUNK == 0 and B <= 8
    layer = 0
    tt = _pick_tile(T, TILE_RECURRENCE)

    c_pad = jnp.pad(c, ((0, 8 - B), (0, 0)))
    mod = _ada(c_pad, w_ada[layer], b_ada[layer].reshape(1, -1))[:B]
    sh1, sc1, g1, sh2, sc2, g2 = (m.reshape(B, 1, D) for m in jnp.split(mod, 6, axis=-1))

    w = w_in[layer]
    qkv_w = 3 * GW
    sizes = (GW, HEADS, HEADS, GW, GW, GW, GW)
    offs = [qkv_w]
    for s in sizes:
        offs.append(offs[-1] + s)
    seg = lambda i: w[:, offs[i]:offs[i + 1]]
    small = jnp.pad(jnp.concatenate([seg(1), seg(2)], axis=1), ((0, 0), (0, GATE_LANES - 2 * HEADS)))
    w_all = jnp.concatenate([w[:, :qkv_w], seg(0), seg(3), seg(4), seg(5), seg(6), small], axis=1).astype(BF16)
    lane_pad = lambda v: jnp.pad(v.astype(F32).reshape(1, HEADS), ((0, 0), (HEADS, GATE_LANES - 2 * HEADS)))
    idx = jnp.arange(CUMSUM_ROWS)
    tri = ((idx[:, None] >= idx[None, :]) & (idx[:, None] // CHUNK == idx[None, :] // CHUNK)).astype(BF16)

    qa, ka, va, ga, sm, bcum, kb, ib, qb, gb = _inproj(
        layer, x, norm1_w[layer].reshape(1, D), sc1, sh1, w_all, conv_w[layer].astype(F32),
        lane_pad(gdn_a_log[layer]), lane_pad(gdn_dt_bias[layer]), hg_lb.astype(F32), tri,
        _pick_tile(T, TILE_INPROJ))

    gct = sm[:, :, HEADS:2 * HEADS].transpose(0, 2, 1).reshape(B, HEADS, T // CHUNK, CHUNK)
    oa = _gdn(qa, ka, va, ga, sm, gct, gdn_norm_w[layer].reshape(1, DH), tt)
    ob = _hgrn(qb, kb, ib, bcum, gb, hg_norm_w[layer].reshape(1, GW), tt)

    wo = w_out[layer].astype(BF16)
    x1, h2, eidx, rank, wts, cnt = _outproj(oa, ob, x, wo[:GW], wo[GW:], g1, norm2_w[layer].reshape(1, D), sc2, sh2,
                                            w_router[layer].T, router_bias[layer].reshape(N_EXPERTS, 1),
                                            _pick_tile(T, TILE_OUTPROJ))

    counts = cnt[:, 0].astype(jnp.int32)
    padded = (counts + MOE_ROWS - 1) // MOE_ROWS * MOE_ROWS
    ends = jnp.cumsum(padded)
    offsets = ends - padded
    n_tiles = (M * TOP_K) // MOE_ROWS + N_EXPERTS
    n_used = (ends[-1] // MOE_ROWS).astype(jnp.int32)
    tile_ids = jnp.minimum(jnp.arange(n_tiles + FFN_PIPELINE_STEPS, dtype=jnp.int32), n_used - 1)
    tile_expert = jnp.sum(((ends // MOE_ROWS)[None, :] <= tile_ids[:, None]).astype(jnp.int32), axis=1)
    tile_expert = jnp.minimum(tile_expert, N_EXPERTS - 1)
    slot = _slots(offsets.astype(jnp.int32), eidx, rank)
    first_tile = jnp.take(offsets, tile_expert) // MOE_ROWS
    n_valid = jnp.clip(jnp.take(counts, tile_expert) - (tile_ids - first_tile) * MOE_ROWS, 0, MOE_ROWS)

    assert D == ROW_TILE * LANES
    xs = _dispatch(slot, h2, n_tiles * MOE_ROWS, _pick_tile(M, TILE_DISPATCH))
    ys = _ffn(tile_expert.astype(jnp.int32), n_used.reshape(1), n_valid.astype(jnp.int32), xs,
              w_gate[layer], w_up[layer], w_down[layer])
    wsgu = jnp.concatenate([ws_gate[layer], ws_up[layer]], axis=-1).astype(BF16)
    out = _combine(slot, wts.reshape(M, GATE_LANES), h2, x1.reshape(M, D), g2, final_norm_w.reshape(1, D),
                   wsgu, ws_down[layer].astype(BF16), ys, T, _pick_tile(T, TILE_COMBINE))
    return out.reshape(B, T, D)
```

```python
import functools

import jax
import jax.numpy as jnp
from jax import lax
from jax.experimental import pallas as pl
from jax.experimental.pallas import tpu as pltpu

F32 = jnp.float32
BF16 = jnp.bfloat16

EPS = 1e-6
CHUNK = 64
SUB = 8
HEADS = 4
DH = 128
GW = HEADS * DH
CONV_K = 4
N_EXPERTS = 64
N_GROUPS = 8
GROUP_SIZE = N_EXPERTS // N_GROUPS
TOPK_GROUPS = 4
TOP_K = 8
D_EXPERT = 256
ROUTE_SCALE = 2.5
GATE_LANES = 128
GDN_CHUNKS_PER_ITER = 8
INV_BLOCK = 16
CUMSUM_ROWS = 128
MOE_ROWS = 1024
ROW_TILE, LANES = 8, 128

TILE_INPROJ = 1024
TILE_RECURRENCE = 512
TILE_OUTPROJ = 1024
TILE_DISPATCH = 1024
TILE_COMBINE = 256
ADA_COLS = 1024

VMEM_LIMIT_BYTES = 56 * 1024 * 1024

ACT = BF16
ROW_DTYPE = F32


def _silu(x):
    return x * jax.nn.sigmoid(x)


def _dot(a, b):
    return jnp.dot(a.astype(BF16), b.astype(BF16), preferred_element_type=F32)


def _dot_nt(a, b):
    return lax.dot_general(a.astype(BF16), b.astype(BF16), (((1,), (1,)), ((), ())),
                           preferred_element_type=F32)


def _dot_tn(a, b):
    return lax.dot_general(a.astype(BF16), b.astype(BF16), (((0,), (0,)), ((), ())),
                           preferred_element_type=F32)


def _split2(x):
    hi = x.astype(BF16)
    lo = (x - hi.astype(F32)).astype(BF16)
    return hi, lo


def _dot3(a, b, dot=_dot):
    ah, al = _split2(a)
    bh, bl = _split2(b)
    return dot(ah, bh) + dot(ah, bl) + dot(al, bh)


def _cumsum_rows(tri, x):
    hi = x.astype(BF16)
    r = x - hi.astype(F32)
    mid = r.astype(BF16)
    lo = (r - mid.astype(F32)).astype(BF16)
    g = tri.shape[0]
    groups = []
    for r0 in range(0, x.shape[0], g):
        rows = slice(r0, r0 + g)
        groups.append(jnp.dot(tri, hi[rows], preferred_element_type=F32)
                      + jnp.dot(tri, mid[rows], preferred_element_type=F32)
                      + jnp.dot(tri, lo[rows], preferred_element_type=F32))
    return jnp.concatenate(groups, axis=0)


def _lane_pick(tile, lane, idx):
    return jnp.sum(jnp.where(lane == idx, tile, 0.0), axis=1, keepdims=True)


def _rms(x):
    return x * lax.rsqrt(jnp.mean(x * x, axis=-1, keepdims=True) + EPS)


def _to_rows(x3):
    r = x3.shape[0]
    xt = jnp.swapaxes(x3.reshape(r // ROW_TILE, ROW_TILE, ROW_TILE, LANES), 1, 2)
    return jnp.concatenate([xt[:, s].reshape(r, LANES) for s in range(ROW_TILE)], axis=1)


def _to_row_tiles(x):
    r = x.shape[0]
    xt = jnp.stack([x[:, s * LANES:(s + 1) * LANES].reshape(r // ROW_TILE, ROW_TILE, LANES) for s in range(ROW_TILE)],
                   axis=1)
    return jnp.swapaxes(xt, 1, 2).reshape(r, ROW_TILE, LANES)


def _ada_kernel(c_ref, w_ref, b_ref, o_ref):
    ca = _silu(c_ref[...])
    o_ref[...] = _dot3(ca, w_ref[...]) + b_ref[...]


def _ada(c_pad, w, b):
    rows, d = c_pad.shape
    n = w.shape[1]
    tn = ADA_COLS
    return pl.pallas_call(
        _ada_kernel,
        grid=(n // tn,),
        in_specs=[pl.BlockSpec((rows, d), lambda j: (0, 0)),
                  pl.BlockSpec((d, tn), lambda j: (0, j)),
                  pl.BlockSpec((1, tn), lambda j: (0, j))],
        out_specs=pl.BlockSpec((rows, tn), lambda j: (0, j)),
        out_shape=jax.ShapeDtypeStruct((rows, n), F32),
        compiler_params=pltpu.CompilerParams(dimension_semantics=("arbitrary",),
                                             vmem_limit_bytes=VMEM_LIMIT_BYTES),
        name="ada",
    )(c_pad, w, b)


def _inproj_kernel(layer, x_ref, n1_ref, sc_ref, sh_ref, w_ref, cw_ref, alog_ref, dt_ref, lb_ref, tri_ref,
                   qa_ref, ka_ref, va_ref, ga_ref, sm_ref, b_ref, kb_ref, ib_ref, qb_ref, gb_ref,
                   pbuf):
    tt = x_ref.shape[1]
    t = pl.program_id(1)

    h = _rms(x_ref[0]) * n1_ref[...]
    h = h * (1.0 + sc_ref[0]) + sh_ref[0]
    hb = h.astype(BF16)

    def proj(g, width=GW):
        return jnp.dot(hb, w_ref[:, g * GW:g * GW + width], preferred_element_type=F32)

    @pl.when(t == 0)
    def _():
        pbuf[:, 0:8, :] = jnp.zeros((3, 8, GW), F32)

    for g, out_ref in enumerate((qa_ref, ka_ref, va_ref)):
        cols = slice(g * GW, (g + 1) * GW)
        p = proj(g)
        pbuf[g, 8:8 + tt, :] = p
        y = p * cw_ref[CONV_K - 1:CONV_K, cols]
        for j in range(1, CONV_K):
            y = y + pbuf[g, 8 - j:8 - j + tt, :] * cw_ref[CONV_K - 1 - j:CONV_K - j, cols]
        pbuf[g, 0:8, :] = pbuf[g, tt:tt + 8, :]
        y = _silu(y)
        if g == 2:
            out_ref[0] = y.astype(out_ref.dtype)
        else:
            scale = DH ** -0.5 if g == 0 else 1.0
            for hh in range(HEADS):
                hs = slice(hh * DH, (hh + 1) * DH)
                yh = y[:, hs]
                inv = lax.rsqrt(jnp.sum(yh * yh, axis=-1, keepdims=True) + EPS)
                out_ref[0, :, hs] = (yh * inv * scale).astype(out_ref.dtype)

    ga_ref[0] = _silu(proj(3)).astype(ga_ref.dtype)

    ps = proj(8, GATE_LANES)
    lane = lax.broadcasted_iota(jnp.int32, ps.shape, 1)
    beta = jax.nn.sigmoid(ps)
    z = ps + dt_ref[...]
    softplus = jnp.maximum(z, 0.0) + jnp.log1p(jnp.exp(-jnp.abs(z)))
    g_log = -jnp.exp(alog_ref[...]) * softplus
    tri = tri_ref[...]
    gc = _cumsum_rows(tri, jnp.where((lane >= HEADS) & (lane < 2 * HEADS), g_log, 0.0))
    sm_ref[0] = jnp.where(lane < HEADS, beta, gc)

    hl = lb_ref[...]
    e = jnp.exp(hl - jnp.max(hl, axis=0, keepdims=True))
    lb = jnp.sum(e[0:layer + 1], axis=0, keepdims=True) / jnp.sum(e, axis=0, keepdims=True)
    fr = proj(4)
    logf = jnp.log(lb + (1.0 - lb) * jax.nn.sigmoid(fr))
    b_ref[0] = _cumsum_rows(tri, logf)
    kb_ref[0] = ((1.0 - lb) * jax.nn.sigmoid(-fr)).astype(kb_ref.dtype)
    ib_ref[0] = proj(5).astype(ib_ref.dtype)
    qb_ref[0] = _silu(proj(6)).astype(qb_ref.dtype)
    gb_ref[0] = _silu(proj(7)).astype(gb_ref.dtype)


def _inproj(layer, x, n1, sc1, sh1, w_all, conv_w, alog_pad, dt_pad, hg_lb, tri, tt):
    B, T, D = x.shape
    const = lambda shape: pl.BlockSpec(shape, lambda b, t: (0,) * len(shape), pipeline_mode=pl.Buffered(1))
    act = lambda dt: jax.ShapeDtypeStruct((B, T, GW), dt)
    tile = lambda w: pl.BlockSpec((1, tt, w), lambda b, t: (b, t, 0))
    per_batch = pl.BlockSpec((1, 1, D), lambda b, t: (b, 0, 0))
    return pl.pallas_call(
        functools.partial(_inproj_kernel, layer),
        grid=(B, T // tt),
        in_specs=[tile(D), const((1, D)), per_batch, per_batch,
                  const(w_all.shape), const(conv_w.shape), const((1, GATE_LANES)), const((1, GATE_LANES)),
                  const(hg_lb.shape), const(tri.shape)],
        out_specs=[tile(GW), tile(GW), tile(GW), tile(GW), tile(GATE_LANES), tile(GW),
                   tile(GW), tile(GW), tile(GW), tile(GW)],
        out_shape=[act(ACT), act(ACT), act(ACT), act(ACT),
                   jax.ShapeDtypeStruct((B, T, GATE_LANES), F32), act(F32),
                   act(ACT), act(ACT), act(ACT), act(ACT)],
        scratch_shapes=[pltpu.VMEM((3, tt + 8, GW), F32)],
        compiler_params=pltpu.CompilerParams(dimension_semantics=("arbitrary", "arbitrary"),
                                             vmem_limit_bytes=VMEM_LIMIT_BYTES),
        name="inproj",
    )(x, n1, sc1, sh1, w_all, conv_w, alog_pad, dt_pad, hg_lb, tri)


def _gdn_prep_kernel(q_ref, k_ref, v_ref, sm_ref, gct_ref, o_ref, qt_ref, m_ref, n_ref):
    tt = q_ref.shape[1]
    nc = tt // CHUNK
    row = lax.broadcasted_iota(jnp.int32, (CHUNK, CHUNK), 0)
    col = lax.broadcasted_iota(jnp.int32, (CHUNK, CHUNK), 1)
    causal = row >= col
    diag_blk = (row > col) & (row // INV_BLOCK == col // INV_BLOCK)
    off_blk = row // INV_BLOCK > col // INV_BLOCK
    eye = jnp.where(row == col, 1.0, 0.0)
    lane = lax.broadcasted_iota(jnp.int32, (CHUNK, GATE_LANES), 1)
    assert INV_BLOCK == 16 and CHUNK == 4 * INV_BLOCK

    def body(i, carry):
        chains = [(GDN_CHUNKS_PER_ITER * i + j, hh) for j in range(GDN_CHUNKS_PER_ITER) for hh in range(HEADS)]
        rows = [pl.ds(pl.multiple_of(c * CHUNK, CHUNK), CHUNK) for c, _ in chains]
        hs = [slice(hh * DH, (hh + 1) * DH) for _, hh in chains]
        n = range(len(chains))
        sm = [sm_ref[0, rows[j], :] for j in n]
        q = [q_ref[0, rows[j], hs[j]].astype(F32) for j in n]
        k = [k_ref[0, rows[j], hs[j]].astype(F32) for j in n]
        v = [v_ref[0, rows[j], hs[j]].astype(F32) for j in n]
        beta = [_lane_pick(sm[j], lane, chains[j][1]) for j in n]
        gcol = [_lane_pick(sm[j], lane, HEADS + chains[j][1]) for j in n]
        grow = [gct_ref[0, hh, pl.ds(c, 1), :] for c, hh in chains]
        decay = [jnp.exp(jnp.where(causal, gcol[j] - grow[j], -jnp.inf)) for j in n]
        kb = [k[j] * beta[j] for j in n]
        L = [_dot_nt(kb[j], k[j]) * decay[j] for j in n]
        dg = [jnp.where(diag_blk, L[j], 0.0) for j in n]
        off = [jnp.where(off_blk, L[j], 0.0) for j in n]
        dinv = [eye - dg[j] for j in n]
        pw = [_dot3(dg[j], dg[j]) for j in n]
        for _ in range(2):
            dinv = [dinv[j] + _dot3(dinv[j], pw[j]) for j in n]
            pw = [_dot3(pw[j], pw[j]) for j in n]
        dinv = [dinv[j] + _dot3(dinv[j], pw[j]) for j in n]
        f1 = [_dot(dinv[j], off[j]) for j in n]
        f2 = [_dot(f1[j], f1[j]) for j in n]
        f3 = [_dot(f1[j], f2[j]) for j in n]
        tinv = [_dot(eye - f1[j] + f2[j] - f3[j], dinv[j]) for j in n]
        eg = [jnp.exp(gcol[j]) for j in n]
        sol = [_dot(tinv[j], jnp.concatenate([v[j] * beta[j], kb[j] * eg[j]], axis=1)) for j in n]
        attn = [_dot_nt(q[j], k[j]) * decay[j] for j in n]
        k_tail = [k[j] * jnp.exp(gcol[j][CHUNK - 1:CHUNK, :] - gcol[j]) for j in n]
        au = [_dot(attn[j], sol[j]) for j in n]
        ku = [_dot_tn(k_tail[j], sol[j]) for j in n]
        for j, (c, hh) in enumerate(chains):
            o_ref[0, rows[j], hs[j]] = au[j][:, :DH]
            qt_ref[0, rows[j], hs[j]] = (q[j] * eg[j] - au[j][:, DH:]).astype(qt_ref.dtype)
            n_ref[0, hh, c] = ku[j][:, :DH].astype(n_ref.dtype)
            m_ref[0, hh, c] = (-ku[j][:, DH:]).astype(m_ref.dtype)
        return carry

    lax.fori_loop(0, nc // GDN_CHUNKS_PER_ITER, body, 0)


def _gdn_scan_kernel(o_ref, qt_ref, m_ref, n_ref, gct_ref, sg_ref, nw_ref, out_ref, s_ref):
    nb, tt = o_ref.shape[0], o_ref.shape[1]
    nc = tt // CHUNK

    @pl.when(pl.program_id(0) == 0)
    def _():
        s_ref[...] = jnp.zeros(s_ref.shape, F32)

    nw = nw_ref[...]

    def body(c, carry):
        rows = pl.ds(pl.multiple_of(c * CHUNK, CHUNK), CHUNK)
        for b in range(nb):
            for hh in range(HEADS):
                hs = slice(hh * DH, (hh + 1) * DH)
                S = s_ref[b, hh]
                Sb = S.astype(BF16)
                glast = gct_ref[b, hh, pl.ds(c, 1), :][:, CHUNK - 1:CHUNK]
                o = o_ref[b, rows, hs] + jnp.dot(qt_ref[b, rows, hs], Sb, preferred_element_type=F32)
                s_ref[b, hh] = (S * jnp.exp(glast) + jnp.dot(m_ref[b, hh, c], Sb, preferred_element_type=F32)
                                + n_ref[b, hh, c].astype(F32))
                o = _rms(o) * nw * sg_ref[b, rows, hs].astype(F32)
                out_ref[b, rows, hs] = o.astype(out_ref.dtype)
        return carry

    lax.fori_loop(0, nc, body, 0)


def _gdn(q, k, v, sg, sm, gct, nw, tt):
    B, T, _ = q.shape
    nc = tt // CHUNK
    assert nc % GDN_CHUNKS_PER_ITER == 0
    n_chunks = T // CHUNK
    tile = lambda w: pl.BlockSpec((1, tt, w), lambda b, t: (b, t, 0))
    mat = jax.ShapeDtypeStruct((B, HEADS, n_chunks, DH, DH), ACT)
    o_part, qt, m, n = pl.pallas_call(
        _gdn_prep_kernel,
        grid=(B, T // tt),
        in_specs=[tile(GW), tile(GW), tile(GW), tile(GATE_LANES),
                  pl.BlockSpec((1, HEADS, nc, CHUNK), lambda b, t: (b, 0, t, 0))],
        out_specs=[tile(GW), tile(GW),
                   pl.BlockSpec((1, HEADS, nc, DH, DH), lambda b, t: (b, 0, t, 0, 0)),
                   pl.BlockSpec((1, HEADS, nc, DH, DH), lambda b, t: (b, 0, t, 0, 0))],
        out_shape=[jax.ShapeDtypeStruct((B, T, GW), F32), jax.ShapeDtypeStruct((B, T, GW), ACT), mat, mat],
        compiler_params=pltpu.CompilerParams(dimension_semantics=("arbitrary", "arbitrary"),
                                             vmem_limit_bytes=VMEM_LIMIT_BYTES),
        name="gdn_prep",
    )(q, k, v, sm, gct)

    full = lambda w: pl.BlockSpec((B, tt, w), lambda t: (0, t, 0))
    mats = pl.BlockSpec((B, HEADS, nc, DH, DH), lambda t: (0, 0, t, 0, 0))
    return pl.pallas_call(
        _gdn_scan_kernel,
        grid=(T // tt,),
        in_specs=[full(GW), full(GW), mats, mats,
                  pl.BlockSpec((B, HEADS, nc, CHUNK), lambda t: (0, 0, t, 0)),
                  full(GW), pl.BlockSpec((1, DH), lambda t: (0, 0))],
        out_specs=full(GW),
        out_shape=jax.ShapeDtypeStruct((B, T, GW), ACT),
        scratch_shapes=[pltpu.VMEM((B, HEADS, DH, DH), F32)],
        compiler_params=pltpu.CompilerParams(dimension_semantics=("arbitrary",),
                                             vmem_limit_bytes=VMEM_LIMIT_BYTES),
        name="gdn_scan",
    )(o_part, qt, m, n, gct, sg, nw)


def _hgrn_kernel(q_ref, k_ref, v_ref, b_ref, sg_ref, nw_ref, o_ref, st_ref):
    nb, tt = q_ref.shape[0], q_ref.shape[1]
    nc = tt // CHUNK

    @pl.when(pl.program_id(0) == 0)
    def _():
        st_ref[...] = jnp.zeros(st_ref.shape, F32)

    row = lax.broadcasted_iota(jnp.int32, (CHUNK, CHUNK), 0)
    col = lax.broadcasted_iota(jnp.int32, (CHUNK, CHUNK), 1)
    diag_block = ((col // SUB) == (row // SUB)) & (col <= row)
    chains = [(bi, hh) for bi in range(nb) for hh in range(HEADS)]
    n = range(len(chains))
    hs = [slice(hh * DH, (hh + 1) * DH) for _, hh in chains]

    def body(c, carry):
        rows = pl.ds(pl.multiple_of(c * CHUNK, CHUNK), CHUNK)
        q = [q_ref[bi, rows, hs[j]].astype(F32) for j, (bi, _) in enumerate(chains)]
        k = [k_ref[bi, rows, hs[j]].astype(F32) for j, (bi, _) in enumerate(chains)]
        v = [v_ref[bi, rows, hs[j]].astype(F32) for j, (bi, _) in enumerate(chains)]
        b = [b_ref[bi, rows, hs[j]] for j, (bi, _) in enumerate(chains)]
        blast = [b[j][CHUNK - 1:CHUNK, :] for j in n]
        st = [st_ref[bi, hh] for bi, hh in chains]
        o = [_dot_nt(q[j] * jnp.exp(b[j]), st[j]) for j in n]
        k_tail = [k[j] * jnp.exp(blast[j] - b[j]) for j in n]
        for j, (bi, hh) in enumerate(chains):
            st_ref[bi, hh] = st[j] * jnp.exp(blast[j]) + _dot_tn(v[j], k_tail[j])

        blocks = [[jnp.zeros((SUB, CHUNK), F32)] for _ in n]
        for i in range(1, CHUNK // SUB):
            lo, hi = i * SUB, (i + 1) * SUB
            for j in n:
                r = b[j][lo:lo + 1, :]
                qi = q[j][lo:hi] * jnp.exp(b[j][lo:hi] - r)
                kj = k[j][:lo] * jnp.exp(jnp.minimum(r - b[j][:lo], 0.0))
                kj = jnp.concatenate([kj, jnp.zeros((CHUNK - lo, DH), F32)], axis=0)
                blocks[j].append(_dot_nt(qi, kj))
        a = []
        for j in n:
            f = jnp.exp(jnp.minimum(b[j] - pltpu.roll(b[j], 1, 0), 0.0))
            e = None
            a_diag = jnp.zeros((CHUNK, CHUNK), F32)
            for delta in range(SUB):
                if delta == 0:
                    term = q[j] * k[j]
                else:
                    fsh = f if delta == 1 else pltpu.roll(f, delta - 1, 0)
                    e = fsh if e is None else e * fsh
                    term = q[j] * pltpu.roll(k[j], delta, 0) * e
                colv = jnp.sum(term, axis=1, keepdims=True)
                a_diag = jnp.where(row - col == delta, colv, a_diag)
            a.append(jnp.where(diag_block, a_diag, jnp.concatenate(blocks[j], axis=0)))

        o = [o[j] + _dot(a[j], v[j]) for j in n]
        for bi in range(nb):
            ob = jnp.concatenate(o[bi * HEADS:(bi + 1) * HEADS], axis=1)
            ob = _rms(ob) * nw_ref[...] * sg_ref[bi, rows, :].astype(F32)
            o_ref[bi, rows, :] = ob.astype(o_ref.dtype)
        return carry

    lax.fori_loop(0, nc, body, 0)


def _hgrn(q, k, v, b, sg, nw, tt):
    B, T, _ = q.shape
    tile = pl.BlockSpec((B, tt, GW), lambda t: (0, t, 0))
    return pl.pallas_call(
        _hgrn_kernel,
        grid=(T // tt,),
        in_specs=[tile, tile, tile, tile, tile, pl.BlockSpec((1, GW), lambda t: (0, 0))],
        out_specs=tile,
        out_shape=jax.ShapeDtypeStruct((B, T, GW), ACT),
        scratch_shapes=[pltpu.VMEM((B, HEADS, DH, DH), F32)],
        compiler_params=pltpu.CompilerParams(dimension_semantics=("arbitrary",),
                                             vmem_limit_bytes=VMEM_LIMIT_BYTES),
        name="hgrn",
    )(q, k, v, b, sg, nw)


def _outproj_kernel(oa_ref, ob_ref, x_ref, wa_ref, wb_ref, g1_ref, n2_ref, sc_ref, sh_ref, wr_ref, rb_ref,
                    x1_ref, h2_ref, eidx_ref, rank_ref, wts_ref, cnt_ref):
    tm = x_ref.shape[1]

    @pl.when((pl.program_id(0) == 0) & (pl.program_id(1) == 0))
    def _():
        cnt_ref[...] = jnp.zeros(cnt_ref.shape, F32)

    mix = (jnp.dot(oa_ref[0], wa_ref[...], preferred_element_type=F32)
           + jnp.dot(ob_ref[0], wb_ref[...], preferred_element_type=F32))
    x1 = x_ref[0] + g1_ref[0] * mix
    x1_ref[0] = x1
    h2 = _rms(x1) * n2_ref[...]
    h2 = h2 * (1.0 + sc_ref[0]) + sh_ref[0]
    h2_ref[...] = _to_row_tiles(h2).astype(h2_ref.dtype)

    scores = jax.nn.sigmoid(_dot3(wr_ref[...], h2, dot=_dot_nt))
    sel = scores + rb_ref[...]
    sub = lax.broadcasted_iota(jnp.int32, (GROUP_SIZE, tm), 0)
    neg = -jnp.inf
    groups = range(N_GROUPS)

    def take_max(blk):
        m = jnp.max(blk, axis=0, keepdims=True)
        first = jnp.min(jnp.where(blk == m, sub, GROUP_SIZE), axis=0, keepdims=True)
        hit = sub == first
        return m, hit, jnp.where(hit, neg, blk)

    blk_of = lambda a, g: a[g * GROUP_SIZE:(g + 1) * GROUP_SIZE]
    sel_blk = [blk_of(sel, g) for g in groups]
    group_score = jnp.zeros((N_GROUPS, tm), F32)
    for g in groups:
        m1, _, rest = take_max(sel_blk[g])
        m2 = jnp.max(rest, axis=0, keepdims=True)
        group_score = jnp.where(sub == g, m1 + m2, group_score)
    group_on = jnp.zeros((N_GROUPS, tm), F32)
    for _ in range(TOPK_GROUPS):
        _, hit, group_score = take_max(group_score)
        group_on = jnp.where(hit, 1.0, group_on)

    cand = [jnp.where(group_on[g:g + 1] > 0.0, sel_blk[g], neg) for g in groups]
    picked = [jnp.zeros((GROUP_SIZE, tm), F32) for _ in groups]
    chosen = []
    for _ in range(TOP_K):
        m = jnp.max(functools.reduce(jnp.maximum, cand), axis=0, keepdims=True)
        first = functools.reduce(jnp.minimum, [jnp.where(cand[g] == m, sub + g * GROUP_SIZE, N_EXPERTS)
                                               for g in groups])
        first = jnp.min(first, axis=0, keepdims=True)
        chosen.append(first)
        for g in groups:
            hit = (sub + g * GROUP_SIZE) == first
            picked[g] = jnp.where(hit, 1.0, picked[g])
            cand[g] = jnp.where(hit, neg, cand[g])

    picked_all = jnp.concatenate(picked, axis=0)
    r_i = lax.broadcasted_iota(jnp.int32, (tm, tm), 0)
    c_i = lax.broadcasted_iota(jnp.int32, (tm, tm), 1)
    earlier = jnp.where(r_i < c_i, 1.0, 0.0).astype(BF16)
    before = jnp.dot(picked_all.astype(BF16), earlier, preferred_element_type=F32) + cnt_ref[:, 0:1]
    cnt_ref[...] = cnt_ref[...] + jnp.sum(picked_all, axis=1, keepdims=True)

    def pick_value(table, first):
        parts = [jnp.where((sub + g * GROUP_SIZE) == first, blk_of(table, g), 0.0) for g in groups]
        return jnp.sum(functools.reduce(jnp.add, parts), axis=0, keepdims=True)

    w_k = [pick_value(scores, f) for f in chosen]
    denom = functools.reduce(jnp.add, w_k)
    eidx = jnp.zeros((TOP_K, tm), jnp.int32)
    rank = jnp.zeros((TOP_K, tm), jnp.int32)
    wts = jnp.zeros((TOP_K, tm), F32)
    for k in range(TOP_K):
        eidx = jnp.where(sub == k, chosen[k], eidx)
        rank = jnp.where(sub == k, pick_value(before, chosen[k]).astype(jnp.int32), rank)
        wts = jnp.where(sub == k, w_k[k] / denom * ROUTE_SCALE, wts)
    eidx_ref[...] = eidx
    rank_ref[...] = rank
    pad = jnp.zeros((GATE_LANES - TOP_K, tm), F32)
    wts_ref[0] = jnp.concatenate([wts, pad], axis=0).T


def _outproj(oa, ob, x, wa, wb, g1, n2, sc2, sh2, wr_t, rb, tm):
    B, T, D = x.shape
    nt = T // tm
    const = lambda shape: pl.BlockSpec(shape, lambda b, t: (0,) * len(shape))
    tile = lambda w: pl.BlockSpec((1, tm, w), lambda b, t: (b, t, 0))
    per_batch = pl.BlockSpec((1, 1, D), lambda b, t: (b, 0, 0))
    picks = pl.BlockSpec((TOP_K, tm), lambda b, t: (0, b * nt + t))
    return pl.pallas_call(
        _outproj_kernel,
        grid=(B, nt),
        in_specs=[tile(GW), tile(GW), tile(D), const(wa.shape), const(wb.shape), per_batch,
                  const((1, D)), per_batch, per_batch, const(wr_t.shape), const(rb.shape)],
        out_specs=[tile(D), pl.BlockSpec((tm, ROW_TILE, LANES), lambda b, t: (b * nt + t, 0, 0)),
                   picks, picks, tile(GATE_LANES), const((N_EXPERTS, GATE_LANES))],
        out_shape=[jax.ShapeDtypeStruct((B, T, D), F32), jax.ShapeDtypeStruct((B * T, ROW_TILE, LANES), ROW_DTYPE),
                   jax.ShapeDtypeStruct((TOP_K, B * T), jnp.int32), jax.ShapeDtypeStruct((TOP_K, B * T), jnp.int32),
                   jax.ShapeDtypeStruct((B, T, GATE_LANES), F32),
                   jax.ShapeDtypeStruct((N_EXPERTS, GATE_LANES), F32)],
        compiler_params=pltpu.CompilerParams(dimension_semantics=("arbitrary", "arbitrary"),
                                             vmem_limit_bytes=VMEM_LIMIT_BYTES),
        name="outproj",
    )(oa, ob, x, wa, wb, g1, n2, sc2, sh2, wr_t, rb)


def _slots_kernel(off_ref, eidx_ref, rank_ref, slot_ref):
    eidx = eidx_ref[...]

    def add_expert(e, acc):
        return acc + jnp.where(eidx == e, off_ref[e], 0)

    slot_ref[...] = lax.fori_loop(0, N_EXPERTS, add_expert, rank_ref[...])


def _slots(offsets, eidx, rank):
    k, m = eidx.shape
    tile = pl.BlockSpec((k, m), lambda i: (0, 0))
    return pl.pallas_call(
        _slots_kernel,
        grid=(1,),
        in_specs=[pl.BlockSpec(memory_space=pltpu.SMEM), tile, tile],
        out_specs=tile,
        out_shape=jax.ShapeDtypeStruct((k, m), jnp.int32),
        compiler_params=pltpu.CompilerParams(dimension_semantics=("arbitrary",),
                                             vmem_limit_bytes=VMEM_LIMIT_BYTES),
        name="slots",
    )(offsets, eidx, rank)


def _dispatch_kernel(slot_ref, h_ref, xs_ref, sem):
    tmd = h_ref.shape[0]

    def start_rows(j, c):
        for k in range(TOP_K):
            pltpu.make_async_copy(h_ref.at[pl.ds(j, 1)], xs_ref.at[pl.ds(slot_ref[k, j], 1)], sem).start(
                priority=k % 2)
        return c

    lax.fori_loop(0, tmd, start_rows, 0, unroll=4)
    for _ in range(TOP_K):
        pltpu.make_async_copy(h_ref, xs_ref.at[pl.ds(0, tmd)], sem).wait()


def _dispatch(slot, h2, n_rows, tmd):
    M = h2.shape[0]
    return pl.pallas_call(
        _dispatch_kernel,
        grid=(M // tmd,),
        in_specs=[pl.BlockSpec((TOP_K, tmd), lambda i: (0, i), memory_space=pltpu.SMEM),
                  pl.BlockSpec((tmd, ROW_TILE, LANES), lambda i: (i, 0, 0))],
        out_specs=pl.BlockSpec(memory_space=pl.ANY),
        out_shape=jax.ShapeDtypeStruct((n_rows, ROW_TILE, LANES), h2.dtype),
        scratch_shapes=[pltpu.SemaphoreType.DMA],
        compiler_params=pltpu.CompilerParams(dimension_semantics=("arbitrary",),
                                             vmem_limit_bytes=VMEM_LIMIT_BYTES),
        name="dispatch",
    )(slot, h2)


FFN_PIPELINE_STEPS = 2


def _ffn_kernel(te_ref, nu_ref, nv_ref, x_ref, wg_ref, wu_ref, wd_ref, y_ref, wgu_s, wd_s, xstd, ystd):
    i = pl.program_id(0)
    row = lax.broadcasted_iota(jnp.int32, (MOE_ROWS, 1), 0)
    tile_mm = jnp.maximum(i - 1, 0)
    tile_out = jnp.maximum(i - 2, 0)

    @pl.when(i == 0)
    def _():
        xstd[...] = jnp.zeros(xstd.shape, BF16)
        ystd[...] = jnp.zeros(ystd.shape, F32)

    @pl.when(i < nu_ref[0] + FFN_PIPELINE_STEPS)
    def _():
        @pl.when((i == 0) | (te_ref[tile_mm] != te_ref[tile_out]))
        def _():
            wgu_s[:, :D_EXPERT] = wg_ref[0].astype(BF16)
            wgu_s[:, D_EXPERT:] = wu_ref[0].astype(BF16)
            wd_s[...] = wd_ref[0].astype(BF16)

        cur = i % 2
        xstd[cur] = jnp.where(row < nv_ref[i], _to_rows(x_ref[...].astype(F32)), 0.0).astype(BF16)
        y_ref[...] = _to_row_tiles(ystd[cur]).astype(y_ref.dtype)
        gu = jnp.dot(xstd[1 - cur], wgu_s[...], preferred_element_type=F32)
        act = _silu(gu[:, :D_EXPERT]) * gu[:, D_EXPERT:]
        ystd[1 - cur] = jnp.dot(act.astype(BF16), wd_s[...], preferred_element_type=F32)


def _ffn(tile_expert, n_used, n_valid, xs, wg, wu, wd):
    D = wg.shape[1]
    n_tiles = xs.shape[0] // MOE_ROWS
    block = (MOE_ROWS, ROW_TILE, LANES)
    expert = lambda i, te, nu, nv: (te[jnp.maximum(i - 1, 0)], 0, 0)
    return pl.pallas_call(
        _ffn_kernel,
        grid_spec=pltpu.PrefetchScalarGridSpec(
            num_scalar_prefetch=3,
            grid=(n_tiles + FFN_PIPELINE_STEPS,),
            in_specs=[pl.BlockSpec(block, lambda i, te, nu, nv: (jnp.minimum(i, nu[0] - 1), 0, 0)),
                      pl.BlockSpec((1, D, D_EXPERT), expert),
                      pl.BlockSpec((1, D, D_EXPERT), expert),
                      pl.BlockSpec((1, D_EXPERT, D), expert)],
            out_specs=pl.BlockSpec(block, lambda i, te, nu, nv: (jnp.clip(i - 2, 0, nu[0] - 1), 0, 0)),
            scratch_shapes=[pltpu.VMEM((D, 2 * D_EXPERT), BF16), pltpu.VMEM((D_EXPERT, D), BF16),
                            pltpu.VMEM((2, MOE_ROWS, D), BF16), pltpu.VMEM((2, MOE_ROWS, D), F32)]),
        out_shape=jax.ShapeDtypeStruct(xs.shape, xs.dtype),
        compiler_params=pltpu.CompilerParams(dimension_semantics=("arbitrary",),
                                             vmem_limit_bytes=VMEM_LIMIT_BYTES),
        name="ffn",
    )(tile_expert, n_used, n_valid, xs, wg, wu, wd)


def _combine_kernel(slot_ref, next_slot_ref, w_ref, h_ref, x1_ref, g2_ref, fw_ref, wsgu_ref, wsd_ref, y_ref,
                    o_ref, buf0, buf1, sem):
    tmc = h_ref.shape[0]
    i = pl.program_id(0)
    last = pl.num_programs(0) - 1
    bufs = (buf0, buf1)
    assert TOP_K == ROW_TILE

    def start_rows(slots, b):
        for j in range(tmc):
            for k in range(TOP_K):
                pltpu.make_async_copy(y_ref.at[pl.ds(slots[k, j], 1)], bufs[b].at[k, pl.ds(j, 1)], sem.at[b]).start(
                    priority=k % 2)

    def wait_rows(b):
        for k in range(TOP_K):
            pltpu.make_async_copy(y_ref.at[pl.ds(0, tmc)], bufs[b].at[k], sem.at[b]).wait()

    @pl.when(i == 0)
    def _():
        start_rows(slot_ref, 0)

    def step(b):
        wait_rows(b)
        start_rows(next_slot_ref, 1 - b)
        gu = jnp.dot(_to_rows(h_ref[...].astype(F32)).astype(BF16), wsgu_ref[...], preferred_element_type=F32)
        act = _silu(gu[:, :D_EXPERT]) * gu[:, D_EXPERT:]
        shared = jnp.dot(act.astype(BF16), wsd_ref[...], preferred_element_type=F32)
        w = w_ref[...]
        lane = lax.broadcasted_iota(jnp.int32, w.shape, 1)
        w3 = _to_row_tiles(jnp.concatenate(
            [jnp.broadcast_to(_lane_pick(w, lane, k), (tmc, LANES)) for k in range(TOP_K)], axis=1))
        routed = jnp.zeros((tmc, ROW_TILE, LANES), F32)
        for k in range(TOP_K):
            routed = routed + jnp.broadcast_to(w3[:, k:k + 1, :], w3.shape) * bufs[b][k].astype(F32)
        acc = shared + _to_rows(routed)
        y = x1_ref[...] + g2_ref[0] * acc
        o_ref[...] = _rms(y) * fw_ref[...]

        @pl.when(i == last)
        def _():
            wait_rows(1 - b)

    for b in range(2):
        @pl.when(i % 2 == b)
        def _():
            step(b)


def _combine(slot, wts, h2, x1, g2, fw, wsgu, wsd, ys, T, tmc):
    M, D = x1.shape
    const = lambda shape: pl.BlockSpec(shape, lambda i: (0,) * len(shape))
    tile = lambda w: pl.BlockSpec((tmc, w), lambda i: (i, 0))
    row_tiled = pl.BlockSpec((tmc, ROW_TILE, LANES), lambda i: (i, 0, 0))
    n_steps = M // tmc
    return pl.pallas_call(
        _combine_kernel,
        grid=(n_steps,),
        in_specs=[pl.BlockSpec((TOP_K, tmc), lambda i: (0, i), memory_space=pltpu.SMEM),
                  pl.BlockSpec((TOP_K, tmc), lambda i: (0, jnp.minimum(i + 1, n_steps - 1)), memory_space=pltpu.SMEM),
                  tile(GATE_LANES), row_tiled, tile(D),
                  pl.BlockSpec((1, 1, D), lambda i: (i // (T // tmc), 0, 0)),
                  const((1, D)), const(wsgu.shape), const(wsd.shape),
                  pl.BlockSpec(memory_space=pl.ANY)],
        out_specs=tile(D),
        out_shape=jax.ShapeDtypeStruct((M, D), F32),
        scratch_shapes=[pltpu.VMEM((TOP_K, tmc, ROW_TILE, LANES), ys.dtype),
                        pltpu.VMEM((TOP_K, tmc, ROW_TILE, LANES), ys.dtype), pltpu.SemaphoreType.DMA((2,))],
        compiler_params=pltpu.CompilerParams(dimension_semantics=("arbitrary",),
                                             vmem_limit_bytes=VMEM_LIMIT_BYTES),
        name="combine",
    )(slot, slot, wts, h2, x1, g2, fw, wsgu, wsd, ys)


def _pick_tile(n, want):
    t = min(n, want)
    assert n % t == 0 and t % CHUNK == 0, (n, want)
    return t


def kernel(x, c, w_ada, b_ada, norm1_w, w_in, conv_w, gdn_a_log, gdn_dt_bias, gdn_norm_w, hg_lb, hg_norm_w,
           w_out, norm2_w, w_router, router_bias, w_gate, w_up, w_down, ws_gate, ws_up, ws_down, final_norm_w):
    B, T, D = x.shape
    M = B * T
    depth = w_ada.shape[0]
    assert depth == 1 and T % CHUNK == 0 and B <= 8
    layer = 0
    tt = _pick_tile(T, TILE_RECURRENCE)

    c_pad = jnp.pad(c, ((0, 8 - B), (0, 0)))
    mod = _ada(c_pad, w_ada[layer], b_ada[layer].reshape(1, -1))[:B]
    sh1, sc1, g1, sh2, sc2, g2 = (m.reshape(B, 1, D) for m in jnp.split(mod, 6, axis=-1))

    w = w_in[layer]
    qkv_w = 3 * GW
    sizes = (GW, HEADS, HEADS, GW, GW, GW, GW)
    offs = [qkv_w]
    for s in sizes:
        offs.append(offs[-1] + s)
    seg = lambda i: w[:, offs[i]:offs[i + 1]]
    small = jnp.pad(jnp.concatenate([seg(1), seg(2)], axis=1), ((0, 0), (0, GATE_LANES - 2 * HEADS)))
    w_all = jnp.concatenate([w[:, :qkv_w], seg(0), seg(3), seg(4), seg(5), seg(6), small], axis=1).astype(BF16)
    lane_pad = lambda v: jnp.pad(v.astype(F32).reshape(1, HEADS), ((0, 0), (HEADS, GATE_LANES - 2 * HEADS)))
    idx = jnp.arange(CUMSUM_ROWS)
    tri = ((idx[:, None] >= idx[None, :]) & (idx[:, None] // CHUNK == idx[None, :] // CHUNK)).astype(BF16)

    qa, ka, va, ga, sm, bcum, kb, ib, qb, gb = _inproj(
        layer, x, norm1_w[layer].reshape(1, D), sc1, sh1, w_all, conv_w[layer].astype(F32),
        lane_pad(gdn_a_log[layer]), lane_pad(gdn_dt_bias[layer]), hg_lb.astype(F32), tri,
        _pick_tile(T, TILE_INPROJ))

    gct = sm[:, :, HEADS:2 * HEADS].transpose(0, 2, 1).reshape(B, HEADS, T // CHUNK, CHUNK)
    oa = _gdn(qa, ka, va, ga, sm, gct, gdn_norm_w[layer].reshape(1, DH), tt)
    ob = _hgrn(qb, kb, ib, bcum, gb, hg_norm_w[layer].reshape(1, GW), tt)

    wo = w_out[layer].astype(BF16)
    x1, h2, eidx, rank, wts, cnt = _outproj(oa, ob, x, wo[:GW], wo[GW:], g1, norm2_w[layer].reshape(1, D), sc2, sh2,
                                            w_router[layer].T, router_bias[layer].reshape(N_EXPERTS, 1),
                                            _pick_tile(T, TILE_OUTPROJ))

    counts = cnt[:, 0].astype(jnp.int32)
    padded = (counts + MOE_ROWS - 1) // MOE_ROWS * MOE_ROWS
    ends = jnp.cumsum(padded)
    offsets = ends - padded
    n_tiles = (M * TOP_K) // MOE_ROWS + N_EXPERTS
    n_used = (ends[-1] // MOE_ROWS).astype(jnp.int32)
    tile_ids = jnp.minimum(jnp.arange(n_tiles + FFN_PIPELINE_STEPS, dtype=jnp.int32), n_used - 1)
    tile_expert = jnp.sum(((ends // MOE_ROWS)[None, :] <= tile_ids[:, None]).astype(jnp.int32), axis=1)
    tile_expert = jnp.minimum(tile_expert, N_EXPERTS - 1)
    slot = _slots(offsets.astype(jnp.int32), eidx, rank)
    first_tile = jnp.take(offsets, tile_expert) // MOE_ROWS
    n_valid = jnp.clip(jnp.take(counts, tile_expert) - (tile_ids - first_tile) * MOE_ROWS, 0, MOE_ROWS)

    assert D == ROW_TILE * LANES
    xs = _dispatch(slot, h2, n_tiles * MOE_ROWS, _pick_tile(M, TILE_DISPATCH))
    ys = _ffn(tile_expert.astype(jnp.int32), n_used.reshape(1), n_valid.astype(jnp.int32), xs,
              w_gate[layer], w_up[layer], w_down[layer])
    wsgu = jnp.concatenate([ws_gate[layer], ws_up[layer]], axis=-1).astype(BF16)
    out = _combine(slot, wts.reshape(M, GATE_LANES), h2, x1.reshape(M, D), g2, final_norm_w.reshape(1, D),
                   wsgu, ws_down[layer].astype(BF16), ys, T, _pick_tile(T, TILE_COMBINE))
    return out.reshape(B, T, D)
```

```python
import functools

import jax
import jax.numpy as jnp
from jax import lax
from jax.experimental import pallas as pl
from jax.experimental.pallas import tpu as pltpu

F32 = jnp.float32
BF16 = jnp.bfloat16

EPS = 1e-6
CHUNK = 64
SUB = 8
HEADS = 4
DH = 128
GW = HEADS * DH
CONV_K = 4
N_EXPERTS = 64
N_GROUPS = 8
GROUP_SIZE = N_EXPERTS // N_GROUPS
TOPK_GROUPS = 4
TOP_K = 8
D_EXPERT = 256
ROUTE_SCALE = 2.5
GATE_LANES = 128
GDN_CHUNKS_PER_ITER = 8
INV_BLOCK = 16
CUMSUM_ROWS = 128
MOE_ROWS = 1024
ROW_TILE, LANES = 8, 128

TILE_INPROJ = 1024
TILE_RECURRENCE = 512
TILE_OUTPROJ = 1024
TILE_DISPATCH = 1024
TILE_COMBINE = 256
ADA_COLS = 1024

VMEM_LIMIT_BYTES = 56 * 1024 * 1024

ACT = BF16
ROW_DTYPE = F32


def _silu(x):
    return x * jax.nn.sigmoid(x)


def _dot(a, b):
    return jnp.dot(a.astype(BF16), b.astype(BF16), preferred_element_type=F32)


def _dot_nt(a, b):
    return lax.dot_general(a.astype(BF16), b.astype(BF16), (((1,), (1,)), ((), ())),
                           preferred_element_type=F32)


def _dot_tn(a, b):
    return lax.dot_general(a.astype(BF16), b.astype(BF16), (((0,), (0,)), ((), ())),
                           preferred_element_type=F32)


def _split2(x):
    hi = x.astype(BF16)
    lo = (x - hi.astype(F32)).astype(BF16)
    return hi, lo


def _dot3(a, b, dot=_dot):
    ah, al = _split2(a)
    bh, bl = _split2(b)
    return dot(ah, bh) + dot(ah, bl) + dot(al, bh)


def _cumsum_rows(tri, x):
    hi = x.astype(BF16)
    r = x - hi.astype(F32)
    mid = r.astype(BF16)
    lo = (r - mid.astype(F32)).astype(BF16)
    g = tri.shape[0]
    groups = []
    for r0 in range(0, x.shape[0], g):
        rows = slice(r0, r0 + g)
        groups.append(jnp.dot(tri, hi[rows], preferred_element_type=F32)
                      + jnp.dot(tri, mid[rows], preferred_element_type=F32)
                      + jnp.dot(tri, lo[rows], preferred_element_type=F32))
    return jnp.concatenate(groups, axis=0)


def _lane_pick(tile, lane, idx):
    return jnp.sum(jnp.where(lane == idx, tile, 0.0), axis=1, keepdims=True)


def _rms(x):
    return x * lax.rsqrt(jnp.mean(x * x, axis=-1, keepdims=True) + EPS)


def _to_rows(x3):
    r = x3.shape[0]
    xt = jnp.swapaxes(x3.reshape(r // ROW_TILE, ROW_TILE, ROW_TILE, LANES), 1, 2)
    return jnp.concatenate([xt[:, s].reshape(r, LANES) for s in range(ROW_TILE)], axis=1)


def _to_row_tiles(x):
    r = x.shape[0]
    xt = jnp.stack([x[:, s * LANES:(s + 1) * LANES].reshape(r // ROW_TILE, ROW_TILE, LANES) for s in range(ROW_TILE)],
                   axis=1)
    return jnp.swapaxes(xt, 1, 2).reshape(r, ROW_TILE, LANES)


def _ada_kernel(c_ref, w_ref, b_ref, o_ref):
    ca = _silu(c_ref[...])
    o_ref[...] = _dot3(ca, w_ref[...]) + b_ref[...]


def _ada(c_pad, w, b):
    rows, d = c_pad.shape
    n = w.shape[1]
    tn = ADA_COLS
    return pl.pallas_call(
        _ada_kernel,
        grid=(n // tn,),
        in_specs=[pl.BlockSpec((rows, d), lambda j: (0, 0)),
                  pl.BlockSpec((d, tn), lambda j: (0, j)),
                  pl.BlockSpec((1, tn), lambda j: (0, j))],
        out_specs=pl.BlockSpec((rows, tn), lambda j: (0, j)),
        out_shape=jax.ShapeDtypeStruct((rows, n), F32),
        compiler_params=pltpu.CompilerParams(dimension_semantics=("arbitrary",),
                                             vmem_limit_bytes=VMEM_LIMIT_BYTES),
        name="ada",
    )(c_pad, w, b)


def _inproj_kernel(layer, x_ref, n1_ref, sc_ref, sh_ref, w_ref, cw_ref, alog_ref, dt_ref, lb_ref, tri_ref,
                   qa_ref, ka_ref, va_ref, ga_ref, sm_ref, b_ref, kb_ref, ib_ref, qb_ref, gb_ref,
                   pbuf):
    tt = x_ref.shape[1]
    t = pl.program_id(1)

    h = _rms(x_ref[0]) * n1_ref[...]
    h = h * (1.0 + sc_ref[0]) + sh_ref[0]
    hb = h.astype(BF16)

    def proj(g, width=GW):
        return jnp.dot(hb, w_ref[:, g * GW:g * GW + width], preferred_element_type=F32)

    @pl.when(t == 0)
    def _():
        pbuf[:, 0:8, :] = jnp.zeros((3, 8, GW), F32)

    for g, out_ref in enumerate((qa_ref, ka_ref, va_ref)):
        cols = slice(g * GW, (g + 1) * GW)
        p = proj(g)
        pbuf[g, 8:8 + tt, :] = p
        y = p * cw_ref[CONV_K - 1:CONV_K, cols]
        for j in range(1, CONV_K):
            y = y + pbuf[g, 8 - j:8 - j + tt, :] * cw_ref[CONV_K - 1 - j:CONV_K - j, cols]
        pbuf[g, 0:8, :] = pbuf[g, tt:tt + 8, :]
        y = _silu(y)
        if g == 2:
            out_ref[0] = y.astype(out_ref.dtype)
        else:
            scale = DH ** -0.5 if g == 0 else 1.0
            for hh in range(HEADS):
                hs = slice(hh * DH, (hh + 1) * DH)
                yh = y[:, hs]
                inv = lax.rsqrt(jnp.sum(yh * yh, axis=-1, keepdims=True) + EPS)
                out_ref[0, :, hs] = (yh * inv * scale).astype(out_ref.dtype)

    ga_ref[0] = _silu(proj(3)).astype(ga_ref.dtype)

    ps = proj(8, GATE_LANES)
    lane = lax.broadcasted_iota(jnp.int32, ps.shape, 1)
    beta = jax.nn.sigmoid(ps)
    z = ps + dt_ref[...]
    softplus = jnp.maximum(z, 0.0) + jnp.log1p(jnp.exp(-jnp.abs(z)))
    g_log = -jnp.exp(alog_ref[...]) * softplus
    tri = tri_ref[...]
    gc = _cumsum_rows(tri, jnp.where((lane >= HEADS) & (lane < 2 * HEADS), g_log, 0.0))
    sm_ref[0] = jnp.where(lane < HEADS, beta, gc)

    hl = lb_ref[...]
    e = jnp.exp(hl - jnp.max(hl, axis=0, keepdims=True))
    lb = jnp.sum(e[0:layer + 1], axis=0, keepdims=True) / jnp.sum(e, axis=0, keepdims=True)
    fr = proj(4)
    logf = jnp.log(lb + (1.0 - lb) * jax.nn.sigmoid(fr))
    b_ref[0] = _cumsum_rows(tri, logf)
    kb_ref[0] = ((1.0 - lb) * jax.nn.sigmoid(-fr)).astype(kb_ref.dtype)
    ib_ref[0] = proj(5).astype(ib_ref.dtype)
    qb_ref[0] = _silu(proj(6)).astype(qb_ref.dtype)
    gb_ref[0] = _silu(proj(7)).astype(gb_ref.dtype)


def _inproj(layer, x, n1, sc1, sh1, w_all, conv_w, alog_pad, dt_pad, hg_lb, tri, tt):
    B, T, D = x.shape
    const = lambda shape: pl.BlockSpec(shape, lambda b, t: (0,) * len(shape), pipeline_mode=pl.Buffered(1))
    act = lambda dt: jax.ShapeDtypeStruct((B, T, GW), dt)
    tile = lambda w: pl.BlockSpec((1, tt, w), lambda b, t: (b, t, 0))
    per_batch = pl.BlockSpec((1, 1, D), lambda b, t: (b, 0, 0))
    return pl.pallas_call(
        functools.partial(_inproj_kernel, layer),
        grid=(B, T // tt),
        in_specs=[tile(D), const((1, D)), per_batch, per_batch,
                  const(w_all.shape), const(conv_w.shape), const((1, GATE_LANES)), const((1, GATE_LANES)),
                  const(hg_lb.shape), const(tri.shape)],
        out_specs=[tile(GW), tile(GW), tile(GW), tile(GW), tile(GATE_LANES), tile(GW),
                   tile(GW), tile(GW), tile(GW), tile(GW)],
        out_shape=[act(ACT), act(ACT), act(ACT), act(ACT),
                   jax.ShapeDtypeStruct((B, T, GATE_LANES), F32), act(F32),
                   act(ACT), act(ACT), act(ACT), act(ACT)],
        scratch_shapes=[pltpu.VMEM((3, tt + 8, GW), F32)],
        compiler_params=pltpu.CompilerParams(dimension_semantics=("arbitrary", "arbitrary"),
                                             vmem_limit_bytes=VMEM_LIMIT_BYTES),
        name="inproj",
    )(x, n1, sc1, sh1, w_all, conv_w, alog_pad, dt_pad, hg_lb, tri)


def _gdn_prep_kernel(q_ref, k_ref, v_ref, sm_ref, gct_ref, o_ref, qt_ref, m_ref, n_ref):
    tt = q_ref.shape[1]
    nc = tt // CHUNK
    row = lax.broadcasted_iota(jnp.int32, (CHUNK, CHUNK), 0)
    col = lax.broadcasted_iota(jnp.int32, (CHUNK, CHUNK), 1)
    causal = row >= col
    diag_blk = (row > col) & (row // INV_BLOCK == col // INV_BLOCK)
    off_blk = row // INV_BLOCK > col // INV_BLOCK
    eye = jnp.where(row == col, 1.0, 0.0)
    lane = lax.broadcasted_iota(jnp.int32, (CHUNK, GATE_LANES), 1)
    assert INV_BLOCK == 16 and CHUNK == 4 * INV_BLOCK

    def body(i, carry):
        chains = [(GDN_CHUNKS_PER_ITER * i + j, hh) for j in range(GDN_CHUNKS_PER_ITER) for hh in range(HEADS)]
        rows = [pl.ds(pl.multiple_of(c * CHUNK, CHUNK), CHUNK) for c, _ in chains]
        hs = [slice(hh * DH, (hh + 1) * DH) for _, hh in chains]
        n = range(len(chains))
        sm = [sm_ref[0, rows[j], :] for j in n]
        q = [q_ref[0, rows[j], hs[j]].astype(F32) for j in n]
        k = [k_ref[0, rows[j], hs[j]].astype(F32) for j in n]
        v = [v_ref[0, rows[j], hs[j]].astype(F32) for j in n]
        beta = [_lane_pick(sm[j], lane, chains[j][1]) for j in n]
        gcol = [_lane_pick(sm[j], lane, HEADS + chains[j][1]) for j in n]
        grow = [gct_ref[0, hh, pl.ds(c, 1), :] for c, hh in chains]
        decay = [jnp.exp(jnp.where(causal, gcol[j] - grow[j], -jnp.inf)) for j in n]
        kb = [k[j] * beta[j] for j in n]
        L = [_dot_nt(kb[j], k[j]) * decay[j] for j in n]
        dg = [jnp.where(diag_blk, L[j], 0.0) for j in n]
        off = [jnp.where(off_blk, L[j], 0.0) for j in n]
        dinv = [eye - dg[j] for j in n]
        pw = [_dot3(dg[j], dg[j]) for j in n]
        for _ in range(2):
            dinv = [dinv[j] + _dot3(dinv[j], pw[j]) for j in n]
            pw = [_dot3(pw[j], pw[j]) for j in n]
        dinv = [dinv[j] + _dot3(dinv[j], pw[j]) for j in n]
        f1 = [_dot(dinv[j], off[j]) for j in n]
        f2 = [_dot(f1[j], f1[j]) for j in n]
        f3 = [_dot(f1[j], f2[j]) for j in n]
        tinv = [_dot(eye - f1[j] + f2[j] - f3[j], dinv[j]) for j in n]
        eg = [jnp.exp(gcol[j]) for j in n]
        sol = [_dot(tinv[j], jnp.concatenate([v[j] * beta[j], kb[j] * eg[j]], axis=1)) for j in n]
        attn = [_dot_nt(q[j], k[j]) * decay[j] for j in n]
        k_tail = [k[j] * jnp.exp(gcol[j][CHUNK - 1:CHUNK, :] - gcol[j]) for j in n]
        au = [_dot(attn[j], sol[j]) for j in n]
        ku = [_dot_tn(k_tail[j], sol[j]) for j in n]
        for j, (c, hh) in enumerate(chains):
            o_ref[0, rows[j], hs[j]] = au[j][:, :DH]
            qt_ref[0, rows[j], hs[j]] = (q[j] * eg[j] - au[j][:, DH:]).astype(qt_ref.dtype)
            n_ref[0, hh, c] = ku[j][:, :DH].astype(n_ref.dtype)
            m_ref[0, hh, c] = (-ku[j][:, DH:]).astype(m_ref.dtype)
        return carry

    lax.fori_loop(0, nc // GDN_CHUNKS_PER_ITER, body, 0)


def _gdn_scan_kernel(o_ref, qt_ref, m_ref, n_ref, gct_ref, sg_ref, nw_ref, out_ref, s_ref):
    nb, tt = o_ref.shape[0], o_ref.shape[1]
    nc = tt // CHUNK

    @pl.when(pl.program_id(0) == 0)
    def _():
        s_ref[...] = jnp.zeros(s_ref.shape, F32)

    nw = nw_ref[...]

    def body(c, carry):
        rows = pl.ds(pl.multiple_of(c * CHUNK, CHUNK), CHUNK)
        for b in range(nb):
            for hh in range(HEADS):
                hs = slice(hh * DH, (hh + 1) * DH)
                S = s_ref[b, hh]
                Sb = S.astype(BF16)
                glast = gct_ref[b, hh, pl.ds(c, 1), :][:, CHUNK - 1:CHUNK]
                o = o_ref[b, rows, hs] + jnp.dot(qt_ref[b, rows, hs], Sb, preferred_element_type=F32)
                s_ref[b, hh] = (S * jnp.exp(glast) + jnp.dot(m_ref[b, hh, c], Sb, preferred_element_type=F32)
                                + n_ref[b, hh, c].astype(F32))
                o = _rms(o) * nw * sg_ref[b, rows, hs].astype(F32)
                out_ref[b, rows, hs] = o.astype(out_ref.dtype)
        return carry

    lax.fori_loop(0, nc, body, 0)


def _gdn(q, k, v, sg, sm, gct, nw, tt):
    B, T, _ = q.shape
    nc = tt // CHUNK
    assert nc % GDN_CHUNKS_PER_ITER == 0
    n_chunks = T // CHUNK
    tile = lambda w: pl.BlockSpec((1, tt, w), lambda b, t: (b, t, 0))
    mat = jax.ShapeDtypeStruct((B, HEADS, n_chunks, DH, DH), ACT)
    o_part, qt, m, n = pl.pallas_call(
        _gdn_prep_kernel,
        grid=(B, T // tt),
        in_specs=[tile(GW), tile(GW), tile(GW), tile(GATE_LANES),
                  pl.BlockSpec((1, HEADS, nc, CHUNK), lambda b, t: (b, 0, t, 0))],
        out_specs=[tile(GW), tile(GW),
                   pl.BlockSpec((1, HEADS, nc, DH, DH), lambda b, t: (b, 0, t, 0, 0)),
                   pl.BlockSpec((1, HEADS, nc, DH, DH), lambda b, t: (b, 0, t, 0, 0))],
        out_shape=[jax.ShapeDtypeStruct((B, T, GW), F32), jax.ShapeDtypeStruct((B, T, GW), ACT), mat, mat],
        compiler_params=pltpu.CompilerParams(dimension_semantics=("arbitrary", "arbitrary"),
                                             vmem_limit_bytes=VMEM_LIMIT_BYTES),
        name="gdn_prep",
    )(q, k, v, sm, gct)

    full = lambda w: pl.BlockSpec((B, tt, w), lambda t: (0, t, 0))
    mats = pl.BlockSpec((B, HEADS, nc, DH, DH), lambda t: (0, 0, t, 0, 0))
    return pl.pallas_call(
        _gdn_scan_kernel,
        grid=(T // tt,),
        in_specs=[full(GW), full(GW), mats, mats,
                  pl.BlockSpec((B, HEADS, nc, CHUNK), lambda t: (0, 0, t, 0)),
                  full(GW), pl.BlockSpec((1, DH), lambda t: (0, 0))],
        out_specs=full(GW),
        out_shape=jax.ShapeDtypeStruct((B, T, GW), ACT),
        scratch_shapes=[pltpu.VMEM((B, HEADS, DH, DH), F32)],
        compiler_params=pltpu.CompilerParams(dimension_semantics=("arbitrary",),
                                             vmem_limit_bytes=VMEM_LIMIT_BYTES),
        name="gdn_scan",
    )(o_part, qt, m, n, gct, sg, nw)


def _hgrn_kernel(q_ref, k_ref, v_ref, b_ref, sg_ref, nw_ref, o_ref, st_ref):
    nb, tt = q_ref.shape[0], q_ref.shape[1]
    nc = tt // CHUNK

    @pl.when(pl.program_id(0) == 0)
    def _():
        st_ref[...] = jnp.zeros(st_ref.shape, F32)

    row = lax.broadcasted_iota(jnp.int32, (CHUNK, CHUNK), 0)
    col = lax.broadcasted_iota(jnp.int32, (CHUNK, CHUNK), 1)
    diag_block = ((col // SUB) == (row // SUB)) & (col <= row)
    chains = [(bi, hh) for bi in range(nb) for hh in range(HEADS)]
    n = range(len(chains))
    hs = [slice(hh * DH, (hh + 1) * DH) for _, hh in chains]

    def body(c, carry):
        rows = pl.ds(pl.multiple_of(c * CHUNK, CHUNK), CHUNK)
        q = [q_ref[bi, rows, hs[j]].astype(F32) for j, (bi, _) in enumerate(chains)]
        k = [k_ref[bi, rows, hs[j]].astype(F32) for j, (bi, _) in enumerate(chains)]
        v = [v_ref[bi, rows, hs[j]].astype(F32) for j, (bi, _) in enumerate(chains)]
        b = [b_ref[bi, rows, hs[j]] for j, (bi, _) in enumerate(chains)]
        blast = [b[j][CHUNK - 1:CHUNK, :] for j in n]
        st = [st_ref[bi, hh] for bi, hh in chains]
        o = [_dot_nt(q[j] * jnp.exp(b[j]), st[j]) for j in n]
        k_tail = [k[j] * jnp.exp(blast[j] - b[j]) for j in n]
        for j, (bi, hh) in enumerate(chains):
            st_ref[bi, hh] = st[j] * jnp.exp(blast[j]) + _dot_tn(v[j], k_tail[j])

        blocks = [[jnp.zeros((SUB, CHUNK), F32)] for _ in n]
        for i in range(1, CHUNK // SUB):
            lo, hi = i * SUB, (i + 1) * SUB
            for j in n:
                r = b[j][lo:lo + 1, :]
                qi = q[j][lo:hi] * jnp.exp(b[j][lo:hi] - r)
                kj = k[j][:lo] * jnp.exp(jnp.minimum(r - b[j][:lo], 0.0))
                kj = jnp.concatenate([kj, jnp.zeros((CHUNK - lo, DH), F32)], axis=0)
                blocks[j].append(_dot_nt(qi, kj))
        a = []
        for j in n:
            f = jnp.exp(jnp.minimum(b[j] - pltpu.roll(b[j], 1, 0), 0.0))
            e = None
            a_diag = jnp.zeros((CHUNK, CHUNK), F32)
            for delta in range(SUB):
                if delta == 0:
                    term = q[j] * k[j]
                else:
                    fsh = f if delta == 1 else pltpu.roll(f, delta - 1, 0)
                    e = fsh if e is None else e * fsh
                    term = q[j] * pltpu.roll(k[j], delta, 0) * e
                colv = jnp.sum(term, axis=1, keepdims=True)
                a_diag = jnp.where(row - col == delta, colv, a_diag)
            a.append(jnp.where(diag_block, a_diag, jnp.concatenate(blocks[j], axis=0)))

        o = [o[j] + _dot(a[j], v[j]) for j in n]
        for bi in range(nb):
            ob = jnp.concatenate(o[bi * HEADS:(bi + 1) * HEADS], axis=1)
            ob = _rms(ob) * nw_ref[...] * sg_ref[bi, rows, :].astype(F32)
            o_ref[bi, rows, :] = ob.astype(o_ref.dtype)
        return carry

    lax.fori_loop(0, nc, body, 0)


def _hgrn(q, k, v, b, sg, nw, tt):
    B, T, _ = q.shape
    tile = pl.BlockSpec((B, tt, GW), lambda t: (0, t, 0))
    return pl.pallas_call(
        _hgrn_kernel,
        grid=(T // tt,),
        in_specs=[tile, tile, tile, tile, tile, pl.BlockSpec((1, GW), lambda t: (0, 0))],
        out_specs=tile,
        out_shape=jax.ShapeDtypeStruct((B, T, GW), ACT),
        scratch_shapes=[pltpu.VMEM((B, HEADS, DH, DH), F32)],
        compiler_params=pltpu.CompilerParams(dimension_semantics=("arbitrary",),
                                             vmem_limit_bytes=VMEM_LIMIT_BYTES),
        name="hgrn",
    )(q, k, v, b, sg, nw)


def _outproj_kernel(oa_ref, ob_ref, x_ref, wa_ref, wb_ref, g1_ref, n2_ref, sc_ref, sh_ref, wr_ref, rb_ref,
                    x1_ref, h2_ref, eidx_ref, rank_ref, wts_ref, cnt_ref):
    tm = x_ref.shape[1]

    @pl.when((pl.program_id(0) == 0) & (pl.program_id(1) == 0))
    def _():
        cnt_ref[...] = jnp.zeros(cnt_ref.shape, F32)

    mix = (jnp.dot(oa_ref[0], wa_ref[...], preferred_element_type=F32)
           + jnp.dot(ob_ref[0], wb_ref[...], preferred_element_type=F32))
    x1 = x_ref[0] + g1_ref[0] * mix
    x1_ref[0] = x1
    h2 = _rms(x1) * n2_ref[...]
    h2 = h2 * (1.0 + sc_ref[0]) + sh_ref[0]
    h2_ref[...] = _to_row_tiles(h2).astype(h2_ref.dtype)

    scores = jax.nn.sigmoid(_dot3(wr_ref[...], h2, dot=_dot_nt))
    sel = scores + rb_ref[...]
    sub = lax.broadcasted_iota(jnp.int32, (GROUP_SIZE, tm), 0)
    neg = -jnp.inf
    groups = range(N_GROUPS)

    def take_max(blk):
        m = jnp.max(blk, axis=0, keepdims=True)
        first = jnp.min(jnp.where(blk == m, sub, GROUP_SIZE), axis=0, keepdims=True)
        hit = sub == first
        return m, hit, jnp.where(hit, neg, blk)

    blk_of = lambda a, g: a[g * GROUP_SIZE:(g + 1) * GROUP_SIZE]
    sel_blk = [blk_of(sel, g) for g in groups]
    group_score = jnp.zeros((N_GROUPS, tm), F32)
    for g in groups:
        m1, _, rest = take_max(sel_blk[g])
        m2 = jnp.max(rest, axis=0, keepdims=True)
        group_score = jnp.where(sub == g, m1 + m2, group_score)
    group_on = jnp.zeros((N_GROUPS, tm), F32)
    for _ in range(TOPK_GROUPS):
        _, hit, group_score = take_max(group_score)
        group_on = jnp.where(hit, 1.0, group_on)

    cand = [jnp.where(group_on[g:g + 1] > 0.0, sel_blk[g], neg) for g in groups]
    picked = [jnp.zeros((GROUP_SIZE, tm), F32) for _ in groups]
    chosen = []
    for _ in range(TOP_K):
        m = jnp.max(functools.reduce(jnp.maximum, cand), axis=0, keepdims=True)
        first = functools.reduce(jnp.minimum, [jnp.where(cand[g] == m, sub + g * GROUP_SIZE, N_EXPERTS)
                                               for g in groups])
        first = jnp.min(first, axis=0, keepdims=True)
        chosen.append(first)
        for g in groups:
            hit = (sub + g * GROUP_SIZE) == first
            picked[g] = jnp.where(hit, 1.0, picked[g])
            cand[g] = jnp.where(hit, neg, cand[g])

    picked_all = jnp.concatenate(picked, axis=0)
    r_i = lax.broadcasted_iota(jnp.int32, (tm, tm), 0)
    c_i = lax.broadcasted_iota(jnp.int32, (tm, tm), 1)
    earlier = jnp.where(r_i < c_i, 1.0, 0.0).astype(BF16)
    before = jnp.dot(picked_all.astype(BF16), earlier, preferred_element_type=F32) + cnt_ref[:, 0:1]
    cnt_ref[...] = cnt_ref[...] + jnp.sum(picked_all, axis=1, keepdims=True)

    def pick_value(table, first):
        parts = [jnp.where((sub + g * GROUP_SIZE) == first, blk_of(table, g), 0.0) for g in groups]
        return jnp.sum(functools.reduce(jnp.add, parts), axis=0, keepdims=True)

    w_k = [pick_value(scores, f) for f in chosen]
    denom = functools.reduce(jnp.add, w_k)
    eidx = jnp.zeros((TOP_K, tm), jnp.int32)
    rank = jnp.zeros((TOP_K, tm), jnp.int32)
    wts = jnp.zeros((TOP_K, tm), F32)
    for k in range(TOP_K):
        eidx = jnp.where(sub == k, chosen[k], eidx)
        rank = jnp.where(sub == k, pick_value(before, chosen[k]).astype(jnp.int32), rank)
        wts = jnp.where(sub == k, w_k[k] / denom * ROUTE_SCALE, wts)
    eidx_ref[...] = eidx
    rank_ref[...] = rank
    pad = jnp.zeros((GATE_LANES - TOP_K, tm), F32)
    wts_ref[0] = jnp.concatenate([wts, pad], axis=0).T


def _outproj(oa, ob, x, wa, wb, g1, n2, sc2, sh2, wr_t, rb, tm):
    B, T, D = x.shape
    nt = T // tm
    const = lambda shape: pl.BlockSpec(shape, lambda b, t: (0,) * len(shape))
    tile = lambda w: pl.BlockSpec((1, tm, w), lambda b, t: (b, t, 0))
    per_batch = pl.BlockSpec((1, 1, D), lambda b, t: (b, 0, 0))
    picks = pl.BlockSpec((TOP_K, tm), lambda b, t: (0, b * nt + t))
    return pl.pallas_call(
        _outproj_kernel,
        grid=(B, nt),
        in_specs=[tile(GW), tile(GW), tile(D), const(wa.shape), const(wb.shape), per_batch,
                  const((1, D)), per_batch, per_batch, const(wr_t.shape), const(rb.shape)],
        out_specs=[tile(D), pl.BlockSpec((tm, ROW_TILE, LANES), lambda b, t: (b * nt + t, 0, 0)),
                   picks, picks, tile(GATE_LANES), const((N_EXPERTS, GATE_LANES))],
        out_shape=[jax.ShapeDtypeStruct((B, T, D), F32), jax.ShapeDtypeStruct((B * T, ROW_TILE, LANES), ROW_DTYPE),
                   jax.ShapeDtypeStruct((TOP_K, B * T), jnp.int32), jax.ShapeDtypeStruct((TOP_K, B * T), jnp.int32),
                   jax.ShapeDtypeStruct((B, T, GATE_LANES), F32),
                   jax.ShapeDtypeStruct((N_EXPERTS, GATE_LANES), F32)],
        compiler_params=pltpu.CompilerParams(dimension_semantics=("arbitrary", "arbitrary"),
                                             vmem_limit_bytes=VMEM_LIMIT_BYTES),
        name="outproj",
    )(oa, ob, x, wa, wb, g1, n2, sc2, sh2, wr_t, rb)


def _slots_kernel(off_ref, eidx_ref, rank_ref, slot_ref):
    eidx = eidx_ref[...]

    def add_expert(e, acc):
        return acc + jnp.where(eidx == e, off_ref[e], 0)

    slot_ref[...] = lax.fori_loop(0, N_EXPERTS, add_expert, rank_ref[...])


def _slots(offsets, eidx, rank):
    k, m = eidx.shape
    tile = pl.BlockSpec((k, m), lambda i: (0, 0))
    return pl.pallas_call(
        _slots_kernel,
        grid=(1,),
        in_specs=[pl.BlockSpec(memory_space=pltpu.SMEM), tile, tile],
        out_specs=tile,
        out_shape=jax.ShapeDtypeStruct((k, m), jnp.int32),
        compiler_params=pltpu.CompilerParams(dimension_semantics=("arbitrary",),
                                             vmem_limit_bytes=VMEM_LIMIT_BYTES),
        name="slots",
    )(offsets, eidx, rank)


def _dispatch_kernel(slot_ref, h_ref, h_hbm_ref, xs_ref, sem):
    tmd = h_ref.shape[0]
    base = pl.program_id(0) * tmd

    def start_rows(j, c):
        for k in range(TOP_K):
            src = h_ref.at[pl.ds(j, 1)] if k < TOP_K // 2 else h_hbm_ref.at[pl.ds(base + j, 1)]
            pltpu.make_async_copy(src, xs_ref.at[pl.ds(slot_ref[k, j], 1)], sem).start(priority=k % 2)
        return c

    lax.fori_loop(0, tmd, start_rows, 0)
    for _ in range(TOP_K):
        pltpu.make_async_copy(h_ref, xs_ref.at[pl.ds(0, tmd)], sem).wait()


def _dispatch(slot, h2, n_rows, tmd):
    M = h2.shape[0]
    return pl.pallas_call(
        _dispatch_kernel,
        grid=(M // tmd,),
        in_specs=[pl.BlockSpec((TOP_K, tmd), lambda i: (0, i), memory_space=pltpu.SMEM),
                  pl.BlockSpec((tmd, ROW_TILE, LANES), lambda i: (i, 0, 0)),
                  pl.BlockSpec(memory_space=pl.ANY)],
        out_specs=pl.BlockSpec(memory_space=pl.ANY),
        out_shape=jax.ShapeDtypeStruct((n_rows, ROW_TILE, LANES), h2.dtype),
        scratch_shapes=[pltpu.SemaphoreType.DMA],
        compiler_params=pltpu.CompilerParams(dimension_semantics=("arbitrary",),
                                             vmem_limit_bytes=VMEM_LIMIT_BYTES),
        name="dispatch",
    )(slot, h2, h2)


FFN_PIPELINE_STEPS = 2


def _ffn_kernel(te_ref, nu_ref, nv_ref, x_ref, wg_ref, wu_ref, wd_ref, y_ref, wgu_s, wd_s, xstd, ystd):
    i = pl.program_id(0)
    row = lax.broadcasted_iota(jnp.int32, (MOE_ROWS, 1), 0)
    tile_mm = jnp.maximum(i - 1, 0)
    tile_out = jnp.maximum(i - 2, 0)

    @pl.when(i == 0)
    def _():
        xstd[...] = jnp.zeros(xstd.shape, BF16)
        ystd[...] = jnp.zeros(ystd.shape, F32)

    @pl.when(i < nu_ref[0] + FFN_PIPELINE_STEPS)
    def _():
        @pl.when((i == 0) | (te_ref[tile_mm] != te_ref[tile_out]))
        def _():
            wgu_s[:, :D_EXPERT] = wg_ref[0].astype(BF16)
            wgu_s[:, D_EXPERT:] = wu_ref[0].astype(BF16)
            wd_s[...] = wd_ref[0].astype(BF16)

        cur = i % 2
        xstd[cur] = jnp.where(row < nv_ref[i], _to_rows(x_ref[...].astype(F32)), 0.0).astype(BF16)
        y_ref[...] = _to_row_tiles(ystd[cur]).astype(y_ref.dtype)
        gu = jnp.dot(xstd[1 - cur], wgu_s[...], preferred_element_type=F32)
        act = _silu(gu[:, :D_EXPERT]) * gu[:, D_EXPERT:]
        ystd[1 - cur] = jnp.dot(act.astype(BF16), wd_s[...], preferred_element_type=F32)


def _ffn(tile_expert, n_used, n_valid, xs, wg, wu, wd):
    D = wg.shape[1]
    n_tiles = xs.shape[0] // MOE_ROWS
    block = (MOE_ROWS, ROW_TILE, LANES)
    expert = lambda i, te, nu, nv: (te[jnp.maximum(i - 1, 0)], 0, 0)
    return pl.pallas_call(
        _ffn_kernel,
        grid_spec=pltpu.PrefetchScalarGridSpec(
            num_scalar_prefetch=3,
            grid=(n_tiles + FFN_PIPELINE_STEPS,),
            in_specs=[pl.BlockSpec(block, lambda i, te, nu, nv: (jnp.minimum(i, nu[0] - 1), 0, 0)),
                      pl.BlockSpec((1, D, D_EXPERT), expert),
                      pl.BlockSpec((1, D, D_EXPERT), expert),
                      pl.BlockSpec((1, D_EXPERT, D), expert)],
            out_specs=pl.BlockSpec(block, lambda i, te, nu, nv: (jnp.clip(i - 2, 0, nu[0] - 1), 0, 0)),
            scratch_shapes=[pltpu.VMEM((D, 2 * D_EXPERT), BF16), pltpu.VMEM((D_EXPERT, D), BF16),
                            pltpu.VMEM((2, MOE_ROWS, D), BF16), pltpu.VMEM((2, MOE_ROWS, D), F32)]),
        out_shape=jax.ShapeDtypeStruct(xs.shape, xs.dtype),
        compiler_params=pltpu.CompilerParams(dimension_semantics=("arbitrary",),
                                             vmem_limit_bytes=VMEM_LIMIT_BYTES),
        name="ffn",
    )(tile_expert, n_used, n_valid, xs, wg, wu, wd)


def _combine_kernel(slot_ref, next_slot_ref, w_ref, h_ref, x1_ref, g2_ref, fw_ref, wsgu_ref, wsd_ref, y_ref,
                    o_ref, buf0, buf1, sem):
    tmc = h_ref.shape[0]
    i = pl.program_id(0)
    last = pl.num_programs(0) - 1
    bufs = (buf0, buf1)
    assert TOP_K == ROW_TILE

    def start_rows(slots, b):
        for j in range(tmc):
            for k in range(TOP_K):
                pltpu.make_async_copy(y_ref.at[pl.ds(slots[k, j], 1)], bufs[b].at[k, pl.ds(j, 1)], sem.at[b]).start(
                    priority=k % 2)

    def wait_rows(b):
        for k in range(TOP_K):
            pltpu.make_async_copy(y_ref.at[pl.ds(0, tmc)], bufs[b].at[k], sem.at[b]).wait()

    @pl.when(i == 0)
    def _():
        start_rows(slot_ref, 0)

    def step(b):
        wait_rows(b)
        start_rows(next_slot_ref, 1 - b)
        gu = jnp.dot(_to_rows(h_ref[...].astype(F32)).astype(BF16), wsgu_ref[...], preferred_element_type=F32)
        act = _silu(gu[:, :D_EXPERT]) * gu[:, D_EXPERT:]
        shared = jnp.dot(act.astype(BF16), wsd_ref[...], preferred_element_type=F32)
        w = w_ref[...]
        lane = lax.broadcasted_iota(jnp.int32, w.shape, 1)
        w3 = _to_row_tiles(jnp.concatenate(
            [jnp.broadcast_to(_lane_pick(w, lane, k), (tmc, LANES)) for k in range(TOP_K)], axis=1))
        routed = jnp.zeros((tmc, ROW_TILE, LANES), F32)
        for k in range(TOP_K):
            routed = routed + jnp.broadcast_to(w3[:, k:k + 1, :], w3.shape) * bufs[b][k].astype(F32)
        acc = shared + _to_rows(routed)
        y = x1_ref[...] + g2_ref[0] * acc
        o_ref[...] = _rms(y) * fw_ref[...]

        @pl.when(i == last)
        def _():
            wait_rows(1 - b)

    for b in range(2):
        @pl.when(i % 2 == b)
        def _():
            step(b)


def _combine(slot, wts, h2, x1, g2, fw, wsgu, wsd, ys, T, tmc):
    M, D = x1.shape
    const = lambda shape: pl.BlockSpec(shape, lambda i: (0,) * len(shape))
    tile = lambda w: pl.BlockSpec((tmc, w), lambda i: (i, 0))
    row_tiled = pl.BlockSpec((tmc, ROW_TILE, LANES), lambda i: (i, 0, 0))
    n_steps = M // tmc
    return pl.pallas_call(
        _combine_kernel,
        grid=(n_steps,),
        in_specs=[pl.BlockSpec((TOP_K, tmc), lambda i: (0, i), memory_space=pltpu.SMEM),
                  pl.BlockSpec((TOP_K, tmc), lambda i: (0, jnp.minimum(i + 1, n_steps - 1)), memory_space=pltpu.SMEM),
                  tile(GATE_LANES), row_tiled, tile(D),
                  pl.BlockSpec((1, 1, D), lambda i: (i // (T // tmc), 0, 0)),
                  const((1, D)), const(wsgu.shape), const(wsd.shape),
                  pl.BlockSpec(memory_space=pl.ANY)],
        out_specs=tile(D),
        out_shape=jax.ShapeDtypeStruct((M, D), F32),
        scratch_shapes=[pltpu.VMEM((TOP_K, tmc, ROW_TILE, LANES), ys.dtype),
                        pltpu.VMEM((TOP_K, tmc, ROW_TILE, LANES), ys.dtype), pltpu.SemaphoreType.DMA((2,))],
        compiler_params=pltpu.CompilerParams(dimension_semantics=("arbitrary",),
                                             vmem_limit_bytes=VMEM_LIMIT_BYTES),
        name="combine",
    )(slot, slot, wts, h2, x1, g2, fw, wsgu, wsd, ys)


def _pick_tile(n, want):
    t = min(n, want)
    assert n % t == 0 and t % CHUNK == 0, (n, want)
    return t


def kernel(x, c, w_ada, b_ada, norm1_w, w_in, conv_w, gdn_a_log, gdn_dt_bias, gdn_norm_w, hg_lb, hg_norm_w,
           w_out, norm2_w, w_router, router_bias, w_gate, w_up, w_down, ws_gate, ws_up, ws_down, final_norm_w):
    B, T, D = x.shape
    M = B * T
    depth = w_ada.shape[0]
    assert depth == 1 and T % CHUNK == 0 and B <= 8
    layer = 0
    tt = _pick_tile(T, TILE_RECURRENCE)

    c_pad = jnp.pad(c, ((0, 8 - B), (0, 0)))
    mod = _ada(c_pad, w_ada[layer], b_ada[layer].reshape(1, -1))[:B]
    sh1, sc1, g1, sh2, sc2, g2 = (m.reshape(B, 1, D) for m in jnp.split(mod, 6, axis=-1))

    w = w_in[layer]
    qkv_w = 3 * GW
    sizes = (GW, HEADS, HEADS, GW, GW, GW, GW)
    offs = [qkv_w]
    for s in sizes:
        offs.append(offs[-1] + s)
    seg = lambda i: w[:, offs[i]:offs[i + 1]]
    small = jnp.pad(jnp.concatenate([seg(1), seg(2)], axis=1), ((0, 0), (0, GATE_LANES - 2 * HEADS)))
    w_all = jnp.concatenate([w[:, :qkv_w], seg(0), seg(3), seg(4), seg(5), seg(6), small], axis=1).astype(BF16)
    lane_pad = lambda v: jnp.pad(v.astype(F32).reshape(1, HEADS), ((0, 0), (HEADS, GATE_LANES - 2 * HEADS)))
    idx = jnp.arange(CUMSUM_ROWS)
    tri = ((idx[:, None] >= idx[None, :]) & (idx[:, None] // CHUNK == idx[None, :] // CHUNK)).astype(BF16)

    qa, ka, va, ga, sm, bcum, kb, ib, qb, gb = _inproj(
        layer, x, norm1_w[layer].reshape(1, D), sc1, sh1, w_all, conv_w[layer].astype(F32),
        lane_pad(gdn_a_log[layer]), lane_pad(gdn_dt_bias[layer]), hg_lb.astype(F32), tri,
        _pick_tile(T, TILE_INPROJ))

    gct = sm[:, :, HEADS:2 * HEADS].transpose(0, 2, 1).reshape(B, HEADS, T // CHUNK, CHUNK)
    oa = _gdn(qa, ka, va, ga, sm, gct, gdn_norm_w[layer].reshape(1, DH), tt)
    ob = _hgrn(qb, kb, ib, bcum, gb, hg_norm_w[layer].reshape(1, GW), tt)

    wo = w_out[layer].astype(BF16)
    x1, h2, eidx, rank, wts, cnt = _outproj(oa, ob, x, wo[:GW], wo[GW:], g1, norm2_w[layer].reshape(1, D), sc2, sh2,
                                            w_router[layer].T, router_bias[layer].reshape(N_EXPERTS, 1),
                                            _pick_tile(T, TILE_OUTPROJ))

    counts = cnt[:, 0].astype(jnp.int32)
    padded = (counts + MOE_ROWS - 1) // MOE_ROWS * MOE_ROWS
    ends = jnp.cumsum(padded)
    offsets = ends - padded
    n_tiles = (M * TOP_K) // MOE_ROWS + N_EXPERTS
    n_used = (ends[-1] // MOE_ROWS).astype(jnp.int32)
    tile_ids = jnp.minimum(jnp.arange(n_tiles + FFN_PIPELINE_STEPS, dtype=jnp.int32), n_used - 1)
    tile_expert = jnp.sum(((ends // MOE_ROWS)[None, :] <= tile_ids[:, None]).astype(jnp.int32), axis=1)
    tile_expert = jnp.minimum(tile_expert, N_EXPERTS - 1)
    slot = _slots(offsets.astype(jnp.int32), eidx, rank)
    first_tile = jnp.take(offsets, tile_expert) // MOE_ROWS
    n_valid = jnp.clip(jnp.take(counts, tile_expert) - (tile_ids - first_tile) * MOE_ROWS, 0, MOE_ROWS)

    assert D == ROW_TILE * LANES
    xs = _dispatch(slot, h2, n_tiles * MOE_ROWS, _pick_tile(M, TILE_DISPATCH))
    ys = _ffn(tile_expert.astype(jnp.int32), n_used.reshape(1), n_valid.astype(jnp.int32), xs,
              w_gate[layer], w_up[layer], w_down[layer])
    wsgu = jnp.concatenate([ws_gate[layer], ws_up[layer]], axis=-1).astype(BF16)
    out = _combine(slot, wts.reshape(M, GATE_LANES), h2, x1.reshape(M, D), g2, final_norm_w.reshape(1, D),
                   wsgu, ws_down[layer].astype(BF16), ys, T, _pick_tile(T, TILE_COMBINE))
    return out.reshape(B, T, D)
```

```python
import functools

import jax
import jax.numpy as jnp
from jax import lax
from jax.experimental import pallas as pl
from jax.experimental.pallas import tpu as pltpu

F32 = jnp.float32
BF16 = jnp.bfloat16

EPS = 1e-6
CHUNK = 64
SUB = 8
HEADS = 4
DH = 128
GW = HEADS * DH
CONV_K = 4
N_EXPERTS = 64
N_GROUPS = 8
GROUP_SIZE = N_EXPERTS // N_GROUPS
TOPK_GROUPS = 4
TOP_K = 8
D_EXPERT = 256
ROUTE_SCALE = 2.5
GATE_LANES = 128
GDN_CHUNKS_PER_ITER = 8
INV_BLOCK = 16
CUMSUM_ROWS = 128
MOE_ROWS = 1024
ROW_TILE, LANES = 8, 128

TILE_INPROJ = 1024
TILE_RECURRENCE = 512
TILE_OUTPROJ = 1024
TILE_DISPATCH = 1024
TILE_COMBINE = 256
ADA_COLS = 1024

VMEM_LIMIT_BYTES = 56 * 1024 * 1024

ACT = BF16
ROW_DTYPE = F32


def _silu(x):
    return x * jax.nn.sigmoid(x)


def _dot(a, b):
    return jnp.dot(a.astype(BF16), b.astype(BF16), preferred_element_type=F32)


def _dot_nt(a, b):
    return lax.dot_general(a.astype(BF16), b.astype(BF16), (((1,), (1,)), ((), ())),
                           preferred_element_type=F32)


def _dot_tn(a, b):
    return lax.dot_general(a.astype(BF16), b.astype(BF16), (((0,), (0,)), ((), ())),
                           preferred_element_type=F32)


def _split2(x):
    hi = x.astype(BF16)
    lo = (x - hi.astype(F32)).astype(BF16)
    return hi, lo


def _dot3(a, b, dot=_dot):
    ah, al = _split2(a)
    bh, bl = _split2(b)
    return dot(ah, bh) + dot(ah, bl) + dot(al, bh)


def _cumsum_rows(tri, x):
    hi = x.astype(BF16)
    r = x - hi.astype(F32)
    mid = r.astype(BF16)
    lo = (r - mid.astype(F32)).astype(BF16)
    g = tri.shape[0]
    groups = []
    for r0 in range(0, x.shape[0], g):
        rows = slice(r0, r0 + g)
        groups.append(jnp.dot(tri, hi[rows], preferred_element_type=F32)
                      + jnp.dot(tri, mid[rows], preferred_element_type=F32)
                      + jnp.dot(tri, lo[rows], preferred_element_type=F32))
    return jnp.concatenate(groups, axis=0)


def _lane_pick(tile, lane, idx):
    return jnp.sum(jnp.where(lane == idx, tile, 0.0), axis=1, keepdims=True)


def _rms(x):
    return x * lax.rsqrt(jnp.mean(x * x, axis=-1, keepdims=True) + EPS)


def _to_rows(x3):
    r = x3.shape[0]
    xt = jnp.swapaxes(x3.reshape(r // ROW_TILE, ROW_TILE, ROW_TILE, LANES), 1, 2)
    return jnp.concatenate([xt[:, s].reshape(r, LANES) for s in range(ROW_TILE)], axis=1)


def _to_row_tiles(x):
    r = x.shape[0]
    xt = jnp.stack([x[:, s * LANES:(s + 1) * LANES].reshape(r // ROW_TILE, ROW_TILE, LANES) for s in range(ROW_TILE)],
                   axis=1)
    return jnp.swapaxes(xt, 1, 2).reshape(r, ROW_TILE, LANES)


def _ada_kernel(c_ref, w_ref, b_ref, o_ref):
    ca = _silu(c_ref[...])
    o_ref[...] = _dot3(ca, w_ref[...]) + b_ref[...]


def _ada(c_pad, w, b):
    rows, d = c_pad.shape
    n = w.shape[1]
    tn = ADA_COLS
    return pl.pallas_call(
        _ada_kernel,
        grid=(n // tn,),
        in_specs=[pl.BlockSpec((rows, d), lambda j: (0, 0)),
                  pl.BlockSpec((d, tn), lambda j: (0, j)),
                  pl.BlockSpec((1, tn), lambda j: (0, j))],
        out_specs=pl.BlockSpec((rows, tn), lambda j: (0, j)),
        out_shape=jax.ShapeDtypeStruct((rows, n), F32),
        compiler_params=pltpu.CompilerParams(dimension_semantics=("arbitrary",),
                                             vmem_limit_bytes=VMEM_LIMIT_BYTES),
        name="ada",
    )(c_pad, w, b)


def _inproj_kernel(layer, x_ref, n1_ref, sc_ref, sh_ref, w_ref, cw_ref, alog_ref, dt_ref, lb_ref, tri_ref,
                   qa_ref, ka_ref, va_ref, ga_ref, sm_ref, b_ref, kb_ref, ib_ref, qb_ref, gb_ref,
                   pbuf):
    tt = x_ref.shape[1]
    t = pl.program_id(1)

    h = _rms(x_ref[0]) * n1_ref[...]
    h = h * (1.0 + sc_ref[0]) + sh_ref[0]
    hb = h.astype(BF16)

    def proj(g, width=GW):
        return jnp.dot(hb, w_ref[:, g * GW:g * GW + width], preferred_element_type=F32)

    @pl.when(t == 0)
    def _():
        pbuf[:, 0:8, :] = jnp.zeros((3, 8, GW), F32)

    for g, out_ref in enumerate((qa_ref, ka_ref, va_ref)):
        cols = slice(g * GW, (g + 1) * GW)
        p = proj(g)
        pbuf[g, 8:8 + tt, :] = p
        y = p * cw_ref[CONV_K - 1:CONV_K, cols]
        for j in range(1, CONV_K):
            y = y + pbuf[g, 8 - j:8 - j + tt, :] * cw_ref[CONV_K - 1 - j:CONV_K - j, cols]
        pbuf[g, 0:8, :] = pbuf[g, tt:tt + 8, :]
        y = _silu(y)
        if g == 2:
            out_ref[0] = y.astype(out_ref.dtype)
        else:
            scale = DH ** -0.5 if g == 0 else 1.0
            for hh in range(HEADS):
                hs = slice(hh * DH, (hh + 1) * DH)
                yh = y[:, hs]
                inv = lax.rsqrt(jnp.sum(yh * yh, axis=-1, keepdims=True) + EPS)
                out_ref[0, :, hs] = (yh * inv * scale).astype(out_ref.dtype)

    ga_ref[0] = _silu(proj(3)).astype(ga_ref.dtype)

    ps = proj(8, GATE_LANES)
    lane = lax.broadcasted_iota(jnp.int32, ps.shape, 1)
    beta = jax.nn.sigmoid(ps)
    z = ps + dt_ref[...]
    softplus = jnp.maximum(z, 0.0) + jnp.log1p(jnp.exp(-jnp.abs(z)))
    g_log = -jnp.exp(alog_ref[...]) * softplus
    tri = tri_ref[...]
    gc = _cumsum_rows(tri, jnp.where((lane >= HEADS) & (lane < 2 * HEADS), g_log, 0.0))
    sm_ref[0] = jnp.where(lane < HEADS, beta, gc)

    hl = lb_ref[...]
    e = jnp.exp(hl - jnp.max(hl, axis=0, keepdims=True))
    lb = jnp.sum(e[0:layer + 1], axis=0, keepdims=True) / jnp.sum(e, axis=0, keepdims=True)
    fr = proj(4)
    logf = jnp.log(lb + (1.0 - lb) * jax.nn.sigmoid(fr))
    b_ref[0] = _cumsum_rows(tri, logf)
    kb_ref[0] = ((1.0 - lb) * jax.nn.sigmoid(-fr)).astype(kb_ref.dtype)
    ib_ref[0] = proj(5).astype(ib_ref.dtype)
    qb_ref[0] = _silu(proj(6)).astype(qb_ref.dtype)
    gb_ref[0] = _silu(proj(7)).astype(gb_ref.dtype)


def _inproj(layer, x, n1, sc1, sh1, w_all, conv_w, alog_pad, dt_pad, hg_lb, tri, tt):
    B, T, D = x.shape
    const = lambda shape: pl.BlockSpec(shape, lambda b, t: (0,) * len(shape), pipeline_mode=pl.Buffered(1))
    act = lambda dt: jax.ShapeDtypeStruct((B, T, GW), dt)
    tile = lambda w: pl.BlockSpec((1, tt, w), lambda b, t: (b, t, 0))
    per_batch = pl.BlockSpec((1, 1, D), lambda b, t: (b, 0, 0))
    return pl.pallas_call(
        functools.partial(_inproj_kernel, layer),
        grid=(B, T // tt),
        in_specs=[tile(D), const((1, D)), per_batch, per_batch,
                  const(w_all.shape), const(conv_w.shape), const((1, GATE_LANES)), const((1, GATE_LANES)),
                  const(hg_lb.shape), const(tri.shape)],
        out_specs=[tile(GW), tile(GW), tile(GW), tile(GW), tile(GATE_LANES), tile(GW),
                   tile(GW), tile(GW), tile(GW), tile(GW)],
        out_shape=[act(ACT), act(ACT), act(ACT), act(ACT),
                   jax.ShapeDtypeStruct((B, T, GATE_LANES), F32), act(F32),
                   act(ACT), act(ACT), act(ACT), act(ACT)],
        scratch_shapes=[pltpu.VMEM((3, tt + 8, GW), F32)],
        compiler_params=pltpu.CompilerParams(dimension_semantics=("arbitrary", "arbitrary"),
                                             vmem_limit_bytes=VMEM_LIMIT_BYTES),
        name="inproj",
    )(x, n1, sc1, sh1, w_all, conv_w, alog_pad, dt_pad, hg_lb, tri)


def _gdn_prep_kernel(q_ref, k_ref, v_ref, sm_ref, gct_ref, o_ref, qt_ref, m_ref, n_ref):
    tt = q_ref.shape[1]
    nc = tt // CHUNK
    row = lax.broadcasted_iota(jnp.int32, (CHUNK, CHUNK), 0)
    col = lax.broadcasted_iota(jnp.int32, (CHUNK, CHUNK), 1)
    causal = row >= col
    diag_blk = (row > col) & (row // INV_BLOCK == col // INV_BLOCK)
    off_blk = row // INV_BLOCK > col // INV_BLOCK
    eye = jnp.where(row == col, 1.0, 0.0)
    lane = lax.broadcasted_iota(jnp.int32, (CHUNK, GATE_LANES), 1)
    assert INV_BLOCK == 16 and CHUNK == 4 * INV_BLOCK

    def body(i, carry):
        chains = [(GDN_CHUNKS_PER_ITER * i + j, hh) for j in range(GDN_CHUNKS_PER_ITER) for hh in range(HEADS)]
        rows = [pl.ds(pl.multiple_of(c * CHUNK, CHUNK), CHUNK) for c, _ in chains]
        hs = [slice(hh * DH, (hh + 1) * DH) for _, hh in chains]
        n = range(len(chains))
        sm = [sm_ref[0, rows[j], :] for j in n]
        q = [q_ref[0, rows[j], hs[j]].astype(F32) for j in n]
        k = [k_ref[0, rows[j], hs[j]].astype(F32) for j in n]
        v = [v_ref[0, rows[j], hs[j]].astype(F32) for j in n]
        beta = [_lane_pick(sm[j], lane, chains[j][1]) for j in n]
        gcol = [_lane_pick(sm[j], lane, HEADS + chains[j][1]) for j in n]
        grow = [gct_ref[0, hh, pl.ds(c, 1), :] for c, hh in chains]
        decay = [jnp.exp(jnp.where(causal, gcol[j] - grow[j], -jnp.inf)) for j in n]
        kb = [k[j] * beta[j] for j in n]
        L = [_dot_nt(kb[j], k[j]) * decay[j] for j in n]
        dg = [jnp.where(diag_blk, L[j], 0.0) for j in n]
        off = [jnp.where(off_blk, L[j], 0.0) for j in n]
        dinv = [eye - dg[j] for j in n]
        pw = [_dot3(dg[j], dg[j]) for j in n]
        for _ in range(2):
            dinv = [dinv[j] + _dot3(dinv[j], pw[j]) for j in n]
            pw = [_dot3(pw[j], pw[j]) for j in n]
        dinv = [dinv[j] + _dot3(dinv[j], pw[j]) for j in n]
        f1 = [_dot(dinv[j], off[j]) for j in n]
        f2 = [_dot(f1[j], f1[j]) for j in n]
        f3 = [_dot(f1[j], f2[j]) for j in n]
        tinv = [_dot(eye - f1[j] + f2[j] - f3[j], dinv[j]) for j in n]
        eg = [jnp.exp(gcol[j]) for j in n]
        sol = [_dot(tinv[j], jnp.concatenate([v[j] * beta[j], kb[j] * eg[j]], axis=1)) for j in n]
        attn = [_dot_nt(q[j], k[j]) * decay[j] for j in n]
        k_tail = [k[j] * jnp.exp(gcol[j][CHUNK - 1:CHUNK, :] - gcol[j]) for j in n]
        au = [_dot(attn[j], sol[j]) for j in n]
        ku = [_dot_tn(k_tail[j], sol[j]) for j in n]
        for j, (c, hh) in enumerate(chains):
            o_ref[0, rows[j], hs[j]] = au[j][:, :DH]
            qt_ref[0, rows[j], hs[j]] = (q[j] * eg[j] - au[j][:, DH:]).astype(qt_ref.dtype)
            n_ref[0, hh, c] = ku[j][:, :DH].astype(n_ref.dtype)
            m_ref[0, hh, c] = (-ku[j][:, DH:]).astype(m_ref.dtype)
        return carry

    lax.fori_loop(0, nc // GDN_CHUNKS_PER_ITER, body, 0)


def _gdn_scan_kernel(o_ref, qt_ref, m_ref, n_ref, gct_ref, sg_ref, nw_ref, out_ref, s_ref):
    nb, tt = o_ref.shape[0], o_ref.shape[1]
    nc = tt // CHUNK

    @pl.when(pl.program_id(0) == 0)
    def _():
        s_ref[...] = jnp.zeros(s_ref.shape, F32)

    nw = nw_ref[...]

    def body(c, carry):
        rows = pl.ds(pl.multiple_of(c * CHUNK, CHUNK), CHUNK)
        for b in range(nb):
            for hh in range(HEADS):
                hs = slice(hh * DH, (hh + 1) * DH)
                S = s_ref[b, hh]
                Sb = S.astype(BF16)
                glast = gct_ref[b, hh, pl.ds(c, 1), :][:, CHUNK - 1:CHUNK]
                o = o_ref[b, rows, hs] + jnp.dot(qt_ref[b, rows, hs], Sb, preferred_element_type=F32)
                s_ref[b, hh] = (S * jnp.exp(glast) + jnp.dot(m_ref[b, hh, c], Sb, preferred_element_type=F32)
                                + n_ref[b, hh, c].astype(F32))
                o = _rms(o) * nw * sg_ref[b, rows, hs].astype(F32)
                out_ref[b, rows, hs] = o.astype(out_ref.dtype)
        return carry

    lax.fori_loop(0, nc, body, 0)


def _gdn(q, k, v, sg, sm, gct, nw, tt):
    B, T, _ = q.shape
    nc = tt // CHUNK
    assert nc % GDN_CHUNKS_PER_ITER == 0
    n_chunks = T // CHUNK
    tile = lambda w: pl.BlockSpec((1, tt, w), lambda b, t: (b, t, 0))
    mat = jax.ShapeDtypeStruct((B, HEADS, n_chunks, DH, DH), ACT)
    o_part, qt, m, n = pl.pallas_call(
        _gdn_prep_kernel,
        grid=(B, T // tt),
        in_specs=[tile(GW), tile(GW), tile(GW), tile(GATE_LANES),
                  pl.BlockSpec((1, HEADS, nc, CHUNK), lambda b, t: (b, 0, t, 0))],
        out_specs=[tile(GW), tile(GW),
                   pl.BlockSpec((1, HEADS, nc, DH, DH), lambda b, t: (b, 0, t, 0, 0)),
                   pl.BlockSpec((1, HEADS, nc, DH, DH), lambda b, t: (b, 0, t, 0, 0))],
        out_shape=[jax.ShapeDtypeStruct((B, T, GW), F32), jax.ShapeDtypeStruct((B, T, GW), ACT), mat, mat],
        compiler_params=pltpu.CompilerParams(dimension_semantics=("arbitrary", "arbitrary"),
                                             vmem_limit_bytes=VMEM_LIMIT_BYTES),
        name="gdn_prep",
    )(q, k, v, sm, gct)

    full = lambda w: pl.BlockSpec((B, tt, w), lambda t: (0, t, 0))
    mats = pl.BlockSpec((B, HEADS, nc, DH, DH), lambda t: (0, 0, t, 0, 0))
    return pl.pallas_call(
        _gdn_scan_kernel,
        grid=(T // tt,),
        in_specs=[full(GW), full(GW), mats, mats,
                  pl.BlockSpec((B, HEADS, nc, CHUNK), lambda t: (0, 0, t, 0)),
                  full(GW), pl.BlockSpec((1, DH), lambda t: (0, 0))],
        out_specs=full(GW),
        out_shape=jax.ShapeDtypeStruct((B, T, GW), ACT),
        scratch_shapes=[pltpu.VMEM((B, HEADS, DH, DH), F32)],
        compiler_params=pltpu.CompilerParams(dimension_semantics=("arbitrary",),
                                             vmem_limit_bytes=VMEM_LIMIT_BYTES),
        name="gdn_scan",
    )(o_part, qt, m, n, gct, sg, nw)


def _hgrn_kernel(q_ref, k_ref, v_ref, b_ref, sg_ref, nw_ref, o_ref, st_ref):
    nb, tt = q_ref.shape[0], q_ref.shape[1]
    nc = tt // CHUNK

    @pl.when(pl.program_id(0) == 0)
    def _():
        st_ref[...] = jnp.zeros(st_ref.shape, F32)

    row = lax.broadcasted_iota(jnp.int32, (CHUNK, CHUNK), 0)
    col = lax.broadcasted_iota(jnp.int32, (CHUNK, CHUNK), 1)
    diag_block = ((col // SUB) == (row // SUB)) & (col <= row)
    chains = [(bi, hh) for bi in range(nb) for hh in range(HEADS)]
    n = range(len(chains))
    hs = [slice(hh * DH, (hh + 1) * DH) for _, hh in chains]

    def body(c, carry):
        rows = pl.ds(pl.multiple_of(c * CHUNK, CHUNK), CHUNK)
        q = [q_ref[bi, rows, hs[j]].astype(F32) for j, (bi, _) in enumerate(chains)]
        k = [k_ref[bi, rows, hs[j]].astype(F32) for j, (bi, _) in enumerate(chains)]
        v = [v_ref[bi, rows, hs[j]].astype(F32) for j, (bi, _) in enumerate(chains)]
        b = [b_ref[bi, rows, hs[j]] for j, (bi, _) in enumerate(chains)]
        blast = [b[j][CHUNK - 1:CHUNK, :] for j in n]
        st = [st_ref[bi, hh] for bi, hh in chains]
        o = [_dot_nt(q[j] * jnp.exp(b[j]), st[j]) for j in n]
        k_tail = [k[j] * jnp.exp(blast[j] - b[j]) for j in n]
        for j, (bi, hh) in enumerate(chains):
            st_ref[bi, hh] = st[j] * jnp.exp(blast[j]) + _dot_tn(v[j], k_tail[j])

        blocks = [[jnp.zeros((SUB, CHUNK), F32)] for _ in n]
        for i in range(1, CHUNK // SUB):
            lo, hi = i * SUB, (i + 1) * SUB
            for j in n:
                r = b[j][lo:lo + 1, :]
                qi = q[j][lo:hi] * jnp.exp(b[j][lo:hi] - r)
                kj = k[j][:lo] * jnp.exp(jnp.minimum(r - b[j][:lo], 0.0))
                kj = jnp.concatenate([kj, jnp.zeros((CHUNK - lo, DH), F32)], axis=0)
                blocks[j].append(_dot_nt(qi, kj))
        a = []
        for j in n:
            f = jnp.exp(jnp.minimum(b[j] - pltpu.roll(b[j], 1, 0), 0.0))
            e = None
            a_diag = jnp.zeros((CHUNK, CHUNK), F32)
            for delta in range(SUB):
                if delta == 0:
                    term = q[j] * k[j]
                else:
                    fsh = f if delta == 1 else pltpu.roll(f, delta - 1, 0)
                    e = fsh if e is None else e * fsh
                    term = q[j] * pltpu.roll(k[j], delta, 0) * e
                colv = jnp.sum(term, axis=1, keepdims=True)
                a_diag = jnp.where(row - col == delta, colv, a_diag)
            a.append(jnp.where(diag_block, a_diag, jnp.concatenate(blocks[j], axis=0)))

        o = [o[j] + _dot(a[j], v[j]) for j in n]
        for bi in range(nb):
            ob = jnp.concatenate(o[bi * HEADS:(bi + 1) * HEADS], axis=1)
            ob = _rms(ob) * nw_ref[...] * sg_ref[bi, rows, :].astype(F32)
            o_ref[bi, rows, :] = ob.astype(o_ref.dtype)
        return carry

    lax.fori_loop(0, nc, body, 0)


def _hgrn(q, k, v, b, sg, nw, tt):
    B, T, _ = q.shape
    tile = pl.BlockSpec((B, tt, GW), lambda t: (0, t, 0))
    return pl.pallas_call(
        _hgrn_kernel,
        grid=(T // tt,),
        in_specs=[tile, tile, tile, tile, tile, pl.BlockSpec((1, GW), lambda t: (0, 0))],
        out_specs=tile,
        out_shape=jax.ShapeDtypeStruct((B, T, GW), ACT),
        scratch_shapes=[pltpu.VMEM((B, HEADS, DH, DH), F32)],
        compiler_params=pltpu.CompilerParams(dimension_semantics=("arbitrary",),
                                             vmem_limit_bytes=VMEM_LIMIT_BYTES),
        name="hgrn",
    )(q, k, v, b, sg, nw)


def _outproj_kernel(oa_ref, ob_ref, x_ref, wa_ref, wb_ref, g1_ref, n2_ref, sc_ref, sh_ref, wr_ref, rb_ref,
                    x1_ref, h2_ref, eidx_ref, rank_ref, wts_ref, cnt_ref):
    tm = x_ref.shape[1]

    @pl.when((pl.program_id(0) == 0) & (pl.program_id(1) == 0))
    def _():
        cnt_ref[...] = jnp.zeros(cnt_ref.shape, F32)

    mix = (jnp.dot(oa_ref[0], wa_ref[...], preferred_element_type=F32)
           + jnp.dot(ob_ref[0], wb_ref[...], preferred_element_type=F32))
    x1 = x_ref[0] + g1_ref[0] * mix
    x1_ref[0] = x1
    h2 = _rms(x1) * n2_ref[...]
    h2 = h2 * (1.0 + sc_ref[0]) + sh_ref[0]
    h2_ref[...] = _to_row_tiles(h2).astype(h2_ref.dtype)

    scores = jax.nn.sigmoid(_dot3(wr_ref[...], h2, dot=_dot_nt))
    sel = scores + rb_ref[...]
    sub = lax.broadcasted_iota(jnp.int32, (GROUP_SIZE, tm), 0)
    neg = -jnp.inf
    groups = range(N_GROUPS)

    def take_max(blk):
        m = jnp.max(blk, axis=0, keepdims=True)
        first = jnp.min(jnp.where(blk == m, sub, GROUP_SIZE), axis=0, keepdims=True)
        hit = sub == first
        return m, hit, jnp.where(hit, neg, blk)

    blk_of = lambda a, g: a[g * GROUP_SIZE:(g + 1) * GROUP_SIZE]
    sel_blk = [blk_of(sel, g) for g in groups]
    group_score = jnp.zeros((N_GROUPS, tm), F32)
    for g in groups:
        m1, _, rest = take_max(sel_blk[g])
        m2 = jnp.max(rest, axis=0, keepdims=True)
        group_score = jnp.where(sub == g, m1 + m2, group_score)
    group_on = jnp.zeros((N_GROUPS, tm), F32)
    for _ in range(TOPK_GROUPS):
        _, hit, group_score = take_max(group_score)
        group_on = jnp.where(hit, 1.0, group_on)

    cand = [jnp.where(group_on[g:g + 1] > 0.0, sel_blk[g], neg) for g in groups]
    picked = [jnp.zeros((GROUP_SIZE, tm), F32) for _ in groups]
    chosen = []
    for _ in range(TOP_K):
        m = jnp.max(functools.reduce(jnp.maximum, cand), axis=0, keepdims=True)
        first = functools.reduce(jnp.minimum, [jnp.where(cand[g] == m, sub + g * GROUP_SIZE, N_EXPERTS)
                                               for g in groups])
        first = jnp.min(first, axis=0, keepdims=True)
        chosen.append(first)
        for g in groups:
            hit = (sub + g * GROUP_SIZE) == first
            picked[g] = jnp.where(hit, 1.0, picked[g])
            cand[g] = jnp.where(hit, neg, cand[g])

    picked_all = jnp.concatenate(picked, axis=0)
    r_i = lax.broadcasted_iota(jnp.int32, (tm, tm), 0)
    c_i = lax.broadcasted_iota(jnp.int32, (tm, tm), 1)
    earlier = jnp.where(r_i < c_i, 1.0, 0.0).astype(BF16)
    before = jnp.dot(picked_all.astype(BF16), earlier, preferred_element_type=F32) + cnt_ref[:, 0:1]
    cnt_ref[...] = cnt_ref[...] + jnp.sum(picked_all, axis=1, keepdims=True)

    def pick_value(table, first):
        parts = [jnp.where((sub + g * GROUP_SIZE) == first, blk_of(table, g), 0.0) for g in groups]
        return jnp.sum(functools.reduce(jnp.add, parts), axis=0, keepdims=True)

    w_k = [pick_value(scores, f) for f in chosen]
    denom = functools.reduce(jnp.add, w_k)
    eidx = jnp.zeros((TOP_K, tm), jnp.int32)
    rank = jnp.zeros((TOP_K, tm), jnp.int32)
    wts = jnp.zeros((TOP_K, tm), F32)
    for k in range(TOP_K):
        eidx = jnp.where(sub == k, chosen[k], eidx)
        rank = jnp.where(sub == k, pick_value(before, chosen[k]).astype(jnp.int32), rank)
        wts = jnp.where(sub == k, w_k[k] / denom * ROUTE_SCALE, wts)
    eidx_ref[...] = eidx
    rank_ref[...] = rank
    pad = jnp.zeros((GATE_LANES - TOP_K, tm), F32)
    wts_ref[0] = jnp.concatenate([wts, pad], axis=0).T


def _outproj(oa, ob, x, wa, wb, g1, n2, sc2, sh2, wr_t, rb, tm):
    B, T, D = x.shape
    nt = T // tm
    const = lambda shape: pl.BlockSpec(shape, lambda b, t: (0,) * len(shape))
    tile = lambda w: pl.BlockSpec((1, tm, w), lambda b, t: (b, t, 0))
    per_batch = pl.BlockSpec((1, 1, D), lambda b, t: (b, 0, 0))
    picks = pl.BlockSpec((TOP_K, tm), lambda b, t: (0, b * nt + t))
    return pl.pallas_call(
        _outproj_kernel,
        grid=(B, nt),
        in_specs=[tile(GW), tile(GW), tile(D), const(wa.shape), const(wb.shape), per_batch,
                  const((1, D)), per_batch, per_batch, const(wr_t.shape), const(rb.shape)],
        out_specs=[tile(D), pl.BlockSpec((tm, ROW_TILE, LANES), lambda b, t: (b * nt + t, 0, 0)),
                   picks, picks, tile(GATE_LANES), const((N_EXPERTS, GATE_LANES))],
        out_shape=[jax.ShapeDtypeStruct((B, T, D), F32), jax.ShapeDtypeStruct((B * T, ROW_TILE, LANES), ROW_DTYPE),
                   jax.ShapeDtypeStruct((TOP_K, B * T), jnp.int32), jax.ShapeDtypeStruct((TOP_K, B * T), jnp.int32),
                   jax.ShapeDtypeStruct((B, T, GATE_LANES), F32),
                   jax.ShapeDtypeStruct((N_EXPERTS, GATE_LANES), F32)],
        compiler_params=pltpu.CompilerParams(dimension_semantics=("arbitrary", "arbitrary"),
                                             vmem_limit_bytes=VMEM_LIMIT_BYTES),
        name="outproj",
    )(oa, ob, x, wa, wb, g1, n2, sc2, sh2, wr_t, rb)


def _slots_kernel(off_ref, eidx_ref, rank_ref, slot_ref):
    eidx = eidx_ref[...]

    def add_expert(e, acc):
        return acc + jnp.where(eidx == e, off_ref[e], 0)

    slot_ref[...] = lax.fori_loop(0, N_EXPERTS, add_expert, rank_ref[...])


def _slots(offsets, eidx, rank):
    k, m = eidx.shape
    tile = pl.BlockSpec((k, m), lambda i: (0, 0))
    return pl.pallas_call(
        _slots_kernel,
        grid=(1,),
        in_specs=[pl.BlockSpec(memory_space=pltpu.SMEM), tile, tile],
        out_specs=tile,
        out_shape=jax.ShapeDtypeStruct((k, m), jnp.int32),
        compiler_params=pltpu.CompilerParams(dimension_semantics=("arbitrary",),
                                             vmem_limit_bytes=VMEM_LIMIT_BYTES),
        name="slots",
    )(offsets, eidx, rank)


def _dispatch_kernel(slot_ref, h_ref, xs_ref, sem):
    tmd = h_ref.shape[0]

    def start_rows(j, c):
        for k in range(TOP_K):
            pltpu.make_async_copy(h_ref.at[pl.ds(j, 1)], xs_ref.at[pl.ds(slot_ref[k, j], 1)], sem).start(
                priority=k % 2)
        return c

    lax.fori_loop(0, tmd, start_rows, 0)
    for _ in range(TOP_K):
        pltpu.make_async_copy(h_ref, xs_ref.at[pl.ds(0, tmd)], sem).wait()


def _dispatch(slot, h2, n_rows, tmd):
    M = h2.shape[0]
    return pl.pallas_call(
        _dispatch_kernel,
        grid=(M // tmd,),
        in_specs=[pl.BlockSpec((TOP_K, tmd), lambda i: (0, i), memory_space=pltpu.SMEM),
                  pl.BlockSpec((tmd, ROW_TILE, LANES), lambda i: (i, 0, 0))],
        out_specs=pl.BlockSpec(memory_space=pl.ANY),
        out_shape=jax.ShapeDtypeStruct((n_rows, ROW_TILE, LANES), h2.dtype),
        scratch_shapes=[pltpu.SemaphoreType.DMA],
        compiler_params=pltpu.CompilerParams(dimension_semantics=("arbitrary",),
                                             vmem_limit_bytes=VMEM_LIMIT_BYTES),
        name="dispatch",
    )(slot, h2)


FFN_PIPELINE_STEPS = 1


def _ffn_kernel(te_ref, nu_ref, nv_ref, x_ref, wg_ref, wu_ref, wd_ref, y_ref, wgu_s, wd_s, xstd):
    i = pl.program_id(0)
    row = lax.broadcasted_iota(jnp.int32, (MOE_ROWS, 1), 0)
    tile_mm = jnp.maximum(i - 1, 0)
    tile_prev = jnp.maximum(i - 2, 0)

    @pl.when(i == 0)
    def _():
        xstd[...] = jnp.zeros(xstd.shape, BF16)

    @pl.when(i < nu_ref[0] + FFN_PIPELINE_STEPS)
    def _():
        @pl.when((i == 0) | (te_ref[tile_mm] != te_ref[tile_prev]))
        def _():
            wgu_s[:, :D_EXPERT] = wg_ref[0].astype(BF16)
            wgu_s[:, D_EXPERT:] = wu_ref[0].astype(BF16)
            wd_s[...] = wd_ref[0].astype(BF16)

        cur = i % 2
        xstd[cur] = jnp.where(row < nv_ref[i], _to_rows(x_ref[...].astype(F32)), 0.0).astype(BF16)
        gu = jnp.dot(xstd[1 - cur], wgu_s[...], preferred_element_type=F32)
        act = _silu(gu[:, :D_EXPERT]) * gu[:, D_EXPERT:]
        y = jnp.dot(act.astype(BF16), wd_s[...], preferred_element_type=F32)
        y_ref[...] = _to_row_tiles(y).astype(y_ref.dtype)


def _ffn(tile_expert, n_used, n_valid, xs, wg, wu, wd):
    D = wg.shape[1]
    n_tiles = xs.shape[0] // MOE_ROWS
    block = (MOE_ROWS, ROW_TILE, LANES)
    expert = lambda i, te, nu, nv: (te[jnp.maximum(i - 1, 0)], 0, 0)
    return pl.pallas_call(
        _ffn_kernel,
        grid_spec=pltpu.PrefetchScalarGridSpec(
            num_scalar_prefetch=3,
            grid=(n_tiles + FFN_PIPELINE_STEPS,),
            in_specs=[pl.BlockSpec(block, lambda i, te, nu, nv: (jnp.minimum(i, nu[0] - 1), 0, 0)),
                      pl.BlockSpec((1, D, D_EXPERT), expert),
                      pl.BlockSpec((1, D, D_EXPERT), expert),
                      pl.BlockSpec((1, D_EXPERT, D), expert)],
            out_specs=pl.BlockSpec(block, lambda i, te, nu, nv: (jnp.clip(i - FFN_PIPELINE_STEPS, 0, nu[0] - 1), 0, 0)),
            scratch_shapes=[pltpu.VMEM((D, 2 * D_EXPERT), BF16), pltpu.VMEM((D_EXPERT, D), BF16),
                            pltpu.VMEM((2, MOE_ROWS, D), BF16)]),
        out_shape=jax.ShapeDtypeStruct(xs.shape, xs.dtype),
        compiler_params=pltpu.CompilerParams(dimension_semantics=("arbitrary",),
                                             vmem_limit_bytes=VMEM_LIMIT_BYTES),
        name="ffn",
    )(tile_expert, n_used, n_valid, xs, wg, wu, wd)


def _combine_kernel(slot_ref, next_slot_ref, w_ref, h_ref, x1_ref, g2_ref, fw_ref, wsgu_ref, wsd_ref, y_ref,
                    o_ref, buf0, buf1, sem):
    tmc = h_ref.shape[0]
    i = pl.program_id(0)
    last = pl.num_programs(0) - 1
    bufs = (buf0, buf1)
    assert TOP_K == ROW_TILE

    def start_rows(slots, b):
        for j in range(tmc):
            for k in range(TOP_K):
                pltpu.make_async_copy(y_ref.at[pl.ds(slots[k, j], 1)], bufs[b].at[k, pl.ds(j, 1)], sem.at[b]).start(
                    priority=k % 2)

    def wait_rows(b):
        for k in range(TOP_K):
            pltpu.make_async_copy(y_ref.at[pl.ds(0, tmc)], bufs[b].at[k], sem.at[b]).wait()

    @pl.when(i == 0)
    def _():
        start_rows(slot_ref, 0)

    def step(b):
        wait_rows(b)
        start_rows(next_slot_ref, 1 - b)
        gu = jnp.dot(_to_rows(h_ref[...].astype(F32)).astype(BF16), wsgu_ref[...], preferred_element_type=F32)
        act = _silu(gu[:, :D_EXPERT]) * gu[:, D_EXPERT:]
        shared = jnp.dot(act.astype(BF16), wsd_ref[...], preferred_element_type=F32)
        w = w_ref[...]
        lane = lax.broadcasted_iota(jnp.int32, w.shape, 1)
        w3 = _to_row_tiles(jnp.concatenate(
            [jnp.broadcast_to(_lane_pick(w, lane, k), (tmc, LANES)) for k in range(TOP_K)], axis=1))
        routed = jnp.zeros((tmc, ROW_TILE, LANES), F32)
        for k in range(TOP_K):
            routed = routed + jnp.broadcast_to(w3[:, k:k + 1, :], w3.shape) * bufs[b][k].astype(F32)
        acc = shared + _to_rows(routed)
        y = x1_ref[...] + g2_ref[0] * acc
        o_ref[...] = _rms(y) * fw_ref[...]

        @pl.when(i == last)
        def _():
            wait_rows(1 - b)

    for b in range(2):
        @pl.when(i % 2 == b)
        def _():
            step(b)


def _combine(slot, wts, h2, x1, g2, fw, wsgu, wsd, ys, T, tmc):
    M, D = x1.shape
    const = lambda shape: pl.BlockSpec(shape, lambda i: (0,) * len(shape))
    tile = lambda w: pl.BlockSpec((tmc, w), lambda i: (i, 0))
    row_tiled = pl.BlockSpec((tmc, ROW_TILE, LANES), lambda i: (i, 0, 0))
    n_steps = M // tmc
    return pl.pallas_call(
        _combine_kernel,
        grid=(n_steps,),
        in_specs=[pl.BlockSpec((TOP_K, tmc), lambda i: (0, i), memory_space=pltpu.SMEM),
                  pl.BlockSpec((TOP_K, tmc), lambda i: (0, jnp.minimum(i + 1, n_steps - 1)), memory_space=pltpu.SMEM),
                  tile(GATE_LANES), row_tiled, tile(D),
                  pl.BlockSpec((1, 1, D), lambda i: (i // (T // tmc), 0, 0)),
                  const((1, D)), const(wsgu.shape), const(wsd.shape),
                  pl.BlockSpec(memory_space=pl.ANY)],
        out_specs=tile(D),
        out_shape=jax.ShapeDtypeStruct((M, D), F32),
        scratch_shapes=[pltpu.VMEM((TOP_K, tmc, ROW_TILE, LANES), ys.dtype),
                        pltpu.VMEM((TOP_K, tmc, ROW_TILE, LANES), ys.dtype), pltpu.SemaphoreType.DMA((2,))],
        compiler_params=pltpu.CompilerParams(dimension_semantics=("arbitrary",),
                                             vmem_limit_bytes=VMEM_LIMIT_BYTES),
        name="combine",
    )(slot, slot, wts, h2, x1, g2, fw, wsgu, wsd, ys)


def _pick_tile(n, want):
    t = min(n, want)
    assert n % t == 0 and t % CHUNK == 0, (n, want)
    return t


def kernel(x, c, w_ada, b_ada, norm1_w, w_in, conv_w, gdn_a_log, gdn_dt_bias, gdn_norm_w, hg_lb, hg_norm_w,
           w_out, norm2_w, w_router, router_bias, w_gate, w_up, w_down, ws_gate, ws_up, ws_down, final_norm_w):
    B, T, D = x.shape
    M = B * T
    depth = w_ada.shape[0]
    assert depth == 1 and T % CHUNK == 0 and B <= 8
    layer = 0
    tt = _pick_tile(T, TILE_RECURRENCE)

    c_pad = jnp.pad(c, ((0, 8 - B), (0, 0)))
    mod = _ada(c_pad, w_ada[layer], b_ada[layer].reshape(1, -1))[:B]
    sh1, sc1, g1, sh2, sc2, g2 = (m.reshape(B, 1, D) for m in jnp.split(mod, 6, axis=-1))

    w = w_in[layer]
    qkv_w = 3 * GW
    sizes = (GW, HEADS, HEADS, GW, GW, GW, GW)
    offs = [qkv_w]
    for s in sizes:
        offs.append(offs[-1] + s)
    seg = lambda i: w[:, offs[i]:offs[i + 1]]
    small = jnp.pad(jnp.concatenate([seg(1), seg(2)], axis=1), ((0, 0), (0, GATE_LANES - 2 * HEADS)))
    w_all = jnp.concatenate([w[:, :qkv_w], seg(0), seg(3), seg(4), seg(5), seg(6), small], axis=1).astype(BF16)
    lane_pad = lambda v: jnp.pad(v.astype(F32).reshape(1, HEADS), ((0, 0), (HEADS, GATE_LANES - 2 * HEADS)))
    idx = jnp.arange(CUMSUM_ROWS)
    tri = ((idx[:, None] >= idx[None, :]) & (idx[:, None] // CHUNK == idx[None, :] // CHUNK)).astype(BF16)

    qa, ka, va, ga, sm, bcum, kb, ib, qb, gb = _inproj(
        layer, x, norm1_w[layer].reshape(1, D), sc1, sh1, w_all, conv_w[layer].astype(F32),
        lane_pad(gdn_a_log[layer]), lane_pad(gdn_dt_bias[layer]), hg_lb.astype(F32), tri,
        _pick_tile(T, TILE_INPROJ))

    gct = sm[:, :, HEADS:2 * HEADS].transpose(0, 2, 1).reshape(B, HEADS, T // CHUNK, CHUNK)
    oa = _gdn(qa, ka, va, ga, sm, gct, gdn_norm_w[layer].reshape(1, DH), tt)
    ob = _hgrn(qb, kb, ib, bcum, gb, hg_norm_w[layer].reshape(1, GW), tt)

    wo = w_out[layer].astype(BF16)
    x1, h2, eidx, rank, wts, cnt = _outproj(oa, ob, x, wo[:GW], wo[GW:], g1, norm2_w[layer].reshape(1, D), sc2, sh2,
                                            w_router[layer].T, router_bias[layer].reshape(N_EXPERTS, 1),
                                            _pick_tile(T, TILE_OUTPROJ))

    counts = cnt[:, 0].astype(jnp.int32)
    padded = (counts + MOE_ROWS - 1) // MOE_ROWS * MOE_ROWS
    ends = jnp.cumsum(padded)
    offsets = ends - padded
    n_tiles = (M * TOP_K) // MOE_ROWS + N_EXPERTS
    n_used = (ends[-1] // MOE_ROWS).astype(jnp.int32)
    tile_ids = jnp.minimum(jnp.arange(n_tiles + FFN_PIPELINE_STEPS, dtype=jnp.int32), n_used - 1)
    tile_expert = jnp.sum(((ends // MOE_ROWS)[None, :] <= tile_ids[:, None]).astype(jnp.int32), axis=1)
    tile_expert = jnp.minimum(tile_expert, N_EXPERTS - 1)
    slot = _slots(offsets.astype(jnp.int32), eidx, rank)
    first_tile = jnp.take(offsets, tile_expert) // MOE_ROWS
    n_valid = jnp.clip(jnp.take(counts, tile_expert) - (tile_ids - first_tile) * MOE_ROWS, 0, MOE_ROWS)

    assert D == ROW_TILE * LANES
    xs = _dispatch(slot, h2, n_tiles * MOE_ROWS, _pick_tile(M, TILE_DISPATCH))
    ys = _ffn(tile_expert.astype(jnp.int32), n_used.reshape(1), n_valid.astype(jnp.int32), xs,
              w_gate[layer], w_up[layer], w_down[layer])
    wsgu = jnp.concatenate([ws_gate[layer], ws_up[layer]], axis=-1).astype(BF16)
    out = _combine(slot, wts.reshape(M, GATE_LANES), h2, x1.reshape(M, D), g2, final_norm_w.reshape(1, D),
                   wsgu, ws_down[layer].astype(BF16), ys, T, _pick_tile(T, TILE_COMBINE))
    return out.reshape(B, T, D)
```

```python
import functools

import jax
import jax.numpy as jnp
from jax import lax
from jax.experimental import pallas as pl
from jax.experimental.pallas import tpu as pltpu

F32 = jnp.float32
BF16 = jnp.bfloat16

EPS = 1e-6
CHUNK = 64
SUB = 8
HEADS = 4
DH = 128
GW = HEADS * DH
CONV_K = 4
N_EXPERTS = 64
N_GROUPS = 8
GROUP_SIZE = N_EXPERTS // N_GROUPS
TOPK_GROUPS = 4
TOP_K = 8
D_EXPERT = 256
ROUTE_SCALE = 2.5
GATE_LANES = 128
GDN_CHUNKS_PER_ITER = 8
INV_BLOCK = 16
CUMSUM_ROWS = 128
MOE_ROWS = 1024
ROW_TILE, LANES = 8, 128

TILE_INPROJ = 1024
TILE_RECURRENCE = 512
TILE_OUTPROJ = 1024
TILE_DISPATCH = 1024
TILE_COMBINE = 256
ADA_COLS = 1024

VMEM_LIMIT_BYTES = 56 * 1024 * 1024

ACT = BF16
ROW_DTYPE = F32


def _silu(x):
    return x * jax.nn.sigmoid(x)


def _dot(a, b):
    return jnp.dot(a.astype(BF16), b.astype(BF16), preferred_element_type=F32)


def _dot_nt(a, b):
    return lax.dot_general(a.astype(BF16), b.astype(BF16), (((1,), (1,)), ((), ())),
                           preferred_element_type=F32)


def _dot_tn(a, b):
    return lax.dot_general(a.astype(BF16), b.astype(BF16), (((0,), (0,)), ((), ())),
                           preferred_element_type=F32)


def _split2(x):
    hi = x.astype(BF16)
    lo = (x - hi.astype(F32)).astype(BF16)
    return hi, lo


def _dot3(a, b, dot=_dot):
    ah, al = _split2(a)
    bh, bl = _split2(b)
    return dot(ah, bh) + dot(ah, bl) + dot(al, bh)


def _cumsum_rows(tri, x):
    hi = x.astype(BF16)
    r = x - hi.astype(F32)
    mid = r.astype(BF16)
    lo = (r - mid.astype(F32)).astype(BF16)
    g = tri.shape[0]
    groups = []
    for r0 in range(0, x.shape[0], g):
        rows = slice(r0, r0 + g)
        groups.append(jnp.dot(tri, hi[rows], preferred_element_type=F32)
                      + jnp.dot(tri, mid[rows], preferred_element_type=F32)
                      + jnp.dot(tri, lo[rows], preferred_element_type=F32))
    return jnp.concatenate(groups, axis=0)


def _lane_pick(tile, lane, idx):
    return jnp.sum(jnp.where(lane == idx, tile, 0.0), axis=1, keepdims=True)


def _rms(x):
    return x * lax.rsqrt(jnp.mean(x * x, axis=-1, keepdims=True) + EPS)


def _to_rows(x3):
    r = x3.shape[0]
    xt = jnp.swapaxes(x3.reshape(r // ROW_TILE, ROW_TILE, ROW_TILE, LANES), 1, 2)
    return jnp.concatenate([xt[:, s].reshape(r, LANES) for s in range(ROW_TILE)], axis=1)


def _to_row_tiles(x):
    r = x.shape[0]
    xt = jnp.stack([x[:, s * LANES:(s + 1) * LANES].reshape(r // ROW_TILE, ROW_TILE, LANES) for s in range(ROW_TILE)],
                   axis=1)
    return jnp.swapaxes(xt, 1, 2).reshape(r, ROW_TILE, LANES)


def _ada_kernel(c_ref, w_ref, b_ref, o_ref):
    ca = _silu(c_ref[...])
    o_ref[...] = _dot3(ca, w_ref[...]) + b_ref[...]


def _ada(c_pad, w, b):
    rows, d = c_pad.shape
    n = w.shape[1]
    tn = ADA_COLS
    return pl.pallas_call(
        _ada_kernel,
        grid=(n // tn,),
        in_specs=[pl.BlockSpec((rows, d), lambda j: (0, 0)),
                  pl.BlockSpec((d, tn), lambda j: (0, j)),
                  pl.BlockSpec((1, tn), lambda j: (0, j))],
        out_specs=pl.BlockSpec((rows, tn), lambda j: (0, j)),
        out_shape=jax.ShapeDtypeStruct((rows, n), F32),
        compiler_params=pltpu.CompilerParams(dimension_semantics=("arbitrary",),
                                             vmem_limit_bytes=VMEM_LIMIT_BYTES),
        name="ada",
    )(c_pad, w, b)


def _inproj_kernel(layer, x_ref, n1_ref, sc_ref, sh_ref, w_ref, cw_ref, alog_ref, dt_ref, lb_ref, tri_ref,
                   qa_ref, ka_ref, va_ref, ga_ref, sm_ref, b_ref, kb_ref, ib_ref, qb_ref, gb_ref,
                   pbuf):
    tt = x_ref.shape[1]
    t = pl.program_id(1)

    h = _rms(x_ref[0]) * n1_ref[...]
    h = h * (1.0 + sc_ref[0]) + sh_ref[0]
    hb = h.astype(BF16)

    def proj(g, width=GW):
        return jnp.dot(hb, w_ref[:, g * GW:g * GW + width], preferred_element_type=F32)

    @pl.when(t == 0)
    def _():
        pbuf[:, 0:8, :] = jnp.zeros((3, 8, GW), F32)

    for g, out_ref in enumerate((qa_ref, ka_ref, va_ref)):
        cols = slice(g * GW, (g + 1) * GW)
        p = proj(g)
        pbuf[g, 8:8 + tt, :] = p
        y = p * cw_ref[CONV_K - 1:CONV_K, cols]
        for j in range(1, CONV_K):
            y = y + pbuf[g, 8 - j:8 - j + tt, :] * cw_ref[CONV_K - 1 - j:CONV_K - j, cols]
        pbuf[g, 0:8, :] = pbuf[g, tt:tt + 8, :]
        y = _silu(y)
        if g == 2:
            out_ref[0] = y.astype(out_ref.dtype)
        else:
            scale = DH ** -0.5 if g == 0 else 1.0
            for hh in range(HEADS):
                hs = slice(hh * DH, (hh + 1) * DH)
                yh = y[:, hs]
                inv = lax.rsqrt(jnp.sum(yh * yh, axis=-1, keepdims=True) + EPS)
                out_ref[0, :, hs] = (yh * inv * scale).astype(out_ref.dtype)

    ga_ref[0] = _silu(proj(3)).astype(ga_ref.dtype)

    ps = proj(8, GATE_LANES)
    lane = lax.broadcasted_iota(jnp.int32, ps.shape, 1)
    beta = jax.nn.sigmoid(ps)
    z = ps + dt_ref[...]
    softplus = jnp.maximum(z, 0.0) + jnp.log1p(jnp.exp(-jnp.abs(z)))
    g_log = -jnp.exp(alog_ref[...]) * softplus
    tri = tri_ref[...]
    gc = _cumsum_rows(tri, jnp.where((lane >= HEADS) & (lane < 2 * HEADS), g_log, 0.0))
    sm_ref[0] = jnp.where(lane < HEADS, beta, gc)

    hl = lb_ref[...]
    e = jnp.exp(hl - jnp.max(hl, axis=0, keepdims=True))
    lb = jnp.sum(e[0:layer + 1], axis=0, keepdims=True) / jnp.sum(e, axis=0, keepdims=True)
    fr = proj(4)
    logf = jnp.log(lb + (1.0 - lb) * jax.nn.sigmoid(fr))
    b_ref[0] = _cumsum_rows(tri, logf)
    kb_ref[0] = ((1.0 - lb) * jax.nn.sigmoid(-fr)).astype(kb_ref.dtype)
    ib_ref[0] = proj(5).astype(ib_ref.dtype)
    qb_ref[0] = _silu(proj(6)).astype(qb_ref.dtype)
    gb_ref[0] = _silu(proj(7)).astype(gb_ref.dtype)


def _inproj(layer, x, n1, sc1, sh1, w_all, conv_w, alog_pad, dt_pad, hg_lb, tri, tt):
    B, T, D = x.shape
    const = lambda shape: pl.BlockSpec(shape, lambda b, t: (0,) * len(shape), pipeline_mode=pl.Buffered(1))
    act = lambda dt: jax.ShapeDtypeStruct((B, T, GW), dt)
    tile = lambda w: pl.BlockSpec((1, tt, w), lambda b, t: (b, t, 0))
    per_batch = pl.BlockSpec((1, 1, D), lambda b, t: (b, 0, 0))
    return pl.pallas_call(
        functools.partial(_inproj_kernel, layer),
        grid=(B, T // tt),
        in_specs=[tile(D), const((1, D)), per_batch, per_batch,
                  const(w_all.shape), const(conv_w.shape), const((1, GATE_LANES)), const((1, GATE_LANES)),
                  const(hg_lb.shape), const(tri.shape)],
        out_specs=[tile(GW), tile(GW), tile(GW), tile(GW), tile(GATE_LANES), tile(GW),
                   tile(GW), tile(GW), tile(GW), tile(GW)],
        out_shape=[act(ACT), act(ACT), act(ACT), act(ACT),
                   jax.ShapeDtypeStruct((B, T, GATE_LANES), F32), act(F32),
                   act(ACT), act(ACT), act(ACT), act(ACT)],
        scratch_shapes=[pltpu.VMEM((3, tt + 8, GW), F32)],
        compiler_params=pltpu.CompilerParams(dimension_semantics=("arbitrary", "arbitrary"),
                                             vmem_limit_bytes=VMEM_LIMIT_BYTES),
        name="inproj",
    )(x, n1, sc1, sh1, w_all, conv_w, alog_pad, dt_pad, hg_lb, tri)


def _gdn_prep_kernel(q_ref, k_ref, v_ref, sm_ref, gct_ref, o_ref, qt_ref, m_ref, n_ref):
    tt = q_ref.shape[1]
    nc = tt // CHUNK
    row = lax.broadcasted_iota(jnp.int32, (CHUNK, CHUNK), 0)
    col = lax.broadcasted_iota(jnp.int32, (CHUNK, CHUNK), 1)
    causal = row >= col
    diag_blk = (row > col) & (row // INV_BLOCK == col // INV_BLOCK)
    off_blk = row // INV_BLOCK > col // INV_BLOCK
    eye = jnp.where(row == col, 1.0, 0.0)
    lane = lax.broadcasted_iota(jnp.int32, (CHUNK, GATE_LANES), 1)
    assert INV_BLOCK == 16 and CHUNK == 4 * INV_BLOCK

    def body(i, carry):
        chains = [(GDN_CHUNKS_PER_ITER * i + j, hh) for j in range(GDN_CHUNKS_PER_ITER) for hh in range(HEADS)]
        rows = [pl.ds(pl.multiple_of(c * CHUNK, CHUNK), CHUNK) for c, _ in chains]
        hs = [slice(hh * DH, (hh + 1) * DH) for _, hh in chains]
        n = range(len(chains))
        sm = [sm_ref[0, rows[j], :] for j in n]
        q = [q_ref[0, rows[j], hs[j]].astype(F32) for j in n]
        k = [k_ref[0, rows[j], hs[j]].astype(F32) for j in n]
        v = [v_ref[0, rows[j], hs[j]].astype(F32) for j in n]
        beta = [_lane_pick(sm[j], lane, chains[j][1]) for j in n]
        gcol = [_lane_pick(sm[j], lane, HEADS + chains[j][1]) for j in n]
        grow = [gct_ref[0, hh, pl.ds(c, 1), :] for c, hh in chains]
        decay = [jnp.exp(jnp.where(causal, gcol[j] - grow[j], -jnp.inf)) for j in n]
        kb = [k[j] * beta[j] for j in n]
        L = [_dot_nt(kb[j], k[j]) * decay[j] for j in n]
        dg = [jnp.where(diag_blk, L[j], 0.0) for j in n]
        off = [jnp.where(off_blk, L[j], 0.0) for j in n]
        dinv = [eye - dg[j] for j in n]
        pw = [_dot3(dg[j], dg[j]) for j in n]
        for _ in range(2):
            dinv = [dinv[j] + _dot3(dinv[j], pw[j]) for j in n]
            pw = [_dot3(pw[j], pw[j]) for j in n]
        dinv = [dinv[j] + _dot3(dinv[j], pw[j]) for j in n]
        f1 = [_dot(dinv[j], off[j]) for j in n]
        f2 = [_dot(f1[j], f1[j]) for j in n]
        f3 = [_dot(f1[j], f2[j]) for j in n]
        tinv = [_dot(eye - f1[j] + f2[j] - f3[j], dinv[j]) for j in n]
        eg = [jnp.exp(gcol[j]) for j in n]
        sol = [_dot(tinv[j], jnp.concatenate([v[j] * beta[j], kb[j] * eg[j]], axis=1)) for j in n]
        attn = [_dot_nt(q[j], k[j]) * decay[j] for j in n]
        k_tail = [k[j] * jnp.exp(gcol[j][CHUNK - 1:CHUNK, :] - gcol[j]) for j in n]
        au = [_dot(attn[j], sol[j]) for j in n]
        ku = [_dot_tn(k_tail[j], sol[j]) for j in n]
        for j, (c, hh) in enumerate(chains):
            o_ref[0, rows[j], hs[j]] = au[j][:, :DH]
            qt_ref[0, rows[j], hs[j]] = (q[j] * eg[j] - au[j][:, DH:]).astype(qt_ref.dtype)
            n_ref[0, hh, c] = ku[j][:, :DH].astype(n_ref.dtype)
            m_ref[0, hh, c] = (-ku[j][:, DH:]).astype(m_ref.dtype)
        return carry

    lax.fori_loop(0, nc // GDN_CHUNKS_PER_ITER, body, 0)


def _gdn_scan_kernel(o_ref, qt_ref, m_ref, n_ref, gct_ref, sg_ref, nw_ref, out_ref, s_ref):
    nb, tt = o_ref.shape[0], o_ref.shape[1]
    nc = tt // CHUNK

    @pl.when(pl.program_id(0) == 0)
    def _():
        s_ref[...] = jnp.zeros(s_ref.shape, F32)

    nw = nw_ref[...]

    def body(c, carry):
        rows = pl.ds(pl.multiple_of(c * CHUNK, CHUNK), CHUNK)
        for b in range(nb):
            for hh in range(HEADS):
                hs = slice(hh * DH, (hh + 1) * DH)
                S = s_ref[b, hh]
                Sb = S.astype(BF16)
                glast = gct_ref[b, hh, pl.ds(c, 1), :][:, CHUNK - 1:CHUNK]
                o = o_ref[b, rows, hs] + jnp.dot(qt_ref[b, rows, hs], Sb, preferred_element_type=F32)
                s_ref[b, hh] = (S * jnp.exp(glast) + jnp.dot(m_ref[b, hh, c], Sb, preferred_element_type=F32)
                                + n_ref[b, hh, c].astype(F32))
                o = _rms(o) * nw * sg_ref[b, rows, hs].astype(F32)
                out_ref[b, rows, hs] = o.astype(out_ref.dtype)
        return carry

    lax.fori_loop(0, nc, body, 0)


def _gdn(q, k, v, sg, sm, gct, nw, tt):
    B, T, _ = q.shape
    nc = tt // CHUNK
    assert nc % GDN_CHUNKS_PER_ITER == 0
    n_chunks = T // CHUNK
    tile = lambda w: pl.BlockSpec((1, tt, w), lambda b, t: (b, t, 0))
    mat = jax.ShapeDtypeStruct((B, HEADS, n_chunks, DH, DH), ACT)
    o_part, qt, m, n = pl.pallas_call(
        _gdn_prep_kernel,
        grid=(B, T // tt),
        in_specs=[tile(GW), tile(GW), tile(GW), tile(GATE_LANES),
                  pl.BlockSpec((1, HEADS, nc, CHUNK), lambda b, t: (b, 0, t, 0))],
        out_specs=[tile(GW), tile(GW),
                   pl.BlockSpec((1, HEADS, nc, DH, DH), lambda b, t: (b, 0, t, 0, 0)),
                   pl.BlockSpec((1, HEADS, nc, DH, DH), lambda b, t: (b, 0, t, 0, 0))],
        out_shape=[jax.ShapeDtypeStruct((B, T, GW), F32), jax.ShapeDtypeStruct((B, T, GW), ACT), mat, mat],
        compiler_params=pltpu.CompilerParams(dimension_semantics=("arbitrary", "arbitrary"),
                                             vmem_limit_bytes=VMEM_LIMIT_BYTES),
        name="gdn_prep",
    )(q, k, v, sm, gct)

    full = lambda w: pl.BlockSpec((B, tt, w), lambda t: (0, t, 0))
    mats = pl.BlockSpec((B, HEADS, nc, DH, DH), lambda t: (0, 0, t, 0, 0))
    return pl.pallas_call(
        _gdn_scan_kernel,
        grid=(T // tt,),
        in_specs=[full(GW), full(GW), mats, mats,
                  pl.BlockSpec((B, HEADS, nc, CHUNK), lambda t: (0, 0, t, 0)),
                  full(GW), pl.BlockSpec((1, DH), lambda t: (0, 0))],
        out_specs=full(GW),
        out_shape=jax.ShapeDtypeStruct((B, T, GW), ACT),
        scratch_shapes=[pltpu.VMEM((B, HEADS, DH, DH), F32)],
        compiler_params=pltpu.CompilerParams(dimension_semantics=("arbitrary",),
                                             vmem_limit_bytes=VMEM_LIMIT_BYTES),
        name="gdn_scan",
    )(o_part, qt, m, n, gct, sg, nw)


def _hgrn_kernel(q_ref, k_ref, v_ref, b_ref, sg_ref, nw_ref, o_ref, st_ref):
    nb, tt = q_ref.shape[0], q_ref.shape[1]
    nc = tt // CHUNK

    @pl.when(pl.program_id(0) == 0)
    def _():
        st_ref[...] = jnp.zeros(st_ref.shape, F32)

    row = lax.broadcasted_iota(jnp.int32, (CHUNK, CHUNK), 0)
    col = lax.broadcasted_iota(jnp.int32, (CHUNK, CHUNK), 1)
    diag_block = ((col // SUB) == (row // SUB)) & (col <= row)
    chains = [(bi, hh) for bi in range(nb) for hh in range(HEADS)]
    n = range(len(chains))
    hs = [slice(hh * DH, (hh + 1) * DH) for _, hh in chains]

    def body(c, carry):
        rows = pl.ds(pl.multiple_of(c * CHUNK, CHUNK), CHUNK)
        q = [q_ref[bi, rows, hs[j]].astype(F32) for j, (bi, _) in enumerate(chains)]
        k = [k_ref[bi, rows, hs[j]].astype(F32) for j, (bi, _) in enumerate(chains)]
        v = [v_ref[bi, rows, hs[j]].astype(F32) for j, (bi, _) in enumerate(chains)]
        b = [b_ref[bi, rows, hs[j]] for j, (bi, _) in enumerate(chains)]
        blast = [b[j][CHUNK - 1:CHUNK, :] for j in n]
        st = [st_ref[bi, hh] for bi, hh in chains]
        o = [_dot_nt(q[j] * jnp.exp(b[j]), st[j]) for j in n]
        k_tail = [k[j] * jnp.exp(blast[j] - b[j]) for j in n]
        for j, (bi, hh) in enumerate(chains):
            st_ref[bi, hh] = st[j] * jnp.exp(blast[j]) + _dot_tn(v[j], k_tail[j])

        blocks = [[jnp.zeros((SUB, CHUNK), F32)] for _ in n]
        for i in range(1, CHUNK // SUB):
            lo, hi = i * SUB, (i + 1) * SUB
            for j in n:
                r = b[j][lo:lo + 1, :]
                qi = q[j][lo:hi] * jnp.exp(b[j][lo:hi] - r)
                kj = k[j][:lo] * jnp.exp(jnp.minimum(r - b[j][:lo], 0.0))
                kj = jnp.concatenate([kj, jnp.zeros((CHUNK - lo, DH), F32)], axis=0)
                blocks[j].append(_dot_nt(qi, kj))
        a = []
        for j in n:
            f = jnp.exp(jnp.minimum(b[j] - pltpu.roll(b[j], 1, 0), 0.0))
            e = None
            a_diag = jnp.zeros((CHUNK, CHUNK), F32)
            for delta in range(SUB):
                if delta == 0:
                    term = q[j] * k[j]
                else:
                    fsh = f if delta == 1 else pltpu.roll(f, delta - 1, 0)
                    e = fsh if e is None else e * fsh
                    term = q[j] * pltpu.roll(k[j], delta, 0) * e
                colv = jnp.sum(term, axis=1, keepdims=True)
                a_diag = jnp.where(row - col == delta, colv, a_diag)
            a.append(jnp.where(diag_block, a_diag, jnp.concatenate(blocks[j], axis=0)))

        o = [o[j] + _dot(a[j], v[j]) for j in n]
        for bi in range(nb):
            ob = jnp.concatenate(o[bi * HEADS:(bi + 1) * HEADS], axis=1)
            ob = _rms(ob) * nw_ref[...] * sg_ref[bi, rows, :].astype(F32)
            o_ref[bi, rows, :] = ob.astype(o_ref.dtype)
        return carry

    lax.fori_loop(0, nc, body, 0)


def _hgrn(q, k, v, b, sg, nw, tt):
    B, T, _ = q.shape
    tile = pl.BlockSpec((B, tt, GW), lambda t: (0, t, 0))
    return pl.pallas_call(
        _hgrn_kernel,
        grid=(T // tt,),
        in_specs=[tile, tile, tile, tile, tile, pl.BlockSpec((1, GW), lambda t: (0, 0))],
        out_specs=tile,
        out_shape=jax.ShapeDtypeStruct((B, T, GW), ACT),
        scratch_shapes=[pltpu.VMEM((B, HEADS, DH, DH), F32)],
        compiler_params=pltpu.CompilerParams(dimension_semantics=("arbitrary",),
                                             vmem_limit_bytes=VMEM_LIMIT_BYTES),
        name="hgrn",
    )(q, k, v, b, sg, nw)


def _outproj_kernel(oa_ref, ob_ref, x_ref, wa_ref, wb_ref, g1_ref, n2_ref, sc_ref, sh_ref, wr_ref, rb_ref,
                    x1_ref, h2_ref, eidx_ref, rank_ref, wts_ref, cnt_ref):
    tm = x_ref.shape[1]

    @pl.when((pl.program_id(0) == 0) & (pl.program_id(1) == 0))
    def _():
        cnt_ref[...] = jnp.zeros(cnt_ref.shape, F32)

    mix = (jnp.dot(oa_ref[0], wa_ref[...], preferred_element_type=F32)
           + jnp.dot(ob_ref[0], wb_ref[...], preferred_element_type=F32))
    x1 = x_ref[0] + g1_ref[0] * mix
    x1_ref[0] = x1
    h2 = _rms(x1) * n2_ref[...]
    h2 = h2 * (1.0 + sc_ref[0]) + sh_ref[0]
    h2_ref[...] = _to_row_tiles(h2).astype(h2_ref.dtype)

    scores = jax.nn.sigmoid(_dot3(wr_ref[...], h2, dot=_dot_nt))
    sel = scores + rb_ref[...]
    sub = lax.broadcasted_iota(jnp.int32, (GROUP_SIZE, tm), 0)
    neg = -jnp.inf
    groups = range(N_GROUPS)

    def take_max(blk):
        m = jnp.max(blk, axis=0, keepdims=True)
        first = jnp.min(jnp.where(blk == m, sub, GROUP_SIZE), axis=0, keepdims=True)
        hit = sub == first
        return m, hit, jnp.where(hit, neg, blk)

    blk_of = lambda a, g: a[g * GROUP_SIZE:(g + 1) * GROUP_SIZE]
    sel_blk = [blk_of(sel, g) for g in groups]
    group_score = jnp.zeros((N_GROUPS, tm), F32)
    for g in groups:
        m1, _, rest = take_max(sel_blk[g])
        m2 = jnp.max(rest, axis=0, keepdims=True)
        group_score = jnp.where(sub == g, m1 + m2, group_score)
    group_on = jnp.zeros((N_GROUPS, tm), F32)
    for _ in range(TOPK_GROUPS):
        _, hit, group_score = take_max(group_score)
        group_on = jnp.where(hit, 1.0, group_on)

    cand = [jnp.where(group_on[g:g + 1] > 0.0, sel_blk[g], neg) for g in groups]
    picked = [jnp.zeros((GROUP_SIZE, tm), F32) for _ in groups]
    chosen = []
    for _ in range(TOP_K):
        m = jnp.max(functools.reduce(jnp.maximum, cand), axis=0, keepdims=True)
        first = functools.reduce(jnp.minimum, [jnp.where(cand[g] == m, sub + g * GROUP_SIZE, N_EXPERTS)
                                               for g in groups])
        first = jnp.min(first, axis=0, keepdims=True)
        chosen.append(first)
        for g in groups:
            hit = (sub + g * GROUP_SIZE) == first
            picked[g] = jnp.where(hit, 1.0, picked[g])
            cand[g] = jnp.where(hit, neg, cand[g])

    picked_all = jnp.concatenate(picked, axis=0)
    r_i = lax.broadcasted_iota(jnp.int32, (tm, tm), 0)
    c_i = lax.broadcasted_iota(jnp.int32, (tm, tm), 1)
    earlier = jnp.where(r_i < c_i, 1.0, 0.0).astype(BF16)
    before = jnp.dot(picked_all.astype(BF16), earlier, preferred_element_type=F32) + cnt_ref[:, 0:1]
    cnt_ref[...] = cnt_ref[...] + jnp.sum(picked_all, axis=1, keepdims=True)

    def pick_value(table, first):
        parts = [jnp.where((sub + g * GROUP_SIZE) == first, blk_of(table, g), 0.0) for g in groups]
        return jnp.sum(functools.reduce(jnp.add, parts), axis=0, keepdims=True)

    w_k = [pick_value(scores, f) for f in chosen]
    denom = functools.reduce(jnp.add, w_k)
    eidx = jnp.zeros((TOP_K, tm), jnp.int32)
    rank = jnp.zeros((TOP_K, tm), jnp.int32)
    wts = jnp.zeros((TOP_K, tm), F32)
    for k in range(TOP_K):
        eidx = jnp.where(sub == k, chosen[k], eidx)
        rank = jnp.where(sub == k, pick_value(before, chosen[k]).astype(jnp.int32), rank)
        wts = jnp.where(sub == k, w_k[k] / denom * ROUTE_SCALE, wts)
    eidx_ref[...] = eidx
    rank_ref[...] = rank
    pad = jnp.zeros((GATE_LANES - TOP_K, tm), F32)
    wts_ref[0] = jnp.concatenate([wts, pad], axis=0).T


def _outproj(oa, ob, x, wa, wb, g1, n2, sc2, sh2, wr_t, rb, tm):
    B, T, D = x.shape
    nt = T // tm
    const = lambda shape: pl.BlockSpec(shape, lambda b, t: (0,) * len(shape))
    tile = lambda w: pl.BlockSpec((1, tm, w), lambda b, t: (b, t, 0))
    per_batch = pl.BlockSpec((1, 1, D), lambda b, t: (b, 0, 0))
    picks = pl.BlockSpec((TOP_K, tm), lambda b, t: (0, b * nt + t))
    return pl.pallas_call(
        _outproj_kernel,
        grid=(B, nt),
        in_specs=[tile(GW), tile(GW), tile(D), const(wa.shape), const(wb.shape), per_batch,
                  const((1, D)), per_batch, per_batch, const(wr_t.shape), const(rb.shape)],
        out_specs=[tile(D), pl.BlockSpec((tm, ROW_TILE, LANES), lambda b, t: (b * nt + t, 0, 0)),
                   picks, picks, tile(GATE_LANES), const((N_EXPERTS, GATE_LANES))],
        out_shape=[jax.ShapeDtypeStruct((B, T, D), F32), jax.ShapeDtypeStruct((B * T, ROW_TILE, LANES), ROW_DTYPE),
                   jax.ShapeDtypeStruct((TOP_K, B * T), jnp.int32), jax.ShapeDtypeStruct((TOP_K, B * T), jnp.int32),
                   jax.ShapeDtypeStruct((B, T, GATE_LANES), F32),
                   jax.ShapeDtypeStruct((N_EXPERTS, GATE_LANES), F32)],
        compiler_params=pltpu.CompilerParams(dimension_semantics=("arbitrary", "arbitrary"),
                                             vmem_limit_bytes=VMEM_LIMIT_BYTES),
        name="outproj",
    )(oa, ob, x, wa, wb, g1, n2, sc2, sh2, wr_t, rb)


def _slots_kernel(off_ref, eidx_ref, rank_ref, slot_ref):
    eidx = eidx_ref[...]

    def add_expert(e, acc):
        return acc + jnp.where(eidx == e, off_ref[e], 0)

    slot_ref[...] = lax.fori_loop(0, N_EXPERTS, add_expert, rank_ref[...])


def _slots(offsets, eidx, rank):
    k, m = eidx.shape
    tile = pl.BlockSpec((k, m), lambda i: (0, 0))
    return pl.pallas_call(
        _slots_kernel,
        grid=(1,),
        in_specs=[pl.BlockSpec(memory_space=pltpu.SMEM), tile, tile],
        out_specs=tile,
        out_shape=jax.ShapeDtypeStruct((k, m), jnp.int32),
        compiler_params=pltpu.CompilerParams(dimension_semantics=("arbitrary",),
                                             vmem_limit_bytes=VMEM_LIMIT_BYTES),
        name="slots",
    )(offsets, eidx, rank)


def _dispatch_kernel(slot_ref, h_ref, xs_ref, sem):
    tmd = h_ref.shape[0]

    def start_rows(j, c):
        for k in range(TOP_K):
            pltpu.make_async_copy(h_ref.at[pl.ds(j, 1)], xs_ref.at[pl.ds(slot_ref[k, j], 1)], sem).start(
                priority=k % 2)
        return c

    lax.fori_loop(0, tmd, start_rows, 0)
    for _ in range(TOP_K):
        pltpu.make_async_copy(h_ref, xs_ref.at[pl.ds(0, tmd)], sem).wait()


def _dispatch(slot, h2, n_rows, tmd):
    M = h2.shape[0]
    return pl.pallas_call(
        _dispatch_kernel,
        grid=(M // tmd,),
        in_specs=[pl.BlockSpec((TOP_K, tmd), lambda i: (0, i), memory_space=pltpu.SMEM),
                  pl.BlockSpec((tmd, ROW_TILE, LANES), lambda i: (i, 0, 0))],
        out_specs=pl.BlockSpec(memory_space=pl.ANY),
        out_shape=jax.ShapeDtypeStruct((n_rows, ROW_TILE, LANES), h2.dtype),
        scratch_shapes=[pltpu.SemaphoreType.DMA],
        compiler_params=pltpu.CompilerParams(dimension_semantics=("arbitrary",),
                                             vmem_limit_bytes=VMEM_LIMIT_BYTES),
        name="dispatch",
    )(slot, h2)


FFN_PIPELINE_STEPS = 0


def _ffn_kernel(te_ref, nu_ref, nv_ref, x_ref, wg_ref, wu_ref, wd_ref, y_ref, wgu_s, wd_s):
    i = pl.program_id(0)
    row = lax.broadcasted_iota(jnp.int32, (MOE_ROWS, 1), 0)

    @pl.when(i < nu_ref[0])
    def _():
        @pl.when((i == 0) | (te_ref[i] != te_ref[jnp.maximum(i - 1, 0)]))
        def _():
            wgu_s[:, :D_EXPERT] = wg_ref[0].astype(BF16)
            wgu_s[:, D_EXPERT:] = wu_ref[0].astype(BF16)
            wd_s[...] = wd_ref[0].astype(BF16)

        x = jnp.where(row < nv_ref[i], _to_rows(x_ref[...].astype(F32)), 0.0).astype(BF16)
        gu = jnp.dot(x, wgu_s[...], preferred_element_type=F32)
        act = _silu(gu[:, :D_EXPERT]) * gu[:, D_EXPERT:]
        y = jnp.dot(act.astype(BF16), wd_s[...], preferred_element_type=F32)
        y_ref[...] = _to_row_tiles(y).astype(y_ref.dtype)


def _ffn(tile_expert, n_used, n_valid, xs, wg, wu, wd):
    D = wg.shape[1]
    n_tiles = xs.shape[0] // MOE_ROWS
    block = (MOE_ROWS, ROW_TILE, LANES)
    expert = lambda i, te, nu, nv: (te[i], 0, 0)
    rows = pl.BlockSpec(block, lambda i, te, nu, nv: (jnp.minimum(i, nu[0] - 1), 0, 0))
    return pl.pallas_call(
        _ffn_kernel,
        grid_spec=pltpu.PrefetchScalarGridSpec(
            num_scalar_prefetch=3,
            grid=(n_tiles,),
            in_specs=[rows,
                      pl.BlockSpec((1, D, D_EXPERT), expert),
                      pl.BlockSpec((1, D, D_EXPERT), expert),
                      pl.BlockSpec((1, D_EXPERT, D), expert)],
            out_specs=rows,
            scratch_shapes=[pltpu.VMEM((D, 2 * D_EXPERT), BF16), pltpu.VMEM((D_EXPERT, D), BF16)]),
        out_shape=jax.ShapeDtypeStruct(xs.shape, xs.dtype),
        compiler_params=pltpu.CompilerParams(dimension_semantics=("arbitrary",),
                                             vmem_limit_bytes=VMEM_LIMIT_BYTES),
        name="ffn",
    )(tile_expert, n_used, n_valid, xs, wg, wu, wd)


def _combine_kernel(slot_ref, next_slot_ref, w_ref, h_ref, x1_ref, g2_ref, fw_ref, wsgu_ref, wsd_ref, y_ref,
                    o_ref, buf0, buf1, sem):
    tmc = h_ref.shape[0]
    i = pl.program_id(0)
    last = pl.num_programs(0) - 1
    bufs = (buf0, buf1)
    assert TOP_K == ROW_TILE

    def start_rows(slots, b):
        for j in range(tmc):
            for k in range(TOP_K):
                pltpu.make_async_copy(y_ref.at[pl.ds(slots[k, j], 1)], bufs[b].at[k, pl.ds(j, 1)], sem.at[b]).start(
                    priority=k % 2)

    def wait_rows(b):
        for k in range(TOP_K):
            pltpu.make_async_copy(y_ref.at[pl.ds(0, tmc)], bufs[b].at[k], sem.at[b]).wait()

    @pl.when(i == 0)
    def _():
        start_rows(slot_ref, 0)

    def step(b):
        wait_rows(b)
        start_rows(next_slot_ref, 1 - b)
        gu = jnp.dot(_to_rows(h_ref[...].astype(F32)).astype(BF16), wsgu_ref[...], preferred_element_type=F32)
        act = _silu(gu[:, :D_EXPERT]) * gu[:, D_EXPERT:]
        shared = jnp.dot(act.astype(BF16), wsd_ref[...], preferred_element_type=F32)
        w = w_ref[...]
        lane = lax.broadcasted_iota(jnp.int32, w.shape, 1)
        w3 = _to_row_tiles(jnp.concatenate(
            [jnp.broadcast_to(_lane_pick(w, lane, k), (tmc, LANES)) for k in range(TOP_K)], axis=1))
        routed = jnp.zeros((tmc, ROW_TILE, LANES), F32)
        for k in range(TOP_K):
            routed = routed + jnp.broadcast_to(w3[:, k:k + 1, :], w3.shape) * bufs[b][k].astype(F32)
        acc = shared + _to_rows(routed)
        y = x1_ref[...] + g2_ref[0] * acc
        o_ref[...] = _rms(y) * fw_ref[...]

        @pl.when(i == last)
        def _():
            wait_rows(1 - b)

    for b in range(2):
        @pl.when(i % 2 == b)
        def _():
            step(b)


def _combine(slot, wts, h2, x1, g2, fw, wsgu, wsd, ys, T, tmc):
    M, D = x1.shape
    const = lambda shape: pl.BlockSpec(shape, lambda i: (0,) * len(shape))
    tile = lambda w: pl.BlockSpec((tmc, w), lambda i: (i, 0))
    row_tiled = pl.BlockSpec((tmc, ROW_TILE, LANES), lambda i: (i, 0, 0))
    n_steps = M // tmc
    return pl.pallas_call(
        _combine_kernel,
        grid=(n_steps,),
        in_specs=[pl.BlockSpec((TOP_K, tmc), lambda i: (0, i), memory_space=pltpu.SMEM),
                  pl.BlockSpec((TOP_K, tmc), lambda i: (0, jnp.minimum(i + 1, n_steps - 1)), memory_space=pltpu.SMEM),
                  tile(GATE_LANES), row_tiled, tile(D),
                  pl.BlockSpec((1, 1, D), lambda i: (i // (T // tmc), 0, 0)),
                  const((1, D)), const(wsgu.shape), const(wsd.shape),
                  pl.BlockSpec(memory_space=pl.ANY)],
        out_specs=tile(D),
        out_shape=jax.ShapeDtypeStruct((M, D), F32),
        scratch_shapes=[pltpu.VMEM((TOP_K, tmc, ROW_TILE, LANES), ys.dtype),
                        pltpu.VMEM((TOP_K, tmc, ROW_TILE, LANES), ys.dtype), pltpu.SemaphoreType.DMA((2,))],
        compiler_params=pltpu.CompilerParams(dimension_semantics=("arbitrary",),
                                             vmem_limit_bytes=VMEM_LIMIT_BYTES),
        name="combine",
    )(slot, slot, wts, h2, x1, g2, fw, wsgu, wsd, ys)


def _pick_tile(n, want):
    t = min(n, want)
    assert n % t == 0 and t % CHUNK == 0, (n, want)
    return t


def kernel(x, c, w_ada, b_ada, norm1_w, w_in, conv_w, gdn_a_log, gdn_dt_bias, gdn_norm_w, hg_lb, hg_norm_w,
           w_out, norm2_w, w_router, router_bias, w_gate, w_up, w_down, ws_gate, ws_up, ws_down, final_norm_w):
    B, T, D = x.shape
    M = B * T
    depth = w_ada.shape[0]
    assert depth == 1 and T % CHUNK == 0 and B <= 8
    layer = 0
    tt = _pick_tile(T, TILE_RECURRENCE)

    c_pad = jnp.pad(c, ((0, 8 - B), (0, 0)))
    mod = _ada(c_pad, w_ada[layer], b_ada[layer].reshape(1, -1))[:B]
    sh1, sc1, g1, sh2, sc2, g2 = (m.reshape(B, 1, D) for m in jnp.split(mod, 6, axis=-1))

    w = w_in[layer]
    qkv_w = 3 * GW
    sizes = (GW, HEADS, HEADS, GW, GW, GW, GW)
    offs = [qkv_w]
    for s in sizes:
        offs.append(offs[-1] + s)
    seg = lambda i: w[:, offs[i]:offs[i + 1]]
    small = jnp.pad(jnp.concatenate([seg(1), seg(2)], axis=1), ((0, 0), (0, GATE_LANES - 2 * HEADS)))
    w_all = jnp.concatenate([w[:, :qkv_w], seg(0), seg(3), seg(4), seg(5), seg(6), small], axis=1).astype(BF16)
    lane_pad = lambda v: jnp.pad(v.astype(F32).reshape(1, HEADS), ((0, 0), (HEADS, GATE_LANES - 2 * HEADS)))
    idx = jnp.arange(CUMSUM_ROWS)
    tri = ((idx[:, None] >= idx[None, :]) & (idx[:, None] // CHUNK == idx[None, :] // CHUNK)).astype(BF16)

    qa, ka, va, ga, sm, bcum, kb, ib, qb, gb = _inproj(
        layer, x, norm1_w[layer].reshape(1, D), sc1, sh1, w_all, conv_w[layer].astype(F32),
        lane_pad(gdn_a_log[layer]), lane_pad(gdn_dt_bias[layer]), hg_lb.astype(F32), tri,
        _pick_tile(T, TILE_INPROJ))

    gct = sm[:, :, HEADS:2 * HEADS].transpose(0, 2, 1).reshape(B, HEADS, T // CHUNK, CHUNK)
    oa = _gdn(qa, ka, va, ga, sm, gct, gdn_norm_w[layer].reshape(1, DH), tt)
    ob = _hgrn(qb, kb, ib, bcum, gb, hg_norm_w[layer].reshape(1, GW), tt)

    wo = w_out[layer].astype(BF16)
    x1, h2, eidx, rank, wts, cnt = _outproj(oa, ob, x, wo[:GW], wo[GW:], g1, norm2_w[layer].reshape(1, D), sc2, sh2,
                                            w_router[layer].T, router_bias[layer].reshape(N_EXPERTS, 1),
                                            _pick_tile(T, TILE_OUTPROJ))

    counts = cnt[:, 0].astype(jnp.int32)
    padded = (counts + MOE_ROWS - 1) // MOE_ROWS * MOE_ROWS
    ends = jnp.cumsum(padded)
    offsets = ends - padded
    n_tiles = (M * TOP_K) // MOE_ROWS + N_EXPERTS
    n_used = (ends[-1] // MOE_ROWS).astype(jnp.int32)
    tile_ids = jnp.minimum(jnp.arange(n_tiles + FFN_PIPELINE_STEPS, dtype=jnp.int32), n_used - 1)
    tile_expert = jnp.sum(((ends // MOE_ROWS)[None, :] <= tile_ids[:, None]).astype(jnp.int32), axis=1)
    tile_expert = jnp.minimum(tile_expert, N_EXPERTS - 1)
    slot = _slots(offsets.astype(jnp.int32), eidx, rank)
    first_tile = jnp.take(offsets, tile_expert) // MOE_ROWS
    n_valid = jnp.clip(jnp.take(counts, tile_expert) - (tile_ids - first_tile) * MOE_ROWS, 0, MOE_ROWS)

    assert D == ROW_TILE * LANES
    xs = _dispatch(slot, h2, n_tiles * MOE_ROWS, _pick_tile(M, TILE_DISPATCH))
    ys = _ffn(tile_expert.astype(jnp.int32), n_used.reshape(1), n_valid.astype(jnp.int32), xs,
              w_gate[layer], w_up[layer], w_down[layer])
    wsgu = jnp.concatenate([ws_gate[layer], ws_up[layer]], axis=-1).astype(BF16)
    out = _combine(slot, wts.reshape(M, GATE_LANES), h2, x1.reshape(M, D), g2, final_norm_w.reshape(1, D),
                   wsgu, ws_down[layer].astype(BF16), ys, T, _pick_tile(T, TILE_COMBINE))
    return out.reshape(B, T, D)
```
